```python
import math
import jax, jax.numpy as jnp
from jax import lax
import numpy as np

D_MODEL = 1024
BATCH = 16
SEQ = 2048
DEPTH = 2

HEAD_DIM = 64
N_HEADS_FOX = 8
N_HEADS_SB = 8
DIL_PATTERNS = ((128, 1), (512, 4), (2048, 16))
N_DIL_GROUPS = len(DIL_PATTERNS)
N_HEADS_DIL = 4
D_FF = 4 * D_MODEL
ROPE_THETA = 10000.0
Q_BLOCK = 128
EPS = 1e-6
N_BRANCHES = 3

W_FOX = N_HEADS_FOX * HEAD_DIM
W_SB = N_HEADS_SB * HEAD_DIM
W_DIL = N_HEADS_DIL * HEAD_DIM
SZ_FOX_QKV = 3 * W_FOX
SZ_FORGET = N_HEADS_FOX
SZ_SB_QKV = 3 * W_SB
SZ_DIL_QKV = 3 * N_DIL_GROUPS * W_DIL
SZ_GATES = N_BRANCHES * D_MODEL
D_IN = SZ_FOX_QKV + SZ_FORGET + SZ_SB_QKV + SZ_DIL_QKV + SZ_GATES

kernel_name = "gated_parallel_fox_stickbreak_dilated"


def rms_norm(x, g):
    xf = x.astype(jnp.float32)
    y = xf * lax.rsqrt(jnp.mean(xf * xf, axis=-1, keepdims=True) + EPS)
    return (y * g.astype(jnp.float32)).astype(x.dtype)


def rope(x, positions):
    half = x.shape[-1] // 2
    inv = 1.0 / (ROPE_THETA ** (jnp.arange(half, dtype=jnp.float32) / half))
    ang = positions.astype(jnp.float32)[..., None] * inv
    cos = jnp.cos(ang)[:, :, None, :]
    sin = jnp.sin(ang)[:, :, None, :]
    xf = x.astype(jnp.float32)
    x1, x2 = xf[..., :half], xf[..., half:]
    return jnp.concatenate([x1 * cos - x2 * sin, x2 * cos + x1 * sin], axis=-1).astype(x.dtype)


def forgetting_attention(q, k, v, f_cum):
    B, T, H, D = q.shape
    scale = 1.0 / math.sqrt(D)
    f_t = jnp.transpose(f_cum, (0, 2, 1))
    outs = []
    for i in range(T // Q_BLOCK):
        lo, hi = i * Q_BLOCK, (i + 1) * Q_BLOCK
        s = jnp.einsum('bqhd,bkhd->bhqk', q[:, lo:hi], k[:, :hi]).astype(jnp.float32) * scale
        s = s + f_t[:, :, lo:hi, None] - f_t[:, :, None, :hi]
        mask = (lo + np.arange(Q_BLOCK))[:, None] >= np.arange(hi)[None, :]
        s = jnp.where(mask, s, -jnp.inf)
        p = jax.nn.softmax(s, axis=-1).astype(v.dtype)
        outs.append(jnp.einsum('bhqk,bkhd->bqhd', p, v[:, :hi]))
    return jnp.concatenate(outs, axis=1)


def stick_breaking_attention(q, k, v):
    B, T, H, D = q.shape
    scale = 1.0 / math.sqrt(D)
    outs = []
    for i in range(T // Q_BLOCK):
        lo, hi = i * Q_BLOCK, (i + 1) * Q_BLOCK
        z = jnp.einsum('bqhd,bkhd->bhqk', q[:, lo:hi], k[:, :hi]).astype(jnp.float32) * scale
        mask = (lo + np.arange(Q_BLOCK))[:, None] > np.arange(hi)[None, :]
        log_not = jnp.where(mask, jax.nn.log_sigmoid(-z), 0.0)
        later = lax.cumsum(log_not, axis=3, reverse=True) - log_not
        a = jnp.where(mask, jnp.exp(jax.nn.log_sigmoid(z) + later), 0.0)
        outs.append(jnp.einsum('bhqk,bkhd->bqhd', a.astype(v.dtype), v[:, :hi]))
    return jnp.concatenate(outs, axis=1)


def dilated_window_attention(q, k, v, window, dilation):
    B, T, H, D = q.shape
    n = T // dilation
    W = window // dilation
    Z = B * dilation
    scale = 1.0 / math.sqrt(D)

    def to_streams(a):
        return a.reshape(B, n, dilation, H, D).transpose(0, 2, 1, 3, 4).reshape(Z, n, H, D)

    qs, ks, vs = to_streams(q), to_streams(k), to_streams(v)
    qb = math.gcd(n, Q_BLOCK)
    nb = n // qb
    pad = ((0, 0), (W, 0), (0, 0), (0, 0))
    kp, vp = jnp.pad(ks, pad), jnp.pad(vs, pad)
    idx = np.arange(nb)[:, None] * qb + np.arange(qb + W)[None, :]
    kblk, vblk = kp[:, idx], vp[:, idx]
    qblk = qs.reshape(Z, nb, qb, H, D)
    s = jnp.einsum('znqhd,znkhd->znhqk', qblk, kblk).astype(jnp.float32) * scale
    dist = np.arange(qb)[:, None] + W - np.arange(qb + W)[None, :]
    band = (dist >= 0) & (dist <= W)
    mask = band[None] & (idx - W >= 0)[:, None, :]
    s = jnp.where(mask[None, :, None], s, -jnp.inf)
    m = jnp.max(s, axis=-1, keepdims=True)
    p = jnp.exp(s - m)
    den = jnp.sum(p, axis=-1, keepdims=True)
    o = jnp.einsum('znhqk,znkhd->znqhd', (p / den).astype(v.dtype), vblk)
    lse = jnp.transpose((m + jnp.log(den))[..., 0], (0, 1, 3, 2))
    o = o.reshape(B, dilation, n, H, D).transpose(0, 2, 1, 3, 4).reshape(B, T, H, D)
    lse = lse.reshape(B, dilation, n, H).transpose(0, 2, 1, 3).reshape(B, T, H)
    return o, lse


def _fwd_setup_inputs(seed: int = 0) -> dict:
    key = jax.random.key(seed)
    ks = jax.random.split(key, 17)
    f32 = jnp.float32

    def nrm(k, shape, fan_in, mult=1.0):
        return jax.random.normal(k, shape, f32) * (mult * fan_in ** -0.5)

    def gain(k, shape):
        return 1.0 + 0.05 * jax.random.normal(k, shape, f32)

    x = jax.random.normal(ks[0], (BATCH, SEQ, D_MODEL), f32)
    positions = jnp.broadcast_to(jnp.arange(SEQ, dtype=jnp.int32), (BATCH, SEQ))
    return {
        "x": x,
        "positions": positions,
        "attn_norm": gain(ks[1], (DEPTH, D_MODEL)),
        "w_in": nrm(ks[2], (DEPTH, D_MODEL, D_IN), D_MODEL),
        "b_forget": 2.0 + 0.1 * jax.random.normal(ks[3], (DEPTH, N_HEADS_FOX), f32),
        "q_norm_fox": gain(ks[4], (DEPTH, HEAD_DIM)),
        "k_norm_fox": gain(ks[5], (DEPTH, HEAD_DIM)),
        "q_norm_dil": gain(ks[6], (DEPTH, HEAD_DIM)),
        "k_norm_dil": gain(ks[7], (DEPTH, HEAD_DIM)),
        "w_up_fox": nrm(ks[8], (DEPTH, W_FOX, D_MODEL), W_FOX),
        "w_up_sb": nrm(ks[9], (DEPTH, W_SB, D_MODEL), W_SB),
        "w_up_dil": nrm(ks[10], (DEPTH, W_DIL, D_MODEL), W_DIL),
        "w_out": nrm(ks[11], (DEPTH, D_MODEL, D_MODEL), D_MODEL),
        "mlp_norm": gain(ks[12], (DEPTH, D_MODEL)),
        "w_mlp_in": nrm(ks[13], (DEPTH, D_MODEL, D_FF), D_MODEL),
        "w_mlp_out": nrm(ks[14], (DEPTH, D_FF, D_MODEL), D_FF, 0.5),
    }


def _fwd_reference(x, positions, attn_norm, w_in, b_forget, q_norm_fox, k_norm_fox, q_norm_dil, k_norm_dil,
              w_up_fox, w_up_sb, w_up_dil, w_out, mlp_norm, w_mlp_in, w_mlp_out):
    B, T, _ = x.shape
    o1 = SZ_FOX_QKV
    o2 = o1 + SZ_FORGET
    o3 = o2 + SZ_SB_QKV
    o4 = o3 + SZ_DIL_QKV
    for l in range(DEPTH):
        h = rms_norm(x, attn_norm[l])
        proj = h @ w_in[l]

        fox = proj[..., :o1].reshape(B, T, 3, N_HEADS_FOX, HEAD_DIM)
        qa = rms_norm(fox[:, :, 0], q_norm_fox[l])
        ka = rms_norm(fox[:, :, 1], k_norm_fox[l])
        va = fox[:, :, 2]
        log_f = jax.nn.log_sigmoid(proj[..., o1:o2].astype(jnp.float32) + b_forget[l].astype(jnp.float32))
        f_cum = jnp.cumsum(log_f, axis=1)
        out_a = forgetting_attention(qa, ka, va, f_cum)

        sb = proj[..., o2:o3].reshape(B, T, 3, N_HEADS_SB, HEAD_DIM)
        out_b = stick_breaking_attention(sb[:, :, 0], sb[:, :, 1], sb[:, :, 2])

        dil = proj[..., o3:o4].reshape(B, T, 3, N_DIL_GROUPS * N_HEADS_DIL, HEAD_DIM)
        qc = rope(rms_norm(dil[:, :, 0], q_norm_dil[l]), positions)
        kc = rope(rms_norm(dil[:, :, 1], k_norm_dil[l]), positions)
        vc = dil[:, :, 2]
        group_o, group_lse = [], []
        for g, (window, dilation) in enumerate(DIL_PATTERNS):
            sl = slice(g * N_HEADS_DIL, (g + 1) * N_HEADS_DIL)
            o_g, lse_g = dilated_window_attention(qc[:, :, sl], kc[:, :, sl], vc[:, :, sl], window, dilation)
            group_o.append(o_g)
            group_lse.append(lse_g)
        wts = jax.nn.softmax(jnp.stack(group_lse, axis=0), axis=0)
        out_c = jnp.sum(wts[..., None].astype(vc.dtype) * jnp.stack(group_o, axis=0), axis=0)

        gates = jax.nn.sigmoid(proj[..., o4:].astype(jnp.float32)).astype(x.dtype).reshape(B, T, N_BRANCHES, D_MODEL)
        y_a = out_a.reshape(B, T, W_FOX) @ w_up_fox[l]
        y_b = out_b.reshape(B, T, W_SB) @ w_up_sb[l]
        y_c = out_c.reshape(B, T, W_DIL) @ w_up_dil[l]
        merged = gates[:, :, 0] * y_a + gates[:, :, 1] * y_b + gates[:, :, 2] * y_c
        x = x + merged @ w_out[l]

        h2 = rms_norm(x, mlp_norm[l])
        x = x + jnp.square(jax.nn.relu(h2 @ w_mlp_in[l])) @ w_mlp_out[l]
    return x


import jax as _jax
import jax.numpy as _jnp

TWIN_FORMAT = 'train_step'
FWD_PARAMS = ['x', 'positions', 'attn_norm', 'w_in', 'b_forget', 'q_norm_fox', 'k_norm_fox', 'q_norm_dil', 'k_norm_dil', 'w_up_fox', 'w_up_sb', 'w_up_dil', 'w_out', 'mlp_norm', 'w_mlp_in', 'w_mlp_out']
TWIN_WEIGHTS = ['attn_norm', 'w_in', 'b_forget', 'q_norm_fox', 'k_norm_fox', 'q_norm_dil', 'k_norm_dil', 'w_up_fox', 'w_up_sb', 'w_up_dil', 'w_out', 'mlp_norm', 'w_mlp_in', 'w_mlp_out']
TWIN_DIFF_INPUT = 'x'
TWIN_INPUTS = ['x', 'positions', 'attn_norm', 'w_in', 'b_forget', 'q_norm_fox', 'k_norm_fox', 'q_norm_dil', 'k_norm_dil', 'w_up_fox', 'w_up_sb', 'w_up_dil', 'w_out', 'mlp_norm', 'w_mlp_in', 'w_mlp_out', 'loss_target', 'm_attn_norm', 'm_w_in', 'm_b_forget', 'm_q_norm_fox', 'm_k_norm_fox', 'm_q_norm_dil', 'm_k_norm_dil', 'm_w_up_fox', 'm_w_up_sb', 'm_w_up_dil', 'm_w_out', 'm_mlp_norm', 'm_w_mlp_in', 'm_w_mlp_out', 'v_attn_norm', 'v_w_in', 'v_b_forget', 'v_q_norm_fox', 'v_k_norm_fox', 'v_q_norm_dil', 'v_k_norm_dil', 'v_w_up_fox', 'v_w_up_sb', 'v_w_up_dil', 'v_w_out', 'v_mlp_norm', 'v_w_mlp_in', 'v_w_mlp_out']
TWIN_OUTPUTS = ['loss', 'grad_x', 'grad_attn_norm', 'grad_w_in', 'grad_b_forget', 'grad_q_norm_fox', 'grad_k_norm_fox', 'grad_q_norm_dil', 'grad_k_norm_dil', 'grad_w_up_fox', 'grad_w_up_sb', 'grad_w_up_dil', 'grad_w_out', 'grad_mlp_norm', 'grad_w_mlp_in', 'grad_w_mlp_out', 'delta_attn_norm', 'delta_w_in', 'delta_b_forget', 'delta_q_norm_fox', 'delta_k_norm_fox', 'delta_q_norm_dil', 'delta_k_norm_dil', 'delta_w_up_fox', 'delta_w_up_sb', 'delta_w_up_dil', 'delta_w_out', 'delta_mlp_norm', 'delta_w_mlp_in', 'delta_w_mlp_out', 'new_m_attn_norm', 'new_m_w_in', 'new_m_b_forget', 'new_m_q_norm_fox', 'new_m_k_norm_fox', 'new_m_q_norm_dil', 'new_m_k_norm_dil', 'new_m_w_up_fox', 'new_m_w_up_sb', 'new_m_w_up_dil', 'new_m_w_out', 'new_m_mlp_norm', 'new_m_w_mlp_in', 'new_m_w_mlp_out', 'new_v_attn_norm', 'new_v_w_in', 'new_v_b_forget', 'new_v_q_norm_fox', 'new_v_k_norm_fox', 'new_v_q_norm_dil', 'new_v_k_norm_dil', 'new_v_w_up_fox', 'new_v_w_up_sb', 'new_v_w_up_dil', 'new_v_w_out', 'new_v_mlp_norm', 'new_v_w_mlp_in', 'new_v_w_mlp_out']
TWIN_LEAF_KINDS = {'loss': 'loss', 'grad_x': 'grad_x', 'grad_attn_norm': 'grad_w', 'grad_w_in': 'grad_w', 'grad_b_forget': 'grad_w', 'grad_q_norm_fox': 'grad_w', 'grad_k_norm_fox': 'grad_w', 'grad_q_norm_dil': 'grad_w', 'grad_k_norm_dil': 'grad_w', 'grad_w_up_fox': 'grad_w', 'grad_w_up_sb': 'grad_w', 'grad_w_up_dil': 'grad_w', 'grad_w_out': 'grad_w', 'grad_mlp_norm': 'grad_w', 'grad_w_mlp_in': 'grad_w', 'grad_w_mlp_out': 'grad_w', 'delta_attn_norm': 'delta_w', 'delta_w_in': 'delta_w', 'delta_b_forget': 'delta_w', 'delta_q_norm_fox': 'delta_w', 'delta_k_norm_fox': 'delta_w', 'delta_q_norm_dil': 'delta_w', 'delta_k_norm_dil': 'delta_w', 'delta_w_up_fox': 'delta_w', 'delta_w_up_sb': 'delta_w', 'delta_w_up_dil': 'delta_w', 'delta_w_out': 'delta_w', 'delta_mlp_norm': 'delta_w', 'delta_w_mlp_in': 'delta_w', 'delta_w_mlp_out': 'delta_w', 'new_m_attn_norm': 'new_m', 'new_m_w_in': 'new_m', 'new_m_b_forget': 'new_m', 'new_m_q_norm_fox': 'new_m', 'new_m_k_norm_fox': 'new_m', 'new_m_q_norm_dil': 'new_m', 'new_m_k_norm_dil': 'new_m', 'new_m_w_up_fox': 'new_m', 'new_m_w_up_sb': 'new_m', 'new_m_w_up_dil': 'new_m', 'new_m_w_out': 'new_m', 'new_m_mlp_norm': 'new_m', 'new_m_w_mlp_in': 'new_m', 'new_m_w_mlp_out': 'new_m', 'new_v_attn_norm': 'new_v', 'new_v_w_in': 'new_v', 'new_v_b_forget': 'new_v', 'new_v_q_norm_fox': 'new_v', 'new_v_k_norm_fox': 'new_v', 'new_v_q_norm_dil': 'new_v', 'new_v_k_norm_dil': 'new_v', 'new_v_w_up_fox': 'new_v', 'new_v_w_up_sb': 'new_v', 'new_v_w_up_dil': 'new_v', 'new_v_w_out': 'new_v', 'new_v_mlp_norm': 'new_v', 'new_v_w_mlp_in': 'new_v', 'new_v_w_mlp_out': 'new_v'}


def _forward(args):
    return _fwd_reference(*[args[k] for k in FWD_PARAMS])


def _output_shape():
    out = _jax.eval_shape(lambda: _forward(_fwd_setup_inputs(0)))
    return out.shape, out.dtype

N_MICROBATCH = 1
ADAM_LR = 0.001
ADAM_B1 = 0.9
ADAM_B2 = 0.999
ADAM_EPS = 1e-08
ADAM_WD = 0.01
ADAM_STEP = 10
PER_EXAMPLE_BATCH_AXIS = {'x': 0, 'positions': 0, 'loss_target': 0}
SHARED_INPUTS = []
_WEIGHT_DTYPES = {'attn_norm': _jnp.float32, 'w_in': _jnp.float32, 'b_forget': _jnp.float32, 'q_norm_fox': _jnp.float32, 'k_norm_fox': _jnp.float32, 'q_norm_dil': _jnp.float32, 'k_norm_dil': _jnp.float32, 'w_up_fox': _jnp.float32, 'w_up_sb': _jnp.float32, 'w_up_dil': _jnp.float32, 'w_out': _jnp.float32, 'mlp_norm': _jnp.float32, 'w_mlp_in': _jnp.float32, 'w_mlp_out': _jnp.float32}
MOMENT_SCALE = {'attn_norm': 7.984534e+00, 'w_in': 5.418891e-01, 'b_forget': 6.036571e+01, 'q_norm_fox': 9.179531e+00, 'k_norm_fox': 9.071600e+00, 'q_norm_dil': 6.205394e-01, 'k_norm_dil': 6.302705e-01, 'w_up_fox': 9.835814e-01, 'w_up_sb': 9.711183e-01, 'w_up_dil': 4.535215e-01, 'w_out': 1.313457e+00, 'mlp_norm': 2.433798e+01, 'w_mlp_in': 5.975114e-01, 'w_mlp_out': 8.119639e+00}


def _to_microbatches(a, axis):
    t = _jnp.moveaxis(a, axis, 0)
    t = t.reshape((N_MICROBATCH, t.shape[0] // N_MICROBATCH) + t.shape[1:])
    return _jnp.moveaxis(t, 1, axis + 1)


def setup_inputs(seed: int = 0) -> dict:
    inp = _fwd_setup_inputs(seed)
    key = _jax.random.fold_in(_jax.random.key(seed), 7919)
    shape, _ = _output_shape()
    out = dict(inp)
    out["loss_target"] = _jax.random.normal(_jax.random.fold_in(key, 0), shape, _jnp.float32)
    for i, name in enumerate(TWIN_WEIGHTS):
        w = inp[name].astype(_jnp.float32)
        if MOMENT_SCALE is None:
            s = _jnp.sqrt(_jnp.mean(_jnp.square(w)) + 1e-30)
        else:
            s = MOMENT_SCALE[name]
        km, kv = _jax.random.split(_jax.random.fold_in(key, i + 1))
        out[name] = w
        out["m_" + name] = s * _jax.random.normal(km, w.shape, _jnp.float32)
        out["v_" + name] = (s * s) * _jax.random.uniform(kv, w.shape, _jnp.float32, 0.5, 1.5)
    if N_MICROBATCH > 1:
        for name, axis in PER_EXAMPLE_BATCH_AXIS.items():
            out[name] = _to_microbatches(out[name], axis)
    return {'x': out['x'], 'positions': out['positions'], 'attn_norm': out['attn_norm'], 'w_in': out['w_in'], 'b_forget': out['b_forget'], 'q_norm_fox': out['q_norm_fox'], 'k_norm_fox': out['k_norm_fox'], 'q_norm_dil': out['q_norm_dil'], 'k_norm_dil': out['k_norm_dil'], 'w_up_fox': out['w_up_fox'], 'w_up_sb': out['w_up_sb'], 'w_up_dil': out['w_up_dil'], 'w_out': out['w_out'], 'mlp_norm': out['mlp_norm'], 'w_mlp_in': out['w_mlp_in'], 'w_mlp_out': out['w_mlp_out'], 'loss_target': out['loss_target'], 'm_attn_norm': out['m_attn_norm'], 'm_w_in': out['m_w_in'], 'm_b_forget': out['m_b_forget'], 'm_q_norm_fox': out['m_q_norm_fox'], 'm_k_norm_fox': out['m_k_norm_fox'], 'm_q_norm_dil': out['m_q_norm_dil'], 'm_k_norm_dil': out['m_k_norm_dil'], 'm_w_up_fox': out['m_w_up_fox'], 'm_w_up_sb': out['m_w_up_sb'], 'm_w_up_dil': out['m_w_up_dil'], 'm_w_out': out['m_w_out'], 'm_mlp_norm': out['m_mlp_norm'], 'm_w_mlp_in': out['m_w_mlp_in'], 'm_w_mlp_out': out['m_w_mlp_out'], 'v_attn_norm': out['v_attn_norm'], 'v_w_in': out['v_w_in'], 'v_b_forget': out['v_b_forget'], 'v_q_norm_fox': out['v_q_norm_fox'], 'v_k_norm_fox': out['v_k_norm_fox'], 'v_q_norm_dil': out['v_q_norm_dil'], 'v_k_norm_dil': out['v_k_norm_dil'], 'v_w_up_fox': out['v_w_up_fox'], 'v_w_up_sb': out['v_w_up_sb'], 'v_w_up_dil': out['v_w_up_dil'], 'v_w_out': out['v_w_out'], 'v_mlp_norm': out['v_mlp_norm'], 'v_w_mlp_in': out['v_w_mlp_in'], 'v_w_mlp_out': out['v_w_mlp_out']}


def _loss(weights, diff, rest, loss_target):
    with _jax.named_scope("forward"):
        args = {**rest, TWIN_DIFF_INPUT: diff, **{k: w.astype(_WEIGHT_DTYPES[k]) for k, w in weights.items()}}
        y = _forward(args)
    with _jax.named_scope("loss_head"):
        err = _jnp.square(y.astype(_jnp.float32) - loss_target)
        return 0.5 * _jnp.sum(_jnp.mean(err, axis=-1)) if err.ndim else 0.5 * err


def _adamw(w, g, m, v):
    m = ADAM_B1 * m + (1.0 - ADAM_B1) * g
    v = ADAM_B2 * v + (1.0 - ADAM_B2) * _jnp.square(g)
    m_hat = m / (1.0 - ADAM_B1 ** ADAM_STEP)
    v_hat = v / (1.0 - ADAM_B2 ** ADAM_STEP)
    delta = -ADAM_LR * (m_hat / (_jnp.sqrt(v_hat) + ADAM_EPS) + ADAM_WD * w)
    return delta, m, v


def reference(x, positions, attn_norm, w_in, b_forget, q_norm_fox, k_norm_fox, q_norm_dil, k_norm_dil, w_up_fox, w_up_sb, w_up_dil, w_out, mlp_norm, w_mlp_in, w_mlp_out, loss_target, m_attn_norm, m_w_in, m_b_forget, m_q_norm_fox, m_k_norm_fox, m_q_norm_dil, m_k_norm_dil, m_w_up_fox, m_w_up_sb, m_w_up_dil, m_w_out, m_mlp_norm, m_w_mlp_in, m_w_mlp_out, v_attn_norm, v_w_in, v_b_forget, v_q_norm_fox, v_k_norm_fox, v_q_norm_dil, v_k_norm_dil, v_w_up_fox, v_w_up_sb, v_w_up_dil, v_w_out, v_mlp_norm, v_w_mlp_in, v_w_mlp_out):
    given = dict(x=x, positions=positions, attn_norm=attn_norm, w_in=w_in, b_forget=b_forget, q_norm_fox=q_norm_fox, k_norm_fox=k_norm_fox, q_norm_dil=q_norm_dil, k_norm_dil=k_norm_dil, w_up_fox=w_up_fox, w_up_sb=w_up_sb, w_up_dil=w_up_dil, w_out=w_out, mlp_norm=mlp_norm, w_mlp_in=w_mlp_in, w_mlp_out=w_mlp_out, loss_target=loss_target, m_attn_norm=m_attn_norm, m_w_in=m_w_in, m_b_forget=m_b_forget, m_q_norm_fox=m_q_norm_fox, m_k_norm_fox=m_k_norm_fox, m_q_norm_dil=m_q_norm_dil, m_k_norm_dil=m_k_norm_dil, m_w_up_fox=m_w_up_fox, m_w_up_sb=m_w_up_sb, m_w_up_dil=m_w_up_dil, m_w_out=m_w_out, m_mlp_norm=m_mlp_norm, m_w_mlp_in=m_w_mlp_in, m_w_mlp_out=m_w_mlp_out, v_attn_norm=v_attn_norm, v_w_in=v_w_in, v_b_forget=v_b_forget, v_q_norm_fox=v_q_norm_fox, v_k_norm_fox=v_k_norm_fox, v_q_norm_dil=v_q_norm_dil, v_k_norm_dil=v_k_norm_dil, v_w_up_fox=v_w_up_fox, v_w_up_sb=v_w_up_sb, v_w_up_dil=v_w_up_dil, v_w_out=v_w_out, v_mlp_norm=v_mlp_norm, v_w_mlp_in=v_w_mlp_in, v_w_mlp_out=v_w_mlp_out)
    weights = {n: given[n] for n in TWIN_WEIGHTS}
    shared = {n: given[n] for n in SHARED_INPUTS}
    per_example = {n: given[n] for n in ['x', 'positions']}
    grad_fn = _jax.value_and_grad(_loss, argnums=(0, 1))

    def one_microbatch(ex, loss_target):
        ex = dict(ex)
        diff = ex.pop(TWIN_DIFF_INPUT)
        return grad_fn(weights, diff, {**shared, **ex}, loss_target)

    if N_MICROBATCH == 1:
        loss, (grad_w, grad_x) = one_microbatch(per_example, given["loss_target"])
    else:
        def body(carry, xs):
            loss_sum, grad_sum = carry
            l_k, (gw_k, gx_k) = one_microbatch(xs[0], xs[1])
            with _jax.named_scope("update"):
                return (loss_sum + l_k, _jax.tree.map(_jnp.add, grad_sum, gw_k)), gx_k

        init = (_jnp.zeros((), _jnp.float32), _jax.tree.map(_jnp.zeros_like, weights))
        (loss, grad_w), grad_x = _jax.lax.scan(body, init, (per_example, given["loss_target"]))
    with _jax.named_scope("update"):
        delta_w, new_m, new_v = {}, {}, {}
        for n in TWIN_WEIGHTS:
            delta_w[n], new_m[n], new_v[n] = _adamw(weights[n], grad_w[n], given["m_" + n], given["v_" + n])
    return (loss, grad_x, *[grad_w[n] for n in TWIN_WEIGHTS], *[delta_w[n] for n in TWIN_WEIGHTS],
            *[new_m[n] for n in TWIN_WEIGHTS], *[new_v[n] for n in TWIN_WEIGHTS])
```

```python
import jax
import jax.numpy as jnp
from jax import lax
from jax.experimental import pallas as pl
from jax.experimental.pallas import tpu as pltpu

F32 = jnp.float32
BF16 = jnp.bfloat16

HEAD_DIM = 64
LANES = 128
N_HEADS_FOX = 8
N_HEADS_SB = 8
N_HEADS_DIL = 4
DIL_PATTERNS = ((128, 1), (512, 4), (2048, 16))
ROPE_THETA = 10000.0
EPS = 1e-6
SCALE = 0.125
W_FOX = N_HEADS_FOX * HEAD_DIM
W_SB = N_HEADS_SB * HEAD_DIM
W_DIL = N_HEADS_DIL * HEAD_DIM
W_DILQ = len(DIL_PATTERNS) * W_DIL
P_FOX = 0
P_SB = 3 * W_FOX
P_DIL = P_SB + 3 * W_SB
P_GATE = P_DIL + 3 * W_DILQ
N_DEV = 8
ATT_BLK = 128
NEG = -1e30
VMEM_LIMIT = 56 * 1024 * 1024

ADAM_LR = 0.001
ADAM_B1 = 0.9
ADAM_B2 = 0.999
ADAM_EPS = 1e-08
ADAM_WD = 0.01
ADAM_STEP = 10

NT = (((1,), (1,)), ((), ()))
TN = (((0,), (0,)), ((), ()))
MESH = pl.DeviceIdType.MESH


def _pcall(body, **kw):
    return pl.pallas_call(body, **kw)


def _params(sem=None):
    return pltpu.CompilerParams(dimension_semantics=sem, vmem_limit_bytes=VMEM_LIMIT)


def _tile(dim, target, mult=LANES):
    t = (min(dim, target) // mult) * mult
    while t >= mult:
        if dim % t == 0:
            return t
        t -= mult
    return dim


def _mm(a, b, *, ta=False, tb=False, add=None, out_dtype=F32, name, tm=1024, tn=512, tk=1024):
    m, k = (a.shape[1], a.shape[0]) if ta else a.shape
    n = b.shape[0] if tb else b.shape[1]
    tm, tn, tk = _tile(m, tm), _tile(n, tn), _tile(k, tk)
    nk = k // tk
    dn = (((0,) if ta else (1,), (1,) if tb else (0,)), ((), ()))

    def body(*refs):
        if add is None:
            a_ref, b_ref, o_ref, acc = refs
        else:
            a_ref, b_ref, add_ref, o_ref, acc = refs
        kk = pl.program_id(2)
        part = lax.dot_general(a_ref[...].astype(BF16), b_ref[...].astype(BF16), dn, preferred_element_type=F32)

        @pl.when(kk == 0)
        def _():
            acc[...] = part

        @pl.when(kk > 0)
        def _():
            acc[...] += part

        @pl.when(kk == nk - 1)
        def _():
            r = acc[...]
            if add is not None:
                r = r + add_ref[...]
            o_ref[...] = r.astype(out_dtype)

    a_spec = pl.BlockSpec((tk, tm), lambda i, j, q: (q, i)) if ta else pl.BlockSpec((tm, tk), lambda i, j, q: (i, q))
    b_spec = pl.BlockSpec((tn, tk), lambda i, j, q: (j, q)) if tb else pl.BlockSpec((tk, tn), lambda i, j, q: (q, j))
    o_spec = pl.BlockSpec((tm, tn), lambda i, j, q: (i, j))
    ins, specs = [a, b], [a_spec, b_spec]
    if add is not None:
        ins.append(add)
        specs.append(o_spec)
    return _pcall(
        body, name=name, grid=(m // tm, n // tn, nk), in_specs=specs, out_specs=o_spec,
        out_shape=jax.ShapeDtypeStruct((m, n), out_dtype), scratch_shapes=[pltpu.VMEM((tm, tn), F32)],
        compiler_params=_params(("parallel", "parallel", "arbitrary")),
    )(*ins)


def _rmsnorm_fwd(x, g, *, name):
    n, d = x.shape
    tm = _tile(n, 256, 8)

    def body(x_ref, g_ref, h_ref):
        xv = x_ref[...]
        inv = lax.rsqrt(jnp.mean(xv * xv, axis=1, keepdims=True) + EPS)
        h_ref[...] = (xv * inv * g_ref[...]).astype(BF16)

    row = pl.BlockSpec((tm, d), lambda i: (i, 0))
    return _pcall(
        body, name=name, grid=(n // tm,), in_specs=[row, pl.BlockSpec((1, d), lambda i: (0, 0))], out_specs=row,
        out_shape=jax.ShapeDtypeStruct((n, d), BF16), compiler_params=_params(("parallel",)),
    )(x, g)


def _rmsnorm_bwd(x, g, dh, dres, *, name):
    n, d = x.shape
    tm = _tile(n, 256, 8)

    def body(x_ref, g_ref, dh_ref, dres_ref, dx_ref, dg_ref):
        @pl.when(pl.program_id(0) == 0)
        def _():
            dg_ref[...] = jnp.zeros_like(dg_ref)

        xv = x_ref[...]
        inv = lax.rsqrt(jnp.mean(xv * xv, axis=1, keepdims=True) + EPS)
        y = xv * inv
        dhv = dh_ref[...]
        dg_ref[...] += jnp.sum(dhv * y, axis=0, keepdims=True)
        dy = dhv * g_ref[...]
        dx_ref[...] = dres_ref[...] + inv * (dy - y * jnp.mean(dy * y, axis=1, keepdims=True))

    row = pl.BlockSpec((tm, d), lambda i: (i, 0))
    vec = pl.BlockSpec((1, d), lambda i: (0, 0))
    return _pcall(
        body, name=name, grid=(n // tm,), in_specs=[row, vec, row, row], out_specs=[row, vec],
        out_shape=[jax.ShapeDtypeStruct((n, d), F32), jax.ShapeDtypeStruct((1, d), F32)],
        compiler_params=_params(("arbitrary",)),
    )(x, g, dh, dres)


def _gate_specs(n, d):
    bw = 256 if d % 256 == 0 else LANES
    tm = _tile(n, 512, 8)
    nb = d // bw
    yspec = pl.BlockSpec((tm, bw), lambda i, j: (i, j))
    gspecs = [pl.BlockSpec((tm, bw), lambda i, j, b=b: (i, P_GATE // bw + b * nb + j)) for b in range(3)]
    return tm, bw, nb, yspec, gspecs


def _gate_merge_fwd(proj, ys, *, name):
    n, d = ys[0].shape
    tm, bw, nb, yspec, gspecs = _gate_specs(n, d)

    def body(g0, g1, g2, y0, y1, y2, o_ref):
        acc = jax.nn.sigmoid(g0[...]) * y0[...]
        acc += jax.nn.sigmoid(g1[...]) * y1[...]
        acc += jax.nn.sigmoid(g2[...]) * y2[...]
        o_ref[...] = acc.astype(BF16)

    return _pcall(
        body, name=name, grid=(n // tm, nb), in_specs=gspecs + [yspec] * 3, out_specs=yspec,
        out_shape=jax.ShapeDtypeStruct((n, d), BF16), compiler_params=_params(("parallel", "parallel")),
    )(proj, proj, proj, *ys)


def _gate_merge_bwd(proj, ys, dmerged, *, name):
    n, d = ys[0].shape
    tm, bw, nb, yspec, gspecs = _gate_specs(n, d)

    def body(g0, g1, g2, y0, y1, y2, dm_ref, dy0, dy1, dy2, dgl0, dgl1, dgl2):
        dm = dm_ref[...]
        for g_ref, y_ref, dy_ref, dgl_ref in ((g0, y0, dy0, dgl0), (g1, y1, dy1, dgl1), (g2, y2, dy2, dgl2)):
            s = jax.nn.sigmoid(g_ref[...])
            dy_ref[...] = (dm * s).astype(BF16)
            dgl_ref[...] = (dm * y_ref[...] * s * (1.0 - s)).astype(BF16)

    sds = jax.ShapeDtypeStruct((n, d), BF16)
    return _pcall(
        body, name=name, grid=(n // tm, nb), in_specs=gspecs + [yspec] * 4, out_specs=[yspec] * 6,
        out_shape=[sds] * 6, compiler_params=_params(("parallel", "parallel")),
    )(proj, proj, proj, *ys, dmerged)


def _relu2_fwd(u, *, name):
    n, f = u.shape
    tm, tf = _tile(n, 512, 8), _tile(f, 1024)

    def body(u_ref, a_ref):
        r = jnp.maximum(u_ref[...], 0.0)
        a_ref[...] = (r * r).astype(BF16)

    spec = pl.BlockSpec((tm, tf), lambda i, j: (i, j))
    return _pcall(
        body, name=name, grid=(n // tm, f // tf), in_specs=[spec], out_specs=spec,
        out_shape=jax.ShapeDtypeStruct((n, f), BF16), compiler_params=_params(("parallel", "parallel")),
    )(u)


def _relu2_bwd(da, u, *, name):
    n, f = u.shape
    tm, tf = _tile(n, 512, 8), _tile(f, 1024)

    def body(da_ref, u_ref, du_ref):
        du_ref[...] = (da_ref[...] * (2.0 * jnp.maximum(u_ref[...], 0.0))).astype(BF16)

    spec = pl.BlockSpec((tm, tf), lambda i, j: (i, j))
    return _pcall(
        body, name=name, grid=(n // tm, f // tf), in_specs=[spec, spec], out_specs=spec,
        out_shape=jax.ShapeDtypeStruct((n, f), BF16), compiler_params=_params(("parallel", "parallel")),
    )(da, u)


def _loss_head(y, tgt, *, name):
    n, d = y.shape
    tm = _tile(n, 256, 8)
    steps = n // tm

    def body(y_ref, t_ref, dy_ref, loss_ref, acc):
        i = pl.program_id(0)

        @pl.when(i == 0)
        def _():
            acc[...] = jnp.zeros_like(acc)

        e = y_ref[...] - t_ref[...]
        dy_ref[...] = e * (1.0 / d)
        acc[...] += jnp.sum(e * e, axis=0, keepdims=True)

        @pl.when(i == steps - 1)
        def _():
            tot = jnp.sum(acc[...], axis=1, keepdims=True) * (0.5 / d)
            loss_ref[...] = jnp.broadcast_to(tot, loss_ref.shape)

    row = pl.BlockSpec((tm, d), lambda i: (i, 0))
    return _pcall(
        body, name=name, grid=(steps,), in_specs=[row, row], out_specs=[row, pl.BlockSpec((8, LANES), lambda i: (0, 0))],
        out_shape=[jax.ShapeDtypeStruct((n, d), F32), jax.ShapeDtypeStruct((8, LANES), F32)],
        scratch_shapes=[pltpu.VMEM((1, d), F32)], compiler_params=_params(("arbitrary",)),
    )(y, tgt)


def _dil_combine(os_, lses, *, name):
    n, w = os_[0].shape
    tm = _tile(n, 512, 8)

    def body(o0, o1, o2, l0, l1, l2, out_ref, lse_ref):
        a, b, c = l0[...], l1[...], l2[...]
        m = jnp.maximum(jnp.maximum(a, b), c)
        ea, eb, ec = jnp.exp(a - m), jnp.exp(b - m), jnp.exp(c - m)
        den = ea + eb + ec
        out_ref[...] = (ea * o0[...] + eb * o1[...] + ec * o2[...]) / den
        lse_ref[...] = m + jnp.log(den)

    spec = pl.BlockSpec((tm, w), lambda i: (i, 0))
    sds = jax.ShapeDtypeStruct((n, w), F32)
    return _pcall(
        body, name=name, grid=(n // tm,), in_specs=[spec] * 6, out_specs=[spec, spec], out_shape=[sds, sds],
        compiler_params=_params(("parallel",)),
    )(*os_, *lses)


def _adamw(gsrc, w, m, v, *, name):
    s, r, _ = gsrc.shape
    tr = _tile(r, 512, 16)
    c1 = 1.0 / (1.0 - ADAM_B1 ** ADAM_STEP)
    c2 = 1.0 / (1.0 - ADAM_B2 ** ADAM_STEP)

    def body(gs_ref, w_ref, m_ref, v_ref, g_ref, d_ref, m2_ref, v2_ref):
        g = gs_ref[0].astype(F32)
        for i in range(1, s):
            g = g + gs_ref[i].astype(F32)
        m2 = ADAM_B1 * m_ref[...] + (1.0 - ADAM_B1) * g
        v2 = ADAM_B2 * v_ref[...] + (1.0 - ADAM_B2) * (g * g)
        g_ref[...] = g
        m2_ref[...] = m2
        v2_ref[...] = v2
        d_ref[...] = -ADAM_LR * ((m2 * c1) / (jnp.sqrt(v2 * c2) + ADAM_EPS) + ADAM_WD * w_ref[...])

    spec = pl.BlockSpec((tr, LANES), lambda i: (i, 0))
    sds = jax.ShapeDtypeStruct((r, LANES), F32)
    return _pcall(
        body, name=name, grid=(r // tr,), in_specs=[pl.BlockSpec((s, tr, LANES), lambda i: (0, i, 0)), spec, spec, spec],
        out_specs=[spec] * 4, out_shape=[sds] * 4, compiler_params=_params(("parallel",)),
    )(gsrc, w, m, v)


def _mask_a():
    return lax.broadcasted_iota(jnp.int32, (1, LANES), 1) < HEAD_DIM


def _half_sum(x, m_a):
    sa = jnp.sum(jnp.where(m_a, x, 0.0), axis=1, keepdims=True)
    sb = jnp.sum(jnp.where(m_a, 0.0, x), axis=1, keepdims=True)
    return jnp.where(m_a, sa, sb)


def _head_inv(x, m_a):
    return lax.rsqrt(_half_sum(x * x, m_a) * (1.0 / HEAD_DIM) + EPS)


def _swap32(x):
    first = (lax.broadcasted_iota(jnp.int32, (1, LANES), 1) % HEAD_DIM) < (HEAD_DIM // 2)
    return jnp.where(first, pltpu.roll(x, LANES - HEAD_DIM // 2, 1), pltpu.roll(x, HEAD_DIM // 2, 1))


def _head_col(t, h):
    if h == 1:
        t = pltpu.roll(t, HEAD_DIM, 1)
    return t[:, 0:1]


def _tri(blk, rel):
    r = lax.broadcasted_iota(jnp.int32, (blk, blk), 0)
    c = lax.broadcasted_iota(jnp.int32, (blk, blk), 1)
    return jnp.where(rel(r, c), 1.0, 0.0).astype(BF16)


def _cumdot(x, u, parts):
    acc = None
    r = x
    for i in range(parts):
        xi = r.astype(BF16)
        t = jnp.dot(xi, u, preferred_element_type=F32)
        acc = t if acc is None else acc + t
        if i + 1 < parts:
            r = r - xi.astype(F32)
    return acc


def _rows(i, blk):
    return pl.ds(pl.multiple_of(i * blk, blk), blk)


def _col_spec(n, off):
    return pl.BlockSpec((None, n, LANES), lambda z, p, off=off: (z, 0, off + p))


def _prep_qk(q_ref, k_ref, v_ref, gq_ref, gk_ref, rope_refs, qn_s, kn_s, vb_s, nb, blk, m_a):
    def prep(c, carry):
        rows = _rows(c, blk)
        for src, g_ref, dst, scale in ((q_ref, gq_ref, qn_s, SCALE), (k_ref, gk_ref, kn_s, None)):
            xv = src[rows, :]
            xn = xv * _head_inv(xv, m_a) * g_ref[...]
            if rope_refs is not None:
                xn = xn * rope_refs[0][rows, :] + _swap32(xn) * rope_refs[1][rows, :]
            if scale is not None:
                xn = xn * scale
            dst[rows, :] = xn.astype(BF16)
        vb_s[rows, :] = v_ref[rows, :].astype(BF16)
        return carry

    lax.fori_loop(0, nb, prep, 0)


def _attn_fwd(src, offs, npairs, gq, gk, *, rope=None, kbias=None, window, name):
    zs, n, _ = src.shape
    blk = min(ATT_BLK, n)
    nb = n // blk
    wblk = min(-(-window // blk), nb)

    def body(*refs):
        it = iter(refs)
        q_ref, k_ref, v_ref, gq_ref, gk_ref = (next(it) for _ in range(5))
        rope_refs = (next(it), next(it)) if rope is not None else None
        kb_ref = next(it) if kbias is not None else None
        o_ref, lse_ref, qn_s, kn_s, vb_s = (next(it) for _ in range(5))
        m_a = _mask_a()
        _prep_qk(q_ref, k_ref, v_ref, gq_ref, gk_ref, rope_refs, qn_s, kn_s, vb_s, nb, blk, m_a)
        dif = lax.broadcasted_iota(jnp.int32, (blk, blk), 0) - lax.broadcasted_iota(jnp.int32, (blk, blk), 1)

        def qloop(qi, carry):
            rows = _rows(qi, blk)
            qb = qn_s[rows, :]
            lo = jnp.maximum(qi - wblk, 0)
            res = []
            for h in (0, 1):
                qm = jnp.where(m_a if h == 0 else jnp.logical_not(m_a), qb, 0)

                def kloop(kj, st, h=h, qm=qm):
                    m, l, acc = st
                    cols = _rows(kj, blk)
                    s = lax.dot_general(qm, kn_s[cols, :], NT, preferred_element_type=F32)
                    if kb_ref is not None:
                        s = s + kb_ref[h, kj]
                    d = dif + (qi - kj) * blk
                    s = jnp.where((d >= 0) & (d <= window), s, NEG)
                    m2 = jnp.maximum(m, jnp.max(s, axis=1, keepdims=True))
                    al = jnp.exp(m - m2)
                    p = jnp.exp(s - m2)
                    l2 = al * l + jnp.sum(p, axis=1, keepdims=True)
                    acc2 = al * acc + jnp.dot(p.astype(BF16), vb_s[cols, :], preferred_element_type=F32)
                    return m2, l2, acc2

                init = (jnp.full((blk, 1), NEG, F32), jnp.zeros((blk, 1), F32), jnp.zeros((blk, LANES), F32))
                m, l, acc = lax.fori_loop(lo, qi + 1, kloop, init)
                res.append((acc / l, m + jnp.log(l)))
            o_ref[rows, :] = jnp.where(m_a, res[0][0], res[1][0])
            lse_ref[rows, :] = jnp.where(m_a, res[0][1], res[1][1])
            return carry

        lax.fori_loop(0, nb, qloop, 0)

    vec = pl.BlockSpec((1, LANES), lambda z, p: (0, 0))
    ins = [src, src, src, gq, gk]
    specs = [_col_spec(n, offs[0]), _col_spec(n, offs[1]), _col_spec(n, offs[2]), vec, vec]
    if rope is not None:
        ins += list(rope)
        specs += [pl.BlockSpec((None, n, LANES), lambda z, p: (z, 0, 0))] * 2
    if kbias is not None:
        ins.append(kbias)
        specs.append(pl.BlockSpec((None, 2, nb, 1, blk), lambda z, p: (z, p, 0, 0, 0)))
    ospec = _col_spec(n, 0)
    sds = jax.ShapeDtypeStruct((zs, n, LANES * npairs), F32)
    return _pcall(
        body, name=name, grid=(zs, npairs), in_specs=specs, out_specs=[ospec, ospec], out_shape=[sds, sds],
        scratch_shapes=[pltpu.VMEM((n, LANES), BF16)] * 3, compiler_params=_params(("parallel", "parallel")),
    )(*ins)


def _attn_bwd(src, offs, npairs, gq, gk, o, do, lse, *, rope=None, kbias=None, window, name):
    zs, n, _ = src.shape
    blk = min(ATT_BLK, n)
    nb = n // blk
    wblk = min(-(-window // blk), nb)

    def body(*refs):
        it = iter(refs)
        q_ref, k_ref, v_ref, gq_ref, gk_ref, o_ref, do_ref, lse_ref = (next(it) for _ in range(8))
        rope_refs = (next(it), next(it)) if rope is not None else None
        kb_ref = next(it) if kbias is not None else None
        dq_ref, dk_ref, dv_ref, dg_ref = (next(it) for _ in range(4))
        dkb_ref = next(it) if kbias is not None else None
        qn_s, kn_s, vb_s, dqn_s, dkn_s = (next(it) for _ in range(5))
        m_a = _mask_a()

        @pl.when((pl.program_id(0) == 0) & (pl.program_id(1) == 0))
        def _():
            dg_ref[...] = jnp.zeros_like(dg_ref)

        _prep_qk(q_ref, k_ref, v_ref, gq_ref, gk_ref, rope_refs, qn_s, kn_s, vb_s, nb, blk, m_a)
        dkn_s[...] = jnp.zeros_like(dkn_s)
        dv_ref[...] = jnp.zeros_like(dv_ref)
        if dkb_ref is not None:
            dkb_ref[...] = jnp.zeros_like(dkb_ref)
        dif = lax.broadcasted_iota(jnp.int32, (blk, blk), 0) - lax.broadcasted_iota(jnp.int32, (blk, blk), 1)

        def qloop(qi, carry):
            rows = _rows(qi, blk)
            qb = qn_s[rows, :]
            dob = do_ref[rows, :]
            lse_t = lse_ref[rows, :]
            delta_t = _half_sum(dob * o_ref[rows, :], m_a)
            lo = jnp.maximum(qi - wblk, 0)
            dqs = []
            for h in (0, 1):
                m_h = m_a if h == 0 else jnp.logical_not(m_a)
                qm = jnp.where(m_h, qb, 0)
                dom = jnp.where(m_h, dob, 0.0).astype(BF16)
                lse_c = _head_col(lse_t, h)
                delta_c = _head_col(delta_t, h)

                def kloop(kj, st, h=h, qm=qm, dom=dom, lse_c=lse_c, delta_c=delta_c):
                    dq_acc, rs = st
                    cols = _rows(kj, blk)
                    kn = kn_s[cols, :]
                    s = lax.dot_general(qm, kn, NT, preferred_element_type=F32)
                    if kb_ref is not None:
                        s = s + kb_ref[h, kj]
                    d = dif + (qi - kj) * blk
                    p = jnp.where((d >= 0) & (d <= window), jnp.exp(s - lse_c), 0.0)
                    dp = lax.dot_general(dom, vb_s[cols, :], NT, preferred_element_type=F32)
                    ds = p * (dp - delta_c)
                    dsb = ds.astype(BF16)
                    dkn_s[cols, :] += lax.dot_general(dsb, qm, TN, preferred_element_type=F32)
                    dv_ref[cols, :] += lax.dot_general(p.astype(BF16), dom, TN, preferred_element_type=F32)
                    if dkb_ref is not None:
                        dkb_ref[h, kj] += jnp.sum(ds, axis=0, keepdims=True)
                        rs = rs + jnp.sum(ds, axis=1, keepdims=True)
                    return dq_acc + jnp.dot(dsb, kn, preferred_element_type=F32), rs

                dq_h, rs = lax.fori_loop(lo, qi + 1, kloop, (jnp.zeros((blk, LANES), F32), jnp.zeros((blk, 1), F32)))
                dqs.append(dq_h)
                if dkb_ref is not None:
                    dkb_ref[h, qi] -= jnp.transpose(jnp.broadcast_to(rs, (blk, blk)))[0:1, :]
            dqn_s[rows, :] = jnp.where(m_a, dqs[0], dqs[1]) * SCALE
            return carry

        lax.fori_loop(0, nb, qloop, 0)

        def finish(c, carry):
            rows = _rows(c, blk)
            out = []
            for src_ref, g_ref, d_s, dst in ((q_ref, gq_ref, dqn_s, dq_ref), (k_ref, gk_ref, dkn_s, dk_ref)):
                xv = src_ref[rows, :]
                inv = _head_inv(xv, m_a)
                y = xv * inv
                dxn = d_s[rows, :]
                if rope_refs is not None:
                    dxn = dxn * rope_refs[0][rows, :] + _swap32(dxn * rope_refs[1][rows, :])
                dy = dxn * g_ref[...]
                dst[rows, :] = inv * (dy - y * (_half_sum(dy * y, m_a) * (1.0 / HEAD_DIM)))
                out.append(jnp.sum(dxn * y, axis=0, keepdims=True))
            return carry[0] + out[0], carry[1] + out[1]

        zero = jnp.zeros((1, LANES), F32)
        dgq, dgk = lax.fori_loop(0, nb, finish, (zero, zero))
        dg_ref[0:1, :] += dgq
        dg_ref[1:2, :] += dgk

    vec = pl.BlockSpec((1, LANES), lambda z, p: (0, 0))
    ospec = _col_spec(n, 0)
    ins = [src, src, src, gq, gk, o, do, lse]
    specs = [_col_spec(n, offs[0]), _col_spec(n, offs[1]), _col_spec(n, offs[2]), vec, vec, ospec, ospec, ospec]
    if rope is not None:
        ins += list(rope)
        specs += [pl.BlockSpec((None, n, LANES), lambda z, p: (z, 0, 0))] * 2
    sds = jax.ShapeDtypeStruct((zs, n, LANES * npairs), F32)
    out_shape = [sds, sds, sds, jax.ShapeDtypeStruct((8, LANES), F32)]
    out_specs = [ospec, ospec, ospec, pl.BlockSpec((8, LANES), lambda z, p: (0, 0))]
    if kbias is not None:
        ins.append(kbias)
        kbspec = pl.BlockSpec((None, 2, nb, 1, blk), lambda z, p: (z, p, 0, 0, 0))
        specs.append(kbspec)
        out_shape.append(jax.ShapeDtypeStruct(kbias.shape, F32))
        out_specs.append(kbspec)
    return _pcall(
        body, name=name, grid=(zs, npairs), in_specs=specs, out_specs=out_specs, out_shape=out_shape,
        scratch_shapes=[pltpu.VMEM((n, LANES), BF16)] * 3 + [pltpu.VMEM((n, LANES), F32)] * 2,
        compiler_params=_params(("arbitrary", "arbitrary")),
    )(*ins)


def _log_sig_pair(z):
    lsn = -(jnp.maximum(z, 0.0) + jnp.log(1.0 + jnp.exp(-jnp.abs(z))))
    return lsn, z + lsn


def _prep_sb(q_ref, k_ref, v_ref, qs_s, kb_s, vb_s, nb, blk):
    def prep(c, carry):
        rows = _rows(c, blk)
        qs_s[rows, :] = (q_ref[rows, :] * SCALE).astype(BF16)
        kb_s[rows, :] = k_ref[rows, :].astype(BF16)
        vb_s[rows, :] = v_ref[rows, :].astype(BF16)
        return carry

    lax.fori_loop(0, nb, prep, 0)


def _sb_fwd(src, offs, npairs, *, name):
    zs, n, _ = src.shape
    blk = min(ATT_BLK, n)
    nb = n // blk

    def body(q_ref, k_ref, v_ref, o_ref, lt_ref, qs_s, kb_s, vb_s):
        m_a = _mask_a()
        _prep_sb(q_ref, k_ref, v_ref, qs_s, kb_s, vb_s, nb, blk)
        dif = lax.broadcasted_iota(jnp.int32, (blk, blk), 0) - lax.broadcasted_iota(jnp.int32, (blk, blk), 1)
        u_gt = _tri(blk, lambda r, c: r > c)

        def qloop(qi, carry):
            rows = _rows(qi, blk)
            qb = qs_s[rows, :]
            res = []
            for h in (0, 1):
                qm = jnp.where(m_a if h == 0 else jnp.logical_not(m_a), qb, 0)

                def kloop(t, st, qm=qm):
                    c, acc = st
                    kj = qi - t
                    cols = _rows(kj, blk)
                    z = lax.dot_general(qm, kb_s[cols, :], NT, preferred_element_type=F32)
                    ok = (dif + t * blk) > 0
                    lsn, lsp = _log_sig_pair(z)
                    lsn = jnp.where(ok, lsn, 0.0)
                    later = c + _cumdot(lsn, u_gt, 3)
                    a = jnp.where(ok, jnp.exp(lsp + later), 0.0)
                    acc = acc + jnp.dot(a.astype(BF16), vb_s[cols, :], preferred_element_type=F32)
                    return c + jnp.sum(lsn, axis=1, keepdims=True), acc

                res.append(lax.fori_loop(0, qi + 1, kloop, (jnp.zeros((blk, 1), F32), jnp.zeros((blk, LANES), F32))))
            o_ref[rows, :] = jnp.where(m_a, res[0][1], res[1][1])
            lt_ref[rows, :] = jnp.where(m_a, res[0][0], res[1][0])
            return carry

        lax.fori_loop(0, nb, qloop, 0)

    ospec = _col_spec(n, 0)
    sds = jax.ShapeDtypeStruct((zs, n, LANES * npairs), F32)
    return _pcall(
        body, name=name, grid=(zs, npairs), in_specs=[_col_spec(n, offs[0]), _col_spec(n, offs[1]), _col_spec(n, offs[2])],
        out_specs=[ospec, ospec], out_shape=[sds, sds], scratch_shapes=[pltpu.VMEM((n, LANES), BF16)] * 3,
        compiler_params=_params(("parallel", "parallel")),
    )(src, src, src)


def _sb_bwd(src, offs, npairs, do, ltot, *, name):
    zs, n, _ = src.shape
    blk = min(ATT_BLK, n)
    nb = n // blk

    def body(q_ref, k_ref, v_ref, do_ref, lt_ref, dq_ref, dk_ref, dv_ref, qs_s, kb_s, vb_s):
        m_a = _mask_a()
        _prep_sb(q_ref, k_ref, v_ref, qs_s, kb_s, vb_s, nb, blk)
        dk_ref[...] = jnp.zeros_like(dk_ref)
        dv_ref[...] = jnp.zeros_like(dv_ref)
        dif = lax.broadcasted_iota(jnp.int32, (blk, blk), 0) - lax.broadcasted_iota(jnp.int32, (blk, blk), 1)
        u_le = _tri(blk, lambda r, c: r <= c)
        u_lt = _tri(blk, lambda r, c: r < c)

        def qloop(qi, carry):
            rows = _rows(qi, blk)
            qb = qs_s[rows, :]
            dob = do_ref[rows, :]
            lt_t = lt_ref[rows, :]
            dqs = []
            for h in (0, 1):
                m_h = m_a if h == 0 else jnp.logical_not(m_a)
                qm = jnp.where(m_h, qb, 0)
                dom = jnp.where(m_h, dob, 0.0).astype(BF16)
                lt_c = _head_col(lt_t, h)

                def kloop(kj, st, qm=qm, dom=dom, lt_c=lt_c):
                    lp, ep, dq_acc = st
                    cols = _rows(kj, blk)
                    kb = kb_s[cols, :]
                    z = lax.dot_general(qm, kb, NT, preferred_element_type=F32)
                    ok = (dif + (qi - kj) * blk) > 0
                    lsn, lsp = _log_sig_pair(z)
                    lsn = jnp.where(ok, lsn, 0.0)
                    later = lt_c - (lp + _cumdot(lsn, u_le, 3))
                    a = jnp.where(ok, jnp.exp(lsp + later), 0.0)
                    da = lax.dot_general(dom, vb_s[cols, :], NT, preferred_element_type=F32)
                    e = da * a
                    big_e = ep + _cumdot(e, u_lt, 2)
                    sig = jax.nn.sigmoid(z)
                    dz = jnp.where(ok, e * (1.0 - sig) - big_e * sig, 0.0)
                    dzb = dz.astype(BF16)
                    dk_ref[cols, :] += lax.dot_general(dzb, qm, TN, preferred_element_type=F32)
                    dv_ref[cols, :] += lax.dot_general(a.astype(BF16), dom, TN, preferred_element_type=F32)
                    dq_acc = dq_acc + jnp.dot(dzb, kb, preferred_element_type=F32)
                    return lp + jnp.sum(lsn, axis=1, keepdims=True), ep + jnp.sum(e, axis=1, keepdims=True), dq_acc

                zero = jnp.zeros((blk, 1), F32)
                dqs.append(lax.fori_loop(0, qi + 1, kloop, (zero, zero, jnp.zeros((blk, LANES), F32)))[2])
            dq_ref[rows, :] = jnp.where(m_a, dqs[0], dqs[1]) * SCALE
            return carry

        lax.fori_loop(0, nb, qloop, 0)

    ospec = _col_spec(n, 0)
    sds = jax.ShapeDtypeStruct((zs, n, LANES * npairs), F32)
    return _pcall(
        body, name=name, grid=(zs, npairs),
        in_specs=[_col_spec(n, offs[0]), _col_spec(n, offs[1]), _col_spec(n, offs[2]), ospec, ospec],
        out_specs=[ospec] * 3, out_shape=[sds] * 3, scratch_shapes=[pltpu.VMEM((n, LANES), BF16)] * 3,
        compiler_params=_params(("parallel", "parallel")),
    )(src, src, src, do, ltot)


def _fox_gate_fwd(lg, bias, *, name):
    bs, nh, t = lg.shape
    blk = min(LANES, t)

    def body(lg_ref, b_ref, kb_ref):
        u_le = _tri(blk, lambda r, c: r <= c)
        carry = jnp.zeros((nh, 1), F32)
        for j in range(t // blk):
            sl = slice(j * blk, (j + 1) * blk)
            xv = lg_ref[:, sl] + b_ref[...]
            lf = jnp.minimum(xv, 0.0) - jnp.log(1.0 + jnp.exp(-jnp.abs(xv)))
            kb_ref[:, sl] = -(carry + _cumdot(lf, u_le, 3))
            carry = carry + jnp.sum(lf, axis=1, keepdims=True)

    spec = pl.BlockSpec((None, nh, t), lambda i: (i, 0, 0))
    return _pcall(
        body, name=name, grid=(bs,), in_specs=[spec, pl.BlockSpec((nh, 1), lambda i: (0, 0))], out_specs=spec,
        out_shape=jax.ShapeDtypeStruct((bs, nh, t), F32), compiler_params=_params(("parallel",)),
    )(lg, bias)


def _fox_gate_bwd(dkb, lg, bias, *, name):
    bs, nh, t = lg.shape
    blk = min(LANES, t)

    def body(dkb_ref, lg_ref, b_ref, dlg_ref, db_ref):
        @pl.when(pl.program_id(0) == 0)
        def _():
            db_ref[...] = jnp.zeros_like(db_ref)

        u_ge = _tri(blk, lambda r, c: r >= c)
        carry = jnp.zeros((nh, 1), F32)
        tot = jnp.zeros((nh, 1), F32)
        for j in reversed(range(t // blk)):
            sl = slice(j * blk, (j + 1) * blk)
            df = -dkb_ref[:, sl]
            dlf = carry + _cumdot(df, u_ge, 3)
            carry = carry + jnp.sum(df, axis=1, keepdims=True)
            xv = lg_ref[:, sl] + b_ref[...]
            dlg = dlf * jax.nn.sigmoid(-xv)
            dlg_ref[:, sl] = dlg
            tot = tot + jnp.sum(dlg, axis=1, keepdims=True)
        db_ref[...] += jnp.broadcast_to(tot, db_ref.shape)

    spec = pl.BlockSpec((None, nh, t), lambda i: (i, 0, 0))
    return _pcall(
        body, name=name, grid=(bs,), in_specs=[spec, spec, pl.BlockSpec((nh, 1), lambda i: (0, 0))],
        out_specs=[spec, pl.BlockSpec((nh, LANES), lambda i: (0, 0))],
        out_shape=[jax.ShapeDtypeStruct((bs, nh, t), F32), jax.ShapeDtypeStruct((nh, LANES), F32)],
        compiler_params=_params(("arbitrary",)),
    )(dkb, lg, bias)


def _place():
    return lax.axis_index("x"), lax.axis_index("y"), lax.axis_index("c")


def _flip(v, f):
    return 1 - v if f else v


FLIPS = [(fx, fy, fc) for fx in (0, 1) for fy in (0, 1) for fc in (0, 1)][1:]


def _gather_big(blob, *, name):
    r, c = blob.shape

    def body(x_ref, out_ref, send_sems, recv_sems, local_sem):
        x, y, cc = _place()
        me, sibling = (x, y, cc), (x, y, 1 - cc)
        chips = [(1 - x, y), (x, 1 - y), (1 - x, 1 - y)]

        def slot(px, py, pc):
            return out_ref.at[4 * px + 2 * py + pc]

        def copy(k, block, to, src=None):
            return pltpu.make_async_remote_copy(
                src_ref=slot(*block) if src is None else src, dst_ref=slot(*block),
                send_sem=send_sems.at[k], recv_sem=recv_sems.at[k], device_id=to, device_id_type=MESH)

        mine = pltpu.make_async_copy(x_ref, slot(*me), local_sem)
        mine.start()
        first = [copy(0, me, sibling, src=x_ref)]
        first += [copy(1 + j, me, (*chip, cc), src=x_ref) for j, chip in enumerate(chips)]
        for cp in first:
            cp.start()
        passed = [copy(4 + j, (*chip, cc), sibling) for j, chip in enumerate(chips)]
        for j, chip in enumerate(chips):
            copy(1 + j, (*chip, cc), me).wait_recv()
            passed[j].start()
        copy(0, sibling, me).wait_recv()
        for j, chip in enumerate(chips):
            copy(4 + j, (*chip, 1 - cc), me).wait_recv()
        for cp in first + passed:
            cp.wait_send()
        mine.wait()

    hbm = pl.BlockSpec(memory_space=pl.ANY)
    return _pcall(
        body, name=name, in_specs=[hbm], out_specs=hbm, out_shape=jax.ShapeDtypeStruct((N_DEV, r, c), blob.dtype),
        scratch_shapes=[pltpu.SemaphoreType.DMA((7,)), pltpu.SemaphoreType.DMA((7,)), pltpu.SemaphoreType.DMA(())],
    )(blob)


def _scatter_big(g, *, name):
    _, r, c = g.shape

    def body(g_ref, recv_ref, send_sems, recv_sems, local_sem):
        x, y, cc = _place()
        my = 4 * x + 2 * y + cc
        mine = pltpu.make_async_copy(g_ref.at[my], recv_ref.at[my], local_sem)
        mine.start()
        copies = []
        for k, (fx, fy, fc) in enumerate(FLIPS):
            px, py, pc = _flip(x, fx), _flip(y, fy), _flip(cc, fc)
            copies.append(pltpu.make_async_remote_copy(
                src_ref=g_ref.at[4 * px + 2 * py + pc], dst_ref=recv_ref.at[my],
                send_sem=send_sems.at[k], recv_sem=recv_sems.at[k], device_id=(px, py, pc), device_id_type=MESH))
        for cp in copies:
            cp.start()
        for cp in copies:
            cp.wait_recv()
        for cp in copies:
            cp.wait_send()
        mine.wait()

    hbm = pl.BlockSpec(memory_space=pl.ANY)
    return _pcall(
        body, name=name, in_specs=[hbm], out_specs=hbm, out_shape=jax.ShapeDtypeStruct(g.shape, g.dtype),
        scratch_shapes=[pltpu.SemaphoreType.DMA((7,)), pltpu.SemaphoreType.DMA((7,)), pltpu.SemaphoreType.DMA(())],
    )(g)


def _allreduce_small(blob, *, name):
    r, c = blob.shape

    def body(x_ref, out_ref, buf, send_sems, recv_sems):
        x, y, cc = _place()
        my = 4 * x + 2 * y + cc
        copies = []
        for k, (fx, fy, fc) in enumerate(FLIPS):
            peer = (_flip(x, fx), _flip(y, fy), _flip(cc, fc))
            copies.append(pltpu.make_async_remote_copy(
                src_ref=x_ref, dst_ref=buf.at[my], send_sem=send_sems.at[k], recv_sem=recv_sems.at[k],
                device_id=peer, device_id_type=MESH))
        for cp in copies:
            cp.start()
        buf[my] = x_ref[...]
        for cp in copies:
            cp.wait_recv()
        for cp in copies:
            cp.wait_send()
        acc = buf[0]
        for i in range(1, N_DEV):
            acc = acc + buf[i]
        out_ref[...] = acc

    vmem = pl.BlockSpec(memory_space=pltpu.VMEM)
    return _pcall(
        body, name=name, in_specs=[vmem], out_specs=vmem, out_shape=jax.ShapeDtypeStruct((r, c), F32),
        scratch_shapes=[pltpu.VMEM((N_DEV, r, c), F32), pltpu.SemaphoreType.DMA((7,)), pltpu.SemaphoreType.DMA((7,))],
    )(blob)


BIG = ("w_in", "w_up_fox", "w_up_sb", "w_up_dil", "w_out", "w_mlp_in", "w_mlp_out")
ROW_SHARDED = ("w_out", "w_mlp_out")
SMALL = ("attn_norm", "b_forget", "q_norm_fox", "k_norm_fox", "q_norm_dil", "k_norm_dil", "mlp_norm")
BLOB_ROWS = 512


def _pack(parts, dtype):
    flat = jnp.concatenate([p.reshape(-1).astype(dtype) for p in parts])
    size = -(-flat.shape[0] // (BLOB_ROWS * LANES)) * (BLOB_ROWS * LANES)
    return jnp.pad(flat, (0, size - flat.shape[0])).reshape(-1, LANES)


def _pack_slots(parts, dtype):
    flat = jnp.concatenate([p.reshape(N_DEV, -1).astype(dtype) for p in parts], axis=1)
    size = -(-flat.shape[1] // (BLOB_ROWS * LANES)) * (BLOB_ROWS * LANES)
    return jnp.pad(flat, ((0, 0), (0, size - flat.shape[1]))).reshape(N_DEV, -1, LANES)


def _unpack(blob, shapes):
    lead = blob.shape[:-2]
    flat = blob.reshape(*lead, -1)
    out, off = [], 0
    for shp in shapes:
        size = 1
        for s in shp:
            size *= s
        out.append(flat[..., off:off + size].reshape(*lead, *shp))
        off += size
    return out


def _join_shards(name, sh):
    if name in ROW_SHARDED:
        return jnp.transpose(sh, (1, 0, 2, 3)).reshape(sh.shape[1], -1, sh.shape[3])
    return jnp.transpose(sh, (1, 2, 0, 3)).reshape(sh.shape[1], sh.shape[2], -1)


def _split_shards(name, full):
    dep, a, b = full.shape
    if name in ROW_SHARDED:
        return jnp.transpose(full.reshape(dep, N_DEV, a // N_DEV, b), (1, 0, 2, 3))
    return jnp.transpose(full.reshape(dep, a, N_DEV, b // N_DEV), (2, 0, 1, 3))


def _permute_in(w, dp):
    o1 = 3 * W_FOX
    o2 = o1 + N_HEADS_FOX
    pad = [(0, 0)] * (w.ndim - 1) + [(0, dp - w.shape[-1])]
    return jnp.pad(jnp.concatenate([w[..., :o1], w[..., o2:], w[..., o1:o2]], axis=-1), pad)


def _unpermute_in(wp, d_in):
    o1 = 3 * W_FOX
    fg = d_in - N_HEADS_FOX
    return jnp.concatenate([wp[..., :o1], wp[..., fg:d_in], wp[..., o1:fg]], axis=-1)


def _to_streams(a, r):
    b, t, c = a.shape
    if r == 1:
        return a
    return jnp.transpose(a.reshape(b, t // r, r, c), (0, 2, 1, 3)).reshape(b * r, t // r, c)


def _from_streams(a, r, b):
    if r == 1:
        return a
    z, n, c = a.shape
    return jnp.transpose(a.reshape(b, r, n, c), (0, 2, 1, 3)).reshape(b, n * r, c)


def _rope_tables(positions):
    half = HEAD_DIM // 2
    inv = 1.0 / (ROPE_THETA ** (jnp.arange(half, dtype=F32) / half))
    ang = positions.astype(F32)[..., None] * inv
    cos, sin = jnp.cos(ang), jnp.sin(ang)
    return jnp.tile(cos, (1, 1, 4)), jnp.tile(jnp.concatenate([-sin, sin], axis=-1), (1, 1, 2))


def _gain2(g):
    return jnp.tile(g.reshape(1, HEAD_DIM), (1, 2))


def _layer_fwd(l, x, w, small, ropes, bl, t):
    n, d = x.shape
    s = {}
    s["x"] = x
    s["h"] = _rmsnorm_fwd(x, small["attn_norm"][l].reshape(1, d), name=f"norm_attn_fwd{l}")
    proj = _mm(s["h"], w["w_in"][l], name=f"mm_proj{l}")
    s["proj"] = proj
    dp = proj.shape[1]
    proj3 = proj.reshape(bl, t, dp)
    p_fg = P_GATE + 3 * d

    lg = jnp.transpose(proj3[:, :, p_fg:p_fg + N_HEADS_FOX], (0, 2, 1))
    s["lg"] = lg
    kb = _fox_gate_fwd(lg, small["b_forget"][l].reshape(N_HEADS_FOX, 1), name=f"fox_gate_fwd{l}")
    blk = min(ATT_BLK, t)
    kb5 = kb.reshape(bl, N_HEADS_FOX, t // blk, 1, blk)
    s["kb5"] = kb5
    gqf, gkf = _gain2(small["q_norm_fox"][l]), _gain2(small["k_norm_fox"][l])
    fo = P_FOX // LANES
    fox_offs = (fo, fo + W_FOX // LANES, fo + 2 * W_FOX // LANES)
    out_a, lse_a = _attn_fwd(proj3, fox_offs, N_HEADS_FOX // 2, gqf, gkf, kbias=kb5, window=t, name=f"fox_fwd{l}")
    s["out_a"], s["lse_a"] = out_a, lse_a

    so = P_SB // LANES
    sb_offs = (so, so + W_SB // LANES, so + 2 * W_SB // LANES)
    out_b, lt_b = _sb_fwd(proj3, sb_offs, N_HEADS_SB // 2, name=f"sb_fwd{l}")
    s["out_b"], s["lt_b"] = out_b, lt_b

    gqd, gkd = _gain2(small["q_norm_dil"][l]), _gain2(small["k_norm_dil"][l])
    os_, lses, s["dil_src"] = [], [], []
    for g, (window, r) in enumerate(DIL_PATTERNS):
        c0 = P_DIL + g * W_DIL
        if r == 1:
            src, offs = proj3, (c0 // LANES, (c0 + W_DILQ) // LANES, (c0 + 2 * W_DILQ) // LANES)
        else:
            qkv = jnp.concatenate([proj3[:, :, c0 + i * W_DILQ:c0 + i * W_DILQ + W_DIL] for i in range(3)], axis=-1)
            src, offs = _to_streams(qkv, r), (0, W_DIL // LANES, 2 * W_DIL // LANES)
        s["dil_src"].append((src, offs))
        o_g, lse_g = _attn_fwd(src, offs, N_HEADS_DIL // 2, gqd, gkd, rope=ropes[g], window=window // r, name=f"dil_fwd{l}_{g}")
        os_.append(_from_streams(o_g, r, bl).reshape(n, W_DIL))
        lses.append(_from_streams(lse_g, r, bl).reshape(n, W_DIL))
    out_c, lse_c = _dil_combine(os_, lses, name=f"dil_combine{l}")
    s["out_c"], s["lse_c"] = out_c, lse_c

    ys = [_mm(out_a.reshape(n, W_FOX), w["w_up_fox"][l], name=f"mm_up_fox{l}"),
          _mm(out_b.reshape(n, W_SB), w["w_up_sb"][l], name=f"mm_up_sb{l}"),
          _mm(out_c, w["w_up_dil"][l], name=f"mm_up_dil{l}")]
    s["ys"] = ys
    s["merged"] = _gate_merge_fwd(proj, ys, name=f"gate_merge_fwd{l}")
    x1 = _mm(s["merged"], w["w_out"][l], add=x, name=f"mm_out{l}")
    s["x1"] = x1

    s["h2"] = _rmsnorm_fwd(x1, small["mlp_norm"][l].reshape(1, d), name=f"norm_mlp_fwd{l}")
    s["u"] = _mm(s["h2"], w["w_mlp_in"][l], name=f"mm_mlp_in{l}")
    s["a"] = _relu2_fwd(s["u"], name=f"relu2_fwd{l}")
    x2 = _mm(s["a"], w["w_mlp_out"][l], add=x1, name=f"mm_mlp_out{l}")
    return x2, s


def _layer_bwd(l, dx2, s, w, small, ropes, bl, t):
    n, d = dx2.shape
    gw, gs = {}, {}
    da = _mm(dx2, w["w_mlp_out"][l], tb=True, name=f"mm_da{l}")
    du = _relu2_bwd(da, s["u"], name=f"relu2_bwd{l}")
    gw["w_mlp_out"] = _mm(s["a"], dx2, ta=True, name=f"mm_dw_mlp_out{l}")
    gw["w_mlp_in"] = _mm(s["h2"], du, ta=True, name=f"mm_dw_mlp_in{l}")
    dh2 = _mm(du, w["w_mlp_in"][l], tb=True, name=f"mm_dh2{l}")
    dx1, gs["mlp_norm"] = _rmsnorm_bwd(s["x1"], small["mlp_norm"][l].reshape(1, d), dh2, dx2, name=f"norm_mlp_bwd{l}")

    dmerged = _mm(dx1, w["w_out"][l], tb=True, name=f"mm_dmerged{l}")
    gw["w_out"] = _mm(s["merged"], dx1, ta=True, name=f"mm_dw_out{l}")
    dya, dyb, dyc, dgl0, dgl1, dgl2 = _gate_merge_bwd(s["proj"], s["ys"], dmerged, name=f"gate_merge_bwd{l}")
    out_a2, out_b2 = s["out_a"].reshape(n, W_FOX), s["out_b"].reshape(n, W_SB)
    gw["w_up_fox"] = _mm(out_a2, dya, ta=True, name=f"mm_dw_up_fox{l}")
    gw["w_up_sb"] = _mm(out_b2, dyb, ta=True, name=f"mm_dw_up_sb{l}")
    gw["w_up_dil"] = _mm(s["out_c"], dyc, ta=True, name=f"mm_dw_up_dil{l}")
    dout_a = _mm(dya, w["w_up_fox"][l], tb=True, name=f"mm_dout_a{l}").reshape(bl, t, W_FOX)
    dout_b = _mm(dyb, w["w_up_sb"][l], tb=True, name=f"mm_dout_b{l}").reshape(bl, t, W_SB)
    dout_c = _mm(dyc, w["w_up_dil"][l], tb=True, name=f"mm_dout_c{l}").reshape(bl, t, W_DIL)

    proj3 = s["proj"].reshape(bl, t, -1)
    gqf, gkf = _gain2(small["q_norm_fox"][l]), _gain2(small["k_norm_fox"][l])
    fo = P_FOX // LANES
    fox_offs = (fo, fo + W_FOX // LANES, fo + 2 * W_FOX // LANES)
    dq_a, dk_a, dv_a, dg_a, dkb5 = _attn_bwd(proj3, fox_offs, N_HEADS_FOX // 2, gqf, gkf, s["out_a"], dout_a, s["lse_a"],
                                             kbias=s["kb5"], window=t, name=f"fox_bwd{l}")
    gs["fox_gains"] = dg_a
    dlg, gs["b_forget"] = _fox_gate_bwd(dkb5.reshape(bl, N_HEADS_FOX, t), s["lg"], small["b_forget"][l].reshape(N_HEADS_FOX, 1),
                                        name=f"fox_gate_bwd{l}")
    so = P_SB // LANES
    sb_offs = (so, so + W_SB // LANES, so + 2 * W_SB // LANES)
    dq_b, dk_b, dv_b = _sb_bwd(proj3, sb_offs, N_HEADS_SB // 2, dout_b, s["lt_b"], name=f"sb_bwd{l}")
    gqd, gkd = _gain2(small["q_norm_dil"][l]), _gain2(small["k_norm_dil"][l])
    out_c3, lse_c3 = s["out_c"].reshape(bl, t, W_DIL), s["lse_c"].reshape(bl, t, W_DIL)
    dqs, dks, dvs, dgd = [], [], [], None
    for g, (window, r) in enumerate(DIL_PATTERNS):
        src, offs = s["dil_src"][g]
        dq_g, dk_g, dv_g, dg_g = _attn_bwd(src, offs, N_HEADS_DIL // 2, gqd, gkd, _to_streams(out_c3, r), _to_streams(dout_c, r),
                                           _to_streams(lse_c3, r), rope=ropes[g], window=window // r, name=f"dil_bwd{l}_{g}")
        dqs.append(_from_streams(dq_g, r, bl))
        dks.append(_from_streams(dk_g, r, bl))
        dvs.append(_from_streams(dv_g, r, bl))
        dgd = dg_g if dgd is None else jnp.concatenate([dgd, dg_g], axis=0)
    gs["dil_gains"] = dgd

    dp = s["proj"].shape[1]
    parts = [dq_a, dk_a, dv_a, dq_b, dk_b, dv_b] + dqs + dks + dvs
    parts = [p.reshape(n, -1).astype(BF16) for p in parts] + [dgl0, dgl1, dgl2, jnp.transpose(dlg, (0, 2, 1)).reshape(n, -1).astype(BF16)]
    used = P_GATE + 3 * d + N_HEADS_FOX
    dproj = jnp.concatenate(parts + [jnp.zeros((n, dp - used), BF16)], axis=1)
    gw["w_in"] = _mm(s["h"], dproj, ta=True, name=f"mm_dw_in{l}")
    dh = _mm(dproj, w["w_in"][l], tb=True, name=f"mm_dh{l}")
    dx, gs["attn_norm"] = _rmsnorm_bwd(s["x"], small["attn_norm"][l].reshape(1, d), dh, dx1, name=f"norm_attn_bwd{l}")
    return dx, gw, gs


def kernel(x, positions, attn_norm, w_in, b_forget, q_norm_fox, k_norm_fox, q_norm_dil, k_norm_dil, w_up_fox, w_up_sb, w_up_dil, w_out, mlp_norm, w_mlp_in, w_mlp_out, loss_target, m_attn_norm, m_w_in, m_b_forget, m_q_norm_fox, m_k_norm_fox, m_q_norm_dil, m_k_norm_dil, m_w_up_fox, m_w_up_sb, m_w_up_dil, m_w_out, m_mlp_norm, m_w_mlp_in, m_w_mlp_out, v_attn_norm, v_w_in, v_b_forget, v_q_norm_fox, v_k_norm_fox, v_q_norm_dil, v_k_norm_dil, v_w_up_fox, v_w_up_sb, v_w_up_dil, v_w_out, v_mlp_norm, v_w_mlp_in, v_w_mlp_out):
    bl, t, d = x.shape
    n = bl * t
    depth = attn_norm.shape[0]
    wl = dict(w_in=w_in, w_up_fox=w_up_fox, w_up_sb=w_up_sb, w_up_dil=w_up_dil, w_out=w_out, w_mlp_in=w_mlp_in, w_mlp_out=w_mlp_out)
    ml = dict(w_in=m_w_in, w_up_fox=m_w_up_fox, w_up_sb=m_w_up_sb, w_up_dil=m_w_up_dil, w_out=m_w_out, w_mlp_in=m_w_mlp_in, w_mlp_out=m_w_mlp_out)
    vl = dict(w_in=v_w_in, w_up_fox=v_w_up_fox, w_up_sb=v_w_up_sb, w_up_dil=v_w_up_dil, w_out=v_w_out, w_mlp_in=v_w_mlp_in, w_mlp_out=v_w_mlp_out)
    small = dict(attn_norm=attn_norm, b_forget=b_forget, q_norm_fox=q_norm_fox, k_norm_fox=k_norm_fox, q_norm_dil=q_norm_dil,
                 k_norm_dil=k_norm_dil, mlp_norm=mlp_norm)
    m_small = dict(attn_norm=m_attn_norm, b_forget=m_b_forget, q_norm_fox=m_q_norm_fox, k_norm_fox=m_k_norm_fox,
                   q_norm_dil=m_q_norm_dil, k_norm_dil=m_k_norm_dil, mlp_norm=m_mlp_norm)
    v_small = dict(attn_norm=v_attn_norm, b_forget=v_b_forget, q_norm_fox=v_q_norm_fox, k_norm_fox=v_k_norm_fox,
                   q_norm_dil=v_q_norm_dil, k_norm_dil=v_k_norm_dil, mlp_norm=v_mlp_norm)

    shard_shapes = [wl[k].shape for k in BIG]
    gathered = _gather_big(_pack([wl[k] for k in BIG], BF16), name="gather_weights")
    w = {k: _join_shards(k, sh) for k, sh in zip(BIG, _unpack(gathered, shard_shapes))}
    d_in = w["w_in"].shape[-1]
    dp = -(-(d_in) // 512) * 512
    w["w_in"] = _permute_in(w["w_in"], dp)

    cos, sin = _rope_tables(positions)
    ropes = [(_to_streams(cos, r), _to_streams(sin, r)) for _, r in DIL_PATTERNS]

    h = x.reshape(n, d)
    saved = []
    for l in range(depth):
        h, s = _layer_fwd(l, h, w, small, ropes, bl, t)
        saved.append(s)
    dy, loss_part = _loss_head(h, loss_target.reshape(n, d), name="loss_head")

    gws, gss = [None] * depth, [None] * depth
    for l in reversed(range(depth)):
        dy, gws[l], gss[l] = _layer_bwd(l, dy, saved[l], w, small, ropes, bl, t)
    grad_x = dy.reshape(bl, t, d)

    full = {k: jnp.stack([gws[l][k] for l in range(depth)]) for k in BIG}
    full["w_in"] = _unpermute_in(full["w_in"], d_in)
    slots = _pack_slots([_split_shards(k, full[k]) for k in BIG], BF16)
    recv = _scatter_big(slots, name="scatter_grads")
    outs = _adamw(recv, _pack([wl[k] for k in BIG], F32), _pack([ml[k] for k in BIG], F32), _pack([vl[k] for k in BIG], F32),
                  name="adamw_big")
    g_big, d_big, m_big, v_big = (dict(zip(BIG, _unpack(o, shard_shapes))) for o in outs)

    rows = [loss_part]
    for l in range(depth):
        gs = gss[l]
        rows += [gs["attn_norm"].reshape(-1, LANES), gs["mlp_norm"].reshape(-1, LANES), gs["fox_gains"], gs["dil_gains"], gs["b_forget"]]
    row_counts = [r.shape[0] for r in rows]
    part = jnp.concatenate(rows, axis=0)
    pad_rows = -(-part.shape[0] // 8) * 8 - part.shape[0]
    summed = _allreduce_small(jnp.pad(part, ((0, pad_rows), (0, 0))), name="allreduce_small")
    pieces, off = [], 0
    for c in row_counts:
        pieces.append(summed[off:off + c])
        off += c
    loss = pieces[0][0, 0]

    def fold(row):
        return row[:HEAD_DIM] + row[HEAD_DIM:]

    g_small = {k: [] for k in SMALL}
    for l in range(depth):
        an, mn, fg, dg, bf = pieces[1 + 5 * l:6 + 5 * l]
        g_small["attn_norm"].append(an.reshape(d))
        g_small["mlp_norm"].append(mn.reshape(d))
        g_small["q_norm_fox"].append(fold(fg[0]))
        g_small["k_norm_fox"].append(fold(fg[1]))
        g_small["q_norm_dil"].append(fold(dg[0]) + fold(dg[8]) + fold(dg[16]))
        g_small["k_norm_dil"].append(fold(dg[1]) + fold(dg[9]) + fold(dg[17]))
        g_small["b_forget"].append(bf[:, 0])
    g_small = {k: jnp.stack(vs) for k, vs in g_small.items()}
    small_shapes = [small[k].shape for k in SMALL]
    outs = _adamw(_pack([g_small[k] for k in SMALL], F32)[None], _pack([small[k] for k in SMALL], F32),
                  _pack([m_small[k] for k in SMALL], F32), _pack([v_small[k] for k in SMALL], F32), name="adamw_small")
    g_sm, d_sm, m_sm, v_sm = (dict(zip(SMALL, _unpack(o, small_shapes))) for o in outs)

    order = ("attn_norm", "w_in", "b_forget", "q_norm_fox", "k_norm_fox", "q_norm_dil", "k_norm_dil", "w_up_fox", "w_up_sb",
             "w_up_dil", "w_out", "mlp_norm", "w_mlp_in", "w_mlp_out")
    res = [loss, grad_x]
    for big, sm in ((g_big, g_sm), (d_big, d_sm), (m_big, m_sm), (v_big, v_sm)):
        res += [big[k] if k in big else sm[k] for k in order]
    return tuple(res)
```

```python
import jax
import jax.numpy as jnp
from jax import lax
from jax.experimental import pallas as pl
from jax.experimental.pallas import tpu as pltpu

F32 = jnp.float32
BF16 = jnp.bfloat16

HEAD_DIM = 64
LANES = 128
N_HEADS_FOX = 8
N_HEADS_SB = 8
N_HEADS_DIL = 4
DIL_PATTERNS = ((128, 1), (512, 4), (2048, 16))
ROPE_THETA = 10000.0
EPS = 1e-6
SCALE = 0.125
W_FOX = N_HEADS_FOX * HEAD_DIM
W_SB = N_HEADS_SB * HEAD_DIM
W_DIL = N_HEADS_DIL * HEAD_DIM
W_DILQ = len(DIL_PATTERNS) * W_DIL
P_FOX = 0
P_SB = 3 * W_FOX
P_DIL = P_SB + 3 * W_SB
P_GATE = P_DIL + 3 * W_DILQ
N_DEV = 8
ATT_BLK = 256
NEG = -1e30
VMEM_LIMIT = 56 * 1024 * 1024
ADAMW_BLOCK_ELEMS = 128 * 1024

ADAM_LR = 0.001
ADAM_B1 = 0.9
ADAM_B2 = 0.999
ADAM_EPS = 1e-08
ADAM_WD = 0.01
ADAM_STEP = 10

NT = (((1,), (1,)), ((), ()))
TN = (((0,), (0,)), ((), ()))
MESH = pl.DeviceIdType.MESH


def _pcall(body, **kw):
    return pl.pallas_call(body, **kw)


def _params(sem=None):
    return pltpu.CompilerParams(dimension_semantics=sem, vmem_limit_bytes=VMEM_LIMIT)


def _tile(dim, target, mult=LANES):
    t = (min(dim, target) // mult) * mult
    while t >= mult:
        if dim % t == 0:
            return t
        t -= mult
    return dim


def _mm(a, b, *, ta=False, tb=False, add=None, out_dtype=F32, name, tm=1024, tn=512, tk=1024):
    m, k = (a.shape[1], a.shape[0]) if ta else a.shape
    n = b.shape[0] if tb else b.shape[1]
    tm, tn, tk = _tile(m, tm), _tile(n, tn), _tile(k, tk)
    nk = k // tk
    dn = (((0,) if ta else (1,), (1,) if tb else (0,)), ((), ()))

    def body(*refs):
        if add is None:
            a_ref, b_ref, o_ref, acc = refs
        else:
            a_ref, b_ref, add_ref, o_ref, acc = refs
        kk = pl.program_id(2)
        part = lax.dot_general(a_ref[...].astype(BF16), b_ref[...].astype(BF16), dn, preferred_element_type=F32)

        @pl.when(kk == 0)
        def _():
            acc[...] = part

        @pl.when(kk > 0)
        def _():
            acc[...] += part

        @pl.when(kk == nk - 1)
        def _():
            r = acc[...]
            if add is not None:
                r = r + add_ref[...]
            o_ref[...] = r.astype(out_dtype)

    a_spec = pl.BlockSpec((tk, tm), lambda i, j, q: (q, i)) if ta else pl.BlockSpec((tm, tk), lambda i, j, q: (i, q))
    b_spec = pl.BlockSpec((tn, tk), lambda i, j, q: (j, q)) if tb else pl.BlockSpec((tk, tn), lambda i, j, q: (q, j))
    o_spec = pl.BlockSpec((tm, tn), lambda i, j, q: (i, j))
    ins, specs = [a, b], [a_spec, b_spec]
    if add is not None:
        ins.append(add)
        specs.append(o_spec)
    return _pcall(
        body, name=name, grid=(m // tm, n // tn, nk), in_specs=specs, out_specs=o_spec,
        out_shape=jax.ShapeDtypeStruct((m, n), out_dtype), scratch_shapes=[pltpu.VMEM((tm, tn), F32)],
        compiler_params=_params(("parallel", "parallel", "arbitrary")),
    )(*ins)


def _rmsnorm_fwd(x, g, *, name):
    n, d = x.shape
    tm = _tile(n, 256, 8)

    def body(x_ref, g_ref, h_ref):
        xv = x_ref[...]
        inv = lax.rsqrt(jnp.mean(xv * xv, axis=1, keepdims=True) + EPS)
        h_ref[...] = (xv * inv * g_ref[...]).astype(BF16)

    row = pl.BlockSpec((tm, d), lambda i: (i, 0))
    return _pcall(
        body, name=name, grid=(n // tm,), in_specs=[row, pl.BlockSpec((1, d), lambda i: (0, 0))], out_specs=row,
        out_shape=jax.ShapeDtypeStruct((n, d), BF16), compiler_params=_params(("parallel",)),
    )(x, g)


def _rmsnorm_bwd(x, g, dh, dres, *, name):
    n, d = x.shape
    tm = _tile(n, 256, 8)

    def body(x_ref, g_ref, dh_ref, dres_ref, dx_ref, dg_ref):
        @pl.when(pl.program_id(0) == 0)
        def _():
            dg_ref[...] = jnp.zeros_like(dg_ref)

        xv = x_ref[...]
        inv = lax.rsqrt(jnp.mean(xv * xv, axis=1, keepdims=True) + EPS)
        y = xv * inv
        dhv = dh_ref[...]
        dg_ref[...] += jnp.sum(dhv * y, axis=0, keepdims=True)
        dy = dhv * g_ref[...]
        dx_ref[...] = dres_ref[...] + inv * (dy - y * jnp.mean(dy * y, axis=1, keepdims=True))

    row = pl.BlockSpec((tm, d), lambda i: (i, 0))
    vec = pl.BlockSpec((1, d), lambda i: (0, 0))
    return _pcall(
        body, name=name, grid=(n // tm,), in_specs=[row, vec, row, row], out_specs=[row, vec],
        out_shape=[jax.ShapeDtypeStruct((n, d), F32), jax.ShapeDtypeStruct((1, d), F32)],
        compiler_params=_params(("arbitrary",)),
    )(x, g, dh, dres)


def _gate_specs(n, d):
    bw = 256 if d % 256 == 0 else LANES
    tm = _tile(n, 512, 8)
    nb = d // bw
    yspec = pl.BlockSpec((tm, bw), lambda i, j: (i, j))
    gspecs = [pl.BlockSpec((tm, bw), lambda i, j, b=b: (i, P_GATE // bw + b * nb + j)) for b in range(3)]
    return tm, bw, nb, yspec, gspecs


def _gate_merge_fwd(proj, ys, *, name):
    n, d = ys[0].shape
    tm, bw, nb, yspec, gspecs = _gate_specs(n, d)

    def body(g0, g1, g2, y0, y1, y2, o_ref):
        acc = jax.nn.sigmoid(g0[...]) * y0[...]
        acc += jax.nn.sigmoid(g1[...]) * y1[...]
        acc += jax.nn.sigmoid(g2[...]) * y2[...]
        o_ref[...] = acc.astype(BF16)

    return _pcall(
        body, name=name, grid=(n // tm, nb), in_specs=gspecs + [yspec] * 3, out_specs=yspec,
        out_shape=jax.ShapeDtypeStruct((n, d), BF16), compiler_params=_params(("parallel", "parallel")),
    )(proj, proj, proj, *ys)


def _gate_merge_bwd(proj, ys, dmerged, *, name):
    n, d = ys[0].shape
    tm, bw, nb, yspec, gspecs = _gate_specs(n, d)

    def body(g0, g1, g2, y0, y1, y2, dm_ref, dy0, dy1, dy2, dgl0, dgl1, dgl2):
        dm = dm_ref[...]
        for g_ref, y_ref, dy_ref, dgl_ref in ((g0, y0, dy0, dgl0), (g1, y1, dy1, dgl1), (g2, y2, dy2, dgl2)):
            s = jax.nn.sigmoid(g_ref[...])
            dy_ref[...] = (dm * s).astype(BF16)
            dgl_ref[...] = (dm * y_ref[...] * s * (1.0 - s)).astype(BF16)

    sds = jax.ShapeDtypeStruct((n, d), BF16)
    return _pcall(
        body, name=name, grid=(n // tm, nb), in_specs=gspecs + [yspec] * 4, out_specs=[yspec] * 6,
        out_shape=[sds] * 6, compiler_params=_params(("parallel", "parallel")),
    )(proj, proj, proj, *ys, dmerged)


def _relu2_fwd(u, *, name):
    n, f = u.shape
    tm, tf = _tile(n, 512, 8), _tile(f, 1024)

    def body(u_ref, a_ref):
        r = jnp.maximum(u_ref[...], 0.0)
        a_ref[...] = (r * r).astype(BF16)

    spec = pl.BlockSpec((tm, tf), lambda i, j: (i, j))
    return _pcall(
        body, name=name, grid=(n // tm, f // tf), in_specs=[spec], out_specs=spec,
        out_shape=jax.ShapeDtypeStruct((n, f), BF16), compiler_params=_params(("parallel", "parallel")),
    )(u)


def _relu2_bwd(da, u, *, name):
    n, f = u.shape
    tm, tf = _tile(n, 512, 8), _tile(f, 1024)

    def body(da_ref, u_ref, du_ref):
        du_ref[...] = (da_ref[...] * (2.0 * jnp.maximum(u_ref[...], 0.0))).astype(BF16)

    spec = pl.BlockSpec((tm, tf), lambda i, j: (i, j))
    return _pcall(
        body, name=name, grid=(n // tm, f // tf), in_specs=[spec, spec], out_specs=spec,
        out_shape=jax.ShapeDtypeStruct((n, f), BF16), compiler_params=_params(("parallel", "parallel")),
    )(da, u)


def _loss_head(y, tgt, *, name):
    n, d = y.shape
    tm = _tile(n, 256, 8)
    steps = n // tm

    def body(y_ref, t_ref, dy_ref, loss_ref, acc):
        i = pl.program_id(0)

        @pl.when(i == 0)
        def _():
            acc[...] = jnp.zeros_like(acc)

        e = y_ref[...] - t_ref[...]
        dy_ref[...] = e * (1.0 / d)
        acc[...] += jnp.sum(e * e, axis=0, keepdims=True)

        @pl.when(i == steps - 1)
        def _():
            tot = jnp.sum(acc[...], axis=1, keepdims=True) * (0.5 / d)
            loss_ref[...] = jnp.broadcast_to(tot, loss_ref.shape)

    row = pl.BlockSpec((tm, d), lambda i: (i, 0))
    return _pcall(
        body, name=name, grid=(steps,), in_specs=[row, row], out_specs=[row, pl.BlockSpec((8, LANES), lambda i: (0, 0))],
        out_shape=[jax.ShapeDtypeStruct((n, d), F32), jax.ShapeDtypeStruct((8, LANES), F32)],
        scratch_shapes=[pltpu.VMEM((1, d), F32)], compiler_params=_params(("arbitrary",)),
    )(y, tgt)


def _dil_combine(os_, lses, *, name):
    n, w = os_[0].shape
    tm = _tile(n, 512, 8)

    def body(o0, o1, o2, l0, l1, l2, out_ref, lse_ref):
        a, b, c = l0[...], l1[...], l2[...]
        m = jnp.maximum(jnp.maximum(a, b), c)
        ea, eb, ec = jnp.exp(a - m), jnp.exp(b - m), jnp.exp(c - m)
        den = ea + eb + ec
        out_ref[...] = (ea * o0[...] + eb * o1[...] + ec * o2[...]) / den
        lse_ref[...] = m + jnp.log(den)

    spec = pl.BlockSpec((tm, w), lambda i: (i, 0))
    sds = jax.ShapeDtypeStruct((n, w), F32)
    return _pcall(
        body, name=name, grid=(n // tm,), in_specs=[spec] * 6, out_specs=[spec, spec], out_shape=[sds, sds],
        compiler_params=_params(("parallel",)),
    )(*os_, *lses)


def _adamw(gsrc, w, m, v, *, name):
    s, dep, a, b = gsrc.shape
    ta = _tile(a, max(16, (ADAMW_BLOCK_ELEMS // b) // 16 * 16), 16)
    c1 = 1.0 / (1.0 - ADAM_B1 ** ADAM_STEP)
    c2 = 1.0 / (1.0 - ADAM_B2 ** ADAM_STEP)

    def body(gs_ref, w_ref, m_ref, v_ref, g_ref, d_ref, m2_ref, v2_ref):
        g = gs_ref[0].astype(F32)
        for i in range(1, s):
            g = g + gs_ref[i].astype(F32)
        m2 = ADAM_B1 * m_ref[...] + (1.0 - ADAM_B1) * g
        v2 = ADAM_B2 * v_ref[...] + (1.0 - ADAM_B2) * (g * g)
        g_ref[...] = g
        m2_ref[...] = m2
        v2_ref[...] = v2
        d_ref[...] = -ADAM_LR * ((m2 * c1) / (jnp.sqrt(v2 * c2) + ADAM_EPS) + ADAM_WD * w_ref[...])

    spec = pl.BlockSpec((None, ta, b), lambda l, i: (l, i, 0))
    sds = jax.ShapeDtypeStruct((dep, a, b), F32)
    return _pcall(
        body, name=name, grid=(dep, a // ta),
        in_specs=[pl.BlockSpec((s, None, ta, b), lambda l, i: (0, l, i, 0)), spec, spec, spec],
        out_specs=[spec] * 4, out_shape=[sds] * 4, compiler_params=_params(("parallel", "parallel")),
    )(gsrc, w, m, v)


def _mask_a():
    return lax.broadcasted_iota(jnp.int32, (1, LANES), 1) < HEAD_DIM


def _half_sum(x, m_a):
    sa = jnp.sum(jnp.where(m_a, x, 0.0), axis=1, keepdims=True)
    sb = jnp.sum(jnp.where(m_a, 0.0, x), axis=1, keepdims=True)
    return jnp.where(m_a, sa, sb)


def _head_inv(x, m_a):
    return lax.rsqrt(_half_sum(x * x, m_a) * (1.0 / HEAD_DIM) + EPS)


def _swap32(x):
    first = (lax.broadcasted_iota(jnp.int32, (1, LANES), 1) % HEAD_DIM) < (HEAD_DIM // 2)
    return jnp.where(first, pltpu.roll(x, LANES - HEAD_DIM // 2, 1), pltpu.roll(x, HEAD_DIM // 2, 1))


def _head_col(t, h):
    if h == 1:
        t = pltpu.roll(t, HEAD_DIM, 1)
    return t[:, 0:1]


def _tri(blk, rel):
    r = lax.broadcasted_iota(jnp.int32, (blk, blk), 0)
    c = lax.broadcasted_iota(jnp.int32, (blk, blk), 1)
    return jnp.where(rel(r, c), 1.0, 0.0).astype(BF16)


def _cumdot(x, u, parts):
    acc = None
    r = x
    for i in range(parts):
        xi = r.astype(BF16)
        t = jnp.dot(xi, u, preferred_element_type=F32)
        acc = t if acc is None else acc + t
        if i + 1 < parts:
            r = r - xi.astype(F32)
    return acc


def _rows(i, blk):
    return pl.ds(pl.multiple_of(i * blk, blk), blk)


def _col_spec(n, off):
    return pl.BlockSpec((None, n, LANES), lambda z, p, off=off: (z, 0, off + p))


def _att_blk(n):
    return ATT_BLK if n % ATT_BLK == 0 else min(LANES, n)


def _loop(lo, hi, fn):
    def it(i, c):
        fn(i)
        return c

    lax.fori_loop(lo, hi, it, 0)


def _prep_qk(q_ref, k_ref, v_ref, gq_ref, gk_ref, rope_refs, qn_s, kn_s, vb_s, nb, blk, m_a):
    def prep(c):
        rows = _rows(c, blk)
        for src, g_ref, dst, scale in ((q_ref, gq_ref, qn_s, SCALE), (k_ref, gk_ref, kn_s, None)):
            xv = src[rows, :]
            xn = xv * _head_inv(xv, m_a) * g_ref[...]
            if rope_refs is not None:
                xn = xn * rope_refs[0][rows, :] + _swap32(xn) * rope_refs[1][rows, :]
            if scale is not None:
                xn = xn * scale
            dst[rows, :] = xn.astype(BF16)
        vb_s[rows, :] = v_ref[rows, :].astype(BF16)

    _loop(0, nb, prep)


def _attn_fwd(src, offs, npairs, gq, gk, *, rope=None, kbias=None, window, name):
    zs, n, _ = src.shape
    blk = _att_blk(n)
    nb = n // blk
    full = window >= n
    wblk = min(-(-window // blk), nb)

    def body(*refs):
        it = iter(refs)
        q_ref, k_ref, v_ref, gq_ref, gk_ref = (next(it) for _ in range(5))
        rope_refs = (next(it), next(it)) if rope is not None else None
        kb_ref = next(it) if kbias is not None else None
        o_ref, lse_ref, qn_s, kn_s, vb_s, acc_s, m_s, l_s = (next(it) for _ in range(8))
        m_a = _mask_a()
        m_b = jnp.logical_not(m_a)
        _prep_qk(q_ref, k_ref, v_ref, gq_ref, gk_ref, rope_refs, qn_s, kn_s, vb_s, nb, blk, m_a)
        dif = lax.broadcasted_iota(jnp.int32, (blk, blk), 0) - lax.broadcasted_iota(jnp.int32, (blk, blk), 1)

        def qblock(qi):
            rows = _rows(qi, blk)
            qb = qn_s[rows, :]
            qms = (jnp.where(m_a, qb, 0), jnp.where(m_b, qb, 0))
            m_s[...] = jnp.full(m_s.shape, NEG, F32)
            l_s[...] = jnp.zeros_like(l_s)
            acc_s[...] = jnp.zeros_like(acc_s)

            def step(kj, masked):
                cols = _rows(kj, blk)
                kn, vb = kn_s[cols, :], vb_s[cols, :]
                if masked:
                    d = dif + (qi - kj) * blk
                    ok = (d >= 0) & (d <= window)
                for h in (0, 1):
                    s = lax.dot_general(qms[h], kn, NT, preferred_element_type=F32)
                    if kb_ref is not None:
                        s = s + kb_ref[h, kj]
                    if masked:
                        s = jnp.where(ok, s, NEG)
                    m = m_s[h]
                    m2 = jnp.maximum(m, jnp.max(s, axis=1, keepdims=True))
                    al = jnp.exp(m - m2)
                    p = jnp.exp(s - m2)
                    l_s[h] = al * l_s[h] + jnp.sum(p, axis=1, keepdims=True)
                    acc_s[h] = al * acc_s[h] + jnp.dot(p.astype(BF16), vb, preferred_element_type=F32)
                    m_s[h] = m2

            if full:
                _loop(0, qi, lambda kj: step(kj, False))
                step(qi, True)
            else:
                _loop(jnp.maximum(qi - wblk, 0), qi + 1, lambda kj: step(kj, True))
            o_ref[rows, :] = jnp.where(m_a, acc_s[0] / l_s[0], acc_s[1] / l_s[1])
            lse_ref[rows, :] = jnp.where(m_a, m_s[0] + jnp.log(l_s[0]), m_s[1] + jnp.log(l_s[1]))

        _loop(0, nb, qblock)

    vec = pl.BlockSpec((1, LANES), lambda z, p: (0, 0))
    ins = [src, src, src, gq, gk]
    specs = [_col_spec(n, offs[0]), _col_spec(n, offs[1]), _col_spec(n, offs[2]), vec, vec]
    if rope is not None:
        ins += list(rope)
        specs += [pl.BlockSpec((None, n, LANES), lambda z, p: (z, 0, 0))] * 2
    if kbias is not None:
        ins.append(kbias)
        specs.append(pl.BlockSpec((None, 2, nb, 1, blk), lambda z, p: (z, p, 0, 0, 0)))
    ospec = _col_spec(n, 0)
    sds = jax.ShapeDtypeStruct((zs, n, LANES * npairs), F32)
    scratch = [pltpu.VMEM((n, LANES), BF16)] * 3
    scratch += [pltpu.VMEM((2, blk, LANES), F32), pltpu.VMEM((2, blk, 1), F32), pltpu.VMEM((2, blk, 1), F32)]
    return _pcall(
        body, name=name, grid=(zs, npairs), in_specs=specs, out_specs=[ospec, ospec], out_shape=[sds, sds],
        scratch_shapes=scratch, compiler_params=_params(("parallel", "parallel")),
    )(*ins)


def _attn_bwd(src, offs, npairs, gq, gk, o, do, lse, *, rope=None, kbias=None, window, name):
    zs, n, _ = src.shape
    blk = _att_blk(n)
    nb = n // blk
    full = window >= n
    wblk = min(-(-window // blk), nb)

    def body(*refs):
        it = iter(refs)
        q_ref, k_ref, v_ref, gq_ref, gk_ref, o_ref, do_ref, lse_ref = (next(it) for _ in range(8))
        rope_refs = (next(it), next(it)) if rope is not None else None
        kb_ref = next(it) if kbias is not None else None
        dq_ref, dk_ref, dv_ref, dg_ref = (next(it) for _ in range(4))
        dkb_ref = next(it) if kbias is not None else None
        qn_s, kn_s, vb_s, dqn_s, dkn_s, dq_s, rs_s = (next(it) for _ in range(7))
        m_a = _mask_a()
        m_b = jnp.logical_not(m_a)

        @pl.when((pl.program_id(0) == 0) & (pl.program_id(1) == 0))
        def _():
            dg_ref[...] = jnp.zeros_like(dg_ref)

        _prep_qk(q_ref, k_ref, v_ref, gq_ref, gk_ref, rope_refs, qn_s, kn_s, vb_s, nb, blk, m_a)
        dkn_s[...] = jnp.zeros_like(dkn_s)
        dv_ref[...] = jnp.zeros_like(dv_ref)
        if dkb_ref is not None:
            dkb_ref[...] = jnp.zeros_like(dkb_ref)
        dif = lax.broadcasted_iota(jnp.int32, (blk, blk), 0) - lax.broadcasted_iota(jnp.int32, (blk, blk), 1)

        def qblock(qi):
            rows = _rows(qi, blk)
            qb = qn_s[rows, :]
            dob = do_ref[rows, :]
            lse_t = lse_ref[rows, :]
            delta_t = _half_sum(dob * o_ref[rows, :], m_a)
            qms = (jnp.where(m_a, qb, 0), jnp.where(m_b, qb, 0))
            doms = (jnp.where(m_a, dob, 0.0).astype(BF16), jnp.where(m_b, dob, 0.0).astype(BF16))
            lse_c = (_head_col(lse_t, 0), _head_col(lse_t, 1))
            delta_c = (_head_col(delta_t, 0), _head_col(delta_t, 1))
            dq_s[...] = jnp.zeros_like(dq_s)
            if dkb_ref is not None:
                rs_s[...] = jnp.zeros_like(rs_s)

            def step(kj, masked):
                cols = _rows(kj, blk)
                kn, vb = kn_s[cols, :], vb_s[cols, :]
                if masked:
                    d = dif + (qi - kj) * blk
                    ok = (d >= 0) & (d <= window)
                dk_acc = dv_acc = None
                for h in (0, 1):
                    s = lax.dot_general(qms[h], kn, NT, preferred_element_type=F32)
                    if kb_ref is not None:
                        s = s + kb_ref[h, kj]
                    if masked:
                        s = jnp.where(ok, s, NEG)
                    p = jnp.exp(s - lse_c[h])
                    dp = lax.dot_general(doms[h], vb, NT, preferred_element_type=F32)
                    ds = p * (dp - delta_c[h])
                    dsb = ds.astype(BF16)
                    tk = lax.dot_general(dsb, qms[h], TN, preferred_element_type=F32)
                    tv = lax.dot_general(p.astype(BF16), doms[h], TN, preferred_element_type=F32)
                    dk_acc = tk if dk_acc is None else dk_acc + tk
                    dv_acc = tv if dv_acc is None else dv_acc + tv
                    dq_s[h] += jnp.dot(dsb, kn, preferred_element_type=F32)
                    if dkb_ref is not None:
                        dkb_ref[h, kj] += jnp.sum(ds, axis=0, keepdims=True)
                        rs_s[h] += jnp.sum(ds, axis=1, keepdims=True)
                dkn_s[cols, :] += dk_acc
                dv_ref[cols, :] += dv_acc

            if full:
                _loop(0, qi, lambda kj: step(kj, False))
                step(qi, True)
            else:
                _loop(jnp.maximum(qi - wblk, 0), qi + 1, lambda kj: step(kj, True))
            dqn_s[rows, :] = jnp.where(m_a, dq_s[0], dq_s[1]) * SCALE
            if dkb_ref is not None:
                for h in (0, 1):
                    dkb_ref[h, qi] -= jnp.transpose(jnp.broadcast_to(rs_s[h], (blk, blk)))[0:1, :]

        _loop(0, nb, qblock)

        def finish(c, carry):
            rows = _rows(c, blk)
            out = []
            for src_ref, g_ref, d_s, dst in ((q_ref, gq_ref, dqn_s, dq_ref), (k_ref, gk_ref, dkn_s, dk_ref)):
                xv = src_ref[rows, :]
                inv = _head_inv(xv, m_a)
                y = xv * inv
                dxn = d_s[rows, :]
                if rope_refs is not None:
                    dxn = dxn * rope_refs[0][rows, :] + _swap32(dxn * rope_refs[1][rows, :])
                dy = dxn * g_ref[...]
                dst[rows, :] = inv * (dy - y * (_half_sum(dy * y, m_a) * (1.0 / HEAD_DIM)))
                out.append(jnp.sum(dxn * y, axis=0, keepdims=True))
            return carry[0] + out[0], carry[1] + out[1]

        zero = jnp.zeros((1, LANES), F32)
        dgq, dgk = lax.fori_loop(0, nb, finish, (zero, zero))
        dg_ref[0:1, :] += dgq
        dg_ref[1:2, :] += dgk

    vec = pl.BlockSpec((1, LANES), lambda z, p: (0, 0))
    ospec = _col_spec(n, 0)
    ins = [src, src, src, gq, gk, o, do, lse]
    specs = [_col_spec(n, offs[0]), _col_spec(n, offs[1]), _col_spec(n, offs[2]), vec, vec, ospec, ospec, ospec]
    if rope is not None:
        ins += list(rope)
        specs += [pl.BlockSpec((None, n, LANES), lambda z, p: (z, 0, 0))] * 2
    sds = jax.ShapeDtypeStruct((zs, n, LANES * npairs), F32)
    out_shape = [sds, sds, sds, jax.ShapeDtypeStruct((8, LANES), F32)]
    out_specs = [ospec, ospec, ospec, pl.BlockSpec((8, LANES), lambda z, p: (0, 0))]
    if kbias is not None:
        ins.append(kbias)
        kbspec = pl.BlockSpec((None, 2, nb, 1, blk), lambda z, p: (z, p, 0, 0, 0))
        specs.append(kbspec)
        out_shape.append(jax.ShapeDtypeStruct(kbias.shape, F32))
        out_specs.append(kbspec)
    scratch = [pltpu.VMEM((n, LANES), BF16)] * 3 + [pltpu.VMEM((n, LANES), F32)] * 2
    scratch += [pltpu.VMEM((2, blk, LANES), F32), pltpu.VMEM((2, blk, 1), F32)]
    return _pcall(
        body, name=name, grid=(zs, npairs), in_specs=specs, out_specs=out_specs, out_shape=out_shape,
        scratch_shapes=scratch, compiler_params=_params(("arbitrary", "arbitrary")),
    )(*ins)


def _log_sig_pair(z):
    lsn = -(jnp.maximum(z, 0.0) + jnp.log(1.0 + jnp.exp(-jnp.abs(z))))
    return lsn, z + lsn


def _prep_sb(q_ref, k_ref, v_ref, qs_s, kb_s, vb_s, nb, blk):
    def prep(c):
        rows = _rows(c, blk)
        qs_s[rows, :] = (q_ref[rows, :] * SCALE).astype(BF16)
        kb_s[rows, :] = k_ref[rows, :].astype(BF16)
        vb_s[rows, :] = v_ref[rows, :].astype(BF16)

    _loop(0, nb, prep)


def _sb_fwd(src, offs, npairs, *, name):
    zs, n, _ = src.shape
    blk = _att_blk(n)
    nb = n // blk

    def body(q_ref, k_ref, v_ref, o_ref, lt_ref, qs_s, kb_s, vb_s, acc_s, c_s):
        m_a = _mask_a()
        m_b = jnp.logical_not(m_a)
        _prep_sb(q_ref, k_ref, v_ref, qs_s, kb_s, vb_s, nb, blk)
        diag_ok = lax.broadcasted_iota(jnp.int32, (blk, blk), 0) > lax.broadcasted_iota(jnp.int32, (blk, blk), 1)
        u_gt = _tri(blk, lambda r, c: r > c)

        def qblock(qi):
            rows = _rows(qi, blk)
            qb = qs_s[rows, :]
            qms = (jnp.where(m_a, qb, 0), jnp.where(m_b, qb, 0))
            acc_s[...] = jnp.zeros_like(acc_s)
            c_s[...] = jnp.zeros_like(c_s)

            def step(kj, masked):
                cols = _rows(kj, blk)
                kb, vb = kb_s[cols, :], vb_s[cols, :]
                for h in (0, 1):
                    z = lax.dot_general(qms[h], kb, NT, preferred_element_type=F32)
                    lsn, lsp = _log_sig_pair(z)
                    if masked:
                        lsn = jnp.where(diag_ok, lsn, 0.0)
                    a = jnp.exp(lsp + (c_s[h] + _cumdot(lsn, u_gt, 3)))
                    if masked:
                        a = jnp.where(diag_ok, a, 0.0)
                    acc_s[h] += jnp.dot(a.astype(BF16), vb, preferred_element_type=F32)
                    c_s[h] += jnp.sum(lsn, axis=1, keepdims=True)

            step(qi, True)
            _loop(0, qi, lambda t: step(qi - 1 - t, False))
            o_ref[rows, :] = jnp.where(m_a, acc_s[0], acc_s[1])
            lt_ref[rows, :] = jnp.where(m_a, c_s[0], c_s[1])

        _loop(0, nb, qblock)

    ospec = _col_spec(n, 0)
    sds = jax.ShapeDtypeStruct((zs, n, LANES * npairs), F32)
    scratch = [pltpu.VMEM((n, LANES), BF16)] * 3 + [pltpu.VMEM((2, blk, LANES), F32), pltpu.VMEM((2, blk, 1), F32)]
    return _pcall(
        body, name=name, grid=(zs, npairs), in_specs=[_col_spec(n, offs[0]), _col_spec(n, offs[1]), _col_spec(n, offs[2])],
        out_specs=[ospec, ospec], out_shape=[sds, sds], scratch_shapes=scratch,
        compiler_params=_params(("parallel", "parallel")),
    )(src, src, src)


def _sb_bwd(src, offs, npairs, do, ltot, *, name):
    zs, n, _ = src.shape
    blk = _att_blk(n)
    nb = n // blk

    def body(q_ref, k_ref, v_ref, do_ref, lt_ref, dq_ref, dk_ref, dv_ref, qs_s, kb_s, vb_s, dq_s, lp_s, ep_s):
        m_a = _mask_a()
        m_b = jnp.logical_not(m_a)
        _prep_sb(q_ref, k_ref, v_ref, qs_s, kb_s, vb_s, nb, blk)
        dk_ref[...] = jnp.zeros_like(dk_ref)
        dv_ref[...] = jnp.zeros_like(dv_ref)
        diag_ok = lax.broadcasted_iota(jnp.int32, (blk, blk), 0) > lax.broadcasted_iota(jnp.int32, (blk, blk), 1)
        u_le = _tri(blk, lambda r, c: r <= c)
        u_lt = _tri(blk, lambda r, c: r < c)

        def qblock(qi):
            rows = _rows(qi, blk)
            qb = qs_s[rows, :]
            dob = do_ref[rows, :]
            lt_t = lt_ref[rows, :]
            qms = (jnp.where(m_a, qb, 0), jnp.where(m_b, qb, 0))
            doms = (jnp.where(m_a, dob, 0.0).astype(BF16), jnp.where(m_b, dob, 0.0).astype(BF16))
            lt_c = (_head_col(lt_t, 0), _head_col(lt_t, 1))
            dq_s[...] = jnp.zeros_like(dq_s)
            lp_s[...] = jnp.zeros_like(lp_s)
            ep_s[...] = jnp.zeros_like(ep_s)

            def step(kj, masked):
                cols = _rows(kj, blk)
                kb, vb = kb_s[cols, :], vb_s[cols, :]
                dk_acc = dv_acc = None
                for h in (0, 1):
                    z = lax.dot_general(qms[h], kb, NT, preferred_element_type=F32)
                    lsn, lsp = _log_sig_pair(z)
                    if masked:
                        lsn = jnp.where(diag_ok, lsn, 0.0)
                    a = jnp.exp(lsp + (lt_c[h] - (lp_s[h] + _cumdot(lsn, u_le, 3))))
                    if masked:
                        a = jnp.where(diag_ok, a, 0.0)
                    da = lax.dot_general(doms[h], vb, NT, preferred_element_type=F32)
                    e = da * a
                    big_e = ep_s[h] + _cumdot(e, u_lt, 2)
                    sig = jax.nn.sigmoid(z)
                    dz = e * (1.0 - sig) - big_e * sig
                    if masked:
                        dz = jnp.where(diag_ok, dz, 0.0)
                    dzb = dz.astype(BF16)
                    tk = lax.dot_general(dzb, qms[h], TN, preferred_element_type=F32)
                    tv = lax.dot_general(a.astype(BF16), doms[h], TN, preferred_element_type=F32)
                    dk_acc = tk if dk_acc is None else dk_acc + tk
                    dv_acc = tv if dv_acc is None else dv_acc + tv
                    dq_s[h] += jnp.dot(dzb, kb, preferred_element_type=F32)
                    lp_s[h] += jnp.sum(lsn, axis=1, keepdims=True)
                    ep_s[h] += jnp.sum(e, axis=1, keepdims=True)
                dk_ref[cols, :] += dk_acc
                dv_ref[cols, :] += dv_acc

            _loop(0, qi, lambda kj: step(kj, False))
            step(qi, True)
            dq_ref[rows, :] = jnp.where(m_a, dq_s[0], dq_s[1]) * SCALE

        _loop(0, nb, qblock)

    ospec = _col_spec(n, 0)
    sds = jax.ShapeDtypeStruct((zs, n, LANES * npairs), F32)
    scratch = [pltpu.VMEM((n, LANES), BF16)] * 3
    scratch += [pltpu.VMEM((2, blk, LANES), F32), pltpu.VMEM((2, blk, 1), F32), pltpu.VMEM((2, blk, 1), F32)]
    return _pcall(
        body, name=name, grid=(zs, npairs),
        in_specs=[_col_spec(n, offs[0]), _col_spec(n, offs[1]), _col_spec(n, offs[2]), ospec, ospec],
        out_specs=[ospec] * 3, out_shape=[sds] * 3, scratch_shapes=scratch,
        compiler_params=_params(("parallel", "parallel")),
    )(src, src, src, do, ltot)


def _fox_gate_fwd(lg, bias, *, name):
    bs, nh, t = lg.shape
    blk = min(LANES, t)

    def body(lg_ref, b_ref, kb_ref):
        u_le = _tri(blk, lambda r, c: r <= c)
        carry = jnp.zeros((nh, 1), F32)
        for j in range(t // blk):
            sl = slice(j * blk, (j + 1) * blk)
            xv = lg_ref[:, sl] + b_ref[...]
            lf = jnp.minimum(xv, 0.0) - jnp.log(1.0 + jnp.exp(-jnp.abs(xv)))
            kb_ref[:, sl] = -(carry + _cumdot(lf, u_le, 3))
            carry = carry + jnp.sum(lf, axis=1, keepdims=True)

    spec = pl.BlockSpec((None, nh, t), lambda i: (i, 0, 0))
    return _pcall(
        body, name=name, grid=(bs,), in_specs=[spec, pl.BlockSpec((nh, 1), lambda i: (0, 0))], out_specs=spec,
        out_shape=jax.ShapeDtypeStruct((bs, nh, t), F32), compiler_params=_params(("parallel",)),
    )(lg, bias)


def _fox_gate_bwd(dkb, lg, bias, *, name):
    bs, nh, t = lg.shape
    blk = min(LANES, t)

    def body(dkb_ref, lg_ref, b_ref, dlg_ref, db_ref):
        @pl.when(pl.program_id(0) == 0)
        def _():
            db_ref[...] = jnp.zeros_like(db_ref)

        u_ge = _tri(blk, lambda r, c: r >= c)
        carry = jnp.zeros((nh, 1), F32)
        tot = jnp.zeros((nh, 1), F32)
        for j in reversed(range(t // blk)):
            sl = slice(j * blk, (j + 1) * blk)
            df = -dkb_ref[:, sl]
            dlf = carry + _cumdot(df, u_ge, 3)
            carry = carry + jnp.sum(df, axis=1, keepdims=True)
            xv = lg_ref[:, sl] + b_ref[...]
            dlg = dlf * jax.nn.sigmoid(-xv)
            dlg_ref[:, sl] = dlg
            tot = tot + jnp.sum(dlg, axis=1, keepdims=True)
        db_ref[...] += jnp.broadcast_to(tot, db_ref.shape)

    spec = pl.BlockSpec((None, nh, t), lambda i: (i, 0, 0))
    return _pcall(
        body, name=name, grid=(bs,), in_specs=[spec, spec, pl.BlockSpec((nh, 1), lambda i: (0, 0))],
        out_specs=[spec, pl.BlockSpec((nh, LANES), lambda i: (0, 0))],
        out_shape=[jax.ShapeDtypeStruct((bs, nh, t), F32), jax.ShapeDtypeStruct((nh, LANES), F32)],
        compiler_params=_params(("arbitrary",)),
    )(dkb, lg, bias)


def _place():
    return lax.axis_index("x"), lax.axis_index("y"), lax.axis_index("c")


def _flip(v, f):
    return 1 - v if f else v


FLIPS = [(fx, fy, fc) for fx in (0, 1) for fy in (0, 1) for fc in (0, 1)][1:]


def _gather_many(shards, *, name):
    nw = len(shards)

    def body(*refs):
        x_refs, out_refs = refs[:nw], refs[nw:2 * nw]
        send_sems, recv_sems, local_sems = refs[2 * nw:]
        x, y, cc = _place()
        me, sibling = (x, y, cc), (x, y, 1 - cc)
        chips = [(1 - x, y), (x, 1 - y), (1 - x, 1 - y)]

        def slot(i, px, py, pc):
            return out_refs[i].at[4 * px + 2 * py + pc]

        def copy(i, k, block, to, src=None):
            return pltpu.make_async_remote_copy(
                src_ref=slot(i, *block) if src is None else src, dst_ref=slot(i, *block),
                send_sem=send_sems.at[k, i], recv_sem=recv_sems.at[k, i], device_id=to, device_id_type=MESH)

        mine = [pltpu.make_async_copy(x_refs[i], slot(i, *me), local_sems.at[i]) for i in range(nw)]
        first = []
        for i in range(nw):
            mine[i].start()
            first.append(copy(i, 0, me, sibling, src=x_refs[i]))
            first += [copy(i, 1 + j, me, (*chip, cc), src=x_refs[i]) for j, chip in enumerate(chips)]
        for cp in first:
            cp.start()
        passed = []
        for i in range(nw):
            for j, chip in enumerate(chips):
                copy(i, 1 + j, (*chip, cc), me).wait_recv()
                passed.append(copy(i, 4 + j, (*chip, cc), sibling))
                passed[-1].start()
        for i in range(nw):
            copy(i, 0, sibling, me).wait_recv()
            for j, chip in enumerate(chips):
                copy(i, 4 + j, (*chip, 1 - cc), me).wait_recv()
        for cp in first + passed:
            cp.wait_send()
        for cp in mine:
            cp.wait()

    hbm = pl.BlockSpec(memory_space=pl.ANY)
    return _pcall(
        body, name=name, in_specs=[hbm] * nw, out_specs=[hbm] * nw,
        out_shape=[jax.ShapeDtypeStruct((N_DEV, *s.shape), s.dtype) for s in shards],
        scratch_shapes=[pltpu.SemaphoreType.DMA((7, nw)), pltpu.SemaphoreType.DMA((7, nw)), pltpu.SemaphoreType.DMA((nw,))],
    )(*shards)


def _scatter_many(slots, *, name):
    nw = len(slots)

    def body(*refs):
        g_refs, recv_refs = refs[:nw], refs[nw:2 * nw]
        send_sems, recv_sems, local_sems = refs[2 * nw:]
        x, y, cc = _place()
        my = 4 * x + 2 * y + cc
        mine, copies = [], []
        for i in range(nw):
            mine.append(pltpu.make_async_copy(g_refs[i].at[my], recv_refs[i].at[my], local_sems.at[i]))
            for k, (fx, fy, fc) in enumerate(FLIPS):
                px, py, pc = _flip(x, fx), _flip(y, fy), _flip(cc, fc)
                copies.append(pltpu.make_async_remote_copy(
                    src_ref=g_refs[i].at[4 * px + 2 * py + pc], dst_ref=recv_refs[i].at[my],
                    send_sem=send_sems.at[k, i], recv_sem=recv_sems.at[k, i], device_id=(px, py, pc), device_id_type=MESH))
        for cp in mine + copies:
            cp.start()
        for cp in copies:
            cp.wait_recv()
        for cp in copies:
            cp.wait_send()
        for cp in mine:
            cp.wait()

    hbm = pl.BlockSpec(memory_space=pl.ANY)
    return _pcall(
        body, name=name, in_specs=[hbm] * nw, out_specs=[hbm] * nw,
        out_shape=[jax.ShapeDtypeStruct(s.shape, s.dtype) for s in slots],
        scratch_shapes=[pltpu.SemaphoreType.DMA((7, nw)), pltpu.SemaphoreType.DMA((7, nw)), pltpu.SemaphoreType.DMA((nw,))],
    )(*slots)


def _allreduce_small(blob, *, name):
    r, c = blob.shape

    def body(x_ref, out_ref, buf, send_sems, recv_sems):
        x, y, cc = _place()
        my = 4 * x + 2 * y + cc
        copies = []
        for k, (fx, fy, fc) in enumerate(FLIPS):
            peer = (_flip(x, fx), _flip(y, fy), _flip(cc, fc))
            copies.append(pltpu.make_async_remote_copy(
                src_ref=x_ref, dst_ref=buf.at[my], send_sem=send_sems.at[k], recv_sem=recv_sems.at[k],
                device_id=peer, device_id_type=MESH))
        for cp in copies:
            cp.start()
        buf[my] = x_ref[...]
        for cp in copies:
            cp.wait_recv()
        for cp in copies:
            cp.wait_send()
        acc = buf[0]
        for i in range(1, N_DEV):
            acc = acc + buf[i]
        out_ref[...] = acc

    vmem = pl.BlockSpec(memory_space=pltpu.VMEM)
    return _pcall(
        body, name=name, in_specs=[vmem], out_specs=vmem, out_shape=jax.ShapeDtypeStruct((r, c), F32),
        scratch_shapes=[pltpu.VMEM((N_DEV, r, c), F32), pltpu.SemaphoreType.DMA((7,)), pltpu.SemaphoreType.DMA((7,))],
    )(blob)


BIG = ("w_in", "w_mlp_in", "w_mlp_out", "w_up_fox", "w_up_sb", "w_up_dil", "w_out")
ROW_SHARDED = ("w_out", "w_mlp_out")
SMALL = ("attn_norm", "b_forget", "q_norm_fox", "k_norm_fox", "q_norm_dil", "k_norm_dil", "mlp_norm")
BLOB_ROWS = 512


def _pack(parts, dtype):
    flat = jnp.concatenate([p.reshape(-1).astype(dtype) for p in parts])
    size = -(-flat.shape[0] // (BLOB_ROWS * LANES)) * (BLOB_ROWS * LANES)
    return jnp.pad(flat, (0, size - flat.shape[0])).reshape(-1, LANES)


def _unpack(blob, shapes):
    flat = blob.reshape(-1)
    out, off = [], 0
    for shp in shapes:
        size = 1
        for s in shp:
            size *= s
        out.append(flat[off:off + size].reshape(shp))
        off += size
    return out


def _join_shards(name, sh):
    if name in ROW_SHARDED:
        return jnp.transpose(sh, (1, 0, 2, 3)).reshape(sh.shape[1], -1, sh.shape[3])
    return jnp.transpose(sh, (1, 2, 0, 3)).reshape(sh.shape[1], sh.shape[2], -1)


def _split_shards(name, full):
    dep, a, b = full.shape
    if name in ROW_SHARDED:
        return jnp.transpose(full.reshape(dep, N_DEV, a // N_DEV, b), (1, 0, 2, 3))
    return jnp.transpose(full.reshape(dep, a, N_DEV, b // N_DEV), (2, 0, 1, 3))


def _permute_in(w, dp):
    o1 = 3 * W_FOX
    o2 = o1 + N_HEADS_FOX
    pad = [(0, 0)] * (w.ndim - 1) + [(0, dp - w.shape[-1])]
    return jnp.pad(jnp.concatenate([w[..., :o1], w[..., o2:], w[..., o1:o2]], axis=-1), pad)


def _unpermute_in(wp, d_in):
    o1 = 3 * W_FOX
    fg = d_in - N_HEADS_FOX
    return jnp.concatenate([wp[..., :o1], wp[..., fg:d_in], wp[..., o1:fg]], axis=-1)


def _to_streams(a, r):
    b, t, c = a.shape
    if r == 1:
        return a
    return jnp.transpose(a.reshape(b, t // r, r, c), (0, 2, 1, 3)).reshape(b * r, t // r, c)


def _from_streams(a, r, b):
    if r == 1:
        return a
    z, n, c = a.shape
    return jnp.transpose(a.reshape(b, r, n, c), (0, 2, 1, 3)).reshape(b, n * r, c)


def _rope_tables(positions):
    half = HEAD_DIM // 2
    inv = 1.0 / (ROPE_THETA ** (jnp.arange(half, dtype=F32) / half))
    ang = positions.astype(F32)[..., None] * inv
    cos, sin = jnp.cos(ang), jnp.sin(ang)
    return jnp.tile(cos, (1, 1, 4)), jnp.tile(jnp.concatenate([-sin, sin], axis=-1), (1, 1, 2))


def _gain2(g):
    return jnp.tile(g.reshape(1, HEAD_DIM), (1, 2))


def _layer_fwd(l, x, w, small, ropes, bl, t):
    n, d = x.shape
    s = {}
    s["x"] = x
    s["h"] = _rmsnorm_fwd(x, small["attn_norm"][l].reshape(1, d), name=f"norm_attn_fwd{l}")
    proj = _mm(s["h"], w["w_in"][l], name=f"mm_proj{l}")
    s["proj"] = proj
    dp = proj.shape[1]
    proj3 = proj.reshape(bl, t, dp)
    p_fg = P_GATE + 3 * d

    lg = jnp.transpose(proj3[:, :, p_fg:p_fg + N_HEADS_FOX], (0, 2, 1))
    s["lg"] = lg
    kb = _fox_gate_fwd(lg, small["b_forget"][l].reshape(N_HEADS_FOX, 1), name=f"fox_gate_fwd{l}")
    blk = _att_blk(t)
    kb5 = kb.reshape(bl, N_HEADS_FOX, t // blk, 1, blk)
    s["kb5"] = kb5
    gqf, gkf = _gain2(small["q_norm_fox"][l]), _gain2(small["k_norm_fox"][l])
    fo = P_FOX // LANES
    fox_offs = (fo, fo + W_FOX // LANES, fo + 2 * W_FOX // LANES)
    out_a, lse_a = _attn_fwd(proj3, fox_offs, N_HEADS_FOX // 2, gqf, gkf, kbias=kb5, window=t, name=f"fox_fwd{l}")
    s["out_a"], s["lse_a"] = out_a, lse_a

    so = P_SB // LANES
    sb_offs = (so, so + W_SB // LANES, so + 2 * W_SB // LANES)
    out_b, lt_b = _sb_fwd(proj3, sb_offs, N_HEADS_SB // 2, name=f"sb_fwd{l}")
    s["out_b"], s["lt_b"] = out_b, lt_b

    gqd, gkd = _gain2(small["q_norm_dil"][l]), _gain2(small["k_norm_dil"][l])
    os_, lses, s["dil_src"] = [], [], []
    for g, (window, r) in enumerate(DIL_PATTERNS):
        c0 = P_DIL + g * W_DIL
        if r == 1:
            src, offs = proj3, (c0 // LANES, (c0 + W_DILQ) // LANES, (c0 + 2 * W_DILQ) // LANES)
        else:
            qkv = jnp.concatenate([proj3[:, :, c0 + i * W_DILQ:c0 + i * W_DILQ + W_DIL] for i in range(3)], axis=-1)
            src, offs = _to_streams(qkv, r), (0, W_DIL // LANES, 2 * W_DIL // LANES)
        s["dil_src"].append((src, offs))
        o_g, lse_g = _attn_fwd(src, offs, N_HEADS_DIL // 2, gqd, gkd, rope=ropes[g], window=window // r, name=f"dil_fwd{l}_{g}")
        os_.append(_from_streams(o_g, r, bl).reshape(n, W_DIL))
        lses.append(_from_streams(lse_g, r, bl).reshape(n, W_DIL))
    out_c, lse_c = _dil_combine(os_, lses, name=f"dil_combine{l}")
    s["out_c"], s["lse_c"] = out_c, lse_c

    ys = [_mm(out_a.reshape(n, W_FOX), w["w_up_fox"][l], name=f"mm_up_fox{l}"),
          _mm(out_b.reshape(n, W_SB), w["w_up_sb"][l], name=f"mm_up_sb{l}"),
          _mm(out_c, w["w_up_dil"][l], name=f"mm_up_dil{l}")]
    s["ys"] = ys
    s["merged"] = _gate_merge_fwd(proj, ys, name=f"gate_merge_fwd{l}")
    x1 = _mm(s["merged"], w["w_out"][l], add=x, name=f"mm_out{l}")
    s["x1"] = x1

    s["h2"] = _rmsnorm_fwd(x1, small["mlp_norm"][l].reshape(1, d), name=f"norm_mlp_fwd{l}")
    s["u"] = _mm(s["h2"], w["w_mlp_in"][l], name=f"mm_mlp_in{l}")
    s["a"] = _relu2_fwd(s["u"], name=f"relu2_fwd{l}")
    x2 = _mm(s["a"], w["w_mlp_out"][l], add=x1, name=f"mm_mlp_out{l}")
    return x2, s


def _layer_bwd(l, dx2, s, w, small, ropes, bl, t):
    n, d = dx2.shape
    gw, gs = {}, {}
    da = _mm(dx2, w["w_mlp_out"][l], tb=True, name=f"mm_da{l}")
    du = _relu2_bwd(da, s["u"], name=f"relu2_bwd{l}")
    gw["w_mlp_out"] = _mm(s["a"], dx2, ta=True, name=f"mm_dw_mlp_out{l}")
    gw["w_mlp_in"] = _mm(s["h2"], du, ta=True, name=f"mm_dw_mlp_in{l}")
    dh2 = _mm(du, w["w_mlp_in"][l], tb=True, name=f"mm_dh2{l}")
    dx1, gs["mlp_norm"] = _rmsnorm_bwd(s["x1"], small["mlp_norm"][l].reshape(1, d), dh2, dx2, name=f"norm_mlp_bwd{l}")

    dmerged = _mm(dx1, w["w_out"][l], tb=True, name=f"mm_dmerged{l}")
    gw["w_out"] = _mm(s["merged"], dx1, ta=True, name=f"mm_dw_out{l}")
    dya, dyb, dyc, dgl0, dgl1, dgl2 = _gate_merge_bwd(s["proj"], s["ys"], dmerged, name=f"gate_merge_bwd{l}")
    out_a2, out_b2 = s["out_a"].reshape(n, W_FOX), s["out_b"].reshape(n, W_SB)
    gw["w_up_fox"] = _mm(out_a2, dya, ta=True, name=f"mm_dw_up_fox{l}")
    gw["w_up_sb"] = _mm(out_b2, dyb, ta=True, name=f"mm_dw_up_sb{l}")
    gw["w_up_dil"] = _mm(s["out_c"], dyc, ta=True, name=f"mm_dw_up_dil{l}")
    dout_a = _mm(dya, w["w_up_fox"][l], tb=True, name=f"mm_dout_a{l}").reshape(bl, t, W_FOX)
    dout_b = _mm(dyb, w["w_up_sb"][l], tb=True, name=f"mm_dout_b{l}").reshape(bl, t, W_SB)
    dout_c = _mm(dyc, w["w_up_dil"][l], tb=True, name=f"mm_dout_c{l}").reshape(bl, t, W_DIL)

    proj3 = s["proj"].reshape(bl, t, -1)
    gqf, gkf = _gain2(small["q_norm_fox"][l]), _gain2(small["k_norm_fox"][l])
    fo = P_FOX // LANES
    fox_offs = (fo, fo + W_FOX // LANES, fo + 2 * W_FOX // LANES)
    dq_a, dk_a, dv_a, dg_a, dkb5 = _attn_bwd(proj3, fox_offs, N_HEADS_FOX // 2, gqf, gkf, s["out_a"], dout_a, s["lse_a"],
                                             kbias=s["kb5"], window=t, name=f"fox_bwd{l}")
    gs["fox_gains"] = dg_a
    dlg, gs["b_forget"] = _fox_gate_bwd(dkb5.reshape(bl, N_HEADS_FOX, t), s["lg"], small["b_forget"][l].reshape(N_HEADS_FOX, 1),
                                        name=f"fox_gate_bwd{l}")
    so = P_SB // LANES
    sb_offs = (so, so + W_SB // LANES, so + 2 * W_SB // LANES)
    dq_b, dk_b, dv_b = _sb_bwd(proj3, sb_offs, N_HEADS_SB // 2, dout_b, s["lt_b"], name=f"sb_bwd{l}")
    gqd, gkd = _gain2(small["q_norm_dil"][l]), _gain2(small["k_norm_dil"][l])
    out_c3, lse_c3 = s["out_c"].reshape(bl, t, W_DIL), s["lse_c"].reshape(bl, t, W_DIL)
    dqs, dks, dvs, dgd = [], [], [], None
    for g, (window, r) in enumerate(DIL_PATTERNS):
        src, offs = s["dil_src"][g]
        dq_g, dk_g, dv_g, dg_g = _attn_bwd(src, offs, N_HEADS_DIL // 2, gqd, gkd, _to_streams(out_c3, r), _to_streams(dout_c, r),
                                           _to_streams(lse_c3, r), rope=ropes[g], window=window // r, name=f"dil_bwd{l}_{g}")
        dqs.append(_from_streams(dq_g, r, bl))
        dks.append(_from_streams(dk_g, r, bl))
        dvs.append(_from_streams(dv_g, r, bl))
        dgd = dg_g if dgd is None else jnp.concatenate([dgd, dg_g], axis=0)
    gs["dil_gains"] = dgd

    dp = s["proj"].shape[1]
    parts = [dq_a, dk_a, dv_a, dq_b, dk_b, dv_b] + dqs + dks + dvs
    parts = [p.reshape(n, -1).astype(BF16) for p in parts] + [dgl0, dgl1, dgl2, jnp.transpose(dlg, (0, 2, 1)).reshape(n, -1).astype(BF16)]
    used = P_GATE + 3 * d + N_HEADS_FOX
    dproj = jnp.concatenate(parts + [jnp.zeros((n, dp - used), BF16)], axis=1)
    gw["w_in"] = _mm(s["h"], dproj, ta=True, name=f"mm_dw_in{l}")
    dh = _mm(dproj, w["w_in"][l], tb=True, name=f"mm_dh{l}")
    dx, gs["attn_norm"] = _rmsnorm_bwd(s["x"], small["attn_norm"][l].reshape(1, d), dh, dx1, name=f"norm_attn_bwd{l}")
    return dx, gw, gs


def kernel(x, positions, attn_norm, w_in, b_forget, q_norm_fox, k_norm_fox, q_norm_dil, k_norm_dil, w_up_fox, w_up_sb, w_up_dil, w_out, mlp_norm, w_mlp_in, w_mlp_out, loss_target, m_attn_norm, m_w_in, m_b_forget, m_q_norm_fox, m_k_norm_fox, m_q_norm_dil, m_k_norm_dil, m_w_up_fox, m_w_up_sb, m_w_up_dil, m_w_out, m_mlp_norm, m_w_mlp_in, m_w_mlp_out, v_attn_norm, v_w_in, v_b_forget, v_q_norm_fox, v_k_norm_fox, v_q_norm_dil, v_k_norm_dil, v_w_up_fox, v_w_up_sb, v_w_up_dil, v_w_out, v_mlp_norm, v_w_mlp_in, v_w_mlp_out):
    bl, t, d = x.shape
    n = bl * t
    depth = attn_norm.shape[0]
    wl = dict(w_in=w_in, w_up_fox=w_up_fox, w_up_sb=w_up_sb, w_up_dil=w_up_dil, w_out=w_out, w_mlp_in=w_mlp_in, w_mlp_out=w_mlp_out)
    ml = dict(w_in=m_w_in, w_up_fox=m_w_up_fox, w_up_sb=m_w_up_sb, w_up_dil=m_w_up_dil, w_out=m_w_out, w_mlp_in=m_w_mlp_in, w_mlp_out=m_w_mlp_out)
    vl = dict(w_in=v_w_in, w_up_fox=v_w_up_fox, w_up_sb=v_w_up_sb, w_up_dil=v_w_up_dil, w_out=v_w_out, w_mlp_in=v_w_mlp_in, w_mlp_out=v_w_mlp_out)
    small = dict(attn_norm=attn_norm, b_forget=b_forget, q_norm_fox=q_norm_fox, k_norm_fox=k_norm_fox, q_norm_dil=q_norm_dil,
                 k_norm_dil=k_norm_dil, mlp_norm=mlp_norm)
    m_small = dict(attn_norm=m_attn_norm, b_forget=m_b_forget, q_norm_fox=m_q_norm_fox, k_norm_fox=m_k_norm_fox,
                   q_norm_dil=m_q_norm_dil, k_norm_dil=m_k_norm_dil, mlp_norm=m_mlp_norm)
    v_small = dict(attn_norm=v_attn_norm, b_forget=v_b_forget, q_norm_fox=v_q_norm_fox, k_norm_fox=v_k_norm_fox,
                   q_norm_dil=v_q_norm_dil, k_norm_dil=v_k_norm_dil, mlp_norm=v_mlp_norm)

    gathered = _gather_many([wl[k].astype(BF16) for k in BIG], name="gather_weights")
    w = {k: _join_shards(k, sh) for k, sh in zip(BIG, gathered)}
    d_in = w["w_in"].shape[-1]
    dp = -(-(d_in) // 512) * 512
    w["w_in"] = _permute_in(w["w_in"], dp)

    cos, sin = _rope_tables(positions)
    ropes = [(_to_streams(cos, r), _to_streams(sin, r)) for _, r in DIL_PATTERNS]

    h = x.reshape(n, d)
    saved = []
    for l in range(depth):
        h, s = _layer_fwd(l, h, w, small, ropes, bl, t)
        saved.append(s)
    dy, loss_part = _loss_head(h, loss_target.reshape(n, d), name="loss_head")

    gws, gss = [None] * depth, [None] * depth
    for l in reversed(range(depth)):
        dy, gws[l], gss[l] = _layer_bwd(l, dy, saved[l], w, small, ropes, bl, t)
    grad_x = dy.reshape(bl, t, d)

    full = {k: jnp.stack([gws[l][k] for l in range(depth)]) for k in BIG}
    full["w_in"] = _unpermute_in(full["w_in"], d_in)
    recv = _scatter_many([_split_shards(k, full[k]).astype(BF16) for k in BIG], name="scatter_grads")
    g_big, d_big, m_big, v_big = {}, {}, {}, {}
    for k, r in zip(BIG, recv):
        g_big[k], d_big[k], m_big[k], v_big[k] = _adamw(r, wl[k], ml[k], vl[k], name=f"adamw_{k}")

    rows = [loss_part]
    for l in range(depth):
        gs = gss[l]
        rows += [gs["attn_norm"].reshape(-1, LANES), gs["mlp_norm"].reshape(-1, LANES), gs["fox_gains"], gs["dil_gains"], gs["b_forget"]]
    row_counts = [r.shape[0] for r in rows]
    part = jnp.concatenate(rows, axis=0)
    pad_rows = -(-part.shape[0] // 8) * 8 - part.shape[0]
    summed = _allreduce_small(jnp.pad(part, ((0, pad_rows), (0, 0))), name="allreduce_small")
    pieces, off = [], 0
    for c in row_counts:
        pieces.append(summed[off:off + c])
        off += c
    loss = pieces[0][0, 0]

    def fold(row):
        return row[:HEAD_DIM] + row[HEAD_DIM:]

    g_small = {k: [] for k in SMALL}
    for l in range(depth):
        an, mn, fg, dg, bf = pieces[1 + 5 * l:6 + 5 * l]
        g_small["attn_norm"].append(an.reshape(d))
        g_small["mlp_norm"].append(mn.reshape(d))
        g_small["q_norm_fox"].append(fold(fg[0]))
        g_small["k_norm_fox"].append(fold(fg[1]))
        g_small["q_norm_dil"].append(fold(dg[0]) + fold(dg[8]) + fold(dg[16]))
        g_small["k_norm_dil"].append(fold(dg[1]) + fold(dg[9]) + fold(dg[17]))
        g_small["b_forget"].append(bf[:, 0])
    g_small = {k: jnp.stack(vs) for k, vs in g_small.items()}
    small_shapes = [small[k].shape for k in SMALL]
    outs = _adamw(_pack([g_small[k] for k in SMALL], F32)[None, None], _pack([small[k] for k in SMALL], F32)[None],
                  _pack([m_small[k] for k in SMALL], F32)[None], _pack([v_small[k] for k in SMALL], F32)[None], name="adamw_small")
    g_sm, d_sm, m_sm, v_sm = (dict(zip(SMALL, _unpack(o, small_shapes))) for o in outs)

    order = ("attn_norm", "w_in", "b_forget", "q_norm_fox", "k_norm_fox", "q_norm_dil", "k_norm_dil", "w_up_fox", "w_up_sb",
             "w_up_dil", "w_out", "mlp_norm", "w_mlp_in", "w_mlp_out")
    res = [loss, grad_x]
    for big, sm in ((g_big, g_sm), (d_big, d_sm), (m_big, m_sm), (v_big, v_sm)):
        res += [big[k] if k in big else sm[k] for k in order]
    return tuple(res)
```

```python
import jax
import jax.numpy as jnp
from jax import lax
from jax.experimental import pallas as pl
from jax.experimental.pallas import tpu as pltpu

F32 = jnp.float32
BF16 = jnp.bfloat16

HEAD_DIM = 64
LANES = 128
N_HEADS_FOX = 8
N_HEADS_SB = 8
N_HEADS_DIL = 4
DIL_PATTERNS = ((128, 1), (512, 4), (2048, 16))
ROPE_THETA = 10000.0
EPS = 1e-6
SCALE = 0.125
W_FOX = N_HEADS_FOX * HEAD_DIM
W_SB = N_HEADS_SB * HEAD_DIM
W_DIL = N_HEADS_DIL * HEAD_DIM
W_DILQ = len(DIL_PATTERNS) * W_DIL
P_FOX = 0
P_SB = 3 * W_FOX
P_DIL = P_SB + 3 * W_SB
P_GATE = P_DIL + 3 * W_DILQ
N_DEV = 8
ATT_BLK = 256
NEG = -1e30
VMEM_LIMIT = 56 * 1024 * 1024
ADAMW_BLOCK_ELEMS = 128 * 1024

ADAM_LR = 0.001
ADAM_B1 = 0.9
ADAM_B2 = 0.999
ADAM_EPS = 1e-08
ADAM_WD = 0.01
ADAM_STEP = 10

NT = (((1,), (1,)), ((), ()))
TN = (((0,), (0,)), ((), ()))
MESH = pl.DeviceIdType.MESH


def _pcall(body, **kw):
    return pl.pallas_call(body, **kw)


def _params(sem=None):
    return pltpu.CompilerParams(dimension_semantics=sem, vmem_limit_bytes=VMEM_LIMIT)


class _Cargo:
    def __init__(self, ins, out_shape, sems, start, finish, on_done, aliases=None):
        self.ins, self.out_shape, self.sems = list(ins), list(out_shape), list(sems)
        self.start, self.finish, self.on_done, self.aliases = start, finish, on_done, dict(aliases or {})


def _call(body, *, cargo=None, name, grid=(), in_specs, out_specs, out_shape, scratch_shapes=(), compiler_params=None):
    if cargo is None:
        kw = dict(grid=grid) if grid else {}
        if compiler_params is not None:
            kw["compiler_params"] = compiler_params
        return _pcall(body, name=name, in_specs=in_specs, out_specs=out_specs, out_shape=out_shape,
                      scratch_shapes=list(scratch_shapes), **kw)
    single = not isinstance(out_shape, (list, tuple))
    o_specs, o_shape = ([out_specs], [out_shape]) if single else (list(out_specs), list(out_shape))
    n_in, n_out, n_scr = len(in_specs), len(o_shape), len(scratch_shapes)
    c_in, c_out = len(cargo.ins), len(cargo.out_shape)

    def wrapped(*refs):
        ins, cins = refs[:n_in], refs[n_in:n_in + c_in]
        o0 = n_in + c_in
        outs, couts = refs[o0:o0 + n_out], refs[o0 + n_out:o0 + n_out + c_out]
        s0 = o0 + n_out + c_out
        scr, sems = refs[s0:s0 + n_scr], refs[s0 + n_scr:]
        first = last = None
        for ax, size in enumerate(grid):
            pid = pl.program_id(ax)
            first = (pid == 0) if first is None else first & (pid == 0)
            last = (pid == size - 1) if last is None else last & (pid == size - 1)
        if first is None:
            cargo.start(cins, couts, sems)
            body(*ins, *outs, *scr)
            cargo.finish(cins, couts, sems)
            return

        @pl.when(first)
        def _():
            cargo.start(cins, couts, sems)

        body(*ins, *outs, *scr)

        @pl.when(last)
        def _():
            cargo.finish(cins, couts, sems)

    hbm = pl.BlockSpec(memory_space=pl.ANY)
    kw = dict(grid=grid, compiler_params=_params(("arbitrary",) * len(grid))) if grid else {}
    call = _pcall(
        wrapped, name=name, in_specs=list(in_specs) + [hbm] * c_in, out_specs=o_specs + [hbm] * c_out,
        out_shape=o_shape + cargo.out_shape, scratch_shapes=list(scratch_shapes) + cargo.sems,
        input_output_aliases={n_in + i: n_out + j for i, j in cargo.aliases.items()}, **kw)

    def run(*args):
        res = call(*args, *cargo.ins)
        cargo.on_done(list(res[n_out:]))
        return res[0] if single else list(res[:n_out])

    return run


def _tile(dim, target, mult=LANES):
    t = (min(dim, target) // mult) * mult
    while t >= mult:
        if dim % t == 0:
            return t
        t -= mult
    return dim


def _mm(a, b, *, ta=False, tb=False, add=None, out_dtype=F32, name, tm=1024, tn=512, tk=1024, cargo=None):
    m, k = (a.shape[1], a.shape[0]) if ta else a.shape
    n = b.shape[0] if tb else b.shape[1]
    tm, tn, tk = _tile(m, tm), _tile(n, tn), _tile(k, tk)
    nk = k // tk
    dn = (((0,) if ta else (1,), (1,) if tb else (0,)), ((), ()))

    def body(*refs):
        if add is None:
            a_ref, b_ref, o_ref, acc = refs
        else:
            a_ref, b_ref, add_ref, o_ref, acc = refs
        kk = pl.program_id(2)
        part = lax.dot_general(a_ref[...].astype(BF16), b_ref[...].astype(BF16), dn, preferred_element_type=F32)

        @pl.when(kk == 0)
        def _():
            acc[...] = part

        @pl.when(kk > 0)
        def _():
            acc[...] += part

        @pl.when(kk == nk - 1)
        def _():
            r = acc[...]
            if add is not None:
                r = r + add_ref[...]
            o_ref[...] = r.astype(out_dtype)

    a_spec = pl.BlockSpec((tk, tm), lambda i, j, q: (q, i)) if ta else pl.BlockSpec((tm, tk), lambda i, j, q: (i, q))
    b_spec = pl.BlockSpec((tn, tk), lambda i, j, q: (j, q)) if tb else pl.BlockSpec((tk, tn), lambda i, j, q: (q, j))
    o_spec = pl.BlockSpec((tm, tn), lambda i, j, q: (i, j))
    ins, specs = [a, b], [a_spec, b_spec]
    if add is not None:
        ins.append(add)
        specs.append(o_spec)
    return _call(
        body, cargo=cargo, name=name, grid=(m // tm, n // tn, nk), in_specs=specs, out_specs=o_spec,
        out_shape=jax.ShapeDtypeStruct((m, n), out_dtype), scratch_shapes=[pltpu.VMEM((tm, tn), F32)],
        compiler_params=_params(("parallel", "parallel", "arbitrary")),
    )(*ins)


def _rmsnorm_fwd(x, g, *, name):
    n, d = x.shape
    tm = _tile(n, 256, 8)

    def body(x_ref, g_ref, h_ref):
        xv = x_ref[...]
        inv = lax.rsqrt(jnp.mean(xv * xv, axis=1, keepdims=True) + EPS)
        h_ref[...] = (xv * inv * g_ref[...]).astype(BF16)

    row = pl.BlockSpec((tm, d), lambda i: (i, 0))
    return _pcall(
        body, name=name, grid=(n // tm,), in_specs=[row, pl.BlockSpec((1, d), lambda i: (0, 0))], out_specs=row,
        out_shape=jax.ShapeDtypeStruct((n, d), BF16), compiler_params=_params(("parallel",)),
    )(x, g)


def _rmsnorm_bwd(x, g, dh, dres, *, name):
    n, d = x.shape
    tm = _tile(n, 256, 8)

    def body(x_ref, g_ref, dh_ref, dres_ref, dx_ref, dg_ref):
        @pl.when(pl.program_id(0) == 0)
        def _():
            dg_ref[...] = jnp.zeros_like(dg_ref)

        xv = x_ref[...]
        inv = lax.rsqrt(jnp.mean(xv * xv, axis=1, keepdims=True) + EPS)
        y = xv * inv
        dhv = dh_ref[...]
        dg_ref[...] += jnp.sum(dhv * y, axis=0, keepdims=True)
        dy = dhv * g_ref[...]
        dx_ref[...] = dres_ref[...] + inv * (dy - y * jnp.mean(dy * y, axis=1, keepdims=True))

    row = pl.BlockSpec((tm, d), lambda i: (i, 0))
    vec = pl.BlockSpec((1, d), lambda i: (0, 0))
    return _pcall(
        body, name=name, grid=(n // tm,), in_specs=[row, vec, row, row], out_specs=[row, vec],
        out_shape=[jax.ShapeDtypeStruct((n, d), F32), jax.ShapeDtypeStruct((1, d), F32)],
        compiler_params=_params(("arbitrary",)),
    )(x, g, dh, dres)


def _gate_specs(n, d):
    bw = 256 if d % 256 == 0 else LANES
    tm = _tile(n, 512, 8)
    nb = d // bw
    yspec = pl.BlockSpec((tm, bw), lambda i, j: (i, j))
    gspecs = [pl.BlockSpec((tm, bw), lambda i, j, b=b: (i, P_GATE // bw + b * nb + j)) for b in range(3)]
    return tm, bw, nb, yspec, gspecs


def _gate_merge_fwd(proj, ys, *, name):
    n, d = ys[0].shape
    tm, bw, nb, yspec, gspecs = _gate_specs(n, d)

    def body(g0, g1, g2, y0, y1, y2, o_ref):
        acc = jax.nn.sigmoid(g0[...]) * y0[...]
        acc += jax.nn.sigmoid(g1[...]) * y1[...]
        acc += jax.nn.sigmoid(g2[...]) * y2[...]
        o_ref[...] = acc.astype(BF16)

    return _pcall(
        body, name=name, grid=(n // tm, nb), in_specs=gspecs + [yspec] * 3, out_specs=yspec,
        out_shape=jax.ShapeDtypeStruct((n, d), BF16), compiler_params=_params(("parallel", "parallel")),
    )(proj, proj, proj, *ys)


def _gate_merge_bwd(proj, ys, dmerged, *, name):
    n, d = ys[0].shape
    tm, bw, nb, yspec, gspecs = _gate_specs(n, d)

    def body(g0, g1, g2, y0, y1, y2, dm_ref, dy0, dy1, dy2, dgl0, dgl1, dgl2):
        dm = dm_ref[...]
        for g_ref, y_ref, dy_ref, dgl_ref in ((g0, y0, dy0, dgl0), (g1, y1, dy1, dgl1), (g2, y2, dy2, dgl2)):
            s = jax.nn.sigmoid(g_ref[...])
            dy_ref[...] = (dm * s).astype(BF16)
            dgl_ref[...] = (dm * y_ref[...] * s * (1.0 - s)).astype(BF16)

    sds = jax.ShapeDtypeStruct((n, d), BF16)
    return _pcall(
        body, name=name, grid=(n // tm, nb), in_specs=gspecs + [yspec] * 4, out_specs=[yspec] * 6,
        out_shape=[sds] * 6, compiler_params=_params(("parallel", "parallel")),
    )(proj, proj, proj, *ys, dmerged)


def _relu2_fwd(u, *, name):
    n, f = u.shape
    tm, tf = _tile(n, 512, 8), _tile(f, 1024)

    def body(u_ref, a_ref):
        r = jnp.maximum(u_ref[...], 0.0)
        a_ref[...] = (r * r).astype(BF16)

    spec = pl.BlockSpec((tm, tf), lambda i, j: (i, j))
    return _pcall(
        body, name=name, grid=(n // tm, f // tf), in_specs=[spec], out_specs=spec,
        out_shape=jax.ShapeDtypeStruct((n, f), BF16), compiler_params=_params(("parallel", "parallel")),
    )(u)


def _relu2_bwd(da, u, *, name):
    n, f = u.shape
    tm, tf = _tile(n, 512, 8), _tile(f, 1024)

    def body(da_ref, u_ref, du_ref):
        du_ref[...] = (da_ref[...] * (2.0 * jnp.maximum(u_ref[...], 0.0))).astype(BF16)

    spec = pl.BlockSpec((tm, tf), lambda i, j: (i, j))
    return _pcall(
        body, name=name, grid=(n // tm, f // tf), in_specs=[spec, spec], out_specs=spec,
        out_shape=jax.ShapeDtypeStruct((n, f), BF16), compiler_params=_params(("parallel", "parallel")),
    )(da, u)


def _loss_head(y, tgt, *, name):
    n, d = y.shape
    tm = _tile(n, 256, 8)
    steps = n // tm

    def body(y_ref, t_ref, dy_ref, loss_ref, acc):
        i = pl.program_id(0)

        @pl.when(i == 0)
        def _():
            acc[...] = jnp.zeros_like(acc)

        e = y_ref[...] - t_ref[...]
        dy_ref[...] = e * (1.0 / d)
        acc[...] += jnp.sum(e * e, axis=0, keepdims=True)

        @pl.when(i == steps - 1)
        def _():
            tot = jnp.sum(acc[...], axis=1, keepdims=True) * (0.5 / d)
            loss_ref[...] = jnp.broadcast_to(tot, loss_ref.shape)

    row = pl.BlockSpec((tm, d), lambda i: (i, 0))
    return _pcall(
        body, name=name, grid=(steps,), in_specs=[row, row], out_specs=[row, pl.BlockSpec((8, LANES), lambda i: (0, 0))],
        out_shape=[jax.ShapeDtypeStruct((n, d), F32), jax.ShapeDtypeStruct((8, LANES), F32)],
        scratch_shapes=[pltpu.VMEM((1, d), F32)], compiler_params=_params(("arbitrary",)),
    )(y, tgt)


def _dil_combine(os_, lses, *, name):
    n, w = os_[0].shape
    tm = _tile(n, 512, 8)

    def body(o0, o1, o2, l0, l1, l2, out_ref, lse_ref):
        a, b, c = l0[...], l1[...], l2[...]
        m = jnp.maximum(jnp.maximum(a, b), c)
        ea, eb, ec = jnp.exp(a - m), jnp.exp(b - m), jnp.exp(c - m)
        den = ea + eb + ec
        out_ref[...] = (ea * o0[...] + eb * o1[...] + ec * o2[...]) / den
        lse_ref[...] = m + jnp.log(den)

    spec = pl.BlockSpec((tm, w), lambda i: (i, 0))
    sds = jax.ShapeDtypeStruct((n, w), F32)
    return _pcall(
        body, name=name, grid=(n // tm,), in_specs=[spec] * 6, out_specs=[spec, spec], out_shape=[sds, sds],
        compiler_params=_params(("parallel",)),
    )(*os_, *lses)


def _adamw(gsrc, w, m, v, *, name):
    s, dep, a, b = gsrc.shape
    ta = _tile(a, max(16, (ADAMW_BLOCK_ELEMS // b) // 16 * 16), 16)
    c1 = 1.0 / (1.0 - ADAM_B1 ** ADAM_STEP)
    c2 = 1.0 / (1.0 - ADAM_B2 ** ADAM_STEP)

    def body(gs_ref, w_ref, m_ref, v_ref, g_ref, d_ref, m2_ref, v2_ref):
        g = gs_ref[0].astype(F32)
        for i in range(1, s):
            g = g + gs_ref[i].astype(F32)
        m2 = ADAM_B1 * m_ref[...] + (1.0 - ADAM_B1) * g
        v2 = ADAM_B2 * v_ref[...] + (1.0 - ADAM_B2) * (g * g)
        g_ref[...] = g
        m2_ref[...] = m2
        v2_ref[...] = v2
        d_ref[...] = -ADAM_LR * ((m2 * c1) / (jnp.sqrt(v2 * c2) + ADAM_EPS) + ADAM_WD * w_ref[...])

    spec = pl.BlockSpec((None, ta, b), lambda l, i: (l, i, 0))
    sds = jax.ShapeDtypeStruct((dep, a, b), F32)
    return _pcall(
        body, name=name, grid=(dep, a // ta),
        in_specs=[pl.BlockSpec((s, None, ta, b), lambda l, i: (0, l, i, 0)), spec, spec, spec],
        out_specs=[spec] * 4, out_shape=[sds] * 4, compiler_params=_params(("parallel", "parallel")),
    )(gsrc, w, m, v)


def _mask_a():
    return lax.broadcasted_iota(jnp.int32, (1, LANES), 1) < HEAD_DIM


def _half_sum(x, m_a):
    sa = jnp.sum(jnp.where(m_a, x, 0.0), axis=1, keepdims=True)
    sb = jnp.sum(jnp.where(m_a, 0.0, x), axis=1, keepdims=True)
    return jnp.where(m_a, sa, sb)


def _head_inv(x, m_a):
    return lax.rsqrt(_half_sum(x * x, m_a) * (1.0 / HEAD_DIM) + EPS)


def _swap32(x):
    first = (lax.broadcasted_iota(jnp.int32, (1, LANES), 1) % HEAD_DIM) < (HEAD_DIM // 2)
    return jnp.where(first, pltpu.roll(x, LANES - HEAD_DIM // 2, 1), pltpu.roll(x, HEAD_DIM // 2, 1))


def _head_col(t, h):
    if h == 1:
        t = pltpu.roll(t, HEAD_DIM, 1)
    return t[:, 0:1]


def _tri(blk, rel):
    r = lax.broadcasted_iota(jnp.int32, (blk, blk), 0)
    c = lax.broadcasted_iota(jnp.int32, (blk, blk), 1)
    return jnp.where(rel(r, c), 1.0, 0.0).astype(BF16)


def _cumdot(x, u, parts):
    acc = None
    r = x
    for i in range(parts):
        xi = r.astype(BF16)
        t = jnp.dot(xi, u, preferred_element_type=F32)
        acc = t if acc is None else acc + t
        if i + 1 < parts:
            r = r - xi.astype(F32)
    return acc


def _rows(i, blk):
    return pl.ds(pl.multiple_of(i * blk, blk), blk)


def _col_spec(n, off):
    return pl.BlockSpec((None, n, LANES), lambda z, p, off=off: (z, 0, off + p))


def _att_blk(n):
    return ATT_BLK if n % ATT_BLK == 0 else min(LANES, n)


def _loop(lo, hi, fn):
    def it(i, c):
        fn(i)
        return c

    lax.fori_loop(lo, hi, it, 0)


def _prep_qk(q_ref, k_ref, v_ref, gq_ref, gk_ref, rope_refs, qn_s, kn_s, vb_s, nb, blk, m_a):
    def prep(c):
        rows = _rows(c, blk)
        for src, g_ref, dst, scale in ((q_ref, gq_ref, qn_s, SCALE), (k_ref, gk_ref, kn_s, None)):
            xv = src[rows, :]
            xn = xv * _head_inv(xv, m_a) * g_ref[...]
            if rope_refs is not None:
                xn = xn * rope_refs[0][rows, :] + _swap32(xn) * rope_refs[1][rows, :]
            if scale is not None:
                xn = xn * scale
            dst[rows, :] = xn.astype(BF16)
        vb_s[rows, :] = v_ref[rows, :].astype(BF16)

    _loop(0, nb, prep)


def _attn_fwd(src, offs, npairs, gq, gk, *, rope=None, kbias=None, window, name, cargo=None):
    zs, n, _ = src.shape
    blk = _att_blk(n)
    nb = n // blk
    full = window >= n
    wblk = min(-(-window // blk), nb)

    def body(*refs):
        it = iter(refs)
        q_ref, k_ref, v_ref, gq_ref, gk_ref = (next(it) for _ in range(5))
        rope_refs = (next(it), next(it)) if rope is not None else None
        kb_ref = next(it) if kbias is not None else None
        o_ref, lse_ref, qn_s, kn_s, vb_s, acc_s, m_s, l_s = (next(it) for _ in range(8))
        m_a = _mask_a()
        m_b = jnp.logical_not(m_a)
        _prep_qk(q_ref, k_ref, v_ref, gq_ref, gk_ref, rope_refs, qn_s, kn_s, vb_s, nb, blk, m_a)
        dif = lax.broadcasted_iota(jnp.int32, (blk, blk), 0) - lax.broadcasted_iota(jnp.int32, (blk, blk), 1)

        def qblock(qi):
            rows = _rows(qi, blk)
            qb = qn_s[rows, :]
            qms = (jnp.where(m_a, qb, 0), jnp.where(m_b, qb, 0))
            m_s[...] = jnp.full(m_s.shape, NEG, F32)
            l_s[...] = jnp.zeros_like(l_s)
            acc_s[...] = jnp.zeros_like(acc_s)

            def step(kj, masked):
                cols = _rows(kj, blk)
                kn, vb = kn_s[cols, :], vb_s[cols, :]
                if masked:
                    d = dif + (qi - kj) * blk
                    ok = (d >= 0) & (d <= window)
                for h in (0, 1):
                    s = lax.dot_general(qms[h], kn, NT, preferred_element_type=F32)
                    if kb_ref is not None:
                        s = s + kb_ref[h, kj]
                    if masked:
                        s = jnp.where(ok, s, NEG)
                    m = m_s[h]
                    m2 = jnp.maximum(m, jnp.max(s, axis=1, keepdims=True))
                    al = jnp.exp(m - m2)
                    p = jnp.exp(s - m2)
                    l_s[h] = al * l_s[h] + jnp.sum(p, axis=1, keepdims=True)
                    acc_s[h] = al * acc_s[h] + jnp.dot(p.astype(BF16), vb, preferred_element_type=F32)
                    m_s[h] = m2

            if full:
                _loop(0, qi, lambda kj: step(kj, False))
                step(qi, True)
            else:
                _loop(jnp.maximum(qi - wblk, 0), qi + 1, lambda kj: step(kj, True))
            o_ref[rows, :] = jnp.where(m_a, acc_s[0] / l_s[0], acc_s[1] / l_s[1])
            lse_ref[rows, :] = jnp.where(m_a, m_s[0] + jnp.log(l_s[0]), m_s[1] + jnp.log(l_s[1]))

        _loop(0, nb, qblock)

    vec = pl.BlockSpec((1, LANES), lambda z, p: (0, 0))
    ins = [src, src, src, gq, gk]
    specs = [_col_spec(n, offs[0]), _col_spec(n, offs[1]), _col_spec(n, offs[2]), vec, vec]
    if rope is not None:
        ins += list(rope)
        specs += [pl.BlockSpec((None, n, LANES), lambda z, p: (z, 0, 0))] * 2
    if kbias is not None:
        ins.append(kbias)
        specs.append(pl.BlockSpec((None, 2, nb, 1, blk), lambda z, p: (z, p, 0, 0, 0)))
    ospec = _col_spec(n, 0)
    sds = jax.ShapeDtypeStruct((zs, n, LANES * npairs), F32)
    scratch = [pltpu.VMEM((n, LANES), BF16)] * 3
    scratch += [pltpu.VMEM((2, blk, LANES), F32), pltpu.VMEM((2, blk, 1), F32), pltpu.VMEM((2, blk, 1), F32)]
    return _call(
        body, cargo=cargo, name=name, grid=(zs, npairs), in_specs=specs, out_specs=[ospec, ospec], out_shape=[sds, sds],
        scratch_shapes=scratch, compiler_params=_params(("parallel", "parallel")),
    )(*ins)


def _attn_bwd(src, offs, npairs, gq, gk, o, do, lse, *, rope=None, kbias=None, window, name, cargo=None):
    zs, n, _ = src.shape
    blk = _att_blk(n)
    nb = n // blk
    full = window >= n
    wblk = min(-(-window // blk), nb)

    def body(*refs):
        it = iter(refs)
        q_ref, k_ref, v_ref, gq_ref, gk_ref, o_ref, do_ref, lse_ref = (next(it) for _ in range(8))
        rope_refs = (next(it), next(it)) if rope is not None else None
        kb_ref = next(it) if kbias is not None else None
        dq_ref, dk_ref, dv_ref, dg_ref = (next(it) for _ in range(4))
        dkb_ref = next(it) if kbias is not None else None
        qn_s, kn_s, vb_s, dqn_s, dkn_s, dq_s, rs_s = (next(it) for _ in range(7))
        m_a = _mask_a()
        m_b = jnp.logical_not(m_a)

        @pl.when((pl.program_id(0) == 0) & (pl.program_id(1) == 0))
        def _():
            dg_ref[...] = jnp.zeros_like(dg_ref)

        _prep_qk(q_ref, k_ref, v_ref, gq_ref, gk_ref, rope_refs, qn_s, kn_s, vb_s, nb, blk, m_a)
        dkn_s[...] = jnp.zeros_like(dkn_s)
        dv_ref[...] = jnp.zeros_like(dv_ref)
        if dkb_ref is not None:
            dkb_ref[...] = jnp.zeros_like(dkb_ref)
        dif = lax.broadcasted_iota(jnp.int32, (blk, blk), 0) - lax.broadcasted_iota(jnp.int32, (blk, blk), 1)

        def qblock(qi):
            rows = _rows(qi, blk)
            qb = qn_s[rows, :]
            dob = do_ref[rows, :]
            lse_t = lse_ref[rows, :]
            delta_t = _half_sum(dob * o_ref[rows, :], m_a)
            qms = (jnp.where(m_a, qb, 0), jnp.where(m_b, qb, 0))
            doms = (jnp.where(m_a, dob, 0.0).astype(BF16), jnp.where(m_b, dob, 0.0).astype(BF16))
            lse_c = (_head_col(lse_t, 0), _head_col(lse_t, 1))
            delta_c = (_head_col(delta_t, 0), _head_col(delta_t, 1))
            dq_s[...] = jnp.zeros_like(dq_s)
            if dkb_ref is not None:
                rs_s[...] = jnp.zeros_like(rs_s)

            def step(kj, masked):
                cols = _rows(kj, blk)
                kn, vb = kn_s[cols, :], vb_s[cols, :]
                if masked:
                    d = dif + (qi - kj) * blk
                    ok = (d >= 0) & (d <= window)
                dk_acc = dv_acc = None
                for h in (0, 1):
                    s = lax.dot_general(qms[h], kn, NT, preferred_element_type=F32)
                    if kb_ref is not None:
                        s = s + kb_ref[h, kj]
                    if masked:
                        s = jnp.where(ok, s, NEG)
                    p = jnp.exp(s - lse_c[h])
                    dp = lax.dot_general(doms[h], vb, NT, preferred_element_type=F32)
                    ds = p * (dp - delta_c[h])
                    dsb = ds.astype(BF16)
                    tk = lax.dot_general(dsb, qms[h], TN, preferred_element_type=F32)
                    tv = lax.dot_general(p.astype(BF16), doms[h], TN, preferred_element_type=F32)
                    dk_acc = tk if dk_acc is None else dk_acc + tk
                    dv_acc = tv if dv_acc is None else dv_acc + tv
                    dq_s[h] += jnp.dot(dsb, kn, preferred_element_type=F32)
                    if dkb_ref is not None:
                        dkb_ref[h, kj] += jnp.sum(ds, axis=0, keepdims=True)
                        rs_s[h] += jnp.sum(ds, axis=1, keepdims=True)
                dkn_s[cols, :] += dk_acc
                dv_ref[cols, :] += dv_acc

            if full:
                _loop(0, qi, lambda kj: step(kj, False))
                step(qi, True)
            else:
                _loop(jnp.maximum(qi - wblk, 0), qi + 1, lambda kj: step(kj, True))
            dqn_s[rows, :] = jnp.where(m_a, dq_s[0], dq_s[1]) * SCALE
            if dkb_ref is not None:
                for h in (0, 1):
                    dkb_ref[h, qi] -= jnp.transpose(jnp.broadcast_to(rs_s[h], (blk, blk)))[0:1, :]

        _loop(0, nb, qblock)

        def finish(c, carry):
            rows = _rows(c, blk)
            out = []
            for src_ref, g_ref, d_s, dst in ((q_ref, gq_ref, dqn_s, dq_ref), (k_ref, gk_ref, dkn_s, dk_ref)):
                xv = src_ref[rows, :]
                inv = _head_inv(xv, m_a)
                y = xv * inv
                dxn = d_s[rows, :]
                if rope_refs is not None:
                    dxn = dxn * rope_refs[0][rows, :] + _swap32(dxn * rope_refs[1][rows, :])
                dy = dxn * g_ref[...]
                dst[rows, :] = inv * (dy - y * (_half_sum(dy * y, m_a) * (1.0 / HEAD_DIM)))
                out.append(jnp.sum(dxn * y, axis=0, keepdims=True))
            return carry[0] + out[0], carry[1] + out[1]

        zero = jnp.zeros((1, LANES), F32)
        dgq, dgk = lax.fori_loop(0, nb, finish, (zero, zero))
        dg_ref[0:1, :] += dgq
        dg_ref[1:2, :] += dgk

    vec = pl.BlockSpec((1, LANES), lambda z, p: (0, 0))
    ospec = _col_spec(n, 0)
    ins = [src, src, src, gq, gk, o, do, lse]
    specs = [_col_spec(n, offs[0]), _col_spec(n, offs[1]), _col_spec(n, offs[2]), vec, vec, ospec, ospec, ospec]
    if rope is not None:
        ins += list(rope)
        specs += [pl.BlockSpec((None, n, LANES), lambda z, p: (z, 0, 0))] * 2
    sds = jax.ShapeDtypeStruct((zs, n, LANES * npairs), F32)
    out_shape = [sds, sds, sds, jax.ShapeDtypeStruct((8, LANES), F32)]
    out_specs = [ospec, ospec, ospec, pl.BlockSpec((8, LANES), lambda z, p: (0, 0))]
    if kbias is not None:
        ins.append(kbias)
        kbspec = pl.BlockSpec((None, 2, nb, 1, blk), lambda z, p: (z, p, 0, 0, 0))
        specs.append(kbspec)
        out_shape.append(jax.ShapeDtypeStruct(kbias.shape, F32))
        out_specs.append(kbspec)
    scratch = [pltpu.VMEM((n, LANES), BF16)] * 3 + [pltpu.VMEM((n, LANES), F32)] * 2
    scratch += [pltpu.VMEM((2, blk, LANES), F32), pltpu.VMEM((2, blk, 1), F32)]
    return _call(
        body, cargo=cargo, name=name, grid=(zs, npairs), in_specs=specs, out_specs=out_specs, out_shape=out_shape,
        scratch_shapes=scratch, compiler_params=_params(("arbitrary", "arbitrary")),
    )(*ins)


def _log_sig_pair(z):
    lsn = -(jnp.maximum(z, 0.0) + jnp.log(1.0 + jnp.exp(-jnp.abs(z))))
    return lsn, z + lsn


def _prep_sb(q_ref, k_ref, v_ref, qs_s, kb_s, vb_s, nb, blk):
    def prep(c):
        rows = _rows(c, blk)
        qs_s[rows, :] = (q_ref[rows, :] * SCALE).astype(BF16)
        kb_s[rows, :] = k_ref[rows, :].astype(BF16)
        vb_s[rows, :] = v_ref[rows, :].astype(BF16)

    _loop(0, nb, prep)


def _sb_fwd(src, offs, npairs, *, name, cargo=None):
    zs, n, _ = src.shape
    blk = _att_blk(n)
    nb = n // blk

    def body(q_ref, k_ref, v_ref, o_ref, lt_ref, qs_s, kb_s, vb_s, acc_s, c_s):
        m_a = _mask_a()
        m_b = jnp.logical_not(m_a)
        _prep_sb(q_ref, k_ref, v_ref, qs_s, kb_s, vb_s, nb, blk)
        diag_ok = lax.broadcasted_iota(jnp.int32, (blk, blk), 0) > lax.broadcasted_iota(jnp.int32, (blk, blk), 1)
        u_gt = _tri(blk, lambda r, c: r > c)

        def qblock(qi):
            rows = _rows(qi, blk)
            qb = qs_s[rows, :]
            qms = (jnp.where(m_a, qb, 0), jnp.where(m_b, qb, 0))
            acc_s[...] = jnp.zeros_like(acc_s)
            c_s[...] = jnp.zeros_like(c_s)

            def step(kj, masked):
                cols = _rows(kj, blk)
                kb, vb = kb_s[cols, :], vb_s[cols, :]
                for h in (0, 1):
                    z = lax.dot_general(qms[h], kb, NT, preferred_element_type=F32)
                    lsn, lsp = _log_sig_pair(z)
                    if masked:
                        lsn = jnp.where(diag_ok, lsn, 0.0)
                    a = jnp.exp(lsp + (c_s[h] + _cumdot(lsn, u_gt, 3)))
                    if masked:
                        a = jnp.where(diag_ok, a, 0.0)
                    acc_s[h] += jnp.dot(a.astype(BF16), vb, preferred_element_type=F32)
                    c_s[h] += jnp.sum(lsn, axis=1, keepdims=True)

            step(qi, True)
            _loop(0, qi, lambda t: step(qi - 1 - t, False))
            o_ref[rows, :] = jnp.where(m_a, acc_s[0], acc_s[1])
            lt_ref[rows, :] = jnp.where(m_a, c_s[0], c_s[1])

        _loop(0, nb, qblock)

    ospec = _col_spec(n, 0)
    sds = jax.ShapeDtypeStruct((zs, n, LANES * npairs), F32)
    scratch = [pltpu.VMEM((n, LANES), BF16)] * 3 + [pltpu.VMEM((2, blk, LANES), F32), pltpu.VMEM((2, blk, 1), F32)]
    return _call(
        body, cargo=cargo, name=name, grid=(zs, npairs),
        in_specs=[_col_spec(n, offs[0]), _col_spec(n, offs[1]), _col_spec(n, offs[2])],
        out_specs=[ospec, ospec], out_shape=[sds, sds], scratch_shapes=scratch,
        compiler_params=_params(("parallel", "parallel")),
    )(src, src, src)


def _sb_bwd(src, offs, npairs, do, ltot, *, name, cargo=None):
    zs, n, _ = src.shape
    blk = _att_blk(n)
    nb = n // blk

    def body(q_ref, k_ref, v_ref, do_ref, lt_ref, dq_ref, dk_ref, dv_ref, qs_s, kb_s, vb_s, dq_s, lp_s, ep_s):
        m_a = _mask_a()
        m_b = jnp.logical_not(m_a)
        _prep_sb(q_ref, k_ref, v_ref, qs_s, kb_s, vb_s, nb, blk)
        dk_ref[...] = jnp.zeros_like(dk_ref)
        dv_ref[...] = jnp.zeros_like(dv_ref)
        diag_ok = lax.broadcasted_iota(jnp.int32, (blk, blk), 0) > lax.broadcasted_iota(jnp.int32, (blk, blk), 1)
        u_le = _tri(blk, lambda r, c: r <= c)
        u_lt = _tri(blk, lambda r, c: r < c)

        def qblock(qi):
            rows = _rows(qi, blk)
            qb = qs_s[rows, :]
            dob = do_ref[rows, :]
            lt_t = lt_ref[rows, :]
            qms = (jnp.where(m_a, qb, 0), jnp.where(m_b, qb, 0))
            doms = (jnp.where(m_a, dob, 0.0).astype(BF16), jnp.where(m_b, dob, 0.0).astype(BF16))
            lt_c = (_head_col(lt_t, 0), _head_col(lt_t, 1))
            dq_s[...] = jnp.zeros_like(dq_s)
            lp_s[...] = jnp.zeros_like(lp_s)
            ep_s[...] = jnp.zeros_like(ep_s)

            def step(kj, masked):
                cols = _rows(kj, blk)
                kb, vb = kb_s[cols, :], vb_s[cols, :]
                dk_acc = dv_acc = None
                for h in (0, 1):
                    z = lax.dot_general(qms[h], kb, NT, preferred_element_type=F32)
                    lsn, lsp = _log_sig_pair(z)
                    if masked:
                        lsn = jnp.where(diag_ok, lsn, 0.0)
                    a = jnp.exp(lsp + (lt_c[h] - (lp_s[h] + _cumdot(lsn, u_le, 3))))
                    if masked:
                        a = jnp.where(diag_ok, a, 0.0)
                    da = lax.dot_general(doms[h], vb, NT, preferred_element_type=F32)
                    e = da * a
                    big_e = ep_s[h] + _cumdot(e, u_lt, 2)
                    sig = jax.nn.sigmoid(z)
                    dz = e * (1.0 - sig) - big_e * sig
                    if masked:
                        dz = jnp.where(diag_ok, dz, 0.0)
                    dzb = dz.astype(BF16)
                    tk = lax.dot_general(dzb, qms[h], TN, preferred_element_type=F32)
                    tv = lax.dot_general(a.astype(BF16), doms[h], TN, preferred_element_type=F32)
                    dk_acc = tk if dk_acc is None else dk_acc + tk
                    dv_acc = tv if dv_acc is None else dv_acc + tv
                    dq_s[h] += jnp.dot(dzb, kb, preferred_element_type=F32)
                    lp_s[h] += jnp.sum(lsn, axis=1, keepdims=True)
                    ep_s[h] += jnp.sum(e, axis=1, keepdims=True)
                dk_ref[cols, :] += dk_acc
                dv_ref[cols, :] += dv_acc

            _loop(0, qi, lambda kj: step(kj, False))
            step(qi, True)
            dq_ref[rows, :] = jnp.where(m_a, dq_s[0], dq_s[1]) * SCALE

        _loop(0, nb, qblock)

    ospec = _col_spec(n, 0)
    sds = jax.ShapeDtypeStruct((zs, n, LANES * npairs), F32)
    scratch = [pltpu.VMEM((n, LANES), BF16)] * 3
    scratch += [pltpu.VMEM((2, blk, LANES), F32), pltpu.VMEM((2, blk, 1), F32), pltpu.VMEM((2, blk, 1), F32)]
    return _call(
        body, cargo=cargo, name=name, grid=(zs, npairs),
        in_specs=[_col_spec(n, offs[0]), _col_spec(n, offs[1]), _col_spec(n, offs[2]), ospec, ospec],
        out_specs=[ospec] * 3, out_shape=[sds] * 3, scratch_shapes=scratch,
        compiler_params=_params(("parallel", "parallel")),
    )(src, src, src, do, ltot)


def _fox_gate_fwd(lg, bias, *, name):
    bs, nh, t = lg.shape
    blk = min(LANES, t)

    def body(lg_ref, b_ref, kb_ref):
        u_le = _tri(blk, lambda r, c: r <= c)
        carry = jnp.zeros((nh, 1), F32)
        for j in range(t // blk):
            sl = slice(j * blk, (j + 1) * blk)
            xv = lg_ref[:, sl] + b_ref[...]
            lf = jnp.minimum(xv, 0.0) - jnp.log(1.0 + jnp.exp(-jnp.abs(xv)))
            kb_ref[:, sl] = -(carry + _cumdot(lf, u_le, 3))
            carry = carry + jnp.sum(lf, axis=1, keepdims=True)

    spec = pl.BlockSpec((None, nh, t), lambda i: (i, 0, 0))
    return _pcall(
        body, name=name, grid=(bs,), in_specs=[spec, pl.BlockSpec((nh, 1), lambda i: (0, 0))], out_specs=spec,
        out_shape=jax.ShapeDtypeStruct((bs, nh, t), F32), compiler_params=_params(("parallel",)),
    )(lg, bias)


def _fox_gate_bwd(dkb, lg, bias, *, name):
    bs, nh, t = lg.shape
    blk = min(LANES, t)

    def body(dkb_ref, lg_ref, b_ref, dlg_ref, db_ref):
        @pl.when(pl.program_id(0) == 0)
        def _():
            db_ref[...] = jnp.zeros_like(db_ref)

        u_ge = _tri(blk, lambda r, c: r >= c)
        carry = jnp.zeros((nh, 1), F32)
        tot = jnp.zeros((nh, 1), F32)
        for j in reversed(range(t // blk)):
            sl = slice(j * blk, (j + 1) * blk)
            df = -dkb_ref[:, sl]
            dlf = carry + _cumdot(df, u_ge, 3)
            carry = carry + jnp.sum(df, axis=1, keepdims=True)
            xv = lg_ref[:, sl] + b_ref[...]
            dlg = dlf * jax.nn.sigmoid(-xv)
            dlg_ref[:, sl] = dlg
            tot = tot + jnp.sum(dlg, axis=1, keepdims=True)
        db_ref[...] += jnp.broadcast_to(tot, db_ref.shape)

    spec = pl.BlockSpec((None, nh, t), lambda i: (i, 0, 0))
    return _pcall(
        body, name=name, grid=(bs,), in_specs=[spec, spec, pl.BlockSpec((nh, 1), lambda i: (0, 0))],
        out_specs=[spec, pl.BlockSpec((nh, LANES), lambda i: (0, 0))],
        out_shape=[jax.ShapeDtypeStruct((bs, nh, t), F32), jax.ShapeDtypeStruct((nh, LANES), F32)],
        compiler_params=_params(("arbitrary",)),
    )(dkb, lg, bias)


def _place():
    return lax.axis_index("x"), lax.axis_index("y"), lax.axis_index("c")


def _flip(v, f):
    return 1 - v if f else v


FLIPS = [(fx, fy, fc) for fx in (0, 1) for fy in (0, 1) for fc in (0, 1)][1:]


def _comm_sems(nw):
    return [pltpu.SemaphoreType.DMA((7, nw)), pltpu.SemaphoreType.DMA((7, nw)), pltpu.SemaphoreType.DMA((nw,))]


def _gather_cargo(shards, on_done):
    nw = len(shards)

    def parts(x_refs, out_refs, sems):
        send_sems, recv_sems, local_sems = sems
        x, y, cc = _place()
        me, sibling = (x, y, cc), (x, y, 1 - cc)
        chips = [(1 - x, y), (x, 1 - y), (1 - x, 1 - y)]

        def slot(i, px, py, pc):
            return out_refs[i].at[4 * px + 2 * py + pc]

        def copy(i, k, block, to, src=None):
            return pltpu.make_async_remote_copy(
                src_ref=slot(i, *block) if src is None else src, dst_ref=slot(i, *block),
                send_sem=send_sems.at[k, i], recv_sem=recv_sems.at[k, i], device_id=to, device_id_type=MESH)

        mine = [pltpu.make_async_copy(x_refs[i], slot(i, *me), local_sems.at[i]) for i in range(nw)]
        first = []
        for i in range(nw):
            first.append(copy(i, 0, me, sibling, src=x_refs[i]))
            first += [copy(i, 1 + j, me, (*chip, cc), src=x_refs[i]) for j, chip in enumerate(chips)]
        return me, sibling, chips, cc, copy, mine, first

    def start(x_refs, out_refs, sems):
        *_, mine, first = parts(x_refs, out_refs, sems)
        for cp in mine + first:
            cp.start()

    def finish(x_refs, out_refs, sems):
        me, sibling, chips, cc, copy, mine, first = parts(x_refs, out_refs, sems)
        passed = []
        for i in range(nw):
            for j, chip in enumerate(chips):
                copy(i, 1 + j, (*chip, cc), me).wait_recv()
                passed.append(copy(i, 4 + j, (*chip, cc), sibling))
                passed[-1].start()
        for i in range(nw):
            copy(i, 0, sibling, me).wait_recv()
            for j, chip in enumerate(chips):
                copy(i, 4 + j, (*chip, 1 - cc), me).wait_recv()
        for cp in first + passed:
            cp.wait_send()
        for cp in mine:
            cp.wait()

    out_shape = [jax.ShapeDtypeStruct((N_DEV, *s.shape), s.dtype) for s in shards]
    return _Cargo(shards, out_shape, _comm_sems(nw), start, finish, on_done)


def _scatter_cargo(slots, prev, layer, depth, on_done):
    nw = len(slots)

    def parts(refs, recv_refs, sems):
        g_refs = refs[:nw]
        send_sems, recv_sems, local_sems = sems
        x, y, cc = _place()
        my = 4 * x + 2 * y + cc
        mine, copies = [], []
        for i in range(nw):
            mine.append(pltpu.make_async_copy(g_refs[i].at[my], recv_refs[i].at[my, layer], local_sems.at[i]))
            for k, (fx, fy, fc) in enumerate(FLIPS):
                px, py, pc = _flip(x, fx), _flip(y, fy), _flip(cc, fc)
                copies.append(pltpu.make_async_remote_copy(
                    src_ref=g_refs[i].at[4 * px + 2 * py + pc], dst_ref=recv_refs[i].at[my, layer],
                    send_sem=send_sems.at[k, i], recv_sem=recv_sems.at[k, i], device_id=(px, py, pc), device_id_type=MESH))
        return mine, copies

    def start(refs, recv_refs, sems):
        mine, copies = parts(refs, recv_refs, sems)
        for cp in mine + copies:
            cp.start()

    def finish(refs, recv_refs, sems):
        mine, copies = parts(refs, recv_refs, sems)
        for cp in copies:
            cp.wait_recv()
        for cp in copies:
            cp.wait_send()
        for cp in mine:
            cp.wait()

    ins, aliases = list(slots), {}
    for i, p in enumerate(prev):
        if p is not None:
            aliases[len(ins)] = i
            ins.append(p)
    out_shape = [jax.ShapeDtypeStruct((N_DEV, depth, *s.shape[1:]), s.dtype) for s in slots]
    return _Cargo(ins, out_shape, _comm_sems(nw), start, finish, on_done, aliases)


def _exchange(cargo, *, name):
    def body(*refs):
        c_in = len(cargo.ins)
        c_out = len(cargo.out_shape)
        cargo.start(refs[:c_in], refs[c_in:c_in + c_out], refs[c_in + c_out:])
        cargo.finish(refs[:c_in], refs[c_in:c_in + c_out], refs[c_in + c_out:])

    hbm = pl.BlockSpec(memory_space=pl.ANY)
    res = _pcall(
        body, name=name, in_specs=[hbm] * len(cargo.ins), out_specs=[hbm] * len(cargo.out_shape), out_shape=cargo.out_shape,
        scratch_shapes=cargo.sems, input_output_aliases=dict(cargo.aliases),
    )(*cargo.ins)
    cargo.on_done(list(res))


def _allreduce_small(blob, *, name):
    r, c = blob.shape

    def body(x_ref, out_ref, buf, send_sems, recv_sems):
        x, y, cc = _place()
        my = 4 * x + 2 * y + cc
        copies = []
        for k, (fx, fy, fc) in enumerate(FLIPS):
            peer = (_flip(x, fx), _flip(y, fy), _flip(cc, fc))
            copies.append(pltpu.make_async_remote_copy(
                src_ref=x_ref, dst_ref=buf.at[my], send_sem=send_sems.at[k], recv_sem=recv_sems.at[k],
                device_id=peer, device_id_type=MESH))
        for cp in copies:
            cp.start()
        buf[my] = x_ref[...]
        for cp in copies:
            cp.wait_recv()
        for cp in copies:
            cp.wait_send()
        acc = buf[0]
        for i in range(1, N_DEV):
            acc = acc + buf[i]
        out_ref[...] = acc

    vmem = pl.BlockSpec(memory_space=pltpu.VMEM)
    return _pcall(
        body, name=name, in_specs=[vmem], out_specs=vmem, out_shape=jax.ShapeDtypeStruct((r, c), F32),
        scratch_shapes=[pltpu.VMEM((N_DEV, r, c), F32), pltpu.SemaphoreType.DMA((7,)), pltpu.SemaphoreType.DMA((7,))],
    )(blob)


BIG = ("w_in", "w_mlp_in", "w_mlp_out", "w_up_fox", "w_up_sb", "w_up_dil", "w_out")
ROW_SHARDED = ("w_out", "w_mlp_out")
SMALL = ("attn_norm", "b_forget", "q_norm_fox", "k_norm_fox", "q_norm_dil", "k_norm_dil", "mlp_norm")
BLOB_ROWS = 512


def _pack(parts, dtype):
    flat = jnp.concatenate([p.reshape(-1).astype(dtype) for p in parts])
    size = -(-flat.shape[0] // (BLOB_ROWS * LANES)) * (BLOB_ROWS * LANES)
    return jnp.pad(flat, (0, size - flat.shape[0])).reshape(-1, LANES)


def _unpack(blob, shapes):
    flat = blob.reshape(-1)
    out, off = [], 0
    for shp in shapes:
        size = 1
        for s in shp:
            size *= s
        out.append(flat[off:off + size].reshape(shp))
        off += size
    return out


def _join_shards(name, sh):
    if name in ROW_SHARDED:
        return sh.reshape(-1, sh.shape[2])
    return jnp.transpose(sh, (1, 0, 2)).reshape(sh.shape[1], -1)


def _split_shards(name, full):
    a, b = full.shape
    if name in ROW_SHARDED:
        return full.reshape(N_DEV, a // N_DEV, b)
    return jnp.transpose(full.reshape(a, N_DEV, b // N_DEV), (1, 0, 2))


def _permute_in(w, dp):
    o1 = 3 * W_FOX
    o2 = o1 + N_HEADS_FOX
    pad = [(0, 0)] * (w.ndim - 1) + [(0, dp - w.shape[-1])]
    return jnp.pad(jnp.concatenate([w[..., :o1], w[..., o2:], w[..., o1:o2]], axis=-1), pad)


def _unpermute_in(wp, d_in):
    o1 = 3 * W_FOX
    fg = d_in - N_HEADS_FOX
    return jnp.concatenate([wp[..., :o1], wp[..., fg:d_in], wp[..., o1:fg]], axis=-1)


def _to_streams(a, r):
    b, t, c = a.shape
    if r == 1:
        return a
    return jnp.transpose(a.reshape(b, t // r, r, c), (0, 2, 1, 3)).reshape(b * r, t // r, c)


def _from_streams(a, r, b):
    if r == 1:
        return a
    z, n, c = a.shape
    return jnp.transpose(a.reshape(b, r, n, c), (0, 2, 1, 3)).reshape(b, n * r, c)


def _rope_tables(positions):
    half = HEAD_DIM // 2
    inv = 1.0 / (ROPE_THETA ** (jnp.arange(half, dtype=F32) / half))
    ang = positions.astype(F32)[..., None] * inv
    cos, sin = jnp.cos(ang), jnp.sin(ang)
    return jnp.tile(cos, (1, 1, 4)), jnp.tile(jnp.concatenate([-sin, sin], axis=-1), (1, 1, 2))


def _gain2(g):
    return jnp.tile(g.reshape(1, HEAD_DIM), (1, 2))


def _layer_fwd(l, x, w, small, ropes, bl, t, cargo):
    n, d = x.shape
    s = {}
    s["x"] = x
    s["h"] = _rmsnorm_fwd(x, small["attn_norm"][l].reshape(1, d), name=f"norm_attn_fwd{l}")
    proj = _mm(s["h"], w["w_in"][l], name=f"mm_proj{l}")
    s["proj"] = proj
    dp = proj.shape[1]
    proj3 = proj.reshape(bl, t, dp)
    p_fg = P_GATE + 3 * d

    lg = jnp.transpose(proj3[:, :, p_fg:p_fg + N_HEADS_FOX], (0, 2, 1))
    s["lg"] = lg
    kb = _fox_gate_fwd(lg, small["b_forget"][l].reshape(N_HEADS_FOX, 1), name=f"fox_gate_fwd{l}")
    blk = _att_blk(t)
    kb5 = kb.reshape(bl, N_HEADS_FOX, t // blk, 1, blk)
    s["kb5"] = kb5
    gqf, gkf = _gain2(small["q_norm_fox"][l]), _gain2(small["k_norm_fox"][l])
    fo = P_FOX // LANES
    fox_offs = (fo, fo + W_FOX // LANES, fo + 2 * W_FOX // LANES)
    out_a, lse_a = _attn_fwd(proj3, fox_offs, N_HEADS_FOX // 2, gqf, gkf, kbias=kb5, window=t, name=f"fox_fwd{l}",
                             cargo=cargo.get("fox_fwd"))
    s["out_a"], s["lse_a"] = out_a, lse_a

    so = P_SB // LANES
    sb_offs = (so, so + W_SB // LANES, so + 2 * W_SB // LANES)
    out_b, lt_b = _sb_fwd(proj3, sb_offs, N_HEADS_SB // 2, name=f"sb_fwd{l}", cargo=cargo.get("sb_fwd"))
    s["out_b"], s["lt_b"] = out_b, lt_b

    gqd, gkd = _gain2(small["q_norm_dil"][l]), _gain2(small["k_norm_dil"][l])
    os_, lses, s["dil_src"] = [], [], []
    for g, (window, r) in enumerate(DIL_PATTERNS):
        c0 = P_DIL + g * W_DIL
        if r == 1:
            src, offs = proj3, (c0 // LANES, (c0 + W_DILQ) // LANES, (c0 + 2 * W_DILQ) // LANES)
        else:
            qkv = jnp.concatenate([proj3[:, :, c0 + i * W_DILQ:c0 + i * W_DILQ + W_DIL] for i in range(3)], axis=-1)
            src, offs = _to_streams(qkv, r), (0, W_DIL // LANES, 2 * W_DIL // LANES)
        s["dil_src"].append((src, offs))
        o_g, lse_g = _attn_fwd(src, offs, N_HEADS_DIL // 2, gqd, gkd, rope=ropes[g], window=window // r, name=f"dil_fwd{l}_{g}")
        os_.append(_from_streams(o_g, r, bl).reshape(n, W_DIL))
        lses.append(_from_streams(lse_g, r, bl).reshape(n, W_DIL))
    out_c, lse_c = _dil_combine(os_, lses, name=f"dil_combine{l}")
    s["out_c"], s["lse_c"] = out_c, lse_c

    ys = [_mm(out_a.reshape(n, W_FOX), w["w_up_fox"][l], name=f"mm_up_fox{l}"),
          _mm(out_b.reshape(n, W_SB), w["w_up_sb"][l], name=f"mm_up_sb{l}"),
          _mm(out_c, w["w_up_dil"][l], name=f"mm_up_dil{l}")]
    s["ys"] = ys
    s["merged"] = _gate_merge_fwd(proj, ys, name=f"gate_merge_fwd{l}")
    x1 = _mm(s["merged"], w["w_out"][l], add=x, name=f"mm_out{l}")
    s["x1"] = x1

    s["h2"] = _rmsnorm_fwd(x1, small["mlp_norm"][l].reshape(1, d), name=f"norm_mlp_fwd{l}")
    s["u"] = _mm(s["h2"], w["w_mlp_in"][l], name=f"mm_mlp_in{l}")
    s["a"] = _relu2_fwd(s["u"], name=f"relu2_fwd{l}")
    x2 = _mm(s["a"], w["w_mlp_out"][l], add=x1, name=f"mm_mlp_out{l}")
    return x2, s


def _layer_bwd(l, dx2, s, w, small, ropes, bl, t, hooks):
    n, d = dx2.shape
    gw, gs = {}, {}

    def cargo(call):
        return hooks[call](gw) if call in hooks else None
    da = _mm(dx2, w["w_mlp_out"][l], tb=True, name=f"mm_da{l}")
    du = _relu2_bwd(da, s["u"], name=f"relu2_bwd{l}")
    gw["w_mlp_out"] = _mm(s["a"], dx2, ta=True, name=f"mm_dw_mlp_out{l}")
    gw["w_mlp_in"] = _mm(s["h2"], du, ta=True, name=f"mm_dw_mlp_in{l}")
    dh2 = _mm(du, w["w_mlp_in"][l], tb=True, name=f"mm_dh2{l}")
    dx1, gs["mlp_norm"] = _rmsnorm_bwd(s["x1"], small["mlp_norm"][l].reshape(1, d), dh2, dx2, name=f"norm_mlp_bwd{l}")

    dmerged = _mm(dx1, w["w_out"][l], tb=True, name=f"mm_dmerged{l}")
    gw["w_out"] = _mm(s["merged"], dx1, ta=True, name=f"mm_dw_out{l}")
    dya, dyb, dyc, dgl0, dgl1, dgl2 = _gate_merge_bwd(s["proj"], s["ys"], dmerged, name=f"gate_merge_bwd{l}")
    out_a2, out_b2 = s["out_a"].reshape(n, W_FOX), s["out_b"].reshape(n, W_SB)
    gw["w_up_fox"] = _mm(out_a2, dya, ta=True, name=f"mm_dw_up_fox{l}")
    gw["w_up_sb"] = _mm(out_b2, dyb, ta=True, name=f"mm_dw_up_sb{l}")
    gw["w_up_dil"] = _mm(s["out_c"], dyc, ta=True, name=f"mm_dw_up_dil{l}")
    dout_a = _mm(dya, w["w_up_fox"][l], tb=True, name=f"mm_dout_a{l}").reshape(bl, t, W_FOX)
    dout_b = _mm(dyb, w["w_up_sb"][l], tb=True, name=f"mm_dout_b{l}").reshape(bl, t, W_SB)
    dout_c = _mm(dyc, w["w_up_dil"][l], tb=True, name=f"mm_dout_c{l}").reshape(bl, t, W_DIL)

    proj3 = s["proj"].reshape(bl, t, -1)
    gqf, gkf = _gain2(small["q_norm_fox"][l]), _gain2(small["k_norm_fox"][l])
    fo = P_FOX // LANES
    fox_offs = (fo, fo + W_FOX // LANES, fo + 2 * W_FOX // LANES)
    dq_a, dk_a, dv_a, dg_a, dkb5 = _attn_bwd(proj3, fox_offs, N_HEADS_FOX // 2, gqf, gkf, s["out_a"], dout_a, s["lse_a"],
                                             kbias=s["kb5"], window=t, name=f"fox_bwd{l}", cargo=cargo("fox_bwd"))
    gs["fox_gains"] = dg_a
    dlg, gs["b_forget"] = _fox_gate_bwd(dkb5.reshape(bl, N_HEADS_FOX, t), s["lg"], small["b_forget"][l].reshape(N_HEADS_FOX, 1),
                                        name=f"fox_gate_bwd{l}")
    so = P_SB // LANES
    sb_offs = (so, so + W_SB // LANES, so + 2 * W_SB // LANES)
    dq_b, dk_b, dv_b = _sb_bwd(proj3, sb_offs, N_HEADS_SB // 2, dout_b, s["lt_b"], name=f"sb_bwd{l}", cargo=cargo("sb_bwd"))
    gqd, gkd = _gain2(small["q_norm_dil"][l]), _gain2(small["k_norm_dil"][l])
    out_c3, lse_c3 = s["out_c"].reshape(bl, t, W_DIL), s["lse_c"].reshape(bl, t, W_DIL)
    dqs, dks, dvs, dgd = [], [], [], None
    for g, (window, r) in enumerate(DIL_PATTERNS):
        src, offs = s["dil_src"][g]
        dq_g, dk_g, dv_g, dg_g = _attn_bwd(src, offs, N_HEADS_DIL // 2, gqd, gkd, _to_streams(out_c3, r), _to_streams(dout_c, r),
                                           _to_streams(lse_c3, r), rope=ropes[g], window=window // r, name=f"dil_bwd{l}_{g}")
        dqs.append(_from_streams(dq_g, r, bl))
        dks.append(_from_streams(dk_g, r, bl))
        dvs.append(_from_streams(dv_g, r, bl))
        dgd = dg_g if dgd is None else jnp.concatenate([dgd, dg_g], axis=0)
    gs["dil_gains"] = dgd

    dp = s["proj"].shape[1]
    parts = [dq_a, dk_a, dv_a, dq_b, dk_b, dv_b] + dqs + dks + dvs
    parts = [p.reshape(n, -1).astype(BF16) for p in parts] + [dgl0, dgl1, dgl2, jnp.transpose(dlg, (0, 2, 1)).reshape(n, -1).astype(BF16)]
    used = P_GATE + 3 * d + N_HEADS_FOX
    dproj = jnp.concatenate(parts + [jnp.zeros((n, dp - used), BF16)], axis=1)
    gw["w_in"] = _mm(s["h"], dproj, ta=True, name=f"mm_dw_in{l}")
    dh = _mm(dproj, w["w_in"][l], tb=True, name=f"mm_dh{l}", cargo=cargo("mm_dh"))
    dx, gs["attn_norm"] = _rmsnorm_bwd(s["x"], small["attn_norm"][l].reshape(1, d), dh, dx1, name=f"norm_attn_bwd{l}")
    return dx, gw, gs


def kernel(x, positions, attn_norm, w_in, b_forget, q_norm_fox, k_norm_fox, q_norm_dil, k_norm_dil, w_up_fox, w_up_sb, w_up_dil, w_out, mlp_norm, w_mlp_in, w_mlp_out, loss_target, m_attn_norm, m_w_in, m_b_forget, m_q_norm_fox, m_k_norm_fox, m_q_norm_dil, m_k_norm_dil, m_w_up_fox, m_w_up_sb, m_w_up_dil, m_w_out, m_mlp_norm, m_w_mlp_in, m_w_mlp_out, v_attn_norm, v_w_in, v_b_forget, v_q_norm_fox, v_k_norm_fox, v_q_norm_dil, v_k_norm_dil, v_w_up_fox, v_w_up_sb, v_w_up_dil, v_w_out, v_mlp_norm, v_w_mlp_in, v_w_mlp_out):
    bl, t, d = x.shape
    n = bl * t
    depth = attn_norm.shape[0]
    wl = dict(w_in=w_in, w_up_fox=w_up_fox, w_up_sb=w_up_sb, w_up_dil=w_up_dil, w_out=w_out, w_mlp_in=w_mlp_in, w_mlp_out=w_mlp_out)
    ml = dict(w_in=m_w_in, w_up_fox=m_w_up_fox, w_up_sb=m_w_up_sb, w_up_dil=m_w_up_dil, w_out=m_w_out, w_mlp_in=m_w_mlp_in, w_mlp_out=m_w_mlp_out)
    vl = dict(w_in=v_w_in, w_up_fox=v_w_up_fox, w_up_sb=v_w_up_sb, w_up_dil=v_w_up_dil, w_out=v_w_out, w_mlp_in=v_w_mlp_in, w_mlp_out=v_w_mlp_out)
    small = dict(attn_norm=attn_norm, b_forget=b_forget, q_norm_fox=q_norm_fox, k_norm_fox=k_norm_fox, q_norm_dil=q_norm_dil,
                 k_norm_dil=k_norm_dil, mlp_norm=mlp_norm)
    m_small = dict(attn_norm=m_attn_norm, b_forget=m_b_forget, q_norm_fox=m_q_norm_fox, k_norm_fox=m_k_norm_fox,
                   q_norm_dil=m_q_norm_dil, k_norm_dil=m_k_norm_dil, mlp_norm=m_mlp_norm)
    v_small = dict(attn_norm=v_attn_norm, b_forget=v_b_forget, q_norm_fox=v_q_norm_fox, k_norm_fox=v_k_norm_fox,
                   q_norm_dil=v_q_norm_dil, k_norm_dil=v_k_norm_dil, mlp_norm=v_mlp_norm)

    d_in = w_in.shape[-1] * N_DEV
    dp = -(-d_in // 512) * 512
    rest = [k for k in BIG if k != "w_in"]
    w = {k: [None] * depth for k in BIG}

    def gather(items):
        def done(res):
            for (k, l), sh in zip(items, res):
                whole = _join_shards(k, sh)
                w[k][l] = _permute_in(whole, dp) if k == "w_in" else whole

        return _gather_cargo([wl[k][l].astype(BF16) for k, l in items], done)

    _exchange(gather([("w_in", 0)]), name="gather_first")
    cos, sin = _rope_tables(positions)
    ropes = [(_to_streams(cos, r), _to_streams(sin, r)) for _, r in DIL_PATTERNS]

    h = x.reshape(n, d)
    saved = []
    for l in range(depth):
        cargo = {"fox_fwd": gather([(k, l) for k in rest])}
        if l + 1 < depth:
            cargo["sb_fwd"] = gather([("w_in", l + 1)])
        h, s = _layer_fwd(l, h, w, small, ropes, bl, t, cargo)
        saved.append(s)
    dy, loss_part = _loss_head(h, loss_target.reshape(n, d), name="loss_head")

    recv = {}

    def scatter(names, l, grads):
        def done(res):
            recv.update(zip(names, res))

        slots = [_split_shards(k, _unpermute_in(grads[k], d_in) if k == "w_in" else grads[k]).astype(BF16) for k in names]
        return _scatter_cargo(slots, [recv.get(k) for k in names], l, depth, done)

    gss = [None] * depth
    for l in reversed(range(depth)):
        hooks = {"fox_bwd": lambda gw, l=l: scatter(rest, l, gw), "mm_dh": lambda gw, l=l: scatter(["w_in"], l, gw)}
        dy, _, gss[l] = _layer_bwd(l, dy, saved[l], w, small, ropes, bl, t, hooks)
    grad_x = dy.reshape(bl, t, d)

    g_big, d_big, m_big, v_big = {}, {}, {}, {}
    for k in BIG:
        g_big[k], d_big[k], m_big[k], v_big[k] = _adamw(recv[k], wl[k], ml[k], vl[k], name=f"adamw_{k}")

    rows = [loss_part]
    for l in range(depth):
        gs = gss[l]
        rows += [gs["attn_norm"].reshape(-1, LANES), gs["mlp_norm"].reshape(-1, LANES), gs["fox_gains"], gs["dil_gains"], gs["b_forget"]]
    row_counts = [r.shape[0] for r in rows]
    part = jnp.concatenate(rows, axis=0)
    pad_rows = -(-part.shape[0] // 8) * 8 - part.shape[0]
    summed = _allreduce_small(jnp.pad(part, ((0, pad_rows), (0, 0))), name="allreduce_small")
    pieces, off = [], 0
    for c in row_counts:
        pieces.append(summed[off:off + c])
        off += c
    loss = pieces[0][0, 0]

    def fold(row):
        return row[:HEAD_DIM] + row[HEAD_DIM:]

    g_small = {k: [] for k in SMALL}
    for l in range(depth):
        an, mn, fg, dg, bf = pieces[1 + 5 * l:6 + 5 * l]
        g_small["attn_norm"].append(an.reshape(d))
        g_small["mlp_norm"].append(mn.reshape(d))
        g_small["q_norm_fox"].append(fold(fg[0]))
        g_small["k_norm_fox"].append(fold(fg[1]))
        g_small["q_norm_dil"].append(fold(dg[0]) + fold(dg[8]) + fold(dg[16]))
        g_small["k_norm_dil"].append(fold(dg[1]) + fold(dg[9]) + fold(dg[17]))
        g_small["b_forget"].append(bf[:, 0])
    g_small = {k: jnp.stack(vs) for k, vs in g_small.items()}
    small_shapes = [small[k].shape for k in SMALL]
    outs = _adamw(_pack([g_small[k] for k in SMALL], F32)[None, None], _pack([small[k] for k in SMALL], F32)[None],
                  _pack([m_small[k] for k in SMALL], F32)[None], _pack([v_small[k] for k in SMALL], F32)[None], name="adamw_small")
    g_sm, d_sm, m_sm, v_sm = (dict(zip(SMALL, _unpack(o, small_shapes))) for o in outs)

    order = ("attn_norm", "w_in", "b_forget", "q_norm_fox", "k_norm_fox", "q_norm_dil", "k_norm_dil", "w_up_fox", "w_up_sb",
             "w_up_dil", "w_out", "mlp_norm", "w_mlp_in", "w_mlp_out")
    res = [loss, grad_x]
    for big, sm in ((g_big, g_sm), (d_big, d_sm), (m_big, m_sm), (v_big, v_sm)):
        res += [big[k] if k in big else sm[k] for k in order]
    return tuple(res)
```

```python
import jax
import jax.numpy as jnp
from jax import lax
from jax.experimental import pallas as pl
from jax.experimental.pallas import tpu as pltpu

F32 = jnp.float32
BF16 = jnp.bfloat16

HEAD_DIM = 64
LANES = 128
N_HEADS_FOX = 8
N_HEADS_SB = 8
N_HEADS_DIL = 4
DIL_PATTERNS = ((128, 1), (512, 4), (2048, 16))
ROPE_THETA = 10000.0
EPS = 1e-6
SCALE = 0.125
W_FOX = N_HEADS_FOX * HEAD_DIM
W_SB = N_HEADS_SB * HEAD_DIM
W_DIL = N_HEADS_DIL * HEAD_DIM
W_DILQ = len(DIL_PATTERNS) * W_DIL
P_FOX = 0
P_SB = 3 * W_FOX
P_DIL = P_SB + 3 * W_SB
P_GATE = P_DIL + 3 * W_DILQ
N_DEV = 8
ATT_BLK = 256
ATT_BQ = 512
NEG = -1e30
VMEM_LIMIT = 56 * 1024 * 1024
ADAMW_BLOCK_ELEMS = 128 * 1024

ADAM_LR = 0.001
ADAM_B1 = 0.9
ADAM_B2 = 0.999
ADAM_EPS = 1e-08
ADAM_WD = 0.01
ADAM_STEP = 10

NT = (((1,), (1,)), ((), ()))
MESH = pl.DeviceIdType.MESH


def _pcall(body, **kw):
    return pl.pallas_call(body, **kw)


def _params(sem=None):
    return pltpu.CompilerParams(dimension_semantics=sem, vmem_limit_bytes=VMEM_LIMIT)


class _Cargo:
    def __init__(self, ins, out_shape, sems, start, finish, on_done, aliases=None):
        self.ins, self.out_shape, self.sems = list(ins), list(out_shape), list(sems)
        self.start, self.finish, self.on_done, self.aliases = start, finish, on_done, dict(aliases or {})


def _call(body, *, cargo=None, name, grid=(), in_specs, out_specs, out_shape, scratch_shapes=(), compiler_params=None):
    if cargo is None:
        kw = dict(grid=grid) if grid else {}
        if compiler_params is not None:
            kw["compiler_params"] = compiler_params
        return _pcall(body, name=name, in_specs=in_specs, out_specs=out_specs, out_shape=out_shape,
                      scratch_shapes=list(scratch_shapes), **kw)
    single = not isinstance(out_shape, (list, tuple))
    o_specs, o_shape = ([out_specs], [out_shape]) if single else (list(out_specs), list(out_shape))
    n_in, n_out, n_scr = len(in_specs), len(o_shape), len(scratch_shapes)
    c_in, c_out = len(cargo.ins), len(cargo.out_shape)

    def wrapped(*refs):
        ins, cins = refs[:n_in], refs[n_in:n_in + c_in]
        o0 = n_in + c_in
        outs, couts = refs[o0:o0 + n_out], refs[o0 + n_out:o0 + n_out + c_out]
        s0 = o0 + n_out + c_out
        scr, sems = refs[s0:s0 + n_scr], refs[s0 + n_scr:]
        first = last = None
        for ax, size in enumerate(grid):
            pid = pl.program_id(ax)
            first = (pid == 0) if first is None else first & (pid == 0)
            last = (pid == size - 1) if last is None else last & (pid == size - 1)
        if first is None:
            cargo.start(cins, couts, sems)
            body(*ins, *outs, *scr)
            cargo.finish(cins, couts, sems)
            return

        @pl.when(first)
        def _():
            cargo.start(cins, couts, sems)

        body(*ins, *outs, *scr)

        @pl.when(last)
        def _():
            cargo.finish(cins, couts, sems)

    hbm = pl.BlockSpec(memory_space=pl.ANY)
    kw = dict(grid=grid, compiler_params=_params(("arbitrary",) * len(grid))) if grid else {}
    call = _pcall(
        wrapped, name=name, in_specs=list(in_specs) + [hbm] * c_in, out_specs=o_specs + [hbm] * c_out,
        out_shape=o_shape + cargo.out_shape, scratch_shapes=list(scratch_shapes) + cargo.sems,
        input_output_aliases={n_in + i: n_out + j for i, j in cargo.aliases.items()}, **kw)

    def run(*args):
        res = call(*args, *cargo.ins)
        cargo.on_done(list(res[n_out:]))
        return res[0] if single else list(res[:n_out])

    return run


def _tile(dim, target, mult=LANES):
    t = (min(dim, target) // mult) * mult
    while t >= mult:
        if dim % t == 0:
            return t
        t -= mult
    return dim


def _mm(a, b, *, ta=False, tb=False, add=None, out_dtype=F32, name, tm=1024, tn=512, tk=1024, cargo=None):
    m, k = (a.shape[1], a.shape[0]) if ta else a.shape
    n = b.shape[0] if tb else b.shape[1]
    tm, tn, tk = _tile(m, tm), _tile(n, tn), _tile(k, tk)
    nk = k // tk
    dn = (((0,) if ta else (1,), (1,) if tb else (0,)), ((), ()))

    def body(*refs):
        if add is None:
            a_ref, b_ref, o_ref, acc = refs
        else:
            a_ref, b_ref, add_ref, o_ref, acc = refs
        kk = pl.program_id(2)
        part = lax.dot_general(a_ref[...].astype(BF16), b_ref[...].astype(BF16), dn, preferred_element_type=F32)

        @pl.when(kk == 0)
        def _():
            acc[...] = part

        @pl.when(kk > 0)
        def _():
            acc[...] += part

        @pl.when(kk == nk - 1)
        def _():
            r = acc[...]
            if add is not None:
                r = r + add_ref[...]
            o_ref[...] = r.astype(out_dtype)

    a_spec = pl.BlockSpec((tk, tm), lambda i, j, q: (q, i)) if ta else pl.BlockSpec((tm, tk), lambda i, j, q: (i, q))
    b_spec = pl.BlockSpec((tn, tk), lambda i, j, q: (j, q)) if tb else pl.BlockSpec((tk, tn), lambda i, j, q: (q, j))
    o_spec = pl.BlockSpec((tm, tn), lambda i, j, q: (i, j))
    ins, specs = [a, b], [a_spec, b_spec]
    if add is not None:
        ins.append(add)
        specs.append(o_spec)
    return _call(
        body, cargo=cargo, name=name, grid=(m // tm, n // tn, nk), in_specs=specs, out_specs=o_spec,
        out_shape=jax.ShapeDtypeStruct((m, n), out_dtype), scratch_shapes=[pltpu.VMEM((tm, tn), F32)],
        compiler_params=_params(("parallel", "parallel", "arbitrary")),
    )(*ins)


def _rmsnorm_fwd(x, g, *, name):
    n, d = x.shape
    tm = _tile(n, 256, 8)

    def body(x_ref, g_ref, h_ref):
        xv = x_ref[...]
        inv = lax.rsqrt(jnp.mean(xv * xv, axis=1, keepdims=True) + EPS)
        h_ref[...] = (xv * inv * g_ref[...]).astype(BF16)

    row = pl.BlockSpec((tm, d), lambda i: (i, 0))
    return _pcall(
        body, name=name, grid=(n // tm,), in_specs=[row, pl.BlockSpec((1, d), lambda i: (0, 0))], out_specs=row,
        out_shape=jax.ShapeDtypeStruct((n, d), BF16), compiler_params=_params(("parallel",)),
    )(x, g)


def _rmsnorm_bwd(x, g, dh, dres, *, name):
    n, d = x.shape
    tm = _tile(n, 256, 8)

    def body(x_ref, g_ref, dh_ref, dres_ref, dx_ref, dg_ref):
        @pl.when(pl.program_id(0) == 0)
        def _():
            dg_ref[...] = jnp.zeros_like(dg_ref)

        xv = x_ref[...]
        inv = lax.rsqrt(jnp.mean(xv * xv, axis=1, keepdims=True) + EPS)
        y = xv * inv
        dhv = dh_ref[...]
        dg_ref[...] += jnp.sum(dhv * y, axis=0, keepdims=True)
        dy = dhv * g_ref[...]
        dx_ref[...] = dres_ref[...] + inv * (dy - y * jnp.mean(dy * y, axis=1, keepdims=True))

    row = pl.BlockSpec((tm, d), lambda i: (i, 0))
    vec = pl.BlockSpec((1, d), lambda i: (0, 0))
    return _pcall(
        body, name=name, grid=(n // tm,), in_specs=[row, vec, row, row], out_specs=[row, vec],
        out_shape=[jax.ShapeDtypeStruct((n, d), F32), jax.ShapeDtypeStruct((1, d), F32)],
        compiler_params=_params(("arbitrary",)),
    )(x, g, dh, dres)


def _gate_specs(n, d):
    bw = 256 if d % 256 == 0 else LANES
    tm = _tile(n, 512, 8)
    nb = d // bw
    yspec = pl.BlockSpec((tm, bw), lambda i, j: (i, j))
    gspecs = [pl.BlockSpec((tm, bw), lambda i, j, b=b: (i, P_GATE // bw + b * nb + j)) for b in range(3)]
    return tm, bw, nb, yspec, gspecs


def _gate_merge_fwd(proj, ys, *, name):
    n, d = ys[0].shape
    tm, bw, nb, yspec, gspecs = _gate_specs(n, d)

    def body(g0, g1, g2, y0, y1, y2, o_ref):
        acc = jax.nn.sigmoid(g0[...]) * y0[...]
        acc += jax.nn.sigmoid(g1[...]) * y1[...]
        acc += jax.nn.sigmoid(g2[...]) * y2[...]
        o_ref[...] = acc.astype(BF16)

    return _pcall(
        body, name=name, grid=(n // tm, nb), in_specs=gspecs + [yspec] * 3, out_specs=yspec,
        out_shape=jax.ShapeDtypeStruct((n, d), BF16), compiler_params=_params(("parallel", "parallel")),
    )(proj, proj, proj, *ys)


def _gate_merge_bwd(proj, ys, dmerged, *, name):
    n, d = ys[0].shape
    tm, bw, nb, yspec, gspecs = _gate_specs(n, d)

    def body(g0, g1, g2, y0, y1, y2, dm_ref, dy0, dy1, dy2, dgl0, dgl1, dgl2):
        dm = dm_ref[...]
        for g_ref, y_ref, dy_ref, dgl_ref in ((g0, y0, dy0, dgl0), (g1, y1, dy1, dgl1), (g2, y2, dy2, dgl2)):
            s = jax.nn.sigmoid(g_ref[...])
            dy_ref[...] = (dm * s).astype(BF16)
            dgl_ref[...] = (dm * y_ref[...] * s * (1.0 - s)).astype(BF16)

    sds = jax.ShapeDtypeStruct((n, d), BF16)
    return _pcall(
        body, name=name, grid=(n // tm, nb), in_specs=gspecs + [yspec] * 4, out_specs=[yspec] * 6,
        out_shape=[sds] * 6, compiler_params=_params(("parallel", "parallel")),
    )(proj, proj, proj, *ys, dmerged)


def _relu2_fwd(u, *, name):
    n, f = u.shape
    tm, tf = _tile(n, 512, 8), _tile(f, 1024)

    def body(u_ref, a_ref):
        r = jnp.maximum(u_ref[...], 0.0)
        a_ref[...] = (r * r).astype(BF16)

    spec = pl.BlockSpec((tm, tf), lambda i, j: (i, j))
    return _pcall(
        body, name=name, grid=(n // tm, f // tf), in_specs=[spec], out_specs=spec,
        out_shape=jax.ShapeDtypeStruct((n, f), BF16), compiler_params=_params(("parallel", "parallel")),
    )(u)


def _relu2_bwd(da, u, *, name):
    n, f = u.shape
    tm, tf = _tile(n, 512, 8), _tile(f, 1024)

    def body(da_ref, u_ref, du_ref):
        du_ref[...] = (da_ref[...] * (2.0 * jnp.maximum(u_ref[...], 0.0))).astype(BF16)

    spec = pl.BlockSpec((tm, tf), lambda i, j: (i, j))
    return _pcall(
        body, name=name, grid=(n // tm, f // tf), in_specs=[spec, spec], out_specs=spec,
        out_shape=jax.ShapeDtypeStruct((n, f), BF16), compiler_params=_params(("parallel", "parallel")),
    )(da, u)


def _loss_head(y, tgt, *, name):
    n, d = y.shape
    tm = _tile(n, 256, 8)
    steps = n // tm

    def body(y_ref, t_ref, dy_ref, loss_ref, acc):
        i = pl.program_id(0)

        @pl.when(i == 0)
        def _():
            acc[...] = jnp.zeros_like(acc)

        e = y_ref[...] - t_ref[...]
        dy_ref[...] = e * (1.0 / d)
        acc[...] += jnp.sum(e * e, axis=0, keepdims=True)

        @pl.when(i == steps - 1)
        def _():
            tot = jnp.sum(acc[...], axis=1, keepdims=True) * (0.5 / d)
            loss_ref[...] = jnp.broadcast_to(tot, loss_ref.shape)

    row = pl.BlockSpec((tm, d), lambda i: (i, 0))
    return _pcall(
        body, name=name, grid=(steps,), in_specs=[row, row], out_specs=[row, pl.BlockSpec((8, LANES), lambda i: (0, 0))],
        out_shape=[jax.ShapeDtypeStruct((n, d), F32), jax.ShapeDtypeStruct((8, LANES), F32)],
        scratch_shapes=[pltpu.VMEM((1, d), F32)], compiler_params=_params(("arbitrary",)),
    )(y, tgt)


def _dil_weights(lses, *, name):
    shp = lses[0].shape

    def body(l0, l1, l2, lse_ref, w0, w1, w2):
        a, b, c = l0[...], l1[...], l2[...]
        m = jnp.maximum(jnp.maximum(a, b), c)
        ea, eb, ec = jnp.exp(a - m), jnp.exp(b - m), jnp.exp(c - m)
        den = ea + eb + ec
        lse_ref[...] = m + jnp.log(den)
        w0[...] = ea / den
        w1[...] = eb / den
        w2[...] = ec / den

    vmem = pl.BlockSpec(memory_space=pltpu.VMEM)
    return _pcall(body, name=name, in_specs=[vmem] * 3, out_specs=[vmem] * 4, out_shape=[jax.ShapeDtypeStruct(shp, F32)] * 4)(*lses)


def _dil_mix(os_, ws, *, name):
    n, w = os_[0].shape
    tm = _tile(n, 512, 8)

    def body(o0, o1, o2, w0, w1, w2, out_ref):
        out_ref[...] = w0[...] * o0[...] + w1[...] * o1[...] + w2[...] * o2[...]

    spec = pl.BlockSpec((tm, w), lambda i: (i, 0))
    return _pcall(
        body, name=name, grid=(n // tm,), in_specs=[spec] * 6, out_specs=spec, out_shape=jax.ShapeDtypeStruct((n, w), F32),
        compiler_params=_params(("parallel",)),
    )(*os_, *ws)


def _adamw(gsrc, w, m, v, *, name):
    s, dep, a, b = gsrc.shape
    ta = _tile(a, max(16, (ADAMW_BLOCK_ELEMS // b) // 16 * 16), 16)
    c1 = 1.0 / (1.0 - ADAM_B1 ** ADAM_STEP)
    c2 = 1.0 / (1.0 - ADAM_B2 ** ADAM_STEP)

    def body(gs_ref, w_ref, m_ref, v_ref, g_ref, d_ref, m2_ref, v2_ref):
        g = gs_ref[0].astype(F32)
        for i in range(1, s):
            g = g + gs_ref[i].astype(F32)
        m2 = ADAM_B1 * m_ref[...] + (1.0 - ADAM_B1) * g
        v2 = ADAM_B2 * v_ref[...] + (1.0 - ADAM_B2) * (g * g)
        g_ref[...] = g
        m2_ref[...] = m2
        v2_ref[...] = v2
        d_ref[...] = -ADAM_LR * ((m2 * c1) / (jnp.sqrt(v2 * c2) + ADAM_EPS) + ADAM_WD * w_ref[...])

    spec = pl.BlockSpec((None, ta, b), lambda l, i: (l, i, 0))
    sds = jax.ShapeDtypeStruct((dep, a, b), F32)
    return _pcall(
        body, name=name, grid=(dep, a // ta),
        in_specs=[pl.BlockSpec((s, None, ta, b), lambda l, i: (0, l, i, 0)), spec, spec, spec],
        out_specs=[spec] * 4, out_shape=[sds] * 4, compiler_params=_params(("parallel", "parallel")),
    )(gsrc, w, m, v)


def _mask_a():
    return lax.broadcasted_iota(jnp.int32, (1, LANES), 1) < HEAD_DIM


def _half_sum(x, m_a):
    sa = jnp.sum(jnp.where(m_a, x, 0.0), axis=1, keepdims=True)
    sb = jnp.sum(jnp.where(m_a, 0.0, x), axis=1, keepdims=True)
    return jnp.where(m_a, sa, sb)


def _head_inv(x, m_a):
    return lax.rsqrt(_half_sum(x * x, m_a) * (1.0 / HEAD_DIM) + EPS)


def _swap32(x):
    first = (lax.broadcasted_iota(jnp.int32, (1, LANES), 1) % HEAD_DIM) < (HEAD_DIM // 2)
    return jnp.where(first, pltpu.roll(x, LANES - HEAD_DIM // 2, 1), pltpu.roll(x, HEAD_DIM // 2, 1))


def _tri(blk, rel):
    r = lax.broadcasted_iota(jnp.int32, (blk, blk), 0)
    c = lax.broadcasted_iota(jnp.int32, (blk, blk), 1)
    return jnp.where(rel(r, c), 1.0, 0.0).astype(BF16)


def _cumdot(x, u, parts):
    acc = None
    r = x
    for i in range(parts):
        xi = r.astype(BF16)
        t = jnp.dot(xi, u, preferred_element_type=F32)
        acc = t if acc is None else acc + t
        if i + 1 < parts:
            r = r - xi.astype(F32)
    return acc


def _rows(i, blk):
    return pl.ds(pl.multiple_of(i * blk, blk), blk)


def _col_spec(n, off):
    return pl.BlockSpec((None, n, LANES), lambda z, p, off=off: (z, 0, off + p))


def _att_blk(n):
    return ATT_BLK if n % ATT_BLK == 0 else min(LANES, n)


def _att_blocks(n):
    bk = _att_blk(n)
    return (ATT_BQ if n % ATT_BQ == 0 else bk), bk


def _loop(lo, hi, fn):
    def it(i, c):
        fn(i)
        return c

    lax.fori_loop(lo, hi, it, 0)


def _normed(src, g_ref, rope_refs, rows, m_a):
    xv = src[rows, :]
    xn = xv * _head_inv(xv, m_a) * g_ref[...]
    if rope_refs is not None:
        xn = xn * rope_refs[0][rows, :] + _swap32(xn) * rope_refs[1][rows, :]
    return xn


def _bias_lane(h):
    return HEAD_DIM if h == 0 else 0


def _k_for_head(kn, kb_row, h, m_h, lane, blk):
    out = jnp.where(m_h, kn, 0.0)
    if kb_row is not None:
        col = jnp.transpose(jnp.broadcast_to(kb_row, (LANES, blk)))
        hi = col.astype(BF16).astype(F32)
        mid = (col - hi).astype(BF16).astype(F32)
        lo = col - hi - mid
        b = _bias_lane(h)
        out = jnp.where(lane == b, hi, jnp.where(lane == b + 1, mid, jnp.where(lane == b + 2, lo, out)))
    return out.astype(BF16)


def _q_for_head(qb, h, m_h, lane, biased):
    out = jnp.where(m_h, qb, 0)
    if biased:
        b = _bias_lane(h)
        out = jnp.where((lane >= b) & (lane < b + 3), jnp.ones_like(out), out)
    return out


def _head_rows(x, parts=3):
    rr = lax.broadcasted_iota(jnp.int32, (8, LANES), 0)
    ll = lax.broadcasted_iota(jnp.int32, (8, LANES), 1)
    sel = jnp.where(((rr == 0) & (ll < HEAD_DIM)) | ((rr == 1) & (ll >= HEAD_DIM)), 1.0, 0.0).astype(BF16)
    acc = None
    rem = x
    for i in range(parts):
        xi = rem.astype(BF16)
        t = lax.dot_general(sel, xi, NT, preferred_element_type=F32)
        acc = t if acc is None else acc + t
        if i + 1 < parts:
            rem = rem - xi.astype(F32)
    return acc


def _cumdot_left(u, x, parts):
    acc = None
    rem = x
    for i in range(parts):
        xi = rem.astype(BF16)
        t = jnp.dot(u, xi, preferred_element_type=F32)
        acc = t if acc is None else acc + t
        if i + 1 < parts:
            rem = rem - xi.astype(F32)
    return acc


def _q_minus_k(bk, bq):
    return lax.broadcasted_iota(jnp.int32, (bk, bq), 1) - lax.broadcasted_iota(jnp.int32, (bk, bq), 0)


def _stat_spec(nb, blk):
    return pl.BlockSpec((None, 2, nb, 1, blk), lambda z, p: (z, p, 0, 0, 0))


def _attn_fwd(src, offs, npairs, gq, gk, *, rope=None, kbias=None, window, name, cargo=None):
    zs, n, _ = src.shape
    bq, bk = _att_blocks(n)
    nq, nk, rq = n // bq, n // bk, bq // bk
    full = window >= n
    wblk = -(-window // bk)
    biased = kbias is not None

    def body(*refs):
        it = iter(refs)
        q_ref, k_ref, v_ref, gq_ref, gk_ref = (next(it) for _ in range(5))
        rope_refs = (next(it), next(it)) if rope is not None else None
        kb_ref = next(it) if biased else None
        o_ref, lse_ref, qn_s, kh_s, vt_s, acc_s, m_s = (next(it) for _ in range(7))
        m_a = _mask_a()
        masks = (m_a, jnp.logical_not(m_a))
        lane = lax.broadcasted_iota(jnp.int32, (1, LANES), 1)
        row = lax.broadcasted_iota(jnp.int32, (LANES, 1), 0)

        def prep(c):
            rows = _rows(c, bk)
            qn_s[rows, :] = (_normed(q_ref, gq_ref, rope_refs, rows, m_a) * SCALE).astype(BF16)
            kn = _normed(k_ref, gk_ref, rope_refs, rows, m_a)
            vt = jnp.transpose(v_ref[rows, :])
            for h in (0, 1):
                kh_s[h, rows, :] = _k_for_head(kn, kb_ref[h, c] if biased else None, h, masks[h], lane, bk)
                vt_s[h, c] = jnp.where(row == _bias_lane(h), 1.0, vt).astype(BF16)

        _loop(0, nk, prep)
        qk = _q_minus_k(bk, bq)

        def qblock(qi):
            rows = _rows(qi, bq)
            qb = qn_s[rows, :]
            qh = [_q_for_head(qb, h, masks[h], lane, biased) for h in (0, 1)]
            m_s[...] = jnp.full(m_s.shape, NEG, F32)
            acc_s[...] = jnp.zeros_like(acc_s)

            def step(kj, masked):
                cols = _rows(kj, bk)
                sts = [lax.dot_general(kh_s[h, cols, :], qh[h], NT, preferred_element_type=F32) for h in (0, 1)]
                old = [(m_s[h], acc_s[h]) for h in (0, 1)]
                if masked:
                    d = qk + (qi * bq - kj * bk)
                    ok = (d >= 0) & (d <= window)
                    sts = [jnp.where(ok, st, NEG) for st in sts]
                new = []
                for h in (0, 1):
                    m, acc = old[h]
                    m2 = jnp.maximum(m, jnp.max(sts[h], axis=0, keepdims=True))
                    pt = jnp.exp(sts[h] - m2).astype(BF16)
                    new.append((m2, jnp.exp(m - m2) * acc + jnp.dot(vt_s[h, kj], pt, preferred_element_type=F32)))
                for h in (0, 1):
                    m_s[h], acc_s[h] = new[h]

            if full:
                _loop(0, qi * rq, lambda kj: step(kj, False))
                _loop(qi * rq, (qi + 1) * rq, lambda kj: step(kj, True))
            else:
                _loop(jnp.maximum(qi * rq - wblk, 0), (qi + 1) * rq, lambda kj: step(kj, True))
            outs = []
            for h in (0, 1):
                acc_t = acc_s[h]
                den = acc_t[_bias_lane(h):_bias_lane(h) + 1, :]
                outs.append(jnp.transpose(acc_t / den))
                lse_ref[h, qi] = m_s[h] + jnp.log(den)
            o_ref[rows, :] = jnp.where(m_a, outs[0], outs[1])

        _loop(0, nq, qblock)

    vec = pl.BlockSpec((1, LANES), lambda z, p: (0, 0))
    ins = [src, src, src, gq, gk]
    specs = [_col_spec(n, offs[0]), _col_spec(n, offs[1]), _col_spec(n, offs[2]), vec, vec]
    if rope is not None:
        ins += list(rope)
        specs += [pl.BlockSpec((None, n, LANES), lambda z, p: (z, 0, 0))] * 2
    if biased:
        ins.append(kbias)
        specs.append(_stat_spec(nk, bk))
    scratch = [pltpu.VMEM((n, LANES), BF16), pltpu.VMEM((2, n, LANES), BF16), pltpu.VMEM((2, nk, LANES, bk), BF16)]
    scratch += [pltpu.VMEM((2, LANES, bq), F32), pltpu.VMEM((2, 1, bq), F32)]
    return _call(
        body, cargo=cargo, name=name, grid=(zs, npairs), in_specs=specs, out_specs=[_col_spec(n, 0), _stat_spec(nq, bq)],
        out_shape=[jax.ShapeDtypeStruct((zs, n, LANES * npairs), F32), jax.ShapeDtypeStruct((zs, 2 * npairs, nq, 1, bq), F32)],
        scratch_shapes=scratch, compiler_params=_params(("parallel", "parallel")),
    )(*ins)


def _attn_bwd(src, offs, npairs, gq, gk, o, do, lse, *, rope=None, kbias=None, window, name, cargo=None):
    zs, n, _ = src.shape
    bq, bk = _att_blocks(n)
    nq, nk, rq = n // bq, n // bk, bq // bk
    full = window >= n
    wblk = -(-window // bk)
    biased = kbias is not None

    def body(*refs):
        it = iter(refs)
        q_ref, k_ref, v_ref, gq_ref, gk_ref, o_ref, do_ref, lse_ref = (next(it) for _ in range(8))
        rope_refs = (next(it), next(it)) if rope is not None else None
        kb_ref = next(it) if biased else None
        dq_ref, dk_ref, dv_ref, dg_ref = (next(it) for _ in range(4))
        dkb_ref = next(it) if biased else None
        qn_s, kh_s, vb_s, kt_s, dqn_s, dkh_s, dq_s, rs_s = (next(it) for _ in range(8))
        m_a = _mask_a()
        masks = (m_a, jnp.logical_not(m_a))
        lane = lax.broadcasted_iota(jnp.int32, (1, LANES), 1)

        @pl.when((pl.program_id(0) == 0) & (pl.program_id(1) == 0))
        def _():
            dg_ref[...] = jnp.zeros_like(dg_ref)

        def prep(c):
            rows = _rows(c, bk)
            qn_s[rows, :] = (_normed(q_ref, gq_ref, rope_refs, rows, m_a) * SCALE).astype(BF16)
            kn = _normed(k_ref, gk_ref, rope_refs, rows, m_a)
            kt_s[c] = jnp.transpose(kn).astype(BF16)
            vb_s[rows, :] = v_ref[rows, :].astype(BF16)
            for h in (0, 1):
                kh_s[h, rows, :] = _k_for_head(kn, kb_ref[h, c] if biased else None, h, masks[h], lane, bk)

        _loop(0, nk, prep)
        dkh_s[...] = jnp.zeros_like(dkh_s)
        dv_ref[...] = jnp.zeros_like(dv_ref)
        qk = _q_minus_k(bk, bq)

        def qblock(qi):
            rows = _rows(qi, bq)
            qb = qn_s[rows, :]
            dob = do_ref[rows, :]
            delta = _head_rows(dob * o_ref[rows, :])
            qh = [_q_for_head(qb, h, masks[h], lane, biased) for h in (0, 1)]
            doms = [jnp.where(masks[h], dob, 0.0).astype(BF16) for h in (0, 1)]
            lses = [lse_ref[h, qi] for h in (0, 1)]
            dq_s[...] = jnp.zeros_like(dq_s)
            if biased:
                for h in (0, 1):
                    rs_s[h, qi] = jnp.zeros((1, bq), F32)

            def step(kj, masked):
                cols = _rows(kj, bk)
                vb = vb_s[cols, :]
                kt = kt_s[kj]
                sts = [lax.dot_general(kh_s[h, cols, :], qh[h], NT, preferred_element_type=F32) for h in (0, 1)]
                dpts = [lax.dot_general(vb, doms[h], NT, preferred_element_type=F32) for h in (0, 1)]
                if masked:
                    d = qk + (qi * bq - kj * bk)
                    ok = (d >= 0) & (d <= window)
                    sts = [jnp.where(ok, st, NEG) for st in sts]
                new = []
                for h in (0, 1):
                    pt = jnp.exp(sts[h] - lses[h])
                    dst = pt * (dpts[h] - delta[h:h + 1, :])
                    dsb = dst.astype(BF16)
                    tk = jnp.dot(dsb, qh[h], preferred_element_type=F32)
                    if biased:
                        tk = tk + jnp.dot((dst - dsb.astype(F32)).astype(BF16), qh[h], preferred_element_type=F32)
                    tv = jnp.dot(pt.astype(BF16), doms[h], preferred_element_type=F32)
                    tq = jnp.dot(kt, dsb, preferred_element_type=F32)
                    new.append((tk, tv, tq, jnp.sum(dst, axis=0, keepdims=True) if biased else None))
                for h in (0, 1):
                    dkh_s[h, cols, :] += new[h][0]
                    dq_s[h] += new[h][2]
                    if biased:
                        rs_s[h, qi] += new[h][3]
                dv_ref[cols, :] += new[0][1] + new[1][1]

            if full:
                _loop(0, qi * rq, lambda kj: step(kj, False))
                _loop(qi * rq, (qi + 1) * rq, lambda kj: step(kj, True))
            else:
                _loop(jnp.maximum(qi * rq - wblk, 0), (qi + 1) * rq, lambda kj: step(kj, True))
            dqn_s[rows, :] = jnp.where(m_a, jnp.transpose(dq_s[0]), jnp.transpose(dq_s[1])) * SCALE

        _loop(0, nq, qblock)

        def finish(c, carry):
            rows = _rows(c, bq)
            out = []
            dk_pair = [dkh_s[0, rows, :], dkh_s[1, rows, :]]
            if biased:
                for h in (0, 1):
                    b = _bias_lane(h)
                    dkb_row = jnp.transpose(dk_pair[h])[b:b + 1, :] - rs_s[h, c]
                    for j in range(rq):
                        dkb_ref[h, c * rq + j] = dkb_row[:, j * bk:(j + 1) * bk]
            grads = (dqn_s[rows, :], jnp.where(m_a, dk_pair[0], dk_pair[1]))
            for src_ref, g_ref, dxn, dst in ((q_ref, gq_ref, grads[0], dq_ref), (k_ref, gk_ref, grads[1], dk_ref)):
                xv = src_ref[rows, :]
                inv = _head_inv(xv, m_a)
                y = xv * inv
                if rope_refs is not None:
                    dxn = dxn * rope_refs[0][rows, :] + _swap32(dxn * rope_refs[1][rows, :])
                dy = dxn * g_ref[...]
                dst[rows, :] = inv * (dy - y * (_half_sum(dy * y, m_a) * (1.0 / HEAD_DIM)))
                out.append(jnp.sum(dxn * y, axis=0, keepdims=True))
            return carry[0] + out[0], carry[1] + out[1]

        zero = jnp.zeros((1, LANES), F32)
        dgq, dgk = lax.fori_loop(0, nq, finish, (zero, zero))
        dg_ref[0:1, :] += dgq
        dg_ref[1:2, :] += dgk

    vec = pl.BlockSpec((1, LANES), lambda z, p: (0, 0))
    ospec = _col_spec(n, 0)
    ins = [src, src, src, gq, gk, o, do, lse]
    specs = [_col_spec(n, offs[0]), _col_spec(n, offs[1]), _col_spec(n, offs[2]), vec, vec, ospec, ospec, _stat_spec(nq, bq)]
    if rope is not None:
        ins += list(rope)
        specs += [pl.BlockSpec((None, n, LANES), lambda z, p: (z, 0, 0))] * 2
    sds = jax.ShapeDtypeStruct((zs, n, LANES * npairs), F32)
    out_shape = [sds, sds, sds, jax.ShapeDtypeStruct((8, LANES), F32)]
    out_specs = [ospec, ospec, ospec, pl.BlockSpec((8, LANES), lambda z, p: (0, 0))]
    if biased:
        ins.append(kbias)
        specs.append(_stat_spec(nk, bk))
        out_shape.append(jax.ShapeDtypeStruct(kbias.shape, F32))
        out_specs.append(_stat_spec(nk, bk))
    scratch = [pltpu.VMEM((n, LANES), BF16), pltpu.VMEM((2, n, LANES), BF16), pltpu.VMEM((n, LANES), BF16)]
    scratch += [pltpu.VMEM((nk, LANES, bk), BF16), pltpu.VMEM((n, LANES), F32), pltpu.VMEM((2, n, LANES), F32)]
    scratch += [pltpu.VMEM((2, LANES, bq), F32), pltpu.VMEM((2, nq, 1, bq), F32)]
    return _call(
        body, cargo=cargo, name=name, grid=(zs, npairs), in_specs=specs, out_specs=out_specs, out_shape=out_shape,
        scratch_shapes=scratch, compiler_params=_params(("arbitrary", "arbitrary")),
    )(*ins)


def _log_sig_pair(z):
    lsn = -(jnp.maximum(z, 0.0) + jnp.log(1.0 + jnp.exp(-jnp.abs(z))))
    return lsn, z + lsn


def _sb_fwd(src, offs, npairs, *, name, cargo=None):
    zs, n, _ = src.shape
    bq, bk = _att_blocks(n)
    nq, nk, rq = n // bq, n // bk, bq // bk

    def body(q_ref, k_ref, v_ref, o_ref, lt_ref, qs_s, kb_s, vt_s, acc_s, c_s):
        m_a = _mask_a()
        masks = (m_a, jnp.logical_not(m_a))

        def prep(c):
            rows = _rows(c, bk)
            qs_s[rows, :] = (q_ref[rows, :] * SCALE).astype(BF16)
            kb_s[rows, :] = k_ref[rows, :].astype(BF16)
            vt_s[c] = jnp.transpose(v_ref[rows, :]).astype(BF16)

        _loop(0, nk, prep)
        qk = _q_minus_k(bk, bq)
        u_gt = _tri(bk, lambda r, c: c > r)

        def qblock(qi):
            rows = _rows(qi, bq)
            qb = qs_s[rows, :]
            qms = [jnp.where(masks[h], qb, 0) for h in (0, 1)]
            acc_s[...] = jnp.zeros_like(acc_s)
            c_s[...] = jnp.zeros_like(c_s)

            def step(kj, masked):
                kb, vt = kb_s[_rows(kj, bk), :], vt_s[kj]
                zts = [lax.dot_general(kb, qms[h], NT, preferred_element_type=F32) for h in (0, 1)]
                old = [c_s[h] for h in (0, 1)]
                if masked:
                    ok = (qk + (qi * bq - kj * bk)) > 0
                new = []
                for h in (0, 1):
                    lsn, lsp = _log_sig_pair(zts[h])
                    if masked:
                        lsn = jnp.where(ok, lsn, 0.0)
                    at = jnp.exp(lsp + (old[h] + _cumdot_left(u_gt, lsn, 3)))
                    if masked:
                        at = jnp.where(ok, at, 0.0)
                    new.append((jnp.dot(vt, at.astype(BF16), preferred_element_type=F32),
                                old[h] + jnp.sum(lsn, axis=0, keepdims=True)))
                for h in (0, 1):
                    acc_s[h] += new[h][0]
                    c_s[h] = new[h][1]

            _loop(0, rq, lambda t: step((qi + 1) * rq - 1 - t, True))
            _loop(0, qi * rq, lambda t: step(qi * rq - 1 - t, False))
            o_ref[rows, :] = jnp.where(m_a, jnp.transpose(acc_s[0]), jnp.transpose(acc_s[1]))
            for h in (0, 1):
                lt_ref[h, qi] = c_s[h]

        _loop(0, nq, qblock)

    scratch = [pltpu.VMEM((n, LANES), BF16)] * 2 + [pltpu.VMEM((nk, LANES, bk), BF16)]
    scratch += [pltpu.VMEM((2, LANES, bq), F32), pltpu.VMEM((2, 1, bq), F32)]
    return _call(
        body, cargo=cargo, name=name, grid=(zs, npairs),
        in_specs=[_col_spec(n, offs[0]), _col_spec(n, offs[1]), _col_spec(n, offs[2])],
        out_specs=[_col_spec(n, 0), _stat_spec(nq, bq)],
        out_shape=[jax.ShapeDtypeStruct((zs, n, LANES * npairs), F32), jax.ShapeDtypeStruct((zs, 2 * npairs, nq, 1, bq), F32)],
        scratch_shapes=scratch, compiler_params=_params(("parallel", "parallel")),
    )(src, src, src)


def _sb_bwd(src, offs, npairs, do, ltot, *, name, cargo=None):
    zs, n, _ = src.shape
    bq, bk = _att_blocks(n)
    nq, nk, rq = n // bq, n // bk, bq // bk

    def body(q_ref, k_ref, v_ref, do_ref, lt_ref, dq_ref, dk_ref, dv_ref, qs_s, kb_s, vb_s, kt_s, dq_s, lp_s, ep_s):
        m_a = _mask_a()
        masks = (m_a, jnp.logical_not(m_a))

        def prep(c):
            rows = _rows(c, bk)
            qs_s[rows, :] = (q_ref[rows, :] * SCALE).astype(BF16)
            kv = k_ref[rows, :]
            kb_s[rows, :] = kv.astype(BF16)
            kt_s[c] = jnp.transpose(kv).astype(BF16)
            vb_s[rows, :] = v_ref[rows, :].astype(BF16)

        _loop(0, nk, prep)
        dk_ref[...] = jnp.zeros_like(dk_ref)
        dv_ref[...] = jnp.zeros_like(dv_ref)
        qk = _q_minus_k(bk, bq)
        u_le = _tri(bk, lambda r, c: c <= r)
        u_lt = _tri(bk, lambda r, c: c < r)

        def qblock(qi):
            rows = _rows(qi, bq)
            qb = qs_s[rows, :]
            dob = do_ref[rows, :]
            qms = [jnp.where(masks[h], qb, 0) for h in (0, 1)]
            doms = [jnp.where(masks[h], dob, 0.0).astype(BF16) for h in (0, 1)]
            lts = [lt_ref[h, qi] for h in (0, 1)]
            dq_s[...] = jnp.zeros_like(dq_s)
            lp_s[...] = jnp.zeros_like(lp_s)
            ep_s[...] = jnp.zeros_like(ep_s)

            def step(kj, masked):
                cols = _rows(kj, bk)
                kb, vb, kt = kb_s[cols, :], vb_s[cols, :], kt_s[kj]
                zts = [lax.dot_general(kb, qms[h], NT, preferred_element_type=F32) for h in (0, 1)]
                dats = [lax.dot_general(vb, doms[h], NT, preferred_element_type=F32) for h in (0, 1)]
                old = [(lp_s[h], ep_s[h]) for h in (0, 1)]
                if masked:
                    ok = (qk + (qi * bq - kj * bk)) > 0
                new = []
                for h in (0, 1):
                    lp, ep = old[h]
                    zt = zts[h]
                    lsn, lsp = _log_sig_pair(zt)
                    if masked:
                        lsn = jnp.where(ok, lsn, 0.0)
                    at = jnp.exp(lsp + (lts[h] - (lp + _cumdot_left(u_le, lsn, 3))))
                    if masked:
                        at = jnp.where(ok, at, 0.0)
                    et = dats[h] * at
                    big_e = ep + _cumdot_left(u_lt, et, 2)
                    sig = jax.nn.sigmoid(zt)
                    dzt = et * (1.0 - sig) - big_e * sig
                    if masked:
                        dzt = jnp.where(ok, dzt, 0.0)
                    dzb = dzt.astype(BF16)
                    new.append((jnp.dot(dzb, qms[h], preferred_element_type=F32),
                                jnp.dot(at.astype(BF16), doms[h], preferred_element_type=F32),
                                jnp.dot(kt, dzb, preferred_element_type=F32),
                                lp + jnp.sum(lsn, axis=0, keepdims=True), ep + jnp.sum(et, axis=0, keepdims=True)))
                for h in (0, 1):
                    dq_s[h] += new[h][2]
                    lp_s[h], ep_s[h] = new[h][3], new[h][4]
                dk_ref[cols, :] += new[0][0] + new[1][0]
                dv_ref[cols, :] += new[0][1] + new[1][1]

            _loop(0, qi * rq, lambda kj: step(kj, False))
            _loop(qi * rq, (qi + 1) * rq, lambda kj: step(kj, True))
            dq_ref[rows, :] = jnp.where(m_a, jnp.transpose(dq_s[0]), jnp.transpose(dq_s[1])) * SCALE

        _loop(0, nq, qblock)

    ospec = _col_spec(n, 0)
    sds = jax.ShapeDtypeStruct((zs, n, LANES * npairs), F32)
    scratch = [pltpu.VMEM((n, LANES), BF16)] * 3 + [pltpu.VMEM((nk, LANES, bk), BF16)]
    scratch += [pltpu.VMEM((2, LANES, bq), F32), pltpu.VMEM((2, 1, bq), F32), pltpu.VMEM((2, 1, bq), F32)]
    return _call(
        body, cargo=cargo, name=name, grid=(zs, npairs),
        in_specs=[_col_spec(n, offs[0]), _col_spec(n, offs[1]), _col_spec(n, offs[2]), ospec, _stat_spec(nq, bq)],
        out_specs=[ospec] * 3, out_shape=[sds] * 3, scratch_shapes=scratch,
        compiler_params=_params(("parallel", "parallel")),
    )(src, src, src, do, ltot)


def _fox_gate_fwd(lg, bias, *, name):
    bs, nh, t = lg.shape
    blk = min(LANES, t)

    def body(lg_ref, b_ref, kb_ref):
        u_le = _tri(blk, lambda r, c: r <= c)
        carry = jnp.zeros((nh, 1), F32)
        for j in range(t // blk):
            sl = slice(j * blk, (j + 1) * blk)
            xv = lg_ref[:, sl] + b_ref[...]
            lf = jnp.minimum(xv, 0.0) - jnp.log(1.0 + jnp.exp(-jnp.abs(xv)))
            kb_ref[:, sl] = -(carry + _cumdot(lf, u_le, 3))
            carry = carry + jnp.sum(lf, axis=1, keepdims=True)

    spec = pl.BlockSpec((None, nh, t), lambda i: (i, 0, 0))
    return _pcall(
        body, name=name, grid=(bs,), in_specs=[spec, pl.BlockSpec((nh, 1), lambda i: (0, 0))], out_specs=spec,
        out_shape=jax.ShapeDtypeStruct((bs, nh, t), F32), compiler_params=_params(("parallel",)),
    )(lg, bias)


def _fox_gate_bwd(dkb, lg, bias, *, name):
    bs, nh, t = lg.shape
    blk = min(LANES, t)

    def body(dkb_ref, lg_ref, b_ref, dlg_ref, db_ref):
        @pl.when(pl.program_id(0) == 0)
        def _():
            db_ref[...] = jnp.zeros_like(db_ref)

        u_ge = _tri(blk, lambda r, c: r >= c)
        carry = jnp.zeros((nh, 1), F32)
        tot = jnp.zeros((nh, 1), F32)
        for j in reversed(range(t // blk)):
            sl = slice(j * blk, (j + 1) * blk)
            df = -dkb_ref[:, sl]
            dlf = carry + _cumdot(df, u_ge, 3)
            carry = carry + jnp.sum(df, axis=1, keepdims=True)
            xv = lg_ref[:, sl] + b_ref[...]
            dlg = dlf * jax.nn.sigmoid(-xv)
            dlg_ref[:, sl] = dlg
            tot = tot + jnp.sum(dlg, axis=1, keepdims=True)
        db_ref[...] += jnp.broadcast_to(tot, db_ref.shape)

    spec = pl.BlockSpec((None, nh, t), lambda i: (i, 0, 0))
    return _pcall(
        body, name=name, grid=(bs,), in_specs=[spec, spec, pl.BlockSpec((nh, 1), lambda i: (0, 0))],
        out_specs=[spec, pl.BlockSpec((nh, LANES), lambda i: (0, 0))],
        out_shape=[jax.ShapeDtypeStruct((bs, nh, t), F32), jax.ShapeDtypeStruct((nh, LANES), F32)],
        compiler_params=_params(("arbitrary",)),
    )(dkb, lg, bias)


def _place():
    return lax.axis_index("x"), lax.axis_index("y"), lax.axis_index("c")


def _flip(v, f):
    return 1 - v if f else v


FLIPS = [(fx, fy, fc) for fx in (0, 1) for fy in (0, 1) for fc in (0, 1)][1:]


def _comm_sems(nw):
    return [pltpu.SemaphoreType.DMA((7, nw)), pltpu.SemaphoreType.DMA((7, nw)), pltpu.SemaphoreType.DMA((nw,))]


def _gather_cargo(shards, on_done):
    nw = len(shards)

    def parts(x_refs, out_refs, sems):
        send_sems, recv_sems, local_sems = sems
        x, y, cc = _place()
        me, sibling = (x, y, cc), (x, y, 1 - cc)
        chips = [(1 - x, y), (x, 1 - y), (1 - x, 1 - y)]

        def slot(i, px, py, pc):
            return out_refs[i].at[4 * px + 2 * py + pc]

        def copy(i, k, block, to, src=None):
            return pltpu.make_async_remote_copy(
                src_ref=slot(i, *block) if src is None else src, dst_ref=slot(i, *block),
                send_sem=send_sems.at[k, i], recv_sem=recv_sems.at[k, i], device_id=to, device_id_type=MESH)

        mine = [pltpu.make_async_copy(x_refs[i], slot(i, *me), local_sems.at[i]) for i in range(nw)]
        first = []
        for i in range(nw):
            first.append(copy(i, 0, me, sibling, src=x_refs[i]))
            first += [copy(i, 1 + j, me, (*chip, cc), src=x_refs[i]) for j, chip in enumerate(chips)]
        return me, sibling, chips, cc, copy, mine, first

    def start(x_refs, out_refs, sems):
        *_, mine, first = parts(x_refs, out_refs, sems)
        for cp in mine + first:
            cp.start()

    def finish(x_refs, out_refs, sems):
        me, sibling, chips, cc, copy, mine, first = parts(x_refs, out_refs, sems)
        passed = []
        for i in range(nw):
            for j, chip in enumerate(chips):
                copy(i, 1 + j, (*chip, cc), me).wait_recv()
                passed.append(copy(i, 4 + j, (*chip, cc), sibling))
                passed[-1].start()
        for i in range(nw):
            copy(i, 0, sibling, me).wait_recv()
            for j, chip in enumerate(chips):
                copy(i, 4 + j, (*chip, 1 - cc), me).wait_recv()
        for cp in first + passed:
            cp.wait_send()
        for cp in mine:
            cp.wait()

    out_shape = [jax.ShapeDtypeStruct((N_DEV, *s.shape), s.dtype) for s in shards]
    return _Cargo(shards, out_shape, _comm_sems(nw), start, finish, on_done)


def _scatter_cargo(slots, prev, layer, depth, on_done):
    nw = len(slots)

    def parts(refs, recv_refs, sems):
        g_refs = refs[:nw]
        send_sems, recv_sems, local_sems = sems
        x, y, cc = _place()
        my = 4 * x + 2 * y + cc
        mine, copies = [], []
        for i in range(nw):
            mine.append(pltpu.make_async_copy(g_refs[i].at[my], recv_refs[i].at[my, layer], local_sems.at[i]))
            for k, (fx, fy, fc) in enumerate(FLIPS):
                px, py, pc = _flip(x, fx), _flip(y, fy), _flip(cc, fc)
                copies.append(pltpu.make_async_remote_copy(
                    src_ref=g_refs[i].at[4 * px + 2 * py + pc], dst_ref=recv_refs[i].at[my, layer],
                    send_sem=send_sems.at[k, i], recv_sem=recv_sems.at[k, i], device_id=(px, py, pc), device_id_type=MESH))
        return mine, copies

    def start(refs, recv_refs, sems):
        mine, copies = parts(refs, recv_refs, sems)
        for cp in mine + copies:
            cp.start()

    def finish(refs, recv_refs, sems):
        mine, copies = parts(refs, recv_refs, sems)
        for cp in copies:
            cp.wait_recv()
        for cp in copies:
            cp.wait_send()
        for cp in mine:
            cp.wait()

    ins, aliases = list(slots), {}
    for i, p in enumerate(prev):
        if p is not None:
            aliases[len(ins)] = i
            ins.append(p)
    out_shape = [jax.ShapeDtypeStruct((N_DEV, depth, *s.shape[1:]), s.dtype) for s in slots]
    return _Cargo(ins, out_shape, _comm_sems(nw), start, finish, on_done, aliases)


def _exchange(cargo, *, name):
    def body(*refs):
        c_in = len(cargo.ins)
        c_out = len(cargo.out_shape)
        cargo.start(refs[:c_in], refs[c_in:c_in + c_out], refs[c_in + c_out:])
        cargo.finish(refs[:c_in], refs[c_in:c_in + c_out], refs[c_in + c_out:])

    hbm = pl.BlockSpec(memory_space=pl.ANY)
    res = _pcall(
        body, name=name, in_specs=[hbm] * len(cargo.ins), out_specs=[hbm] * len(cargo.out_shape), out_shape=cargo.out_shape,
        scratch_shapes=cargo.sems, input_output_aliases=dict(cargo.aliases),
    )(*cargo.ins)
    cargo.on_done(list(res))


def _allreduce_small(blob, *, name):
    r, c = blob.shape

    def body(x_ref, out_ref, buf, send_sems, recv_sems):
        x, y, cc = _place()
        my = 4 * x + 2 * y + cc
        copies = []
        for k, (fx, fy, fc) in enumerate(FLIPS):
            peer = (_flip(x, fx), _flip(y, fy), _flip(cc, fc))
            copies.append(pltpu.make_async_remote_copy(
                src_ref=x_ref, dst_ref=buf.at[my], send_sem=send_sems.at[k], recv_sem=recv_sems.at[k],
                device_id=peer, device_id_type=MESH))
        for cp in copies:
            cp.start()
        buf[my] = x_ref[...]
        for cp in copies:
            cp.wait_recv()
        for cp in copies:
            cp.wait_send()
        acc = buf[0]
        for i in range(1, N_DEV):
            acc = acc + buf[i]
        out_ref[...] = acc

    vmem = pl.BlockSpec(memory_space=pltpu.VMEM)
    return _pcall(
        body, name=name, in_specs=[vmem], out_specs=vmem, out_shape=jax.ShapeDtypeStruct((r, c), F32),
        scratch_shapes=[pltpu.VMEM((N_DEV, r, c), F32), pltpu.SemaphoreType.DMA((7,)), pltpu.SemaphoreType.DMA((7,))],
    )(blob)


BIG = ("w_in", "w_mlp_in", "w_mlp_out", "w_up_fox", "w_up_sb", "w_up_dil", "w_out")
ROW_SHARDED = ("w_out", "w_mlp_out")
SMALL = ("attn_norm", "b_forget", "q_norm_fox", "k_norm_fox", "q_norm_dil", "k_norm_dil", "mlp_norm")
BLOB_ROWS = 512


def _pack(parts, dtype):
    flat = jnp.concatenate([p.reshape(-1).astype(dtype) for p in parts])
    size = -(-flat.shape[0] // (BLOB_ROWS * LANES)) * (BLOB_ROWS * LANES)
    return jnp.pad(flat, (0, size - flat.shape[0])).reshape(-1, LANES)


def _unpack(blob, shapes):
    flat = blob.reshape(-1)
    out, off = [], 0
    for shp in shapes:
        size = 1
        for s in shp:
            size *= s
        out.append(flat[off:off + size].reshape(shp))
        off += size
    return out


def _join_shards(name, sh):
    if name in ROW_SHARDED:
        return sh.reshape(-1, sh.shape[2])
    return jnp.transpose(sh, (1, 0, 2)).reshape(sh.shape[1], -1)


def _split_shards(name, full):
    a, b = full.shape
    if name in ROW_SHARDED:
        return full.reshape(N_DEV, a // N_DEV, b)
    return jnp.transpose(full.reshape(a, N_DEV, b // N_DEV), (1, 0, 2))


def _permute_in(w, dp):
    o1 = 3 * W_FOX
    o2 = o1 + N_HEADS_FOX
    pad = [(0, 0)] * (w.ndim - 1) + [(0, dp - w.shape[-1])]
    return jnp.pad(jnp.concatenate([w[..., :o1], w[..., o2:], w[..., o1:o2]], axis=-1), pad)


def _unpermute_in(wp, d_in):
    o1 = 3 * W_FOX
    fg = d_in - N_HEADS_FOX
    return jnp.concatenate([wp[..., :o1], wp[..., fg:d_in], wp[..., o1:fg]], axis=-1)


def _to_streams(a, r):
    b, t, c = a.shape
    if r == 1:
        return a
    return jnp.transpose(a.reshape(b, t // r, r, c), (0, 2, 1, 3)).reshape(b * r, t // r, c)


def _from_streams(a, r, b):
    if r == 1:
        return a
    z, n, c = a.shape
    return jnp.transpose(a.reshape(b, r, n, c), (0, 2, 1, 3)).reshape(b, n * r, c)


def _stat_to_tokens(st, r, b):
    hh = st.shape[1]
    n = st.shape[2] * st.shape[4]
    return jnp.transpose(st.reshape(b, r, hh, n), (0, 2, 3, 1)).reshape(b, hh, n * r)


def _stat_to_streams(tok, r, blk):
    b, hh, t = tok.shape
    n = t // r
    return jnp.transpose(tok.reshape(b, hh, n, r), (0, 3, 1, 2)).reshape(b * r, hh, n // blk, 1, blk)


def _rope_tables(positions):
    half = HEAD_DIM // 2
    inv = 1.0 / (ROPE_THETA ** (jnp.arange(half, dtype=F32) / half))
    ang = positions.astype(F32)[..., None] * inv
    cos, sin = jnp.cos(ang), jnp.sin(ang)
    return jnp.tile(cos, (1, 1, 4)), jnp.tile(jnp.concatenate([-sin, sin], axis=-1), (1, 1, 2))


def _gain2(g):
    return jnp.tile(g.reshape(1, HEAD_DIM), (1, 2))


def _layer_fwd(l, x, w, small, ropes, bl, t, cargo):
    n, d = x.shape
    s = {}
    s["x"] = x
    s["h"] = _rmsnorm_fwd(x, small["attn_norm"][l].reshape(1, d), name=f"norm_attn_fwd{l}")
    proj = _mm(s["h"], w["w_in"][l], name=f"mm_proj{l}")
    s["proj"] = proj
    dp = proj.shape[1]
    proj3 = proj.reshape(bl, t, dp)
    p_fg = P_GATE + 3 * d

    lg = jnp.transpose(proj3[:, :, p_fg:p_fg + N_HEADS_FOX], (0, 2, 1))
    s["lg"] = lg
    kb = _fox_gate_fwd(lg, small["b_forget"][l].reshape(N_HEADS_FOX, 1), name=f"fox_gate_fwd{l}")
    blk = _att_blk(t)
    kb5 = kb.reshape(bl, N_HEADS_FOX, t // blk, 1, blk)
    s["kb5"] = kb5
    gqf, gkf = _gain2(small["q_norm_fox"][l]), _gain2(small["k_norm_fox"][l])
    fo = P_FOX // LANES
    fox_offs = (fo, fo + W_FOX // LANES, fo + 2 * W_FOX // LANES)
    out_a, lse_a = _attn_fwd(proj3, fox_offs, N_HEADS_FOX // 2, gqf, gkf, kbias=kb5, window=t, name=f"fox_fwd{l}",
                             cargo=cargo.get("fox_fwd"))
    s["out_a"], s["lse_a"] = out_a, lse_a

    so = P_SB // LANES
    sb_offs = (so, so + W_SB // LANES, so + 2 * W_SB // LANES)
    out_b, lt_b = _sb_fwd(proj3, sb_offs, N_HEADS_SB // 2, name=f"sb_fwd{l}", cargo=cargo.get("sb_fwd"))
    s["out_b"], s["lt_b"] = out_b, lt_b

    gqd, gkd = _gain2(small["q_norm_dil"][l]), _gain2(small["k_norm_dil"][l])
    os_, lses, s["dil_src"] = [], [], []
    for g, (window, r) in enumerate(DIL_PATTERNS):
        c0 = P_DIL + g * W_DIL
        if r == 1:
            src, offs = proj3, (c0 // LANES, (c0 + W_DILQ) // LANES, (c0 + 2 * W_DILQ) // LANES)
        else:
            qkv = jnp.concatenate([proj3[:, :, c0 + i * W_DILQ:c0 + i * W_DILQ + W_DIL] for i in range(3)], axis=-1)
            src, offs = _to_streams(qkv, r), (0, W_DIL // LANES, 2 * W_DIL // LANES)
        s["dil_src"].append((src, offs))
        o_g, lse_g = _attn_fwd(src, offs, N_HEADS_DIL // 2, gqd, gkd, rope=ropes[g], window=window // r, name=f"dil_fwd{l}_{g}")
        os_.append(_from_streams(o_g, r, bl).reshape(n, W_DIL))
        lses.append(_stat_to_tokens(lse_g, r, bl).reshape(bl * N_HEADS_DIL, t))
    lse_c, *ws = _dil_weights(lses, name=f"dil_weights{l}")
    ws = [jnp.repeat(jnp.transpose(wg.reshape(bl, N_HEADS_DIL, t), (0, 2, 1)).reshape(n, N_HEADS_DIL), HEAD_DIM, axis=1) for wg in ws]
    out_c = _dil_mix(os_, ws, name=f"dil_mix{l}")
    s["out_c"], s["lse_c"] = out_c, lse_c.reshape(bl, N_HEADS_DIL, t)

    ys = [_mm(out_a.reshape(n, W_FOX), w["w_up_fox"][l], name=f"mm_up_fox{l}"),
          _mm(out_b.reshape(n, W_SB), w["w_up_sb"][l], name=f"mm_up_sb{l}"),
          _mm(out_c, w["w_up_dil"][l], name=f"mm_up_dil{l}")]
    s["ys"] = ys
    s["merged"] = _gate_merge_fwd(proj, ys, name=f"gate_merge_fwd{l}")
    x1 = _mm(s["merged"], w["w_out"][l], add=x, name=f"mm_out{l}")
    s["x1"] = x1

    s["h2"] = _rmsnorm_fwd(x1, small["mlp_norm"][l].reshape(1, d), name=f"norm_mlp_fwd{l}")
    s["u"] = _mm(s["h2"], w["w_mlp_in"][l], name=f"mm_mlp_in{l}")
    s["a"] = _relu2_fwd(s["u"], name=f"relu2_fwd{l}")
    x2 = _mm(s["a"], w["w_mlp_out"][l], add=x1, name=f"mm_mlp_out{l}")
    return x2, s


def _layer_bwd(l, dx2, s, w, small, ropes, bl, t, hooks):
    n, d = dx2.shape
    gw, gs = {}, {}

    def cargo(call):
        return hooks[call](gw) if call in hooks else None
    da = _mm(dx2, w["w_mlp_out"][l], tb=True, name=f"mm_da{l}")
    du = _relu2_bwd(da, s["u"], name=f"relu2_bwd{l}")
    gw["w_mlp_out"] = _mm(s["a"], dx2, ta=True, name=f"mm_dw_mlp_out{l}")
    gw["w_mlp_in"] = _mm(s["h2"], du, ta=True, name=f"mm_dw_mlp_in{l}")
    dh2 = _mm(du, w["w_mlp_in"][l], tb=True, name=f"mm_dh2{l}")
    dx1, gs["mlp_norm"] = _rmsnorm_bwd(s["x1"], small["mlp_norm"][l].reshape(1, d), dh2, dx2, name=f"norm_mlp_bwd{l}")

    dmerged = _mm(dx1, w["w_out"][l], tb=True, name=f"mm_dmerged{l}")
    gw["w_out"] = _mm(s["merged"], dx1, ta=True, name=f"mm_dw_out{l}")
    dya, dyb, dyc, dgl0, dgl1, dgl2 = _gate_merge_bwd(s["proj"], s["ys"], dmerged, name=f"gate_merge_bwd{l}")
    out_a2, out_b2 = s["out_a"].reshape(n, W_FOX), s["out_b"].reshape(n, W_SB)
    gw["w_up_fox"] = _mm(out_a2, dya, ta=True, name=f"mm_dw_up_fox{l}")
    gw["w_up_sb"] = _mm(out_b2, dyb, ta=True, name=f"mm_dw_up_sb{l}")
    gw["w_up_dil"] = _mm(s["out_c"], dyc, ta=True, name=f"mm_dw_up_dil{l}")
    dout_a = _mm(dya, w["w_up_fox"][l], tb=True, name=f"mm_dout_a{l}").reshape(bl, t, W_FOX)
    dout_b = _mm(dyb, w["w_up_sb"][l], tb=True, name=f"mm_dout_b{l}").reshape(bl, t, W_SB)
    dout_c = _mm(dyc, w["w_up_dil"][l], tb=True, name=f"mm_dout_c{l}").reshape(bl, t, W_DIL)

    proj3 = s["proj"].reshape(bl, t, -1)
    gqf, gkf = _gain2(small["q_norm_fox"][l]), _gain2(small["k_norm_fox"][l])
    fo = P_FOX // LANES
    fox_offs = (fo, fo + W_FOX // LANES, fo + 2 * W_FOX // LANES)
    dq_a, dk_a, dv_a, dg_a, dkb5 = _attn_bwd(proj3, fox_offs, N_HEADS_FOX // 2, gqf, gkf, s["out_a"], dout_a, s["lse_a"],
                                             kbias=s["kb5"], window=t, name=f"fox_bwd{l}", cargo=cargo("fox_bwd"))
    gs["fox_gains"] = dg_a
    dlg, gs["b_forget"] = _fox_gate_bwd(dkb5.reshape(bl, N_HEADS_FOX, t), s["lg"], small["b_forget"][l].reshape(N_HEADS_FOX, 1),
                                        name=f"fox_gate_bwd{l}")
    so = P_SB // LANES
    sb_offs = (so, so + W_SB // LANES, so + 2 * W_SB // LANES)
    dq_b, dk_b, dv_b = _sb_bwd(proj3, sb_offs, N_HEADS_SB // 2, dout_b, s["lt_b"], name=f"sb_bwd{l}", cargo=cargo("sb_bwd"))
    gqd, gkd = _gain2(small["q_norm_dil"][l]), _gain2(small["k_norm_dil"][l])
    out_c3 = s["out_c"].reshape(bl, t, W_DIL)
    dqs, dks, dvs, dgd = [], [], [], None
    for g, (window, r) in enumerate(DIL_PATTERNS):
        src, offs = s["dil_src"][g]
        lse_g = _stat_to_streams(s["lse_c"], r, _att_blocks(t // r)[0])
        dq_g, dk_g, dv_g, dg_g = _attn_bwd(src, offs, N_HEADS_DIL // 2, gqd, gkd, _to_streams(out_c3, r), _to_streams(dout_c, r),
                                           lse_g, rope=ropes[g], window=window // r, name=f"dil_bwd{l}_{g}")
        dqs.append(_from_streams(dq_g, r, bl))
        dks.append(_from_streams(dk_g, r, bl))
        dvs.append(_from_streams(dv_g, r, bl))
        dgd = dg_g if dgd is None else jnp.concatenate([dgd, dg_g], axis=0)
    gs["dil_gains"] = dgd

    dp = s["proj"].shape[1]
    parts = [dq_a, dk_a, dv_a, dq_b, dk_b, dv_b] + dqs + dks + dvs
    parts = [p.reshape(n, -1).astype(BF16) for p in parts] + [dgl0, dgl1, dgl2, jnp.transpose(dlg, (0, 2, 1)).reshape(n, -1).astype(BF16)]
    used = P_GATE + 3 * d + N_HEADS_FOX
    dproj = jnp.concatenate(parts + [jnp.zeros((n, dp - used), BF16)], axis=1)
    gw["w_in"] = _mm(s["h"], dproj, ta=True, name=f"mm_dw_in{l}")
    dh = _mm(dproj, w["w_in"][l], tb=True, name=f"mm_dh{l}", cargo=cargo("mm_dh"))
    dx, gs["attn_norm"] = _rmsnorm_bwd(s["x"], small["attn_norm"][l].reshape(1, d), dh, dx1, name=f"norm_attn_bwd{l}")
    return dx, gw, gs


def kernel(x, positions, attn_norm, w_in, b_forget, q_norm_fox, k_norm_fox, q_norm_dil, k_norm_dil, w_up_fox, w_up_sb, w_up_dil, w_out, mlp_norm, w_mlp_in, w_mlp_out, loss_target, m_attn_norm, m_w_in, m_b_forget, m_q_norm_fox, m_k_norm_fox, m_q_norm_dil, m_k_norm_dil, m_w_up_fox, m_w_up_sb, m_w_up_dil, m_w_out, m_mlp_norm, m_w_mlp_in, m_w_mlp_out, v_attn_norm, v_w_in, v_b_forget, v_q_norm_fox, v_k_norm_fox, v_q_norm_dil, v_k_norm_dil, v_w_up_fox, v_w_up_sb, v_w_up_dil, v_w_out, v_mlp_norm, v_w_mlp_in, v_w_mlp_out):
    bl, t, d = x.shape
    n = bl * t
    depth = attn_norm.shape[0]
    wl = dict(w_in=w_in, w_up_fox=w_up_fox, w_up_sb=w_up_sb, w_up_dil=w_up_dil, w_out=w_out, w_mlp_in=w_mlp_in, w_mlp_out=w_mlp_out)
    ml = dict(w_in=m_w_in, w_up_fox=m_w_up_fox, w_up_sb=m_w_up_sb, w_up_dil=m_w_up_dil, w_out=m_w_out, w_mlp_in=m_w_mlp_in, w_mlp_out=m_w_mlp_out)
    vl = dict(w_in=v_w_in, w_up_fox=v_w_up_fox, w_up_sb=v_w_up_sb, w_up_dil=v_w_up_dil, w_out=v_w_out, w_mlp_in=v_w_mlp_in, w_mlp_out=v_w_mlp_out)
    small = dict(attn_norm=attn_norm, b_forget=b_forget, q_norm_fox=q_norm_fox, k_norm_fox=k_norm_fox, q_norm_dil=q_norm_dil,
                 k_norm_dil=k_norm_dil, mlp_norm=mlp_norm)
    m_small = dict(attn_norm=m_attn_norm, b_forget=m_b_forget, q_norm_fox=m_q_norm_fox, k_norm_fox=m_k_norm_fox,
                   q_norm_dil=m_q_norm_dil, k_norm_dil=m_k_norm_dil, mlp_norm=m_mlp_norm)
    v_small = dict(attn_norm=v_attn_norm, b_forget=v_b_forget, q_norm_fox=v_q_norm_fox, k_norm_fox=v_k_norm_fox,
                   q_norm_dil=v_q_norm_dil, k_norm_dil=v_k_norm_dil, mlp_norm=v_mlp_norm)

    d_in = w_in.shape[-1] * N_DEV
    dp = -(-d_in // 512) * 512
    rest = [k for k in BIG if k != "w_in"]
    w = {k: [None] * depth for k in BIG}

    def gather(items):
        def done(res):
            for (k, l), sh in zip(items, res):
                whole = _join_shards(k, sh)
                w[k][l] = _permute_in(whole, dp) if k == "w_in" else whole

        return _gather_cargo([wl[k][l].astype(BF16) for k, l in items], done)

    _exchange(gather([("w_in", 0)]), name="gather_first")
    cos, sin = _rope_tables(positions)
    ropes = [(_to_streams(cos, r), _to_streams(sin, r)) for _, r in DIL_PATTERNS]

    h = x.reshape(n, d)
    saved = []
    for l in range(depth):
        cargo = {"fox_fwd": gather([(k, l) for k in rest])}
        if l + 1 < depth:
            cargo["sb_fwd"] = gather([("w_in", l + 1)])
        h, s = _layer_fwd(l, h, w, small, ropes, bl, t, cargo)
        saved.append(s)
    dy, loss_part = _loss_head(h, loss_target.reshape(n, d), name="loss_head")

    recv = {}

    def scatter(names, l, grads):
        def done(res):
            recv.update(zip(names, res))

        slots = [_split_shards(k, _unpermute_in(grads[k], d_in) if k == "w_in" else grads[k]).astype(BF16) for k in names]
        return _scatter_cargo(slots, [recv.get(k) for k in names], l, depth, done)

    gss = [None] * depth
    for l in reversed(range(depth)):
        hooks = {"fox_bwd": lambda gw, l=l: scatter(rest, l, gw), "mm_dh": lambda gw, l=l: scatter(["w_in"], l, gw)}
        dy, _, gss[l] = _layer_bwd(l, dy, saved[l], w, small, ropes, bl, t, hooks)
    grad_x = dy.reshape(bl, t, d)

    g_big, d_big, m_big, v_big = {}, {}, {}, {}
    for k in BIG:
        g_big[k], d_big[k], m_big[k], v_big[k] = _adamw(recv[k], wl[k], ml[k], vl[k], name=f"adamw_{k}")

    rows = [loss_part]
    for l in range(depth):
        gs = gss[l]
        rows += [gs["attn_norm"].reshape(-1, LANES), gs["mlp_norm"].reshape(-1, LANES), gs["fox_gains"], gs["dil_gains"], gs["b_forget"]]
    row_counts = [r.shape[0] for r in rows]
    part = jnp.concatenate(rows, axis=0)
    pad_rows = -(-part.shape[0] // 8) * 8 - part.shape[0]
    summed = _allreduce_small(jnp.pad(part, ((0, pad_rows), (0, 0))), name="allreduce_small")
    pieces, off = [], 0
    for c in row_counts:
        pieces.append(summed[off:off + c])
        off += c
    loss = pieces[0][0, 0]

    def fold(row):
        return row[:HEAD_DIM] + row[HEAD_DIM:]

    g_small = {k: [] for k in SMALL}
    for l in range(depth):
        an, mn, fg, dg, bf = pieces[1 + 5 * l:6 + 5 * l]
        g_small["attn_norm"].append(an.reshape(d))
        g_small["mlp_norm"].append(mn.reshape(d))
        g_small["q_norm_fox"].append(fold(fg[0]))
        g_small["k_norm_fox"].append(fold(fg[1]))
        g_small["q_norm_dil"].append(fold(dg[0]) + fold(dg[8]) + fold(dg[16]))
        g_small["k_norm_dil"].append(fold(dg[1]) + fold(dg[9]) + fold(dg[17]))
        g_small["b_forget"].append(bf[:, 0])
    g_small = {k: jnp.stack(vs) for k, vs in g_small.items()}
    small_shapes = [small[k].shape for k in SMALL]
    outs = _adamw(_pack([g_small[k] for k in SMALL], F32)[None, None], _pack([small[k] for k in SMALL], F32)[None],
                  _pack([m_small[k] for k in SMALL], F32)[None], _pack([v_small[k] for k in SMALL], F32)[None], name="adamw_small")
    g_sm, d_sm, m_sm, v_sm = (dict(zip(SMALL, _unpack(o, small_shapes))) for o in outs)

    order = ("attn_norm", "w_in", "b_forget", "q_norm_fox", "k_norm_fox", "q_norm_dil", "k_norm_dil", "w_up_fox", "w_up_sb",
             "w_up_dil", "w_out", "mlp_norm", "w_mlp_in", "w_mlp_out")
    res = [loss, grad_x]
    for big, sm in ((g_big, g_sm), (d_big, d_sm), (m_big, m_sm), (v_big, v_sm)):
        res += [big[k] if k in big else sm[k] for k in order]
    return tuple(res)
```

```python
import jax
import jax.numpy as jnp
from jax import lax
from jax.experimental import pallas as pl
from jax.experimental.pallas import tpu as pltpu

F32 = jnp.float32
BF16 = jnp.bfloat16

HEAD_DIM = 64
LANES = 128
N_HEADS_FOX = 8
N_HEADS_SB = 8
N_HEADS_DIL = 4
DIL_PATTERNS = ((128, 1), (512, 4), (2048, 16))
ROPE_THETA = 10000.0
EPS = 1e-6
SCALE = 0.125
W_FOX = N_HEADS_FOX * HEAD_DIM
W_SB = N_HEADS_SB * HEAD_DIM
W_DIL = N_HEADS_DIL * HEAD_DIM
W_DILQ = len(DIL_PATTERNS) * W_DIL
P_FOX = 0
P_SB = 3 * W_FOX
P_DIL = P_SB + 3 * W_SB
P_GATE = P_DIL + 3 * W_DILQ
N_DEV = 8
ATT_BLK = 256
ATT_BQ = 512
NEG = -1e30
VMEM_LIMIT = 56 * 1024 * 1024
ADAMW_BLOCK_ELEMS = 128 * 1024

ADAM_LR = 0.001
ADAM_B1 = 0.9
ADAM_B2 = 0.999
ADAM_EPS = 1e-08
ADAM_WD = 0.01
ADAM_STEP = 10

NT = (((1,), (1,)), ((), ()))
MESH = pl.DeviceIdType.MESH


def _pcall(body, **kw):
    return pl.pallas_call(body, **kw)


def _params(sem=None):
    return pltpu.CompilerParams(dimension_semantics=sem, vmem_limit_bytes=VMEM_LIMIT)


class _Cargo:
    def __init__(self, ins, out_shape, sems, start, finish, on_done, aliases=None):
        self.ins, self.out_shape, self.sems = list(ins), list(out_shape), list(sems)
        self.start, self.finish, self.on_done, self.aliases = start, finish, on_done, dict(aliases or {})


def _call(body, *, cargo=None, name, grid=(), in_specs, out_specs, out_shape, scratch_shapes=(), compiler_params=None):
    if cargo is None:
        kw = dict(grid=grid) if grid else {}
        if compiler_params is not None:
            kw["compiler_params"] = compiler_params
        return _pcall(body, name=name, in_specs=in_specs, out_specs=out_specs, out_shape=out_shape,
                      scratch_shapes=list(scratch_shapes), **kw)
    single = not isinstance(out_shape, (list, tuple))
    o_specs, o_shape = ([out_specs], [out_shape]) if single else (list(out_specs), list(out_shape))
    n_in, n_out, n_scr = len(in_specs), len(o_shape), len(scratch_shapes)
    c_in, c_out = len(cargo.ins), len(cargo.out_shape)

    def wrapped(*refs):
        ins, cins = refs[:n_in], refs[n_in:n_in + c_in]
        o0 = n_in + c_in
        outs, couts = refs[o0:o0 + n_out], refs[o0 + n_out:o0 + n_out + c_out]
        s0 = o0 + n_out + c_out
        scr, sems = refs[s0:s0 + n_scr], refs[s0 + n_scr:]
        first = last = None
        for ax, size in enumerate(grid):
            pid = pl.program_id(ax)
            first = (pid == 0) if first is None else first & (pid == 0)
            last = (pid == size - 1) if last is None else last & (pid == size - 1)
        if first is None:
            cargo.start(cins, couts, sems)
            body(*ins, *outs, *scr)
            cargo.finish(cins, couts, sems)
            return

        @pl.when(first)
        def _():
            cargo.start(cins, couts, sems)

        body(*ins, *outs, *scr)

        @pl.when(last)
        def _():
            cargo.finish(cins, couts, sems)

    hbm = pl.BlockSpec(memory_space=pl.ANY)
    kw = dict(grid=grid, compiler_params=_params(("arbitrary",) * len(grid))) if grid else {}
    call = _pcall(
        wrapped, name=name, in_specs=list(in_specs) + [hbm] * c_in, out_specs=o_specs + [hbm] * c_out,
        out_shape=o_shape + cargo.out_shape, scratch_shapes=list(scratch_shapes) + cargo.sems,
        input_output_aliases={n_in + i: n_out + j for i, j in cargo.aliases.items()}, **kw)

    def run(*args):
        res = call(*args, *cargo.ins)
        cargo.on_done(list(res[n_out:]))
        return res[0] if single else list(res[:n_out])

    return run


def _tile(dim, target, mult=LANES):
    t = (min(dim, target) // mult) * mult
    while t >= mult:
        if dim % t == 0:
            return t
        t -= mult
    return dim


def _mm(a, b, *, ta=False, tb=False, add=None, relu2=False, relu_grad_of=None, out_dtype=F32, name, tm=1024, tn=512,
        tk=1024, cargo=None):
    m, k = (a.shape[1], a.shape[0]) if ta else a.shape
    n = b.shape[0] if tb else b.shape[1]
    tm, tn, tk = _tile(m, tm), _tile(n, tn), _tile(k, tk)
    nk = k // tk
    dn = (((0,) if ta else (1,), (1,) if tb else (0,)), ((), ()))

    extra = add if add is not None else relu_grad_of

    def body(*refs):
        a_ref, b_ref = refs[:2]
        x_ref = refs[2] if extra is not None else None
        outs = refs[2 + (extra is not None):-1]
        acc = refs[-1]
        kk = pl.program_id(2)
        part = lax.dot_general(a_ref[...].astype(BF16), b_ref[...].astype(BF16), dn, preferred_element_type=F32)

        def finish(r):
            if add is not None:
                r = r + x_ref[...]
            if relu_grad_of is not None:
                r = r * (2.0 * jnp.maximum(x_ref[...].astype(F32), 0.0))
            outs[0][...] = r.astype(out_dtype)
            if relu2:
                rr = jnp.maximum(r, 0.0)
                outs[1][...] = (rr * rr).astype(BF16)

        if nk == 1:
            finish(part)
            return

        @pl.when(kk == 0)
        def _():
            acc[...] = part

        @pl.when((kk > 0) & (kk < nk - 1))
        def _():
            acc[...] += part

        @pl.when(kk == nk - 1)
        def _():
            finish(acc[...] + part)

    a_spec = pl.BlockSpec((tk, tm), lambda i, j, q: (q, i)) if ta else pl.BlockSpec((tm, tk), lambda i, j, q: (i, q))
    b_spec = pl.BlockSpec((tn, tk), lambda i, j, q: (j, q)) if tb else pl.BlockSpec((tk, tn), lambda i, j, q: (q, j))
    o_spec = pl.BlockSpec((tm, tn), lambda i, j, q: (i, j))
    ins, specs = [a, b], [a_spec, b_spec]
    if extra is not None:
        ins.append(extra)
        specs.append(o_spec)
    sds = jax.ShapeDtypeStruct((m, n), out_dtype)
    return _call(
        body, cargo=cargo, name=name, grid=(m // tm, n // tn, nk), in_specs=specs,
        out_specs=[o_spec, o_spec] if relu2 else o_spec,
        out_shape=[sds, jax.ShapeDtypeStruct((m, n), BF16)] if relu2 else sds,
        scratch_shapes=[pltpu.VMEM((tm, tn) if nk > 1 else (8, LANES), F32)],
        compiler_params=_params(("parallel", "parallel", "arbitrary")),
    )(*ins)


def _rmsnorm_fwd(x, g, *, name):
    n, d = x.shape
    tm = _tile(n, 256, 8)

    def body(x_ref, g_ref, h_ref):
        xv = x_ref[...]
        inv = lax.rsqrt(jnp.mean(xv * xv, axis=1, keepdims=True) + EPS)
        h_ref[...] = (xv * inv * g_ref[...]).astype(BF16)

    row = pl.BlockSpec((tm, d), lambda i: (i, 0))
    return _pcall(
        body, name=name, grid=(n // tm,), in_specs=[row, pl.BlockSpec((1, d), lambda i: (0, 0))], out_specs=row,
        out_shape=jax.ShapeDtypeStruct((n, d), BF16), compiler_params=_params(("parallel",)),
    )(x, g)


def _rmsnorm_bwd(x, g, dh, dres, *, name):
    n, d = x.shape
    tm = _tile(n, 256, 8)

    def body(x_ref, g_ref, dh_ref, dres_ref, dx_ref, dg_ref):
        @pl.when(pl.program_id(0) == 0)
        def _():
            dg_ref[...] = jnp.zeros_like(dg_ref)

        xv = x_ref[...]
        inv = lax.rsqrt(jnp.mean(xv * xv, axis=1, keepdims=True) + EPS)
        y = xv * inv
        dhv = dh_ref[...]
        dg_ref[...] += jnp.sum(dhv * y, axis=0, keepdims=True)
        dy = dhv * g_ref[...]
        dx_ref[...] = dres_ref[...] + inv * (dy - y * jnp.mean(dy * y, axis=1, keepdims=True))

    row = pl.BlockSpec((tm, d), lambda i: (i, 0))
    vec = pl.BlockSpec((1, d), lambda i: (0, 0))
    return _pcall(
        body, name=name, grid=(n // tm,), in_specs=[row, vec, row, row], out_specs=[row, vec],
        out_shape=[jax.ShapeDtypeStruct((n, d), F32), jax.ShapeDtypeStruct((1, d), F32)],
        compiler_params=_params(("arbitrary",)),
    )(x, g, dh, dres)


def _gate_specs(n, d):
    bw = 256 if d % 256 == 0 else LANES
    tm = _tile(n, 512, 8)
    nb = d // bw
    yspec = pl.BlockSpec((tm, bw), lambda i, j: (i, j))
    gspecs = [pl.BlockSpec((tm, bw), lambda i, j, b=b: (i, P_GATE // bw + b * nb + j)) for b in range(3)]
    return tm, bw, nb, yspec, gspecs


def _gate_merge_fwd(proj, ys, *, name):
    n, d = ys[0].shape
    tm, bw, nb, yspec, gspecs = _gate_specs(n, d)

    def body(g0, g1, g2, y0, y1, y2, o_ref):
        acc = jax.nn.sigmoid(g0[...]) * y0[...]
        acc += jax.nn.sigmoid(g1[...]) * y1[...]
        acc += jax.nn.sigmoid(g2[...]) * y2[...]
        o_ref[...] = acc.astype(BF16)

    return _pcall(
        body, name=name, grid=(n // tm, nb), in_specs=gspecs + [yspec] * 3, out_specs=yspec,
        out_shape=jax.ShapeDtypeStruct((n, d), BF16), compiler_params=_params(("parallel", "parallel")),
    )(proj, proj, proj, *ys)


def _gate_merge_bwd(proj, ys, dmerged, *, name):
    n, d = ys[0].shape
    tm, bw, nb, yspec, gspecs = _gate_specs(n, d)

    def body(g0, g1, g2, y0, y1, y2, dm_ref, dy0, dy1, dy2, dgl0, dgl1, dgl2):
        dm = dm_ref[...]
        for g_ref, y_ref, dy_ref, dgl_ref in ((g0, y0, dy0, dgl0), (g1, y1, dy1, dgl1), (g2, y2, dy2, dgl2)):
            s = jax.nn.sigmoid(g_ref[...])
            dy_ref[...] = (dm * s).astype(BF16)
            dgl_ref[...] = (dm * y_ref[...] * s * (1.0 - s)).astype(BF16)

    sds = jax.ShapeDtypeStruct((n, d), BF16)
    return _pcall(
        body, name=name, grid=(n // tm, nb), in_specs=gspecs + [yspec] * 4, out_specs=[yspec] * 6,
        out_shape=[sds] * 6, compiler_params=_params(("parallel", "parallel")),
    )(proj, proj, proj, *ys, dmerged)


def _loss_head(y, tgt, *, name):
    n, d = y.shape
    tm = _tile(n, 256, 8)
    steps = n // tm

    def body(y_ref, t_ref, dy_ref, loss_ref, acc):
        i = pl.program_id(0)

        @pl.when(i == 0)
        def _():
            acc[...] = jnp.zeros_like(acc)

        e = y_ref[...] - t_ref[...]
        dy_ref[...] = e * (1.0 / d)
        acc[...] += jnp.sum(e * e, axis=0, keepdims=True)

        @pl.when(i == steps - 1)
        def _():
            tot = jnp.sum(acc[...], axis=1, keepdims=True) * (0.5 / d)
            loss_ref[...] = jnp.broadcast_to(tot, loss_ref.shape)

    row = pl.BlockSpec((tm, d), lambda i: (i, 0))
    return _pcall(
        body, name=name, grid=(steps,), in_specs=[row, row], out_specs=[row, pl.BlockSpec((8, LANES), lambda i: (0, 0))],
        out_shape=[jax.ShapeDtypeStruct((n, d), F32), jax.ShapeDtypeStruct((8, LANES), F32)],
        scratch_shapes=[pltpu.VMEM((1, d), F32)], compiler_params=_params(("arbitrary",)),
    )(y, tgt)


def _dil_weights(lses, *, name):
    shp = lses[0].shape

    def body(l0, l1, l2, lse_ref, w0, w1, w2):
        a, b, c = l0[...], l1[...], l2[...]
        m = jnp.maximum(jnp.maximum(a, b), c)
        ea, eb, ec = jnp.exp(a - m), jnp.exp(b - m), jnp.exp(c - m)
        den = ea + eb + ec
        lse_ref[...] = m + jnp.log(den)
        w0[...] = ea / den
        w1[...] = eb / den
        w2[...] = ec / den

    vmem = pl.BlockSpec(memory_space=pltpu.VMEM)
    return _pcall(body, name=name, in_specs=[vmem] * 3, out_specs=[vmem] * 4, out_shape=[jax.ShapeDtypeStruct(shp, F32)] * 4)(*lses)


def _dil_mix(os_, ws, *, name):
    n, w = os_[0].shape
    tm = _tile(n, 512, 8)

    def body(o0, o1, o2, w0, w1, w2, out_ref):
        out_ref[...] = w0[...] * o0[...] + w1[...] * o1[...] + w2[...] * o2[...]

    spec = pl.BlockSpec((tm, w), lambda i: (i, 0))
    return _pcall(
        body, name=name, grid=(n // tm,), in_specs=[spec] * 6, out_specs=spec, out_shape=jax.ShapeDtypeStruct((n, w), F32),
        compiler_params=_params(("parallel",)),
    )(*os_, *ws)


def _adamw(gsrc, w, m, v, *, name):
    s, dep, a, b = gsrc.shape
    ta = _tile(a, max(16, (ADAMW_BLOCK_ELEMS // b) // 16 * 16), 16)
    c1 = 1.0 / (1.0 - ADAM_B1 ** ADAM_STEP)
    c2 = 1.0 / (1.0 - ADAM_B2 ** ADAM_STEP)

    def body(gs_ref, w_ref, m_ref, v_ref, g_ref, d_ref, m2_ref, v2_ref):
        g = gs_ref[0].astype(F32)
        for i in range(1, s):
            g = g + gs_ref[i].astype(F32)
        m2 = ADAM_B1 * m_ref[...] + (1.0 - ADAM_B1) * g
        v2 = ADAM_B2 * v_ref[...] + (1.0 - ADAM_B2) * (g * g)
        g_ref[...] = g
        m2_ref[...] = m2
        v2_ref[...] = v2
        d_ref[...] = -ADAM_LR * ((m2 * c1) / (jnp.sqrt(v2 * c2) + ADAM_EPS) + ADAM_WD * w_ref[...])

    spec = pl.BlockSpec((None, ta, b), lambda l, i: (l, i, 0))
    sds = jax.ShapeDtypeStruct((dep, a, b), F32)
    return _pcall(
        body, name=name, grid=(dep, a // ta),
        in_specs=[pl.BlockSpec((s, None, ta, b), lambda l, i: (0, l, i, 0)), spec, spec, spec],
        out_specs=[spec] * 4, out_shape=[sds] * 4, compiler_params=_params(("parallel", "parallel")),
    )(gsrc, w, m, v)


def _mask_a():
    return lax.broadcasted_iota(jnp.int32, (1, LANES), 1) < HEAD_DIM


def _half_sum(x, m_a):
    sa = jnp.sum(jnp.where(m_a, x, 0.0), axis=1, keepdims=True)
    sb = jnp.sum(jnp.where(m_a, 0.0, x), axis=1, keepdims=True)
    return jnp.where(m_a, sa, sb)


def _head_inv(x, m_a):
    return lax.rsqrt(_half_sum(x * x, m_a) * (1.0 / HEAD_DIM) + EPS)


def _swap32(x):
    first = (lax.broadcasted_iota(jnp.int32, (1, LANES), 1) % HEAD_DIM) < (HEAD_DIM // 2)
    return jnp.where(first, pltpu.roll(x, LANES - HEAD_DIM // 2, 1), pltpu.roll(x, HEAD_DIM // 2, 1))


def _tri(blk, rel):
    r = lax.broadcasted_iota(jnp.int32, (blk, blk), 0)
    c = lax.broadcasted_iota(jnp.int32, (blk, blk), 1)
    return jnp.where(rel(r, c), 1.0, 0.0).astype(BF16)


def _cumdot(x, u, parts):
    acc = None
    r = x
    for i in range(parts):
        xi = r.astype(BF16)
        t = jnp.dot(xi, u, preferred_element_type=F32)
        acc = t if acc is None else acc + t
        if i + 1 < parts:
            r = r - xi.astype(F32)
    return acc


def _rows(i, blk):
    return pl.ds(pl.multiple_of(i * blk, blk), blk)


def _col_spec(n, off):
    return pl.BlockSpec((None, n, LANES), lambda z, p, off=off: (z, 0, off + p))


def _att_blk(n):
    return ATT_BLK if n % ATT_BLK == 0 else min(LANES, n)


def _att_blocks(n):
    bk = _att_blk(n)
    return (ATT_BQ if n % ATT_BQ == 0 else bk), bk


def _loop(lo, hi, fn):
    def it(i, c):
        fn(i)
        return c

    lax.fori_loop(lo, hi, it, 0)


def _normed(src, g_ref, rope_refs, rows, m_a):
    xv = src[rows, :]
    xn = xv * _head_inv(xv, m_a) * g_ref[...]
    if rope_refs is not None:
        xn = xn * rope_refs[0][rows, :] + _swap32(xn) * rope_refs[1][rows, :]
    return xn


def _bias_lane(h):
    return HEAD_DIM if h == 0 else 0


def _k_for_head(kn, kb_row, h, m_h, lane, blk):
    out = jnp.where(m_h, kn, 0.0)
    if kb_row is not None:
        col = jnp.transpose(jnp.broadcast_to(kb_row, (LANES, blk)))
        hi = col.astype(BF16).astype(F32)
        mid = (col - hi).astype(BF16).astype(F32)
        lo = col - hi - mid
        b = _bias_lane(h)
        out = jnp.where(lane == b, hi, jnp.where(lane == b + 1, mid, jnp.where(lane == b + 2, lo, out)))
    return out.astype(BF16)


def _q_for_head(qb, h, m_h, lane, biased):
    out = jnp.where(m_h, qb, 0)
    if biased:
        b = _bias_lane(h)
        out = jnp.where((lane >= b) & (lane < b + 3), jnp.ones_like(out), out)
    return out


def _head_rows(x, parts=3):
    rr = lax.broadcasted_iota(jnp.int32, (8, LANES), 0)
    ll = lax.broadcasted_iota(jnp.int32, (8, LANES), 1)
    sel = jnp.where(((rr == 0) & (ll < HEAD_DIM)) | ((rr == 1) & (ll >= HEAD_DIM)), 1.0, 0.0).astype(BF16)
    acc = None
    rem = x
    for i in range(parts):
        xi = rem.astype(BF16)
        t = lax.dot_general(sel, xi, NT, preferred_element_type=F32)
        acc = t if acc is None else acc + t
        if i + 1 < parts:
            rem = rem - xi.astype(F32)
    return acc


def _cumdot_left(u, x, parts):
    acc = None
    rem = x
    for i in range(parts):
        xi = rem.astype(BF16)
        t = jnp.dot(u, xi, preferred_element_type=F32)
        acc = t if acc is None else acc + t
        if i + 1 < parts:
            rem = rem - xi.astype(F32)
    return acc


def _q_minus_k(bk, bq):
    return lax.broadcasted_iota(jnp.int32, (bk, bq), 1) - lax.broadcasted_iota(jnp.int32, (bk, bq), 0)


def _stat_spec(nb, blk):
    return pl.BlockSpec((None, 2, nb, 1, blk), lambda z, p: (z, p, 0, 0, 0))


def _attn_fwd(src, offs, npairs, gq, gk, *, rope=None, kbias=None, window, name, cargo=None):
    zs, n, _ = src.shape
    bq, bk = _att_blocks(n)
    nq, nk, rq = n // bq, n // bk, bq // bk
    full = window >= n
    wblk = -(-window // bk)
    biased = kbias is not None

    def body(*refs):
        it = iter(refs)
        q_ref, k_ref, v_ref, gq_ref, gk_ref = (next(it) for _ in range(5))
        rope_refs = (next(it), next(it)) if rope is not None else None
        kb_ref = next(it) if biased else None
        o_ref, lse_ref, qn_s, kh_s, vt_s, acc_s, m_s = (next(it) for _ in range(7))
        m_a = _mask_a()
        masks = (m_a, jnp.logical_not(m_a))
        lane = lax.broadcasted_iota(jnp.int32, (1, LANES), 1)
        row = lax.broadcasted_iota(jnp.int32, (LANES, 1), 0)

        def prep(c):
            rows = _rows(c, bk)
            qn_s[rows, :] = (_normed(q_ref, gq_ref, rope_refs, rows, m_a) * SCALE).astype(BF16)
            kn = _normed(k_ref, gk_ref, rope_refs, rows, m_a)
            vt = jnp.transpose(v_ref[rows, :])
            for h in (0, 1):
                kh_s[h, rows, :] = _k_for_head(kn, kb_ref[h, c] if biased else None, h, masks[h], lane, bk)
                vt_s[h, c] = jnp.where(row == _bias_lane(h), 1.0, vt).astype(BF16)

        _loop(0, nk, prep)
        qk = _q_minus_k(bk, bq)

        def qblock(qi):
            rows = _rows(qi, bq)
            qb = qn_s[rows, :]
            qh = [_q_for_head(qb, h, masks[h], lane, biased) for h in (0, 1)]
            m_s[...] = jnp.full(m_s.shape, NEG, F32)
            acc_s[...] = jnp.zeros_like(acc_s)

            def step(kj, masked):
                cols = _rows(kj, bk)
                sts = [lax.dot_general(kh_s[h, cols, :], qh[h], NT, preferred_element_type=F32) for h in (0, 1)]
                old = [(m_s[h], acc_s[h]) for h in (0, 1)]
                if masked:
                    d = qk + (qi * bq - kj * bk)
                    ok = (d >= 0) & (d <= window)
                    sts = [jnp.where(ok, st, NEG) for st in sts]
                new = []
                for h in (0, 1):
                    m, acc = old[h]
                    m2 = jnp.maximum(m, jnp.max(sts[h], axis=0, keepdims=True))
                    pt = jnp.exp(sts[h] - m2).astype(BF16)
                    new.append((m2, jnp.exp(m - m2) * acc + jnp.dot(vt_s[h, kj], pt, preferred_element_type=F32)))
                for h in (0, 1):
                    m_s[h], acc_s[h] = new[h]

            if full:
                _loop(0, qi * rq, lambda kj: step(kj, False))
                _loop(qi * rq, (qi + 1) * rq, lambda kj: step(kj, True))
            else:
                _loop(jnp.maximum(qi * rq - wblk, 0), (qi + 1) * rq, lambda kj: step(kj, True))
            outs = []
            for h in (0, 1):
                acc_t = acc_s[h]
                den = acc_t[_bias_lane(h):_bias_lane(h) + 1, :]
                outs.append(jnp.transpose(acc_t / den))
                lse_ref[h, qi] = m_s[h] + jnp.log(den)
            o_ref[rows, :] = jnp.where(m_a, outs[0], outs[1])

        _loop(0, nq, qblock)

    vec = pl.BlockSpec((1, LANES), lambda z, p: (0, 0))
    ins = [src, src, src, gq, gk]
    specs = [_col_spec(n, offs[0]), _col_spec(n, offs[1]), _col_spec(n, offs[2]), vec, vec]
    if rope is not None:
        ins += list(rope)
        specs += [pl.BlockSpec((None, n, LANES), lambda z, p: (z, 0, 0))] * 2
    if biased:
        ins.append(kbias)
        specs.append(_stat_spec(nk, bk))
    scratch = [pltpu.VMEM((n, LANES), BF16), pltpu.VMEM((2, n, LANES), BF16), pltpu.VMEM((2, nk, LANES, bk), BF16)]
    scratch += [pltpu.VMEM((2, LANES, bq), F32), pltpu.VMEM((2, 1, bq), F32)]
    return _call(
        body, cargo=cargo, name=name, grid=(zs, npairs), in_specs=specs, out_specs=[_col_spec(n, 0), _stat_spec(nq, bq)],
        out_shape=[jax.ShapeDtypeStruct((zs, n, LANES * npairs), F32), jax.ShapeDtypeStruct((zs, 2 * npairs, nq, 1, bq), F32)],
        scratch_shapes=scratch, compiler_params=_params(("parallel", "parallel")),
    )(*ins)


def _attn_bwd(src, offs, npairs, gq, gk, o, do, lse, *, rope=None, kbias=None, window, name, cargo=None):
    zs, n, _ = src.shape
    bq, bk = _att_blocks(n)
    nq, nk, rq = n // bq, n // bk, bq // bk
    full = window >= n
    wblk = -(-window // bk)
    biased = kbias is not None

    def body(*refs):
        it = iter(refs)
        q_ref, k_ref, v_ref, gq_ref, gk_ref, o_ref, do_ref, lse_ref = (next(it) for _ in range(8))
        rope_refs = (next(it), next(it)) if rope is not None else None
        kb_ref = next(it) if biased else None
        dq_ref, dk_ref, dv_ref, dg_ref = (next(it) for _ in range(4))
        dkb_ref = next(it) if biased else None
        qn_s, kh_s, vb_s, kt_s, dqn_s, dkh_s, dv_s, dq_s, rs_s = (next(it) for _ in range(9))
        m_a = _mask_a()
        masks = (m_a, jnp.logical_not(m_a))
        lane = lax.broadcasted_iota(jnp.int32, (1, LANES), 1)

        @pl.when((pl.program_id(0) == 0) & (pl.program_id(1) == 0))
        def _():
            dg_ref[...] = jnp.zeros_like(dg_ref)

        def prep(c):
            rows = _rows(c, bk)
            qn_s[rows, :] = (_normed(q_ref, gq_ref, rope_refs, rows, m_a) * SCALE).astype(BF16)
            kn = _normed(k_ref, gk_ref, rope_refs, rows, m_a)
            kt_s[c] = jnp.transpose(kn).astype(BF16)
            vb_s[rows, :] = v_ref[rows, :].astype(BF16)
            for h in (0, 1):
                kh_s[h, rows, :] = _k_for_head(kn, kb_ref[h, c] if biased else None, h, masks[h], lane, bk)

        _loop(0, nk, prep)
        dkh_s[...] = jnp.zeros_like(dkh_s)
        dv_s[...] = jnp.zeros_like(dv_s)
        qk = _q_minus_k(bk, bq)

        def qblock(qi):
            rows = _rows(qi, bq)
            qb = qn_s[rows, :]
            dob = do_ref[rows, :]
            delta = _head_rows(dob * o_ref[rows, :])
            qh = [_q_for_head(qb, h, masks[h], lane, biased) for h in (0, 1)]
            doms = [jnp.where(masks[h], dob, 0.0).astype(BF16) for h in (0, 1)]
            lses = [lse_ref[h, qi] for h in (0, 1)]
            dq_s[...] = jnp.zeros_like(dq_s)
            if biased:
                for h in (0, 1):
                    rs_s[h, qi] = jnp.zeros((1, bq), F32)

            def step(kj, masked):
                cols = _rows(kj, bk)
                vb = vb_s[cols, :]
                kt = kt_s[kj]
                sts = [lax.dot_general(kh_s[h, cols, :], qh[h], NT, preferred_element_type=F32) for h in (0, 1)]
                dpts = [lax.dot_general(vb, doms[h], NT, preferred_element_type=F32) for h in (0, 1)]
                if masked:
                    d = qk + (qi * bq - kj * bk)
                    ok = (d >= 0) & (d <= window)
                    sts = [jnp.where(ok, st, NEG) for st in sts]
                new = []
                for h in (0, 1):
                    pt = jnp.exp(sts[h] - lses[h])
                    dst = pt * (dpts[h] - delta[h:h + 1, :])
                    dsb = dst.astype(BF16)
                    tk = jnp.dot(dsb, qh[h], preferred_element_type=F32)
                    if biased:
                        tk = tk + jnp.dot((dst - dsb.astype(F32)).astype(BF16), qh[h], preferred_element_type=F32)
                    tv = jnp.dot(pt.astype(BF16), doms[h], preferred_element_type=F32)
                    tq = jnp.dot(kt, dsb, preferred_element_type=F32)
                    new.append((tk, tv, tq, jnp.sum(dst, axis=0, keepdims=True) if biased else None))
                for h in (0, 1):
                    dkh_s[h, cols, :] += new[h][0]
                    dq_s[h] += new[h][2]
                    if biased:
                        rs_s[h, qi] += new[h][3]
                dv_s[cols, :] += new[0][1] + new[1][1]

            if full:
                _loop(0, qi * rq, lambda kj: step(kj, False))
                _loop(qi * rq, (qi + 1) * rq, lambda kj: step(kj, True))
            else:
                _loop(jnp.maximum(qi * rq - wblk, 0), (qi + 1) * rq, lambda kj: step(kj, True))
            dqn_s[rows, :] = jnp.where(m_a, jnp.transpose(dq_s[0]), jnp.transpose(dq_s[1])) * SCALE

        _loop(0, nq, qblock)

        def finish(c, carry):
            rows = _rows(c, bq)
            out = []
            dk_pair = [dkh_s[0, rows, :], dkh_s[1, rows, :]]
            if biased:
                for h in (0, 1):
                    b = _bias_lane(h)
                    dkb_row = jnp.transpose(dk_pair[h])[b:b + 1, :] - rs_s[h, c]
                    for j in range(rq):
                        dkb_ref[h, c * rq + j] = dkb_row[:, j * bk:(j + 1) * bk]
            dv_ref[rows, :] = dv_s[rows, :].astype(BF16)
            grads = (dqn_s[rows, :], jnp.where(m_a, dk_pair[0], dk_pair[1]))
            for src_ref, g_ref, dxn, dst in ((q_ref, gq_ref, grads[0], dq_ref), (k_ref, gk_ref, grads[1], dk_ref)):
                xv = src_ref[rows, :]
                inv = _head_inv(xv, m_a)
                y = xv * inv
                if rope_refs is not None:
                    dxn = dxn * rope_refs[0][rows, :] + _swap32(dxn * rope_refs[1][rows, :])
                dy = dxn * g_ref[...]
                dst[rows, :] = (inv * (dy - y * (_half_sum(dy * y, m_a) * (1.0 / HEAD_DIM)))).astype(BF16)
                out.append(jnp.sum(dxn * y, axis=0, keepdims=True))
            return carry[0] + out[0], carry[1] + out[1]

        zero = jnp.zeros((1, LANES), F32)
        dgq, dgk = lax.fori_loop(0, nq, finish, (zero, zero))
        dg_ref[0:1, :] += dgq
        dg_ref[1:2, :] += dgk

    vec = pl.BlockSpec((1, LANES), lambda z, p: (0, 0))
    ospec = _col_spec(n, 0)
    ins = [src, src, src, gq, gk, o, do, lse]
    specs = [_col_spec(n, offs[0]), _col_spec(n, offs[1]), _col_spec(n, offs[2]), vec, vec, ospec, ospec, _stat_spec(nq, bq)]
    if rope is not None:
        ins += list(rope)
        specs += [pl.BlockSpec((None, n, LANES), lambda z, p: (z, 0, 0))] * 2
    sds = jax.ShapeDtypeStruct((zs, n, LANES * npairs), BF16)
    out_shape = [sds, sds, sds, jax.ShapeDtypeStruct((8, LANES), F32)]
    out_specs = [ospec, ospec, ospec, pl.BlockSpec((8, LANES), lambda z, p: (0, 0))]
    if biased:
        ins.append(kbias)
        specs.append(_stat_spec(nk, bk))
        out_shape.append(jax.ShapeDtypeStruct(kbias.shape, F32))
        out_specs.append(_stat_spec(nk, bk))
    scratch = [pltpu.VMEM((n, LANES), BF16), pltpu.VMEM((2, n, LANES), BF16), pltpu.VMEM((n, LANES), BF16)]
    scratch += [pltpu.VMEM((nk, LANES, bk), BF16), pltpu.VMEM((n, LANES), F32), pltpu.VMEM((2, n, LANES), F32)]
    scratch += [pltpu.VMEM((n, LANES), F32), pltpu.VMEM((2, LANES, bq), F32), pltpu.VMEM((2, nq, 1, bq), F32)]
    return _call(
        body, cargo=cargo, name=name, grid=(zs, npairs), in_specs=specs, out_specs=out_specs, out_shape=out_shape,
        scratch_shapes=scratch, compiler_params=_params(("arbitrary", "arbitrary")),
    )(*ins)


SB_LOG_PARTS = 2
SB_GRAD_PARTS = 1


def _log_sig_pair(z, with_sigmoid=False):
    e = jnp.exp(-jnp.abs(z))
    den = 1.0 + e
    lsn = -(jnp.maximum(z, 0.0) + jnp.log(den))
    if with_sigmoid:
        return lsn, z + lsn, jnp.where(z >= 0.0, 1.0, e) / den
    return lsn, z + lsn


def _sb_fwd(src, offs, npairs, *, name, cargo=None):
    zs, n, _ = src.shape
    bq, bk = _att_blocks(n)
    nq, nk, rq = n // bq, n // bk, bq // bk

    def body(q_ref, k_ref, v_ref, o_ref, lt_ref, qs_s, kb_s, vt_s, acc_s, c_s):
        m_a = _mask_a()
        masks = (m_a, jnp.logical_not(m_a))

        def prep(c):
            rows = _rows(c, bk)
            qs_s[rows, :] = (q_ref[rows, :] * SCALE).astype(BF16)
            kb_s[rows, :] = k_ref[rows, :].astype(BF16)
            vt_s[c] = jnp.transpose(v_ref[rows, :]).astype(BF16)

        _loop(0, nk, prep)
        qk = _q_minus_k(bk, bq)
        u_gt = _tri(bk, lambda r, c: c > r)

        def qblock(qi):
            rows = _rows(qi, bq)
            qb = qs_s[rows, :]
            qms = [jnp.where(masks[h], qb, 0) for h in (0, 1)]
            acc_s[...] = jnp.zeros_like(acc_s)
            c_s[...] = jnp.zeros_like(c_s)

            def step(kj, masked):
                kb, vt = kb_s[_rows(kj, bk), :], vt_s[kj]
                zts = [lax.dot_general(kb, qms[h], NT, preferred_element_type=F32) for h in (0, 1)]
                old = [c_s[h] for h in (0, 1)]
                if masked:
                    ok = (qk + (qi * bq - kj * bk)) > 0
                new = []
                for h in (0, 1):
                    lsn, lsp = _log_sig_pair(zts[h])
                    if masked:
                        lsn = jnp.where(ok, lsn, 0.0)
                    at = jnp.exp(lsp + (old[h] + _cumdot_left(u_gt, lsn, SB_LOG_PARTS)))
                    if masked:
                        at = jnp.where(ok, at, 0.0)
                    new.append((jnp.dot(vt, at.astype(BF16), preferred_element_type=F32),
                                old[h] + jnp.sum(lsn, axis=0, keepdims=True)))
                for h in (0, 1):
                    acc_s[h] += new[h][0]
                    c_s[h] = new[h][1]

            _loop(0, rq, lambda t: step((qi + 1) * rq - 1 - t, True))
            _loop(0, qi * rq, lambda t: step(qi * rq - 1 - t, False))
            o_ref[rows, :] = jnp.where(m_a, jnp.transpose(acc_s[0]), jnp.transpose(acc_s[1]))
            for h in (0, 1):
                lt_ref[h, qi] = c_s[h]

        _loop(0, nq, qblock)

    scratch = [pltpu.VMEM((n, LANES), BF16)] * 2 + [pltpu.VMEM((nk, LANES, bk), BF16)]
    scratch += [pltpu.VMEM((2, LANES, bq), F32), pltpu.VMEM((2, 1, bq), F32)]
    return _call(
        body, cargo=cargo, name=name, grid=(zs, npairs),
        in_specs=[_col_spec(n, offs[0]), _col_spec(n, offs[1]), _col_spec(n, offs[2])],
        out_specs=[_col_spec(n, 0), _stat_spec(nq, bq)],
        out_shape=[jax.ShapeDtypeStruct((zs, n, LANES * npairs), F32), jax.ShapeDtypeStruct((zs, 2 * npairs, nq, 1, bq), F32)],
        scratch_shapes=scratch, compiler_params=_params(("parallel", "parallel")),
    )(src, src, src)


def _sb_bwd(src, offs, npairs, do, ltot, *, name, cargo=None):
    zs, n, _ = src.shape
    bq, bk = _att_blocks(n)
    nq, nk, rq = n // bq, n // bk, bq // bk

    def body(q_ref, k_ref, v_ref, do_ref, lt_ref, dq_ref, dk_ref, dv_ref, qs_s, kb_s, vb_s, kt_s, dk_s, dv_s, dq_s, lp_s, ep_s):
        m_a = _mask_a()
        masks = (m_a, jnp.logical_not(m_a))

        def prep(c):
            rows = _rows(c, bk)
            qs_s[rows, :] = (q_ref[rows, :] * SCALE).astype(BF16)
            kv = k_ref[rows, :]
            kb_s[rows, :] = kv.astype(BF16)
            kt_s[c] = jnp.transpose(kv).astype(BF16)
            vb_s[rows, :] = v_ref[rows, :].astype(BF16)

        _loop(0, nk, prep)
        dk_s[...] = jnp.zeros_like(dk_s)
        dv_s[...] = jnp.zeros_like(dv_s)
        qk = _q_minus_k(bk, bq)
        u_le = _tri(bk, lambda r, c: c <= r)
        u_lt = _tri(bk, lambda r, c: c < r)

        def qblock(qi):
            rows = _rows(qi, bq)
            qb = qs_s[rows, :]
            dob = do_ref[rows, :]
            qms = [jnp.where(masks[h], qb, 0) for h in (0, 1)]
            doms = [jnp.where(masks[h], dob, 0.0).astype(BF16) for h in (0, 1)]
            lts = [lt_ref[h, qi] for h in (0, 1)]
            dq_s[...] = jnp.zeros_like(dq_s)
            lp_s[...] = jnp.zeros_like(lp_s)
            ep_s[...] = jnp.zeros_like(ep_s)

            def step(kj, masked):
                cols = _rows(kj, bk)
                kb, vb, kt = kb_s[cols, :], vb_s[cols, :], kt_s[kj]
                zts = [lax.dot_general(kb, qms[h], NT, preferred_element_type=F32) for h in (0, 1)]
                dats = [lax.dot_general(vb, doms[h], NT, preferred_element_type=F32) for h in (0, 1)]
                old = [(lp_s[h], ep_s[h]) for h in (0, 1)]
                if masked:
                    ok = (qk + (qi * bq - kj * bk)) > 0
                new = []
                for h in (0, 1):
                    lp, ep = old[h]
                    lsn, lsp, sig = _log_sig_pair(zts[h], with_sigmoid=True)
                    if masked:
                        lsn = jnp.where(ok, lsn, 0.0)
                    at = jnp.exp(lsp + (lts[h] - (lp + _cumdot_left(u_le, lsn, SB_LOG_PARTS))))
                    if masked:
                        at = jnp.where(ok, at, 0.0)
                    et = dats[h] * at
                    big_e = ep + _cumdot_left(u_lt, et, SB_GRAD_PARTS)
                    dzt = et * (1.0 - sig) - big_e * sig
                    if masked:
                        dzt = jnp.where(ok, dzt, 0.0)
                    dzb = dzt.astype(BF16)
                    new.append((jnp.dot(dzb, qms[h], preferred_element_type=F32),
                                jnp.dot(at.astype(BF16), doms[h], preferred_element_type=F32),
                                jnp.dot(kt, dzb, preferred_element_type=F32),
                                lp + jnp.sum(lsn, axis=0, keepdims=True), ep + jnp.sum(et, axis=0, keepdims=True)))
                for h in (0, 1):
                    dq_s[h] += new[h][2]
                    lp_s[h], ep_s[h] = new[h][3], new[h][4]
                dk_s[cols, :] += new[0][0] + new[1][0]
                dv_s[cols, :] += new[0][1] + new[1][1]

            _loop(0, qi * rq, lambda kj: step(kj, False))
            _loop(qi * rq, (qi + 1) * rq, lambda kj: step(kj, True))
            dq_ref[rows, :] = (jnp.where(m_a, jnp.transpose(dq_s[0]), jnp.transpose(dq_s[1])) * SCALE).astype(BF16)

        _loop(0, nq, qblock)

        def store(c):
            rows = _rows(c, bk)
            dk_ref[rows, :] = dk_s[rows, :].astype(BF16)
            dv_ref[rows, :] = dv_s[rows, :].astype(BF16)

        _loop(0, nk, store)

    ospec = _col_spec(n, 0)
    sds = jax.ShapeDtypeStruct((zs, n, LANES * npairs), BF16)
    scratch = [pltpu.VMEM((n, LANES), BF16)] * 3 + [pltpu.VMEM((nk, LANES, bk), BF16)] + [pltpu.VMEM((n, LANES), F32)] * 2
    scratch += [pltpu.VMEM((2, LANES, bq), F32), pltpu.VMEM((2, 1, bq), F32), pltpu.VMEM((2, 1, bq), F32)]
    return _call(
        body, cargo=cargo, name=name, grid=(zs, npairs),
        in_specs=[_col_spec(n, offs[0]), _col_spec(n, offs[1]), _col_spec(n, offs[2]), ospec, _stat_spec(nq, bq)],
        out_specs=[ospec] * 3, out_shape=[sds] * 3, scratch_shapes=scratch,
        compiler_params=_params(("parallel", "parallel")),
    )(src, src, src, do, ltot)


def _fox_gate_fwd(lg, bias, *, name):
    bs, nh, t = lg.shape
    blk = min(LANES, t)

    def body(lg_ref, b_ref, kb_ref):
        u_le = _tri(blk, lambda r, c: r <= c)
        carry = jnp.zeros((nh, 1), F32)
        for j in range(t // blk):
            sl = slice(j * blk, (j + 1) * blk)
            xv = lg_ref[:, sl] + b_ref[...]
            lf = jnp.minimum(xv, 0.0) - jnp.log(1.0 + jnp.exp(-jnp.abs(xv)))
            kb_ref[:, sl] = -(carry + _cumdot(lf, u_le, 3))
            carry = carry + jnp.sum(lf, axis=1, keepdims=True)

    spec = pl.BlockSpec((None, nh, t), lambda i: (i, 0, 0))
    return _pcall(
        body, name=name, grid=(bs,), in_specs=[spec, pl.BlockSpec((nh, 1), lambda i: (0, 0))], out_specs=spec,
        out_shape=jax.ShapeDtypeStruct((bs, nh, t), F32), compiler_params=_params(("parallel",)),
    )(lg, bias)


def _fox_gate_bwd(dkb, lg, bias, *, name):
    bs, nh, t = lg.shape
    blk = min(LANES, t)

    def body(dkb_ref, lg_ref, b_ref, dlg_ref, db_ref):
        @pl.when(pl.program_id(0) == 0)
        def _():
            db_ref[...] = jnp.zeros_like(db_ref)

        u_ge = _tri(blk, lambda r, c: r >= c)
        carry = jnp.zeros((nh, 1), F32)
        tot = jnp.zeros((nh, 1), F32)
        for j in reversed(range(t // blk)):
            sl = slice(j * blk, (j + 1) * blk)
            df = -dkb_ref[:, sl]
            dlf = carry + _cumdot(df, u_ge, 3)
            carry = carry + jnp.sum(df, axis=1, keepdims=True)
            xv = lg_ref[:, sl] + b_ref[...]
            dlg = dlf * jax.nn.sigmoid(-xv)
            dlg_ref[:, sl] = dlg
            tot = tot + jnp.sum(dlg, axis=1, keepdims=True)
        db_ref[...] += jnp.broadcast_to(tot, db_ref.shape)

    spec = pl.BlockSpec((None, nh, t), lambda i: (i, 0, 0))
    return _pcall(
        body, name=name, grid=(bs,), in_specs=[spec, spec, pl.BlockSpec((nh, 1), lambda i: (0, 0))],
        out_specs=[spec, pl.BlockSpec((nh, LANES), lambda i: (0, 0))],
        out_shape=[jax.ShapeDtypeStruct((bs, nh, t), F32), jax.ShapeDtypeStruct((nh, LANES), F32)],
        compiler_params=_params(("arbitrary",)),
    )(dkb, lg, bias)


def _place():
    return lax.axis_index("x"), lax.axis_index("y"), lax.axis_index("c")


def _flip(v, f):
    return 1 - v if f else v


FLIPS = [(fx, fy, fc) for fx in (0, 1) for fy in (0, 1) for fc in (0, 1)][1:]


def _comm_sems(nw):
    return [pltpu.SemaphoreType.DMA((7, nw)), pltpu.SemaphoreType.DMA((7, nw)), pltpu.SemaphoreType.DMA((nw,))]


def _gather_cargo(shards, on_done):
    nw = len(shards)

    def parts(x_refs, out_refs, sems):
        send_sems, recv_sems, local_sems = sems
        x, y, cc = _place()
        me, sibling = (x, y, cc), (x, y, 1 - cc)
        chips = [(1 - x, y), (x, 1 - y), (1 - x, 1 - y)]

        def slot(i, px, py, pc):
            return out_refs[i].at[4 * px + 2 * py + pc]

        def copy(i, k, block, to, src=None):
            return pltpu.make_async_remote_copy(
                src_ref=slot(i, *block) if src is None else src, dst_ref=slot(i, *block),
                send_sem=send_sems.at[k, i], recv_sem=recv_sems.at[k, i], device_id=to, device_id_type=MESH)

        mine = [pltpu.make_async_copy(x_refs[i], slot(i, *me), local_sems.at[i]) for i in range(nw)]
        first = []
        for i in range(nw):
            first.append(copy(i, 0, me, sibling, src=x_refs[i]))
            first += [copy(i, 1 + j, me, (*chip, cc), src=x_refs[i]) for j, chip in enumerate(chips)]
        return me, sibling, chips, cc, copy, mine, first

    def start(x_refs, out_refs, sems):
        *_, mine, first = parts(x_refs, out_refs, sems)
        for cp in mine + first:
            cp.start()

    def finish(x_refs, out_refs, sems):
        me, sibling, chips, cc, copy, mine, first = parts(x_refs, out_refs, sems)
        passed = []
        for i in range(nw):
            for j, chip in enumerate(chips):
                copy(i, 1 + j, (*chip, cc), me).wait_recv()
                passed.append(copy(i, 4 + j, (*chip, cc), sibling))
                passed[-1].start()
        for i in range(nw):
            copy(i, 0, sibling, me).wait_recv()
            for j, chip in enumerate(chips):
                copy(i, 4 + j, (*chip, 1 - cc), me).wait_recv()
        for cp in first + passed:
            cp.wait_send()
        for cp in mine:
            cp.wait()

    out_shape = [jax.ShapeDtypeStruct((N_DEV, *s.shape), s.dtype) for s in shards]
    return _Cargo(shards, out_shape, _comm_sems(nw), start, finish, on_done)


def _scatter_cargo(slots, prev, layer, depth, on_done):
    nw = len(slots)

    def parts(refs, recv_refs, sems):
        g_refs = refs[:nw]
        send_sems, recv_sems, local_sems = sems
        x, y, cc = _place()
        my = 4 * x + 2 * y + cc
        mine, copies = [], []
        for i in range(nw):
            mine.append(pltpu.make_async_copy(g_refs[i].at[my], recv_refs[i].at[my, layer], local_sems.at[i]))
            for k, (fx, fy, fc) in enumerate(FLIPS):
                px, py, pc = _flip(x, fx), _flip(y, fy), _flip(cc, fc)
                copies.append(pltpu.make_async_remote_copy(
                    src_ref=g_refs[i].at[4 * px + 2 * py + pc], dst_ref=recv_refs[i].at[my, layer],
                    send_sem=send_sems.at[k, i], recv_sem=recv_sems.at[k, i], device_id=(px, py, pc), device_id_type=MESH))
        return mine, copies

    def start(refs, recv_refs, sems):
        mine, copies = parts(refs, recv_refs, sems)
        for cp in mine + copies:
            cp.start()

    def finish(refs, recv_refs, sems):
        mine, copies = parts(refs, recv_refs, sems)
        for cp in copies:
            cp.wait_recv()
        for cp in copies:
            cp.wait_send()
        for cp in mine:
            cp.wait()

    ins, aliases = list(slots), {}
    for i, p in enumerate(prev):
        if p is not None:
            aliases[len(ins)] = i
            ins.append(p)
    out_shape = [jax.ShapeDtypeStruct((N_DEV, depth, *s.shape[1:]), s.dtype) for s in slots]
    return _Cargo(ins, out_shape, _comm_sems(nw), start, finish, on_done, aliases)


def _exchange(cargo, *, name):
    def body(*refs):
        c_in = len(cargo.ins)
        c_out = len(cargo.out_shape)
        cargo.start(refs[:c_in], refs[c_in:c_in + c_out], refs[c_in + c_out:])
        cargo.finish(refs[:c_in], refs[c_in:c_in + c_out], refs[c_in + c_out:])

    hbm = pl.BlockSpec(memory_space=pl.ANY)
    res = _pcall(
        body, name=name, in_specs=[hbm] * len(cargo.ins), out_specs=[hbm] * len(cargo.out_shape), out_shape=cargo.out_shape,
        scratch_shapes=cargo.sems, input_output_aliases=dict(cargo.aliases),
    )(*cargo.ins)
    cargo.on_done(list(res))


def _allreduce_small(blob, *, name):
    r, c = blob.shape

    def body(x_ref, out_ref, buf, send_sems, recv_sems):
        x, y, cc = _place()
        my = 4 * x + 2 * y + cc
        copies = []
        for k, (fx, fy, fc) in enumerate(FLIPS):
            peer = (_flip(x, fx), _flip(y, fy), _flip(cc, fc))
            copies.append(pltpu.make_async_remote_copy(
                src_ref=x_ref, dst_ref=buf.at[my], send_sem=send_sems.at[k], recv_sem=recv_sems.at[k],
                device_id=peer, device_id_type=MESH))
        for cp in copies:
            cp.start()
        buf[my] = x_ref[...]
        for cp in copies:
            cp.wait_recv()
        for cp in copies:
            cp.wait_send()
        acc = buf[0]
        for i in range(1, N_DEV):
            acc = acc + buf[i]
        out_ref[...] = acc

    vmem = pl.BlockSpec(memory_space=pltpu.VMEM)
    return _pcall(
        body, name=name, in_specs=[vmem], out_specs=vmem, out_shape=jax.ShapeDtypeStruct((r, c), F32),
        scratch_shapes=[pltpu.VMEM((N_DEV, r, c), F32), pltpu.SemaphoreType.DMA((7,)), pltpu.SemaphoreType.DMA((7,))],
    )(blob)


BIG = ("w_in", "w_mlp_in", "w_mlp_out", "w_up_fox", "w_up_sb", "w_up_dil", "w_out")
ROW_SHARDED = ("w_out", "w_mlp_out")
SMALL = ("attn_norm", "b_forget", "q_norm_fox", "k_norm_fox", "q_norm_dil", "k_norm_dil", "mlp_norm")
BLOB_ROWS = 512


def _pack(parts, dtype):
    flat = jnp.concatenate([p.reshape(-1).astype(dtype) for p in parts])
    size = -(-flat.shape[0] // (BLOB_ROWS * LANES)) * (BLOB_ROWS * LANES)
    return jnp.pad(flat, (0, size - flat.shape[0])).reshape(-1, LANES)


def _unpack(blob, shapes):
    flat = blob.reshape(-1)
    out, off = [], 0
    for shp in shapes:
        size = 1
        for s in shp:
            size *= s
        out.append(flat[off:off + size].reshape(shp))
        off += size
    return out


def _join_shards(name, sh):
    if name in ROW_SHARDED:
        return sh.reshape(-1, sh.shape[2])
    return jnp.transpose(sh, (1, 0, 2)).reshape(sh.shape[1], -1)


def _split_shards(name, full):
    a, b = full.shape
    if name in ROW_SHARDED:
        return full.reshape(N_DEV, a // N_DEV, b)
    return jnp.transpose(full.reshape(a, N_DEV, b // N_DEV), (1, 0, 2))


def _permute_in(w, dp):
    o1 = 3 * W_FOX
    o2 = o1 + N_HEADS_FOX
    pad = [(0, 0)] * (w.ndim - 1) + [(0, dp - w.shape[-1])]
    return jnp.pad(jnp.concatenate([w[..., :o1], w[..., o2:], w[..., o1:o2]], axis=-1), pad)


def _unpermute_in(wp, d_in):
    o1 = 3 * W_FOX
    fg = d_in - N_HEADS_FOX
    return jnp.concatenate([wp[..., :o1], wp[..., fg:d_in], wp[..., o1:fg]], axis=-1)


def _to_streams(a, r):
    b, t, c = a.shape
    if r == 1:
        return a
    return jnp.transpose(a.reshape(b, t // r, r, c), (0, 2, 1, 3)).reshape(b * r, t // r, c)


def _from_streams(a, r, b):
    if r == 1:
        return a
    z, n, c = a.shape
    return jnp.transpose(a.reshape(b, r, n, c), (0, 2, 1, 3)).reshape(b, n * r, c)


def _stat_to_tokens(st, r, b):
    hh = st.shape[1]
    n = st.shape[2] * st.shape[4]
    return jnp.transpose(st.reshape(b, r, hh, n), (0, 2, 3, 1)).reshape(b, hh, n * r)


def _stat_to_streams(tok, r, blk):
    b, hh, t = tok.shape
    n = t // r
    return jnp.transpose(tok.reshape(b, hh, n, r), (0, 3, 1, 2)).reshape(b * r, hh, n // blk, 1, blk)


def _rope_tables(positions):
    half = HEAD_DIM // 2
    inv = 1.0 / (ROPE_THETA ** (jnp.arange(half, dtype=F32) / half))
    ang = positions.astype(F32)[..., None] * inv
    cos, sin = jnp.cos(ang), jnp.sin(ang)
    return jnp.tile(cos, (1, 1, 4)), jnp.tile(jnp.concatenate([-sin, sin], axis=-1), (1, 1, 2))


def _gain2(g):
    return jnp.tile(g.reshape(1, HEAD_DIM), (1, 2))


def _layer_fwd(l, x, w, small, ropes, bl, t, cargo):
    n, d = x.shape
    s = {}
    s["x"] = x
    s["h"] = _rmsnorm_fwd(x, small["attn_norm"][l].reshape(1, d), name=f"norm_attn_fwd{l}")
    proj = _mm(s["h"], w["w_in"][l], name=f"mm_proj{l}")
    s["proj"] = proj
    dp = proj.shape[1]
    proj3 = proj.reshape(bl, t, dp)
    p_fg = P_GATE + 3 * d

    lg = jnp.transpose(proj3[:, :, p_fg:p_fg + N_HEADS_FOX], (0, 2, 1))
    s["lg"] = lg
    kb = _fox_gate_fwd(lg, small["b_forget"][l].reshape(N_HEADS_FOX, 1), name=f"fox_gate_fwd{l}")
    blk = _att_blk(t)
    kb5 = kb.reshape(bl, N_HEADS_FOX, t // blk, 1, blk)
    s["kb5"] = kb5
    gqf, gkf = _gain2(small["q_norm_fox"][l]), _gain2(small["k_norm_fox"][l])
    fo = P_FOX // LANES
    fox_offs = (fo, fo + W_FOX // LANES, fo + 2 * W_FOX // LANES)
    out_a, lse_a = _attn_fwd(proj3, fox_offs, N_HEADS_FOX // 2, gqf, gkf, kbias=kb5, window=t, name=f"fox_fwd{l}",
                             cargo=cargo.get("fox_fwd"))
    s["out_a"], s["lse_a"] = out_a, lse_a

    so = P_SB // LANES
    sb_offs = (so, so + W_SB // LANES, so + 2 * W_SB // LANES)
    out_b, lt_b = _sb_fwd(proj3, sb_offs, N_HEADS_SB // 2, name=f"sb_fwd{l}", cargo=cargo.get("sb_fwd"))
    s["out_b"], s["lt_b"] = out_b, lt_b

    gqd, gkd = _gain2(small["q_norm_dil"][l]), _gain2(small["k_norm_dil"][l])
    os_, lses, s["dil_src"] = [], [], []
    for g, (window, r) in enumerate(DIL_PATTERNS):
        c0 = P_DIL + g * W_DIL
        if r == 1:
            src, offs = proj3, (c0 // LANES, (c0 + W_DILQ) // LANES, (c0 + 2 * W_DILQ) // LANES)
        else:
            qkv = jnp.concatenate([proj3[:, :, c0 + i * W_DILQ:c0 + i * W_DILQ + W_DIL] for i in range(3)], axis=-1)
            src, offs = _to_streams(qkv, r), (0, W_DIL // LANES, 2 * W_DIL // LANES)
        s["dil_src"].append((src, offs))
        o_g, lse_g = _attn_fwd(src, offs, N_HEADS_DIL // 2, gqd, gkd, rope=ropes[g], window=window // r, name=f"dil_fwd{l}_{g}")
        os_.append(_from_streams(o_g, r, bl).reshape(n, W_DIL))
        lses.append(_stat_to_tokens(lse_g, r, bl).reshape(bl * N_HEADS_DIL, t))
    lse_c, *ws = _dil_weights(lses, name=f"dil_weights{l}")
    ws = [jnp.repeat(jnp.transpose(wg.reshape(bl, N_HEADS_DIL, t), (0, 2, 1)).reshape(n, N_HEADS_DIL), HEAD_DIM, axis=1) for wg in ws]
    out_c = _dil_mix(os_, ws, name=f"dil_mix{l}")
    s["out_c"], s["lse_c"] = out_c, lse_c.reshape(bl, N_HEADS_DIL, t)

    ys = [_mm(out_a.reshape(n, W_FOX), w["w_up_fox"][l], name=f"mm_up_fox{l}"),
          _mm(out_b.reshape(n, W_SB), w["w_up_sb"][l], name=f"mm_up_sb{l}"),
          _mm(out_c, w["w_up_dil"][l], name=f"mm_up_dil{l}")]
    s["ys"] = ys
    s["merged"] = _gate_merge_fwd(proj, ys, name=f"gate_merge_fwd{l}")
    x1 = _mm(s["merged"], w["w_out"][l], add=x, name=f"mm_out{l}")
    s["x1"] = x1

    s["h2"] = _rmsnorm_fwd(x1, small["mlp_norm"][l].reshape(1, d), name=f"norm_mlp_fwd{l}")
    s["u"], s["a"] = _mm(s["h2"], w["w_mlp_in"][l], relu2=True, out_dtype=BF16, name=f"mm_mlp_in{l}")
    x2 = _mm(s["a"], w["w_mlp_out"][l], add=x1, name=f"mm_mlp_out{l}")
    return x2, s


def _layer_bwd(l, dx2, s, w, small, ropes, bl, t, hooks):
    n, d = dx2.shape
    gw, gs = {}, {}

    def cargo(call):
        return hooks[call](gw) if call in hooks else None
    du = _mm(dx2, w["w_mlp_out"][l], tb=True, relu_grad_of=s["u"], out_dtype=BF16, name=f"mm_du{l}")
    gw["w_mlp_out"] = _mm(s["a"], dx2, ta=True, name=f"mm_dw_mlp_out{l}")
    gw["w_mlp_in"] = _mm(s["h2"], du, ta=True, name=f"mm_dw_mlp_in{l}")
    dh2 = _mm(du, w["w_mlp_in"][l], tb=True, name=f"mm_dh2{l}")
    dx1, gs["mlp_norm"] = _rmsnorm_bwd(s["x1"], small["mlp_norm"][l].reshape(1, d), dh2, dx2, name=f"norm_mlp_bwd{l}")

    dmerged = _mm(dx1, w["w_out"][l], tb=True, name=f"mm_dmerged{l}")
    gw["w_out"] = _mm(s["merged"], dx1, ta=True, name=f"mm_dw_out{l}")
    dya, dyb, dyc, dgl0, dgl1, dgl2 = _gate_merge_bwd(s["proj"], s["ys"], dmerged, name=f"gate_merge_bwd{l}")
    out_a2, out_b2 = s["out_a"].reshape(n, W_FOX), s["out_b"].reshape(n, W_SB)
    gw["w_up_fox"] = _mm(out_a2, dya, ta=True, name=f"mm_dw_up_fox{l}")
    gw["w_up_sb"] = _mm(out_b2, dyb, ta=True, name=f"mm_dw_up_sb{l}")
    gw["w_up_dil"] = _mm(s["out_c"], dyc, ta=True, name=f"mm_dw_up_dil{l}")
    dout_a = _mm(dya, w["w_up_fox"][l], tb=True, name=f"mm_dout_a{l}").reshape(bl, t, W_FOX)
    dout_b = _mm(dyb, w["w_up_sb"][l], tb=True, name=f"mm_dout_b{l}").reshape(bl, t, W_SB)
    dout_c = _mm(dyc, w["w_up_dil"][l], tb=True, name=f"mm_dout_c{l}").reshape(bl, t, W_DIL)

    proj3 = s["proj"].reshape(bl, t, -1)
    gqf, gkf = _gain2(small["q_norm_fox"][l]), _gain2(small["k_norm_fox"][l])
    fo = P_FOX // LANES
    fox_offs = (fo, fo + W_FOX // LANES, fo + 2 * W_FOX // LANES)
    dq_a, dk_a, dv_a, dg_a, dkb5 = _attn_bwd(proj3, fox_offs, N_HEADS_FOX // 2, gqf, gkf, s["out_a"], dout_a, s["lse_a"],
                                             kbias=s["kb5"], window=t, name=f"fox_bwd{l}", cargo=cargo("fox_bwd"))
    gs["fox_gains"] = dg_a
    dlg, gs["b_forget"] = _fox_gate_bwd(dkb5.reshape(bl, N_HEADS_FOX, t), s["lg"], small["b_forget"][l].reshape(N_HEADS_FOX, 1),
                                        name=f"fox_gate_bwd{l}")
    so = P_SB // LANES
    sb_offs = (so, so + W_SB // LANES, so + 2 * W_SB // LANES)
    dq_b, dk_b, dv_b = _sb_bwd(proj3, sb_offs, N_HEADS_SB // 2, dout_b, s["lt_b"], name=f"sb_bwd{l}", cargo=cargo("sb_bwd"))
    gqd, gkd = _gain2(small["q_norm_dil"][l]), _gain2(small["k_norm_dil"][l])
    out_c3 = s["out_c"].reshape(bl, t, W_DIL)
    dqs, dks, dvs, dgd = [], [], [], None
    for g, (window, r) in enumerate(DIL_PATTERNS):
        src, offs = s["dil_src"][g]
        lse_g = _stat_to_streams(s["lse_c"], r, _att_blocks(t // r)[0])
        dq_g, dk_g, dv_g, dg_g = _attn_bwd(src, offs, N_HEADS_DIL // 2, gqd, gkd, _to_streams(out_c3, r), _to_streams(dout_c, r),
                                           lse_g, rope=ropes[g], window=window // r, name=f"dil_bwd{l}_{g}")
        dqs.append(_from_streams(dq_g, r, bl))
        dks.append(_from_streams(dk_g, r, bl))
        dvs.append(_from_streams(dv_g, r, bl))
        dgd = dg_g if dgd is None else jnp.concatenate([dgd, dg_g], axis=0)
    gs["dil_gains"] = dgd

    dp = s["proj"].shape[1]
    parts = [dq_a, dk_a, dv_a, dq_b, dk_b, dv_b] + dqs + dks + dvs
    parts = [p.reshape(n, -1).astype(BF16) for p in parts] + [dgl0, dgl1, dgl2, jnp.transpose(dlg, (0, 2, 1)).reshape(n, -1).astype(BF16)]
    used = P_GATE + 3 * d + N_HEADS_FOX
    dproj = jnp.concatenate(parts + [jnp.zeros((n, dp - used), BF16)], axis=1)
    gw["w_in"] = _mm(s["h"], dproj, ta=True, name=f"mm_dw_in{l}")
    dh = _mm(dproj, w["w_in"][l], tb=True, name=f"mm_dh{l}", cargo=cargo("mm_dh"))
    dx, gs["attn_norm"] = _rmsnorm_bwd(s["x"], small["attn_norm"][l].reshape(1, d), dh, dx1, name=f"norm_attn_bwd{l}")
    return dx, gw, gs


def kernel(x, positions, attn_norm, w_in, b_forget, q_norm_fox, k_norm_fox, q_norm_dil, k_norm_dil, w_up_fox, w_up_sb, w_up_dil, w_out, mlp_norm, w_mlp_in, w_mlp_out, loss_target, m_attn_norm, m_w_in, m_b_forget, m_q_norm_fox, m_k_norm_fox, m_q_norm_dil, m_k_norm_dil, m_w_up_fox, m_w_up_sb, m_w_up_dil, m_w_out, m_mlp_norm, m_w_mlp_in, m_w_mlp_out, v_attn_norm, v_w_in, v_b_forget, v_q_norm_fox, v_k_norm_fox, v_q_norm_dil, v_k_norm_dil, v_w_up_fox, v_w_up_sb, v_w_up_dil, v_w_out, v_mlp_norm, v_w_mlp_in, v_w_mlp_out):
    bl, t, d = x.shape
    n = bl * t
    depth = attn_norm.shape[0]
    wl = dict(w_in=w_in, w_up_fox=w_up_fox, w_up_sb=w_up_sb, w_up_dil=w_up_dil, w_out=w_out, w_mlp_in=w_mlp_in, w_mlp_out=w_mlp_out)
    ml = dict(w_in=m_w_in, w_up_fox=m_w_up_fox, w_up_sb=m_w_up_sb, w_up_dil=m_w_up_dil, w_out=m_w_out, w_mlp_in=m_w_mlp_in, w_mlp_out=m_w_mlp_out)
    vl = dict(w_in=v_w_in, w_up_fox=v_w_up_fox, w_up_sb=v_w_up_sb, w_up_dil=v_w_up_dil, w_out=v_w_out, w_mlp_in=v_w_mlp_in, w_mlp_out=v_w_mlp_out)
    small = dict(attn_norm=attn_norm, b_forget=b_forget, q_norm_fox=q_norm_fox, k_norm_fox=k_norm_fox, q_norm_dil=q_norm_dil,
                 k_norm_dil=k_norm_dil, mlp_norm=mlp_norm)
    m_small = dict(attn_norm=m_attn_norm, b_forget=m_b_forget, q_norm_fox=m_q_norm_fox, k_norm_fox=m_k_norm_fox,
                   q_norm_dil=m_q_norm_dil, k_norm_dil=m_k_norm_dil, mlp_norm=m_mlp_norm)
    v_small = dict(attn_norm=v_attn_norm, b_forget=v_b_forget, q_norm_fox=v_q_norm_fox, k_norm_fox=v_k_norm_fox,
                   q_norm_dil=v_q_norm_dil, k_norm_dil=v_k_norm_dil, mlp_norm=v_mlp_norm)

    d_in = w_in.shape[-1] * N_DEV
    dp = -(-d_in // 512) * 512
    rest = [k for k in BIG if k != "w_in"]
    w = {k: [None] * depth for k in BIG}

    def gather(items):
        def done(res):
            for (k, l), sh in zip(items, res):
                whole = _join_shards(k, sh)
                w[k][l] = _permute_in(whole, dp) if k == "w_in" else whole

        return _gather_cargo([wl[k][l].astype(BF16) for k, l in items], done)

    _exchange(gather([("w_in", 0)]), name="gather_first")
    cos, sin = _rope_tables(positions)
    ropes = [(_to_streams(cos, r), _to_streams(sin, r)) for _, r in DIL_PATTERNS]

    h = x.reshape(n, d)
    saved = []
    for l in range(depth):
        cargo = {"fox_fwd": gather([(k, l) for k in rest])}
        if l + 1 < depth:
            cargo["sb_fwd"] = gather([("w_in", l + 1)])
        h, s = _layer_fwd(l, h, w, small, ropes, bl, t, cargo)
        saved.append(s)
    dy, loss_part = _loss_head(h, loss_target.reshape(n, d), name="loss_head")

    recv = {}

    def scatter(names, l, grads):
        def done(res):
            recv.update(zip(names, res))

        slots = [_split_shards(k, _unpermute_in(grads[k], d_in) if k == "w_in" else grads[k]).astype(BF16) for k in names]
        return _scatter_cargo(slots, [recv.get(k) for k in names], l, depth, done)

    gss = [None] * depth
    for l in reversed(range(depth)):
        hooks = {"fox_bwd": lambda gw, l=l: scatter(rest, l, gw), "mm_dh": lambda gw, l=l: scatter(["w_in"], l, gw)}
        dy, _, gss[l] = _layer_bwd(l, dy, saved[l], w, small, ropes, bl, t, hooks)
    grad_x = dy.reshape(bl, t, d)

    g_big, d_big, m_big, v_big = {}, {}, {}, {}
    for k in BIG:
        g_big[k], d_big[k], m_big[k], v_big[k] = _adamw(recv[k], wl[k], ml[k], vl[k], name=f"adamw_{k}")

    rows = [loss_part]
    for l in range(depth):
        gs = gss[l]
        rows += [gs["attn_norm"].reshape(-1, LANES), gs["mlp_norm"].reshape(-1, LANES), gs["fox_gains"], gs["dil_gains"], gs["b_forget"]]
    row_counts = [r.shape[0] for r in rows]
    part = jnp.concatenate(rows, axis=0)
    pad_rows = -(-part.shape[0] // 8) * 8 - part.shape[0]
    summed = _allreduce_small(jnp.pad(part, ((0, pad_rows), (0, 0))), name="allreduce_small")
    pieces, off = [], 0
    for c in row_counts:
        pieces.append(summed[off:off + c])
        off += c
    loss = pieces[0][0, 0]

    def fold(row):
        return row[:HEAD_DIM] + row[HEAD_DIM:]

    g_small = {k: [] for k in SMALL}
    for l in range(depth):
        an, mn, fg, dg, bf = pieces[1 + 5 * l:6 + 5 * l]
        g_small["attn_norm"].append(an.reshape(d))
        g_small["mlp_norm"].append(mn.reshape(d))
        g_small["q_norm_fox"].append(fold(fg[0]))
        g_small["k_norm_fox"].append(fold(fg[1]))
        g_small["q_norm_dil"].append(fold(dg[0]) + fold(dg[8]) + fold(dg[16]))
        g_small["k_norm_dil"].append(fold(dg[1]) + fold(dg[9]) + fold(dg[17]))
        g_small["b_forget"].append(bf[:, 0])
    g_small = {k: jnp.stack(vs) for k, vs in g_small.items()}
    small_shapes = [small[k].shape for k in SMALL]
    outs = _adamw(_pack([g_small[k] for k in SMALL], F32)[None, None], _pack([small[k] for k in SMALL], F32)[None],
                  _pack([m_small[k] for k in SMALL], F32)[None], _pack([v_small[k] for k in SMALL], F32)[None], name="adamw_small")
    g_sm, d_sm, m_sm, v_sm = (dict(zip(SMALL, _unpack(o, small_shapes))) for o in outs)

    order = ("attn_norm", "w_in", "b_forget", "q_norm_fox", "k_norm_fox", "q_norm_dil", "k_norm_dil", "w_up_fox", "w_up_sb",
             "w_up_dil", "w_out", "mlp_norm", "w_mlp_in", "w_mlp_out")
    res = [loss, grad_x]
    for big, sm in ((g_big, g_sm), (d_big, d_sm), (m_big, m_sm), (v_big, v_sm)):
        res += [big[k] if k in big else sm[k] for k in order]
    return tuple(res)
```

```python
import jax
import jax.numpy as jnp
from jax import lax
from jax.experimental import pallas as pl
from jax.experimental.pallas import tpu as pltpu

F32 = jnp.float32
BF16 = jnp.bfloat16

HEAD_DIM = 64
LANES = 128
N_HEADS_FOX = 8
N_HEADS_SB = 8
N_HEADS_DIL = 4
DIL_PATTERNS = ((128, 1), (512, 4), (2048, 16))
ROPE_THETA = 10000.0
EPS = 1e-6
SCALE = 0.125
W_FOX = N_HEADS_FOX * HEAD_DIM
W_SB = N_HEADS_SB * HEAD_DIM
W_DIL = N_HEADS_DIL * HEAD_DIM
W_DILQ = len(DIL_PATTERNS) * W_DIL
P_FOX = 0
P_SB = 3 * W_FOX
P_DIL = P_SB + 3 * W_SB
P_GATE = P_DIL + 3 * W_DILQ
N_DEV = 8
ATT_BLK = 256
ATT_BQ = 512
NEG = -1e30
VMEM_LIMIT = 56 * 1024 * 1024
ADAMW_BLOCK_ELEMS = 128 * 1024
PROJ_TILE = 2176

ADAM_LR = 0.001
ADAM_B1 = 0.9
ADAM_B2 = 0.999
ADAM_EPS = 1e-08
ADAM_WD = 0.01
ADAM_STEP = 10

NT = (((1,), (1,)), ((), ()))
MESH = pl.DeviceIdType.MESH


def _pcall(body, **kw):
    return pl.pallas_call(body, **kw)


def _params(sem=None):
    return pltpu.CompilerParams(dimension_semantics=sem, vmem_limit_bytes=VMEM_LIMIT)


class _Cargo:
    def __init__(self, ins, out_shape, sems, start, finish, on_done, aliases=None):
        self.ins, self.out_shape, self.sems = list(ins), list(out_shape), list(sems)
        self.start, self.finish, self.on_done, self.aliases = start, finish, on_done, dict(aliases or {})


def _call(body, *, cargo=None, name, grid=(), in_specs, out_specs, out_shape, scratch_shapes=(), compiler_params=None):
    if cargo is None:
        kw = dict(grid=grid) if grid else {}
        if compiler_params is not None:
            kw["compiler_params"] = compiler_params
        return _pcall(body, name=name, in_specs=in_specs, out_specs=out_specs, out_shape=out_shape,
                      scratch_shapes=list(scratch_shapes), **kw)
    single = not isinstance(out_shape, (list, tuple))
    o_specs, o_shape = ([out_specs], [out_shape]) if single else (list(out_specs), list(out_shape))
    n_in, n_out, n_scr = len(in_specs), len(o_shape), len(scratch_shapes)
    c_in, c_out = len(cargo.ins), len(cargo.out_shape)

    def wrapped(*refs):
        ins, cins = refs[:n_in], refs[n_in:n_in + c_in]
        o0 = n_in + c_in
        outs, couts = refs[o0:o0 + n_out], refs[o0 + n_out:o0 + n_out + c_out]
        s0 = o0 + n_out + c_out
        scr, sems = refs[s0:s0 + n_scr], refs[s0 + n_scr:]
        first = last = None
        for ax, size in enumerate(grid):
            pid = pl.program_id(ax)
            first = (pid == 0) if first is None else first & (pid == 0)
            last = (pid == size - 1) if last is None else last & (pid == size - 1)
        if first is None:
            cargo.start(cins, couts, sems)
            body(*ins, *outs, *scr)
            cargo.finish(cins, couts, sems)
            return

        @pl.when(first)
        def _():
            cargo.start(cins, couts, sems)

        body(*ins, *outs, *scr)

        @pl.when(last)
        def _():
            cargo.finish(cins, couts, sems)

    hbm = pl.BlockSpec(memory_space=pl.ANY)
    kw = dict(grid=grid, compiler_params=_params(("arbitrary",) * len(grid))) if grid else {}
    call = _pcall(
        wrapped, name=name, in_specs=list(in_specs) + [hbm] * c_in, out_specs=o_specs + [hbm] * c_out,
        out_shape=o_shape + cargo.out_shape, scratch_shapes=list(scratch_shapes) + cargo.sems,
        input_output_aliases={n_in + i: n_out + j for i, j in cargo.aliases.items()}, **kw)

    def run(*args):
        res = call(*args, *cargo.ins)
        cargo.on_done(list(res[n_out:]))
        return res[0] if single else list(res[:n_out])

    return run


def _tile(dim, target, mult=LANES):
    t = (min(dim, target) // mult) * mult
    while t >= mult:
        if dim % t == 0:
            return t
        t -= mult
    return dim


def _mm(a, b, *, ta=False, tb=False, add=None, relu2=False, relu_grad_of=None, out_dtype=F32, name, tm=1024, tn=512,
        tk=1024, cargo=None):
    m, k = (a.shape[1], a.shape[0]) if ta else a.shape
    n = b.shape[0] if tb else b.shape[1]
    tm, tn, tk = _tile(m, tm), _tile(n, tn), _tile(k, tk)
    nk = k // tk
    dn = (((0,) if ta else (1,), (1,) if tb else (0,)), ((), ()))

    extra = add if add is not None else relu_grad_of

    def body(*refs):
        a_ref, b_ref = refs[:2]
        x_ref = refs[2] if extra is not None else None
        outs = refs[2 + (extra is not None):-1]
        acc = refs[-1]
        kk = pl.program_id(2)
        part = lax.dot_general(a_ref[...].astype(BF16), b_ref[...].astype(BF16), dn, preferred_element_type=F32)

        def finish(r):
            if add is not None:
                r = r + x_ref[...]
            if relu_grad_of is not None:
                r = r * (2.0 * jnp.maximum(x_ref[...].astype(F32), 0.0))
            outs[0][...] = r.astype(out_dtype)
            if relu2:
                rr = jnp.maximum(r, 0.0)
                outs[1][...] = (rr * rr).astype(BF16)

        if nk == 1:
            finish(part)
            return

        @pl.when(kk == 0)
        def _():
            acc[...] = part

        @pl.when((kk > 0) & (kk < nk - 1))
        def _():
            acc[...] += part

        @pl.when(kk == nk - 1)
        def _():
            finish(acc[...] + part)

    a_spec = pl.BlockSpec((tk, tm), lambda i, j, q: (q, i)) if ta else pl.BlockSpec((tm, tk), lambda i, j, q: (i, q))
    b_spec = pl.BlockSpec((tn, tk), lambda i, j, q: (j, q)) if tb else pl.BlockSpec((tk, tn), lambda i, j, q: (q, j))
    o_spec = pl.BlockSpec((tm, tn), lambda i, j, q: (i, j))
    ins, specs = [a, b], [a_spec, b_spec]
    if extra is not None:
        ins.append(extra)
        specs.append(o_spec)
    sds = jax.ShapeDtypeStruct((m, n), out_dtype)
    return _call(
        body, cargo=cargo, name=name, grid=(m // tm, n // tn, nk), in_specs=specs,
        out_specs=[o_spec, o_spec] if relu2 else o_spec,
        out_shape=[sds, jax.ShapeDtypeStruct((m, n), BF16)] if relu2 else sds,
        scratch_shapes=[pltpu.VMEM((tm, tn) if nk > 1 else (8, LANES), F32)],
        compiler_params=_params(("parallel", "parallel", "arbitrary")),
    )(*ins)


def _rmsnorm_fwd(x, g, *, name):
    n, d = x.shape
    tm = _tile(n, 256, 8)

    def body(x_ref, g_ref, h_ref):
        xv = x_ref[...]
        inv = lax.rsqrt(jnp.mean(xv * xv, axis=1, keepdims=True) + EPS)
        h_ref[...] = (xv * inv * g_ref[...]).astype(BF16)

    row = pl.BlockSpec((tm, d), lambda i: (i, 0))
    return _pcall(
        body, name=name, grid=(n // tm,), in_specs=[row, pl.BlockSpec((1, d), lambda i: (0, 0))], out_specs=row,
        out_shape=jax.ShapeDtypeStruct((n, d), BF16), compiler_params=_params(("parallel",)),
    )(x, g)


def _rmsnorm_bwd(x, g, dh, dres, *, name):
    n, d = x.shape
    tm = _tile(n, 256, 8)

    def body(x_ref, g_ref, dh_ref, dres_ref, dx_ref, dg_ref):
        @pl.when(pl.program_id(0) == 0)
        def _():
            dg_ref[...] = jnp.zeros_like(dg_ref)

        xv = x_ref[...]
        inv = lax.rsqrt(jnp.mean(xv * xv, axis=1, keepdims=True) + EPS)
        y = xv * inv
        dhv = dh_ref[...]
        dg_ref[...] += jnp.sum(dhv * y, axis=0, keepdims=True)
        dy = dhv * g_ref[...]
        dx_ref[...] = dres_ref[...] + inv * (dy - y * jnp.mean(dy * y, axis=1, keepdims=True))

    row = pl.BlockSpec((tm, d), lambda i: (i, 0))
    vec = pl.BlockSpec((1, d), lambda i: (0, 0))
    return _pcall(
        body, name=name, grid=(n // tm,), in_specs=[row, vec, row, row], out_specs=[row, vec],
        out_shape=[jax.ShapeDtypeStruct((n, d), F32), jax.ShapeDtypeStruct((1, d), F32)],
        compiler_params=_params(("arbitrary",)),
    )(x, g, dh, dres)


def _gate_specs(n, d):
    bw = 256 if d % 256 == 0 else LANES
    tm = _tile(n, 512, 8)
    nb = d // bw
    yspec = pl.BlockSpec((tm, bw), lambda i, j: (i, j))
    gspecs = [pl.BlockSpec((tm, bw), lambda i, j, b=b: (i, P_GATE // bw + b * nb + j)) for b in range(3)]
    return tm, bw, nb, yspec, gspecs


def _gate_merge_fwd(proj, ys, *, name):
    n, d = ys[0].shape
    tm, bw, nb, yspec, gspecs = _gate_specs(n, d)

    def body(g0, g1, g2, y0, y1, y2, o_ref):
        acc = jax.nn.sigmoid(g0[...]) * y0[...]
        acc += jax.nn.sigmoid(g1[...]) * y1[...]
        acc += jax.nn.sigmoid(g2[...]) * y2[...]
        o_ref[...] = acc.astype(BF16)

    return _pcall(
        body, name=name, grid=(n // tm, nb), in_specs=gspecs + [yspec] * 3, out_specs=yspec,
        out_shape=jax.ShapeDtypeStruct((n, d), BF16), compiler_params=_params(("parallel", "parallel")),
    )(proj, proj, proj, *ys)


def _gate_merge_bwd(proj, ys, dmerged, *, name):
    n, d = ys[0].shape
    tm, bw, nb, yspec, gspecs = _gate_specs(n, d)

    def body(g0, g1, g2, y0, y1, y2, dm_ref, dy0, dy1, dy2, dgl0, dgl1, dgl2):
        dm = dm_ref[...]
        for g_ref, y_ref, dy_ref, dgl_ref in ((g0, y0, dy0, dgl0), (g1, y1, dy1, dgl1), (g2, y2, dy2, dgl2)):
            s = jax.nn.sigmoid(g_ref[...])
            dy_ref[...] = (dm * s).astype(BF16)
            dgl_ref[...] = (dm * y_ref[...] * s * (1.0 - s)).astype(BF16)

    sds = jax.ShapeDtypeStruct((n, d), BF16)
    return _pcall(
        body, name=name, grid=(n // tm, nb), in_specs=gspecs + [yspec] * 4, out_specs=[yspec] * 6,
        out_shape=[sds] * 6, compiler_params=_params(("parallel", "parallel")),
    )(proj, proj, proj, *ys, dmerged)


def _loss_head(y, tgt, *, name):
    n, d = y.shape
    tm = _tile(n, 256, 8)
    steps = n // tm

    def body(y_ref, t_ref, dy_ref, loss_ref, acc):
        i = pl.program_id(0)

        @pl.when(i == 0)
        def _():
            acc[...] = jnp.zeros_like(acc)

        e = y_ref[...] - t_ref[...]
        dy_ref[...] = e * (1.0 / d)
        acc[...] += jnp.sum(e * e, axis=0, keepdims=True)

        @pl.when(i == steps - 1)
        def _():
            tot = jnp.sum(acc[...], axis=1, keepdims=True) * (0.5 / d)
            loss_ref[...] = jnp.broadcast_to(tot, loss_ref.shape)

    row = pl.BlockSpec((tm, d), lambda i: (i, 0))
    return _pcall(
        body, name=name, grid=(steps,), in_specs=[row, row], out_specs=[row, pl.BlockSpec((8, LANES), lambda i: (0, 0))],
        out_shape=[jax.ShapeDtypeStruct((n, d), F32), jax.ShapeDtypeStruct((8, LANES), F32)],
        scratch_shapes=[pltpu.VMEM((1, d), F32)], compiler_params=_params(("arbitrary",)),
    )(y, tgt)


def _dil_weights(lses, *, name):
    shp = lses[0].shape

    def body(l0, l1, l2, lse_ref, w0, w1, w2):
        a, b, c = l0[...], l1[...], l2[...]
        m = jnp.maximum(jnp.maximum(a, b), c)
        ea, eb, ec = jnp.exp(a - m), jnp.exp(b - m), jnp.exp(c - m)
        den = ea + eb + ec
        lse_ref[...] = m + jnp.log(den)
        w0[...] = ea / den
        w1[...] = eb / den
        w2[...] = ec / den

    vmem = pl.BlockSpec(memory_space=pltpu.VMEM)
    return _pcall(body, name=name, in_specs=[vmem] * 3, out_specs=[vmem] * 4, out_shape=[jax.ShapeDtypeStruct(shp, F32)] * 4)(*lses)


def _dil_mix(os_, ws, *, name):
    n, w = os_[0].shape
    tm = _tile(n, 512, 8)

    def body(o0, o1, o2, w0, w1, w2, out_ref):
        out_ref[...] = w0[...] * o0[...] + w1[...] * o1[...] + w2[...] * o2[...]

    spec = pl.BlockSpec((tm, w), lambda i: (i, 0))
    return _pcall(
        body, name=name, grid=(n // tm,), in_specs=[spec] * 6, out_specs=spec, out_shape=jax.ShapeDtypeStruct((n, w), F32),
        compiler_params=_params(("parallel",)),
    )(*os_, *ws)


def _adamw(gsrc, w, m, v, *, name):
    s, dep, a, b = gsrc.shape
    ta = _tile(a, max(16, (ADAMW_BLOCK_ELEMS // b) // 16 * 16), 16)
    c1 = 1.0 / (1.0 - ADAM_B1 ** ADAM_STEP)
    c2 = 1.0 / (1.0 - ADAM_B2 ** ADAM_STEP)

    def body(gs_ref, w_ref, m_ref, v_ref, g_ref, d_ref, m2_ref, v2_ref):
        g = gs_ref[0].astype(F32)
        for i in range(1, s):
            g = g + gs_ref[i].astype(F32)
        m2 = ADAM_B1 * m_ref[...] + (1.0 - ADAM_B1) * g
        v2 = ADAM_B2 * v_ref[...] + (1.0 - ADAM_B2) * (g * g)
        g_ref[...] = g
        m2_ref[...] = m2
        v2_ref[...] = v2
        d_ref[...] = -ADAM_LR * ((m2 * c1) / (jnp.sqrt(v2 * c2) + ADAM_EPS) + ADAM_WD * w_ref[...])

    spec = pl.BlockSpec((None, ta, b), lambda l, i: (l, i, 0))
    sds = jax.ShapeDtypeStruct((dep, a, b), F32)
    return _pcall(
        body, name=name, grid=(dep, a // ta),
        in_specs=[pl.BlockSpec((s, None, ta, b), lambda l, i: (0, l, i, 0)), spec, spec, spec],
        out_specs=[spec] * 4, out_shape=[sds] * 4, compiler_params=_params(("parallel", "parallel")),
    )(gsrc, w, m, v)


def _mask_a():
    return lax.broadcasted_iota(jnp.int32, (1, LANES), 1) < HEAD_DIM


def _half_sum(x, m_a):
    sa = jnp.sum(jnp.where(m_a, x, 0.0), axis=1, keepdims=True)
    sb = jnp.sum(jnp.where(m_a, 0.0, x), axis=1, keepdims=True)
    return jnp.where(m_a, sa, sb)


def _head_inv(x, m_a):
    return lax.rsqrt(_half_sum(x * x, m_a) * (1.0 / HEAD_DIM) + EPS)


def _swap32(x):
    first = (lax.broadcasted_iota(jnp.int32, (1, LANES), 1) % HEAD_DIM) < (HEAD_DIM // 2)
    return jnp.where(first, pltpu.roll(x, LANES - HEAD_DIM // 2, 1), pltpu.roll(x, HEAD_DIM // 2, 1))


def _tri(blk, rel):
    r = lax.broadcasted_iota(jnp.int32, (blk, blk), 0)
    c = lax.broadcasted_iota(jnp.int32, (blk, blk), 1)
    return jnp.where(rel(r, c), 1.0, 0.0).astype(BF16)


def _cumdot(x, u, parts):
    acc = None
    r = x
    for i in range(parts):
        xi = r.astype(BF16)
        t = jnp.dot(xi, u, preferred_element_type=F32)
        acc = t if acc is None else acc + t
        if i + 1 < parts:
            r = r - xi.astype(F32)
    return acc


def _rows(i, blk):
    return pl.ds(pl.multiple_of(i * blk, blk), blk)


def _col_spec(n, off):
    return pl.BlockSpec((None, n, LANES), lambda z, p, off=off: (z, 0, off + p))


def _att_blk(n):
    return ATT_BLK if n % ATT_BLK == 0 else min(LANES, n)


def _att_blocks(n):
    bk = _att_blk(n)
    return (ATT_BQ if n % ATT_BQ == 0 else bk), bk


def _loop(lo, hi, fn):
    def it(i, c):
        fn(i)
        return c

    lax.fori_loop(lo, hi, it, 0)


def _normed(src, g_ref, rope_refs, rows, m_a):
    xv = src[rows, :]
    xn = xv * _head_inv(xv, m_a) * g_ref[...]
    if rope_refs is not None:
        xn = xn * rope_refs[0][rows, :] + _swap32(xn) * rope_refs[1][rows, :]
    return xn


def _bias_lane(h):
    return HEAD_DIM if h == 0 else 0


def _k_for_head(kn, kb_row, h, m_h, lane, blk):
    out = jnp.where(m_h, kn, 0.0)
    if kb_row is not None:
        col = jnp.transpose(jnp.broadcast_to(kb_row, (LANES, blk)))
        hi = col.astype(BF16).astype(F32)
        mid = (col - hi).astype(BF16).astype(F32)
        lo = col - hi - mid
        b = _bias_lane(h)
        out = jnp.where(lane == b, hi, jnp.where(lane == b + 1, mid, jnp.where(lane == b + 2, lo, out)))
    return out.astype(BF16)


def _q_for_head(qb, h, m_h, lane, biased):
    out = jnp.where(m_h, qb, 0)
    if biased:
        b = _bias_lane(h)
        out = jnp.where((lane >= b) & (lane < b + 3), jnp.ones_like(out), out)
    return out


def _head_rows(x, parts=3):
    rr = lax.broadcasted_iota(jnp.int32, (8, LANES), 0)
    ll = lax.broadcasted_iota(jnp.int32, (8, LANES), 1)
    sel = jnp.where(((rr == 0) & (ll < HEAD_DIM)) | ((rr == 1) & (ll >= HEAD_DIM)), 1.0, 0.0).astype(BF16)
    acc = None
    rem = x
    for i in range(parts):
        xi = rem.astype(BF16)
        t = lax.dot_general(sel, xi, NT, preferred_element_type=F32)
        acc = t if acc is None else acc + t
        if i + 1 < parts:
            rem = rem - xi.astype(F32)
    return acc


def _cumdot_left(u, x, parts):
    acc = None
    rem = x
    for i in range(parts):
        xi = rem.astype(BF16)
        t = jnp.dot(u, xi, preferred_element_type=F32)
        acc = t if acc is None else acc + t
        if i + 1 < parts:
            rem = rem - xi.astype(F32)
    return acc


def _q_minus_k(bk, bq):
    return lax.broadcasted_iota(jnp.int32, (bk, bq), 1) - lax.broadcasted_iota(jnp.int32, (bk, bq), 0)


def _stat_spec(nb, blk):
    return pl.BlockSpec((None, 2, nb, 1, blk), lambda z, p: (z, p, 0, 0, 0))


def _stream_rows(stride):
    if stride == 1:
        return _rows
    c = pl.program_id(2)
    return lambda i, blk: pl.ds(c + i * (blk * stride), blk, stride=stride)


def _attn_specs(t, npairs, stride, nq, bq):
    col = lambda off: pl.BlockSpec((None, t, LANES), lambda z, p, c: (z, 0, off + p))
    vec = pl.BlockSpec((1, LANES), lambda z, p, c: (0, 0))
    seq = pl.BlockSpec((None, t, LANES), lambda z, p, c: (z, 0, 0))
    stat = pl.BlockSpec((None, 2, nq, 1, bq), lambda z, p, c: (z * stride + c, p, 0, 0, 0))
    return col, vec, seq, stat


def _attn_fwd(src, offs, npairs, gq, gk, *, rope=None, kbias=None, window, stride=1, name, cargo=None):
    bs, t, _ = src.shape
    n = t // stride
    bq, bk = _att_blocks(n)
    nq, nk, rq = n // bq, n // bk, bq // bk
    full = window >= n
    wblk = -(-window // bk)
    biased = kbias is not None

    def body(*refs):
        it = iter(refs)
        q_ref, k_ref, v_ref, gq_ref, gk_ref = (next(it) for _ in range(5))
        rope_refs = (next(it), next(it)) if rope is not None else None
        kb_ref = next(it) if biased else None
        o_ref, lse_ref, qn_s, kh_s, vt_s, acc_s, m_s = (next(it) for _ in range(7))
        m_a = _mask_a()
        masks = (m_a, jnp.logical_not(m_a))
        lane = lax.broadcasted_iota(jnp.int32, (1, LANES), 1)
        row = lax.broadcasted_iota(jnp.int32, (LANES, 1), 0)
        tok = _stream_rows(stride)

        def prep(c):
            rows = _rows(c, bk)
            trows = tok(c, bk)
            qn_s[rows, :] = (_normed(q_ref, gq_ref, rope_refs, trows, m_a) * SCALE).astype(BF16)
            kn = _normed(k_ref, gk_ref, rope_refs, trows, m_a)
            vt = jnp.transpose(v_ref[trows, :])
            for h in (0, 1):
                kh_s[h, rows, :] = _k_for_head(kn, kb_ref[h, c] if biased else None, h, masks[h], lane, bk)
                vt_s[h, c] = jnp.where(row == _bias_lane(h), 1.0, vt).astype(BF16)

        _loop(0, nk, prep)
        qk = _q_minus_k(bk, bq)

        def qblock(qi):
            rows = _rows(qi, bq)
            qb = qn_s[rows, :]
            qh = [_q_for_head(qb, h, masks[h], lane, biased) for h in (0, 1)]
            m_s[...] = jnp.full(m_s.shape, NEG, F32)
            acc_s[...] = jnp.zeros_like(acc_s)

            def step(kj, masked):
                cols = _rows(kj, bk)
                sts = [lax.dot_general(kh_s[h, cols, :], qh[h], NT, preferred_element_type=F32) for h in (0, 1)]
                old = [(m_s[h], acc_s[h]) for h in (0, 1)]
                if masked:
                    d = qk + (qi * bq - kj * bk)
                    ok = (d >= 0) & (d <= window)
                    sts = [jnp.where(ok, st, NEG) for st in sts]
                new = []
                for h in (0, 1):
                    m, acc = old[h]
                    m2 = jnp.maximum(m, jnp.max(sts[h], axis=0, keepdims=True))
                    pt = jnp.exp(sts[h] - m2).astype(BF16)
                    new.append((m2, jnp.exp(m - m2) * acc + jnp.dot(vt_s[h, kj], pt, preferred_element_type=F32)))
                for h in (0, 1):
                    m_s[h], acc_s[h] = new[h]

            if full:
                _loop(0, qi * rq, lambda kj: step(kj, False))
                _loop(qi * rq, (qi + 1) * rq, lambda kj: step(kj, True))
            else:
                _loop(jnp.maximum(qi * rq - wblk, 0), (qi + 1) * rq, lambda kj: step(kj, True))
            outs = []
            for h in (0, 1):
                acc_t = acc_s[h]
                den = acc_t[_bias_lane(h):_bias_lane(h) + 1, :]
                outs.append(jnp.transpose(acc_t / den))
                lse_ref[h, qi] = m_s[h] + jnp.log(den)
            o_ref[tok(qi, bq), :] = jnp.where(m_a, outs[0], outs[1])

        _loop(0, nq, qblock)

    col, vec, seq, stat = _attn_specs(t, npairs, stride, nq, bq)
    ins = [src, src, src, gq, gk]
    specs = [col(offs[0]), col(offs[1]), col(offs[2]), vec, vec]
    if rope is not None:
        ins += list(rope)
        specs += [seq, seq]
    if biased:
        ins.append(kbias)
        specs.append(pl.BlockSpec((None, 2, nk, 1, bk), lambda z, p, c: (z, p, 0, 0, 0)))
    scratch = [pltpu.VMEM((n, LANES), BF16), pltpu.VMEM((2, n, LANES), BF16), pltpu.VMEM((2, nk, LANES, bk), BF16)]
    scratch += [pltpu.VMEM((2, LANES, bq), F32), pltpu.VMEM((2, 1, bq), F32)]
    return _call(
        body, cargo=cargo, name=name, grid=(bs, npairs, stride), in_specs=specs, out_specs=[col(0), stat],
        out_shape=[jax.ShapeDtypeStruct((bs, t, LANES * npairs), F32),
                   jax.ShapeDtypeStruct((bs * stride, 2 * npairs, nq, 1, bq), F32)],
        scratch_shapes=scratch, compiler_params=_params(("parallel", "parallel", "arbitrary")),
    )(*ins)


def _attn_bwd(src, offs, npairs, gq, gk, o, do, lse, *, rope=None, kbias=None, window, stride=1, name, cargo=None):
    bs, t, _ = src.shape
    n = t // stride
    bq, bk = _att_blocks(n)
    nq, nk, rq = n // bq, n // bk, bq // bk
    full = window >= n
    wblk = -(-window // bk)
    biased = kbias is not None
    gdt = BF16 if stride == 1 else F32

    def body(*refs):
        it = iter(refs)
        q_ref, k_ref, v_ref, gq_ref, gk_ref, o_ref, do_ref, lse_ref = (next(it) for _ in range(8))
        rope_refs = (next(it), next(it)) if rope is not None else None
        kb_ref = next(it) if biased else None
        dq_ref, dk_ref, dv_ref, dg_ref = (next(it) for _ in range(4))
        dkb_ref = next(it) if biased else None
        qn_s, kh_s, vb_s, kt_s, dqn_s, dkh_s, dv_s, dq_s, rs_s = (next(it) for _ in range(9))
        m_a = _mask_a()
        masks = (m_a, jnp.logical_not(m_a))
        lane = lax.broadcasted_iota(jnp.int32, (1, LANES), 1)
        tok = _stream_rows(stride)

        @pl.when((pl.program_id(0) == 0) & (pl.program_id(1) == 0) & (pl.program_id(2) == 0))
        def _():
            dg_ref[...] = jnp.zeros_like(dg_ref)

        def prep(c):
            rows = _rows(c, bk)
            trows = tok(c, bk)
            qn_s[rows, :] = (_normed(q_ref, gq_ref, rope_refs, trows, m_a) * SCALE).astype(BF16)
            kn = _normed(k_ref, gk_ref, rope_refs, trows, m_a)
            kt_s[c] = jnp.transpose(kn).astype(BF16)
            vb_s[rows, :] = v_ref[trows, :].astype(BF16)
            for h in (0, 1):
                kh_s[h, rows, :] = _k_for_head(kn, kb_ref[h, c] if biased else None, h, masks[h], lane, bk)

        _loop(0, nk, prep)
        dkh_s[...] = jnp.zeros_like(dkh_s)
        dv_s[...] = jnp.zeros_like(dv_s)
        qk = _q_minus_k(bk, bq)

        def qblock(qi):
            rows = _rows(qi, bq)
            qb = qn_s[rows, :]
            trows = tok(qi, bq)
            dob = do_ref[trows, :]
            delta = _head_rows(dob * o_ref[trows, :])
            qh = [_q_for_head(qb, h, masks[h], lane, biased) for h in (0, 1)]
            doms = [jnp.where(masks[h], dob, 0.0).astype(BF16) for h in (0, 1)]
            lses = [lse_ref[h, qi] for h in (0, 1)]
            dq_s[...] = jnp.zeros_like(dq_s)
            if biased:
                for h in (0, 1):
                    rs_s[h, qi] = jnp.zeros((1, bq), F32)

            def step(kj, masked):
                cols = _rows(kj, bk)
                vb = vb_s[cols, :]
                kt = kt_s[kj]
                sts = [lax.dot_general(kh_s[h, cols, :], qh[h], NT, preferred_element_type=F32) for h in (0, 1)]
                dpts = [lax.dot_general(vb, doms[h], NT, preferred_element_type=F32) for h in (0, 1)]
                if masked:
                    d = qk + (qi * bq - kj * bk)
                    ok = (d >= 0) & (d <= window)
                    sts = [jnp.where(ok, st, NEG) for st in sts]
                new = []
                for h in (0, 1):
                    pt = jnp.exp(sts[h] - lses[h])
                    dst = pt * (dpts[h] - delta[h:h + 1, :])
                    dsb = dst.astype(BF16)
                    tk = jnp.dot(dsb, qh[h], preferred_element_type=F32)
                    if biased:
                        tk = tk + jnp.dot((dst - dsb.astype(F32)).astype(BF16), qh[h], preferred_element_type=F32)
                    tv = jnp.dot(pt.astype(BF16), doms[h], preferred_element_type=F32)
                    tq = jnp.dot(kt, dsb, preferred_element_type=F32)
                    new.append((tk, tv, tq, jnp.sum(dst, axis=0, keepdims=True) if biased else None))
                for h in (0, 1):
                    dkh_s[h, cols, :] += new[h][0]
                    dq_s[h] += new[h][2]
                    if biased:
                        rs_s[h, qi] += new[h][3]
                dv_s[cols, :] += new[0][1] + new[1][1]

            if full:
                _loop(0, qi * rq, lambda kj: step(kj, False))
                _loop(qi * rq, (qi + 1) * rq, lambda kj: step(kj, True))
            else:
                _loop(jnp.maximum(qi * rq - wblk, 0), (qi + 1) * rq, lambda kj: step(kj, True))
            dqn_s[rows, :] = jnp.where(m_a, jnp.transpose(dq_s[0]), jnp.transpose(dq_s[1])) * SCALE

        _loop(0, nq, qblock)

        def finish(c, carry):
            rows = _rows(c, bq)
            out = []
            dk_pair = [dkh_s[0, rows, :], dkh_s[1, rows, :]]
            if biased:
                for h in (0, 1):
                    b = _bias_lane(h)
                    dkb_row = jnp.transpose(dk_pair[h])[b:b + 1, :] - rs_s[h, c]
                    for j in range(rq):
                        dkb_ref[h, c * rq + j] = dkb_row[:, j * bk:(j + 1) * bk]
            trows = tok(c, bq)
            dv_ref[trows, :] = dv_s[rows, :].astype(gdt)
            grads = (dqn_s[rows, :], jnp.where(m_a, dk_pair[0], dk_pair[1]))
            for src_ref, g_ref, dxn, dst in ((q_ref, gq_ref, grads[0], dq_ref), (k_ref, gk_ref, grads[1], dk_ref)):
                xv = src_ref[trows, :]
                inv = _head_inv(xv, m_a)
                y = xv * inv
                if rope_refs is not None:
                    dxn = dxn * rope_refs[0][trows, :] + _swap32(dxn * rope_refs[1][trows, :])
                dy = dxn * g_ref[...]
                dst[trows, :] = (inv * (dy - y * (_half_sum(dy * y, m_a) * (1.0 / HEAD_DIM)))).astype(gdt)
                out.append(jnp.sum(dxn * y, axis=0, keepdims=True))
            return carry[0] + out[0], carry[1] + out[1]

        zero = jnp.zeros((1, LANES), F32)
        dgq, dgk = lax.fori_loop(0, nq, finish, (zero, zero))
        dg_ref[0:1, :] += dgq
        dg_ref[1:2, :] += dgk

    col, vec, seq, stat = _attn_specs(t, npairs, stride, nq, bq)
    ospec = col(0)
    ins = [src, src, src, gq, gk, o, do, lse]
    specs = [col(offs[0]), col(offs[1]), col(offs[2]), vec, vec, ospec, ospec, stat]
    if rope is not None:
        ins += list(rope)
        specs += [seq, seq]
    sds = jax.ShapeDtypeStruct((bs, t, LANES * npairs), gdt)
    out_shape = [sds, sds, sds, jax.ShapeDtypeStruct((8, LANES), F32)]
    out_specs = [ospec, ospec, ospec, pl.BlockSpec((8, LANES), lambda z, p, c: (0, 0))]
    if biased:
        kbspec = pl.BlockSpec((None, 2, nk, 1, bk), lambda z, p, c: (z, p, 0, 0, 0))
        ins.append(kbias)
        specs.append(kbspec)
        out_shape.append(jax.ShapeDtypeStruct(kbias.shape, F32))
        out_specs.append(kbspec)
    scratch = [pltpu.VMEM((n, LANES), BF16), pltpu.VMEM((2, n, LANES), BF16), pltpu.VMEM((n, LANES), BF16)]
    scratch += [pltpu.VMEM((nk, LANES, bk), BF16), pltpu.VMEM((n, LANES), F32), pltpu.VMEM((2, n, LANES), F32)]
    scratch += [pltpu.VMEM((n, LANES), F32), pltpu.VMEM((2, LANES, bq), F32), pltpu.VMEM((2, nq, 1, bq), F32)]
    return _call(
        body, cargo=cargo, name=name, grid=(bs, npairs, stride), in_specs=specs, out_specs=out_specs, out_shape=out_shape,
        scratch_shapes=scratch, compiler_params=_params(("arbitrary", "arbitrary", "arbitrary")),
    )(*ins)


SB_LOG_PARTS = 2
SB_GRAD_PARTS = 1


def _log_sig_pair(z, with_sigmoid=False):
    e = jnp.exp(-jnp.abs(z))
    den = 1.0 + e
    lsn = -(jnp.maximum(z, 0.0) + jnp.log(den))
    if with_sigmoid:
        return lsn, z + lsn, jnp.where(z >= 0.0, 1.0, e) / den
    return lsn, z + lsn


def _sb_fwd(src, offs, npairs, *, name, cargo=None):
    zs, n, _ = src.shape
    bq, bk = _att_blocks(n)
    nq, nk, rq = n // bq, n // bk, bq // bk

    def body(q_ref, k_ref, v_ref, o_ref, lt_ref, qs_s, kb_s, vt_s, acc_s, c_s):
        m_a = _mask_a()
        masks = (m_a, jnp.logical_not(m_a))

        def prep(c):
            rows = _rows(c, bk)
            qs_s[rows, :] = (q_ref[rows, :] * SCALE).astype(BF16)
            kb_s[rows, :] = k_ref[rows, :].astype(BF16)
            vt_s[c] = jnp.transpose(v_ref[rows, :]).astype(BF16)

        _loop(0, nk, prep)
        qk = _q_minus_k(bk, bq)
        u_gt = _tri(bk, lambda r, c: c > r)

        def qblock(qi):
            rows = _rows(qi, bq)
            qb = qs_s[rows, :]
            qms = [jnp.where(masks[h], qb, 0) for h in (0, 1)]
            acc_s[...] = jnp.zeros_like(acc_s)
            c_s[...] = jnp.zeros_like(c_s)

            def step(kj, masked):
                kb, vt = kb_s[_rows(kj, bk), :], vt_s[kj]
                zts = [lax.dot_general(kb, qms[h], NT, preferred_element_type=F32) for h in (0, 1)]
                old = [c_s[h] for h in (0, 1)]
                if masked:
                    ok = (qk + (qi * bq - kj * bk)) > 0
                new = []
                for h in (0, 1):
                    lsn, lsp = _log_sig_pair(zts[h])
                    if masked:
                        lsn = jnp.where(ok, lsn, 0.0)
                    at = jnp.exp(lsp + (old[h] + _cumdot_left(u_gt, lsn, SB_LOG_PARTS)))
                    if masked:
                        at = jnp.where(ok, at, 0.0)
                    new.append((jnp.dot(vt, at.astype(BF16), preferred_element_type=F32),
                                old[h] + jnp.sum(lsn, axis=0, keepdims=True)))
                for h in (0, 1):
                    acc_s[h] += new[h][0]
                    c_s[h] = new[h][1]

            _loop(0, rq, lambda t: step((qi + 1) * rq - 1 - t, True))
            _loop(0, qi * rq, lambda t: step(qi * rq - 1 - t, False))
            o_ref[rows, :] = jnp.where(m_a, jnp.transpose(acc_s[0]), jnp.transpose(acc_s[1]))
            for h in (0, 1):
                lt_ref[h, qi] = c_s[h]

        _loop(0, nq, qblock)

    scratch = [pltpu.VMEM((n, LANES), BF16)] * 2 + [pltpu.VMEM((nk, LANES, bk), BF16)]
    scratch += [pltpu.VMEM((2, LANES, bq), F32), pltpu.VMEM((2, 1, bq), F32)]
    return _call(
        body, cargo=cargo, name=name, grid=(zs, npairs),
        in_specs=[_col_spec(n, offs[0]), _col_spec(n, offs[1]), _col_spec(n, offs[2])],
        out_specs=[_col_spec(n, 0), _stat_spec(nq, bq)],
        out_shape=[jax.ShapeDtypeStruct((zs, n, LANES * npairs), F32), jax.ShapeDtypeStruct((zs, 2 * npairs, nq, 1, bq), F32)],
        scratch_shapes=scratch, compiler_params=_params(("parallel", "parallel")),
    )(src, src, src)


def _sb_bwd(src, offs, npairs, do, ltot, *, name, cargo=None):
    zs, n, _ = src.shape
    bq, bk = _att_blocks(n)
    nq, nk, rq = n // bq, n // bk, bq // bk

    def body(q_ref, k_ref, v_ref, do_ref, lt_ref, dq_ref, dk_ref, dv_ref, qs_s, kb_s, vb_s, kt_s, dk_s, dv_s, dq_s, lp_s, ep_s):
        m_a = _mask_a()
        masks = (m_a, jnp.logical_not(m_a))

        def prep(c):
            rows = _rows(c, bk)
            qs_s[rows, :] = (q_ref[rows, :] * SCALE).astype(BF16)
            kv = k_ref[rows, :]
            kb_s[rows, :] = kv.astype(BF16)
            kt_s[c] = jnp.transpose(kv).astype(BF16)
            vb_s[rows, :] = v_ref[rows, :].astype(BF16)

        _loop(0, nk, prep)
        dk_s[...] = jnp.zeros_like(dk_s)
        dv_s[...] = jnp.zeros_like(dv_s)
        qk = _q_minus_k(bk, bq)
        u_le = _tri(bk, lambda r, c: c <= r)
        u_lt = _tri(bk, lambda r, c: c < r)

        def qblock(qi):
            rows = _rows(qi, bq)
            qb = qs_s[rows, :]
            dob = do_ref[rows, :]
            qms = [jnp.where(masks[h], qb, 0) for h in (0, 1)]
            doms = [jnp.where(masks[h], dob, 0.0).astype(BF16) for h in (0, 1)]
            lts = [lt_ref[h, qi] for h in (0, 1)]
            dq_s[...] = jnp.zeros_like(dq_s)
            lp_s[...] = jnp.zeros_like(lp_s)
            ep_s[...] = jnp.zeros_like(ep_s)

            def step(kj, masked):
                cols = _rows(kj, bk)
                kb, vb, kt = kb_s[cols, :], vb_s[cols, :], kt_s[kj]
                zts = [lax.dot_general(kb, qms[h], NT, preferred_element_type=F32) for h in (0, 1)]
                dats = [lax.dot_general(vb, doms[h], NT, preferred_element_type=F32) for h in (0, 1)]
                old = [(lp_s[h], ep_s[h]) for h in (0, 1)]
                if masked:
                    ok = (qk + (qi * bq - kj * bk)) > 0
                new = []
                for h in (0, 1):
                    lp, ep = old[h]
                    lsn, lsp, sig = _log_sig_pair(zts[h], with_sigmoid=True)
                    if masked:
                        lsn = jnp.where(ok, lsn, 0.0)
                    at = jnp.exp(lsp + (lts[h] - (lp + _cumdot_left(u_le, lsn, SB_LOG_PARTS))))
                    if masked:
                        at = jnp.where(ok, at, 0.0)
                    et = dats[h] * at
                    big_e = ep + _cumdot_left(u_lt, et, SB_GRAD_PARTS)
                    dzt = et * (1.0 - sig) - big_e * sig
                    if masked:
                        dzt = jnp.where(ok, dzt, 0.0)
                    dzb = dzt.astype(BF16)
                    new.append((jnp.dot(dzb, qms[h], preferred_element_type=F32),
                                jnp.dot(at.astype(BF16), doms[h], preferred_element_type=F32),
                                jnp.dot(kt, dzb, preferred_element_type=F32),
                                lp + jnp.sum(lsn, axis=0, keepdims=True), ep + jnp.sum(et, axis=0, keepdims=True)))
                for h in (0, 1):
                    dq_s[h] += new[h][2]
                    lp_s[h], ep_s[h] = new[h][3], new[h][4]
                dk_s[cols, :] += new[0][0] + new[1][0]
                dv_s[cols, :] += new[0][1] + new[1][1]

            _loop(0, qi * rq, lambda kj: step(kj, False))
            _loop(qi * rq, (qi + 1) * rq, lambda kj: step(kj, True))
            dq_ref[rows, :] = (jnp.where(m_a, jnp.transpose(dq_s[0]), jnp.transpose(dq_s[1])) * SCALE).astype(BF16)

        _loop(0, nq, qblock)

        def store(c):
            rows = _rows(c, bk)
            dk_ref[rows, :] = dk_s[rows, :].astype(BF16)
            dv_ref[rows, :] = dv_s[rows, :].astype(BF16)

        _loop(0, nk, store)

    ospec = _col_spec(n, 0)
    sds = jax.ShapeDtypeStruct((zs, n, LANES * npairs), BF16)
    scratch = [pltpu.VMEM((n, LANES), BF16)] * 3 + [pltpu.VMEM((nk, LANES, bk), BF16)] + [pltpu.VMEM((n, LANES), F32)] * 2
    scratch += [pltpu.VMEM((2, LANES, bq), F32), pltpu.VMEM((2, 1, bq), F32), pltpu.VMEM((2, 1, bq), F32)]
    return _call(
        body, cargo=cargo, name=name, grid=(zs, npairs),
        in_specs=[_col_spec(n, offs[0]), _col_spec(n, offs[1]), _col_spec(n, offs[2]), ospec, _stat_spec(nq, bq)],
        out_specs=[ospec] * 3, out_shape=[sds] * 3, scratch_shapes=scratch,
        compiler_params=_params(("parallel", "parallel")),
    )(src, src, src, do, ltot)


def _fox_gate_fwd(lg, bias, *, name):
    bs, nh, t = lg.shape
    blk = min(LANES, t)

    def body(lg_ref, b_ref, kb_ref):
        u_le = _tri(blk, lambda r, c: r <= c)
        carry = jnp.zeros((nh, 1), F32)
        for j in range(t // blk):
            sl = slice(j * blk, (j + 1) * blk)
            xv = lg_ref[:, sl] + b_ref[...]
            lf = jnp.minimum(xv, 0.0) - jnp.log(1.0 + jnp.exp(-jnp.abs(xv)))
            kb_ref[:, sl] = -(carry + _cumdot(lf, u_le, 3))
            carry = carry + jnp.sum(lf, axis=1, keepdims=True)

    spec = pl.BlockSpec((None, nh, t), lambda i: (i, 0, 0))
    return _pcall(
        body, name=name, grid=(bs,), in_specs=[spec, pl.BlockSpec((nh, 1), lambda i: (0, 0))], out_specs=spec,
        out_shape=jax.ShapeDtypeStruct((bs, nh, t), F32), compiler_params=_params(("parallel",)),
    )(lg, bias)


def _fox_gate_bwd(dkb, lg, bias, *, name):
    bs, nh, t = lg.shape
    blk = min(LANES, t)

    def body(dkb_ref, lg_ref, b_ref, dlg_ref, db_ref):
        @pl.when(pl.program_id(0) == 0)
        def _():
            db_ref[...] = jnp.zeros_like(db_ref)

        u_ge = _tri(blk, lambda r, c: r >= c)
        carry = jnp.zeros((nh, 1), F32)
        tot = jnp.zeros((nh, 1), F32)
        for j in reversed(range(t // blk)):
            sl = slice(j * blk, (j + 1) * blk)
            df = -dkb_ref[:, sl]
            dlf = carry + _cumdot(df, u_ge, 3)
            carry = carry + jnp.sum(df, axis=1, keepdims=True)
            xv = lg_ref[:, sl] + b_ref[...]
            dlg = dlf * jax.nn.sigmoid(-xv)
            dlg_ref[:, sl] = dlg
            tot = tot + jnp.sum(dlg, axis=1, keepdims=True)
        db_ref[...] += jnp.broadcast_to(tot, db_ref.shape)

    spec = pl.BlockSpec((None, nh, t), lambda i: (i, 0, 0))
    return _pcall(
        body, name=name, grid=(bs,), in_specs=[spec, spec, pl.BlockSpec((nh, 1), lambda i: (0, 0))],
        out_specs=[spec, pl.BlockSpec((nh, LANES), lambda i: (0, 0))],
        out_shape=[jax.ShapeDtypeStruct((bs, nh, t), F32), jax.ShapeDtypeStruct((nh, LANES), F32)],
        compiler_params=_params(("arbitrary",)),
    )(dkb, lg, bias)


def _place():
    return lax.axis_index("x"), lax.axis_index("y"), lax.axis_index("c")


def _flip(v, f):
    return 1 - v if f else v


FLIPS = [(fx, fy, fc) for fx in (0, 1) for fy in (0, 1) for fc in (0, 1)][1:]


def _comm_sems(nw):
    return [pltpu.SemaphoreType.DMA((7, nw)), pltpu.SemaphoreType.DMA((7, nw)), pltpu.SemaphoreType.DMA((nw,))]


def _gather_cargo(shards, on_done):
    nw = len(shards)

    def parts(x_refs, out_refs, sems):
        send_sems, recv_sems, local_sems = sems
        x, y, cc = _place()
        me, sibling = (x, y, cc), (x, y, 1 - cc)
        chips = [(1 - x, y), (x, 1 - y), (1 - x, 1 - y)]

        def slot(i, px, py, pc):
            return out_refs[i].at[4 * px + 2 * py + pc]

        def copy(i, k, block, to, src=None):
            return pltpu.make_async_remote_copy(
                src_ref=slot(i, *block) if src is None else src, dst_ref=slot(i, *block),
                send_sem=send_sems.at[k, i], recv_sem=recv_sems.at[k, i], device_id=to, device_id_type=MESH)

        mine = [pltpu.make_async_copy(x_refs[i], slot(i, *me), local_sems.at[i]) for i in range(nw)]
        first = []
        for i in range(nw):
            first.append(copy(i, 0, me, sibling, src=x_refs[i]))
            first += [copy(i, 1 + j, me, (*chip, cc), src=x_refs[i]) for j, chip in enumerate(chips)]
        return me, sibling, chips, cc, copy, mine, first

    def start(x_refs, out_refs, sems):
        *_, mine, first = parts(x_refs, out_refs, sems)
        for cp in mine + first:
            cp.start()

    def finish(x_refs, out_refs, sems):
        me, sibling, chips, cc, copy, mine, first = parts(x_refs, out_refs, sems)
        passed = []
        for i in range(nw):
            for j, chip in enumerate(chips):
                copy(i, 1 + j, (*chip, cc), me).wait_recv()
                passed.append(copy(i, 4 + j, (*chip, cc), sibling))
                passed[-1].start()
        for i in range(nw):
            copy(i, 0, sibling, me).wait_recv()
            for j, chip in enumerate(chips):
                copy(i, 4 + j, (*chip, 1 - cc), me).wait_recv()
        for cp in first + passed:
            cp.wait_send()
        for cp in mine:
            cp.wait()

    out_shape = [jax.ShapeDtypeStruct((N_DEV, *s.shape), s.dtype) for s in shards]
    return _Cargo(shards, out_shape, _comm_sems(nw), start, finish, on_done)


def _scatter_cargo(slots, prev, layer, depth, on_done):
    nw = len(slots)

    def parts(refs, recv_refs, sems):
        g_refs = refs[:nw]
        send_sems, recv_sems, local_sems = sems
        x, y, cc = _place()
        my = 4 * x + 2 * y + cc
        mine, copies = [], []
        for i in range(nw):
            mine.append(pltpu.make_async_copy(g_refs[i].at[my], recv_refs[i].at[my, layer], local_sems.at[i]))
            for k, (fx, fy, fc) in enumerate(FLIPS):
                px, py, pc = _flip(x, fx), _flip(y, fy), _flip(cc, fc)
                copies.append(pltpu.make_async_remote_copy(
                    src_ref=g_refs[i].at[4 * px + 2 * py + pc], dst_ref=recv_refs[i].at[my, layer],
                    send_sem=send_sems.at[k, i], recv_sem=recv_sems.at[k, i], device_id=(px, py, pc), device_id_type=MESH))
        return mine, copies

    def start(refs, recv_refs, sems):
        mine, copies = parts(refs, recv_refs, sems)
        for cp in mine + copies:
            cp.start()

    def finish(refs, recv_refs, sems):
        mine, copies = parts(refs, recv_refs, sems)
        for cp in copies:
            cp.wait_recv()
        for cp in copies:
            cp.wait_send()
        for cp in mine:
            cp.wait()

    ins, aliases = list(slots), {}
    for i, p in enumerate(prev):
        if p is not None:
            aliases[len(ins)] = i
            ins.append(p)
    out_shape = [jax.ShapeDtypeStruct((N_DEV, depth, *s.shape[1:]), s.dtype) for s in slots]
    return _Cargo(ins, out_shape, _comm_sems(nw), start, finish, on_done, aliases)


def _exchange(cargo, *, name):
    def body(*refs):
        c_in = len(cargo.ins)
        c_out = len(cargo.out_shape)
        cargo.start(refs[:c_in], refs[c_in:c_in + c_out], refs[c_in + c_out:])
        cargo.finish(refs[:c_in], refs[c_in:c_in + c_out], refs[c_in + c_out:])

    hbm = pl.BlockSpec(memory_space=pl.ANY)
    res = _pcall(
        body, name=name, in_specs=[hbm] * len(cargo.ins), out_specs=[hbm] * len(cargo.out_shape), out_shape=cargo.out_shape,
        scratch_shapes=cargo.sems, input_output_aliases=dict(cargo.aliases),
    )(*cargo.ins)
    cargo.on_done(list(res))


def _allreduce_small(blob, *, name):
    r, c = blob.shape

    def body(x_ref, out_ref, buf, send_sems, recv_sems):
        x, y, cc = _place()
        my = 4 * x + 2 * y + cc
        copies = []
        for k, (fx, fy, fc) in enumerate(FLIPS):
            peer = (_flip(x, fx), _flip(y, fy), _flip(cc, fc))
            copies.append(pltpu.make_async_remote_copy(
                src_ref=x_ref, dst_ref=buf.at[my], send_sem=send_sems.at[k], recv_sem=recv_sems.at[k],
                device_id=peer, device_id_type=MESH))
        for cp in copies:
            cp.start()
        buf[my] = x_ref[...]
        for cp in copies:
            cp.wait_recv()
        for cp in copies:
            cp.wait_send()
        acc = buf[0]
        for i in range(1, N_DEV):
            acc = acc + buf[i]
        out_ref[...] = acc

    vmem = pl.BlockSpec(memory_space=pltpu.VMEM)
    return _pcall(
        body, name=name, in_specs=[vmem], out_specs=vmem, out_shape=jax.ShapeDtypeStruct((r, c), F32),
        scratch_shapes=[pltpu.VMEM((N_DEV, r, c), F32), pltpu.SemaphoreType.DMA((7,)), pltpu.SemaphoreType.DMA((7,))],
    )(blob)


BIG = ("w_in", "w_mlp_in", "w_mlp_out", "w_up_fox", "w_up_sb", "w_up_dil", "w_out")
ROW_SHARDED = ("w_out", "w_mlp_out")
SMALL = ("attn_norm", "b_forget", "q_norm_fox", "k_norm_fox", "q_norm_dil", "k_norm_dil", "mlp_norm")
BLOB_ROWS = 512


def _pack(parts, dtype):
    flat = jnp.concatenate([p.reshape(-1).astype(dtype) for p in parts])
    size = -(-flat.shape[0] // (BLOB_ROWS * LANES)) * (BLOB_ROWS * LANES)
    return jnp.pad(flat, (0, size - flat.shape[0])).reshape(-1, LANES)


def _unpack(blob, shapes):
    flat = blob.reshape(-1)
    out, off = [], 0
    for shp in shapes:
        size = 1
        for s in shp:
            size *= s
        out.append(flat[off:off + size].reshape(shp))
        off += size
    return out


def _join_shards(name, sh):
    if name in ROW_SHARDED:
        return sh.reshape(-1, sh.shape[2])
    return jnp.transpose(sh, (1, 0, 2)).reshape(sh.shape[1], -1)


def _split_shards(name, full):
    a, b = full.shape
    if name in ROW_SHARDED:
        return full.reshape(N_DEV, a // N_DEV, b)
    return jnp.transpose(full.reshape(a, N_DEV, b // N_DEV), (1, 0, 2))


def _permute_in(w, dp):
    o1 = 3 * W_FOX
    o2 = o1 + N_HEADS_FOX
    pad = [(0, 0)] * (w.ndim - 1) + [(0, dp - w.shape[-1])]
    return jnp.pad(jnp.concatenate([w[..., :o1], w[..., o2:], w[..., o1:o2]], axis=-1), pad)


def _unpermute_in(wp, d_in):
    o1 = 3 * W_FOX
    fg = d_in - N_HEADS_FOX
    return jnp.concatenate([wp[..., :o1], wp[..., fg:d_in], wp[..., o1:fg]], axis=-1)


def _stat_to_tokens(st, r, b):
    hh = st.shape[1]
    n = st.shape[2] * st.shape[4]
    return jnp.transpose(st.reshape(b, r, hh, n), (0, 2, 3, 1)).reshape(b, hh, n * r)


def _stat_to_streams(tok, r, blk):
    b, hh, t = tok.shape
    n = t // r
    return jnp.transpose(tok.reshape(b, hh, n, r), (0, 3, 1, 2)).reshape(b * r, hh, n // blk, 1, blk)


def _rope_tables(positions):
    half = HEAD_DIM // 2
    inv = 1.0 / (ROPE_THETA ** (jnp.arange(half, dtype=F32) / half))
    ang = positions.astype(F32)[..., None] * inv
    cos, sin = jnp.cos(ang), jnp.sin(ang)
    return jnp.tile(cos, (1, 1, 4)), jnp.tile(jnp.concatenate([-sin, sin], axis=-1), (1, 1, 2))


def _gain2(g):
    return jnp.tile(g.reshape(1, HEAD_DIM), (1, 2))


def _dil_offs(g):
    c0 = (P_DIL + g * W_DIL) // LANES
    return c0, c0 + W_DILQ // LANES, c0 + 2 * W_DILQ // LANES


def _layer_fwd(l, x, w, small, ropes, bl, t, cargo):
    n, d = x.shape
    s = {}
    s["x"] = x
    s["h"] = _rmsnorm_fwd(x, small["attn_norm"][l].reshape(1, d), name=f"norm_attn_fwd{l}")
    proj = _mm(s["h"], w["w_in"][l], tn=PROJ_TILE, name=f"mm_proj{l}")
    s["proj"] = proj
    dp = proj.shape[1]
    proj3 = proj.reshape(bl, t, dp)
    p_fg = P_GATE + 3 * d

    lg = jnp.transpose(proj3[:, :, p_fg:p_fg + N_HEADS_FOX], (0, 2, 1))
    s["lg"] = lg
    kb = _fox_gate_fwd(lg, small["b_forget"][l].reshape(N_HEADS_FOX, 1), name=f"fox_gate_fwd{l}")
    blk = _att_blk(t)
    kb5 = kb.reshape(bl, N_HEADS_FOX, t // blk, 1, blk)
    s["kb5"] = kb5
    gqf, gkf = _gain2(small["q_norm_fox"][l]), _gain2(small["k_norm_fox"][l])
    fo = P_FOX // LANES
    fox_offs = (fo, fo + W_FOX // LANES, fo + 2 * W_FOX // LANES)
    out_a, lse_a = _attn_fwd(proj3, fox_offs, N_HEADS_FOX // 2, gqf, gkf, kbias=kb5, window=t, name=f"fox_fwd{l}",
                             cargo=cargo.get("fox_fwd"))
    s["out_a"], s["lse_a"] = out_a, lse_a

    so = P_SB // LANES
    sb_offs = (so, so + W_SB // LANES, so + 2 * W_SB // LANES)
    out_b, lt_b = _sb_fwd(proj3, sb_offs, N_HEADS_SB // 2, name=f"sb_fwd{l}", cargo=cargo.get("sb_fwd"))
    s["out_b"], s["lt_b"] = out_b, lt_b

    gqd, gkd = _gain2(small["q_norm_dil"][l]), _gain2(small["k_norm_dil"][l])
    os_, lses = [], []
    for g, (window, r) in enumerate(DIL_PATTERNS):
        o_g, lse_g = _attn_fwd(proj3, _dil_offs(g), N_HEADS_DIL // 2, gqd, gkd, rope=ropes, window=window // r, stride=r,
                               name=f"dil_fwd{l}_{g}")
        os_.append(o_g.reshape(n, W_DIL))
        lses.append(_stat_to_tokens(lse_g, r, bl).reshape(bl * N_HEADS_DIL, t))
    lse_c, *ws = _dil_weights(lses, name=f"dil_weights{l}")
    ws = [jnp.repeat(jnp.transpose(wg.reshape(bl, N_HEADS_DIL, t), (0, 2, 1)).reshape(n, N_HEADS_DIL), HEAD_DIM, axis=1) for wg in ws]
    out_c = _dil_mix(os_, ws, name=f"dil_mix{l}")
    s["out_c"], s["lse_c"] = out_c, lse_c.reshape(bl, N_HEADS_DIL, t)

    ys = [_mm(out_a.reshape(n, W_FOX), w["w_up_fox"][l], name=f"mm_up_fox{l}"),
          _mm(out_b.reshape(n, W_SB), w["w_up_sb"][l], name=f"mm_up_sb{l}"),
          _mm(out_c, w["w_up_dil"][l], name=f"mm_up_dil{l}")]
    s["ys"] = ys
    s["merged"] = _gate_merge_fwd(proj, ys, name=f"gate_merge_fwd{l}")
    x1 = _mm(s["merged"], w["w_out"][l], add=x, name=f"mm_out{l}")
    s["x1"] = x1

    s["h2"] = _rmsnorm_fwd(x1, small["mlp_norm"][l].reshape(1, d), name=f"norm_mlp_fwd{l}")
    s["u"], s["a"] = _mm(s["h2"], w["w_mlp_in"][l], relu2=True, out_dtype=BF16, name=f"mm_mlp_in{l}")
    x2 = _mm(s["a"], w["w_mlp_out"][l], add=x1, name=f"mm_mlp_out{l}")
    return x2, s


def _layer_bwd(l, dx2, s, w, small, ropes, bl, t, hooks):
    n, d = dx2.shape
    gw, gs = {}, {}

    def cargo(call):
        return hooks[call](gw) if call in hooks else None
    du = _mm(dx2, w["w_mlp_out"][l], tb=True, relu_grad_of=s["u"], out_dtype=BF16, name=f"mm_du{l}")
    gw["w_mlp_out"] = _mm(s["a"], dx2, ta=True, name=f"mm_dw_mlp_out{l}")
    gw["w_mlp_in"] = _mm(s["h2"], du, ta=True, name=f"mm_dw_mlp_in{l}")
    dh2 = _mm(du, w["w_mlp_in"][l], tb=True, name=f"mm_dh2{l}")
    dx1, gs["mlp_norm"] = _rmsnorm_bwd(s["x1"], small["mlp_norm"][l].reshape(1, d), dh2, dx2, name=f"norm_mlp_bwd{l}")

    dmerged = _mm(dx1, w["w_out"][l], tb=True, name=f"mm_dmerged{l}")
    gw["w_out"] = _mm(s["merged"], dx1, ta=True, name=f"mm_dw_out{l}")
    dya, dyb, dyc, dgl0, dgl1, dgl2 = _gate_merge_bwd(s["proj"], s["ys"], dmerged, name=f"gate_merge_bwd{l}")
    out_a2, out_b2 = s["out_a"].reshape(n, W_FOX), s["out_b"].reshape(n, W_SB)
    gw["w_up_fox"] = _mm(out_a2, dya, ta=True, name=f"mm_dw_up_fox{l}")
    gw["w_up_sb"] = _mm(out_b2, dyb, ta=True, name=f"mm_dw_up_sb{l}")
    gw["w_up_dil"] = _mm(s["out_c"], dyc, ta=True, name=f"mm_dw_up_dil{l}")
    dout_a = _mm(dya, w["w_up_fox"][l], tb=True, name=f"mm_dout_a{l}").reshape(bl, t, W_FOX)
    dout_b = _mm(dyb, w["w_up_sb"][l], tb=True, name=f"mm_dout_b{l}").reshape(bl, t, W_SB)
    dout_c = _mm(dyc, w["w_up_dil"][l], tb=True, name=f"mm_dout_c{l}").reshape(bl, t, W_DIL)

    proj3 = s["proj"].reshape(bl, t, -1)
    gqf, gkf = _gain2(small["q_norm_fox"][l]), _gain2(small["k_norm_fox"][l])
    fo = P_FOX // LANES
    fox_offs = (fo, fo + W_FOX // LANES, fo + 2 * W_FOX // LANES)
    dq_a, dk_a, dv_a, dg_a, dkb5 = _attn_bwd(proj3, fox_offs, N_HEADS_FOX // 2, gqf, gkf, s["out_a"], dout_a, s["lse_a"],
                                             kbias=s["kb5"], window=t, name=f"fox_bwd{l}", cargo=cargo("fox_bwd"))
    gs["fox_gains"] = dg_a
    dlg, gs["b_forget"] = _fox_gate_bwd(dkb5.reshape(bl, N_HEADS_FOX, t), s["lg"], small["b_forget"][l].reshape(N_HEADS_FOX, 1),
                                        name=f"fox_gate_bwd{l}")
    so = P_SB // LANES
    sb_offs = (so, so + W_SB // LANES, so + 2 * W_SB // LANES)
    dq_b, dk_b, dv_b = _sb_bwd(proj3, sb_offs, N_HEADS_SB // 2, dout_b, s["lt_b"], name=f"sb_bwd{l}", cargo=cargo("sb_bwd"))
    gqd, gkd = _gain2(small["q_norm_dil"][l]), _gain2(small["k_norm_dil"][l])
    out_c3 = s["out_c"].reshape(bl, t, W_DIL)
    dqs, dks, dvs, dgd = [], [], [], None
    for g, (window, r) in enumerate(DIL_PATTERNS):
        lse_g = _stat_to_streams(s["lse_c"], r, _att_blocks(t // r)[0])
        dq_g, dk_g, dv_g, dg_g = _attn_bwd(proj3, _dil_offs(g), N_HEADS_DIL // 2, gqd, gkd, out_c3, dout_c, lse_g, rope=ropes,
                                           window=window // r, stride=r, name=f"dil_bwd{l}_{g}")
        dqs.append(dq_g)
        dks.append(dk_g)
        dvs.append(dv_g)
        dgd = dg_g if dgd is None else jnp.concatenate([dgd, dg_g], axis=0)
    gs["dil_gains"] = dgd

    dp = s["proj"].shape[1]
    parts = [dq_a, dk_a, dv_a, dq_b, dk_b, dv_b] + dqs + dks + dvs
    parts = [p.reshape(n, -1).astype(BF16) for p in parts] + [dgl0, dgl1, dgl2, jnp.transpose(dlg, (0, 2, 1)).reshape(n, -1).astype(BF16)]
    used = P_GATE + 3 * d + N_HEADS_FOX
    dproj = jnp.concatenate(parts + [jnp.zeros((n, dp - used), BF16)], axis=1)
    gw["w_in"] = _mm(s["h"], dproj, ta=True, tn=PROJ_TILE, name=f"mm_dw_in{l}")
    dh = _mm(dproj, w["w_in"][l], tb=True, tn=1024, tk=PROJ_TILE, name=f"mm_dh{l}", cargo=cargo("mm_dh"))
    dx, gs["attn_norm"] = _rmsnorm_bwd(s["x"], small["attn_norm"][l].reshape(1, d), dh, dx1, name=f"norm_attn_bwd{l}")
    return dx, gw, gs


def kernel(x, positions, attn_norm, w_in, b_forget, q_norm_fox, k_norm_fox, q_norm_dil, k_norm_dil, w_up_fox, w_up_sb, w_up_dil, w_out, mlp_norm, w_mlp_in, w_mlp_out, loss_target, m_attn_norm, m_w_in, m_b_forget, m_q_norm_fox, m_k_norm_fox, m_q_norm_dil, m_k_norm_dil, m_w_up_fox, m_w_up_sb, m_w_up_dil, m_w_out, m_mlp_norm, m_w_mlp_in, m_w_mlp_out, v_attn_norm, v_w_in, v_b_forget, v_q_norm_fox, v_k_norm_fox, v_q_norm_dil, v_k_norm_dil, v_w_up_fox, v_w_up_sb, v_w_up_dil, v_w_out, v_mlp_norm, v_w_mlp_in, v_w_mlp_out):
    bl, t, d = x.shape
    n = bl * t
    depth = attn_norm.shape[0]
    wl = dict(w_in=w_in, w_up_fox=w_up_fox, w_up_sb=w_up_sb, w_up_dil=w_up_dil, w_out=w_out, w_mlp_in=w_mlp_in, w_mlp_out=w_mlp_out)
    ml = dict(w_in=m_w_in, w_up_fox=m_w_up_fox, w_up_sb=m_w_up_sb, w_up_dil=m_w_up_dil, w_out=m_w_out, w_mlp_in=m_w_mlp_in, w_mlp_out=m_w_mlp_out)
    vl = dict(w_in=v_w_in, w_up_fox=v_w_up_fox, w_up_sb=v_w_up_sb, w_up_dil=v_w_up_dil, w_out=v_w_out, w_mlp_in=v_w_mlp_in, w_mlp_out=v_w_mlp_out)
    small = dict(attn_norm=attn_norm, b_forget=b_forget, q_norm_fox=q_norm_fox, k_norm_fox=k_norm_fox, q_norm_dil=q_norm_dil,
                 k_norm_dil=k_norm_dil, mlp_norm=mlp_norm)
    m_small = dict(attn_norm=m_attn_norm, b_forget=m_b_forget, q_norm_fox=m_q_norm_fox, k_norm_fox=m_k_norm_fox,
                   q_norm_dil=m_q_norm_dil, k_norm_dil=m_k_norm_dil, mlp_norm=m_mlp_norm)
    v_small = dict(attn_norm=v_attn_norm, b_forget=v_b_forget, q_norm_fox=v_q_norm_fox, k_norm_fox=v_k_norm_fox,
                   q_norm_dil=v_q_norm_dil, k_norm_dil=v_k_norm_dil, mlp_norm=v_mlp_norm)

    d_in = w_in.shape[-1] * N_DEV
    dp = -(-d_in // 512) * 512
    rest = [k for k in BIG if k != "w_in"]
    w = {k: [None] * depth for k in BIG}

    def gather(items):
        def done(res):
            for (k, l), sh in zip(items, res):
                whole = _join_shards(k, sh)
                w[k][l] = _permute_in(whole, dp) if k == "w_in" else whole

        return _gather_cargo([wl[k][l].astype(BF16) for k, l in items], done)

    _exchange(gather([("w_in", 0)]), name="gather_first")
    cos, sin = _rope_tables(positions)
    ropes = (cos, sin)

    h = x.reshape(n, d)
    saved = []
    for l in range(depth):
        cargo = {"fox_fwd": gather([(k, l) for k in rest])}
        if l + 1 < depth:
            cargo["sb_fwd"] = gather([("w_in", l + 1)])
        h, s = _layer_fwd(l, h, w, small, ropes, bl, t, cargo)
        saved.append(s)
    dy, loss_part = _loss_head(h, loss_target.reshape(n, d), name="loss_head")

    recv = {}

    def scatter(names, l, grads):
        def done(res):
            recv.update(zip(names, res))

        slots = [_split_shards(k, _unpermute_in(grads[k], d_in) if k == "w_in" else grads[k]).astype(BF16) for k in names]
        return _scatter_cargo(slots, [recv.get(k) for k in names], l, depth, done)

    gss = [None] * depth
    for l in reversed(range(depth)):
        hooks = {"fox_bwd": lambda gw, l=l: scatter(rest, l, gw), "mm_dh": lambda gw, l=l: scatter(["w_in"], l, gw)}
        dy, _, gss[l] = _layer_bwd(l, dy, saved[l], w, small, ropes, bl, t, hooks)
    grad_x = dy.reshape(bl, t, d)

    g_big, d_big, m_big, v_big = {}, {}, {}, {}
    for k in BIG:
        g_big[k], d_big[k], m_big[k], v_big[k] = _adamw(recv[k], wl[k], ml[k], vl[k], name=f"adamw_{k}")

    rows = [loss_part]
    for l in range(depth):
        gs = gss[l]
        rows += [gs["attn_norm"].reshape(-1, LANES), gs["mlp_norm"].reshape(-1, LANES), gs["fox_gains"], gs["dil_gains"], gs["b_forget"]]
    row_counts = [r.shape[0] for r in rows]
    part = jnp.concatenate(rows, axis=0)
    pad_rows = -(-part.shape[0] // 8) * 8 - part.shape[0]
    summed = _allreduce_small(jnp.pad(part, ((0, pad_rows), (0, 0))), name="allreduce_small")
    pieces, off = [], 0
    for c in row_counts:
        pieces.append(summed[off:off + c])
        off += c
    loss = pieces[0][0, 0]

    def fold(row):
        return row[:HEAD_DIM] + row[HEAD_DIM:]

    g_small = {k: [] for k in SMALL}
    for l in range(depth):
        an, mn, fg, dg, bf = pieces[1 + 5 * l:6 + 5 * l]
        g_small["attn_norm"].append(an.reshape(d))
        g_small["mlp_norm"].append(mn.reshape(d))
        g_small["q_norm_fox"].append(fold(fg[0]))
        g_small["k_norm_fox"].append(fold(fg[1]))
        g_small["q_norm_dil"].append(fold(dg[0]) + fold(dg[8]) + fold(dg[16]))
        g_small["k_norm_dil"].append(fold(dg[1]) + fold(dg[9]) + fold(dg[17]))
        g_small["b_forget"].append(bf[:, 0])
    g_small = {k: jnp.stack(vs) for k, vs in g_small.items()}
    small_shapes = [small[k].shape for k in SMALL]
    outs = _adamw(_pack([g_small[k] for k in SMALL], F32)[None, None], _pack([small[k] for k in SMALL], F32)[None],
                  _pack([m_small[k] for k in SMALL], F32)[None], _pack([v_small[k] for k in SMALL], F32)[None], name="adamw_small")
    g_sm, d_sm, m_sm, v_sm = (dict(zip(SMALL, _unpack(o, small_shapes))) for o in outs)

    order = ("attn_norm", "w_in", "b_forget", "q_norm_fox", "k_norm_fox", "q_norm_dil", "k_norm_dil", "w_up_fox", "w_up_sb",
             "w_up_dil", "w_out", "mlp_norm", "w_mlp_in", "w_mlp_out")
    res = [loss, grad_x]
    for big, sm in ((g_big, g_sm), (d_big, d_sm), (m_big, m_sm), (v_big, v_sm)):
        res += [big[k] if k in big else sm[k] for k in order]
    return tuple(res)
```

```python
import jax
import jax.numpy as jnp
from jax import lax
from jax.experimental import pallas as pl
from jax.experimental.pallas import tpu as pltpu

F32 = jnp.float32
BF16 = jnp.bfloat16

HEAD_DIM = 64
LANES = 128
N_HEADS_FOX = 8
N_HEADS_SB = 8
N_HEADS_DIL = 4
DIL_PATTERNS = ((128, 1), (512, 4), (2048, 16))
ROPE_THETA = 10000.0
EPS = 1e-6
SCALE = 0.125
W_FOX = N_HEADS_FOX * HEAD_DIM
W_SB = N_HEADS_SB * HEAD_DIM
W_DIL = N_HEADS_DIL * HEAD_DIM
W_DILQ = len(DIL_PATTERNS) * W_DIL
P_FOX = 0
P_SB = 3 * W_FOX
P_DIL = P_SB + 3 * W_SB
P_GATE = P_DIL + 3 * W_DILQ
N_DEV = 8
ATT_BLK = 256
ATT_BQ = 512
NEG = -1e30
VMEM_LIMIT = 56 * 1024 * 1024
ADAMW_BLOCK_ELEMS = 128 * 1024
PROJ_TILE = 2176

ADAM_LR = 0.001
ADAM_B1 = 0.9
ADAM_B2 = 0.999
ADAM_EPS = 1e-08
ADAM_WD = 0.01
ADAM_STEP = 10

NT = (((1,), (1,)), ((), ()))
MESH = pl.DeviceIdType.MESH


def _pcall(body, **kw):
    return pl.pallas_call(body, **kw)


def _params(sem=None):
    return pltpu.CompilerParams(dimension_semantics=sem, vmem_limit_bytes=VMEM_LIMIT)


class _Cargo:
    def __init__(self, ins, out_shape, sems, start, finish, on_done, aliases=None):
        self.ins, self.out_shape, self.sems = list(ins), list(out_shape), list(sems)
        self.start, self.finish, self.on_done, self.aliases = start, finish, on_done, dict(aliases or {})


def _call(body, *, cargo=None, name, grid=(), in_specs, out_specs, out_shape, scratch_shapes=(), compiler_params=None):
    if cargo is None:
        kw = dict(grid=grid) if grid else {}
        if compiler_params is not None:
            kw["compiler_params"] = compiler_params
        return _pcall(body, name=name, in_specs=in_specs, out_specs=out_specs, out_shape=out_shape,
                      scratch_shapes=list(scratch_shapes), **kw)
    single = not isinstance(out_shape, (list, tuple))
    o_specs, o_shape = ([out_specs], [out_shape]) if single else (list(out_specs), list(out_shape))
    n_in, n_out, n_scr = len(in_specs), len(o_shape), len(scratch_shapes)
    c_in, c_out = len(cargo.ins), len(cargo.out_shape)

    def wrapped(*refs):
        ins, cins = refs[:n_in], refs[n_in:n_in + c_in]
        o0 = n_in + c_in
        outs, couts = refs[o0:o0 + n_out], refs[o0 + n_out:o0 + n_out + c_out]
        s0 = o0 + n_out + c_out
        scr, sems = refs[s0:s0 + n_scr], refs[s0 + n_scr:]
        first = last = None
        for ax, size in enumerate(grid):
            pid = pl.program_id(ax)
            first = (pid == 0) if first is None else first & (pid == 0)
            last = (pid == size - 1) if last is None else last & (pid == size - 1)
        if first is None:
            cargo.start(cins, couts, sems)
            body(*ins, *outs, *scr)
            cargo.finish(cins, couts, sems)
            return

        @pl.when(first)
        def _():
            cargo.start(cins, couts, sems)

        body(*ins, *outs, *scr)

        @pl.when(last)
        def _():
            cargo.finish(cins, couts, sems)

    hbm = pl.BlockSpec(memory_space=pl.ANY)
    kw = dict(grid=grid, compiler_params=_params(("arbitrary",) * len(grid))) if grid else {}
    call = _pcall(
        wrapped, name=name, in_specs=list(in_specs) + [hbm] * c_in, out_specs=o_specs + [hbm] * c_out,
        out_shape=o_shape + cargo.out_shape, scratch_shapes=list(scratch_shapes) + cargo.sems,
        input_output_aliases={n_in + i: n_out + j for i, j in cargo.aliases.items()}, **kw)

    def run(*args):
        res = call(*args, *cargo.ins)
        cargo.on_done(list(res[n_out:]))
        return res[0] if single else list(res[:n_out])

    return run


def _tile(dim, target, mult=LANES):
    t = (min(dim, target) // mult) * mult
    while t >= mult:
        if dim % t == 0:
            return t
        t -= mult
    return dim


def _mm(a, b, *, ta=False, tb=False, add=None, relu2=False, relu_grad_of=None, out_dtype=F32, name, tm=1024, tn=1024,
        tk=1024, cargo=None):
    m, k = (a.shape[1], a.shape[0]) if ta else a.shape
    n = b.shape[0] if tb else b.shape[1]
    tm, tn, tk = _tile(m, tm), _tile(n, tn), _tile(k, tk)
    nk = k // tk
    dn = (((0,) if ta else (1,), (1,) if tb else (0,)), ((), ()))

    extra = add if add is not None else relu_grad_of

    def body(*refs):
        a_ref, b_ref = refs[:2]
        x_ref = refs[2] if extra is not None else None
        outs = refs[2 + (extra is not None):-1]
        acc = refs[-1]
        kk = pl.program_id(2)
        part = lax.dot_general(a_ref[...].astype(BF16), b_ref[...].astype(BF16), dn, preferred_element_type=F32)

        def finish(r):
            if add is not None:
                r = r + x_ref[...]
            if relu_grad_of is not None:
                r = r * (2.0 * jnp.maximum(x_ref[...].astype(F32), 0.0))
            outs[0][...] = r.astype(out_dtype)
            if relu2:
                rr = jnp.maximum(r, 0.0)
                outs[1][...] = (rr * rr).astype(BF16)

        if nk == 1:
            finish(part)
            return

        @pl.when(kk == 0)
        def _():
            acc[...] = part

        @pl.when((kk > 0) & (kk < nk - 1))
        def _():
            acc[...] += part

        @pl.when(kk == nk - 1)
        def _():
            finish(acc[...] + part)

    a_spec = pl.BlockSpec((tk, tm), lambda i, j, q: (q, i)) if ta else pl.BlockSpec((tm, tk), lambda i, j, q: (i, q))
    b_spec = pl.BlockSpec((tn, tk), lambda i, j, q: (j, q)) if tb else pl.BlockSpec((tk, tn), lambda i, j, q: (q, j))
    o_spec = pl.BlockSpec((tm, tn), lambda i, j, q: (i, j))
    ins, specs = [a, b], [a_spec, b_spec]
    if extra is not None:
        ins.append(extra)
        specs.append(o_spec)
    sds = jax.ShapeDtypeStruct((m, n), out_dtype)
    return _call(
        body, cargo=cargo, name=name, grid=(m // tm, n // tn, nk), in_specs=specs,
        out_specs=[o_spec, o_spec] if relu2 else o_spec,
        out_shape=[sds, jax.ShapeDtypeStruct((m, n), BF16)] if relu2 else sds,
        scratch_shapes=[pltpu.VMEM((tm, tn) if nk > 1 else (8, LANES), F32)],
        compiler_params=_params(("parallel", "parallel", "arbitrary")),
    )(*ins)


def _rmsnorm_fwd(x, g, *, name):
    n, d = x.shape
    tm = _tile(n, 256, 8)

    def body(x_ref, g_ref, h_ref):
        xv = x_ref[...]
        inv = lax.rsqrt(jnp.mean(xv * xv, axis=1, keepdims=True) + EPS)
        h_ref[...] = (xv * inv * g_ref[...]).astype(BF16)

    row = pl.BlockSpec((tm, d), lambda i: (i, 0))
    return _pcall(
        body, name=name, grid=(n // tm,), in_specs=[row, pl.BlockSpec((1, d), lambda i: (0, 0))], out_specs=row,
        out_shape=jax.ShapeDtypeStruct((n, d), BF16), compiler_params=_params(("parallel",)),
    )(x, g)


def _rmsnorm_bwd(x, g, dh, dres, *, name):
    n, d = x.shape
    tm = _tile(n, 256, 8)

    def body(x_ref, g_ref, dh_ref, dres_ref, dx_ref, dg_ref):
        @pl.when(pl.program_id(0) == 0)
        def _():
            dg_ref[...] = jnp.zeros_like(dg_ref)

        xv = x_ref[...]
        inv = lax.rsqrt(jnp.mean(xv * xv, axis=1, keepdims=True) + EPS)
        y = xv * inv
        dhv = dh_ref[...]
        dg_ref[...] += jnp.sum(dhv * y, axis=0, keepdims=True)
        dy = dhv * g_ref[...]
        dx_ref[...] = dres_ref[...] + inv * (dy - y * jnp.mean(dy * y, axis=1, keepdims=True))

    row = pl.BlockSpec((tm, d), lambda i: (i, 0))
    vec = pl.BlockSpec((1, d), lambda i: (0, 0))
    return _pcall(
        body, name=name, grid=(n // tm,), in_specs=[row, vec, row, row], out_specs=[row, vec],
        out_shape=[jax.ShapeDtypeStruct((n, d), F32), jax.ShapeDtypeStruct((1, d), F32)],
        compiler_params=_params(("arbitrary",)),
    )(x, g, dh, dres)


def _gate_specs(n, d):
    bw = 256 if d % 256 == 0 else LANES
    tm = _tile(n, 512, 8)
    nb = d // bw
    yspec = pl.BlockSpec((tm, bw), lambda i, j: (i, j))
    gspecs = [pl.BlockSpec((tm, bw), lambda i, j, b=b: (i, P_GATE // bw + b * nb + j)) for b in range(3)]
    return tm, bw, nb, yspec, gspecs


def _gate_merge_fwd(proj, ys, *, name):
    n, d = ys[0].shape
    tm, bw, nb, yspec, gspecs = _gate_specs(n, d)

    def body(g0, g1, g2, y0, y1, y2, o_ref):
        acc = jax.nn.sigmoid(g0[...]) * y0[...]
        acc += jax.nn.sigmoid(g1[...]) * y1[...]
        acc += jax.nn.sigmoid(g2[...]) * y2[...]
        o_ref[...] = acc.astype(BF16)

    return _pcall(
        body, name=name, grid=(n // tm, nb), in_specs=gspecs + [yspec] * 3, out_specs=yspec,
        out_shape=jax.ShapeDtypeStruct((n, d), BF16), compiler_params=_params(("parallel", "parallel")),
    )(proj, proj, proj, *ys)


def _gate_merge_bwd(proj, ys, dmerged, *, name):
    n, d = ys[0].shape
    tm, bw, nb, yspec, gspecs = _gate_specs(n, d)

    def body(g0, g1, g2, y0, y1, y2, dm_ref, dy0, dy1, dy2, dgl0, dgl1, dgl2):
        dm = dm_ref[...]
        for g_ref, y_ref, dy_ref, dgl_ref in ((g0, y0, dy0, dgl0), (g1, y1, dy1, dgl1), (g2, y2, dy2, dgl2)):
            s = jax.nn.sigmoid(g_ref[...])
            dy_ref[...] = (dm * s).astype(BF16)
            dgl_ref[...] = (dm * y_ref[...] * s * (1.0 - s)).astype(BF16)

    sds = jax.ShapeDtypeStruct((n, d), BF16)
    return _pcall(
        body, name=name, grid=(n // tm, nb), in_specs=gspecs + [yspec] * 4, out_specs=[yspec] * 6,
        out_shape=[sds] * 6, compiler_params=_params(("parallel", "parallel")),
    )(proj, proj, proj, *ys, dmerged)


def _loss_head(y, tgt, *, name):
    n, d = y.shape
    tm = _tile(n, 256, 8)
    steps = n // tm

    def body(y_ref, t_ref, dy_ref, loss_ref, acc):
        i = pl.program_id(0)

        @pl.when(i == 0)
        def _():
            acc[...] = jnp.zeros_like(acc)

        e = y_ref[...] - t_ref[...]
        dy_ref[...] = e * (1.0 / d)
        acc[...] += jnp.sum(e * e, axis=0, keepdims=True)

        @pl.when(i == steps - 1)
        def _():
            tot = jnp.sum(acc[...], axis=1, keepdims=True) * (0.5 / d)
            loss_ref[...] = jnp.broadcast_to(tot, loss_ref.shape)

    row = pl.BlockSpec((tm, d), lambda i: (i, 0))
    return _pcall(
        body, name=name, grid=(steps,), in_specs=[row, row], out_specs=[row, pl.BlockSpec((8, LANES), lambda i: (0, 0))],
        out_shape=[jax.ShapeDtypeStruct((n, d), F32), jax.ShapeDtypeStruct((8, LANES), F32)],
        scratch_shapes=[pltpu.VMEM((1, d), F32)], compiler_params=_params(("arbitrary",)),
    )(y, tgt)


def _dil_weights(lses, *, name):
    shp = lses[0].shape

    def body(l0, l1, l2, lse_ref, w0, w1, w2):
        a, b, c = l0[...], l1[...], l2[...]
        m = jnp.maximum(jnp.maximum(a, b), c)
        ea, eb, ec = jnp.exp(a - m), jnp.exp(b - m), jnp.exp(c - m)
        den = ea + eb + ec
        lse_ref[...] = m + jnp.log(den)
        w0[...] = ea / den
        w1[...] = eb / den
        w2[...] = ec / den

    vmem = pl.BlockSpec(memory_space=pltpu.VMEM)
    return _pcall(body, name=name, in_specs=[vmem] * 3, out_specs=[vmem] * 4, out_shape=[jax.ShapeDtypeStruct(shp, F32)] * 4)(*lses)


def _dil_mix(os_, ws, *, name):
    n, w = os_[0].shape
    tm = _tile(n, 512, 8)

    def body(o0, o1, o2, w0, w1, w2, out_ref):
        out_ref[...] = w0[...] * o0[...] + w1[...] * o1[...] + w2[...] * o2[...]

    spec = pl.BlockSpec((tm, w), lambda i: (i, 0))
    return _pcall(
        body, name=name, grid=(n // tm,), in_specs=[spec] * 6, out_specs=spec, out_shape=jax.ShapeDtypeStruct((n, w), F32),
        compiler_params=_params(("parallel",)),
    )(*os_, *ws)


def _adamw(gsrc, w, m, v, *, name):
    s, dep, a, b = gsrc.shape
    ta = _tile(a, max(16, (ADAMW_BLOCK_ELEMS // b) // 16 * 16), 16)
    c1 = 1.0 / (1.0 - ADAM_B1 ** ADAM_STEP)
    c2 = 1.0 / (1.0 - ADAM_B2 ** ADAM_STEP)

    def body(gs_ref, w_ref, m_ref, v_ref, g_ref, d_ref, m2_ref, v2_ref):
        g = gs_ref[0].astype(F32)
        for i in range(1, s):
            g = g + gs_ref[i].astype(F32)
        m2 = ADAM_B1 * m_ref[...] + (1.0 - ADAM_B1) * g
        v2 = ADAM_B2 * v_ref[...] + (1.0 - ADAM_B2) * (g * g)
        g_ref[...] = g
        m2_ref[...] = m2
        v2_ref[...] = v2
        d_ref[...] = -ADAM_LR * ((m2 * c1) / (jnp.sqrt(v2 * c2) + ADAM_EPS) + ADAM_WD * w_ref[...])

    spec = pl.BlockSpec((None, ta, b), lambda l, i: (l, i, 0))
    sds = jax.ShapeDtypeStruct((dep, a, b), F32)
    return _pcall(
        body, name=name, grid=(dep, a // ta),
        in_specs=[pl.BlockSpec((s, None, ta, b), lambda l, i: (0, l, i, 0)), spec, spec, spec],
        out_specs=[spec] * 4, out_shape=[sds] * 4, compiler_params=_params(("parallel", "parallel")),
    )(gsrc, w, m, v)


def _mask_a():
    return lax.broadcasted_iota(jnp.int32, (1, LANES), 1) < HEAD_DIM


def _half_sum(x, m_a):
    sa = jnp.sum(jnp.where(m_a, x, 0.0), axis=1, keepdims=True)
    sb = jnp.sum(jnp.where(m_a, 0.0, x), axis=1, keepdims=True)
    return jnp.where(m_a, sa, sb)


def _head_inv(x, m_a):
    return lax.rsqrt(_half_sum(x * x, m_a) * (1.0 / HEAD_DIM) + EPS)


def _swap32(x):
    first = (lax.broadcasted_iota(jnp.int32, (1, LANES), 1) % HEAD_DIM) < (HEAD_DIM // 2)
    return jnp.where(first, pltpu.roll(x, LANES - HEAD_DIM // 2, 1), pltpu.roll(x, HEAD_DIM // 2, 1))


def _tri(blk, rel):
    r = lax.broadcasted_iota(jnp.int32, (blk, blk), 0)
    c = lax.broadcasted_iota(jnp.int32, (blk, blk), 1)
    return jnp.where(rel(r, c), 1.0, 0.0).astype(BF16)


def _cumdot(x, u, parts):
    acc = None
    r = x
    for i in range(parts):
        xi = r.astype(BF16)
        t = jnp.dot(xi, u, preferred_element_type=F32)
        acc = t if acc is None else acc + t
        if i + 1 < parts:
            r = r - xi.astype(F32)
    return acc


def _rows(i, blk):
    return pl.ds(pl.multiple_of(i * blk, blk), blk)


def _col_spec(n, off):
    return pl.BlockSpec((None, n, LANES), lambda z, p, off=off: (z, 0, off + p))


def _att_blk(n):
    return ATT_BLK if n % ATT_BLK == 0 else min(LANES, n)


def _att_blocks(n):
    bk = _att_blk(n)
    return (ATT_BQ if n % ATT_BQ == 0 else bk), bk


def _loop(lo, hi, fn):
    def it(i, c):
        fn(i)
        return c

    lax.fori_loop(lo, hi, it, 0)


def _normed(src, g_ref, rope_refs, rows, m_a):
    xv = src[rows, :]
    xn = xv * _head_inv(xv, m_a) * g_ref[...]
    if rope_refs is not None:
        xn = xn * rope_refs[0][rows, :] + _swap32(xn) * rope_refs[1][rows, :]
    return xn


def _bias_lane(h):
    return HEAD_DIM if h == 0 else 0


def _k_for_head(kn, kb_row, h, m_h, lane, blk):
    out = jnp.where(m_h, kn, 0.0)
    if kb_row is not None:
        col = jnp.transpose(jnp.broadcast_to(kb_row, (LANES, blk)))
        hi = col.astype(BF16).astype(F32)
        mid = (col - hi).astype(BF16).astype(F32)
        lo = col - hi - mid
        b = _bias_lane(h)
        out = jnp.where(lane == b, hi, jnp.where(lane == b + 1, mid, jnp.where(lane == b + 2, lo, out)))
    return out.astype(BF16)


def _q_for_head(qb, h, m_h, lane, biased):
    out = jnp.where(m_h, qb, 0)
    if biased:
        b = _bias_lane(h)
        out = jnp.where((lane >= b) & (lane < b + 3), jnp.ones_like(out), out)
    return out


def _head_rows(x, parts=3):
    rr = lax.broadcasted_iota(jnp.int32, (8, LANES), 0)
    ll = lax.broadcasted_iota(jnp.int32, (8, LANES), 1)
    sel = jnp.where(((rr == 0) & (ll < HEAD_DIM)) | ((rr == 1) & (ll >= HEAD_DIM)), 1.0, 0.0).astype(BF16)
    acc = None
    rem = x
    for i in range(parts):
        xi = rem.astype(BF16)
        t = lax.dot_general(sel, xi, NT, preferred_element_type=F32)
        acc = t if acc is None else acc + t
        if i + 1 < parts:
            rem = rem - xi.astype(F32)
    return acc


def _cumdot_left(u, x, parts):
    acc = None
    rem = x
    for i in range(parts):
        xi = rem.astype(BF16)
        t = jnp.dot(u, xi, preferred_element_type=F32)
        acc = t if acc is None else acc + t
        if i + 1 < parts:
            rem = rem - xi.astype(F32)
    return acc


def _q_minus_k(bk, bq):
    return lax.broadcasted_iota(jnp.int32, (bk, bq), 1) - lax.broadcasted_iota(jnp.int32, (bk, bq), 0)


def _stat_spec(nb, blk):
    return pl.BlockSpec((None, 2, nb, 1, blk), lambda z, p: (z, p, 0, 0, 0))


def _stream_rows(stride):
    if stride == 1:
        return _rows
    c = pl.program_id(2)
    return lambda i, blk: pl.ds(c + i * (blk * stride), blk, stride=stride)


def _attn_specs(t, npairs, stride, nq, bq):
    col = lambda off: pl.BlockSpec((None, t, LANES), lambda z, p, c: (z, 0, off + p))
    vec = pl.BlockSpec((1, LANES), lambda z, p, c: (0, 0))
    seq = pl.BlockSpec((None, t, LANES), lambda z, p, c: (z, 0, 0))
    stat = pl.BlockSpec((None, 2, nq, 1, bq), lambda z, p, c: (z * stride + c, p, 0, 0, 0))
    return col, vec, seq, stat


def _attn_fwd(src, offs, npairs, gq, gk, *, rope=None, kbias=None, window, stride=1, name, cargo=None):
    bs, t, _ = src.shape
    n = t // stride
    bq, bk = _att_blocks(n)
    nq, nk, rq = n // bq, n // bk, bq // bk
    full = window >= n
    wblk = -(-window // bk)
    biased = kbias is not None

    def body(*refs):
        it = iter(refs)
        q_ref, k_ref, v_ref, gq_ref, gk_ref = (next(it) for _ in range(5))
        rope_refs = (next(it), next(it)) if rope is not None else None
        kb_ref = next(it) if biased else None
        o_ref, lse_ref, qn_s, kh_s, vt_s, acc_s, m_s = (next(it) for _ in range(7))
        m_a = _mask_a()
        masks = (m_a, jnp.logical_not(m_a))
        lane = lax.broadcasted_iota(jnp.int32, (1, LANES), 1)
        row = lax.broadcasted_iota(jnp.int32, (LANES, 1), 0)
        tok = _stream_rows(stride)

        def prep(c):
            rows = _rows(c, bk)
            trows = tok(c, bk)
            qn_s[rows, :] = (_normed(q_ref, gq_ref, rope_refs, trows, m_a) * SCALE).astype(BF16)
            kn = _normed(k_ref, gk_ref, rope_refs, trows, m_a)
            vt = jnp.transpose(v_ref[trows, :])
            for h in (0, 1):
                kh_s[h, rows, :] = _k_for_head(kn, kb_ref[h, c] if biased else None, h, masks[h], lane, bk)
                vt_s[h, c] = jnp.where(row == _bias_lane(h), 1.0, vt).astype(BF16)

        _loop(0, nk, prep)
        qk = _q_minus_k(bk, bq)

        def qblock(qi):
            rows = _rows(qi, bq)
            qb = qn_s[rows, :]
            qh = [_q_for_head(qb, h, masks[h], lane, biased) for h in (0, 1)]
            m_s[...] = jnp.full(m_s.shape, NEG, F32)
            acc_s[...] = jnp.zeros_like(acc_s)

            def step(kj, masked):
                cols = _rows(kj, bk)
                sts = [lax.dot_general(kh_s[h, cols, :], qh[h], NT, preferred_element_type=F32) for h in (0, 1)]
                old = [(m_s[h], acc_s[h]) for h in (0, 1)]
                if masked:
                    d = qk + (qi * bq - kj * bk)
                    ok = (d >= 0) & (d <= window)
                    sts = [jnp.where(ok, st, NEG) for st in sts]
                new = []
                for h in (0, 1):
                    m, acc = old[h]
                    m2 = jnp.maximum(m, jnp.max(sts[h], axis=0, keepdims=True))
                    pt = jnp.exp(sts[h] - m2).astype(BF16)
                    new.append((m2, jnp.exp(m - m2) * acc + jnp.dot(vt_s[h, kj], pt, preferred_element_type=F32)))
                for h in (0, 1):
                    m_s[h], acc_s[h] = new[h]

            if full:
                _loop(0, qi * rq, lambda kj: step(kj, False))
                _loop(qi * rq, (qi + 1) * rq, lambda kj: step(kj, True))
            else:
                _loop(jnp.maximum(qi * rq - wblk, 0), (qi + 1) * rq, lambda kj: step(kj, True))
            outs = []
            for h in (0, 1):
                acc_t = acc_s[h]
                den = acc_t[_bias_lane(h):_bias_lane(h) + 1, :]
                outs.append(jnp.transpose(acc_t / den))
                lse_ref[h, qi] = m_s[h] + jnp.log(den)
            o_ref[tok(qi, bq), :] = jnp.where(m_a, outs[0], outs[1])

        _loop(0, nq, qblock)

    col, vec, seq, stat = _attn_specs(t, npairs, stride, nq, bq)
    ins = [src, src, src, gq, gk]
    specs = [col(offs[0]), col(offs[1]), col(offs[2]), vec, vec]
    if rope is not None:
        ins += list(rope)
        specs += [seq, seq]
    if biased:
        ins.append(kbias)
        specs.append(pl.BlockSpec((None, 2, nk, 1, bk), lambda z, p, c: (z, p, 0, 0, 0)))
    scratch = [pltpu.VMEM((n, LANES), BF16), pltpu.VMEM((2, n, LANES), BF16), pltpu.VMEM((2, nk, LANES, bk), BF16)]
    scratch += [pltpu.VMEM((2, LANES, bq), F32), pltpu.VMEM((2, 1, bq), F32)]
    return _call(
        body, cargo=cargo, name=name, grid=(bs, npairs, stride), in_specs=specs, out_specs=[col(0), stat],
        out_shape=[jax.ShapeDtypeStruct((bs, t, LANES * npairs), F32),
                   jax.ShapeDtypeStruct((bs * stride, 2 * npairs, nq, 1, bq), F32)],
        scratch_shapes=scratch, compiler_params=_params(("parallel", "parallel", "arbitrary")),
    )(*ins)


def _attn_bwd(src, offs, npairs, gq, gk, o, do, lse, *, rope=None, kbias=None, window, stride=1, name, cargo=None):
    bs, t, _ = src.shape
    n = t // stride
    bq, bk = _att_blocks(n)
    nq, nk, rq = n // bq, n // bk, bq // bk
    full = window >= n
    wblk = -(-window // bk)
    biased = kbias is not None
    gdt = BF16 if stride == 1 else F32

    def body(*refs):
        it = iter(refs)
        q_ref, k_ref, v_ref, gq_ref, gk_ref, o_ref, do_ref, lse_ref = (next(it) for _ in range(8))
        rope_refs = (next(it), next(it)) if rope is not None else None
        kb_ref = next(it) if biased else None
        dq_ref, dk_ref, dv_ref, dg_ref = (next(it) for _ in range(4))
        dkb_ref = next(it) if biased else None
        qn_s, kh_s, vb_s, kt_s, dqn_s, dkh_s, dv_s, dq_s, rs_s = (next(it) for _ in range(9))
        m_a = _mask_a()
        masks = (m_a, jnp.logical_not(m_a))
        lane = lax.broadcasted_iota(jnp.int32, (1, LANES), 1)
        tok = _stream_rows(stride)

        @pl.when((pl.program_id(0) == 0) & (pl.program_id(1) == 0) & (pl.program_id(2) == 0))
        def _():
            dg_ref[...] = jnp.zeros_like(dg_ref)

        def prep(c):
            rows = _rows(c, bk)
            trows = tok(c, bk)
            qn_s[rows, :] = (_normed(q_ref, gq_ref, rope_refs, trows, m_a) * SCALE).astype(BF16)
            kn = _normed(k_ref, gk_ref, rope_refs, trows, m_a)
            kt_s[c] = jnp.transpose(kn).astype(BF16)
            vb_s[rows, :] = v_ref[trows, :].astype(BF16)
            for h in (0, 1):
                kh_s[h, rows, :] = _k_for_head(kn, kb_ref[h, c] if biased else None, h, masks[h], lane, bk)

        _loop(0, nk, prep)
        dkh_s[...] = jnp.zeros_like(dkh_s)
        dv_s[...] = jnp.zeros_like(dv_s)
        qk = _q_minus_k(bk, bq)

        def qblock(qi):
            rows = _rows(qi, bq)
            qb = qn_s[rows, :]
            trows = tok(qi, bq)
            dob = do_ref[trows, :]
            delta = _head_rows(dob * o_ref[trows, :])
            qh = [_q_for_head(qb, h, masks[h], lane, biased) for h in (0, 1)]
            doms = [jnp.where(masks[h], dob, 0.0).astype(BF16) for h in (0, 1)]
            lses = [lse_ref[h, qi] for h in (0, 1)]
            dq_s[...] = jnp.zeros_like(dq_s)
            if biased:
                for h in (0, 1):
                    rs_s[h, qi] = jnp.zeros((1, bq), F32)

            def step(kj, masked):
                cols = _rows(kj, bk)
                vb = vb_s[cols, :]
                kt = kt_s[kj]
                sts = [lax.dot_general(kh_s[h, cols, :], qh[h], NT, preferred_element_type=F32) for h in (0, 1)]
                dpts = [lax.dot_general(vb, doms[h], NT, preferred_element_type=F32) for h in (0, 1)]
                if masked:
                    d = qk + (qi * bq - kj * bk)
                    ok = (d >= 0) & (d <= window)
                    sts = [jnp.where(ok, st, NEG) for st in sts]
                new = []
                for h in (0, 1):
                    pt = jnp.exp(sts[h] - lses[h])
                    dst = pt * (dpts[h] - delta[h:h + 1, :])
                    dsb = dst.astype(BF16)
                    tk = jnp.dot(dsb, qh[h], preferred_element_type=F32)
                    if biased:
                        tk = tk + jnp.dot((dst - dsb.astype(F32)).astype(BF16), qh[h], preferred_element_type=F32)
                    tv = jnp.dot(pt.astype(BF16), doms[h], preferred_element_type=F32)
                    tq = jnp.dot(kt, dsb, preferred_element_type=F32)
                    new.append((tk, tv, tq, jnp.sum(dst, axis=0, keepdims=True) if biased else None))
                for h in (0, 1):
                    dkh_s[h, cols, :] += new[h][0]
                    dq_s[h] += new[h][2]
                    if biased:
                        rs_s[h, qi] += new[h][3]
                dv_s[cols, :] += new[0][1] + new[1][1]

            if full:
                _loop(0, qi * rq, lambda kj: step(kj, False))
                _loop(qi * rq, (qi + 1) * rq, lambda kj: step(kj, True))
            else:
                _loop(jnp.maximum(qi * rq - wblk, 0), (qi + 1) * rq, lambda kj: step(kj, True))
            dqn_s[rows, :] = jnp.where(m_a, jnp.transpose(dq_s[0]), jnp.transpose(dq_s[1])) * SCALE

        _loop(0, nq, qblock)

        def finish(c, carry):
            rows = _rows(c, bq)
            out = []
            dk_pair = [dkh_s[0, rows, :], dkh_s[1, rows, :]]
            if biased:
                for h in (0, 1):
                    b = _bias_lane(h)
                    dkb_row = jnp.transpose(dk_pair[h])[b:b + 1, :] - rs_s[h, c]
                    for j in range(rq):
                        dkb_ref[h, c * rq + j] = dkb_row[:, j * bk:(j + 1) * bk]
            trows = tok(c, bq)
            dv_ref[trows, :] = dv_s[rows, :].astype(gdt)
            grads = (dqn_s[rows, :], jnp.where(m_a, dk_pair[0], dk_pair[1]))
            for src_ref, g_ref, dxn, dst in ((q_ref, gq_ref, grads[0], dq_ref), (k_ref, gk_ref, grads[1], dk_ref)):
                xv = src_ref[trows, :]
                inv = _head_inv(xv, m_a)
                y = xv * inv
                if rope_refs is not None:
                    dxn = dxn * rope_refs[0][trows, :] + _swap32(dxn * rope_refs[1][trows, :])
                dy = dxn * g_ref[...]
                dst[trows, :] = (inv * (dy - y * (_half_sum(dy * y, m_a) * (1.0 / HEAD_DIM)))).astype(gdt)
                out.append(jnp.sum(dxn * y, axis=0, keepdims=True))
            return carry[0] + out[0], carry[1] + out[1]

        zero = jnp.zeros((1, LANES), F32)
        dgq, dgk = lax.fori_loop(0, nq, finish, (zero, zero))
        dg_ref[0:1, :] += dgq
        dg_ref[1:2, :] += dgk

    col, vec, seq, stat = _attn_specs(t, npairs, stride, nq, bq)
    ospec = col(0)
    ins = [src, src, src, gq, gk, o, do, lse]
    specs = [col(offs[0]), col(offs[1]), col(offs[2]), vec, vec, ospec, ospec, stat]
    if rope is not None:
        ins += list(rope)
        specs += [seq, seq]
    sds = jax.ShapeDtypeStruct((bs, t, LANES * npairs), gdt)
    out_shape = [sds, sds, sds, jax.ShapeDtypeStruct((8, LANES), F32)]
    out_specs = [ospec, ospec, ospec, pl.BlockSpec((8, LANES), lambda z, p, c: (0, 0))]
    if biased:
        kbspec = pl.BlockSpec((None, 2, nk, 1, bk), lambda z, p, c: (z, p, 0, 0, 0))
        ins.append(kbias)
        specs.append(kbspec)
        out_shape.append(jax.ShapeDtypeStruct(kbias.shape, F32))
        out_specs.append(kbspec)
    scratch = [pltpu.VMEM((n, LANES), BF16), pltpu.VMEM((2, n, LANES), BF16), pltpu.VMEM((n, LANES), BF16)]
    scratch += [pltpu.VMEM((nk, LANES, bk), BF16), pltpu.VMEM((n, LANES), F32), pltpu.VMEM((2, n, LANES), F32)]
    scratch += [pltpu.VMEM((n, LANES), F32), pltpu.VMEM((2, LANES, bq), F32), pltpu.VMEM((2, nq, 1, bq), F32)]
    return _call(
        body, cargo=cargo, name=name, grid=(bs, npairs, stride), in_specs=specs, out_specs=out_specs, out_shape=out_shape,
        scratch_shapes=scratch, compiler_params=_params(("arbitrary", "arbitrary", "arbitrary")),
    )(*ins)


SB_LOG_PARTS = 2
SB_GRAD_PARTS = 1


def _log_sig_pair(z, with_sigmoid=False):
    e = jnp.exp(-jnp.abs(z))
    den = 1.0 + e
    lsn = -(jnp.maximum(z, 0.0) + jnp.log(den))
    if with_sigmoid:
        return lsn, z + lsn, jnp.where(z >= 0.0, 1.0, e) / den
    return lsn, z + lsn


def _sb_fwd(src, offs, npairs, *, name, cargo=None):
    zs, n, _ = src.shape
    bq, bk = _att_blocks(n)
    nq, nk, rq = n // bq, n // bk, bq // bk

    def body(q_ref, k_ref, v_ref, o_ref, lt_ref, qs_s, kb_s, vt_s, acc_s, c_s):
        m_a = _mask_a()
        masks = (m_a, jnp.logical_not(m_a))

        def prep(c):
            rows = _rows(c, bk)
            qs_s[rows, :] = (q_ref[rows, :] * SCALE).astype(BF16)
            kb_s[rows, :] = k_ref[rows, :].astype(BF16)
            vt_s[c] = jnp.transpose(v_ref[rows, :]).astype(BF16)

        _loop(0, nk, prep)
        qk = _q_minus_k(bk, bq)
        u_gt = _tri(bk, lambda r, c: c > r)

        def qblock(qi):
            rows = _rows(qi, bq)
            qb = qs_s[rows, :]
            qms = [jnp.where(masks[h], qb, 0) for h in (0, 1)]
            acc_s[...] = jnp.zeros_like(acc_s)
            c_s[...] = jnp.zeros_like(c_s)

            def step(kj, masked):
                kb, vt = kb_s[_rows(kj, bk), :], vt_s[kj]
                zts = [lax.dot_general(kb, qms[h], NT, preferred_element_type=F32) for h in (0, 1)]
                old = [c_s[h] for h in (0, 1)]
                if masked:
                    ok = (qk + (qi * bq - kj * bk)) > 0
                new = []
                for h in (0, 1):
                    lsn, lsp = _log_sig_pair(zts[h])
                    if masked:
                        lsn = jnp.where(ok, lsn, 0.0)
                    at = jnp.exp(lsp + (old[h] + _cumdot_left(u_gt, lsn, SB_LOG_PARTS)))
                    if masked:
                        at = jnp.where(ok, at, 0.0)
                    new.append((jnp.dot(vt, at.astype(BF16), preferred_element_type=F32),
                                old[h] + jnp.sum(lsn, axis=0, keepdims=True)))
                for h in (0, 1):
                    acc_s[h] += new[h][0]
                    c_s[h] = new[h][1]

            _loop(0, rq, lambda t: step((qi + 1) * rq - 1 - t, True))
            _loop(0, qi * rq, lambda t: step(qi * rq - 1 - t, False))
            o_ref[rows, :] = jnp.where(m_a, jnp.transpose(acc_s[0]), jnp.transpose(acc_s[1]))
            for h in (0, 1):
                lt_ref[h, qi] = c_s[h]

        _loop(0, nq, qblock)

    scratch = [pltpu.VMEM((n, LANES), BF16)] * 2 + [pltpu.VMEM((nk, LANES, bk), BF16)]
    scratch += [pltpu.VMEM((2, LANES, bq), F32), pltpu.VMEM((2, 1, bq), F32)]
    return _call(
        body, cargo=cargo, name=name, grid=(zs, npairs),
        in_specs=[_col_spec(n, offs[0]), _col_spec(n, offs[1]), _col_spec(n, offs[2])],
        out_specs=[_col_spec(n, 0), _stat_spec(nq, bq)],
        out_shape=[jax.ShapeDtypeStruct((zs, n, LANES * npairs), F32), jax.ShapeDtypeStruct((zs, 2 * npairs, nq, 1, bq), F32)],
        scratch_shapes=scratch, compiler_params=_params(("parallel", "parallel")),
    )(src, src, src)


def _sb_bwd(src, offs, npairs, do, ltot, *, name, cargo=None):
    zs, n, _ = src.shape
    bq, bk = _att_blocks(n)
    nq, nk, rq = n // bq, n // bk, bq // bk

    def body(q_ref, k_ref, v_ref, do_ref, lt_ref, dq_ref, dk_ref, dv_ref, qs_s, kb_s, vb_s, kt_s, dk_s, dv_s, dq_s, lp_s, ep_s):
        m_a = _mask_a()
        masks = (m_a, jnp.logical_not(m_a))

        def prep(c):
            rows = _rows(c, bk)
            qs_s[rows, :] = (q_ref[rows, :] * SCALE).astype(BF16)
            kv = k_ref[rows, :]
            kb_s[rows, :] = kv.astype(BF16)
            kt_s[c] = jnp.transpose(kv).astype(BF16)
            vb_s[rows, :] = v_ref[rows, :].astype(BF16)

        _loop(0, nk, prep)
        dk_s[...] = jnp.zeros_like(dk_s)
        dv_s[...] = jnp.zeros_like(dv_s)
        qk = _q_minus_k(bk, bq)
        u_le = _tri(bk, lambda r, c: c <= r)
        u_lt = _tri(bk, lambda r, c: c < r)

        def qblock(qi):
            rows = _rows(qi, bq)
            qb = qs_s[rows, :]
            dob = do_ref[rows, :]
            qms = [jnp.where(masks[h], qb, 0) for h in (0, 1)]
            doms = [jnp.where(masks[h], dob, 0.0).astype(BF16) for h in (0, 1)]
            lts = [lt_ref[h, qi] for h in (0, 1)]
            dq_s[...] = jnp.zeros_like(dq_s)
            lp_s[...] = jnp.zeros_like(lp_s)
            ep_s[...] = jnp.zeros_like(ep_s)

            def step(kj, masked):
                cols = _rows(kj, bk)
                kb, vb, kt = kb_s[cols, :], vb_s[cols, :], kt_s[kj]
                zts = [lax.dot_general(kb, qms[h], NT, preferred_element_type=F32) for h in (0, 1)]
                dats = [lax.dot_general(vb, doms[h], NT, preferred_element_type=F32) for h in (0, 1)]
                old = [(lp_s[h], ep_s[h]) for h in (0, 1)]
                if masked:
                    ok = (qk + (qi * bq - kj * bk)) > 0
                new = []
                for h in (0, 1):
                    lp, ep = old[h]
                    lsn, lsp, sig = _log_sig_pair(zts[h], with_sigmoid=True)
                    if masked:
                        lsn = jnp.where(ok, lsn, 0.0)
                    at = jnp.exp(lsp + (lts[h] - (lp + _cumdot_left(u_le, lsn, SB_LOG_PARTS))))
                    if masked:
                        at = jnp.where(ok, at, 0.0)
                    et = dats[h] * at
                    big_e = ep + _cumdot_left(u_lt, et, SB_GRAD_PARTS)
                    dzt = et * (1.0 - sig) - big_e * sig
                    if masked:
                        dzt = jnp.where(ok, dzt, 0.0)
                    dzb = dzt.astype(BF16)
                    new.append((jnp.dot(dzb, qms[h], preferred_element_type=F32),
                                jnp.dot(at.astype(BF16), doms[h], preferred_element_type=F32),
                                jnp.dot(kt, dzb, preferred_element_type=F32),
                                lp + jnp.sum(lsn, axis=0, keepdims=True), ep + jnp.sum(et, axis=0, keepdims=True)))
                for h in (0, 1):
                    dq_s[h] += new[h][2]
                    lp_s[h], ep_s[h] = new[h][3], new[h][4]
                dk_s[cols, :] += new[0][0] + new[1][0]
                dv_s[cols, :] += new[0][1] + new[1][1]

            _loop(0, qi * rq, lambda kj: step(kj, False))
            _loop(qi * rq, (qi + 1) * rq, lambda kj: step(kj, True))
            dq_ref[rows, :] = (jnp.where(m_a, jnp.transpose(dq_s[0]), jnp.transpose(dq_s[1])) * SCALE).astype(BF16)

        _loop(0, nq, qblock)

        def store(c):
            rows = _rows(c, bk)
            dk_ref[rows, :] = dk_s[rows, :].astype(BF16)
            dv_ref[rows, :] = dv_s[rows, :].astype(BF16)

        _loop(0, nk, store)

    ospec = _col_spec(n, 0)
    sds = jax.ShapeDtypeStruct((zs, n, LANES * npairs), BF16)
    scratch = [pltpu.VMEM((n, LANES), BF16)] * 3 + [pltpu.VMEM((nk, LANES, bk), BF16)] + [pltpu.VMEM((n, LANES), F32)] * 2
    scratch += [pltpu.VMEM((2, LANES, bq), F32), pltpu.VMEM((2, 1, bq), F32), pltpu.VMEM((2, 1, bq), F32)]
    return _call(
        body, cargo=cargo, name=name, grid=(zs, npairs),
        in_specs=[_col_spec(n, offs[0]), _col_spec(n, offs[1]), _col_spec(n, offs[2]), ospec, _stat_spec(nq, bq)],
        out_specs=[ospec] * 3, out_shape=[sds] * 3, scratch_shapes=scratch,
        compiler_params=_params(("parallel", "parallel")),
    )(src, src, src, do, ltot)


def _fox_gate_fwd(lg, bias, *, name):
    bs, nh, t = lg.shape
    blk = min(LANES, t)

    def body(lg_ref, b_ref, kb_ref):
        u_le = _tri(blk, lambda r, c: r <= c)
        carry = jnp.zeros((nh, 1), F32)
        for j in range(t // blk):
            sl = slice(j * blk, (j + 1) * blk)
            xv = lg_ref[:, sl] + b_ref[...]
            lf = jnp.minimum(xv, 0.0) - jnp.log(1.0 + jnp.exp(-jnp.abs(xv)))
            kb_ref[:, sl] = -(carry + _cumdot(lf, u_le, 3))
            carry = carry + jnp.sum(lf, axis=1, keepdims=True)

    spec = pl.BlockSpec((None, nh, t), lambda i: (i, 0, 0))
    return _pcall(
        body, name=name, grid=(bs,), in_specs=[spec, pl.BlockSpec((nh, 1), lambda i: (0, 0))], out_specs=spec,
        out_shape=jax.ShapeDtypeStruct((bs, nh, t), F32), compiler_params=_params(("parallel",)),
    )(lg, bias)


def _fox_gate_bwd(dkb, lg, bias, *, name):
    bs, nh, t = lg.shape
    blk = min(LANES, t)

    def body(dkb_ref, lg_ref, b_ref, dlg_ref, db_ref):
        @pl.when(pl.program_id(0) == 0)
        def _():
            db_ref[...] = jnp.zeros_like(db_ref)

        u_ge = _tri(blk, lambda r, c: r >= c)
        carry = jnp.zeros((nh, 1), F32)
        tot = jnp.zeros((nh, 1), F32)
        for j in reversed(range(t // blk)):
            sl = slice(j * blk, (j + 1) * blk)
            df = -dkb_ref[:, sl]
            dlf = carry + _cumdot(df, u_ge, 3)
            carry = carry + jnp.sum(df, axis=1, keepdims=True)
            xv = lg_ref[:, sl] + b_ref[...]
            dlg = dlf * jax.nn.sigmoid(-xv)
            dlg_ref[:, sl] = dlg
            tot = tot + jnp.sum(dlg, axis=1, keepdims=True)
        db_ref[...] += jnp.broadcast_to(tot, db_ref.shape)

    spec = pl.BlockSpec((None, nh, t), lambda i: (i, 0, 0))
    return _pcall(
        body, name=name, grid=(bs,), in_specs=[spec, spec, pl.BlockSpec((nh, 1), lambda i: (0, 0))],
        out_specs=[spec, pl.BlockSpec((nh, LANES), lambda i: (0, 0))],
        out_shape=[jax.ShapeDtypeStruct((bs, nh, t), F32), jax.ShapeDtypeStruct((nh, LANES), F32)],
        compiler_params=_params(("arbitrary",)),
    )(dkb, lg, bias)


def _place():
    return lax.axis_index("x"), lax.axis_index("y"), lax.axis_index("c")


def _flip(v, f):
    return 1 - v if f else v


FLIPS = [(fx, fy, fc) for fx in (0, 1) for fy in (0, 1) for fc in (0, 1)][1:]


def _comm_sems(nw):
    return [pltpu.SemaphoreType.DMA((7, nw)), pltpu.SemaphoreType.DMA((7, nw)), pltpu.SemaphoreType.DMA((nw,))]


def _gather_cargo(shards, on_done):
    nw = len(shards)

    def parts(x_refs, out_refs, sems):
        send_sems, recv_sems, local_sems = sems
        x, y, cc = _place()
        me, sibling = (x, y, cc), (x, y, 1 - cc)
        chips = [(1 - x, y), (x, 1 - y), (1 - x, 1 - y)]

        def slot(i, px, py, pc):
            return out_refs[i].at[4 * px + 2 * py + pc]

        def copy(i, k, block, to, src=None):
            return pltpu.make_async_remote_copy(
                src_ref=slot(i, *block) if src is None else src, dst_ref=slot(i, *block),
                send_sem=send_sems.at[k, i], recv_sem=recv_sems.at[k, i], device_id=to, device_id_type=MESH)

        mine = [pltpu.make_async_copy(x_refs[i], slot(i, *me), local_sems.at[i]) for i in range(nw)]
        first = []
        for i in range(nw):
            first.append(copy(i, 0, me, sibling, src=x_refs[i]))
            first += [copy(i, 1 + j, me, (*chip, cc), src=x_refs[i]) for j, chip in enumerate(chips)]
        return me, sibling, chips, cc, copy, mine, first

    def start(x_refs, out_refs, sems):
        *_, mine, first = parts(x_refs, out_refs, sems)
        for cp in mine + first:
            cp.start()

    def finish(x_refs, out_refs, sems):
        me, sibling, chips, cc, copy, mine, first = parts(x_refs, out_refs, sems)
        passed = []
        for i in range(nw):
            for j, chip in enumerate(chips):
                copy(i, 1 + j, (*chip, cc), me).wait_recv()
                passed.append(copy(i, 4 + j, (*chip, cc), sibling))
                passed[-1].start()
        for i in range(nw):
            copy(i, 0, sibling, me).wait_recv()
            for j, chip in enumerate(chips):
                copy(i, 4 + j, (*chip, 1 - cc), me).wait_recv()
        for cp in first + passed:
            cp.wait_send()
        for cp in mine:
            cp.wait()

    out_shape = [jax.ShapeDtypeStruct((N_DEV, *s.shape), s.dtype) for s in shards]
    return _Cargo(shards, out_shape, _comm_sems(nw), start, finish, on_done)


def _scatter_cargo(slots, prev, layer, depth, on_done):
    nw = len(slots)

    def parts(refs, recv_refs, sems):
        g_refs = refs[:nw]
        send_sems, recv_sems, local_sems = sems
        x, y, cc = _place()
        my = 4 * x + 2 * y + cc
        mine, copies = [], []
        for i in range(nw):
            mine.append(pltpu.make_async_copy(g_refs[i].at[my], recv_refs[i].at[my, layer], local_sems.at[i]))
            for k, (fx, fy, fc) in enumerate(FLIPS):
                px, py, pc = _flip(x, fx), _flip(y, fy), _flip(cc, fc)
                copies.append(pltpu.make_async_remote_copy(
                    src_ref=g_refs[i].at[4 * px + 2 * py + pc], dst_ref=recv_refs[i].at[my, layer],
                    send_sem=send_sems.at[k, i], recv_sem=recv_sems.at[k, i], device_id=(px, py, pc), device_id_type=MESH))
        return mine, copies

    def start(refs, recv_refs, sems):
        mine, copies = parts(refs, recv_refs, sems)
        for cp in mine + copies:
            cp.start()

    def finish(refs, recv_refs, sems):
        mine, copies = parts(refs, recv_refs, sems)
        for cp in copies:
            cp.wait_recv()
        for cp in copies:
            cp.wait_send()
        for cp in mine:
            cp.wait()

    ins, aliases = list(slots), {}
    for i, p in enumerate(prev):
        if p is not None:
            aliases[len(ins)] = i
            ins.append(p)
    out_shape = [jax.ShapeDtypeStruct((N_DEV, depth, *s.shape[1:]), s.dtype) for s in slots]
    return _Cargo(ins, out_shape, _comm_sems(nw), start, finish, on_done, aliases)


def _exchange(cargo, *, name):
    def body(*refs):
        c_in = len(cargo.ins)
        c_out = len(cargo.out_shape)
        cargo.start(refs[:c_in], refs[c_in:c_in + c_out], refs[c_in + c_out:])
        cargo.finish(refs[:c_in], refs[c_in:c_in + c_out], refs[c_in + c_out:])

    hbm = pl.BlockSpec(memory_space=pl.ANY)
    res = _pcall(
        body, name=name, in_specs=[hbm] * len(cargo.ins), out_specs=[hbm] * len(cargo.out_shape), out_shape=cargo.out_shape,
        scratch_shapes=cargo.sems, input_output_aliases=dict(cargo.aliases),
    )(*cargo.ins)
    cargo.on_done(list(res))


def _allreduce_small(blob, *, name):
    r, c = blob.shape

    def body(x_ref, out_ref, buf, send_sems, recv_sems):
        x, y, cc = _place()
        my = 4 * x + 2 * y + cc
        copies = []
        for k, (fx, fy, fc) in enumerate(FLIPS):
            peer = (_flip(x, fx), _flip(y, fy), _flip(cc, fc))
            copies.append(pltpu.make_async_remote_copy(
                src_ref=x_ref, dst_ref=buf.at[my], send_sem=send_sems.at[k], recv_sem=recv_sems.at[k],
                device_id=peer, device_id_type=MESH))
        for cp in copies:
            cp.start()
        buf[my] = x_ref[...]
        for cp in copies:
            cp.wait_recv()
        for cp in copies:
            cp.wait_send()
        acc = buf[0]
        for i in range(1, N_DEV):
            acc = acc + buf[i]
        out_ref[...] = acc

    vmem = pl.BlockSpec(memory_space=pltpu.VMEM)
    return _pcall(
        body, name=name, in_specs=[vmem], out_specs=vmem, out_shape=jax.ShapeDtypeStruct((r, c), F32),
        scratch_shapes=[pltpu.VMEM((N_DEV, r, c), F32), pltpu.SemaphoreType.DMA((7,)), pltpu.SemaphoreType.DMA((7,))],
    )(blob)


BIG = ("w_in", "w_mlp_in", "w_mlp_out", "w_up_fox", "w_up_sb", "w_up_dil", "w_out")
ROW_SHARDED = ("w_out", "w_mlp_out")
SMALL = ("attn_norm", "b_forget", "q_norm_fox", "k_norm_fox", "q_norm_dil", "k_norm_dil", "mlp_norm")
BLOB_ROWS = 512


def _pack(parts, dtype):
    flat = jnp.concatenate([p.reshape(-1).astype(dtype) for p in parts])
    size = -(-flat.shape[0] // (BLOB_ROWS * LANES)) * (BLOB_ROWS * LANES)
    return jnp.pad(flat, (0, size - flat.shape[0])).reshape(-1, LANES)


def _unpack(blob, shapes):
    flat = blob.reshape(-1)
    out, off = [], 0
    for shp in shapes:
        size = 1
        for s in shp:
            size *= s
        out.append(flat[off:off + size].reshape(shp))
        off += size
    return out


def _join_shards(name, sh):
    if name in ROW_SHARDED:
        return sh.reshape(-1, sh.shape[2])
    return jnp.transpose(sh, (1, 0, 2)).reshape(sh.shape[1], -1)


def _split_shards(name, full):
    a, b = full.shape
    if name in ROW_SHARDED:
        return full.reshape(N_DEV, a // N_DEV, b)
    return jnp.transpose(full.reshape(a, N_DEV, b // N_DEV), (1, 0, 2))


def _permute_in(w, dp):
    o1 = 3 * W_FOX
    o2 = o1 + N_HEADS_FOX
    pad = [(0, 0)] * (w.ndim - 1) + [(0, dp - w.shape[-1])]
    return jnp.pad(jnp.concatenate([w[..., :o1], w[..., o2:], w[..., o1:o2]], axis=-1), pad)


def _unpermute_in(wp, d_in):
    o1 = 3 * W_FOX
    fg = d_in - N_HEADS_FOX
    return jnp.concatenate([wp[..., :o1], wp[..., fg:d_in], wp[..., o1:fg]], axis=-1)


def _stat_to_tokens(st, r, b):
    hh = st.shape[1]
    n = st.shape[2] * st.shape[4]
    return jnp.transpose(st.reshape(b, r, hh, n), (0, 2, 3, 1)).reshape(b, hh, n * r)


def _stat_to_streams(tok, r, blk):
    b, hh, t = tok.shape
    n = t // r
    return jnp.transpose(tok.reshape(b, hh, n, r), (0, 3, 1, 2)).reshape(b * r, hh, n // blk, 1, blk)


def _rope_tables(positions):
    half = HEAD_DIM // 2
    inv = 1.0 / (ROPE_THETA ** (jnp.arange(half, dtype=F32) / half))
    ang = positions.astype(F32)[..., None] * inv
    cos, sin = jnp.cos(ang), jnp.sin(ang)
    return jnp.tile(cos, (1, 1, 4)), jnp.tile(jnp.concatenate([-sin, sin], axis=-1), (1, 1, 2))


def _gain2(g):
    return jnp.tile(g.reshape(1, HEAD_DIM), (1, 2))


def _dil_offs(g):
    c0 = (P_DIL + g * W_DIL) // LANES
    return c0, c0 + W_DILQ // LANES, c0 + 2 * W_DILQ // LANES


def _layer_fwd(l, x, w, small, ropes, bl, t, cargo):
    n, d = x.shape
    s = {}
    s["x"] = x
    s["h"] = _rmsnorm_fwd(x, small["attn_norm"][l].reshape(1, d), name=f"norm_attn_fwd{l}")
    proj = _mm(s["h"], w["w_in"][l], tn=PROJ_TILE, name=f"mm_proj{l}")
    s["proj"] = proj
    dp = proj.shape[1]
    proj3 = proj.reshape(bl, t, dp)
    p_fg = P_GATE + 3 * d

    lg = jnp.transpose(proj3[:, :, p_fg:p_fg + N_HEADS_FOX], (0, 2, 1))
    s["lg"] = lg
    kb = _fox_gate_fwd(lg, small["b_forget"][l].reshape(N_HEADS_FOX, 1), name=f"fox_gate_fwd{l}")
    blk = _att_blk(t)
    kb5 = kb.reshape(bl, N_HEADS_FOX, t // blk, 1, blk)
    s["kb5"] = kb5
    gqf, gkf = _gain2(small["q_norm_fox"][l]), _gain2(small["k_norm_fox"][l])
    fo = P_FOX // LANES
    fox_offs = (fo, fo + W_FOX // LANES, fo + 2 * W_FOX // LANES)
    out_a, lse_a = _attn_fwd(proj3, fox_offs, N_HEADS_FOX // 2, gqf, gkf, kbias=kb5, window=t, name=f"fox_fwd{l}",
                             cargo=cargo.get("fox_fwd"))
    s["out_a"], s["lse_a"] = out_a, lse_a

    so = P_SB // LANES
    sb_offs = (so, so + W_SB // LANES, so + 2 * W_SB // LANES)
    out_b, lt_b = _sb_fwd(proj3, sb_offs, N_HEADS_SB // 2, name=f"sb_fwd{l}", cargo=cargo.get("sb_fwd"))
    s["out_b"], s["lt_b"] = out_b, lt_b

    gqd, gkd = _gain2(small["q_norm_dil"][l]), _gain2(small["k_norm_dil"][l])
    os_, lses = [], []
    for g, (window, r) in enumerate(DIL_PATTERNS):
        o_g, lse_g = _attn_fwd(proj3, _dil_offs(g), N_HEADS_DIL // 2, gqd, gkd, rope=ropes, window=window // r, stride=r,
                               name=f"dil_fwd{l}_{g}")
        os_.append(o_g.reshape(n, W_DIL))
        lses.append(_stat_to_tokens(lse_g, r, bl).reshape(bl * N_HEADS_DIL, t))
    lse_c, *ws = _dil_weights(lses, name=f"dil_weights{l}")
    ws = [jnp.repeat(jnp.transpose(wg.reshape(bl, N_HEADS_DIL, t), (0, 2, 1)).reshape(n, N_HEADS_DIL), HEAD_DIM, axis=1) for wg in ws]
    out_c = _dil_mix(os_, ws, name=f"dil_mix{l}")
    s["out_c"], s["lse_c"] = out_c, lse_c.reshape(bl, N_HEADS_DIL, t)

    ys = [_mm(out_a.reshape(n, W_FOX), w["w_up_fox"][l], name=f"mm_up_fox{l}"),
          _mm(out_b.reshape(n, W_SB), w["w_up_sb"][l], name=f"mm_up_sb{l}"),
          _mm(out_c, w["w_up_dil"][l], name=f"mm_up_dil{l}")]
    s["ys"] = ys
    s["merged"] = _gate_merge_fwd(proj, ys, name=f"gate_merge_fwd{l}")
    x1 = _mm(s["merged"], w["w_out"][l], add=x, name=f"mm_out{l}")
    s["x1"] = x1

    s["h2"] = _rmsnorm_fwd(x1, small["mlp_norm"][l].reshape(1, d), name=f"norm_mlp_fwd{l}")
    s["u"], s["a"] = _mm(s["h2"], w["w_mlp_in"][l], relu2=True, out_dtype=BF16, name=f"mm_mlp_in{l}")
    x2 = _mm(s["a"], w["w_mlp_out"][l], add=x1, name=f"mm_mlp_out{l}")
    return x2, s


def _layer_bwd(l, dx2, s, w, small, ropes, bl, t, hooks):
    n, d = dx2.shape
    gw, gs = {}, {}

    def cargo(call):
        return hooks[call](gw) if call in hooks else None
    du = _mm(dx2, w["w_mlp_out"][l], tb=True, relu_grad_of=s["u"], out_dtype=BF16, name=f"mm_du{l}")
    gw["w_mlp_out"] = _mm(s["a"], dx2, ta=True, name=f"mm_dw_mlp_out{l}")
    gw["w_mlp_in"] = _mm(s["h2"], du, ta=True, name=f"mm_dw_mlp_in{l}")
    dh2 = _mm(du, w["w_mlp_in"][l], tb=True, name=f"mm_dh2{l}")
    dx1, gs["mlp_norm"] = _rmsnorm_bwd(s["x1"], small["mlp_norm"][l].reshape(1, d), dh2, dx2, name=f"norm_mlp_bwd{l}")

    dmerged = _mm(dx1, w["w_out"][l], tb=True, name=f"mm_dmerged{l}")
    gw["w_out"] = _mm(s["merged"], dx1, ta=True, name=f"mm_dw_out{l}")
    dya, dyb, dyc, dgl0, dgl1, dgl2 = _gate_merge_bwd(s["proj"], s["ys"], dmerged, name=f"gate_merge_bwd{l}")
    out_a2, out_b2 = s["out_a"].reshape(n, W_FOX), s["out_b"].reshape(n, W_SB)
    gw["w_up_fox"] = _mm(out_a2, dya, ta=True, name=f"mm_dw_up_fox{l}")
    gw["w_up_sb"] = _mm(out_b2, dyb, ta=True, name=f"mm_dw_up_sb{l}")
    gw["w_up_dil"] = _mm(s["out_c"], dyc, ta=True, name=f"mm_dw_up_dil{l}")
    dout_a = _mm(dya, w["w_up_fox"][l], tb=True, name=f"mm_dout_a{l}").reshape(bl, t, W_FOX)
    dout_b = _mm(dyb, w["w_up_sb"][l], tb=True, name=f"mm_dout_b{l}").reshape(bl, t, W_SB)
    dout_c = _mm(dyc, w["w_up_dil"][l], tb=True, name=f"mm_dout_c{l}").reshape(bl, t, W_DIL)

    proj3 = s["proj"].reshape(bl, t, -1)
    gqf, gkf = _gain2(small["q_norm_fox"][l]), _gain2(small["k_norm_fox"][l])
    fo = P_FOX // LANES
    fox_offs = (fo, fo + W_FOX // LANES, fo + 2 * W_FOX // LANES)
    dq_a, dk_a, dv_a, dg_a, dkb5 = _attn_bwd(proj3, fox_offs, N_HEADS_FOX // 2, gqf, gkf, s["out_a"], dout_a, s["lse_a"],
                                             kbias=s["kb5"], window=t, name=f"fox_bwd{l}", cargo=cargo("fox_bwd"))
    gs["fox_gains"] = dg_a
    dlg, gs["b_forget"] = _fox_gate_bwd(dkb5.reshape(bl, N_HEADS_FOX, t), s["lg"], small["b_forget"][l].reshape(N_HEADS_FOX, 1),
                                        name=f"fox_gate_bwd{l}")
    so = P_SB // LANES
    sb_offs = (so, so + W_SB // LANES, so + 2 * W_SB // LANES)
    dq_b, dk_b, dv_b = _sb_bwd(proj3, sb_offs, N_HEADS_SB // 2, dout_b, s["lt_b"], name=f"sb_bwd{l}", cargo=cargo("sb_bwd"))
    gqd, gkd = _gain2(small["q_norm_dil"][l]), _gain2(small["k_norm_dil"][l])
    out_c3 = s["out_c"].reshape(bl, t, W_DIL)
    dqs, dks, dvs, dgd = [], [], [], None
    for g, (window, r) in enumerate(DIL_PATTERNS):
        lse_g = _stat_to_streams(s["lse_c"], r, _att_blocks(t // r)[0])
        dq_g, dk_g, dv_g, dg_g = _attn_bwd(proj3, _dil_offs(g), N_HEADS_DIL // 2, gqd, gkd, out_c3, dout_c, lse_g, rope=ropes,
                                           window=window // r, stride=r, name=f"dil_bwd{l}_{g}")
        dqs.append(dq_g)
        dks.append(dk_g)
        dvs.append(dv_g)
        dgd = dg_g if dgd is None else jnp.concatenate([dgd, dg_g], axis=0)
    gs["dil_gains"] = dgd

    dp = s["proj"].shape[1]
    parts = [dq_a, dk_a, dv_a, dq_b, dk_b, dv_b] + dqs + dks + dvs
    parts = [p.reshape(n, -1).astype(BF16) for p in parts] + [dgl0, dgl1, dgl2, jnp.transpose(dlg, (0, 2, 1)).reshape(n, -1).astype(BF16)]
    used = P_GATE + 3 * d + N_HEADS_FOX
    dproj = jnp.concatenate(parts + [jnp.zeros((n, dp - used), BF16)], axis=1)
    gw["w_in"] = _mm(s["h"], dproj, ta=True, tn=PROJ_TILE, name=f"mm_dw_in{l}")
    dh = _mm(dproj, w["w_in"][l], tb=True, tn=1024, tk=PROJ_TILE, name=f"mm_dh{l}", cargo=cargo("mm_dh"))
    dx, gs["attn_norm"] = _rmsnorm_bwd(s["x"], small["attn_norm"][l].reshape(1, d), dh, dx1, name=f"norm_attn_bwd{l}")
    return dx, gw, gs


def kernel(x, positions, attn_norm, w_in, b_forget, q_norm_fox, k_norm_fox, q_norm_dil, k_norm_dil, w_up_fox, w_up_sb, w_up_dil, w_out, mlp_norm, w_mlp_in, w_mlp_out, loss_target, m_attn_norm, m_w_in, m_b_forget, m_q_norm_fox, m_k_norm_fox, m_q_norm_dil, m_k_norm_dil, m_w_up_fox, m_w_up_sb, m_w_up_dil, m_w_out, m_mlp_norm, m_w_mlp_in, m_w_mlp_out, v_attn_norm, v_w_in, v_b_forget, v_q_norm_fox, v_k_norm_fox, v_q_norm_dil, v_k_norm_dil, v_w_up_fox, v_w_up_sb, v_w_up_dil, v_w_out, v_mlp_norm, v_w_mlp_in, v_w_mlp_out):
    bl, t, d = x.shape
    n = bl * t
    depth = attn_norm.shape[0]
    wl = dict(w_in=w_in, w_up_fox=w_up_fox, w_up_sb=w_up_sb, w_up_dil=w_up_dil, w_out=w_out, w_mlp_in=w_mlp_in, w_mlp_out=w_mlp_out)
    ml = dict(w_in=m_w_in, w_up_fox=m_w_up_fox, w_up_sb=m_w_up_sb, w_up_dil=m_w_up_dil, w_out=m_w_out, w_mlp_in=m_w_mlp_in, w_mlp_out=m_w_mlp_out)
    vl = dict(w_in=v_w_in, w_up_fox=v_w_up_fox, w_up_sb=v_w_up_sb, w_up_dil=v_w_up_dil, w_out=v_w_out, w_mlp_in=v_w_mlp_in, w_mlp_out=v_w_mlp_out)
    small = dict(attn_norm=attn_norm, b_forget=b_forget, q_norm_fox=q_norm_fox, k_norm_fox=k_norm_fox, q_norm_dil=q_norm_dil,
                 k_norm_dil=k_norm_dil, mlp_norm=mlp_norm)
    m_small = dict(attn_norm=m_attn_norm, b_forget=m_b_forget, q_norm_fox=m_q_norm_fox, k_norm_fox=m_k_norm_fox,
                   q_norm_dil=m_q_norm_dil, k_norm_dil=m_k_norm_dil, mlp_norm=m_mlp_norm)
    v_small = dict(attn_norm=v_attn_norm, b_forget=v_b_forget, q_norm_fox=v_q_norm_fox, k_norm_fox=v_k_norm_fox,
                   q_norm_dil=v_q_norm_dil, k_norm_dil=v_k_norm_dil, mlp_norm=v_mlp_norm)

    d_in = w_in.shape[-1] * N_DEV
    dp = -(-d_in // 512) * 512
    rest = [k for k in BIG if k != "w_in"]
    w = {k: [None] * depth for k in BIG}

    def gather(items):
        def done(res):
            for (k, l), sh in zip(items, res):
                whole = _join_shards(k, sh)
                w[k][l] = _permute_in(whole, dp) if k == "w_in" else whole

        return _gather_cargo([wl[k][l].astype(BF16) for k, l in items], done)

    _exchange(gather([("w_in", 0)]), name="gather_first")
    cos, sin = _rope_tables(positions)
    ropes = (cos, sin)

    h = x.reshape(n, d)
    saved = []
    for l in range(depth):
        cargo = {"fox_fwd": gather([(k, l) for k in rest])}
        if l + 1 < depth:
            cargo["sb_fwd"] = gather([("w_in", l + 1)])
        h, s = _layer_fwd(l, h, w, small, ropes, bl, t, cargo)
        saved.append(s)
    dy, loss_part = _loss_head(h, loss_target.reshape(n, d), name="loss_head")

    recv = {}

    def scatter(names, l, grads):
        def done(res):
            recv.update(zip(names, res))

        slots = [_split_shards(k, _unpermute_in(grads[k], d_in) if k == "w_in" else grads[k]).astype(BF16) for k in names]
        return _scatter_cargo(slots, [recv.get(k) for k in names], l, depth, done)

    gss = [None] * depth
    above = None
    for l in reversed(range(depth)):
        hooks = {"fox_bwd": lambda gw, l=l: scatter(rest, l, gw)}
        if above is not None:
            hooks["sb_bwd"] = lambda gw, l=l, g=above: scatter(["w_in"], l + 1, g)
        if l == 0:
            hooks["mm_dh"] = lambda gw: scatter(["w_in"], 0, gw)
        dy, above, gss[l] = _layer_bwd(l, dy, saved[l], w, small, ropes, bl, t, hooks)
    grad_x = dy.reshape(bl, t, d)

    g_big, d_big, m_big, v_big = {}, {}, {}, {}
    for k in BIG:
        g_big[k], d_big[k], m_big[k], v_big[k] = _adamw(recv[k], wl[k], ml[k], vl[k], name=f"adamw_{k}")

    rows = [loss_part]
    for l in range(depth):
        gs = gss[l]
        rows += [gs["attn_norm"].reshape(-1, LANES), gs["mlp_norm"].reshape(-1, LANES), gs["fox_gains"], gs["dil_gains"], gs["b_forget"]]
    row_counts = [r.shape[0] for r in rows]
    part = jnp.concatenate(rows, axis=0)
    pad_rows = -(-part.shape[0] // 8) * 8 - part.shape[0]
    summed = _allreduce_small(jnp.pad(part, ((0, pad_rows), (0, 0))), name="allreduce_small")
    pieces, off = [], 0
    for c in row_counts:
        pieces.append(summed[off:off + c])
        off += c
    loss = pieces[0][0, 0]

    def fold(row):
        return row[:HEAD_DIM] + row[HEAD_DIM:]

    g_small = {k: [] for k in SMALL}
    for l in range(depth):
        an, mn, fg, dg, bf = pieces[1 + 5 * l:6 + 5 * l]
        g_small["attn_norm"].append(an.reshape(d))
        g_small["mlp_norm"].append(mn.reshape(d))
        g_small["q_norm_fox"].append(fold(fg[0]))
        g_small["k_norm_fox"].append(fold(fg[1]))
        g_small["q_norm_dil"].append(fold(dg[0]) + fold(dg[8]) + fold(dg[16]))
        g_small["k_norm_dil"].append(fold(dg[1]) + fold(dg[9]) + fold(dg[17]))
        g_small["b_forget"].append(bf[:, 0])
    g_small = {k: jnp.stack(vs) for k, vs in g_small.items()}
    small_shapes = [small[k].shape for k in SMALL]
    outs = _adamw(_pack([g_small[k] for k in SMALL], F32)[None, None], _pack([small[k] for k in SMALL], F32)[None],
                  _pack([m_small[k] for k in SMALL], F32)[None], _pack([v_small[k] for k in SMALL], F32)[None], name="adamw_small")
    g_sm, d_sm, m_sm, v_sm = (dict(zip(SMALL, _unpack(o, small_shapes))) for o in outs)

    order = ("attn_norm", "w_in", "b_forget", "q_norm_fox", "k_norm_fox", "q_norm_dil", "k_norm_dil", "w_up_fox", "w_up_sb",
             "w_up_dil", "w_out", "mlp_norm", "w_mlp_in", "w_mlp_out")
    res = [loss, grad_x]
    for big, sm in ((g_big, g_sm), (d_big, d_sm), (m_big, m_sm), (v_big, v_sm)):
        res += [big[k] if k in big else sm[k] for k in order]
    return tuple(res)
```

```python
import jax
import jax.numpy as jnp
from jax import lax
from jax.experimental import pallas as pl
from jax.experimental.pallas import tpu as pltpu

F32 = jnp.float32
BF16 = jnp.bfloat16

HEAD_DIM = 64
LANES = 128
N_HEADS_FOX = 8
N_HEADS_SB = 8
N_HEADS_DIL = 4
DIL_PATTERNS = ((128, 1), (512, 4), (2048, 16))
ROPE_THETA = 10000.0
EPS = 1e-6
SCALE = 0.125
W_FOX = N_HEADS_FOX * HEAD_DIM
W_SB = N_HEADS_SB * HEAD_DIM
W_DIL = N_HEADS_DIL * HEAD_DIM
W_DILQ = len(DIL_PATTERNS) * W_DIL
P_FOX = 0
P_SB = 3 * W_FOX
P_DIL = P_SB + 3 * W_SB
P_GATE = P_DIL + 3 * W_DILQ
N_DEV = 8
ATT_BLK = 256
ATT_BQ = 512
NEG = -1e30
VMEM_LIMIT = 56 * 1024 * 1024
ADAMW_BLOCK_ELEMS = 128 * 1024
PROJ_TILE = 2176

ADAM_LR = 0.001
ADAM_B1 = 0.9
ADAM_B2 = 0.999
ADAM_EPS = 1e-08
ADAM_WD = 0.01
ADAM_STEP = 10

NT = (((1,), (1,)), ((), ()))
MESH = pl.DeviceIdType.MESH


def _pcall(body, **kw):
    return pl.pallas_call(body, **kw)


def _params(sem=None):
    return pltpu.CompilerParams(dimension_semantics=sem, vmem_limit_bytes=VMEM_LIMIT)


class _Cargo:
    def __init__(self, ins, out_shape, sems, start, finish, on_done, aliases=None):
        self.ins, self.out_shape, self.sems = list(ins), list(out_shape), list(sems)
        self.start, self.finish, self.on_done, self.aliases = start, finish, on_done, dict(aliases or {})


def _call(body, *, cargo=None, name, grid=(), in_specs, out_specs, out_shape, scratch_shapes=(), compiler_params=None):
    if cargo is None:
        kw = dict(grid=grid) if grid else {}
        if compiler_params is not None:
            kw["compiler_params"] = compiler_params
        return _pcall(body, name=name, in_specs=in_specs, out_specs=out_specs, out_shape=out_shape,
                      scratch_shapes=list(scratch_shapes), **kw)
    single = not isinstance(out_shape, (list, tuple))
    o_specs, o_shape = ([out_specs], [out_shape]) if single else (list(out_specs), list(out_shape))
    n_in, n_out, n_scr = len(in_specs), len(o_shape), len(scratch_shapes)
    c_in, c_out = len(cargo.ins), len(cargo.out_shape)

    def wrapped(*refs):
        ins, cins = refs[:n_in], refs[n_in:n_in + c_in]
        o0 = n_in + c_in
        outs, couts = refs[o0:o0 + n_out], refs[o0 + n_out:o0 + n_out + c_out]
        s0 = o0 + n_out + c_out
        scr, sems = refs[s0:s0 + n_scr], refs[s0 + n_scr:]
        first = last = None
        for ax, size in enumerate(grid):
            pid = pl.program_id(ax)
            first = (pid == 0) if first is None else first & (pid == 0)
            last = (pid == size - 1) if last is None else last & (pid == size - 1)
        if first is None:
            cargo.start(cins, couts, sems)
            body(*ins, *outs, *scr)
            cargo.finish(cins, couts, sems)
            return

        @pl.when(first)
        def _():
            cargo.start(cins, couts, sems)

        body(*ins, *outs, *scr)

        @pl.when(last)
        def _():
            cargo.finish(cins, couts, sems)

    hbm = pl.BlockSpec(memory_space=pl.ANY)
    kw = dict(grid=grid, compiler_params=_params(("arbitrary",) * len(grid))) if grid else {}
    call = _pcall(
        wrapped, name=name, in_specs=list(in_specs) + [hbm] * c_in, out_specs=o_specs + [hbm] * c_out,
        out_shape=o_shape + cargo.out_shape, scratch_shapes=list(scratch_shapes) + cargo.sems,
        input_output_aliases={n_in + i: n_out + j for i, j in cargo.aliases.items()}, **kw)

    def run(*args):
        res = call(*args, *cargo.ins)
        cargo.on_done(list(res[n_out:]))
        return res[0] if single else list(res[:n_out])

    return run


def _tile(dim, target, mult=LANES):
    t = (min(dim, target) // mult) * mult
    while t >= mult:
        if dim % t == 0:
            return t
        t -= mult
    return dim


def _mm(a, b, *, ta=False, tb=False, add=None, relu2=False, relu_grad_of=None, out_dtype=F32, name, tm=1024, tn=1024,
        tk=1024, cargo=None):
    m, k = (a.shape[1], a.shape[0]) if ta else a.shape
    n = b.shape[0] if tb else b.shape[1]
    tm, tn, tk = _tile(m, tm), _tile(n, tn), _tile(k, tk)
    nk = k // tk
    dn = (((0,) if ta else (1,), (1,) if tb else (0,)), ((), ()))

    extra = add if add is not None else relu_grad_of

    def body(*refs):
        a_ref, b_ref = refs[:2]
        x_ref = refs[2] if extra is not None else None
        outs = refs[2 + (extra is not None):-1]
        acc = refs[-1]
        kk = pl.program_id(2)
        part = lax.dot_general(a_ref[...].astype(BF16), b_ref[...].astype(BF16), dn, preferred_element_type=F32)

        def finish(r):
            if add is not None:
                r = r + x_ref[...]
            if relu_grad_of is not None:
                r = r * (2.0 * jnp.maximum(x_ref[...].astype(F32), 0.0))
            outs[0][...] = r.astype(out_dtype)
            if relu2:
                rr = jnp.maximum(r, 0.0)
                outs[1][...] = (rr * rr).astype(BF16)

        if nk == 1:
            finish(part)
            return

        @pl.when(kk == 0)
        def _():
            acc[...] = part

        @pl.when((kk > 0) & (kk < nk - 1))
        def _():
            acc[...] += part

        @pl.when(kk == nk - 1)
        def _():
            finish(acc[...] + part)

    a_spec = pl.BlockSpec((tk, tm), lambda i, j, q: (q, i)) if ta else pl.BlockSpec((tm, tk), lambda i, j, q: (i, q))
    b_spec = pl.BlockSpec((tn, tk), lambda i, j, q: (j, q)) if tb else pl.BlockSpec((tk, tn), lambda i, j, q: (q, j))
    o_spec = pl.BlockSpec((tm, tn), lambda i, j, q: (i, j))
    ins, specs = [a, b], [a_spec, b_spec]
    if extra is not None:
        ins.append(extra)
        specs.append(o_spec)
    sds = jax.ShapeDtypeStruct((m, n), out_dtype)
    return _call(
        body, cargo=cargo, name=name, grid=(m // tm, n // tn, nk), in_specs=specs,
        out_specs=[o_spec, o_spec] if relu2 else o_spec,
        out_shape=[sds, jax.ShapeDtypeStruct((m, n), BF16)] if relu2 else sds,
        scratch_shapes=[pltpu.VMEM((tm, tn) if nk > 1 else (8, LANES), F32)],
        compiler_params=_params(("parallel", "parallel", "arbitrary")),
    )(*ins)


def _rmsnorm_fwd(x, g, *, name):
    n, d = x.shape
    tm = _tile(n, 256, 8)

    def body(x_ref, g_ref, h_ref):
        xv = x_ref[...]
        inv = lax.rsqrt(jnp.mean(xv * xv, axis=1, keepdims=True) + EPS)
        h_ref[...] = (xv * inv * g_ref[...]).astype(BF16)

    row = pl.BlockSpec((tm, d), lambda i: (i, 0))
    return _pcall(
        body, name=name, grid=(n // tm,), in_specs=[row, pl.BlockSpec((1, d), lambda i: (0, 0))], out_specs=row,
        out_shape=jax.ShapeDtypeStruct((n, d), BF16), compiler_params=_params(("parallel",)),
    )(x, g)


def _rmsnorm_bwd(x, g, dh, dres, *, name):
    n, d = x.shape
    tm = _tile(n, 256, 8)

    def body(x_ref, g_ref, dh_ref, dres_ref, dx_ref, dg_ref):
        @pl.when(pl.program_id(0) == 0)
        def _():
            dg_ref[...] = jnp.zeros_like(dg_ref)

        xv = x_ref[...]
        inv = lax.rsqrt(jnp.mean(xv * xv, axis=1, keepdims=True) + EPS)
        y = xv * inv
        dhv = dh_ref[...]
        dg_ref[...] += jnp.sum(dhv * y, axis=0, keepdims=True)
        dy = dhv * g_ref[...]
        dx_ref[...] = dres_ref[...] + inv * (dy - y * jnp.mean(dy * y, axis=1, keepdims=True))

    row = pl.BlockSpec((tm, d), lambda i: (i, 0))
    vec = pl.BlockSpec((1, d), lambda i: (0, 0))
    return _pcall(
        body, name=name, grid=(n // tm,), in_specs=[row, vec, row, row], out_specs=[row, vec],
        out_shape=[jax.ShapeDtypeStruct((n, d), F32), jax.ShapeDtypeStruct((1, d), F32)],
        compiler_params=_params(("arbitrary",)),
    )(x, g, dh, dres)


def _gate_specs(n, d):
    bw = 256 if d % 256 == 0 else LANES
    tm = _tile(n, 512, 8)
    nb = d // bw
    yspec = pl.BlockSpec((tm, bw), lambda i, j: (i, j))
    gspecs = [pl.BlockSpec((tm, bw), lambda i, j, b=b: (i, P_GATE // bw + b * nb + j)) for b in range(3)]
    return tm, bw, nb, yspec, gspecs


def _gate_merge_fwd(proj, ys, *, name):
    n, d = ys[0].shape
    tm, bw, nb, yspec, gspecs = _gate_specs(n, d)

    def body(g0, g1, g2, y0, y1, y2, o_ref):
        acc = jax.nn.sigmoid(g0[...]) * y0[...]
        acc += jax.nn.sigmoid(g1[...]) * y1[...]
        acc += jax.nn.sigmoid(g2[...]) * y2[...]
        o_ref[...] = acc.astype(BF16)

    return _pcall(
        body, name=name, grid=(n // tm, nb), in_specs=gspecs + [yspec] * 3, out_specs=yspec,
        out_shape=jax.ShapeDtypeStruct((n, d), BF16), compiler_params=_params(("parallel", "parallel")),
    )(proj, proj, proj, *ys)


def _gate_merge_bwd(proj, ys, dmerged, *, name):
    n, d = ys[0].shape
    tm, bw, nb, yspec, gspecs = _gate_specs(n, d)

    def body(g0, g1, g2, y0, y1, y2, dm_ref, dy0, dy1, dy2, dgl0, dgl1, dgl2):
        dm = dm_ref[...]
        for g_ref, y_ref, dy_ref, dgl_ref in ((g0, y0, dy0, dgl0), (g1, y1, dy1, dgl1), (g2, y2, dy2, dgl2)):
            s = jax.nn.sigmoid(g_ref[...])
            dy_ref[...] = (dm * s).astype(BF16)
            dgl_ref[...] = (dm * y_ref[...] * s * (1.0 - s)).astype(BF16)

    sds = jax.ShapeDtypeStruct((n, d), BF16)
    return _pcall(
        body, name=name, grid=(n // tm, nb), in_specs=gspecs + [yspec] * 4, out_specs=[yspec] * 6,
        out_shape=[sds] * 6, compiler_params=_params(("parallel", "parallel")),
    )(proj, proj, proj, *ys, dmerged)


def _loss_head(y, tgt, *, name):
    n, d = y.shape
    tm = _tile(n, 256, 8)
    steps = n // tm

    def body(y_ref, t_ref, dy_ref, loss_ref, acc):
        i = pl.program_id(0)

        @pl.when(i == 0)
        def _():
            acc[...] = jnp.zeros_like(acc)

        e = y_ref[...] - t_ref[...]
        dy_ref[...] = e * (1.0 / d)
        acc[...] += jnp.sum(e * e, axis=0, keepdims=True)

        @pl.when(i == steps - 1)
        def _():
            tot = jnp.sum(acc[...], axis=1, keepdims=True) * (0.5 / d)
            loss_ref[...] = jnp.broadcast_to(tot, loss_ref.shape)

    row = pl.BlockSpec((tm, d), lambda i: (i, 0))
    return _pcall(
        body, name=name, grid=(steps,), in_specs=[row, row], out_specs=[row, pl.BlockSpec((8, LANES), lambda i: (0, 0))],
        out_shape=[jax.ShapeDtypeStruct((n, d), F32), jax.ShapeDtypeStruct((8, LANES), F32)],
        scratch_shapes=[pltpu.VMEM((1, d), F32)], compiler_params=_params(("arbitrary",)),
    )(y, tgt)


def _assemble_cols(parts, width, *, name):
    n = parts[0].shape[0]
    tm = _tile(n, 256, 16)
    widths = [p.shape[1] for p in parts]

    def body(*refs):
        o_ref = refs[-1]
        off = 0
        for ref, w in zip(refs[:-1], widths):
            o_ref[:, off:off + w] = ref[...].astype(BF16)
            off += w
        if off < width:
            o_ref[:, off:] = jnp.zeros((tm, width - off), BF16)

    return _pcall(
        body, name=name, grid=(n // tm,), in_specs=[pl.BlockSpec((tm, w), lambda i: (i, 0)) for w in widths],
        out_specs=pl.BlockSpec((tm, width), lambda i: (i, 0)), out_shape=jax.ShapeDtypeStruct((n, width), BF16),
        compiler_params=_params(("parallel",)),
    )(*parts)


def _dil_weights(lses, *, name):
    shp = lses[0].shape

    def body(l0, l1, l2, lse_ref, w0, w1, w2):
        a, b, c = l0[...], l1[...], l2[...]
        m = jnp.maximum(jnp.maximum(a, b), c)
        ea, eb, ec = jnp.exp(a - m), jnp.exp(b - m), jnp.exp(c - m)
        den = ea + eb + ec
        lse_ref[...] = m + jnp.log(den)
        w0[...] = ea / den
        w1[...] = eb / den
        w2[...] = ec / den

    vmem = pl.BlockSpec(memory_space=pltpu.VMEM)
    return _pcall(body, name=name, in_specs=[vmem] * 3, out_specs=[vmem] * 4, out_shape=[jax.ShapeDtypeStruct(shp, F32)] * 4)(*lses)


def _dil_mix(os_, ws, *, name):
    n, w = os_[0].shape
    tm = _tile(n, 512, 8)

    def body(o0, o1, o2, w0, w1, w2, out_ref):
        out_ref[...] = w0[...] * o0[...] + w1[...] * o1[...] + w2[...] * o2[...]

    spec = pl.BlockSpec((tm, w), lambda i: (i, 0))
    return _pcall(
        body, name=name, grid=(n // tm,), in_specs=[spec] * 6, out_specs=spec, out_shape=jax.ShapeDtypeStruct((n, w), F32),
        compiler_params=_params(("parallel",)),
    )(*os_, *ws)


def _adamw(gsrc, w, m, v, *, name):
    s, dep, a, b = gsrc.shape
    ta = _tile(a, max(16, (ADAMW_BLOCK_ELEMS // b) // 16 * 16), 16)
    c1 = 1.0 / (1.0 - ADAM_B1 ** ADAM_STEP)
    c2 = 1.0 / (1.0 - ADAM_B2 ** ADAM_STEP)

    def body(gs_ref, w_ref, m_ref, v_ref, g_ref, d_ref, m2_ref, v2_ref):
        g = gs_ref[0].astype(F32)
        for i in range(1, s):
            g = g + gs_ref[i].astype(F32)
        m2 = ADAM_B1 * m_ref[...] + (1.0 - ADAM_B1) * g
        v2 = ADAM_B2 * v_ref[...] + (1.0 - ADAM_B2) * (g * g)
        g_ref[...] = g
        m2_ref[...] = m2
        v2_ref[...] = v2
        d_ref[...] = -ADAM_LR * ((m2 * c1) / (jnp.sqrt(v2 * c2) + ADAM_EPS) + ADAM_WD * w_ref[...])

    spec = pl.BlockSpec((None, ta, b), lambda l, i: (l, i, 0))
    sds = jax.ShapeDtypeStruct((dep, a, b), F32)
    return _pcall(
        body, name=name, grid=(dep, a // ta),
        in_specs=[pl.BlockSpec((s, None, ta, b), lambda l, i: (0, l, i, 0)), spec, spec, spec],
        out_specs=[spec] * 4, out_shape=[sds] * 4, compiler_params=_params(("parallel", "parallel")),
    )(gsrc, w, m, v)


def _mask_a():
    return lax.broadcasted_iota(jnp.int32, (1, LANES), 1) < HEAD_DIM


def _half_sum(x, m_a):
    sa = jnp.sum(jnp.where(m_a, x, 0.0), axis=1, keepdims=True)
    sb = jnp.sum(jnp.where(m_a, 0.0, x), axis=1, keepdims=True)
    return jnp.where(m_a, sa, sb)


def _head_inv(x, m_a):
    return lax.rsqrt(_half_sum(x * x, m_a) * (1.0 / HEAD_DIM) + EPS)


def _swap32(x):
    first = (lax.broadcasted_iota(jnp.int32, (1, LANES), 1) % HEAD_DIM) < (HEAD_DIM // 2)
    return jnp.where(first, pltpu.roll(x, LANES - HEAD_DIM // 2, 1), pltpu.roll(x, HEAD_DIM // 2, 1))


def _tri(blk, rel):
    r = lax.broadcasted_iota(jnp.int32, (blk, blk), 0)
    c = lax.broadcasted_iota(jnp.int32, (blk, blk), 1)
    return jnp.where(rel(r, c), 1.0, 0.0).astype(BF16)


def _cumdot(x, u, parts):
    acc = None
    r = x
    for i in range(parts):
        xi = r.astype(BF16)
        t = jnp.dot(xi, u, preferred_element_type=F32)
        acc = t if acc is None else acc + t
        if i + 1 < parts:
            r = r - xi.astype(F32)
    return acc


def _rows(i, blk):
    return pl.ds(pl.multiple_of(i * blk, blk), blk)


def _col_spec(n, off):
    return pl.BlockSpec((None, n, LANES), lambda z, p, off=off: (z, 0, off + p))


def _att_blk(n):
    return ATT_BLK if n % ATT_BLK == 0 else min(LANES, n)


def _att_blocks(n):
    bk = _att_blk(n)
    return (ATT_BQ if n % ATT_BQ == 0 else bk), bk


def _loop(lo, hi, fn):
    def it(i, c):
        fn(i)
        return c

    lax.fori_loop(lo, hi, it, 0)


def _normed(src, g_ref, rope_refs, rows, m_a):
    xv = src[rows, :]
    xn = xv * _head_inv(xv, m_a) * g_ref[...]
    if rope_refs is not None:
        xn = xn * rope_refs[0][rows, :] + _swap32(xn) * rope_refs[1][rows, :]
    return xn


def _bias_lane(h):
    return HEAD_DIM if h == 0 else 0


def _k_for_head(kn, kb_row, h, m_h, lane, blk):
    out = jnp.where(m_h, kn, 0.0)
    if kb_row is not None:
        col = jnp.transpose(jnp.broadcast_to(kb_row, (LANES, blk)))
        hi = col.astype(BF16).astype(F32)
        mid = (col - hi).astype(BF16).astype(F32)
        lo = col - hi - mid
        b = _bias_lane(h)
        out = jnp.where(lane == b, hi, jnp.where(lane == b + 1, mid, jnp.where(lane == b + 2, lo, out)))
    return out.astype(BF16)


def _q_for_head(qb, h, m_h, lane, biased):
    out = jnp.where(m_h, qb, 0)
    if biased:
        b = _bias_lane(h)
        out = jnp.where((lane >= b) & (lane < b + 3), jnp.ones_like(out), out)
    return out


def _head_rows(x, parts=3):
    rr = lax.broadcasted_iota(jnp.int32, (8, LANES), 0)
    ll = lax.broadcasted_iota(jnp.int32, (8, LANES), 1)
    sel = jnp.where(((rr == 0) & (ll < HEAD_DIM)) | ((rr == 1) & (ll >= HEAD_DIM)), 1.0, 0.0).astype(BF16)
    acc = None
    rem = x
    for i in range(parts):
        xi = rem.astype(BF16)
        t = lax.dot_general(sel, xi, NT, preferred_element_type=F32)
        acc = t if acc is None else acc + t
        if i + 1 < parts:
            rem = rem - xi.astype(F32)
    return acc


def _cumdot_left(u, x, parts):
    acc = None
    rem = x
    for i in range(parts):
        xi = rem.astype(BF16)
        t = jnp.dot(u, xi, preferred_element_type=F32)
        acc = t if acc is None else acc + t
        if i + 1 < parts:
            rem = rem - xi.astype(F32)
    return acc


def _q_minus_k(bk, bq):
    return lax.broadcasted_iota(jnp.int32, (bk, bq), 1) - lax.broadcasted_iota(jnp.int32, (bk, bq), 0)


def _stat_spec(nb, blk):
    return pl.BlockSpec((None, 2, nb, 1, blk), lambda z, p: (z, p, 0, 0, 0))


def _stream_rows(stride):
    if stride == 1:
        return _rows
    c = pl.program_id(2)
    return lambda i, blk: pl.ds(c + i * (blk * stride), blk, stride=stride)


def _attn_specs(t, npairs, stride, nq, bq):
    col = lambda off: pl.BlockSpec((None, t, LANES), lambda z, p, c: (z, 0, off + p))
    vec = pl.BlockSpec((1, LANES), lambda z, p, c: (0, 0))
    seq = pl.BlockSpec((None, t, LANES), lambda z, p, c: (z, 0, 0))
    stat = pl.BlockSpec((None, 2, nq, 1, bq), lambda z, p, c: (z * stride + c, p, 0, 0, 0))
    return col, vec, seq, stat


def _attn_fwd(src, offs, npairs, gq, gk, *, rope=None, kbias=None, window, stride=1, name, cargo=None):
    bs, t, _ = src.shape
    n = t // stride
    bq, bk = _att_blocks(n)
    nq, nk, rq = n // bq, n // bk, bq // bk
    full = window >= n
    wblk = -(-window // bk)
    biased = kbias is not None

    def body(*refs):
        it = iter(refs)
        q_ref, k_ref, v_ref, gq_ref, gk_ref = (next(it) for _ in range(5))
        rope_refs = (next(it), next(it)) if rope is not None else None
        kb_ref = next(it) if biased else None
        o_ref, lse_ref, qn_s, kh_s, vt_s, acc_s, m_s = (next(it) for _ in range(7))
        m_a = _mask_a()
        masks = (m_a, jnp.logical_not(m_a))
        lane = lax.broadcasted_iota(jnp.int32, (1, LANES), 1)
        row = lax.broadcasted_iota(jnp.int32, (LANES, 1), 0)
        tok = _stream_rows(stride)

        def prep(c):
            rows = _rows(c, bk)
            trows = tok(c, bk)
            qn_s[rows, :] = (_normed(q_ref, gq_ref, rope_refs, trows, m_a) * SCALE).astype(BF16)
            kn = _normed(k_ref, gk_ref, rope_refs, trows, m_a)
            vt = jnp.transpose(v_ref[trows, :])
            for h in (0, 1):
                kh_s[h, rows, :] = _k_for_head(kn, kb_ref[h, c] if biased else None, h, masks[h], lane, bk)
                vt_s[h, c] = jnp.where(row == _bias_lane(h), 1.0, vt).astype(BF16)

        _loop(0, nk, prep)
        qk = _q_minus_k(bk, bq)

        def qblock(qi):
            rows = _rows(qi, bq)
            qb = qn_s[rows, :]
            qh = [_q_for_head(qb, h, masks[h], lane, biased) for h in (0, 1)]
            m_s[...] = jnp.full(m_s.shape, NEG, F32)
            acc_s[...] = jnp.zeros_like(acc_s)

            def step(kj, masked):
                cols = _rows(kj, bk)
                sts = [lax.dot_general(kh_s[h, cols, :], qh[h], NT, preferred_element_type=F32) for h in (0, 1)]
                old = [(m_s[h], acc_s[h]) for h in (0, 1)]
                if masked:
                    d = qk + (qi * bq - kj * bk)
                    ok = (d >= 0) & (d <= window)
                    sts = [jnp.where(ok, st, NEG) for st in sts]
                new = []
                for h in (0, 1):
                    m, acc = old[h]
                    m2 = jnp.maximum(m, jnp.max(sts[h], axis=0, keepdims=True))
                    pt = jnp.exp(sts[h] - m2).astype(BF16)
                    new.append((m2, jnp.exp(m - m2) * acc + jnp.dot(vt_s[h, kj], pt, preferred_element_type=F32)))
                for h in (0, 1):
                    m_s[h], acc_s[h] = new[h]

            if full:
                _loop(0, qi * rq, lambda kj: step(kj, False))
                _loop(qi * rq, (qi + 1) * rq, lambda kj: step(kj, True))
            else:
                _loop(jnp.maximum(qi * rq - wblk, 0), (qi + 1) * rq, lambda kj: step(kj, True))
            outs = []
            for h in (0, 1):
                acc_t = acc_s[h]
                den = acc_t[_bias_lane(h):_bias_lane(h) + 1, :]
                outs.append(jnp.transpose(acc_t / den))
                lse_ref[h, qi] = m_s[h] + jnp.log(den)
            o_ref[tok(qi, bq), :] = jnp.where(m_a, outs[0], outs[1])

        _loop(0, nq, qblock)

    col, vec, seq, stat = _attn_specs(t, npairs, stride, nq, bq)
    ins = [src, src, src, gq, gk]
    specs = [col(offs[0]), col(offs[1]), col(offs[2]), vec, vec]
    if rope is not None:
        ins += list(rope)
        specs += [seq, seq]
    if biased:
        ins.append(kbias)
        specs.append(pl.BlockSpec((None, 2, nk, 1, bk), lambda z, p, c: (z, p, 0, 0, 0)))
    scratch = [pltpu.VMEM((n, LANES), BF16), pltpu.VMEM((2, n, LANES), BF16), pltpu.VMEM((2, nk, LANES, bk), BF16)]
    scratch += [pltpu.VMEM((2, LANES, bq), F32), pltpu.VMEM((2, 1, bq), F32)]
    return _call(
        body, cargo=cargo, name=name, grid=(bs, npairs, stride), in_specs=specs, out_specs=[col(0), stat],
        out_shape=[jax.ShapeDtypeStruct((bs, t, LANES * npairs), F32),
                   jax.ShapeDtypeStruct((bs * stride, 2 * npairs, nq, 1, bq), F32)],
        scratch_shapes=scratch, compiler_params=_params(("parallel", "parallel", "arbitrary")),
    )(*ins)


def _attn_bwd(src, offs, npairs, gq, gk, o, do, lse, *, rope=None, kbias=None, window, stride=1, name, cargo=None):
    bs, t, _ = src.shape
    n = t // stride
    bq, bk = _att_blocks(n)
    nq, nk, rq = n // bq, n // bk, bq // bk
    full = window >= n
    wblk = -(-window // bk)
    biased = kbias is not None
    gdt = BF16 if stride == 1 else F32

    def body(*refs):
        it = iter(refs)
        q_ref, k_ref, v_ref, gq_ref, gk_ref, o_ref, do_ref, lse_ref = (next(it) for _ in range(8))
        rope_refs = (next(it), next(it)) if rope is not None else None
        kb_ref = next(it) if biased else None
        dq_ref, dk_ref, dv_ref, dg_ref = (next(it) for _ in range(4))
        dkb_ref = next(it) if biased else None
        qn_s, kh_s, vb_s, kt_s, dqn_s, dkh_s, dv_s, dq_s, rs_s = (next(it) for _ in range(9))
        m_a = _mask_a()
        masks = (m_a, jnp.logical_not(m_a))
        lane = lax.broadcasted_iota(jnp.int32, (1, LANES), 1)
        tok = _stream_rows(stride)

        @pl.when((pl.program_id(0) == 0) & (pl.program_id(1) == 0) & (pl.program_id(2) == 0))
        def _():
            dg_ref[...] = jnp.zeros_like(dg_ref)

        def prep(c):
            rows = _rows(c, bk)
            trows = tok(c, bk)
            qn_s[rows, :] = (_normed(q_ref, gq_ref, rope_refs, trows, m_a) * SCALE).astype(BF16)
            kn = _normed(k_ref, gk_ref, rope_refs, trows, m_a)
            kt_s[c] = jnp.transpose(kn).astype(BF16)
            vb_s[rows, :] = v_ref[trows, :].astype(BF16)
            for h in (0, 1):
                kh_s[h, rows, :] = _k_for_head(kn, kb_ref[h, c] if biased else None, h, masks[h], lane, bk)

        _loop(0, nk, prep)
        dkh_s[...] = jnp.zeros_like(dkh_s)
        dv_s[...] = jnp.zeros_like(dv_s)
        qk = _q_minus_k(bk, bq)

        def qblock(qi):
            rows = _rows(qi, bq)
            qb = qn_s[rows, :]
            trows = tok(qi, bq)
            dob = do_ref[trows, :]
            delta = _head_rows(dob * o_ref[trows, :])
            qh = [_q_for_head(qb, h, masks[h], lane, biased) for h in (0, 1)]
            doms = [jnp.where(masks[h], dob, 0.0).astype(BF16) for h in (0, 1)]
            lses = [lse_ref[h, qi] for h in (0, 1)]
            dq_s[...] = jnp.zeros_like(dq_s)
            if biased:
                for h in (0, 1):
                    rs_s[h, qi] = jnp.zeros((1, bq), F32)

            def step(kj, masked):
                cols = _rows(kj, bk)
                vb = vb_s[cols, :]
                kt = kt_s[kj]
                sts = [lax.dot_general(kh_s[h, cols, :], qh[h], NT, preferred_element_type=F32) for h in (0, 1)]
                dpts = [lax.dot_general(vb, doms[h], NT, preferred_element_type=F32) for h in (0, 1)]
                if masked:
                    d = qk + (qi * bq - kj * bk)
                    ok = (d >= 0) & (d <= window)
                    sts = [jnp.where(ok, st, NEG) for st in sts]
                new = []
                for h in (0, 1):
                    pt = jnp.exp(sts[h] - lses[h])
                    dst = pt * (dpts[h] - delta[h:h + 1, :])
                    dsb = dst.astype(BF16)
                    tk = jnp.dot(dsb, qh[h], preferred_element_type=F32)
                    if biased:
                        tk = tk + jnp.dot((dst - dsb.astype(F32)).astype(BF16), qh[h], preferred_element_type=F32)
                    tv = jnp.dot(pt.astype(BF16), doms[h], preferred_element_type=F32)
                    tq = jnp.dot(kt, dsb, preferred_element_type=F32)
                    new.append((tk, tv, tq, jnp.sum(dst, axis=0, keepdims=True) if biased else None))
                for h in (0, 1):
                    dkh_s[h, cols, :] += new[h][0]
                    dq_s[h] += new[h][2]
                    if biased:
                        rs_s[h, qi] += new[h][3]
                dv_s[cols, :] += new[0][1] + new[1][1]

            if full:
                _loop(0, qi * rq, lambda kj: step(kj, False))
                _loop(qi * rq, (qi + 1) * rq, lambda kj: step(kj, True))
            else:
                _loop(jnp.maximum(qi * rq - wblk, 0), (qi + 1) * rq, lambda kj: step(kj, True))
            dqn_s[rows, :] = jnp.where(m_a, jnp.transpose(dq_s[0]), jnp.transpose(dq_s[1])) * SCALE

        _loop(0, nq, qblock)

        def finish(c, carry):
            rows = _rows(c, bq)
            out = []
            dk_pair = [dkh_s[0, rows, :], dkh_s[1, rows, :]]
            if biased:
                for h in (0, 1):
                    b = _bias_lane(h)
                    dkb_row = jnp.transpose(dk_pair[h])[b:b + 1, :] - rs_s[h, c]
                    for j in range(rq):
                        dkb_ref[h, c * rq + j] = dkb_row[:, j * bk:(j + 1) * bk]
            trows = tok(c, bq)
            dv_ref[trows, :] = dv_s[rows, :].astype(gdt)
            grads = (dqn_s[rows, :], jnp.where(m_a, dk_pair[0], dk_pair[1]))
            for src_ref, g_ref, dxn, dst in ((q_ref, gq_ref, grads[0], dq_ref), (k_ref, gk_ref, grads[1], dk_ref)):
                xv = src_ref[trows, :]
                inv = _head_inv(xv, m_a)
                y = xv * inv
                if rope_refs is not None:
                    dxn = dxn * rope_refs[0][trows, :] + _swap32(dxn * rope_refs[1][trows, :])
                dy = dxn * g_ref[...]
                dst[trows, :] = (inv * (dy - y * (_half_sum(dy * y, m_a) * (1.0 / HEAD_DIM)))).astype(gdt)
                out.append(jnp.sum(dxn * y, axis=0, keepdims=True))
            return carry[0] + out[0], carry[1] + out[1]

        zero = jnp.zeros((1, LANES), F32)
        dgq, dgk = lax.fori_loop(0, nq, finish, (zero, zero))
        dg_ref[0:1, :] += dgq
        dg_ref[1:2, :] += dgk

    col, vec, seq, stat = _attn_specs(t, npairs, stride, nq, bq)
    ospec = col(0)
    ins = [src, src, src, gq, gk, o, do, lse]
    specs = [col(offs[0]), col(offs[1]), col(offs[2]), vec, vec, ospec, ospec, stat]
    if rope is not None:
        ins += list(rope)
        specs += [seq, seq]
    sds = jax.ShapeDtypeStruct((bs, t, LANES * npairs), gdt)
    out_shape = [sds, sds, sds, jax.ShapeDtypeStruct((8, LANES), F32)]
    out_specs = [ospec, ospec, ospec, pl.BlockSpec((8, LANES), lambda z, p, c: (0, 0))]
    if biased:
        kbspec = pl.BlockSpec((None, 2, nk, 1, bk), lambda z, p, c: (z, p, 0, 0, 0))
        ins.append(kbias)
        specs.append(kbspec)
        out_shape.append(jax.ShapeDtypeStruct(kbias.shape, F32))
        out_specs.append(kbspec)
    scratch = [pltpu.VMEM((n, LANES), BF16), pltpu.VMEM((2, n, LANES), BF16), pltpu.VMEM((n, LANES), BF16)]
    scratch += [pltpu.VMEM((nk, LANES, bk), BF16), pltpu.VMEM((n, LANES), F32), pltpu.VMEM((2, n, LANES), F32)]
    scratch += [pltpu.VMEM((n, LANES), F32), pltpu.VMEM((2, LANES, bq), F32), pltpu.VMEM((2, nq, 1, bq), F32)]
    return _call(
        body, cargo=cargo, name=name, grid=(bs, npairs, stride), in_specs=specs, out_specs=out_specs, out_shape=out_shape,
        scratch_shapes=scratch, compiler_params=_params(("arbitrary", "arbitrary", "arbitrary")),
    )(*ins)


SB_LOG_PARTS = 2
SB_GRAD_PARTS = 1


def _log_sig_pair(z):
    lsn = jnp.minimum(-z, 0.0) - jnp.log(1.0 + jnp.exp(-jnp.abs(z)))
    return lsn, z + lsn


def _sb_fwd(src, offs, npairs, *, name, cargo=None):
    zs, n, _ = src.shape
    bq, bk = _att_blocks(n)
    nq, nk, rq = n // bq, n // bk, bq // bk

    def body(q_ref, k_ref, v_ref, o_ref, lt_ref, qs_s, kb_s, vt_s, acc_s, c_s):
        m_a = _mask_a()
        masks = (m_a, jnp.logical_not(m_a))

        def prep(c):
            rows = _rows(c, bk)
            qs_s[rows, :] = (q_ref[rows, :] * SCALE).astype(BF16)
            kb_s[rows, :] = k_ref[rows, :].astype(BF16)
            vt_s[c] = jnp.transpose(v_ref[rows, :]).astype(BF16)

        _loop(0, nk, prep)
        qk = _q_minus_k(bk, bq)
        u_gt = _tri(bk, lambda r, c: c > r)

        def qblock(qi):
            rows = _rows(qi, bq)
            qb = qs_s[rows, :]
            qms = [jnp.where(masks[h], qb, 0) for h in (0, 1)]
            acc_s[...] = jnp.zeros_like(acc_s)
            c_s[...] = jnp.zeros_like(c_s)

            def step(kj, masked):
                kb, vt = kb_s[_rows(kj, bk), :], vt_s[kj]
                zts = [lax.dot_general(kb, qms[h], NT, preferred_element_type=F32) for h in (0, 1)]
                old = [c_s[h] for h in (0, 1)]
                if masked:
                    ok = (qk + (qi * bq - kj * bk)) > 0
                new = []
                for h in (0, 1):
                    lsn, lsp = _log_sig_pair(zts[h])
                    if masked:
                        lsn = jnp.where(ok, lsn, 0.0)
                    at = jnp.exp(lsp + (old[h] + _cumdot_left(u_gt, lsn, SB_LOG_PARTS)))
                    if masked:
                        at = jnp.where(ok, at, 0.0)
                    new.append((jnp.dot(vt, at.astype(BF16), preferred_element_type=F32),
                                old[h] + jnp.sum(lsn, axis=0, keepdims=True)))
                for h in (0, 1):
                    acc_s[h] += new[h][0]
                    c_s[h] = new[h][1]

            _loop(0, rq, lambda t: step((qi + 1) * rq - 1 - t, True))
            _loop(0, qi * rq, lambda t: step(qi * rq - 1 - t, False))
            o_ref[rows, :] = jnp.where(m_a, jnp.transpose(acc_s[0]), jnp.transpose(acc_s[1]))
            for h in (0, 1):
                lt_ref[h, qi] = c_s[h]

        _loop(0, nq, qblock)

    scratch = [pltpu.VMEM((n, LANES), BF16)] * 2 + [pltpu.VMEM((nk, LANES, bk), BF16)]
    scratch += [pltpu.VMEM((2, LANES, bq), F32), pltpu.VMEM((2, 1, bq), F32)]
    return _call(
        body, cargo=cargo, name=name, grid=(zs, npairs),
        in_specs=[_col_spec(n, offs[0]), _col_spec(n, offs[1]), _col_spec(n, offs[2])],
        out_specs=[_col_spec(n, 0), _stat_spec(nq, bq)],
        out_shape=[jax.ShapeDtypeStruct((zs, n, LANES * npairs), F32), jax.ShapeDtypeStruct((zs, 2 * npairs, nq, 1, bq), F32)],
        scratch_shapes=scratch, compiler_params=_params(("parallel", "parallel")),
    )(src, src, src)


def _sb_bwd(src, offs, npairs, do, ltot, *, name, cargo=None):
    zs, n, _ = src.shape
    bq, bk = _att_blocks(n)
    nq, nk, rq = n // bq, n // bk, bq // bk

    def body(q_ref, k_ref, v_ref, do_ref, lt_ref, dq_ref, dk_ref, dv_ref, qs_s, kb_s, vb_s, kt_s, dk_s, dv_s, dq_s, lp_s, ep_s):
        m_a = _mask_a()
        masks = (m_a, jnp.logical_not(m_a))

        def prep(c):
            rows = _rows(c, bk)
            qs_s[rows, :] = (q_ref[rows, :] * SCALE).astype(BF16)
            kv = k_ref[rows, :]
            kb_s[rows, :] = kv.astype(BF16)
            kt_s[c] = jnp.transpose(kv).astype(BF16)
            vb_s[rows, :] = v_ref[rows, :].astype(BF16)

        _loop(0, nk, prep)
        dk_s[...] = jnp.zeros_like(dk_s)
        dv_s[...] = jnp.zeros_like(dv_s)
        qk = _q_minus_k(bk, bq)
        u_le = _tri(bk, lambda r, c: c <= r)
        u_lt = _tri(bk, lambda r, c: c < r)

        def qblock(qi):
            rows = _rows(qi, bq)
            qb = qs_s[rows, :]
            dob = do_ref[rows, :]
            qms = [jnp.where(masks[h], qb, 0) for h in (0, 1)]
            doms = [jnp.where(masks[h], dob, 0.0).astype(BF16) for h in (0, 1)]
            lts = [lt_ref[h, qi] for h in (0, 1)]
            dq_s[...] = jnp.zeros_like(dq_s)
            lp_s[...] = jnp.zeros_like(lp_s)
            ep_s[...] = jnp.zeros_like(ep_s)

            def step(kj, masked):
                cols = _rows(kj, bk)
                kb, vb, kt = kb_s[cols, :], vb_s[cols, :], kt_s[kj]
                zts = [lax.dot_general(kb, qms[h], NT, preferred_element_type=F32) for h in (0, 1)]
                dats = [lax.dot_general(vb, doms[h], NT, preferred_element_type=F32) for h in (0, 1)]
                old = [(lp_s[h], ep_s[h]) for h in (0, 1)]
                if masked:
                    ok = (qk + (qi * bq - kj * bk)) > 0
                new = []
                for h in (0, 1):
                    lp, ep = old[h]
                    lsn, lsp = _log_sig_pair(zts[h])
                    sig = jnp.exp(lsp)
                    if masked:
                        lsn = jnp.where(ok, lsn, 0.0)
                    at = jnp.exp(lsp + (lts[h] - (lp + _cumdot_left(u_le, lsn, SB_LOG_PARTS))))
                    if masked:
                        at = jnp.where(ok, at, 0.0)
                    et = dats[h] * at
                    big_e = ep + _cumdot_left(u_lt, et, SB_GRAD_PARTS)
                    dzt = et - sig * (et + big_e)
                    if masked:
                        dzt = jnp.where(ok, dzt, 0.0)
                    dzb = dzt.astype(BF16)
                    new.append((jnp.dot(dzb, qms[h], preferred_element_type=F32),
                                jnp.dot(at.astype(BF16), doms[h], preferred_element_type=F32),
                                jnp.dot(kt, dzb, preferred_element_type=F32),
                                lp + jnp.sum(lsn, axis=0, keepdims=True), ep + jnp.sum(et, axis=0, keepdims=True)))
                for h in (0, 1):
                    dq_s[h] += new[h][2]
                    lp_s[h], ep_s[h] = new[h][3], new[h][4]
                dk_s[cols, :] += new[0][0] + new[1][0]
                dv_s[cols, :] += new[0][1] + new[1][1]

            _loop(0, qi * rq, lambda kj: step(kj, False))
            _loop(qi * rq, (qi + 1) * rq, lambda kj: step(kj, True))
            dq_ref[rows, :] = (jnp.where(m_a, jnp.transpose(dq_s[0]), jnp.transpose(dq_s[1])) * SCALE).astype(BF16)

        _loop(0, nq, qblock)

        def store(c):
            rows = _rows(c, bk)
            dk_ref[rows, :] = dk_s[rows, :].astype(BF16)
            dv_ref[rows, :] = dv_s[rows, :].astype(BF16)

        _loop(0, nk, store)

    ospec = _col_spec(n, 0)
    sds = jax.ShapeDtypeStruct((zs, n, LANES * npairs), BF16)
    scratch = [pltpu.VMEM((n, LANES), BF16)] * 3 + [pltpu.VMEM((nk, LANES, bk), BF16)] + [pltpu.VMEM((n, LANES), F32)] * 2
    scratch += [pltpu.VMEM((2, LANES, bq), F32), pltpu.VMEM((2, 1, bq), F32), pltpu.VMEM((2, 1, bq), F32)]
    return _call(
        body, cargo=cargo, name=name, grid=(zs, npairs),
        in_specs=[_col_spec(n, offs[0]), _col_spec(n, offs[1]), _col_spec(n, offs[2]), ospec, _stat_spec(nq, bq)],
        out_specs=[ospec] * 3, out_shape=[sds] * 3, scratch_shapes=scratch,
        compiler_params=_params(("parallel", "parallel")),
    )(src, src, src, do, ltot)


def _fox_gate_fwd(lg, bias, *, name):
    bs, nh, t = lg.shape
    blk = min(LANES, t)

    def body(lg_ref, b_ref, kb_ref):
        u_le = _tri(blk, lambda r, c: r <= c)
        carry = jnp.zeros((nh, 1), F32)
        for j in range(t // blk):
            sl = slice(j * blk, (j + 1) * blk)
            xv = lg_ref[:, sl] + b_ref[...]
            lf = jnp.minimum(xv, 0.0) - jnp.log(1.0 + jnp.exp(-jnp.abs(xv)))
            kb_ref[:, sl] = -(carry + _cumdot(lf, u_le, 3))
            carry = carry + jnp.sum(lf, axis=1, keepdims=True)

    spec = pl.BlockSpec((None, nh, t), lambda i: (i, 0, 0))
    return _pcall(
        body, name=name, grid=(bs,), in_specs=[spec, pl.BlockSpec((nh, 1), lambda i: (0, 0))], out_specs=spec,
        out_shape=jax.ShapeDtypeStruct((bs, nh, t), F32), compiler_params=_params(("parallel",)),
    )(lg, bias)


def _fox_gate_bwd(dkb, lg, bias, *, name):
    bs, nh, t = lg.shape
    blk = min(LANES, t)

    def body(dkb_ref, lg_ref, b_ref, dlg_ref, db_ref):
        @pl.when(pl.program_id(0) == 0)
        def _():
            db_ref[...] = jnp.zeros_like(db_ref)

        u_ge = _tri(blk, lambda r, c: r >= c)
        carry = jnp.zeros((nh, 1), F32)
        tot = jnp.zeros((nh, 1), F32)
        for j in reversed(range(t // blk)):
            sl = slice(j * blk, (j + 1) * blk)
            df = -dkb_ref[:, sl]
            dlf = carry + _cumdot(df, u_ge, 3)
            carry = carry + jnp.sum(df, axis=1, keepdims=True)
            xv = lg_ref[:, sl] + b_ref[...]
            dlg = dlf * jax.nn.sigmoid(-xv)
            dlg_ref[:, sl] = dlg
            tot = tot + jnp.sum(dlg, axis=1, keepdims=True)
        db_ref[...] += jnp.broadcast_to(tot, db_ref.shape)

    spec = pl.BlockSpec((None, nh, t), lambda i: (i, 0, 0))
    return _pcall(
        body, name=name, grid=(bs,), in_specs=[spec, spec, pl.BlockSpec((nh, 1), lambda i: (0, 0))],
        out_specs=[spec, pl.BlockSpec((nh, LANES), lambda i: (0, 0))],
        out_shape=[jax.ShapeDtypeStruct((bs, nh, t), F32), jax.ShapeDtypeStruct((nh, LANES), F32)],
        compiler_params=_params(("arbitrary",)),
    )(dkb, lg, bias)


def _place():
    return lax.axis_index("x"), lax.axis_index("y"), lax.axis_index("c")


def _flip(v, f):
    return 1 - v if f else v


FLIPS = [(fx, fy, fc) for fx in (0, 1) for fy in (0, 1) for fc in (0, 1)][1:]


def _comm_sems(nw):
    return [pltpu.SemaphoreType.DMA((7, nw)), pltpu.SemaphoreType.DMA((7, nw)), pltpu.SemaphoreType.DMA((nw,))]


def _gather_cargo(shards, on_done):
    nw = len(shards)

    def parts(x_refs, out_refs, sems):
        send_sems, recv_sems, local_sems = sems
        x, y, cc = _place()
        me, sibling = (x, y, cc), (x, y, 1 - cc)
        chips = [(1 - x, y), (x, 1 - y), (1 - x, 1 - y)]

        def slot(i, px, py, pc):
            return out_refs[i].at[4 * px + 2 * py + pc]

        def copy(i, k, block, to, src=None):
            return pltpu.make_async_remote_copy(
                src_ref=slot(i, *block) if src is None else src, dst_ref=slot(i, *block),
                send_sem=send_sems.at[k, i], recv_sem=recv_sems.at[k, i], device_id=to, device_id_type=MESH)

        mine = [pltpu.make_async_copy(x_refs[i], slot(i, *me), local_sems.at[i]) for i in range(nw)]
        first = []
        for i in range(nw):
            first.append(copy(i, 0, me, sibling, src=x_refs[i]))
            first += [copy(i, 1 + j, me, (*chip, cc), src=x_refs[i]) for j, chip in enumerate(chips)]
        return me, sibling, chips, cc, copy, mine, first

    def start(x_refs, out_refs, sems):
        *_, mine, first = parts(x_refs, out_refs, sems)
        for cp in mine + first:
            cp.start()

    def finish(x_refs, out_refs, sems):
        me, sibling, chips, cc, copy, mine, first = parts(x_refs, out_refs, sems)
        passed = []
        for i in range(nw):
            for j, chip in enumerate(chips):
                copy(i, 1 + j, (*chip, cc), me).wait_recv()
                passed.append(copy(i, 4 + j, (*chip, cc), sibling))
                passed[-1].start()
        for i in range(nw):
            copy(i, 0, sibling, me).wait_recv()
            for j, chip in enumerate(chips):
                copy(i, 4 + j, (*chip, 1 - cc), me).wait_recv()
        for cp in first + passed:
            cp.wait_send()
        for cp in mine:
            cp.wait()

    out_shape = [jax.ShapeDtypeStruct((N_DEV, *s.shape), s.dtype) for s in shards]
    return _Cargo(shards, out_shape, _comm_sems(nw), start, finish, on_done)


def _scatter_cargo(slots, prev, layer, depth, on_done):
    nw = len(slots)

    def parts(refs, recv_refs, sems):
        g_refs = refs[:nw]
        send_sems, recv_sems, local_sems = sems
        x, y, cc = _place()
        my = 4 * x + 2 * y + cc
        mine, copies = [], []
        for i in range(nw):
            mine.append(pltpu.make_async_copy(g_refs[i].at[my], recv_refs[i].at[my, layer], local_sems.at[i]))
            for k, (fx, fy, fc) in enumerate(FLIPS):
                px, py, pc = _flip(x, fx), _flip(y, fy), _flip(cc, fc)
                copies.append(pltpu.make_async_remote_copy(
                    src_ref=g_refs[i].at[4 * px + 2 * py + pc], dst_ref=recv_refs[i].at[my, layer],
                    send_sem=send_sems.at[k, i], recv_sem=recv_sems.at[k, i], device_id=(px, py, pc), device_id_type=MESH))
        return mine, copies

    def start(refs, recv_refs, sems):
        mine, copies = parts(refs, recv_refs, sems)
        for cp in mine + copies:
            cp.start()

    def finish(refs, recv_refs, sems):
        mine, copies = parts(refs, recv_refs, sems)
        for cp in copies:
            cp.wait_recv()
        for cp in copies:
            cp.wait_send()
        for cp in mine:
            cp.wait()

    ins, aliases = list(slots), {}
    for i, p in enumerate(prev):
        if p is not None:
            aliases[len(ins)] = i
            ins.append(p)
    out_shape = [jax.ShapeDtypeStruct((N_DEV, depth, *s.shape[1:]), s.dtype) for s in slots]
    return _Cargo(ins, out_shape, _comm_sems(nw), start, finish, on_done, aliases)


def _exchange(cargo, *, name):
    def body(*refs):
        c_in = len(cargo.ins)
        c_out = len(cargo.out_shape)
        cargo.start(refs[:c_in], refs[c_in:c_in + c_out], refs[c_in + c_out:])
        cargo.finish(refs[:c_in], refs[c_in:c_in + c_out], refs[c_in + c_out:])

    hbm = pl.BlockSpec(memory_space=pl.ANY)
    res = _pcall(
        body, name=name, in_specs=[hbm] * len(cargo.ins), out_specs=[hbm] * len(cargo.out_shape), out_shape=cargo.out_shape,
        scratch_shapes=cargo.sems, input_output_aliases=dict(cargo.aliases),
    )(*cargo.ins)
    cargo.on_done(list(res))


def _allreduce_small(blob, *, name):
    r, c = blob.shape

    def body(x_ref, out_ref, buf, send_sems, recv_sems):
        x, y, cc = _place()
        my = 4 * x + 2 * y + cc
        copies = []
        for k, (fx, fy, fc) in enumerate(FLIPS):
            peer = (_flip(x, fx), _flip(y, fy), _flip(cc, fc))
            copies.append(pltpu.make_async_remote_copy(
                src_ref=x_ref, dst_ref=buf.at[my], send_sem=send_sems.at[k], recv_sem=recv_sems.at[k],
                device_id=peer, device_id_type=MESH))
        for cp in copies:
            cp.start()
        buf[my] = x_ref[...]
        for cp in copies:
            cp.wait_recv()
        for cp in copies:
            cp.wait_send()
        acc = buf[0]
        for i in range(1, N_DEV):
            acc = acc + buf[i]
        out_ref[...] = acc

    vmem = pl.BlockSpec(memory_space=pltpu.VMEM)
    return _pcall(
        body, name=name, in_specs=[vmem], out_specs=vmem, out_shape=jax.ShapeDtypeStruct((r, c), F32),
        scratch_shapes=[pltpu.VMEM((N_DEV, r, c), F32), pltpu.SemaphoreType.DMA((7,)), pltpu.SemaphoreType.DMA((7,))],
    )(blob)


BIG = ("w_in", "w_mlp_in", "w_mlp_out", "w_up_fox", "w_up_sb", "w_up_dil", "w_out")
ROW_SHARDED = ("w_out", "w_mlp_out")
SMALL = ("attn_norm", "b_forget", "q_norm_fox", "k_norm_fox", "q_norm_dil", "k_norm_dil", "mlp_norm")
BLOB_ROWS = 512


def _pack(parts, dtype):
    flat = jnp.concatenate([p.reshape(-1).astype(dtype) for p in parts])
    size = -(-flat.shape[0] // (BLOB_ROWS * LANES)) * (BLOB_ROWS * LANES)
    return jnp.pad(flat, (0, size - flat.shape[0])).reshape(-1, LANES)


def _unpack(blob, shapes):
    flat = blob.reshape(-1)
    out, off = [], 0
    for shp in shapes:
        size = 1
        for s in shp:
            size *= s
        out.append(flat[off:off + size].reshape(shp))
        off += size
    return out


def _join_shards(name, sh):
    if name in ROW_SHARDED:
        return sh.reshape(-1, sh.shape[2])
    return jnp.transpose(sh, (1, 0, 2)).reshape(sh.shape[1], -1)


def _split_shards(name, full):
    a, b = full.shape
    if name in ROW_SHARDED:
        return full.reshape(N_DEV, a // N_DEV, b)
    return jnp.transpose(full.reshape(a, N_DEV, b // N_DEV), (1, 0, 2))


def _in_segments(d_in):
    o1 = 3 * W_FOX
    o2 = o1 + N_HEADS_FOX
    return (0, o1, 0), (o2, d_in, -N_HEADS_FOX), (o1, o2, d_in - o2)


def _join_w_in(sh, dp):
    b = sh.shape[2]
    pieces = []
    for s, e, _ in _in_segments(N_DEV * b):
        for j in range(s // b, (e - 1) // b + 1):
            pieces.append(sh[j, :, max(s, j * b) - j * b:min(e, (j + 1) * b) - j * b])
    pieces.append(jnp.zeros((sh.shape[1], dp - N_DEV * b), sh.dtype))
    return jnp.concatenate(pieces, axis=1)


def _split_w_in(gp, d_in):
    b = d_in // N_DEV
    shards = []
    for j in range(N_DEV):
        runs = []
        for s, e, shift in sorted(_in_segments(d_in)):
            lo, hi = max(s, j * b), min(e, (j + 1) * b)
            if lo < hi:
                runs.append(gp[:, lo + shift:hi + shift])
        shards.append(jnp.concatenate(runs, axis=1))
    return jnp.stack(shards)


def _stat_to_tokens(st, r, b):
    hh = st.shape[1]
    n = st.shape[2] * st.shape[4]
    return jnp.transpose(st.reshape(b, r, hh, n), (0, 2, 3, 1)).reshape(b, hh, n * r)


def _stat_to_streams(tok, r, blk):
    b, hh, t = tok.shape
    n = t // r
    return jnp.transpose(tok.reshape(b, hh, n, r), (0, 3, 1, 2)).reshape(b * r, hh, n // blk, 1, blk)


def _rope_tables(positions):
    half = HEAD_DIM // 2
    inv = 1.0 / (ROPE_THETA ** (jnp.arange(half, dtype=F32) / half))
    ang = positions.astype(F32)[..., None] * inv
    cos, sin = jnp.cos(ang), jnp.sin(ang)
    return jnp.tile(cos, (1, 1, 4)), jnp.tile(jnp.concatenate([-sin, sin], axis=-1), (1, 1, 2))


def _gain2(g):
    return jnp.tile(g.reshape(1, HEAD_DIM), (1, 2))


def _dil_offs(g):
    c0 = (P_DIL + g * W_DIL) // LANES
    return c0, c0 + W_DILQ // LANES, c0 + 2 * W_DILQ // LANES


def _layer_fwd(l, x, w, small, ropes, bl, t, cargo):
    n, d = x.shape
    s = {}
    s["x"] = x
    s["h"] = _rmsnorm_fwd(x, small["attn_norm"][l].reshape(1, d), name=f"norm_attn_fwd{l}")
    proj = _mm(s["h"], w["w_in"][l], tn=PROJ_TILE, name=f"mm_proj{l}")
    s["proj"] = proj
    dp = proj.shape[1]
    proj3 = proj.reshape(bl, t, dp)
    p_fg = P_GATE + 3 * d

    lg = jnp.transpose(proj3[:, :, p_fg:p_fg + N_HEADS_FOX], (0, 2, 1))
    s["lg"] = lg
    kb = _fox_gate_fwd(lg, small["b_forget"][l].reshape(N_HEADS_FOX, 1), name=f"fox_gate_fwd{l}")
    blk = _att_blk(t)
    kb5 = kb.reshape(bl, N_HEADS_FOX, t // blk, 1, blk)
    s["kb5"] = kb5
    gqf, gkf = _gain2(small["q_norm_fox"][l]), _gain2(small["k_norm_fox"][l])
    fo = P_FOX // LANES
    fox_offs = (fo, fo + W_FOX // LANES, fo + 2 * W_FOX // LANES)
    out_a, lse_a = _attn_fwd(proj3, fox_offs, N_HEADS_FOX // 2, gqf, gkf, kbias=kb5, window=t, name=f"fox_fwd{l}",
                             cargo=cargo.get("fox_fwd"))
    s["out_a"], s["lse_a"] = out_a, lse_a

    so = P_SB // LANES
    sb_offs = (so, so + W_SB // LANES, so + 2 * W_SB // LANES)
    out_b, lt_b = _sb_fwd(proj3, sb_offs, N_HEADS_SB // 2, name=f"sb_fwd{l}", cargo=cargo.get("sb_fwd"))
    s["out_b"], s["lt_b"] = out_b, lt_b

    gqd, gkd = _gain2(small["q_norm_dil"][l]), _gain2(small["k_norm_dil"][l])
    os_, lses = [], []
    for g, (window, r) in enumerate(DIL_PATTERNS):
        o_g, lse_g = _attn_fwd(proj3, _dil_offs(g), N_HEADS_DIL // 2, gqd, gkd, rope=ropes, window=window // r, stride=r,
                               name=f"dil_fwd{l}_{g}")
        os_.append(o_g.reshape(n, W_DIL))
        lses.append(_stat_to_tokens(lse_g, r, bl).reshape(bl * N_HEADS_DIL, t))
    lse_c, *ws = _dil_weights(lses, name=f"dil_weights{l}")
    ws = [jnp.repeat(jnp.transpose(wg.reshape(bl, N_HEADS_DIL, t), (0, 2, 1)).reshape(n, N_HEADS_DIL), HEAD_DIM, axis=1) for wg in ws]
    out_c = _dil_mix(os_, ws, name=f"dil_mix{l}")
    s["out_c"], s["lse_c"] = out_c, lse_c.reshape(bl, N_HEADS_DIL, t)

    ys = [_mm(out_a.reshape(n, W_FOX), w["w_up_fox"][l], name=f"mm_up_fox{l}"),
          _mm(out_b.reshape(n, W_SB), w["w_up_sb"][l], name=f"mm_up_sb{l}"),
          _mm(out_c, w["w_up_dil"][l], name=f"mm_up_dil{l}")]
    s["ys"] = ys
    s["merged"] = _gate_merge_fwd(proj, ys, name=f"gate_merge_fwd{l}")
    x1 = _mm(s["merged"], w["w_out"][l], add=x, name=f"mm_out{l}")
    s["x1"] = x1

    s["h2"] = _rmsnorm_fwd(x1, small["mlp_norm"][l].reshape(1, d), name=f"norm_mlp_fwd{l}")
    s["u"], s["a"] = _mm(s["h2"], w["w_mlp_in"][l], relu2=True, out_dtype=BF16, name=f"mm_mlp_in{l}")
    x2 = _mm(s["a"], w["w_mlp_out"][l], add=x1, name=f"mm_mlp_out{l}")
    return x2, s


def _layer_bwd(l, dx2, s, w, small, ropes, bl, t, hooks):
    n, d = dx2.shape
    gw, gs = {}, {}

    def cargo(call):
        return hooks[call](gw) if call in hooks else None
    du = _mm(dx2, w["w_mlp_out"][l], tb=True, relu_grad_of=s["u"], out_dtype=BF16, name=f"mm_du{l}")
    gw["w_mlp_out"] = _mm(s["a"], dx2, ta=True, name=f"mm_dw_mlp_out{l}")
    gw["w_mlp_in"] = _mm(s["h2"], du, ta=True, name=f"mm_dw_mlp_in{l}")
    dh2 = _mm(du, w["w_mlp_in"][l], tb=True, name=f"mm_dh2{l}")
    dx1, gs["mlp_norm"] = _rmsnorm_bwd(s["x1"], small["mlp_norm"][l].reshape(1, d), dh2, dx2, name=f"norm_mlp_bwd{l}")

    dmerged = _mm(dx1, w["w_out"][l], tb=True, name=f"mm_dmerged{l}")
    gw["w_out"] = _mm(s["merged"], dx1, ta=True, name=f"mm_dw_out{l}")
    dya, dyb, dyc, dgl0, dgl1, dgl2 = _gate_merge_bwd(s["proj"], s["ys"], dmerged, name=f"gate_merge_bwd{l}")
    out_a2, out_b2 = s["out_a"].reshape(n, W_FOX), s["out_b"].reshape(n, W_SB)
    gw["w_up_fox"] = _mm(out_a2, dya, ta=True, name=f"mm_dw_up_fox{l}")
    gw["w_up_sb"] = _mm(out_b2, dyb, ta=True, name=f"mm_dw_up_sb{l}")
    gw["w_up_dil"] = _mm(s["out_c"], dyc, ta=True, name=f"mm_dw_up_dil{l}")
    dout_a = _mm(dya, w["w_up_fox"][l], tb=True, name=f"mm_dout_a{l}").reshape(bl, t, W_FOX)
    dout_b = _mm(dyb, w["w_up_sb"][l], tb=True, name=f"mm_dout_b{l}").reshape(bl, t, W_SB)
    dout_c = _mm(dyc, w["w_up_dil"][l], tb=True, name=f"mm_dout_c{l}").reshape(bl, t, W_DIL)

    proj3 = s["proj"].reshape(bl, t, -1)
    gqf, gkf = _gain2(small["q_norm_fox"][l]), _gain2(small["k_norm_fox"][l])
    fo = P_FOX // LANES
    fox_offs = (fo, fo + W_FOX // LANES, fo + 2 * W_FOX // LANES)
    dq_a, dk_a, dv_a, dg_a, dkb5 = _attn_bwd(proj3, fox_offs, N_HEADS_FOX // 2, gqf, gkf, s["out_a"], dout_a, s["lse_a"],
                                             kbias=s["kb5"], window=t, name=f"fox_bwd{l}", cargo=cargo("fox_bwd"))
    gs["fox_gains"] = dg_a
    dlg, gs["b_forget"] = _fox_gate_bwd(dkb5.reshape(bl, N_HEADS_FOX, t), s["lg"], small["b_forget"][l].reshape(N_HEADS_FOX, 1),
                                        name=f"fox_gate_bwd{l}")
    so = P_SB // LANES
    sb_offs = (so, so + W_SB // LANES, so + 2 * W_SB // LANES)
    dq_b, dk_b, dv_b = _sb_bwd(proj3, sb_offs, N_HEADS_SB // 2, dout_b, s["lt_b"], name=f"sb_bwd{l}", cargo=cargo("sb_bwd"))
    gqd, gkd = _gain2(small["q_norm_dil"][l]), _gain2(small["k_norm_dil"][l])
    out_c3 = s["out_c"].reshape(bl, t, W_DIL)
    dqs, dks, dvs, dgd = [], [], [], None
    for g, (window, r) in enumerate(DIL_PATTERNS):
        lse_g = _stat_to_streams(s["lse_c"], r, _att_blocks(t // r)[0])
        dq_g, dk_g, dv_g, dg_g = _attn_bwd(proj3, _dil_offs(g), N_HEADS_DIL // 2, gqd, gkd, out_c3, dout_c, lse_g, rope=ropes,
                                           window=window // r, stride=r, name=f"dil_bwd{l}_{g}")
        dqs.append(dq_g)
        dks.append(dk_g)
        dvs.append(dv_g)
        dgd = dg_g if dgd is None else jnp.concatenate([dgd, dg_g], axis=0)
    gs["dil_gains"] = dgd

    dlg_cols = jnp.pad(jnp.transpose(dlg, (0, 2, 1)).reshape(n, N_HEADS_FOX), ((0, 0), (0, LANES - N_HEADS_FOX)))
    parts = [p.reshape(n, -1) for p in [dq_a, dk_a, dv_a, dq_b, dk_b, dv_b] + dqs + dks + dvs] + [dgl0, dgl1, dgl2, dlg_cols]
    dproj = _assemble_cols(parts, s["proj"].shape[1], name=f"assemble_dproj{l}")
    gw["w_in"] = _mm(s["h"], dproj, ta=True, tn=PROJ_TILE, name=f"mm_dw_in{l}")
    dh = _mm(dproj, w["w_in"][l], tb=True, tn=1024, tk=PROJ_TILE, name=f"mm_dh{l}", cargo=cargo("mm_dh"))
    dx, gs["attn_norm"] = _rmsnorm_bwd(s["x"], small["attn_norm"][l].reshape(1, d), dh, dx1, name=f"norm_attn_bwd{l}")
    return dx, gw, gs


def kernel(x, positions, attn_norm, w_in, b_forget, q_norm_fox, k_norm_fox, q_norm_dil, k_norm_dil, w_up_fox, w_up_sb, w_up_dil, w_out, mlp_norm, w_mlp_in, w_mlp_out, loss_target, m_attn_norm, m_w_in, m_b_forget, m_q_norm_fox, m_k_norm_fox, m_q_norm_dil, m_k_norm_dil, m_w_up_fox, m_w_up_sb, m_w_up_dil, m_w_out, m_mlp_norm, m_w_mlp_in, m_w_mlp_out, v_attn_norm, v_w_in, v_b_forget, v_q_norm_fox, v_k_norm_fox, v_q_norm_dil, v_k_norm_dil, v_w_up_fox, v_w_up_sb, v_w_up_dil, v_w_out, v_mlp_norm, v_w_mlp_in, v_w_mlp_out):
    bl, t, d = x.shape
    n = bl * t
    depth = attn_norm.shape[0]
    wl = dict(w_in=w_in, w_up_fox=w_up_fox, w_up_sb=w_up_sb, w_up_dil=w_up_dil, w_out=w_out, w_mlp_in=w_mlp_in, w_mlp_out=w_mlp_out)
    ml = dict(w_in=m_w_in, w_up_fox=m_w_up_fox, w_up_sb=m_w_up_sb, w_up_dil=m_w_up_dil, w_out=m_w_out, w_mlp_in=m_w_mlp_in, w_mlp_out=m_w_mlp_out)
    vl = dict(w_in=v_w_in, w_up_fox=v_w_up_fox, w_up_sb=v_w_up_sb, w_up_dil=v_w_up_dil, w_out=v_w_out, w_mlp_in=v_w_mlp_in, w_mlp_out=v_w_mlp_out)
    small = dict(attn_norm=attn_norm, b_forget=b_forget, q_norm_fox=q_norm_fox, k_norm_fox=k_norm_fox, q_norm_dil=q_norm_dil,
                 k_norm_dil=k_norm_dil, mlp_norm=mlp_norm)
    m_small = dict(attn_norm=m_attn_norm, b_forget=m_b_forget, q_norm_fox=m_q_norm_fox, k_norm_fox=m_k_norm_fox,
                   q_norm_dil=m_q_norm_dil, k_norm_dil=m_k_norm_dil, mlp_norm=m_mlp_norm)
    v_small = dict(attn_norm=v_attn_norm, b_forget=v_b_forget, q_norm_fox=v_q_norm_fox, k_norm_fox=v_k_norm_fox,
                   q_norm_dil=v_q_norm_dil, k_norm_dil=v_k_norm_dil, mlp_norm=v_mlp_norm)

    d_in = w_in.shape[-1] * N_DEV
    dp = -(-d_in // 512) * 512
    rest = [k for k in BIG if k != "w_in"]
    w = {k: [None] * depth for k in BIG}

    def gather(items):
        def done(res):
            for (k, l), sh in zip(items, res):
                w[k][l] = _join_w_in(sh, dp) if k == "w_in" else _join_shards(k, sh)

        return _gather_cargo([wl[k][l].astype(BF16) for k, l in items], done)

    _exchange(gather([("w_in", 0)]), name="gather_first")
    cos, sin = _rope_tables(positions)
    ropes = (cos, sin)

    h = x.reshape(n, d)
    saved = []
    for l in range(depth):
        cargo = {"fox_fwd": gather([(k, l) for k in rest])}
        if l + 1 < depth:
            cargo["sb_fwd"] = gather([("w_in", l + 1)])
        h, s = _layer_fwd(l, h, w, small, ropes, bl, t, cargo)
        saved.append(s)
    dy, loss_part = _loss_head(h, loss_target.reshape(n, d), name="loss_head")

    recv = {}

    def scatter(names, l, grads):
        def done(res):
            recv.update(zip(names, res))

        slots = [(_split_w_in(grads[k], d_in) if k == "w_in" else _split_shards(k, grads[k])).astype(BF16) for k in names]
        return _scatter_cargo(slots, [recv.get(k) for k in names], l, depth, done)

    gss = [None] * depth
    above = None
    for l in reversed(range(depth)):
        hooks = {"fox_bwd": lambda gw, l=l: scatter(rest, l, gw)}
        if above is not None:
            hooks["sb_bwd"] = lambda gw, l=l, g=above: scatter(["w_in"], l + 1, g)
        if l == 0:
            hooks["mm_dh"] = lambda gw: scatter(["w_in"], 0, gw)
        dy, above, gss[l] = _layer_bwd(l, dy, saved[l], w, small, ropes, bl, t, hooks)
    grad_x = dy.reshape(bl, t, d)

    g_big, d_big, m_big, v_big = {}, {}, {}, {}
    for k in BIG:
        g_big[k], d_big[k], m_big[k], v_big[k] = _adamw(recv[k], wl[k], ml[k], vl[k], name=f"adamw_{k}")

    rows = [loss_part]
    for l in range(depth):
        gs = gss[l]
        rows += [gs["attn_norm"].reshape(-1, LANES), gs["mlp_norm"].reshape(-1, LANES), gs["fox_gains"], gs["dil_gains"], gs["b_forget"]]
    row_counts = [r.shape[0] for r in rows]
    part = jnp.concatenate(rows, axis=0)
    pad_rows = -(-part.shape[0] // 8) * 8 - part.shape[0]
    summed = _allreduce_small(jnp.pad(part, ((0, pad_rows), (0, 0))), name="allreduce_small")
    pieces, off = [], 0
    for c in row_counts:
        pieces.append(summed[off:off + c])
        off += c
    loss = pieces[0][0, 0]

    def fold(row):
        return row[:HEAD_DIM] + row[HEAD_DIM:]

    g_small = {k: [] for k in SMALL}
    for l in range(depth):
        an, mn, fg, dg, bf = pieces[1 + 5 * l:6 + 5 * l]
        g_small["attn_norm"].append(an.reshape(d))
        g_small["mlp_norm"].append(mn.reshape(d))
        g_small["q_norm_fox"].append(fold(fg[0]))
        g_small["k_norm_fox"].append(fold(fg[1]))
        g_small["q_norm_dil"].append(fold(dg[0]) + fold(dg[8]) + fold(dg[16]))
        g_small["k_norm_dil"].append(fold(dg[1]) + fold(dg[9]) + fold(dg[17]))
        g_small["b_forget"].append(bf[:, 0])
    g_small = {k: jnp.stack(vs) for k, vs in g_small.items()}
    small_shapes = [small[k].shape for k in SMALL]
    outs = _adamw(_pack([g_small[k] for k in SMALL], F32)[None, None], _pack([small[k] for k in SMALL], F32)[None],
                  _pack([m_small[k] for k in SMALL], F32)[None], _pack([v_small[k] for k in SMALL], F32)[None], name="adamw_small")
    g_sm, d_sm, m_sm, v_sm = (dict(zip(SMALL, _unpack(o, small_shapes))) for o in outs)

    order = ("attn_norm", "w_in", "b_forget", "q_norm_fox", "k_norm_fox", "q_norm_dil", "k_norm_dil", "w_up_fox", "w_up_sb",
             "w_up_dil", "w_out", "mlp_norm", "w_mlp_in", "w_mlp_out")
    res = [loss, grad_x]
    for big, sm in ((g_big, g_sm), (d_big, d_sm), (m_big, m_sm), (v_big, v_sm)):
        res += [big[k] if k in big else sm[k] for k in order]
    return tuple(res)
```

```python
import jax
import jax.numpy as jnp
from jax import lax
from jax.experimental import pallas as pl
from jax.experimental.pallas import tpu as pltpu

F32 = jnp.float32
BF16 = jnp.bfloat16

HEAD_DIM = 64
LANES = 128
N_HEADS_FOX = 8
N_HEADS_SB = 8
N_HEADS_DIL = 4
DIL_PATTERNS = ((128, 1), (512, 4), (2048, 16))
ROPE_THETA = 10000.0
EPS = 1e-6
SCALE = 0.125
W_FOX = N_HEADS_FOX * HEAD_DIM
W_SB = N_HEADS_SB * HEAD_DIM
W_DIL = N_HEADS_DIL * HEAD_DIM
W_DILQ = len(DIL_PATTERNS) * W_DIL
P_FOX = 0
P_SB = 3 * W_FOX
P_DIL = P_SB + 3 * W_SB
P_GATE = P_DIL + 3 * W_DILQ
N_DEV = 8
ATT_BLK = 256
ATT_BQ = 512
NEG = -1e30
VMEM_LIMIT = 56 * 1024 * 1024
ADAMW_BLOCK_ELEMS = 128 * 1024
PROJ_TILE = 2176

ADAM_LR = 0.001
ADAM_B1 = 0.9
ADAM_B2 = 0.999
ADAM_EPS = 1e-08
ADAM_WD = 0.01
ADAM_STEP = 10

NT = (((1,), (1,)), ((), ()))
MESH = pl.DeviceIdType.MESH


def _pcall(body, **kw):
    return pl.pallas_call(body, **kw)


def _params(sem=None):
    return pltpu.CompilerParams(dimension_semantics=sem, vmem_limit_bytes=VMEM_LIMIT)


class _Cargo:
    def __init__(self, ins, out_shape, sems, start, finish, on_done, aliases=None):
        self.ins, self.out_shape, self.sems = list(ins), list(out_shape), list(sems)
        self.start, self.finish, self.on_done, self.aliases = start, finish, on_done, dict(aliases or {})


def _call(body, *, cargo=None, name, grid=(), in_specs, out_specs, out_shape, scratch_shapes=(), compiler_params=None):
    if cargo is None:
        kw = dict(grid=grid) if grid else {}
        if compiler_params is not None:
            kw["compiler_params"] = compiler_params
        return _pcall(body, name=name, in_specs=in_specs, out_specs=out_specs, out_shape=out_shape,
                      scratch_shapes=list(scratch_shapes), **kw)
    single = not isinstance(out_shape, (list, tuple))
    o_specs, o_shape = ([out_specs], [out_shape]) if single else (list(out_specs), list(out_shape))
    n_in, n_out, n_scr = len(in_specs), len(o_shape), len(scratch_shapes)
    c_in, c_out = len(cargo.ins), len(cargo.out_shape)

    def wrapped(*refs):
        ins, cins = refs[:n_in], refs[n_in:n_in + c_in]
        o0 = n_in + c_in
        outs, couts = refs[o0:o0 + n_out], refs[o0 + n_out:o0 + n_out + c_out]
        s0 = o0 + n_out + c_out
        scr, sems = refs[s0:s0 + n_scr], refs[s0 + n_scr:]
        first = last = None
        for ax, size in enumerate(grid):
            pid = pl.program_id(ax)
            first = (pid == 0) if first is None else first & (pid == 0)
            last = (pid == size - 1) if last is None else last & (pid == size - 1)
        if first is None:
            cargo.start(cins, couts, sems)
            body(*ins, *outs, *scr)
            cargo.finish(cins, couts, sems)
            return

        @pl.when(first)
        def _():
            cargo.start(cins, couts, sems)

        body(*ins, *outs, *scr)

        @pl.when(last)
        def _():
            cargo.finish(cins, couts, sems)

    hbm = pl.BlockSpec(memory_space=pl.ANY)
    kw = dict(grid=grid, compiler_params=_params(("arbitrary",) * len(grid))) if grid else {}
    call = _pcall(
        wrapped, name=name, in_specs=list(in_specs) + [hbm] * c_in, out_specs=o_specs + [hbm] * c_out,
        out_shape=o_shape + cargo.out_shape, scratch_shapes=list(scratch_shapes) + cargo.sems,
        input_output_aliases={n_in + i: n_out + j for i, j in cargo.aliases.items()}, **kw)

    def run(*args):
        res = call(*args, *cargo.ins)
        cargo.on_done(list(res[n_out:]))
        return res[0] if single else list(res[:n_out])

    return run


def _tile(dim, target, mult=LANES):
    t = (min(dim, target) // mult) * mult
    while t >= mult:
        if dim % t == 0:
            return t
        t -= mult
    return dim


def _mm(a, b, *, ta=False, tb=False, add=None, relu2=False, relu_grad_of=None, out_dtype=F32, name, tm=1024, tn=1024,
        tk=1024, cargo=None):
    m, k = (a.shape[1], a.shape[0]) if ta else a.shape
    n = b.shape[0] if tb else b.shape[1]
    tm, tn, tk = _tile(m, tm), _tile(n, tn), _tile(k, tk)
    nk = k // tk
    dn = (((0,) if ta else (1,), (1,) if tb else (0,)), ((), ()))

    extra = add if add is not None else relu_grad_of

    def body(*refs):
        a_ref, b_ref = refs[:2]
        x_ref = refs[2] if extra is not None else None
        outs = refs[2 + (extra is not None):-1]
        acc = refs[-1]
        kk = pl.program_id(2)
        part = lax.dot_general(a_ref[...].astype(BF16), b_ref[...].astype(BF16), dn, preferred_element_type=F32)

        def finish(r):
            if add is not None:
                r = r + x_ref[...]
            if relu_grad_of is not None:
                r = r * (2.0 * jnp.maximum(x_ref[...].astype(F32), 0.0))
            outs[0][...] = r.astype(out_dtype)
            if relu2:
                rr = jnp.maximum(r, 0.0)
                outs[1][...] = (rr * rr).astype(BF16)

        if nk == 1:
            finish(part)
            return

        @pl.when(kk == 0)
        def _():
            acc[...] = part

        @pl.when((kk > 0) & (kk < nk - 1))
        def _():
            acc[...] += part

        @pl.when(kk == nk - 1)
        def _():
            finish(acc[...] + part)

    a_spec = pl.BlockSpec((tk, tm), lambda i, j, q: (q, i)) if ta else pl.BlockSpec((tm, tk), lambda i, j, q: (i, q))
    b_spec = pl.BlockSpec((tn, tk), lambda i, j, q: (j, q)) if tb else pl.BlockSpec((tk, tn), lambda i, j, q: (q, j))
    o_spec = pl.BlockSpec((tm, tn), lambda i, j, q: (i, j))
    ins, specs = [a, b], [a_spec, b_spec]
    if extra is not None:
        ins.append(extra)
        specs.append(o_spec)
    sds = jax.ShapeDtypeStruct((m, n), out_dtype)
    return _call(
        body, cargo=cargo, name=name, grid=(m // tm, n // tn, nk), in_specs=specs,
        out_specs=[o_spec, o_spec] if relu2 else o_spec,
        out_shape=[sds, jax.ShapeDtypeStruct((m, n), BF16)] if relu2 else sds,
        scratch_shapes=[pltpu.VMEM((tm, tn) if nk > 1 else (8, LANES), F32)],
        compiler_params=_params(("parallel", "parallel", "arbitrary")),
    )(*ins)


def _rmsnorm_fwd(x, g, *, name):
    n, d = x.shape
    tm = _tile(n, 256, 8)

    def body(x_ref, g_ref, h_ref):
        xv = x_ref[...]
        inv = lax.rsqrt(jnp.mean(xv * xv, axis=1, keepdims=True) + EPS)
        h_ref[...] = (xv * inv * g_ref[...]).astype(BF16)

    row = pl.BlockSpec((tm, d), lambda i: (i, 0))
    return _pcall(
        body, name=name, grid=(n // tm,), in_specs=[row, pl.BlockSpec((1, d), lambda i: (0, 0))], out_specs=row,
        out_shape=jax.ShapeDtypeStruct((n, d), BF16), compiler_params=_params(("parallel",)),
    )(x, g)


def _rmsnorm_bwd(x, g, dh, dres, *, name):
    n, d = x.shape
    tm = _tile(n, 256, 8)

    def body(x_ref, g_ref, dh_ref, dres_ref, dx_ref, dg_ref):
        @pl.when(pl.program_id(0) == 0)
        def _():
            dg_ref[...] = jnp.zeros_like(dg_ref)

        xv = x_ref[...]
        inv = lax.rsqrt(jnp.mean(xv * xv, axis=1, keepdims=True) + EPS)
        y = xv * inv
        dhv = dh_ref[...]
        dg_ref[...] += jnp.sum(dhv * y, axis=0, keepdims=True)
        dy = dhv * g_ref[...]
        dx_ref[...] = dres_ref[...] + inv * (dy - y * jnp.mean(dy * y, axis=1, keepdims=True))

    row = pl.BlockSpec((tm, d), lambda i: (i, 0))
    vec = pl.BlockSpec((1, d), lambda i: (0, 0))
    return _pcall(
        body, name=name, grid=(n // tm,), in_specs=[row, vec, row, row], out_specs=[row, vec],
        out_shape=[jax.ShapeDtypeStruct((n, d), F32), jax.ShapeDtypeStruct((1, d), F32)],
        compiler_params=_params(("arbitrary",)),
    )(x, g, dh, dres)


def _gate_specs(n, d):
    bw = 256 if d % 256 == 0 else LANES
    tm = _tile(n, 512, 8)
    nb = d // bw
    yspec = pl.BlockSpec((tm, bw), lambda i, j: (i, j))
    gspecs = [pl.BlockSpec((tm, bw), lambda i, j, b=b: (i, P_GATE // bw + b * nb + j)) for b in range(3)]
    return tm, bw, nb, yspec, gspecs


def _gate_merge_fwd(proj, ys, *, name):
    n, d = ys[0].shape
    tm, bw, nb, yspec, gspecs = _gate_specs(n, d)

    def body(g0, g1, g2, y0, y1, y2, o_ref):
        acc = jax.nn.sigmoid(g0[...]) * y0[...]
        acc += jax.nn.sigmoid(g1[...]) * y1[...]
        acc += jax.nn.sigmoid(g2[...]) * y2[...]
        o_ref[...] = acc.astype(BF16)

    return _pcall(
        body, name=name, grid=(n // tm, nb), in_specs=gspecs + [yspec] * 3, out_specs=yspec,
        out_shape=jax.ShapeDtypeStruct((n, d), BF16), compiler_params=_params(("parallel", "parallel")),
    )(proj, proj, proj, *ys)


def _gate_merge_bwd(proj, ys, dmerged, *, name):
    n, d = ys[0].shape
    tm, bw, nb, yspec, gspecs = _gate_specs(n, d)

    def body(g0, g1, g2, y0, y1, y2, dm_ref, dy0, dy1, dy2, dgl0, dgl1, dgl2):
        dm = dm_ref[...]
        for g_ref, y_ref, dy_ref, dgl_ref in ((g0, y0, dy0, dgl0), (g1, y1, dy1, dgl1), (g2, y2, dy2, dgl2)):
            s = jax.nn.sigmoid(g_ref[...])
            dy_ref[...] = (dm * s).astype(BF16)
            dgl_ref[...] = (dm * y_ref[...] * s * (1.0 - s)).astype(BF16)

    sds = jax.ShapeDtypeStruct((n, d), BF16)
    return _pcall(
        body, name=name, grid=(n // tm, nb), in_specs=gspecs + [yspec] * 4, out_specs=[yspec] * 6,
        out_shape=[sds] * 6, compiler_params=_params(("parallel", "parallel")),
    )(proj, proj, proj, *ys, dmerged)


def _loss_head(y, tgt, *, name):
    n, d = y.shape
    tm = _tile(n, 256, 8)
    steps = n // tm

    def body(y_ref, t_ref, dy_ref, loss_ref, acc):
        i = pl.program_id(0)

        @pl.when(i == 0)
        def _():
            acc[...] = jnp.zeros_like(acc)

        e = y_ref[...] - t_ref[...]
        dy_ref[...] = e * (1.0 / d)
        acc[...] += jnp.sum(e * e, axis=0, keepdims=True)

        @pl.when(i == steps - 1)
        def _():
            tot = jnp.sum(acc[...], axis=1, keepdims=True) * (0.5 / d)
            loss_ref[...] = jnp.broadcast_to(tot, loss_ref.shape)

    row = pl.BlockSpec((tm, d), lambda i: (i, 0))
    return _pcall(
        body, name=name, grid=(steps,), in_specs=[row, row], out_specs=[row, pl.BlockSpec((8, LANES), lambda i: (0, 0))],
        out_shape=[jax.ShapeDtypeStruct((n, d), F32), jax.ShapeDtypeStruct((8, LANES), F32)],
        scratch_shapes=[pltpu.VMEM((1, d), F32)], compiler_params=_params(("arbitrary",)),
    )(y, tgt)


def _assemble_cols(parts, width, *, name):
    n = parts[0].shape[0]
    tm = _tile(n, 256, 16)
    widths = [p.shape[1] for p in parts]

    def body(*refs):
        o_ref = refs[-1]
        off = 0
        for ref, w in zip(refs[:-1], widths):
            o_ref[:, off:off + w] = ref[...].astype(BF16)
            off += w
        if off < width:
            o_ref[:, off:] = jnp.zeros((tm, width - off), BF16)

    return _pcall(
        body, name=name, grid=(n // tm,), in_specs=[pl.BlockSpec((tm, w), lambda i: (i, 0)) for w in widths],
        out_specs=pl.BlockSpec((tm, width), lambda i: (i, 0)), out_shape=jax.ShapeDtypeStruct((n, width), BF16),
        compiler_params=_params(("parallel",)),
    )(*parts)


def _dil_weights(lses, *, name):
    shp = lses[0].shape

    def body(l0, l1, l2, lse_ref, w0, w1, w2):
        a, b, c = l0[...], l1[...], l2[...]
        m = jnp.maximum(jnp.maximum(a, b), c)
        ea, eb, ec = jnp.exp(a - m), jnp.exp(b - m), jnp.exp(c - m)
        den = ea + eb + ec
        lse_ref[...] = m + jnp.log(den)
        w0[...] = ea / den
        w1[...] = eb / den
        w2[...] = ec / den

    vmem = pl.BlockSpec(memory_space=pltpu.VMEM)
    return _pcall(body, name=name, in_specs=[vmem] * 3, out_specs=[vmem] * 4, out_shape=[jax.ShapeDtypeStruct(shp, F32)] * 4)(*lses)


def _dil_mix(os_, ws, *, name):
    n, w = os_[0].shape
    tm = _tile(n, 512, 8)

    def body(o0, o1, o2, w0, w1, w2, out_ref):
        out_ref[...] = w0[...] * o0[...] + w1[...] * o1[...] + w2[...] * o2[...]

    spec = pl.BlockSpec((tm, w), lambda i: (i, 0))
    return _pcall(
        body, name=name, grid=(n // tm,), in_specs=[spec] * 6, out_specs=spec, out_shape=jax.ShapeDtypeStruct((n, w), F32),
        compiler_params=_params(("parallel",)),
    )(*os_, *ws)


def _adamw(gsrc, w, m, v, *, name):
    s, dep, a, b = gsrc.shape
    ta = _tile(a, max(16, (ADAMW_BLOCK_ELEMS // b) // 16 * 16), 16)
    c1 = 1.0 / (1.0 - ADAM_B1 ** ADAM_STEP)
    c2 = 1.0 / (1.0 - ADAM_B2 ** ADAM_STEP)

    def body(gs_ref, w_ref, m_ref, v_ref, g_ref, d_ref, m2_ref, v2_ref):
        g = gs_ref[0].astype(F32)
        for i in range(1, s):
            g = g + gs_ref[i].astype(F32)
        m2 = ADAM_B1 * m_ref[...] + (1.0 - ADAM_B1) * g
        v2 = ADAM_B2 * v_ref[...] + (1.0 - ADAM_B2) * (g * g)
        g_ref[...] = g
        m2_ref[...] = m2
        v2_ref[...] = v2
        d_ref[...] = -ADAM_LR * ((m2 * c1) / (jnp.sqrt(v2 * c2) + ADAM_EPS) + ADAM_WD * w_ref[...])

    spec = pl.BlockSpec((None, ta, b), lambda l, i: (l, i, 0))
    sds = jax.ShapeDtypeStruct((dep, a, b), F32)
    return _pcall(
        body, name=name, grid=(dep, a // ta),
        in_specs=[pl.BlockSpec((s, None, ta, b), lambda l, i: (0, l, i, 0)), spec, spec, spec],
        out_specs=[spec] * 4, out_shape=[sds] * 4, compiler_params=_params(("parallel", "parallel")),
    )(gsrc, w, m, v)


def _mask_a():
    return lax.broadcasted_iota(jnp.int32, (1, LANES), 1) < HEAD_DIM


def _half_sum(x, m_a):
    sa = jnp.sum(jnp.where(m_a, x, 0.0), axis=1, keepdims=True)
    sb = jnp.sum(jnp.where(m_a, 0.0, x), axis=1, keepdims=True)
    return jnp.where(m_a, sa, sb)


def _head_inv(x, m_a):
    return lax.rsqrt(_half_sum(x * x, m_a) * (1.0 / HEAD_DIM) + EPS)


def _swap32(x):
    first = (lax.broadcasted_iota(jnp.int32, (1, LANES), 1) % HEAD_DIM) < (HEAD_DIM // 2)
    return jnp.where(first, pltpu.roll(x, LANES - HEAD_DIM // 2, 1), pltpu.roll(x, HEAD_DIM // 2, 1))


def _tri(blk, rel):
    r = lax.broadcasted_iota(jnp.int32, (blk, blk), 0)
    c = lax.broadcasted_iota(jnp.int32, (blk, blk), 1)
    return jnp.where(rel(r, c), 1.0, 0.0).astype(BF16)


def _cumdot(x, u, parts):
    acc = None
    r = x
    for i in range(parts):
        xi = r.astype(BF16)
        t = jnp.dot(xi, u, preferred_element_type=F32)
        acc = t if acc is None else acc + t
        if i + 1 < parts:
            r = r - xi.astype(F32)
    return acc


def _rows(i, blk):
    return pl.ds(pl.multiple_of(i * blk, blk), blk)


def _col_spec(n, off):
    return pl.BlockSpec((None, n, LANES), lambda z, p, off=off: (z, 0, off + p))


def _att_blk(n):
    return ATT_BLK if n % ATT_BLK == 0 else min(LANES, n)


def _att_blocks(n, wide_keys=False):
    bk = _att_blk(n)
    bq = ATT_BQ if n % ATT_BQ == 0 else bk
    return bq, (bq if wide_keys else bk)


def _loop(lo, hi, fn):
    def it(i, c):
        fn(i)
        return c

    lax.fori_loop(lo, hi, it, 0)


def _normed(src, g_ref, rope_refs, rows, m_a):
    xv = src[rows, :]
    xn = xv * _head_inv(xv, m_a) * g_ref[...]
    if rope_refs is not None:
        xn = xn * rope_refs[0][rows, :] + _swap32(xn) * rope_refs[1][rows, :]
    return xn


def _bias_lane(h):
    return HEAD_DIM if h == 0 else 0


def _k_for_head(kn, kb_row, h, m_h, lane, blk):
    out = jnp.where(m_h, kn, 0.0)
    if kb_row is not None:
        col = jnp.transpose(jnp.broadcast_to(kb_row, (LANES, blk)))
        hi = col.astype(BF16).astype(F32)
        mid = (col - hi).astype(BF16).astype(F32)
        lo = col - hi - mid
        b = _bias_lane(h)
        out = jnp.where(lane == b, hi, jnp.where(lane == b + 1, mid, jnp.where(lane == b + 2, lo, out)))
    return out.astype(BF16)


def _q_for_head(qb, h, m_h, lane, biased):
    out = jnp.where(m_h, qb, 0)
    if biased:
        b = _bias_lane(h)
        out = jnp.where((lane >= b) & (lane < b + 3), jnp.ones_like(out), out)
    return out


def _head_rows(x, parts=3):
    rr = lax.broadcasted_iota(jnp.int32, (8, LANES), 0)
    ll = lax.broadcasted_iota(jnp.int32, (8, LANES), 1)
    sel = jnp.where(((rr == 0) & (ll < HEAD_DIM)) | ((rr == 1) & (ll >= HEAD_DIM)), 1.0, 0.0).astype(BF16)
    acc = None
    rem = x
    for i in range(parts):
        xi = rem.astype(BF16)
        t = lax.dot_general(sel, xi, NT, preferred_element_type=F32)
        acc = t if acc is None else acc + t
        if i + 1 < parts:
            rem = rem - xi.astype(F32)
    return acc


def _cumdot_left(u, x, parts):
    acc = None
    rem = x
    for i in range(parts):
        xi = rem.astype(BF16)
        t = jnp.dot(u, xi, preferred_element_type=F32)
        acc = t if acc is None else acc + t
        if i + 1 < parts:
            rem = rem - xi.astype(F32)
    return acc


def _q_minus_k(bk, bq):
    return lax.broadcasted_iota(jnp.int32, (bk, bq), 1) - lax.broadcasted_iota(jnp.int32, (bk, bq), 0)


def _stat_spec(nb, blk):
    return pl.BlockSpec((None, 2, nb, 1, blk), lambda z, p: (z, p, 0, 0, 0))


def _stream_rows(stride):
    if stride == 1:
        return _rows
    c = pl.program_id(2)
    return lambda i, blk: pl.ds(c + i * (blk * stride), blk, stride=stride)


def _attn_specs(t, npairs, stride, nq, bq):
    col = lambda off: pl.BlockSpec((None, t, LANES), lambda z, p, c: (z, 0, off + p))
    vec = pl.BlockSpec((1, LANES), lambda z, p, c: (0, 0))
    seq = pl.BlockSpec((None, t, LANES), lambda z, p, c: (z, 0, 0))
    stat = pl.BlockSpec((None, 2, nq, 1, bq), lambda z, p, c: (z * stride + c, p, 0, 0, 0))
    return col, vec, seq, stat


def _attn_fwd(src, offs, npairs, gq, gk, *, rope=None, kbias=None, window, stride=1, name, cargo=None):
    bs, t, _ = src.shape
    n = t // stride
    full = window >= n
    bq, bk = _att_blocks(n, wide_keys=full)
    nq, nk, rq = n // bq, n // bk, bq // bk
    wblk = -(-window // bk)
    biased = kbias is not None

    def body(*refs):
        it = iter(refs)
        q_ref, k_ref, v_ref, gq_ref, gk_ref = (next(it) for _ in range(5))
        rope_refs = (next(it), next(it)) if rope is not None else None
        kb_ref = next(it) if biased else None
        o_ref, lse_ref, qn_s, kh_s, vt_s, acc_s, m_s = (next(it) for _ in range(7))
        m_a = _mask_a()
        masks = (m_a, jnp.logical_not(m_a))
        lane = lax.broadcasted_iota(jnp.int32, (1, LANES), 1)
        row = lax.broadcasted_iota(jnp.int32, (LANES, 1), 0)
        tok = _stream_rows(stride)

        def prep(c):
            rows = _rows(c, bk)
            trows = tok(c, bk)
            qn_s[rows, :] = (_normed(q_ref, gq_ref, rope_refs, trows, m_a) * SCALE).astype(BF16)
            kn = _normed(k_ref, gk_ref, rope_refs, trows, m_a)
            vt = jnp.transpose(v_ref[trows, :])
            for h in (0, 1):
                kh_s[h, rows, :] = _k_for_head(kn, kb_ref[h, c] if biased else None, h, masks[h], lane, bk)
                vt_s[h, c] = jnp.where(row == _bias_lane(h), 1.0, vt).astype(BF16)

        _loop(0, nk, prep)
        qk = _q_minus_k(bk, bq)

        def qblock(qi):
            rows = _rows(qi, bq)
            qb = qn_s[rows, :]
            qh = [_q_for_head(qb, h, masks[h], lane, biased) for h in (0, 1)]
            m_s[...] = jnp.full(m_s.shape, NEG, F32)
            acc_s[...] = jnp.zeros_like(acc_s)

            def step(kj, masked):
                cols = _rows(kj, bk)
                sts = [lax.dot_general(kh_s[h, cols, :], qh[h], NT, preferred_element_type=F32) for h in (0, 1)]
                old = [(m_s[h], acc_s[h]) for h in (0, 1)]
                if masked:
                    d = qk + (qi * bq - kj * bk)
                    ok = (d >= 0) & (d <= window)
                    sts = [jnp.where(ok, st, NEG) for st in sts]
                new = []
                for h in (0, 1):
                    m, acc = old[h]
                    m2 = jnp.maximum(m, jnp.max(sts[h], axis=0, keepdims=True))
                    pt = jnp.exp(sts[h] - m2).astype(BF16)
                    new.append((m2, jnp.exp(m - m2) * acc + jnp.dot(vt_s[h, kj], pt, preferred_element_type=F32)))
                for h in (0, 1):
                    m_s[h], acc_s[h] = new[h]

            if full:
                _loop(0, qi * rq, lambda kj: step(kj, False))
                _loop(qi * rq, (qi + 1) * rq, lambda kj: step(kj, True))
            else:
                _loop(jnp.maximum(qi * rq - wblk, 0), (qi + 1) * rq, lambda kj: step(kj, True))
            outs = []
            for h in (0, 1):
                acc_t = acc_s[h]
                den = acc_t[_bias_lane(h):_bias_lane(h) + 1, :]
                outs.append(jnp.transpose(acc_t / den))
                lse_ref[h, qi] = m_s[h] + jnp.log(den)
            o_ref[tok(qi, bq), :] = jnp.where(m_a, outs[0], outs[1])

        _loop(0, nq, qblock)

    col, vec, seq, stat = _attn_specs(t, npairs, stride, nq, bq)
    ins = [src, src, src, gq, gk]
    specs = [col(offs[0]), col(offs[1]), col(offs[2]), vec, vec]
    if rope is not None:
        ins += list(rope)
        specs += [seq, seq]
    if biased:
        ins.append(kbias)
        specs.append(pl.BlockSpec((None, 2, nk, 1, bk), lambda z, p, c: (z, p, 0, 0, 0)))
    scratch = [pltpu.VMEM((n, LANES), BF16), pltpu.VMEM((2, n, LANES), BF16), pltpu.VMEM((2, nk, LANES, bk), BF16)]
    scratch += [pltpu.VMEM((2, LANES, bq), F32), pltpu.VMEM((2, 1, bq), F32)]
    return _call(
        body, cargo=cargo, name=name, grid=(bs, npairs, stride), in_specs=specs, out_specs=[col(0), stat],
        out_shape=[jax.ShapeDtypeStruct((bs, t, LANES * npairs), F32),
                   jax.ShapeDtypeStruct((bs * stride, 2 * npairs, nq, 1, bq), F32)],
        scratch_shapes=scratch, compiler_params=_params(("parallel", "parallel", "arbitrary")),
    )(*ins)


def _attn_bwd(src, offs, npairs, gq, gk, o, do, lse, *, rope=None, kbias=None, window, stride=1, name, cargo=None):
    bs, t, _ = src.shape
    n = t // stride
    full = window >= n
    bq, bk = _att_blocks(n, wide_keys=full)
    nq, nk, rq = n // bq, n // bk, bq // bk
    wblk = -(-window // bk)
    biased = kbias is not None
    gdt = BF16 if stride == 1 else F32

    def body(*refs):
        it = iter(refs)
        q_ref, k_ref, v_ref, gq_ref, gk_ref, o_ref, do_ref, lse_ref = (next(it) for _ in range(8))
        rope_refs = (next(it), next(it)) if rope is not None else None
        kb_ref = next(it) if biased else None
        dq_ref, dk_ref, dv_ref, dg_ref = (next(it) for _ in range(4))
        dkb_ref = next(it) if biased else None
        qn_s, kh_s, vb_s, kt_s, dqn_s, dkh_s, dv_s, dq_s, rs_s = (next(it) for _ in range(9))
        m_a = _mask_a()
        masks = (m_a, jnp.logical_not(m_a))
        lane = lax.broadcasted_iota(jnp.int32, (1, LANES), 1)
        tok = _stream_rows(stride)

        @pl.when((pl.program_id(0) == 0) & (pl.program_id(1) == 0) & (pl.program_id(2) == 0))
        def _():
            dg_ref[...] = jnp.zeros_like(dg_ref)

        def prep(c):
            rows = _rows(c, bk)
            trows = tok(c, bk)
            qn_s[rows, :] = (_normed(q_ref, gq_ref, rope_refs, trows, m_a) * SCALE).astype(BF16)
            kn = _normed(k_ref, gk_ref, rope_refs, trows, m_a)
            kt_s[c] = jnp.transpose(kn).astype(BF16)
            vb_s[rows, :] = v_ref[trows, :].astype(BF16)
            for h in (0, 1):
                kh_s[h, rows, :] = _k_for_head(kn, kb_ref[h, c] if biased else None, h, masks[h], lane, bk)

        _loop(0, nk, prep)
        dkh_s[...] = jnp.zeros_like(dkh_s)
        dv_s[...] = jnp.zeros_like(dv_s)
        qk = _q_minus_k(bk, bq)

        def qblock(qi):
            rows = _rows(qi, bq)
            qb = qn_s[rows, :]
            trows = tok(qi, bq)
            dob = do_ref[trows, :]
            delta = _head_rows(dob * o_ref[trows, :])
            qh = [_q_for_head(qb, h, masks[h], lane, biased) for h in (0, 1)]
            doms = [jnp.where(masks[h], dob, 0.0).astype(BF16) for h in (0, 1)]
            lses = [lse_ref[h, qi] for h in (0, 1)]
            dq_s[...] = jnp.zeros_like(dq_s)
            if biased:
                for h in (0, 1):
                    rs_s[h, qi] = jnp.zeros((1, bq), F32)

            def step(kj, masked):
                cols = _rows(kj, bk)
                vb = vb_s[cols, :]
                kt = kt_s[kj]
                sts = [lax.dot_general(kh_s[h, cols, :], qh[h], NT, preferred_element_type=F32) for h in (0, 1)]
                dpts = [lax.dot_general(vb, doms[h], NT, preferred_element_type=F32) for h in (0, 1)]
                if masked:
                    d = qk + (qi * bq - kj * bk)
                    ok = (d >= 0) & (d <= window)
                    sts = [jnp.where(ok, st, NEG) for st in sts]
                new = []
                for h in (0, 1):
                    pt = jnp.exp(sts[h] - lses[h])
                    dst = pt * (dpts[h] - delta[h:h + 1, :])
                    dsb = dst.astype(BF16)
                    tk = jnp.dot(dsb, qh[h], preferred_element_type=F32)
                    if biased:
                        tk = tk + jnp.dot((dst - dsb.astype(F32)).astype(BF16), qh[h], preferred_element_type=F32)
                    tv = jnp.dot(pt.astype(BF16), doms[h], preferred_element_type=F32)
                    tq = jnp.dot(kt, dsb, preferred_element_type=F32)
                    new.append((tk, tv, tq, jnp.sum(dst, axis=0, keepdims=True) if biased else None))
                for h in (0, 1):
                    dkh_s[h, cols, :] += new[h][0]
                    dq_s[h] += new[h][2]
                    if biased:
                        rs_s[h, qi] += new[h][3]
                dv_s[cols, :] += new[0][1] + new[1][1]

            if full:
                _loop(0, qi * rq, lambda kj: step(kj, False))
                _loop(qi * rq, (qi + 1) * rq, lambda kj: step(kj, True))
            else:
                _loop(jnp.maximum(qi * rq - wblk, 0), (qi + 1) * rq, lambda kj: step(kj, True))
            dqn_s[rows, :] = jnp.where(m_a, jnp.transpose(dq_s[0]), jnp.transpose(dq_s[1])) * SCALE

        _loop(0, nq, qblock)

        def finish(c, carry):
            rows = _rows(c, bq)
            out = []
            dk_pair = [dkh_s[0, rows, :], dkh_s[1, rows, :]]
            if biased:
                for h in (0, 1):
                    b = _bias_lane(h)
                    dkb_row = jnp.transpose(dk_pair[h])[b:b + 1, :] - rs_s[h, c]
                    for j in range(rq):
                        dkb_ref[h, c * rq + j] = dkb_row[:, j * bk:(j + 1) * bk]
            trows = tok(c, bq)
            dv_ref[trows, :] = dv_s[rows, :].astype(gdt)
            grads = (dqn_s[rows, :], jnp.where(m_a, dk_pair[0], dk_pair[1]))
            for src_ref, g_ref, dxn, dst in ((q_ref, gq_ref, grads[0], dq_ref), (k_ref, gk_ref, grads[1], dk_ref)):
                xv = src_ref[trows, :]
                inv = _head_inv(xv, m_a)
                y = xv * inv
                if rope_refs is not None:
                    dxn = dxn * rope_refs[0][trows, :] + _swap32(dxn * rope_refs[1][trows, :])
                dy = dxn * g_ref[...]
                dst[trows, :] = (inv * (dy - y * (_half_sum(dy * y, m_a) * (1.0 / HEAD_DIM)))).astype(gdt)
                out.append(jnp.sum(dxn * y, axis=0, keepdims=True))
            return carry[0] + out[0], carry[1] + out[1]

        zero = jnp.zeros((1, LANES), F32)
        dgq, dgk = lax.fori_loop(0, nq, finish, (zero, zero))
        dg_ref[0:1, :] += dgq
        dg_ref[1:2, :] += dgk

    col, vec, seq, stat = _attn_specs(t, npairs, stride, nq, bq)
    ospec = col(0)
    ins = [src, src, src, gq, gk, o, do, lse]
    specs = [col(offs[0]), col(offs[1]), col(offs[2]), vec, vec, ospec, ospec, stat]
    if rope is not None:
        ins += list(rope)
        specs += [seq, seq]
    sds = jax.ShapeDtypeStruct((bs, t, LANES * npairs), gdt)
    out_shape = [sds, sds, sds, jax.ShapeDtypeStruct((8, LANES), F32)]
    out_specs = [ospec, ospec, ospec, pl.BlockSpec((8, LANES), lambda z, p, c: (0, 0))]
    if biased:
        kbspec = pl.BlockSpec((None, 2, nk, 1, bk), lambda z, p, c: (z, p, 0, 0, 0))
        ins.append(kbias)
        specs.append(kbspec)
        out_shape.append(jax.ShapeDtypeStruct(kbias.shape, F32))
        out_specs.append(kbspec)
    scratch = [pltpu.VMEM((n, LANES), BF16), pltpu.VMEM((2, n, LANES), BF16), pltpu.VMEM((n, LANES), BF16)]
    scratch += [pltpu.VMEM((nk, LANES, bk), BF16), pltpu.VMEM((n, LANES), F32), pltpu.VMEM((2, n, LANES), F32)]
    scratch += [pltpu.VMEM((n, LANES), F32), pltpu.VMEM((2, LANES, bq), F32), pltpu.VMEM((2, nq, 1, bq), F32)]
    return _call(
        body, cargo=cargo, name=name, grid=(bs, npairs, stride), in_specs=specs, out_specs=out_specs, out_shape=out_shape,
        scratch_shapes=scratch, compiler_params=_params(("arbitrary", "arbitrary", "arbitrary")),
    )(*ins)


SB_LOG_PARTS = 2
SB_GRAD_PARTS = 1


def _log_sig_pair(z):
    lsn = jnp.minimum(-z, 0.0) - jnp.log(1.0 + jnp.exp(-jnp.abs(z)))
    return lsn, z + lsn


def _sb_fwd(src, offs, npairs, *, name, cargo=None):
    zs, n, _ = src.shape
    bq, bk = _att_blocks(n)
    nq, nk, rq = n // bq, n // bk, bq // bk

    def body(q_ref, k_ref, v_ref, o_ref, lt_ref, qs_s, kb_s, vt_s, acc_s, c_s):
        m_a = _mask_a()
        masks = (m_a, jnp.logical_not(m_a))

        def prep(c):
            rows = _rows(c, bk)
            qs_s[rows, :] = (q_ref[rows, :] * SCALE).astype(BF16)
            kb_s[rows, :] = k_ref[rows, :].astype(BF16)
            vt_s[c] = jnp.transpose(v_ref[rows, :]).astype(BF16)

        _loop(0, nk, prep)
        qk = _q_minus_k(bk, bq)
        u_gt = _tri(bk, lambda r, c: c > r)

        def qblock(qi):
            rows = _rows(qi, bq)
            qb = qs_s[rows, :]
            qms = [jnp.where(masks[h], qb, 0) for h in (0, 1)]
            acc_s[...] = jnp.zeros_like(acc_s)
            c_s[...] = jnp.zeros_like(c_s)

            def step(kj, masked):
                kb, vt = kb_s[_rows(kj, bk), :], vt_s[kj]
                zts = [lax.dot_general(kb, qms[h], NT, preferred_element_type=F32) for h in (0, 1)]
                old = [c_s[h] for h in (0, 1)]
                if masked:
                    ok = (qk + (qi * bq - kj * bk)) > 0
                new = []
                for h in (0, 1):
                    lsn, lsp = _log_sig_pair(zts[h])
                    if masked:
                        lsn = jnp.where(ok, lsn, 0.0)
                    at = jnp.exp(lsp + (old[h] + _cumdot_left(u_gt, lsn, SB_LOG_PARTS)))
                    if masked:
                        at = jnp.where(ok, at, 0.0)
                    new.append((jnp.dot(vt, at.astype(BF16), preferred_element_type=F32),
                                old[h] + jnp.sum(lsn, axis=0, keepdims=True)))
                for h in (0, 1):
                    acc_s[h] += new[h][0]
                    c_s[h] = new[h][1]

            _loop(0, rq, lambda t: step((qi + 1) * rq - 1 - t, True))
            _loop(0, qi * rq, lambda t: step(qi * rq - 1 - t, False))
            o_ref[rows, :] = jnp.where(m_a, jnp.transpose(acc_s[0]), jnp.transpose(acc_s[1]))
            for h in (0, 1):
                lt_ref[h, qi] = c_s[h]

        _loop(0, nq, qblock)

    scratch = [pltpu.VMEM((n, LANES), BF16)] * 2 + [pltpu.VMEM((nk, LANES, bk), BF16)]
    scratch += [pltpu.VMEM((2, LANES, bq), F32), pltpu.VMEM((2, 1, bq), F32)]
    return _call(
        body, cargo=cargo, name=name, grid=(zs, npairs),
        in_specs=[_col_spec(n, offs[0]), _col_spec(n, offs[1]), _col_spec(n, offs[2])],
        out_specs=[_col_spec(n, 0), _stat_spec(nq, bq)],
        out_shape=[jax.ShapeDtypeStruct((zs, n, LANES * npairs), F32), jax.ShapeDtypeStruct((zs, 2 * npairs, nq, 1, bq), F32)],
        scratch_shapes=scratch, compiler_params=_params(("parallel", "parallel")),
    )(src, src, src)


def _sb_bwd(src, offs, npairs, do, ltot, *, name, cargo=None):
    zs, n, _ = src.shape
    bq, bk = _att_blocks(n)
    nq, nk, rq = n // bq, n // bk, bq // bk

    def body(q_ref, k_ref, v_ref, do_ref, lt_ref, dq_ref, dk_ref, dv_ref, qs_s, kb_s, vb_s, kt_s, dk_s, dv_s, dq_s, lp_s, ep_s):
        m_a = _mask_a()
        masks = (m_a, jnp.logical_not(m_a))

        def prep(c):
            rows = _rows(c, bk)
            qs_s[rows, :] = (q_ref[rows, :] * SCALE).astype(BF16)
            kv = k_ref[rows, :]
            kb_s[rows, :] = kv.astype(BF16)
            kt_s[c] = jnp.transpose(kv).astype(BF16)
            vb_s[rows, :] = v_ref[rows, :].astype(BF16)

        _loop(0, nk, prep)
        dk_s[...] = jnp.zeros_like(dk_s)
        dv_s[...] = jnp.zeros_like(dv_s)
        qk = _q_minus_k(bk, bq)
        u_le = _tri(bk, lambda r, c: c <= r)
        u_lt = _tri(bk, lambda r, c: c < r)

        def qblock(qi):
            rows = _rows(qi, bq)
            qb = qs_s[rows, :]
            dob = do_ref[rows, :]
            qms = [jnp.where(masks[h], qb, 0) for h in (0, 1)]
            doms = [jnp.where(masks[h], dob, 0.0).astype(BF16) for h in (0, 1)]
            lts = [lt_ref[h, qi] for h in (0, 1)]
            dq_s[...] = jnp.zeros_like(dq_s)
            lp_s[...] = jnp.zeros_like(lp_s)
            ep_s[...] = jnp.zeros_like(ep_s)

            def step(kj, masked):
                cols = _rows(kj, bk)
                kb, vb, kt = kb_s[cols, :], vb_s[cols, :], kt_s[kj]
                zts = [lax.dot_general(kb, qms[h], NT, preferred_element_type=F32) for h in (0, 1)]
                dats = [lax.dot_general(vb, doms[h], NT, preferred_element_type=F32) for h in (0, 1)]
                old = [(lp_s[h], ep_s[h]) for h in (0, 1)]
                if masked:
                    ok = (qk + (qi * bq - kj * bk)) > 0
                new = []
                for h in (0, 1):
                    lp, ep = old[h]
                    lsn, lsp = _log_sig_pair(zts[h])
                    sig = jnp.exp(lsp)
                    if masked:
                        lsn = jnp.where(ok, lsn, 0.0)
                    at = jnp.exp(lsp + (lts[h] - (lp + _cumdot_left(u_le, lsn, SB_LOG_PARTS))))
                    if masked:
                        at = jnp.where(ok, at, 0.0)
                    et = dats[h] * at
                    big_e = ep + _cumdot_left(u_lt, et, SB_GRAD_PARTS)
                    dzt = et - sig * (et + big_e)
                    if masked:
                        dzt = jnp.where(ok, dzt, 0.0)
                    dzb = dzt.astype(BF16)
                    new.append((jnp.dot(dzb, qms[h], preferred_element_type=F32),
                                jnp.dot(at.astype(BF16), doms[h], preferred_element_type=F32),
                                jnp.dot(kt, dzb, preferred_element_type=F32),
                                lp + jnp.sum(lsn, axis=0, keepdims=True), ep + jnp.sum(et, axis=0, keepdims=True)))
                for h in (0, 1):
                    dq_s[h] += new[h][2]
                    lp_s[h], ep_s[h] = new[h][3], new[h][4]
                dk_s[cols, :] += new[0][0] + new[1][0]
                dv_s[cols, :] += new[0][1] + new[1][1]

            _loop(0, qi * rq, lambda kj: step(kj, False))
            _loop(qi * rq, (qi + 1) * rq, lambda kj: step(kj, True))
            dq_ref[rows, :] = (jnp.where(m_a, jnp.transpose(dq_s[0]), jnp.transpose(dq_s[1])) * SCALE).astype(BF16)

        _loop(0, nq, qblock)

        def store(c):
            rows = _rows(c, bk)
            dk_ref[rows, :] = dk_s[rows, :].astype(BF16)
            dv_ref[rows, :] = dv_s[rows, :].astype(BF16)

        _loop(0, nk, store)

    ospec = _col_spec(n, 0)
    sds = jax.ShapeDtypeStruct((zs, n, LANES * npairs), BF16)
    scratch = [pltpu.VMEM((n, LANES), BF16)] * 3 + [pltpu.VMEM((nk, LANES, bk), BF16)] + [pltpu.VMEM((n, LANES), F32)] * 2
    scratch += [pltpu.VMEM((2, LANES, bq), F32), pltpu.VMEM((2, 1, bq), F32), pltpu.VMEM((2, 1, bq), F32)]
    return _call(
        body, cargo=cargo, name=name, grid=(zs, npairs),
        in_specs=[_col_spec(n, offs[0]), _col_spec(n, offs[1]), _col_spec(n, offs[2]), ospec, _stat_spec(nq, bq)],
        out_specs=[ospec] * 3, out_shape=[sds] * 3, scratch_shapes=scratch,
        compiler_params=_params(("parallel", "parallel")),
    )(src, src, src, do, ltot)


def _fox_gate_fwd(lg, bias, *, name):
    bs, nh, t = lg.shape
    blk = min(LANES, t)

    def body(lg_ref, b_ref, kb_ref):
        u_le = _tri(blk, lambda r, c: r <= c)
        carry = jnp.zeros((nh, 1), F32)
        for j in range(t // blk):
            sl = slice(j * blk, (j + 1) * blk)
            xv = lg_ref[:, sl] + b_ref[...]
            lf = jnp.minimum(xv, 0.0) - jnp.log(1.0 + jnp.exp(-jnp.abs(xv)))
            kb_ref[:, sl] = -(carry + _cumdot(lf, u_le, 3))
            carry = carry + jnp.sum(lf, axis=1, keepdims=True)

    spec = pl.BlockSpec((None, nh, t), lambda i: (i, 0, 0))
    return _pcall(
        body, name=name, grid=(bs,), in_specs=[spec, pl.BlockSpec((nh, 1), lambda i: (0, 0))], out_specs=spec,
        out_shape=jax.ShapeDtypeStruct((bs, nh, t), F32), compiler_params=_params(("parallel",)),
    )(lg, bias)


def _fox_gate_bwd(dkb, lg, bias, *, name):
    bs, nh, t = lg.shape
    blk = min(LANES, t)

    def body(dkb_ref, lg_ref, b_ref, dlg_ref, db_ref):
        @pl.when(pl.program_id(0) == 0)
        def _():
            db_ref[...] = jnp.zeros_like(db_ref)

        u_ge = _tri(blk, lambda r, c: r >= c)
        carry = jnp.zeros((nh, 1), F32)
        tot = jnp.zeros((nh, 1), F32)
        for j in reversed(range(t // blk)):
            sl = slice(j * blk, (j + 1) * blk)
            df = -dkb_ref[:, sl]
            dlf = carry + _cumdot(df, u_ge, 3)
            carry = carry + jnp.sum(df, axis=1, keepdims=True)
            xv = lg_ref[:, sl] + b_ref[...]
            dlg = dlf * jax.nn.sigmoid(-xv)
            dlg_ref[:, sl] = dlg
            tot = tot + jnp.sum(dlg, axis=1, keepdims=True)
        db_ref[...] += jnp.broadcast_to(tot, db_ref.shape)

    spec = pl.BlockSpec((None, nh, t), lambda i: (i, 0, 0))
    return _pcall(
        body, name=name, grid=(bs,), in_specs=[spec, spec, pl.BlockSpec((nh, 1), lambda i: (0, 0))],
        out_specs=[spec, pl.BlockSpec((nh, LANES), lambda i: (0, 0))],
        out_shape=[jax.ShapeDtypeStruct((bs, nh, t), F32), jax.ShapeDtypeStruct((nh, LANES), F32)],
        compiler_params=_params(("arbitrary",)),
    )(dkb, lg, bias)


def _place():
    return lax.axis_index("x"), lax.axis_index("y"), lax.axis_index("c")


def _flip(v, f):
    return 1 - v if f else v


FLIPS = [(fx, fy, fc) for fx in (0, 1) for fy in (0, 1) for fc in (0, 1)][1:]


def _comm_sems(nw):
    return [pltpu.SemaphoreType.DMA((7, nw)), pltpu.SemaphoreType.DMA((7, nw)), pltpu.SemaphoreType.DMA((nw,))]


def _gather_cargo(shards, on_done):
    nw = len(shards)

    def parts(x_refs, out_refs, sems):
        send_sems, recv_sems, local_sems = sems
        x, y, cc = _place()
        me, sibling = (x, y, cc), (x, y, 1 - cc)
        chips = [(1 - x, y), (x, 1 - y), (1 - x, 1 - y)]

        def slot(i, px, py, pc):
            return out_refs[i].at[4 * px + 2 * py + pc]

        def copy(i, k, block, to, src=None):
            return pltpu.make_async_remote_copy(
                src_ref=slot(i, *block) if src is None else src, dst_ref=slot(i, *block),
                send_sem=send_sems.at[k, i], recv_sem=recv_sems.at[k, i], device_id=to, device_id_type=MESH)

        mine = [pltpu.make_async_copy(x_refs[i], slot(i, *me), local_sems.at[i]) for i in range(nw)]
        first = []
        for i in range(nw):
            first.append(copy(i, 0, me, sibling, src=x_refs[i]))
            first += [copy(i, 1 + j, me, (*chip, cc), src=x_refs[i]) for j, chip in enumerate(chips)]
        return me, sibling, chips, cc, copy, mine, first

    def start(x_refs, out_refs, sems):
        *_, mine, first = parts(x_refs, out_refs, sems)
        for cp in mine + first:
            cp.start()

    def finish(x_refs, out_refs, sems):
        me, sibling, chips, cc, copy, mine, first = parts(x_refs, out_refs, sems)
        passed = []
        for i in range(nw):
            for j, chip in enumerate(chips):
                copy(i, 1 + j, (*chip, cc), me).wait_recv()
                passed.append(copy(i, 4 + j, (*chip, cc), sibling))
                passed[-1].start()
        for i in range(nw):
            copy(i, 0, sibling, me).wait_recv()
            for j, chip in enumerate(chips):
                copy(i, 4 + j, (*chip, 1 - cc), me).wait_recv()
        for cp in first + passed:
            cp.wait_send()
        for cp in mine:
            cp.wait()

    out_shape = [jax.ShapeDtypeStruct((N_DEV, *s.shape), s.dtype) for s in shards]
    return _Cargo(shards, out_shape, _comm_sems(nw), start, finish, on_done)


def _scatter_cargo(slots, prev, layer, depth, on_done, row0=0, rows=None):
    nw = len(slots)

    def parts(refs, recv_refs, sems):
        g_refs = refs[:nw]
        send_sems, recv_sems, local_sems = sems
        x, y, cc = _place()
        my = 4 * x + 2 * y + cc

        def dst(i):
            return recv_refs[i].at[my, layer, pl.ds(row0, slots[i].shape[1])]

        mine, copies = [], []
        for i in range(nw):
            mine.append(pltpu.make_async_copy(g_refs[i].at[my], dst(i), local_sems.at[i]))
            for k, (fx, fy, fc) in enumerate(FLIPS):
                px, py, pc = _flip(x, fx), _flip(y, fy), _flip(cc, fc)
                copies.append(pltpu.make_async_remote_copy(
                    src_ref=g_refs[i].at[4 * px + 2 * py + pc], dst_ref=dst(i),
                    send_sem=send_sems.at[k, i], recv_sem=recv_sems.at[k, i], device_id=(px, py, pc), device_id_type=MESH))
        return mine, copies

    def start(refs, recv_refs, sems):
        mine, copies = parts(refs, recv_refs, sems)
        for cp in mine + copies:
            cp.start()

    def finish(refs, recv_refs, sems):
        mine, copies = parts(refs, recv_refs, sems)
        for cp in copies:
            cp.wait_recv()
        for cp in copies:
            cp.wait_send()
        for cp in mine:
            cp.wait()

    ins, aliases = list(slots), {}
    for i, p in enumerate(prev):
        if p is not None:
            aliases[len(ins)] = i
            ins.append(p)
    out_shape = [jax.ShapeDtypeStruct((N_DEV, depth, rows or s.shape[1], s.shape[2]), s.dtype) for s in slots]
    return _Cargo(ins, out_shape, _comm_sems(nw), start, finish, on_done, aliases)


def _exchange(cargo, *, name):
    def body(*refs):
        c_in = len(cargo.ins)
        c_out = len(cargo.out_shape)
        cargo.start(refs[:c_in], refs[c_in:c_in + c_out], refs[c_in + c_out:])
        cargo.finish(refs[:c_in], refs[c_in:c_in + c_out], refs[c_in + c_out:])

    hbm = pl.BlockSpec(memory_space=pl.ANY)
    res = _pcall(
        body, name=name, in_specs=[hbm] * len(cargo.ins), out_specs=[hbm] * len(cargo.out_shape), out_shape=cargo.out_shape,
        scratch_shapes=cargo.sems, input_output_aliases=dict(cargo.aliases),
    )(*cargo.ins)
    cargo.on_done(list(res))


def _allreduce_small(blob, *, name):
    r, c = blob.shape

    def body(x_ref, out_ref, buf, send_sems, recv_sems):
        x, y, cc = _place()
        my = 4 * x + 2 * y + cc
        copies = []
        for k, (fx, fy, fc) in enumerate(FLIPS):
            peer = (_flip(x, fx), _flip(y, fy), _flip(cc, fc))
            copies.append(pltpu.make_async_remote_copy(
                src_ref=x_ref, dst_ref=buf.at[my], send_sem=send_sems.at[k], recv_sem=recv_sems.at[k],
                device_id=peer, device_id_type=MESH))
        for cp in copies:
            cp.start()
        buf[my] = x_ref[...]
        for cp in copies:
            cp.wait_recv()
        for cp in copies:
            cp.wait_send()
        acc = buf[0]
        for i in range(1, N_DEV):
            acc = acc + buf[i]
        out_ref[...] = acc

    vmem = pl.BlockSpec(memory_space=pltpu.VMEM)
    return _pcall(
        body, name=name, in_specs=[vmem], out_specs=vmem, out_shape=jax.ShapeDtypeStruct((r, c), F32),
        scratch_shapes=[pltpu.VMEM((N_DEV, r, c), F32), pltpu.SemaphoreType.DMA((7,)), pltpu.SemaphoreType.DMA((7,))],
    )(blob)


BIG = ("w_in", "w_mlp_in", "w_mlp_out", "w_up_fox", "w_up_sb", "w_up_dil", "w_out")
ROW_SHARDED = ("w_out", "w_mlp_out")
SMALL = ("attn_norm", "b_forget", "q_norm_fox", "k_norm_fox", "q_norm_dil", "k_norm_dil", "mlp_norm")
BLOB_ROWS = 512


def _pack(parts, dtype):
    flat = jnp.concatenate([p.reshape(-1).astype(dtype) for p in parts])
    size = -(-flat.shape[0] // (BLOB_ROWS * LANES)) * (BLOB_ROWS * LANES)
    return jnp.pad(flat, (0, size - flat.shape[0])).reshape(-1, LANES)


def _unpack(blob, shapes):
    flat = blob.reshape(-1)
    out, off = [], 0
    for shp in shapes:
        size = 1
        for s in shp:
            size *= s
        out.append(flat[off:off + size].reshape(shp))
        off += size
    return out


def _join_shards(name, sh):
    if name in ROW_SHARDED:
        return sh.reshape(-1, sh.shape[2])
    return jnp.transpose(sh, (1, 0, 2)).reshape(sh.shape[1], -1)


def _split_shards(name, full):
    a, b = full.shape
    if name in ROW_SHARDED:
        return full.reshape(N_DEV, a // N_DEV, b)
    return jnp.transpose(full.reshape(a, N_DEV, b // N_DEV), (1, 0, 2))


def _in_segments(d_in):
    o1 = 3 * W_FOX
    o2 = o1 + N_HEADS_FOX
    return (0, o1, 0), (o2, d_in, -N_HEADS_FOX), (o1, o2, d_in - o2)


def _join_w_in(sh, dp):
    b = sh.shape[2]
    pieces = []
    for s, e, _ in _in_segments(N_DEV * b):
        for j in range(s // b, (e - 1) // b + 1):
            pieces.append(sh[j, :, max(s, j * b) - j * b:min(e, (j + 1) * b) - j * b])
    pieces.append(jnp.zeros((sh.shape[1], dp - N_DEV * b), sh.dtype))
    return jnp.concatenate(pieces, axis=1)


def _split_w_in(gp, d_in):
    b = d_in // N_DEV
    shards = []
    for j in range(N_DEV):
        runs = []
        for s, e, shift in sorted(_in_segments(d_in)):
            lo, hi = max(s, j * b), min(e, (j + 1) * b)
            if lo < hi:
                runs.append(gp[:, lo + shift:hi + shift])
        shards.append(jnp.concatenate(runs, axis=1))
    return jnp.stack(shards)


def _stat_to_tokens(st, r, b):
    hh = st.shape[1]
    n = st.shape[2] * st.shape[4]
    return jnp.transpose(st.reshape(b, r, hh, n), (0, 2, 3, 1)).reshape(b, hh, n * r)


def _stat_to_streams(tok, r, blk):
    b, hh, t = tok.shape
    n = t // r
    return jnp.transpose(tok.reshape(b, hh, n, r), (0, 3, 1, 2)).reshape(b * r, hh, n // blk, 1, blk)


def _rope_tables(positions):
    half = HEAD_DIM // 2
    inv = 1.0 / (ROPE_THETA ** (jnp.arange(half, dtype=F32) / half))
    ang = positions.astype(F32)[..., None] * inv
    cos, sin = jnp.cos(ang), jnp.sin(ang)
    return jnp.tile(cos, (1, 1, 4)), jnp.tile(jnp.concatenate([-sin, sin], axis=-1), (1, 1, 2))


def _gain2(g):
    return jnp.tile(g.reshape(1, HEAD_DIM), (1, 2))


def _dil_offs(g):
    c0 = (P_DIL + g * W_DIL) // LANES
    return c0, c0 + W_DILQ // LANES, c0 + 2 * W_DILQ // LANES


def _layer_fwd(l, x, w, small, ropes, bl, t, cargo):
    n, d = x.shape
    s = {}
    s["x"] = x
    s["h"] = _rmsnorm_fwd(x, small["attn_norm"][l].reshape(1, d), name=f"norm_attn_fwd{l}")
    proj = _mm(s["h"], w["w_in"][l], tn=PROJ_TILE, name=f"mm_proj{l}")
    s["proj"] = proj
    dp = proj.shape[1]
    proj3 = proj.reshape(bl, t, dp)
    p_fg = P_GATE + 3 * d

    lg = jnp.transpose(proj3[:, :, p_fg:p_fg + N_HEADS_FOX], (0, 2, 1))
    s["lg"] = lg
    kb = _fox_gate_fwd(lg, small["b_forget"][l].reshape(N_HEADS_FOX, 1), name=f"fox_gate_fwd{l}")
    blk = _att_blocks(t, wide_keys=True)[1]
    kb5 = kb.reshape(bl, N_HEADS_FOX, t // blk, 1, blk)
    s["kb5"] = kb5
    gqf, gkf = _gain2(small["q_norm_fox"][l]), _gain2(small["k_norm_fox"][l])
    fo = P_FOX // LANES
    fox_offs = (fo, fo + W_FOX // LANES, fo + 2 * W_FOX // LANES)
    out_a, lse_a = _attn_fwd(proj3, fox_offs, N_HEADS_FOX // 2, gqf, gkf, kbias=kb5, window=t, name=f"fox_fwd{l}",
                             cargo=cargo.get("fox_fwd"))
    s["out_a"], s["lse_a"] = out_a, lse_a

    so = P_SB // LANES
    sb_offs = (so, so + W_SB // LANES, so + 2 * W_SB // LANES)
    out_b, lt_b = _sb_fwd(proj3, sb_offs, N_HEADS_SB // 2, name=f"sb_fwd{l}", cargo=cargo.get("sb_fwd"))
    s["out_b"], s["lt_b"] = out_b, lt_b

    gqd, gkd = _gain2(small["q_norm_dil"][l]), _gain2(small["k_norm_dil"][l])
    os_, lses = [], []
    for g, (window, r) in enumerate(DIL_PATTERNS):
        o_g, lse_g = _attn_fwd(proj3, _dil_offs(g), N_HEADS_DIL // 2, gqd, gkd, rope=ropes, window=window // r, stride=r,
                               name=f"dil_fwd{l}_{g}")
        os_.append(o_g.reshape(n, W_DIL))
        lses.append(_stat_to_tokens(lse_g, r, bl).reshape(bl * N_HEADS_DIL, t))
    lse_c, *ws = _dil_weights(lses, name=f"dil_weights{l}")
    ws = [jnp.repeat(jnp.transpose(wg.reshape(bl, N_HEADS_DIL, t), (0, 2, 1)).reshape(n, N_HEADS_DIL), HEAD_DIM, axis=1) for wg in ws]
    out_c = _dil_mix(os_, ws, name=f"dil_mix{l}")
    s["out_c"], s["lse_c"] = out_c, lse_c.reshape(bl, N_HEADS_DIL, t)

    ys = [_mm(out_a.reshape(n, W_FOX), w["w_up_fox"][l], name=f"mm_up_fox{l}"),
          _mm(out_b.reshape(n, W_SB), w["w_up_sb"][l], name=f"mm_up_sb{l}"),
          _mm(out_c, w["w_up_dil"][l], name=f"mm_up_dil{l}")]
    s["ys"] = ys
    s["merged"] = _gate_merge_fwd(proj, ys, name=f"gate_merge_fwd{l}")
    x1 = _mm(s["merged"], w["w_out"][l], add=x, name=f"mm_out{l}")
    s["x1"] = x1

    s["h2"] = _rmsnorm_fwd(x1, small["mlp_norm"][l].reshape(1, d), name=f"norm_mlp_fwd{l}")
    s["u"], s["a"] = _mm(s["h2"], w["w_mlp_in"][l], relu2=True, out_dtype=BF16, name=f"mm_mlp_in{l}")
    x2 = _mm(s["a"], w["w_mlp_out"][l], add=x1, name=f"mm_mlp_out{l}")
    return x2, s


def _layer_bwd(l, dx2, s, w, small, ropes, bl, t, hooks):
    n, d = dx2.shape
    gw, gs = {}, {}

    def cargo(call):
        return hooks[call](gw) if call in hooks else None
    du = _mm(dx2, w["w_mlp_out"][l], tb=True, relu_grad_of=s["u"], out_dtype=BF16, name=f"mm_du{l}")
    gw["w_mlp_out"] = _mm(s["a"], dx2, ta=True, name=f"mm_dw_mlp_out{l}")
    gw["w_mlp_in"] = _mm(s["h2"], du, ta=True, name=f"mm_dw_mlp_in{l}")
    dh2 = _mm(du, w["w_mlp_in"][l], tb=True, name=f"mm_dh2{l}")
    dx1, gs["mlp_norm"] = _rmsnorm_bwd(s["x1"], small["mlp_norm"][l].reshape(1, d), dh2, dx2, name=f"norm_mlp_bwd{l}")

    dmerged = _mm(dx1, w["w_out"][l], tb=True, name=f"mm_dmerged{l}")
    gw["w_out"] = _mm(s["merged"], dx1, ta=True, name=f"mm_dw_out{l}")
    dya, dyb, dyc, dgl0, dgl1, dgl2 = _gate_merge_bwd(s["proj"], s["ys"], dmerged, name=f"gate_merge_bwd{l}")
    out_a2, out_b2 = s["out_a"].reshape(n, W_FOX), s["out_b"].reshape(n, W_SB)
    gw["w_up_fox"] = _mm(out_a2, dya, ta=True, name=f"mm_dw_up_fox{l}")
    gw["w_up_sb"] = _mm(out_b2, dyb, ta=True, name=f"mm_dw_up_sb{l}")
    gw["w_up_dil"] = _mm(s["out_c"], dyc, ta=True, name=f"mm_dw_up_dil{l}")
    dout_a = _mm(dya, w["w_up_fox"][l], tb=True, name=f"mm_dout_a{l}").reshape(bl, t, W_FOX)
    dout_b = _mm(dyb, w["w_up_sb"][l], tb=True, name=f"mm_dout_b{l}").reshape(bl, t, W_SB)
    dout_c = _mm(dyc, w["w_up_dil"][l], tb=True, name=f"mm_dout_c{l}").reshape(bl, t, W_DIL)

    proj3 = s["proj"].reshape(bl, t, -1)
    gqf, gkf = _gain2(small["q_norm_fox"][l]), _gain2(small["k_norm_fox"][l])
    fo = P_FOX // LANES
    fox_offs = (fo, fo + W_FOX // LANES, fo + 2 * W_FOX // LANES)
    dq_a, dk_a, dv_a, dg_a, dkb5 = _attn_bwd(proj3, fox_offs, N_HEADS_FOX // 2, gqf, gkf, s["out_a"], dout_a, s["lse_a"],
                                             kbias=s["kb5"], window=t, name=f"fox_bwd{l}", cargo=cargo("fox_bwd"))
    gs["fox_gains"] = dg_a
    dlg, gs["b_forget"] = _fox_gate_bwd(dkb5.reshape(bl, N_HEADS_FOX, t), s["lg"], small["b_forget"][l].reshape(N_HEADS_FOX, 1),
                                        name=f"fox_gate_bwd{l}")
    so = P_SB // LANES
    sb_offs = (so, so + W_SB // LANES, so + 2 * W_SB // LANES)
    dq_b, dk_b, dv_b = _sb_bwd(proj3, sb_offs, N_HEADS_SB // 2, dout_b, s["lt_b"], name=f"sb_bwd{l}", cargo=cargo("sb_bwd"))
    gqd, gkd = _gain2(small["q_norm_dil"][l]), _gain2(small["k_norm_dil"][l])
    out_c3 = s["out_c"].reshape(bl, t, W_DIL)
    dqs, dks, dvs, dgd = [], [], [], None
    for g, (window, r) in enumerate(DIL_PATTERNS):
        lse_g = _stat_to_streams(s["lse_c"], r, _att_blocks(t // r)[0])
        dq_g, dk_g, dv_g, dg_g = _attn_bwd(proj3, _dil_offs(g), N_HEADS_DIL // 2, gqd, gkd, out_c3, dout_c, lse_g, rope=ropes,
                                           window=window // r, stride=r, name=f"dil_bwd{l}_{g}")
        dqs.append(dq_g)
        dks.append(dk_g)
        dvs.append(dv_g)
        dgd = dg_g if dgd is None else jnp.concatenate([dgd, dg_g], axis=0)
    gs["dil_gains"] = dgd

    dlg_cols = jnp.pad(jnp.transpose(dlg, (0, 2, 1)).reshape(n, N_HEADS_FOX), ((0, 0), (0, LANES - N_HEADS_FOX)))
    parts = [p.reshape(n, -1) for p in [dq_a, dk_a, dv_a, dq_b, dk_b, dv_b] + dqs + dks + dvs] + [dgl0, dgl1, dgl2, dlg_cols]
    dproj = _assemble_cols(parts, s["proj"].shape[1], name=f"assemble_dproj{l}")
    if "mm_dw_in_hi" in hooks:
        half = d // 2
        gw["w_in_lo"] = _mm(s["h"][:, :half], dproj, ta=True, tn=PROJ_TILE, name=f"mm_dw_in_lo{l}")
        gw["w_in_hi"] = _mm(s["h"][:, half:], dproj, ta=True, tn=PROJ_TILE, name=f"mm_dw_in_hi{l}", cargo=cargo("mm_dw_in_hi"))
    else:
        gw["w_in"] = _mm(s["h"], dproj, ta=True, tn=PROJ_TILE, name=f"mm_dw_in{l}")
    dh = _mm(dproj, w["w_in"][l], tb=True, tn=1024, tk=PROJ_TILE, name=f"mm_dh{l}", cargo=cargo("mm_dh"))
    dx, gs["attn_norm"] = _rmsnorm_bwd(s["x"], small["attn_norm"][l].reshape(1, d), dh, dx1, name=f"norm_attn_bwd{l}")
    return dx, gw, gs


def kernel(x, positions, attn_norm, w_in, b_forget, q_norm_fox, k_norm_fox, q_norm_dil, k_norm_dil, w_up_fox, w_up_sb, w_up_dil, w_out, mlp_norm, w_mlp_in, w_mlp_out, loss_target, m_attn_norm, m_w_in, m_b_forget, m_q_norm_fox, m_k_norm_fox, m_q_norm_dil, m_k_norm_dil, m_w_up_fox, m_w_up_sb, m_w_up_dil, m_w_out, m_mlp_norm, m_w_mlp_in, m_w_mlp_out, v_attn_norm, v_w_in, v_b_forget, v_q_norm_fox, v_k_norm_fox, v_q_norm_dil, v_k_norm_dil, v_w_up_fox, v_w_up_sb, v_w_up_dil, v_w_out, v_mlp_norm, v_w_mlp_in, v_w_mlp_out):
    bl, t, d = x.shape
    n = bl * t
    depth = attn_norm.shape[0]
    wl = dict(w_in=w_in, w_up_fox=w_up_fox, w_up_sb=w_up_sb, w_up_dil=w_up_dil, w_out=w_out, w_mlp_in=w_mlp_in, w_mlp_out=w_mlp_out)
    ml = dict(w_in=m_w_in, w_up_fox=m_w_up_fox, w_up_sb=m_w_up_sb, w_up_dil=m_w_up_dil, w_out=m_w_out, w_mlp_in=m_w_mlp_in, w_mlp_out=m_w_mlp_out)
    vl = dict(w_in=v_w_in, w_up_fox=v_w_up_fox, w_up_sb=v_w_up_sb, w_up_dil=v_w_up_dil, w_out=v_w_out, w_mlp_in=v_w_mlp_in, w_mlp_out=v_w_mlp_out)
    small = dict(attn_norm=attn_norm, b_forget=b_forget, q_norm_fox=q_norm_fox, k_norm_fox=k_norm_fox, q_norm_dil=q_norm_dil,
                 k_norm_dil=k_norm_dil, mlp_norm=mlp_norm)
    m_small = dict(attn_norm=m_attn_norm, b_forget=m_b_forget, q_norm_fox=m_q_norm_fox, k_norm_fox=m_k_norm_fox,
                   q_norm_dil=m_q_norm_dil, k_norm_dil=m_k_norm_dil, mlp_norm=m_mlp_norm)
    v_small = dict(attn_norm=v_attn_norm, b_forget=v_b_forget, q_norm_fox=v_q_norm_fox, k_norm_fox=v_k_norm_fox,
                   q_norm_dil=v_q_norm_dil, k_norm_dil=v_k_norm_dil, mlp_norm=v_mlp_norm)

    d_in = w_in.shape[-1] * N_DEV
    dp = -(-d_in // 512) * 512
    rest = [k for k in BIG if k != "w_in"]
    w = {k: [None] * depth for k in BIG}

    def gather(items):
        def done(res):
            for (k, l), sh in zip(items, res):
                w[k][l] = _join_w_in(sh, dp) if k == "w_in" else _join_shards(k, sh)

        return _gather_cargo([wl[k][l].astype(BF16) for k, l in items], done)

    _exchange(gather([("w_in", 0)]), name="gather_first")
    cos, sin = _rope_tables(positions)
    ropes = (cos, sin)

    h = x.reshape(n, d)
    saved = []
    for l in range(depth):
        cargo = {"fox_fwd": gather([(k, l) for k in rest])}
        if l + 1 < depth:
            cargo["sb_fwd"] = gather([("w_in", l + 1)])
        h, s = _layer_fwd(l, h, w, small, ropes, bl, t, cargo)
        saved.append(s)
    dy, loss_part = _loss_head(h, loss_target.reshape(n, d), name="loss_head")

    recv = {}

    def scatter(names, l, grads):
        def done(res):
            recv.update(zip(names, res))

        slots = [(_split_w_in(grads[k], d_in) if k == "w_in" else _split_shards(k, grads[k])).astype(BF16) for k in names]
        return _scatter_cargo(slots, [recv.get(k) for k in names], l, depth, done)

    def scatter_w_in_rows(part, l, row0):
        def done(res):
            recv["w_in"] = res[0]

        return _scatter_cargo([_split_w_in(part, d_in).astype(BF16)], [recv.get("w_in")], l, depth, done, row0=row0, rows=d)

    gss = [None] * depth
    above = None
    for l in reversed(range(depth)):
        hooks = {"fox_bwd": lambda gw, l=l: scatter(rest, l, gw)}
        if above is not None:
            hooks["sb_bwd"] = lambda gw, l=l, g=above: scatter(["w_in"], l + 1, g)
        if l == 0:
            hooks["mm_dw_in_hi"] = lambda gw: scatter_w_in_rows(gw["w_in_lo"], 0, 0)
            hooks["mm_dh"] = lambda gw: scatter_w_in_rows(gw["w_in_hi"], 0, d // 2)
        dy, above, gss[l] = _layer_bwd(l, dy, saved[l], w, small, ropes, bl, t, hooks)
    grad_x = dy.reshape(bl, t, d)

    g_big, d_big, m_big, v_big = {}, {}, {}, {}
    for k in BIG:
        g_big[k], d_big[k], m_big[k], v_big[k] = _adamw(recv[k], wl[k], ml[k], vl[k], name=f"adamw_{k}")

    rows = [loss_part]
    for l in range(depth):
        gs = gss[l]
        rows += [gs["attn_norm"].reshape(-1, LANES), gs["mlp_norm"].reshape(-1, LANES), gs["fox_gains"], gs["dil_gains"], gs["b_forget"]]
    row_counts = [r.shape[0] for r in rows]
    part = jnp.concatenate(rows, axis=0)
    pad_rows = -(-part.shape[0] // 8) * 8 - part.shape[0]
    summed = _allreduce_small(jnp.pad(part, ((0, pad_rows), (0, 0))), name="allreduce_small")
    pieces, off = [], 0
    for c in row_counts:
        pieces.append(summed[off:off + c])
        off += c
    loss = pieces[0][0, 0]

    def fold(row):
        return row[:HEAD_DIM] + row[HEAD_DIM:]

    g_small = {k: [] for k in SMALL}
    for l in range(depth):
        an, mn, fg, dg, bf = pieces[1 + 5 * l:6 + 5 * l]
        g_small["attn_norm"].append(an.reshape(d))
        g_small["mlp_norm"].append(mn.reshape(d))
        g_small["q_norm_fox"].append(fold(fg[0]))
        g_small["k_norm_fox"].append(fold(fg[1]))
        g_small["q_norm_dil"].append(fold(dg[0]) + fold(dg[8]) + fold(dg[16]))
        g_small["k_norm_dil"].append(fold(dg[1]) + fold(dg[9]) + fold(dg[17]))
        g_small["b_forget"].append(bf[:, 0])
    g_small = {k: jnp.stack(vs) for k, vs in g_small.items()}
    small_shapes = [small[k].shape for k in SMALL]
    outs = _adamw(_pack([g_small[k] for k in SMALL], F32)[None, None], _pack([small[k] for k in SMALL], F32)[None],
                  _pack([m_small[k] for k in SMALL], F32)[None], _pack([v_small[k] for k in SMALL], F32)[None], name="adamw_small")
    g_sm, d_sm, m_sm, v_sm = (dict(zip(SMALL, _unpack(o, small_shapes))) for o in outs)

    order = ("attn_norm", "w_in", "b_forget", "q_norm_fox", "k_norm_fox", "q_norm_dil", "k_norm_dil", "w_up_fox", "w_up_sb",
             "w_up_dil", "w_out", "mlp_norm", "w_mlp_in", "w_mlp_out")
    res = [loss, grad_x]
    for big, sm in ((g_big, g_sm), (d_big, d_sm), (m_big, m_sm), (v_big, v_sm)):
        res += [big[k] if k in big else sm[k] for k in order]
    return tuple(res)
```

```python
import jax
import jax.numpy as jnp
from jax import lax
from jax.experimental import pallas as pl
from jax.experimental.pallas import tpu as pltpu

F32 = jnp.float32
BF16 = jnp.bfloat16

HEAD_DIM = 64
LANES = 128
N_HEADS_FOX = 8
N_HEADS_SB = 8
N_HEADS_DIL = 4
DIL_PATTERNS = ((128, 1), (512, 4), (2048, 16))
ROPE_THETA = 10000.0
EPS = 1e-6
SCALE = 0.125
W_FOX = N_HEADS_FOX * HEAD_DIM
W_SB = N_HEADS_SB * HEAD_DIM
W_DIL = N_HEADS_DIL * HEAD_DIM
W_DILQ = len(DIL_PATTERNS) * W_DIL
P_FOX = 0
P_SB = 3 * W_FOX
P_DIL = P_SB + 3 * W_SB
P_GATE = P_DIL + 3 * W_DILQ
N_DEV = 8
ATT_BLK = 256
ATT_BQ = 512
NEG = -1e30
VMEM_LIMIT = 56 * 1024 * 1024
ADAMW_BLOCK_ELEMS = 128 * 1024
PROJ_TILE = 2176

ADAM_LR = 0.001
ADAM_B1 = 0.9
ADAM_B2 = 0.999
ADAM_EPS = 1e-08
ADAM_WD = 0.01
ADAM_STEP = 10

NT = (((1,), (1,)), ((), ()))
MESH = pl.DeviceIdType.MESH


def _pcall(body, **kw):
    return pl.pallas_call(body, **kw)


def _params(sem=None):
    return pltpu.CompilerParams(dimension_semantics=sem, vmem_limit_bytes=VMEM_LIMIT)


class _Cargo:
    def __init__(self, ins, out_shape, sems, start, finish, on_done, aliases=None):
        self.ins, self.out_shape, self.sems = list(ins), list(out_shape), list(sems)
        self.start, self.finish, self.on_done, self.aliases = start, finish, on_done, dict(aliases or {})


def _call(body, *, cargo=None, name, grid=(), in_specs, out_specs, out_shape, scratch_shapes=(), compiler_params=None):
    if cargo is None:
        kw = dict(grid=grid) if grid else {}
        if compiler_params is not None:
            kw["compiler_params"] = compiler_params
        return _pcall(body, name=name, in_specs=in_specs, out_specs=out_specs, out_shape=out_shape,
                      scratch_shapes=list(scratch_shapes), **kw)
    single = not isinstance(out_shape, (list, tuple))
    o_specs, o_shape = ([out_specs], [out_shape]) if single else (list(out_specs), list(out_shape))
    n_in, n_out, n_scr = len(in_specs), len(o_shape), len(scratch_shapes)
    c_in, c_out = len(cargo.ins), len(cargo.out_shape)

    def wrapped(*refs):
        ins, cins = refs[:n_in], refs[n_in:n_in + c_in]
        o0 = n_in + c_in
        outs, couts = refs[o0:o0 + n_out], refs[o0 + n_out:o0 + n_out + c_out]
        s0 = o0 + n_out + c_out
        scr, sems = refs[s0:s0 + n_scr], refs[s0 + n_scr:]
        first = last = None
        for ax, size in enumerate(grid):
            pid = pl.program_id(ax)
            first = (pid == 0) if first is None else first & (pid == 0)
            last = (pid == size - 1) if last is None else last & (pid == size - 1)
        if first is None:
            cargo.start(cins, couts, sems)
            body(*ins, *outs, *scr)
            cargo.finish(cins, couts, sems)
            return

        @pl.when(first)
        def _():
            cargo.start(cins, couts, sems)

        body(*ins, *outs, *scr)

        @pl.when(last)
        def _():
            cargo.finish(cins, couts, sems)

    hbm = pl.BlockSpec(memory_space=pl.ANY)
    kw = dict(grid=grid, compiler_params=_params(("arbitrary",) * len(grid))) if grid else {}
    call = _pcall(
        wrapped, name=name, in_specs=list(in_specs) + [hbm] * c_in, out_specs=o_specs + [hbm] * c_out,
        out_shape=o_shape + cargo.out_shape, scratch_shapes=list(scratch_shapes) + cargo.sems,
        input_output_aliases={n_in + i: n_out + j for i, j in cargo.aliases.items()}, **kw)

    def run(*args):
        res = call(*args, *cargo.ins)
        cargo.on_done(list(res[n_out:]))
        return res[0] if single else list(res[:n_out])

    return run


def _tile(dim, target, mult=LANES):
    t = (min(dim, target) // mult) * mult
    while t >= mult:
        if dim % t == 0:
            return t
        t -= mult
    return dim


def _mm(a, b, *, ta=False, tb=False, add=None, relu2=False, relu_grad_of=None, out_dtype=F32, name, tm=1024, tn=1024,
        tk=1024, cargo=None):
    m, k = (a.shape[1], a.shape[0]) if ta else a.shape
    n = b.shape[0] if tb else b.shape[1]
    tm, tn, tk = _tile(m, tm), _tile(n, tn), _tile(k, tk)
    nk = k // tk
    dn = (((0,) if ta else (1,), (1,) if tb else (0,)), ((), ()))

    extra = add if add is not None else relu_grad_of

    def body(*refs):
        a_ref, b_ref = refs[:2]
        x_ref = refs[2] if extra is not None else None
        outs = refs[2 + (extra is not None):-1]
        acc = refs[-1]
        kk = pl.program_id(2)
        part = lax.dot_general(a_ref[...].astype(BF16), b_ref[...].astype(BF16), dn, preferred_element_type=F32)

        def finish(r):
            if add is not None:
                r = r + x_ref[...]
            if relu_grad_of is not None:
                r = r * (2.0 * jnp.maximum(x_ref[...].astype(F32), 0.0))
            outs[0][...] = r.astype(out_dtype)
            if relu2:
                rr = jnp.maximum(r, 0.0)
                outs[1][...] = (rr * rr).astype(BF16)

        if nk == 1:
            finish(part)
            return

        @pl.when(kk == 0)
        def _():
            acc[...] = part

        @pl.when((kk > 0) & (kk < nk - 1))
        def _():
            acc[...] += part

        @pl.when(kk == nk - 1)
        def _():
            finish(acc[...] + part)

    a_spec = pl.BlockSpec((tk, tm), lambda i, j, q: (q, i)) if ta else pl.BlockSpec((tm, tk), lambda i, j, q: (i, q))
    b_spec = pl.BlockSpec((tn, tk), lambda i, j, q: (j, q)) if tb else pl.BlockSpec((tk, tn), lambda i, j, q: (q, j))
    o_spec = pl.BlockSpec((tm, tn), lambda i, j, q: (i, j))
    ins, specs = [a, b], [a_spec, b_spec]
    if extra is not None:
        ins.append(extra)
        specs.append(o_spec)
    sds = jax.ShapeDtypeStruct((m, n), out_dtype)
    return _call(
        body, cargo=cargo, name=name, grid=(m // tm, n // tn, nk), in_specs=specs,
        out_specs=[o_spec, o_spec] if relu2 else o_spec,
        out_shape=[sds, jax.ShapeDtypeStruct((m, n), BF16)] if relu2 else sds,
        scratch_shapes=[pltpu.VMEM((tm, tn) if nk > 1 else (8, LANES), F32)],
        compiler_params=_params(("parallel", "parallel", "arbitrary")),
    )(*ins)


def _rmsnorm_fwd(x, g, *, name):
    n, d = x.shape
    tm = _tile(n, 256, 8)

    def body(x_ref, g_ref, h_ref):
        xv = x_ref[...]
        inv = lax.rsqrt(jnp.mean(xv * xv, axis=1, keepdims=True) + EPS)
        h_ref[...] = (xv * inv * g_ref[...]).astype(BF16)

    row = pl.BlockSpec((tm, d), lambda i: (i, 0))
    return _pcall(
        body, name=name, grid=(n // tm,), in_specs=[row, pl.BlockSpec((1, d), lambda i: (0, 0))], out_specs=row,
        out_shape=jax.ShapeDtypeStruct((n, d), BF16), compiler_params=_params(("parallel",)),
    )(x, g)


def _rmsnorm_bwd(x, g, dh, dres, *, name):
    n, d = x.shape
    tm = _tile(n, 256, 8)

    def body(x_ref, g_ref, dh_ref, dres_ref, dx_ref, dg_ref):
        @pl.when(pl.program_id(0) == 0)
        def _():
            dg_ref[...] = jnp.zeros_like(dg_ref)

        xv = x_ref[...]
        inv = lax.rsqrt(jnp.mean(xv * xv, axis=1, keepdims=True) + EPS)
        y = xv * inv
        dhv = dh_ref[...]
        dg_ref[...] += jnp.sum(dhv * y, axis=0, keepdims=True)
        dy = dhv * g_ref[...]
        dx_ref[...] = dres_ref[...] + inv * (dy - y * jnp.mean(dy * y, axis=1, keepdims=True))

    row = pl.BlockSpec((tm, d), lambda i: (i, 0))
    vec = pl.BlockSpec((1, d), lambda i: (0, 0))
    return _pcall(
        body, name=name, grid=(n // tm,), in_specs=[row, vec, row, row], out_specs=[row, vec],
        out_shape=[jax.ShapeDtypeStruct((n, d), F32), jax.ShapeDtypeStruct((1, d), F32)],
        compiler_params=_params(("arbitrary",)),
    )(x, g, dh, dres)


def _gate_specs(n, d):
    bw = 256 if d % 256 == 0 else LANES
    tm = _tile(n, 512, 8)
    nb = d // bw
    yspec = pl.BlockSpec((tm, bw), lambda i, j: (i, j))
    gspecs = [pl.BlockSpec((tm, bw), lambda i, j, b=b: (i, P_GATE // bw + b * nb + j)) for b in range(3)]
    return tm, bw, nb, yspec, gspecs


def _gate_merge_fwd(proj, ys, *, name):
    n, d = ys[0].shape
    tm, bw, nb, yspec, gspecs = _gate_specs(n, d)

    def body(g0, g1, g2, y0, y1, y2, o_ref):
        acc = jax.nn.sigmoid(g0[...]) * y0[...]
        acc += jax.nn.sigmoid(g1[...]) * y1[...]
        acc += jax.nn.sigmoid(g2[...]) * y2[...]
        o_ref[...] = acc.astype(BF16)

    return _pcall(
        body, name=name, grid=(n // tm, nb), in_specs=gspecs + [yspec] * 3, out_specs=yspec,
        out_shape=jax.ShapeDtypeStruct((n, d), BF16), compiler_params=_params(("parallel", "parallel")),
    )(proj, proj, proj, *ys)


def _gate_merge_bwd(proj, ys, dmerged, *, name):
    n, d = ys[0].shape
    tm, bw, nb, yspec, gspecs = _gate_specs(n, d)

    def body(g0, g1, g2, y0, y1, y2, dm_ref, dy0, dy1, dy2, dgl0, dgl1, dgl2):
        dm = dm_ref[...]
        for g_ref, y_ref, dy_ref, dgl_ref in ((g0, y0, dy0, dgl0), (g1, y1, dy1, dgl1), (g2, y2, dy2, dgl2)):
            s = jax.nn.sigmoid(g_ref[...])
            dy_ref[...] = (dm * s).astype(BF16)
            dgl_ref[...] = (dm * y_ref[...] * s * (1.0 - s)).astype(BF16)

    sds = jax.ShapeDtypeStruct((n, d), BF16)
    return _pcall(
        body, name=name, grid=(n // tm, nb), in_specs=gspecs + [yspec] * 4, out_specs=[yspec] * 6,
        out_shape=[sds] * 6, compiler_params=_params(("parallel", "parallel")),
    )(proj, proj, proj, *ys, dmerged)


def _loss_head(y, tgt, *, name):
    n, d = y.shape
    tm = _tile(n, 256, 8)
    steps = n // tm

    def body(y_ref, t_ref, dy_ref, loss_ref, acc):
        i = pl.program_id(0)

        @pl.when(i == 0)
        def _():
            acc[...] = jnp.zeros_like(acc)

        e = y_ref[...] - t_ref[...]
        dy_ref[...] = e * (1.0 / d)
        acc[...] += jnp.sum(e * e, axis=0, keepdims=True)

        @pl.when(i == steps - 1)
        def _():
            tot = jnp.sum(acc[...], axis=1, keepdims=True) * (0.5 / d)
            loss_ref[...] = jnp.broadcast_to(tot, loss_ref.shape)

    row = pl.BlockSpec((tm, d), lambda i: (i, 0))
    return _pcall(
        body, name=name, grid=(steps,), in_specs=[row, row], out_specs=[row, pl.BlockSpec((8, LANES), lambda i: (0, 0))],
        out_shape=[jax.ShapeDtypeStruct((n, d), F32), jax.ShapeDtypeStruct((8, LANES), F32)],
        scratch_shapes=[pltpu.VMEM((1, d), F32)], compiler_params=_params(("arbitrary",)),
    )(y, tgt)


def _assemble_cols(parts, width, *, name):
    n = parts[0].shape[0]
    tm = _tile(n, 256, 16)
    widths = [p.shape[1] for p in parts]

    def body(*refs):
        o_ref = refs[-1]
        off = 0
        for ref, w in zip(refs[:-1], widths):
            o_ref[:, off:off + w] = ref[...].astype(BF16)
            off += w
        if off < width:
            o_ref[:, off:] = jnp.zeros((tm, width - off), BF16)

    return _pcall(
        body, name=name, grid=(n // tm,), in_specs=[pl.BlockSpec((tm, w), lambda i: (i, 0)) for w in widths],
        out_specs=pl.BlockSpec((tm, width), lambda i: (i, 0)), out_shape=jax.ShapeDtypeStruct((n, width), BF16),
        compiler_params=_params(("parallel",)),
    )(*parts)


def _dil_weights(lses, *, name):
    shp = lses[0].shape

    def body(l0, l1, l2, lse_ref, w0, w1, w2):
        a, b, c = l0[...], l1[...], l2[...]
        m = jnp.maximum(jnp.maximum(a, b), c)
        ea, eb, ec = jnp.exp(a - m), jnp.exp(b - m), jnp.exp(c - m)
        den = ea + eb + ec
        lse_ref[...] = m + jnp.log(den)
        w0[...] = ea / den
        w1[...] = eb / den
        w2[...] = ec / den

    vmem = pl.BlockSpec(memory_space=pltpu.VMEM)
    return _pcall(body, name=name, in_specs=[vmem] * 3, out_specs=[vmem] * 4, out_shape=[jax.ShapeDtypeStruct(shp, F32)] * 4)(*lses)


def _dil_mix(os_, ws, *, name):
    n, w = os_[0].shape
    tm = _tile(n, 512, 8)

    def body(o0, o1, o2, w0, w1, w2, out_ref):
        out_ref[...] = w0[...] * o0[...] + w1[...] * o1[...] + w2[...] * o2[...]

    spec = pl.BlockSpec((tm, w), lambda i: (i, 0))
    return _pcall(
        body, name=name, grid=(n // tm,), in_specs=[spec] * 6, out_specs=spec, out_shape=jax.ShapeDtypeStruct((n, w), F32),
        compiler_params=_params(("parallel",)),
    )(*os_, *ws)


def _adamw(gsrc, w, m, v, *, name):
    s, dep, a, b = gsrc.shape
    ta = _tile(a, max(16, (ADAMW_BLOCK_ELEMS // b) // 16 * 16), 16)
    c1 = 1.0 / (1.0 - ADAM_B1 ** ADAM_STEP)
    c2 = 1.0 / (1.0 - ADAM_B2 ** ADAM_STEP)

    def body(gs_ref, w_ref, m_ref, v_ref, g_ref, d_ref, m2_ref, v2_ref):
        g = gs_ref[0].astype(F32)
        for i in range(1, s):
            g = g + gs_ref[i].astype(F32)
        m2 = ADAM_B1 * m_ref[...] + (1.0 - ADAM_B1) * g
        v2 = ADAM_B2 * v_ref[...] + (1.0 - ADAM_B2) * (g * g)
        g_ref[...] = g
        m2_ref[...] = m2
        v2_ref[...] = v2
        d_ref[...] = -ADAM_LR * ((m2 * c1) / (jnp.sqrt(v2 * c2) + ADAM_EPS) + ADAM_WD * w_ref[...])

    spec = pl.BlockSpec((None, ta, b), lambda l, i: (l, i, 0))
    sds = jax.ShapeDtypeStruct((dep, a, b), F32)
    return _pcall(
        body, name=name, grid=(dep, a // ta),
        in_specs=[pl.BlockSpec((s, None, ta, b), lambda l, i: (0, l, i, 0)), spec, spec, spec],
        out_specs=[spec] * 4, out_shape=[sds] * 4, compiler_params=_params(("parallel", "parallel")),
    )(gsrc, w, m, v)


def _mask_a():
    return lax.broadcasted_iota(jnp.int32, (1, LANES), 1) < HEAD_DIM


def _half_sum(x, m_a):
    sa = jnp.sum(jnp.where(m_a, x, 0.0), axis=1, keepdims=True)
    sb = jnp.sum(jnp.where(m_a, 0.0, x), axis=1, keepdims=True)
    return jnp.where(m_a, sa, sb)


def _head_inv(x, m_a):
    return lax.rsqrt(_half_sum(x * x, m_a) * (1.0 / HEAD_DIM) + EPS)


def _swap32(x):
    first = (lax.broadcasted_iota(jnp.int32, (1, LANES), 1) % HEAD_DIM) < (HEAD_DIM // 2)
    return jnp.where(first, pltpu.roll(x, LANES - HEAD_DIM // 2, 1), pltpu.roll(x, HEAD_DIM // 2, 1))


def _tri(blk, rel):
    r = lax.broadcasted_iota(jnp.int32, (blk, blk), 0)
    c = lax.broadcasted_iota(jnp.int32, (blk, blk), 1)
    return jnp.where(rel(r, c), 1.0, 0.0).astype(BF16)


def _cumdot(x, u, parts):
    acc = None
    r = x
    for i in range(parts):
        xi = r.astype(BF16)
        t = jnp.dot(xi, u, preferred_element_type=F32)
        acc = t if acc is None else acc + t
        if i + 1 < parts:
            r = r - xi.astype(F32)
    return acc


def _rows(i, blk):
    return pl.ds(pl.multiple_of(i * blk, blk), blk)


def _col_spec(n, off):
    return pl.BlockSpec((None, n, LANES), lambda z, p, off=off: (z, 0, off + p))


def _att_blk(n):
    return ATT_BLK if n % ATT_BLK == 0 else min(LANES, n)


def _att_blocks(n, wide_keys=False):
    bk = _att_blk(n)
    bq = ATT_BQ if n % ATT_BQ == 0 else bk
    return bq, (bq if wide_keys else bk)


def _loop(lo, hi, fn):
    def it(i, c):
        fn(i)
        return c

    lax.fori_loop(lo, hi, it, 0)


def _normed(src, g_ref, rope_refs, rows, m_a):
    xv = src[rows, :]
    xn = xv * _head_inv(xv, m_a) * g_ref[...]
    if rope_refs is not None:
        xn = xn * rope_refs[0][rows, :] + _swap32(xn) * rope_refs[1][rows, :]
    return xn


def _bias_lane(h):
    return HEAD_DIM if h == 0 else 0


def _k_for_head(kn, kb_row, h, m_h, lane, blk):
    out = jnp.where(m_h, kn, 0.0)
    if kb_row is not None:
        col = jnp.transpose(jnp.broadcast_to(kb_row, (LANES, blk)))
        hi = col.astype(BF16).astype(F32)
        mid = (col - hi).astype(BF16).astype(F32)
        lo = col - hi - mid
        b = _bias_lane(h)
        out = jnp.where(lane == b, hi, jnp.where(lane == b + 1, mid, jnp.where(lane == b + 2, lo, out)))
    return out.astype(BF16)


def _q_for_head(qb, h, m_h, lane, biased):
    out = jnp.where(m_h, qb, 0)
    if biased:
        b = _bias_lane(h)
        out = jnp.where((lane >= b) & (lane < b + 3), jnp.ones_like(out), out)
    return out


def _head_rows(x, parts=3):
    rr = lax.broadcasted_iota(jnp.int32, (8, LANES), 0)
    ll = lax.broadcasted_iota(jnp.int32, (8, LANES), 1)
    sel = jnp.where(((rr == 0) & (ll < HEAD_DIM)) | ((rr == 1) & (ll >= HEAD_DIM)), 1.0, 0.0).astype(BF16)
    acc = None
    rem = x
    for i in range(parts):
        xi = rem.astype(BF16)
        t = lax.dot_general(sel, xi, NT, preferred_element_type=F32)
        acc = t if acc is None else acc + t
        if i + 1 < parts:
            rem = rem - xi.astype(F32)
    return acc


def _cumdot_left(u, x, parts):
    acc = None
    rem = x
    for i in range(parts):
        xi = rem.astype(BF16)
        t = jnp.dot(u, xi, preferred_element_type=F32)
        acc = t if acc is None else acc + t
        if i + 1 < parts:
            rem = rem - xi.astype(F32)
    return acc


def _q_minus_k(bk, bq):
    return lax.broadcasted_iota(jnp.int32, (bk, bq), 1) - lax.broadcasted_iota(jnp.int32, (bk, bq), 0)


def _stat_spec(nb, blk):
    return pl.BlockSpec((None, 2, nb, 1, blk), lambda z, p: (z, p, 0, 0, 0))


def _stream_rows(stride):
    if stride == 1:
        return _rows
    c = pl.program_id(2)
    return lambda i, blk: pl.ds(c + i * (blk * stride), blk, stride=stride)


def _attn_specs(t, pairs, stride, nq, bq):
    col = lambda off: pl.BlockSpec((None, t, LANES), lambda z, p, c: (z, 0, off + p * pairs))
    vec = pl.BlockSpec((1, LANES), lambda z, p, c: (0, 0))
    seq = pl.BlockSpec((None, t, LANES), lambda z, p, c: (z, 0, 0))
    stat = pl.BlockSpec((None, 2 * pairs, nq, 1, bq), lambda z, p, c: (z * stride + c, p, 0, 0, 0))
    return col, vec, seq, stat


def _attn_fwd(src, offs, npairs, gq, gk, *, rope=None, kbias=None, window, stride=1, pairs=1, name, cargo=None):
    bs, t, _ = src.shape
    n = t // stride
    full = window >= n
    bq, bk = _att_blocks(n, wide_keys=full)
    nq, nk, rq = n // bq, n // bk, bq // bk
    wblk = -(-window // bk)
    biased = kbias is not None
    heads = range(2 * pairs)

    def body(*refs):
        it = iter(refs)
        q_refs, k_refs, v_refs = ([next(it) for _ in range(pairs)] for _ in range(3))
        gq_ref, gk_ref = next(it), next(it)
        rope_refs = (next(it), next(it)) if rope is not None else None
        kb_ref = next(it) if biased else None
        o_refs = [next(it) for _ in range(pairs)]
        lse_ref, qn_s, kh_s, vt_s, acc_s, m_s = (next(it) for _ in range(6))
        m_a = _mask_a()
        masks = (m_a, jnp.logical_not(m_a))
        lane = lax.broadcasted_iota(jnp.int32, (1, LANES), 1)
        row = lax.broadcasted_iota(jnp.int32, (LANES, 1), 0)
        tok = _stream_rows(stride)

        def prep(c):
            rows = _rows(c, bk)
            trows = tok(c, bk)
            for pp in range(pairs):
                qn = _normed(q_refs[pp], gq_ref, rope_refs, trows, m_a)
                qn_s[pp, rows, :] = (qn * SCALE).astype(BF16)
                kn = _normed(k_refs[pp], gk_ref, rope_refs, trows, m_a)
                vt = jnp.transpose(v_refs[pp][trows, :])
                for half in (0, 1):
                    h = 2 * pp + half
                    kh_s[h, rows, :] = _k_for_head(kn, kb_ref[h, c] if biased else None, half, masks[half], lane, bk)
                    vt_s[h, c] = jnp.where(row == _bias_lane(half), 1.0, vt).astype(BF16)

        _loop(0, nk, prep)
        qk = _q_minus_k(bk, bq)

        def qblock(qi):
            rows = _rows(qi, bq)
            qh = [_q_for_head(qn_s[h // 2, rows, :], h % 2, masks[h % 2], lane, biased) for h in heads]
            m_s[...] = jnp.full(m_s.shape, NEG, F32)
            acc_s[...] = jnp.zeros_like(acc_s)

            def step(kj, masked):
                cols = _rows(kj, bk)
                sts = [lax.dot_general(kh_s[h, cols, :], qh[h], NT, preferred_element_type=F32) for h in heads]
                old = [(m_s[h], acc_s[h]) for h in heads]
                if masked:
                    d = qk + (qi * bq - kj * bk)
                    ok = (d >= 0) & (d <= window)
                    sts = [jnp.where(ok, st, NEG) for st in sts]
                new = []
                for h in heads:
                    m, acc = old[h]
                    m2 = jnp.maximum(m, jnp.max(sts[h], axis=0, keepdims=True))
                    pt = jnp.exp(sts[h] - m2).astype(BF16)
                    new.append((m2, jnp.exp(m - m2) * acc + jnp.dot(vt_s[h, kj], pt, preferred_element_type=F32)))
                for h in heads:
                    m_s[h], acc_s[h] = new[h]

            if full:
                _loop(0, qi * rq, lambda kj: step(kj, False))
                _loop(qi * rq, (qi + 1) * rq, lambda kj: step(kj, True))
            else:
                _loop(jnp.maximum(qi * rq - wblk, 0), (qi + 1) * rq, lambda kj: step(kj, True))
            outs = []
            for h in heads:
                acc_t = acc_s[h]
                den = acc_t[_bias_lane(h % 2):_bias_lane(h % 2) + 1, :]
                outs.append(jnp.transpose(acc_t / den))
                lse_ref[h, qi] = m_s[h] + jnp.log(den)
            for pp in range(pairs):
                o_refs[pp][tok(qi, bq), :] = jnp.where(m_a, outs[2 * pp], outs[2 * pp + 1])

        _loop(0, nq, qblock)

    col, vec, seq, stat = _attn_specs(t, pairs, stride, nq, bq)
    ins = [src] * (3 * pairs) + [gq, gk]
    specs = [col(off + pp) for off in offs for pp in range(pairs)] + [vec, vec]
    if rope is not None:
        ins += list(rope)
        specs += [seq, seq]
    if biased:
        ins.append(kbias)
        specs.append(pl.BlockSpec((None, 2, nk, 1, bk), lambda z, p, c: (z, p, 0, 0, 0)))
    scratch = [pltpu.VMEM((pairs, n, LANES), BF16), pltpu.VMEM((2 * pairs, n, LANES), BF16)]
    scratch += [pltpu.VMEM((2 * pairs, nk, LANES, bk), BF16), pltpu.VMEM((2 * pairs, LANES, bq), F32)]
    scratch += [pltpu.VMEM((2 * pairs, 1, bq), F32)]
    assert pairs in (1, npairs)
    ospec = pl.BlockSpec((None, t, LANES), lambda z, p, c: (z, 0, p))
    *os_, lse = _call(
        body, cargo=cargo, name=name, grid=(bs, npairs // pairs, stride), in_specs=specs, out_specs=[ospec] * pairs + [stat],
        out_shape=[jax.ShapeDtypeStruct((bs, t, LANES * npairs // pairs), F32)] * pairs
        + [jax.ShapeDtypeStruct((bs * stride, 2 * npairs, nq, 1, bq), F32)],
        scratch_shapes=scratch, compiler_params=_params(("parallel", "parallel", "arbitrary")),
    )(*ins)
    return (os_[0] if pairs == 1 else jnp.concatenate(os_, axis=-1)), lse


def _attn_bwd(src, offs, npairs, gq, gk, o, do, lse, *, rope=None, kbias=None, window, stride=1, pairs=1, name, cargo=None):
    bs, t, _ = src.shape
    n = t // stride
    full = window >= n
    bq, bk = _att_blocks(n, wide_keys=full)
    nq, nk, rq = n // bq, n // bk, bq // bk
    wblk = -(-window // bk)
    biased = kbias is not None
    heads = range(2 * pairs)
    gdt = BF16 if stride == 1 else F32

    def body(*refs):
        it = iter(refs)
        q_refs, k_refs, v_refs = ([next(it) for _ in range(pairs)] for _ in range(3))
        gq_ref, gk_ref = next(it), next(it)
        o_refs, do_refs = ([next(it) for _ in range(pairs)] for _ in range(2))
        lse_ref = next(it)
        rope_refs = (next(it), next(it)) if rope is not None else None
        kb_ref = next(it) if biased else None
        dq_refs, dk_refs, dv_refs = ([next(it) for _ in range(pairs)] for _ in range(3))
        dg_ref = next(it)
        dkb_ref = next(it) if biased else None
        qn_s, kh_s, vb_s, kt_s, dqn_s, dkh_s, dv_s, dq_s, rs_s = (next(it) for _ in range(9))
        m_a = _mask_a()
        masks = (m_a, jnp.logical_not(m_a))
        lane = lax.broadcasted_iota(jnp.int32, (1, LANES), 1)
        tok = _stream_rows(stride)

        @pl.when((pl.program_id(0) == 0) & (pl.program_id(1) == 0) & (pl.program_id(2) == 0))
        def _():
            dg_ref[...] = jnp.zeros_like(dg_ref)

        def prep(c):
            rows = _rows(c, bk)
            trows = tok(c, bk)
            for pp in range(pairs):
                qn = _normed(q_refs[pp], gq_ref, rope_refs, trows, m_a)
                qn_s[pp, rows, :] = (qn * SCALE).astype(BF16)
                kn = _normed(k_refs[pp], gk_ref, rope_refs, trows, m_a)
                kt_s[pp, c] = jnp.transpose(kn).astype(BF16)
                vb_s[pp, rows, :] = v_refs[pp][trows, :].astype(BF16)
                for half in (0, 1):
                    h = 2 * pp + half
                    kh_s[h, rows, :] = _k_for_head(kn, kb_ref[h, c] if biased else None, half, masks[half], lane, bk)

        _loop(0, nk, prep)
        dkh_s[...] = jnp.zeros_like(dkh_s)
        dv_s[...] = jnp.zeros_like(dv_s)
        qk = _q_minus_k(bk, bq)

        def qblock(qi):
            rows = _rows(qi, bq)
            trows = tok(qi, bq)
            dobs = [do_refs[pp][trows, :] for pp in range(pairs)]
            deltas = [_head_rows(dobs[pp] * o_refs[pp][trows, :]) for pp in range(pairs)]
            qh = [_q_for_head(qn_s[h // 2, rows, :], h % 2, masks[h % 2], lane, biased) for h in heads]
            doms = [jnp.where(masks[h % 2], dobs[h // 2], 0.0).astype(BF16) for h in heads]
            delta = [deltas[h // 2][h % 2:h % 2 + 1, :] for h in heads]
            lses = [lse_ref[h, qi] for h in heads]
            dq_s[...] = jnp.zeros_like(dq_s)
            if biased:
                for h in heads:
                    rs_s[h, qi] = jnp.zeros((1, bq), F32)

            def step(kj, masked):
                cols = _rows(kj, bk)
                vbs = [vb_s[pp, cols, :] for pp in range(pairs)]
                kts = [kt_s[pp, kj] for pp in range(pairs)]
                sts = [lax.dot_general(kh_s[h, cols, :], qh[h], NT, preferred_element_type=F32) for h in heads]
                dpts = [lax.dot_general(vbs[h // 2], doms[h], NT, preferred_element_type=F32) for h in heads]
                if masked:
                    d = qk + (qi * bq - kj * bk)
                    ok = (d >= 0) & (d <= window)
                    sts = [jnp.where(ok, st, NEG) for st in sts]
                new = []
                for h in heads:
                    pt = jnp.exp(sts[h] - lses[h])
                    dst = pt * (dpts[h] - delta[h])
                    dsb = dst.astype(BF16)
                    tk = jnp.dot(dsb, qh[h], preferred_element_type=F32)
                    if biased:
                        tk = tk + jnp.dot((dst - dsb.astype(F32)).astype(BF16), qh[h], preferred_element_type=F32)
                    tv = jnp.dot(pt.astype(BF16), doms[h], preferred_element_type=F32)
                    tq = jnp.dot(kts[h // 2], dsb, preferred_element_type=F32)
                    new.append((tk, tv, tq, jnp.sum(dst, axis=0, keepdims=True) if biased else None))
                for h in heads:
                    dkh_s[h, cols, :] += new[h][0]
                    dq_s[h] += new[h][2]
                    if biased:
                        rs_s[h, qi] += new[h][3]
                for pp in range(pairs):
                    dv_s[pp, cols, :] += new[2 * pp][1] + new[2 * pp + 1][1]

            if full:
                _loop(0, qi * rq, lambda kj: step(kj, False))
                _loop(qi * rq, (qi + 1) * rq, lambda kj: step(kj, True))
            else:
                _loop(jnp.maximum(qi * rq - wblk, 0), (qi + 1) * rq, lambda kj: step(kj, True))
            for pp in range(pairs):
                dqn_s[pp, rows, :] = jnp.where(m_a, jnp.transpose(dq_s[2 * pp]), jnp.transpose(dq_s[2 * pp + 1])) * SCALE

        _loop(0, nq, qblock)

        def finish(c, carry):
            rows = _rows(c, bq)
            trows = tok(c, bq)
            dgq, dgk = carry
            for pp in range(pairs):
                dk_pair = [dkh_s[2 * pp, rows, :], dkh_s[2 * pp + 1, rows, :]]
                if biased:
                    for half in (0, 1):
                        b = _bias_lane(half)
                        dkb_row = jnp.transpose(dk_pair[half])[b:b + 1, :] - rs_s[2 * pp + half, c]
                        for j in range(rq):
                            dkb_ref[2 * pp + half, c * rq + j] = dkb_row[:, j * bk:(j + 1) * bk]
                dv_refs[pp][trows, :] = dv_s[pp, rows, :].astype(gdt)
                grads = (dqn_s[pp, rows, :], jnp.where(m_a, dk_pair[0], dk_pair[1]))
                out = []
                for src_ref, g_ref, dxn, dst in ((q_refs[pp], gq_ref, grads[0], dq_refs[pp]),
                                                 (k_refs[pp], gk_ref, grads[1], dk_refs[pp])):
                    xv = src_ref[trows, :]
                    inv = _head_inv(xv, m_a)
                    y = xv * inv
                    if rope_refs is not None:
                        dxn = dxn * rope_refs[0][trows, :] + _swap32(dxn * rope_refs[1][trows, :])
                    dy = dxn * g_ref[...]
                    dst[trows, :] = (inv * (dy - y * (_half_sum(dy * y, m_a) * (1.0 / HEAD_DIM)))).astype(gdt)
                    out.append(jnp.sum(dxn * y, axis=0, keepdims=True))
                dgq, dgk = dgq + out[0], dgk + out[1]
            return dgq, dgk

        zero = jnp.zeros((1, LANES), F32)
        dgq, dgk = lax.fori_loop(0, nq, finish, (zero, zero))
        dg_ref[0:1, :] += dgq
        dg_ref[1:2, :] += dgk

    assert pairs in (1, npairs)
    col, vec, seq, stat = _attn_specs(t, pairs, stride, nq, bq)
    ins = [src] * (3 * pairs) + [gq, gk] + [o] * pairs + [do] * pairs + [lse]
    specs = [col(off + pp) for off in offs for pp in range(pairs)] + [vec, vec] + [col(pp) for pp in range(pairs)] * 2 + [stat]
    if rope is not None:
        ins += list(rope)
        specs += [seq, seq]
    sds = jax.ShapeDtypeStruct((bs, t, LANES * npairs // pairs), gdt)
    out_shape = [sds] * (3 * pairs) + [jax.ShapeDtypeStruct((8, LANES), F32)]
    ospec = pl.BlockSpec((None, t, LANES), lambda z, p, c: (z, 0, p))
    out_specs = [ospec] * (3 * pairs) + [pl.BlockSpec((8, LANES), lambda z, p, c: (0, 0))]
    if biased:
        kbspec = pl.BlockSpec((None, 2, nk, 1, bk), lambda z, p, c: (z, p, 0, 0, 0))
        ins.append(kbias)
        specs.append(kbspec)
        out_shape.append(jax.ShapeDtypeStruct(kbias.shape, F32))
        out_specs.append(kbspec)
    scratch = [pltpu.VMEM((pairs, n, LANES), BF16), pltpu.VMEM((2 * pairs, n, LANES), BF16), pltpu.VMEM((pairs, n, LANES), BF16)]
    scratch += [pltpu.VMEM((pairs, nk, LANES, bk), BF16), pltpu.VMEM((pairs, n, LANES), F32)]
    scratch += [pltpu.VMEM((2 * pairs, n, LANES), F32), pltpu.VMEM((pairs, n, LANES), F32)]
    scratch += [pltpu.VMEM((2 * pairs, LANES, bq), F32), pltpu.VMEM((2 * pairs, nq, 1, bq), F32)]
    res = _call(
        body, cargo=cargo, name=name, grid=(bs, npairs // pairs, stride), in_specs=specs, out_specs=out_specs, out_shape=out_shape,
        scratch_shapes=scratch, compiler_params=_params(("arbitrary", "arbitrary", "arbitrary")),
    )(*ins)
    return (list(res[:pairs]), list(res[pairs:2 * pairs]), list(res[2 * pairs:3 * pairs]), *res[3 * pairs:])


SB_LOG_PARTS = 2
SB_GRAD_PARTS = 1


def _log_sig_pair(z):
    lsn = jnp.minimum(-z, 0.0) - jnp.log(1.0 + jnp.exp(-jnp.abs(z)))
    return lsn, z + lsn


def _sb_fwd(src, offs, npairs, *, name, cargo=None):
    zs, n, _ = src.shape
    bq, bk = _att_blocks(n)
    nq, nk, rq = n // bq, n // bk, bq // bk

    def body(q_ref, k_ref, v_ref, o_ref, lt_ref, qs_s, kb_s, vt_s, acc_s, c_s):
        m_a = _mask_a()
        masks = (m_a, jnp.logical_not(m_a))

        def prep(c):
            rows = _rows(c, bk)
            qs_s[rows, :] = (q_ref[rows, :] * SCALE).astype(BF16)
            kb_s[rows, :] = k_ref[rows, :].astype(BF16)
            vt_s[c] = jnp.transpose(v_ref[rows, :]).astype(BF16)

        _loop(0, nk, prep)
        qk = _q_minus_k(bk, bq)
        u_gt = _tri(bk, lambda r, c: c > r)

        def qblock(qi):
            rows = _rows(qi, bq)
            qb = qs_s[rows, :]
            qms = [jnp.where(masks[h], qb, 0) for h in (0, 1)]
            acc_s[...] = jnp.zeros_like(acc_s)
            c_s[...] = jnp.zeros_like(c_s)

            def step(kj, masked):
                kb, vt = kb_s[_rows(kj, bk), :], vt_s[kj]
                zts = [lax.dot_general(kb, qms[h], NT, preferred_element_type=F32) for h in (0, 1)]
                old = [c_s[h] for h in (0, 1)]
                if masked:
                    ok = (qk + (qi * bq - kj * bk)) > 0
                new = []
                for h in (0, 1):
                    lsn, lsp = _log_sig_pair(zts[h])
                    if masked:
                        lsn = jnp.where(ok, lsn, 0.0)
                    at = jnp.exp(lsp + (old[h] + _cumdot_left(u_gt, lsn, SB_LOG_PARTS)))
                    if masked:
                        at = jnp.where(ok, at, 0.0)
                    new.append((jnp.dot(vt, at.astype(BF16), preferred_element_type=F32),
                                old[h] + jnp.sum(lsn, axis=0, keepdims=True)))
                for h in (0, 1):
                    acc_s[h] += new[h][0]
                    c_s[h] = new[h][1]

            _loop(0, rq, lambda t: step((qi + 1) * rq - 1 - t, True))
            _loop(0, qi * rq, lambda t: step(qi * rq - 1 - t, False))
            o_ref[rows, :] = jnp.where(m_a, jnp.transpose(acc_s[0]), jnp.transpose(acc_s[1]))
            for h in (0, 1):
                lt_ref[h, qi] = c_s[h]

        _loop(0, nq, qblock)

    scratch = [pltpu.VMEM((n, LANES), BF16)] * 2 + [pltpu.VMEM((nk, LANES, bk), BF16)]
    scratch += [pltpu.VMEM((2, LANES, bq), F32), pltpu.VMEM((2, 1, bq), F32)]
    return _call(
        body, cargo=cargo, name=name, grid=(zs, npairs),
        in_specs=[_col_spec(n, offs[0]), _col_spec(n, offs[1]), _col_spec(n, offs[2])],
        out_specs=[_col_spec(n, 0), _stat_spec(nq, bq)],
        out_shape=[jax.ShapeDtypeStruct((zs, n, LANES * npairs), F32), jax.ShapeDtypeStruct((zs, 2 * npairs, nq, 1, bq), F32)],
        scratch_shapes=scratch, compiler_params=_params(("parallel", "parallel")),
    )(src, src, src)


def _sb_bwd(src, offs, npairs, do, ltot, *, name, cargo=None):
    zs, n, _ = src.shape
    bq, bk = _att_blocks(n)
    nq, nk, rq = n // bq, n // bk, bq // bk

    def body(q_ref, k_ref, v_ref, do_ref, lt_ref, dq_ref, dk_ref, dv_ref, qs_s, kb_s, vb_s, kt_s, dk_s, dv_s, dq_s, lp_s, ep_s):
        m_a = _mask_a()
        masks = (m_a, jnp.logical_not(m_a))

        def prep(c):
            rows = _rows(c, bk)
            qs_s[rows, :] = (q_ref[rows, :] * SCALE).astype(BF16)
            kv = k_ref[rows, :]
            kb_s[rows, :] = kv.astype(BF16)
            kt_s[c] = jnp.transpose(kv).astype(BF16)
            vb_s[rows, :] = v_ref[rows, :].astype(BF16)

        _loop(0, nk, prep)
        dk_s[...] = jnp.zeros_like(dk_s)
        dv_s[...] = jnp.zeros_like(dv_s)
        qk = _q_minus_k(bk, bq)
        u_le = _tri(bk, lambda r, c: c <= r)
        u_lt = _tri(bk, lambda r, c: c < r)

        def qblock(qi):
            rows = _rows(qi, bq)
            qb = qs_s[rows, :]
            dob = do_ref[rows, :]
            qms = [jnp.where(masks[h], qb, 0) for h in (0, 1)]
            doms = [jnp.where(masks[h], dob, 0.0).astype(BF16) for h in (0, 1)]
            lts = [lt_ref[h, qi] for h in (0, 1)]
            dq_s[...] = jnp.zeros_like(dq_s)
            lp_s[...] = jnp.zeros_like(lp_s)
            ep_s[...] = jnp.zeros_like(ep_s)

            def step(kj, masked):
                cols = _rows(kj, bk)
                kb, vb, kt = kb_s[cols, :], vb_s[cols, :], kt_s[kj]
                zts = [lax.dot_general(kb, qms[h], NT, preferred_element_type=F32) for h in (0, 1)]
                dats = [lax.dot_general(vb, doms[h], NT, preferred_element_type=F32) for h in (0, 1)]
                old = [(lp_s[h], ep_s[h]) for h in (0, 1)]
                if masked:
                    ok = (qk + (qi * bq - kj * bk)) > 0
                new = []
                for h in (0, 1):
                    lp, ep = old[h]
                    lsn, lsp = _log_sig_pair(zts[h])
                    sig = jnp.exp(lsp)
                    if masked:
                        lsn = jnp.where(ok, lsn, 0.0)
                    at = jnp.exp(lsp + (lts[h] - (lp + _cumdot_left(u_le, lsn, SB_LOG_PARTS))))
                    if masked:
                        at = jnp.where(ok, at, 0.0)
                    et = dats[h] * at
                    big_e = ep + _cumdot_left(u_lt, et, SB_GRAD_PARTS)
                    dzt = et - sig * (et + big_e)
                    if masked:
                        dzt = jnp.where(ok, dzt, 0.0)
                    dzb = dzt.astype(BF16)
                    new.append((jnp.dot(dzb, qms[h], preferred_element_type=F32),
                                jnp.dot(at.astype(BF16), doms[h], preferred_element_type=F32),
                                jnp.dot(kt, dzb, preferred_element_type=F32),
                                lp + jnp.sum(lsn, axis=0, keepdims=True), ep + jnp.sum(et, axis=0, keepdims=True)))
                for h in (0, 1):
                    dq_s[h] += new[h][2]
                    lp_s[h], ep_s[h] = new[h][3], new[h][4]
                dk_s[cols, :] += new[0][0] + new[1][0]
                dv_s[cols, :] += new[0][1] + new[1][1]

            _loop(0, qi * rq, lambda kj: step(kj, False))
            _loop(qi * rq, (qi + 1) * rq, lambda kj: step(kj, True))
            dq_ref[rows, :] = (jnp.where(m_a, jnp.transpose(dq_s[0]), jnp.transpose(dq_s[1])) * SCALE).astype(BF16)

        _loop(0, nq, qblock)

        def store(c):
            rows = _rows(c, bk)
            dk_ref[rows, :] = dk_s[rows, :].astype(BF16)
            dv_ref[rows, :] = dv_s[rows, :].astype(BF16)

        _loop(0, nk, store)

    ospec = _col_spec(n, 0)
    sds = jax.ShapeDtypeStruct((zs, n, LANES * npairs), BF16)
    scratch = [pltpu.VMEM((n, LANES), BF16)] * 3 + [pltpu.VMEM((nk, LANES, bk), BF16)] + [pltpu.VMEM((n, LANES), F32)] * 2
    scratch += [pltpu.VMEM((2, LANES, bq), F32), pltpu.VMEM((2, 1, bq), F32), pltpu.VMEM((2, 1, bq), F32)]
    return _call(
        body, cargo=cargo, name=name, grid=(zs, npairs),
        in_specs=[_col_spec(n, offs[0]), _col_spec(n, offs[1]), _col_spec(n, offs[2]), ospec, _stat_spec(nq, bq)],
        out_specs=[ospec] * 3, out_shape=[sds] * 3, scratch_shapes=scratch,
        compiler_params=_params(("parallel", "parallel")),
    )(src, src, src, do, ltot)


def _fox_gate_fwd(lg, bias, *, name):
    bs, nh, t = lg.shape
    blk = min(LANES, t)

    def body(lg_ref, b_ref, kb_ref):
        u_le = _tri(blk, lambda r, c: r <= c)
        carry = jnp.zeros((nh, 1), F32)
        for j in range(t // blk):
            sl = slice(j * blk, (j + 1) * blk)
            xv = lg_ref[:, sl] + b_ref[...]
            lf = jnp.minimum(xv, 0.0) - jnp.log(1.0 + jnp.exp(-jnp.abs(xv)))
            kb_ref[:, sl] = -(carry + _cumdot(lf, u_le, 3))
            carry = carry + jnp.sum(lf, axis=1, keepdims=True)

    spec = pl.BlockSpec((None, nh, t), lambda i: (i, 0, 0))
    return _pcall(
        body, name=name, grid=(bs,), in_specs=[spec, pl.BlockSpec((nh, 1), lambda i: (0, 0))], out_specs=spec,
        out_shape=jax.ShapeDtypeStruct((bs, nh, t), F32), compiler_params=_params(("parallel",)),
    )(lg, bias)


def _fox_gate_bwd(dkb, lg, bias, *, name):
    bs, nh, t = lg.shape
    blk = min(LANES, t)

    def body(dkb_ref, lg_ref, b_ref, dlg_ref, db_ref):
        @pl.when(pl.program_id(0) == 0)
        def _():
            db_ref[...] = jnp.zeros_like(db_ref)

        u_ge = _tri(blk, lambda r, c: r >= c)
        carry = jnp.zeros((nh, 1), F32)
        tot = jnp.zeros((nh, 1), F32)
        for j in reversed(range(t // blk)):
            sl = slice(j * blk, (j + 1) * blk)
            df = -dkb_ref[:, sl]
            dlf = carry + _cumdot(df, u_ge, 3)
            carry = carry + jnp.sum(df, axis=1, keepdims=True)
            xv = lg_ref[:, sl] + b_ref[...]
            dlg = dlf * jax.nn.sigmoid(-xv)
            dlg_ref[:, sl] = dlg
            tot = tot + jnp.sum(dlg, axis=1, keepdims=True)
        db_ref[...] += jnp.broadcast_to(tot, db_ref.shape)

    spec = pl.BlockSpec((None, nh, t), lambda i: (i, 0, 0))
    return _pcall(
        body, name=name, grid=(bs,), in_specs=[spec, spec, pl.BlockSpec((nh, 1), lambda i: (0, 0))],
        out_specs=[spec, pl.BlockSpec((nh, LANES), lambda i: (0, 0))],
        out_shape=[jax.ShapeDtypeStruct((bs, nh, t), F32), jax.ShapeDtypeStruct((nh, LANES), F32)],
        compiler_params=_params(("arbitrary",)),
    )(dkb, lg, bias)


def _place():
    return lax.axis_index("x"), lax.axis_index("y"), lax.axis_index("c")


def _flip(v, f):
    return 1 - v if f else v


FLIPS = [(fx, fy, fc) for fx in (0, 1) for fy in (0, 1) for fc in (0, 1)][1:]


def _comm_sems(nw):
    return [pltpu.SemaphoreType.DMA((7, nw)), pltpu.SemaphoreType.DMA((7, nw)), pltpu.SemaphoreType.DMA((nw,))]


def _gather_cargo(shards, on_done):
    nw = len(shards)

    def parts(x_refs, out_refs, sems):
        send_sems, recv_sems, local_sems = sems
        x, y, cc = _place()
        me, sibling = (x, y, cc), (x, y, 1 - cc)
        chips = [(1 - x, y), (x, 1 - y), (1 - x, 1 - y)]

        def slot(i, px, py, pc):
            return out_refs[i].at[4 * px + 2 * py + pc]

        def copy(i, k, block, to, src=None):
            return pltpu.make_async_remote_copy(
                src_ref=slot(i, *block) if src is None else src, dst_ref=slot(i, *block),
                send_sem=send_sems.at[k, i], recv_sem=recv_sems.at[k, i], device_id=to, device_id_type=MESH)

        mine = [pltpu.make_async_copy(x_refs[i], slot(i, *me), local_sems.at[i]) for i in range(nw)]
        first = []
        for i in range(nw):
            first.append(copy(i, 0, me, sibling, src=x_refs[i]))
            first += [copy(i, 1 + j, me, (*chip, cc), src=x_refs[i]) for j, chip in enumerate(chips)]
        return me, sibling, chips, cc, copy, mine, first

    def start(x_refs, out_refs, sems):
        *_, mine, first = parts(x_refs, out_refs, sems)
        for cp in mine + first:
            cp.start()

    def finish(x_refs, out_refs, sems):
        me, sibling, chips, cc, copy, mine, first = parts(x_refs, out_refs, sems)
        passed = []
        for i in range(nw):
            for j, chip in enumerate(chips):
                copy(i, 1 + j, (*chip, cc), me).wait_recv()
                passed.append(copy(i, 4 + j, (*chip, cc), sibling))
                passed[-1].start()
        for i in range(nw):
            copy(i, 0, sibling, me).wait_recv()
            for j, chip in enumerate(chips):
                copy(i, 4 + j, (*chip, 1 - cc), me).wait_recv()
        for cp in first + passed:
            cp.wait_send()
        for cp in mine:
            cp.wait()

    out_shape = [jax.ShapeDtypeStruct((N_DEV, *s.shape), s.dtype) for s in shards]
    return _Cargo(shards, out_shape, _comm_sems(nw), start, finish, on_done)


def _scatter_cargo(slots, prev, layer, depth, on_done, row0=0, rows=None):
    nw = len(slots)

    def parts(refs, recv_refs, sems):
        g_refs = refs[:nw]
        send_sems, recv_sems, local_sems = sems
        x, y, cc = _place()
        my = 4 * x + 2 * y + cc

        def dst(i):
            return recv_refs[i].at[my, layer, pl.ds(row0, slots[i].shape[1])]

        mine, copies = [], []
        for i in range(nw):
            mine.append(pltpu.make_async_copy(g_refs[i].at[my], dst(i), local_sems.at[i]))
            for k, (fx, fy, fc) in enumerate(FLIPS):
                px, py, pc = _flip(x, fx), _flip(y, fy), _flip(cc, fc)
                copies.append(pltpu.make_async_remote_copy(
                    src_ref=g_refs[i].at[4 * px + 2 * py + pc], dst_ref=dst(i),
                    send_sem=send_sems.at[k, i], recv_sem=recv_sems.at[k, i], device_id=(px, py, pc), device_id_type=MESH))
        return mine, copies

    def start(refs, recv_refs, sems):
        mine, copies = parts(refs, recv_refs, sems)
        for cp in mine + copies:
            cp.start()

    def finish(refs, recv_refs, sems):
        mine, copies = parts(refs, recv_refs, sems)
        for cp in copies:
            cp.wait_recv()
        for cp in copies:
            cp.wait_send()
        for cp in mine:
            cp.wait()

    ins, aliases = list(slots), {}
    for i, p in enumerate(prev):
        if p is not None:
            aliases[len(ins)] = i
            ins.append(p)
    out_shape = [jax.ShapeDtypeStruct((N_DEV, depth, rows or s.shape[1], s.shape[2]), s.dtype) for s in slots]
    return _Cargo(ins, out_shape, _comm_sems(nw), start, finish, on_done, aliases)


def _exchange(cargo, *, name):
    def body(*refs):
        c_in = len(cargo.ins)
        c_out = len(cargo.out_shape)
        cargo.start(refs[:c_in], refs[c_in:c_in + c_out], refs[c_in + c_out:])
        cargo.finish(refs[:c_in], refs[c_in:c_in + c_out], refs[c_in + c_out:])

    hbm = pl.BlockSpec(memory_space=pl.ANY)
    res = _pcall(
        body, name=name, in_specs=[hbm] * len(cargo.ins), out_specs=[hbm] * len(cargo.out_shape), out_shape=cargo.out_shape,
        scratch_shapes=cargo.sems, input_output_aliases=dict(cargo.aliases),
    )(*cargo.ins)
    cargo.on_done(list(res))


def _allreduce_small(blob, *, name):
    r, c = blob.shape

    def body(x_ref, out_ref, buf, send_sems, recv_sems):
        x, y, cc = _place()
        my = 4 * x + 2 * y + cc
        copies = []
        for k, (fx, fy, fc) in enumerate(FLIPS):
            peer = (_flip(x, fx), _flip(y, fy), _flip(cc, fc))
            copies.append(pltpu.make_async_remote_copy(
                src_ref=x_ref, dst_ref=buf.at[my], send_sem=send_sems.at[k], recv_sem=recv_sems.at[k],
                device_id=peer, device_id_type=MESH))
        for cp in copies:
            cp.start()
        buf[my] = x_ref[...]
        for cp in copies:
            cp.wait_recv()
        for cp in copies:
            cp.wait_send()
        acc = buf[0]
        for i in range(1, N_DEV):
            acc = acc + buf[i]
        out_ref[...] = acc

    vmem = pl.BlockSpec(memory_space=pltpu.VMEM)
    return _pcall(
        body, name=name, in_specs=[vmem], out_specs=vmem, out_shape=jax.ShapeDtypeStruct((r, c), F32),
        scratch_shapes=[pltpu.VMEM((N_DEV, r, c), F32), pltpu.SemaphoreType.DMA((7,)), pltpu.SemaphoreType.DMA((7,))],
    )(blob)


BIG = ("w_in", "w_mlp_in", "w_mlp_out", "w_up_fox", "w_up_sb", "w_up_dil", "w_out")
ROW_SHARDED = ("w_out", "w_mlp_out")
SMALL = ("attn_norm", "b_forget", "q_norm_fox", "k_norm_fox", "q_norm_dil", "k_norm_dil", "mlp_norm")
BLOB_ROWS = 512


def _pack(parts, dtype):
    flat = jnp.concatenate([p.reshape(-1).astype(dtype) for p in parts])
    size = -(-flat.shape[0] // (BLOB_ROWS * LANES)) * (BLOB_ROWS * LANES)
    return jnp.pad(flat, (0, size - flat.shape[0])).reshape(-1, LANES)


def _unpack(blob, shapes):
    flat = blob.reshape(-1)
    out, off = [], 0
    for shp in shapes:
        size = 1
        for s in shp:
            size *= s
        out.append(flat[off:off + size].reshape(shp))
        off += size
    return out


def _join_shards(name, sh):
    if name in ROW_SHARDED:
        return sh.reshape(-1, sh.shape[2])
    return jnp.transpose(sh, (1, 0, 2)).reshape(sh.shape[1], -1)


def _split_shards(name, full):
    a, b = full.shape
    if name in ROW_SHARDED:
        return full.reshape(N_DEV, a // N_DEV, b)
    return jnp.transpose(full.reshape(a, N_DEV, b // N_DEV), (1, 0, 2))


def _in_segments(d_in):
    o1 = 3 * W_FOX
    o2 = o1 + N_HEADS_FOX
    return (0, o1, 0), (o2, d_in, -N_HEADS_FOX), (o1, o2, d_in - o2)


def _join_w_in(sh, dp):
    b = sh.shape[2]
    pieces = []
    for s, e, _ in _in_segments(N_DEV * b):
        for j in range(s // b, (e - 1) // b + 1):
            pieces.append(sh[j, :, max(s, j * b) - j * b:min(e, (j + 1) * b) - j * b])
    pieces.append(jnp.zeros((sh.shape[1], dp - N_DEV * b), sh.dtype))
    return jnp.concatenate(pieces, axis=1)


def _split_w_in(gp, d_in):
    b = d_in // N_DEV
    shards = []
    for j in range(N_DEV):
        runs = []
        for s, e, shift in sorted(_in_segments(d_in)):
            lo, hi = max(s, j * b), min(e, (j + 1) * b)
            if lo < hi:
                runs.append(gp[:, lo + shift:hi + shift])
        shards.append(jnp.concatenate(runs, axis=1))
    return jnp.stack(shards)


def _stat_to_tokens(st, r, b):
    hh = st.shape[1]
    n = st.shape[2] * st.shape[4]
    return jnp.transpose(st.reshape(b, r, hh, n), (0, 2, 3, 1)).reshape(b, hh, n * r)


def _stat_to_streams(tok, r, blk):
    b, hh, t = tok.shape
    n = t // r
    return jnp.transpose(tok.reshape(b, hh, n, r), (0, 3, 1, 2)).reshape(b * r, hh, n // blk, 1, blk)


def _rope_tables(positions):
    half = HEAD_DIM // 2
    inv = 1.0 / (ROPE_THETA ** (jnp.arange(half, dtype=F32) / half))
    ang = positions.astype(F32)[..., None] * inv
    cos, sin = jnp.cos(ang), jnp.sin(ang)
    return jnp.tile(cos, (1, 1, 4)), jnp.tile(jnp.concatenate([-sin, sin], axis=-1), (1, 1, 2))


def _gain2(g):
    return jnp.tile(g.reshape(1, HEAD_DIM), (1, 2))


def _dil_offs(g):
    c0 = (P_DIL + g * W_DIL) // LANES
    return c0, c0 + W_DILQ // LANES, c0 + 2 * W_DILQ // LANES


def _dil_pairs(r):
    return N_HEADS_DIL // 2 if r > 1 else 1


def _layer_fwd(l, x, w, small, ropes, bl, t, cargo):
    n, d = x.shape
    s = {}
    s["x"] = x
    s["h"] = _rmsnorm_fwd(x, small["attn_norm"][l].reshape(1, d), name=f"norm_attn_fwd{l}")
    proj = _mm(s["h"], w["w_in"][l], tn=PROJ_TILE, name=f"mm_proj{l}")
    s["proj"] = proj
    dp = proj.shape[1]
    proj3 = proj.reshape(bl, t, dp)
    p_fg = P_GATE + 3 * d

    lg = jnp.transpose(proj3[:, :, p_fg:p_fg + N_HEADS_FOX], (0, 2, 1))
    s["lg"] = lg
    kb = _fox_gate_fwd(lg, small["b_forget"][l].reshape(N_HEADS_FOX, 1), name=f"fox_gate_fwd{l}")
    blk = _att_blocks(t, wide_keys=True)[1]
    kb5 = kb.reshape(bl, N_HEADS_FOX, t // blk, 1, blk)
    s["kb5"] = kb5
    gqf, gkf = _gain2(small["q_norm_fox"][l]), _gain2(small["k_norm_fox"][l])
    fo = P_FOX // LANES
    fox_offs = (fo, fo + W_FOX // LANES, fo + 2 * W_FOX // LANES)
    out_a, lse_a = _attn_fwd(proj3, fox_offs, N_HEADS_FOX // 2, gqf, gkf, kbias=kb5, window=t, name=f"fox_fwd{l}",
                             cargo=cargo.get("fox_fwd"))
    s["out_a"], s["lse_a"] = out_a, lse_a

    so = P_SB // LANES
    sb_offs = (so, so + W_SB // LANES, so + 2 * W_SB // LANES)
    out_b, lt_b = _sb_fwd(proj3, sb_offs, N_HEADS_SB // 2, name=f"sb_fwd{l}", cargo=cargo.get("sb_fwd"))
    s["out_b"], s["lt_b"] = out_b, lt_b

    gqd, gkd = _gain2(small["q_norm_dil"][l]), _gain2(small["k_norm_dil"][l])
    os_, lses = [], []
    for g, (window, r) in enumerate(DIL_PATTERNS):
        o_g, lse_g = _attn_fwd(proj3, _dil_offs(g), N_HEADS_DIL // 2, gqd, gkd, rope=ropes, window=window // r, stride=r,
                               pairs=_dil_pairs(r), name=f"dil_fwd{l}_{g}")
        os_.append(o_g.reshape(n, W_DIL))
        lses.append(_stat_to_tokens(lse_g, r, bl).reshape(bl * N_HEADS_DIL, t))
    lse_c, *ws = _dil_weights(lses, name=f"dil_weights{l}")
    ws = [jnp.repeat(jnp.transpose(wg.reshape(bl, N_HEADS_DIL, t), (0, 2, 1)).reshape(n, N_HEADS_DIL), HEAD_DIM, axis=1) for wg in ws]
    out_c = _dil_mix(os_, ws, name=f"dil_mix{l}")
    s["out_c"], s["lse_c"] = out_c, lse_c.reshape(bl, N_HEADS_DIL, t)

    ys = [_mm(out_a.reshape(n, W_FOX), w["w_up_fox"][l], name=f"mm_up_fox{l}"),
          _mm(out_b.reshape(n, W_SB), w["w_up_sb"][l], name=f"mm_up_sb{l}"),
          _mm(out_c, w["w_up_dil"][l], name=f"mm_up_dil{l}")]
    s["ys"] = ys
    s["merged"] = _gate_merge_fwd(proj, ys, name=f"gate_merge_fwd{l}")
    x1 = _mm(s["merged"], w["w_out"][l], add=x, name=f"mm_out{l}")
    s["x1"] = x1

    s["h2"] = _rmsnorm_fwd(x1, small["mlp_norm"][l].reshape(1, d), name=f"norm_mlp_fwd{l}")
    s["u"], s["a"] = _mm(s["h2"], w["w_mlp_in"][l], relu2=True, out_dtype=BF16, name=f"mm_mlp_in{l}")
    x2 = _mm(s["a"], w["w_mlp_out"][l], add=x1, name=f"mm_mlp_out{l}")
    return x2, s


def _layer_bwd(l, dx2, s, w, small, ropes, bl, t, hooks):
    n, d = dx2.shape
    gw, gs = {}, {}

    def cargo(call):
        return hooks[call](gw) if call in hooks else None
    du = _mm(dx2, w["w_mlp_out"][l], tb=True, relu_grad_of=s["u"], out_dtype=BF16, name=f"mm_du{l}")
    gw["w_mlp_out"] = _mm(s["a"], dx2, ta=True, name=f"mm_dw_mlp_out{l}")
    gw["w_mlp_in"] = _mm(s["h2"], du, ta=True, name=f"mm_dw_mlp_in{l}")
    dh2 = _mm(du, w["w_mlp_in"][l], tb=True, name=f"mm_dh2{l}")
    dx1, gs["mlp_norm"] = _rmsnorm_bwd(s["x1"], small["mlp_norm"][l].reshape(1, d), dh2, dx2, name=f"norm_mlp_bwd{l}")

    dmerged = _mm(dx1, w["w_out"][l], tb=True, name=f"mm_dmerged{l}")
    gw["w_out"] = _mm(s["merged"], dx1, ta=True, name=f"mm_dw_out{l}")
    dya, dyb, dyc, dgl0, dgl1, dgl2 = _gate_merge_bwd(s["proj"], s["ys"], dmerged, name=f"gate_merge_bwd{l}")
    out_a2, out_b2 = s["out_a"].reshape(n, W_FOX), s["out_b"].reshape(n, W_SB)
    gw["w_up_fox"] = _mm(out_a2, dya, ta=True, name=f"mm_dw_up_fox{l}")
    gw["w_up_sb"] = _mm(out_b2, dyb, ta=True, name=f"mm_dw_up_sb{l}")
    gw["w_up_dil"] = _mm(s["out_c"], dyc, ta=True, name=f"mm_dw_up_dil{l}")
    dout_a = _mm(dya, w["w_up_fox"][l], tb=True, name=f"mm_dout_a{l}").reshape(bl, t, W_FOX)
    dout_b = _mm(dyb, w["w_up_sb"][l], tb=True, name=f"mm_dout_b{l}").reshape(bl, t, W_SB)
    dout_c = _mm(dyc, w["w_up_dil"][l], tb=True, name=f"mm_dout_c{l}").reshape(bl, t, W_DIL)

    proj3 = s["proj"].reshape(bl, t, -1)
    gqf, gkf = _gain2(small["q_norm_fox"][l]), _gain2(small["k_norm_fox"][l])
    fo = P_FOX // LANES
    fox_offs = (fo, fo + W_FOX // LANES, fo + 2 * W_FOX // LANES)
    (dq_a,), (dk_a,), (dv_a,), dg_a, dkb5 = _attn_bwd(proj3, fox_offs, N_HEADS_FOX // 2, gqf, gkf, s["out_a"], dout_a, s["lse_a"],
                                             kbias=s["kb5"], window=t, name=f"fox_bwd{l}", cargo=cargo("fox_bwd"))
    gs["fox_gains"] = dg_a
    dlg, gs["b_forget"] = _fox_gate_bwd(dkb5.reshape(bl, N_HEADS_FOX, t), s["lg"], small["b_forget"][l].reshape(N_HEADS_FOX, 1),
                                        name=f"fox_gate_bwd{l}")
    so = P_SB // LANES
    sb_offs = (so, so + W_SB // LANES, so + 2 * W_SB // LANES)
    dq_b, dk_b, dv_b = _sb_bwd(proj3, sb_offs, N_HEADS_SB // 2, dout_b, s["lt_b"], name=f"sb_bwd{l}", cargo=cargo("sb_bwd"))
    gqd, gkd = _gain2(small["q_norm_dil"][l]), _gain2(small["k_norm_dil"][l])
    out_c3 = s["out_c"].reshape(bl, t, W_DIL)
    dqs, dks, dvs, dgd = [], [], [], None
    for g, (window, r) in enumerate(DIL_PATTERNS):
        lse_g = _stat_to_streams(s["lse_c"], r, _att_blocks(t // r)[0])
        dq_g, dk_g, dv_g, dg_g = _attn_bwd(proj3, _dil_offs(g), N_HEADS_DIL // 2, gqd, gkd, out_c3, dout_c, lse_g, rope=ropes,
                                           window=window // r, stride=r, pairs=_dil_pairs(r), name=f"dil_bwd{l}_{g}")
        dqs += dq_g
        dks += dk_g
        dvs += dv_g
        dgd = dg_g if dgd is None else jnp.concatenate([dgd, dg_g], axis=0)
    gs["dil_gains"] = dgd

    dlg_cols = jnp.pad(jnp.transpose(dlg, (0, 2, 1)).reshape(n, N_HEADS_FOX), ((0, 0), (0, LANES - N_HEADS_FOX)))
    parts = [p.reshape(n, -1) for p in [dq_a, dk_a, dv_a, dq_b, dk_b, dv_b] + dqs + dks + dvs] + [dgl0, dgl1, dgl2, dlg_cols]
    dproj = _assemble_cols(parts, s["proj"].shape[1], name=f"assemble_dproj{l}")
    if "mm_dw_in_hi" in hooks:
        half = d // 2
        gw["w_in_lo"] = _mm(s["h"][:, :half], dproj, ta=True, tn=PROJ_TILE, name=f"mm_dw_in_lo{l}")
        gw["w_in_hi"] = _mm(s["h"][:, half:], dproj, ta=True, tn=PROJ_TILE, name=f"mm_dw_in_hi{l}", cargo=cargo("mm_dw_in_hi"))
    else:
        gw["w_in"] = _mm(s["h"], dproj, ta=True, tn=PROJ_TILE, name=f"mm_dw_in{l}")
    dh = _mm(dproj, w["w_in"][l], tb=True, tn=1024, tk=PROJ_TILE, name=f"mm_dh{l}", cargo=cargo("mm_dh"))
    dx, gs["attn_norm"] = _rmsnorm_bwd(s["x"], small["attn_norm"][l].reshape(1, d), dh, dx1, name=f"norm_attn_bwd{l}")
    return dx, gw, gs


def kernel(x, positions, attn_norm, w_in, b_forget, q_norm_fox, k_norm_fox, q_norm_dil, k_norm_dil, w_up_fox, w_up_sb, w_up_dil, w_out, mlp_norm, w_mlp_in, w_mlp_out, loss_target, m_attn_norm, m_w_in, m_b_forget, m_q_norm_fox, m_k_norm_fox, m_q_norm_dil, m_k_norm_dil, m_w_up_fox, m_w_up_sb, m_w_up_dil, m_w_out, m_mlp_norm, m_w_mlp_in, m_w_mlp_out, v_attn_norm, v_w_in, v_b_forget, v_q_norm_fox, v_k_norm_fox, v_q_norm_dil, v_k_norm_dil, v_w_up_fox, v_w_up_sb, v_w_up_dil, v_w_out, v_mlp_norm, v_w_mlp_in, v_w_mlp_out):
    bl, t, d = x.shape
    n = bl * t
    depth = attn_norm.shape[0]
    wl = dict(w_in=w_in, w_up_fox=w_up_fox, w_up_sb=w_up_sb, w_up_dil=w_up_dil, w_out=w_out, w_mlp_in=w_mlp_in, w_mlp_out=w_mlp_out)
    ml = dict(w_in=m_w_in, w_up_fox=m_w_up_fox, w_up_sb=m_w_up_sb, w_up_dil=m_w_up_dil, w_out=m_w_out, w_mlp_in=m_w_mlp_in, w_mlp_out=m_w_mlp_out)
    vl = dict(w_in=v_w_in, w_up_fox=v_w_up_fox, w_up_sb=v_w_up_sb, w_up_dil=v_w_up_dil, w_out=v_w_out, w_mlp_in=v_w_mlp_in, w_mlp_out=v_w_mlp_out)
    small = dict(attn_norm=attn_norm, b_forget=b_forget, q_norm_fox=q_norm_fox, k_norm_fox=k_norm_fox, q_norm_dil=q_norm_dil,
                 k_norm_dil=k_norm_dil, mlp_norm=mlp_norm)
    m_small = dict(attn_norm=m_attn_norm, b_forget=m_b_forget, q_norm_fox=m_q_norm_fox, k_norm_fox=m_k_norm_fox,
                   q_norm_dil=m_q_norm_dil, k_norm_dil=m_k_norm_dil, mlp_norm=m_mlp_norm)
    v_small = dict(attn_norm=v_attn_norm, b_forget=v_b_forget, q_norm_fox=v_q_norm_fox, k_norm_fox=v_k_norm_fox,
                   q_norm_dil=v_q_norm_dil, k_norm_dil=v_k_norm_dil, mlp_norm=v_mlp_norm)

    d_in = w_in.shape[-1] * N_DEV
    dp = -(-d_in // 512) * 512
    rest = [k for k in BIG if k != "w_in"]
    w = {k: [None] * depth for k in BIG}

    def gather(items):
        def done(res):
            for (k, l), sh in zip(items, res):
                w[k][l] = _join_w_in(sh, dp) if k == "w_in" else _join_shards(k, sh)

        return _gather_cargo([wl[k][l].astype(BF16) for k, l in items], done)

    _exchange(gather([("w_in", 0)]), name="gather_first")
    cos, sin = _rope_tables(positions)
    ropes = (cos, sin)

    h = x.reshape(n, d)
    saved = []
    for l in range(depth):
        cargo = {"fox_fwd": gather([(k, l) for k in rest])}
        if l + 1 < depth:
            cargo["sb_fwd"] = gather([("w_in", l + 1)])
        h, s = _layer_fwd(l, h, w, small, ropes, bl, t, cargo)
        saved.append(s)
    dy, loss_part = _loss_head(h, loss_target.reshape(n, d), name="loss_head")

    recv = {}

    def scatter(names, l, grads):
        def done(res):
            recv.update(zip(names, res))

        slots = [(_split_w_in(grads[k], d_in) if k == "w_in" else _split_shards(k, grads[k])).astype(BF16) for k in names]
        return _scatter_cargo(slots, [recv.get(k) for k in names], l, depth, done)

    def scatter_w_in_rows(part, l, row0):
        def done(res):
            recv["w_in"] = res[0]

        return _scatter_cargo([_split_w_in(part, d_in).astype(BF16)], [recv.get("w_in")], l, depth, done, row0=row0, rows=d)

    gss = [None] * depth
    above = None
    for l in reversed(range(depth)):
        hooks = {"fox_bwd": lambda gw, l=l: scatter(rest, l, gw)}
        if above is not None:
            hooks["sb_bwd"] = lambda gw, l=l, g=above: scatter(["w_in"], l + 1, g)
        if l == 0:
            hooks["mm_dw_in_hi"] = lambda gw: scatter_w_in_rows(gw["w_in_lo"], 0, 0)
            hooks["mm_dh"] = lambda gw: scatter_w_in_rows(gw["w_in_hi"], 0, d // 2)
        dy, above, gss[l] = _layer_bwd(l, dy, saved[l], w, small, ropes, bl, t, hooks)
    grad_x = dy.reshape(bl, t, d)

    g_big, d_big, m_big, v_big = {}, {}, {}, {}
    for k in BIG:
        g_big[k], d_big[k], m_big[k], v_big[k] = _adamw(recv[k], wl[k], ml[k], vl[k], name=f"adamw_{k}")

    rows = [loss_part]
    for l in range(depth):
        gs = gss[l]
        rows += [gs["attn_norm"].reshape(-1, LANES), gs["mlp_norm"].reshape(-1, LANES), gs["fox_gains"], gs["dil_gains"], gs["b_forget"]]
    row_counts = [r.shape[0] for r in rows]
    part = jnp.concatenate(rows, axis=0)
    pad_rows = -(-part.shape[0] // 8) * 8 - part.shape[0]
    summed = _allreduce_small(jnp.pad(part, ((0, pad_rows), (0, 0))), name="allreduce_small")
    pieces, off = [], 0
    for c in row_counts:
        pieces.append(summed[off:off + c])
        off += c
    loss = pieces[0][0, 0]

    def fold(row):
        return row[:HEAD_DIM] + row[HEAD_DIM:]

    g_small = {k: [] for k in SMALL}
    for l in range(depth):
        an, mn, fg, dg, bf = pieces[1 + 5 * l:6 + 5 * l]
        g_small["attn_norm"].append(an.reshape(d))
        g_small["mlp_norm"].append(mn.reshape(d))
        g_small["q_norm_fox"].append(fold(fg[0]))
        g_small["k_norm_fox"].append(fold(fg[1]))
        g_small["q_norm_dil"].append(fold(dg[0]) + fold(dg[8]) + fold(dg[16]))
        g_small["k_norm_dil"].append(fold(dg[1]) + fold(dg[9]) + fold(dg[17]))
        g_small["b_forget"].append(bf[:, 0])
    g_small = {k: jnp.stack(vs) for k, vs in g_small.items()}
    small_shapes = [small[k].shape for k in SMALL]
    outs = _adamw(_pack([g_small[k] for k in SMALL], F32)[None, None], _pack([small[k] for k in SMALL], F32)[None],
                  _pack([m_small[k] for k in SMALL], F32)[None], _pack([v_small[k] for k in SMALL], F32)[None], name="adamw_small")
    g_sm, d_sm, m_sm, v_sm = (dict(zip(SMALL, _unpack(o, small_shapes))) for o in outs)

    order = ("attn_norm", "w_in", "b_forget", "q_norm_fox", "k_norm_fox", "q_norm_dil", "k_norm_dil", "w_up_fox", "w_up_sb",
             "w_up_dil", "w_out", "mlp_norm", "w_mlp_in", "w_mlp_out")
    res = [loss, grad_x]
    for big, sm in ((g_big, g_sm), (d_big, d_sm), (m_big, m_sm), (v_big, v_sm)):
        res += [big[k] if k in big else sm[k] for k in order]
    return tuple(res)
```

```python
import jax
import jax.numpy as jnp
from jax import lax
from jax.experimental import pallas as pl
from jax.experimental.pallas import tpu as pltpu

F32 = jnp.float32
BF16 = jnp.bfloat16

HEAD_DIM = 64
LANES = 128
N_HEADS_FOX = 8
N_HEADS_SB = 8
N_HEADS_DIL = 4
DIL_PATTERNS = ((128, 1), (512, 4), (2048, 16))
ROPE_THETA = 10000.0
EPS = 1e-6
SCALE = 0.125
W_FOX = N_HEADS_FOX * HEAD_DIM
W_SB = N_HEADS_SB * HEAD_DIM
W_DIL = N_HEADS_DIL * HEAD_DIM
W_DILQ = len(DIL_PATTERNS) * W_DIL
P_FOX = 0
P_SB = 3 * W_FOX
P_DIL = P_SB + 3 * W_SB
P_GATE = P_DIL + 3 * W_DILQ
N_DEV = 8
ATT_BLK = 256
ATT_BQ = 512
NEG = -1e30
VMEM_LIMIT = 56 * 1024 * 1024
ADAMW_BLOCK_ELEMS = 128 * 1024
PROJ_TILE = 2176

ADAM_LR = 0.001
ADAM_B1 = 0.9
ADAM_B2 = 0.999
ADAM_EPS = 1e-08
ADAM_WD = 0.01
ADAM_STEP = 10

NT = (((1,), (1,)), ((), ()))
MESH = pl.DeviceIdType.MESH


def _pcall(body, **kw):
    return pl.pallas_call(body, **kw)


def _params(sem=None):
    return pltpu.CompilerParams(dimension_semantics=sem, vmem_limit_bytes=VMEM_LIMIT)


class _Cargo:
    def __init__(self, ins, out_shape, sems, start, finish, on_done, aliases=None):
        self.ins, self.out_shape, self.sems = list(ins), list(out_shape), list(sems)
        self.start, self.finish, self.on_done, self.aliases = start, finish, on_done, dict(aliases or {})


def _call(body, *, cargo=None, name, grid=(), in_specs, out_specs, out_shape, scratch_shapes=(), compiler_params=None):
    if cargo is None:
        kw = dict(grid=grid) if grid else {}
        if compiler_params is not None:
            kw["compiler_params"] = compiler_params
        return _pcall(body, name=name, in_specs=in_specs, out_specs=out_specs, out_shape=out_shape,
                      scratch_shapes=list(scratch_shapes), **kw)
    single = not isinstance(out_shape, (list, tuple))
    o_specs, o_shape = ([out_specs], [out_shape]) if single else (list(out_specs), list(out_shape))
    n_in, n_out, n_scr = len(in_specs), len(o_shape), len(scratch_shapes)
    c_in, c_out = len(cargo.ins), len(cargo.out_shape)

    def wrapped(*refs):
        ins, cins = refs[:n_in], refs[n_in:n_in + c_in]
        o0 = n_in + c_in
        outs, couts = refs[o0:o0 + n_out], refs[o0 + n_out:o0 + n_out + c_out]
        s0 = o0 + n_out + c_out
        scr, sems = refs[s0:s0 + n_scr], refs[s0 + n_scr:]
        first = last = None
        for ax, size in enumerate(grid):
            pid = pl.program_id(ax)
            first = (pid == 0) if first is None else first & (pid == 0)
            last = (pid == size - 1) if last is None else last & (pid == size - 1)
        if first is None:
            cargo.start(cins, couts, sems)
            body(*ins, *outs, *scr)
            cargo.finish(cins, couts, sems)
            return

        @pl.when(first)
        def _():
            cargo.start(cins, couts, sems)

        body(*ins, *outs, *scr)

        @pl.when(last)
        def _():
            cargo.finish(cins, couts, sems)

    hbm = pl.BlockSpec(memory_space=pl.ANY)
    kw = dict(grid=grid, compiler_params=_params(("arbitrary",) * len(grid))) if grid else {}
    call = _pcall(
        wrapped, name=name, in_specs=list(in_specs) + [hbm] * c_in, out_specs=o_specs + [hbm] * c_out,
        out_shape=o_shape + cargo.out_shape, scratch_shapes=list(scratch_shapes) + cargo.sems,
        input_output_aliases={n_in + i: n_out + j for i, j in cargo.aliases.items()}, **kw)

    def run(*args):
        res = call(*args, *cargo.ins)
        cargo.on_done(list(res[n_out:]))
        return res[0] if single else list(res[:n_out])

    return run


def _tile(dim, target, mult=LANES):
    t = (min(dim, target) // mult) * mult
    while t >= mult:
        if dim % t == 0:
            return t
        t -= mult
    return dim


def _mm(a, b, *, ta=False, tb=False, add=None, relu2=False, relu_grad_of=None, out_dtype=F32, name, tm=1024, tn=1024,
        tk=1024, cargo=None):
    m, k = (a.shape[1], a.shape[0]) if ta else a.shape
    n = b.shape[0] if tb else b.shape[1]
    tm, tn, tk = _tile(m, tm), _tile(n, tn), _tile(k, tk)
    nk = k // tk
    dn = (((0,) if ta else (1,), (1,) if tb else (0,)), ((), ()))

    extra = add if add is not None else relu_grad_of

    def body(*refs):
        a_ref, b_ref = refs[:2]
        x_ref = refs[2] if extra is not None else None
        outs = refs[2 + (extra is not None):-1]
        acc = refs[-1]
        kk = pl.program_id(2)
        part = lax.dot_general(a_ref[...].astype(BF16), b_ref[...].astype(BF16), dn, preferred_element_type=F32)

        def finish(r):
            if add is not None:
                r = r + x_ref[...]
            if relu_grad_of is not None:
                r = r * (2.0 * jnp.maximum(x_ref[...].astype(F32), 0.0))
            outs[0][...] = r.astype(out_dtype)
            if relu2:
                rr = jnp.maximum(r, 0.0)
                outs[1][...] = (rr * rr).astype(BF16)

        if nk == 1:
            finish(part)
            return

        @pl.when(kk == 0)
        def _():
            acc[...] = part

        @pl.when((kk > 0) & (kk < nk - 1))
        def _():
            acc[...] += part

        @pl.when(kk == nk - 1)
        def _():
            finish(acc[...] + part)

    a_spec = pl.BlockSpec((tk, tm), lambda i, j, q: (q, i)) if ta else pl.BlockSpec((tm, tk), lambda i, j, q: (i, q))
    b_spec = pl.BlockSpec((tn, tk), lambda i, j, q: (j, q)) if tb else pl.BlockSpec((tk, tn), lambda i, j, q: (q, j))
    o_spec = pl.BlockSpec((tm, tn), lambda i, j, q: (i, j))
    ins, specs = [a, b], [a_spec, b_spec]
    if extra is not None:
        ins.append(extra)
        specs.append(o_spec)
    sds = jax.ShapeDtypeStruct((m, n), out_dtype)
    return _call(
        body, cargo=cargo, name=name, grid=(m // tm, n // tn, nk), in_specs=specs,
        out_specs=[o_spec, o_spec] if relu2 else o_spec,
        out_shape=[sds, jax.ShapeDtypeStruct((m, n), BF16)] if relu2 else sds,
        scratch_shapes=[pltpu.VMEM((tm, tn) if nk > 1 else (8, LANES), F32)],
        compiler_params=_params(("parallel", "parallel", "arbitrary")),
    )(*ins)


def _rmsnorm_fwd(x, g, *, name):
    n, d = x.shape
    tm = _tile(n, 256, 8)

    def body(x_ref, g_ref, h_ref):
        xv = x_ref[...]
        inv = lax.rsqrt(jnp.mean(xv * xv, axis=1, keepdims=True) + EPS)
        h_ref[...] = (xv * inv * g_ref[...]).astype(BF16)

    row = pl.BlockSpec((tm, d), lambda i: (i, 0))
    return _pcall(
        body, name=name, grid=(n // tm,), in_specs=[row, pl.BlockSpec((1, d), lambda i: (0, 0))], out_specs=row,
        out_shape=jax.ShapeDtypeStruct((n, d), BF16), compiler_params=_params(("parallel",)),
    )(x, g)


def _rmsnorm_bwd(x, g, dh, dres, *, name):
    n, d = x.shape
    tm = _tile(n, 256, 8)

    def body(x_ref, g_ref, dh_ref, dres_ref, dx_ref, dg_ref):
        @pl.when(pl.program_id(0) == 0)
        def _():
            dg_ref[...] = jnp.zeros_like(dg_ref)

        xv = x_ref[...]
        inv = lax.rsqrt(jnp.mean(xv * xv, axis=1, keepdims=True) + EPS)
        y = xv * inv
        dhv = dh_ref[...]
        dg_ref[...] += jnp.sum(dhv * y, axis=0, keepdims=True)
        dy = dhv * g_ref[...]
        dx_ref[...] = dres_ref[...] + inv * (dy - y * jnp.mean(dy * y, axis=1, keepdims=True))

    row = pl.BlockSpec((tm, d), lambda i: (i, 0))
    vec = pl.BlockSpec((1, d), lambda i: (0, 0))
    return _pcall(
        body, name=name, grid=(n // tm,), in_specs=[row, vec, row, row], out_specs=[row, vec],
        out_shape=[jax.ShapeDtypeStruct((n, d), F32), jax.ShapeDtypeStruct((1, d), F32)],
        compiler_params=_params(("arbitrary",)),
    )(x, g, dh, dres)


def _gate_specs(n, d):
    bw = 256 if d % 256 == 0 else LANES
    tm = _tile(n, 512, 8)
    nb = d // bw
    yspec = pl.BlockSpec((tm, bw), lambda i, j: (i, j))
    gspecs = [pl.BlockSpec((tm, bw), lambda i, j, b=b: (i, P_GATE // bw + b * nb + j)) for b in range(3)]
    return tm, bw, nb, yspec, gspecs


def _gate_merge_fwd(proj, ys, *, name):
    n, d = ys[0].shape
    tm, bw, nb, yspec, gspecs = _gate_specs(n, d)

    def body(g0, g1, g2, y0, y1, y2, o_ref):
        acc = jax.nn.sigmoid(g0[...]) * y0[...]
        acc += jax.nn.sigmoid(g1[...]) * y1[...]
        acc += jax.nn.sigmoid(g2[...]) * y2[...]
        o_ref[...] = acc.astype(BF16)

    return _pcall(
        body, name=name, grid=(n // tm, nb), in_specs=gspecs + [yspec] * 3, out_specs=yspec,
        out_shape=jax.ShapeDtypeStruct((n, d), BF16), compiler_params=_params(("parallel", "parallel")),
    )(proj, proj, proj, *ys)


def _gate_merge_bwd(proj, ys, dmerged, *, name):
    n, d = ys[0].shape
    tm, bw, nb, yspec, gspecs = _gate_specs(n, d)

    def body(g0, g1, g2, y0, y1, y2, dm_ref, dy0, dy1, dy2, dgl0, dgl1, dgl2):
        dm = dm_ref[...]
        for g_ref, y_ref, dy_ref, dgl_ref in ((g0, y0, dy0, dgl0), (g1, y1, dy1, dgl1), (g2, y2, dy2, dgl2)):
            s = jax.nn.sigmoid(g_ref[...])
            dy_ref[...] = (dm * s).astype(BF16)
            dgl_ref[...] = (dm * y_ref[...] * s * (1.0 - s)).astype(BF16)

    sds = jax.ShapeDtypeStruct((n, d), BF16)
    return _pcall(
        body, name=name, grid=(n // tm, nb), in_specs=gspecs + [yspec] * 4, out_specs=[yspec] * 6,
        out_shape=[sds] * 6, compiler_params=_params(("parallel", "parallel")),
    )(proj, proj, proj, *ys, dmerged)


def _loss_head(y, tgt, *, name):
    n, d = y.shape
    tm = _tile(n, 256, 8)
    steps = n // tm

    def body(y_ref, t_ref, dy_ref, loss_ref, acc):
        i = pl.program_id(0)

        @pl.when(i == 0)
        def _():
            acc[...] = jnp.zeros_like(acc)

        e = y_ref[...] - t_ref[...]
        dy_ref[...] = e * (1.0 / d)
        acc[...] += jnp.sum(e * e, axis=0, keepdims=True)

        @pl.when(i == steps - 1)
        def _():
            tot = jnp.sum(acc[...], axis=1, keepdims=True) * (0.5 / d)
            loss_ref[...] = jnp.broadcast_to(tot, loss_ref.shape)

    row = pl.BlockSpec((tm, d), lambda i: (i, 0))
    return _pcall(
        body, name=name, grid=(steps,), in_specs=[row, row], out_specs=[row, pl.BlockSpec((8, LANES), lambda i: (0, 0))],
        out_shape=[jax.ShapeDtypeStruct((n, d), F32), jax.ShapeDtypeStruct((8, LANES), F32)],
        scratch_shapes=[pltpu.VMEM((1, d), F32)], compiler_params=_params(("arbitrary",)),
    )(y, tgt)


def _assemble_cols(parts, width, *, name):
    n = parts[0].shape[0]
    tm = _tile(n, 256, 16)
    widths = [p.shape[1] for p in parts]

    def body(*refs):
        o_ref = refs[-1]
        off = 0
        for ref, w in zip(refs[:-1], widths):
            o_ref[:, off:off + w] = ref[...].astype(BF16)
            off += w
        if off < width:
            o_ref[:, off:] = jnp.zeros((tm, width - off), BF16)

    return _pcall(
        body, name=name, grid=(n // tm,), in_specs=[pl.BlockSpec((tm, w), lambda i: (i, 0)) for w in widths],
        out_specs=pl.BlockSpec((tm, width), lambda i: (i, 0)), out_shape=jax.ShapeDtypeStruct((n, width), BF16),
        compiler_params=_params(("parallel",)),
    )(*parts)


def _dil_weights(lses, *, name):
    shp = lses[0].shape

    def body(l0, l1, l2, lse_ref, w0, w1, w2):
        a, b, c = l0[...], l1[...], l2[...]
        m = jnp.maximum(jnp.maximum(a, b), c)
        ea, eb, ec = jnp.exp(a - m), jnp.exp(b - m), jnp.exp(c - m)
        den = ea + eb + ec
        lse_ref[...] = m + jnp.log(den)
        w0[...] = ea / den
        w1[...] = eb / den
        w2[...] = ec / den

    vmem = pl.BlockSpec(memory_space=pltpu.VMEM)
    return _pcall(body, name=name, in_specs=[vmem] * 3, out_specs=[vmem] * 4, out_shape=[jax.ShapeDtypeStruct(shp, F32)] * 4)(*lses)


def _dil_mix(os_, ws, *, name):
    n, w = os_[0].shape
    tm = _tile(n, 512, 8)

    def body(o0, o1, o2, w0, w1, w2, out_ref):
        out_ref[...] = w0[...] * o0[...] + w1[...] * o1[...] + w2[...] * o2[...]

    spec = pl.BlockSpec((tm, w), lambda i: (i, 0))
    return _pcall(
        body, name=name, grid=(n // tm,), in_specs=[spec] * 6, out_specs=spec, out_shape=jax.ShapeDtypeStruct((n, w), F32),
        compiler_params=_params(("parallel",)),
    )(*os_, *ws)


def _adamw(gsrc, w, m, v, *, name):
    s, dep, a, b = gsrc.shape
    ta = _tile(a, max(16, (ADAMW_BLOCK_ELEMS // b) // 16 * 16), 16)
    c1 = 1.0 / (1.0 - ADAM_B1 ** ADAM_STEP)
    c2 = 1.0 / (1.0 - ADAM_B2 ** ADAM_STEP)

    def body(gs_ref, w_ref, m_ref, v_ref, g_ref, d_ref, m2_ref, v2_ref):
        g = gs_ref[0].astype(F32)
        for i in range(1, s):
            g = g + gs_ref[i].astype(F32)
        m2 = ADAM_B1 * m_ref[...] + (1.0 - ADAM_B1) * g
        v2 = ADAM_B2 * v_ref[...] + (1.0 - ADAM_B2) * (g * g)
        g_ref[...] = g
        m2_ref[...] = m2
        v2_ref[...] = v2
        d_ref[...] = -ADAM_LR * ((m2 * c1) / (jnp.sqrt(v2 * c2) + ADAM_EPS) + ADAM_WD * w_ref[...])

    spec = pl.BlockSpec((None, ta, b), lambda l, i: (l, i, 0))
    sds = jax.ShapeDtypeStruct((dep, a, b), F32)
    return _pcall(
        body, name=name, grid=(dep, a // ta),
        in_specs=[pl.BlockSpec((s, None, ta, b), lambda l, i: (0, l, i, 0)), spec, spec, spec],
        out_specs=[spec] * 4, out_shape=[sds] * 4, compiler_params=_params(("parallel", "parallel")),
    )(gsrc, w, m, v)


def _mask_a():
    return lax.broadcasted_iota(jnp.int32, (1, LANES), 1) < HEAD_DIM


def _half_sum(x, m_a):
    sa = jnp.sum(jnp.where(m_a, x, 0.0), axis=1, keepdims=True)
    sb = jnp.sum(jnp.where(m_a, 0.0, x), axis=1, keepdims=True)
    return jnp.where(m_a, sa, sb)


def _head_inv(x, m_a):
    return lax.rsqrt(_half_sum(x * x, m_a) * (1.0 / HEAD_DIM) + EPS)


def _swap32(x):
    first = (lax.broadcasted_iota(jnp.int32, (1, LANES), 1) % HEAD_DIM) < (HEAD_DIM // 2)
    return jnp.where(first, pltpu.roll(x, LANES - HEAD_DIM // 2, 1), pltpu.roll(x, HEAD_DIM // 2, 1))


def _tri(blk, rel):
    r = lax.broadcasted_iota(jnp.int32, (blk, blk), 0)
    c = lax.broadcasted_iota(jnp.int32, (blk, blk), 1)
    return jnp.where(rel(r, c), 1.0, 0.0).astype(BF16)


def _cumdot(x, u, parts):
    acc = None
    r = x
    for i in range(parts):
        xi = r.astype(BF16)
        t = jnp.dot(xi, u, preferred_element_type=F32)
        acc = t if acc is None else acc + t
        if i + 1 < parts:
            r = r - xi.astype(F32)
    return acc


def _rows(i, blk):
    return pl.ds(pl.multiple_of(i * blk, blk), blk)


def _att_blk(n):
    return ATT_BLK if n % ATT_BLK == 0 else min(LANES, n)


def _att_blocks(n, wide_keys=False):
    bk = _att_blk(n)
    bq = ATT_BQ if n % ATT_BQ == 0 else bk
    return bq, (bq if wide_keys else bk)


def _loop(lo, hi, fn):
    def it(i, c):
        fn(i)
        return c

    lax.fori_loop(lo, hi, it, 0)


def _normed(src, g_ref, rope_refs, rows, m_a):
    xv = src[rows, :]
    xn = xv * _head_inv(xv, m_a) * g_ref[...]
    if rope_refs is not None:
        xn = xn * rope_refs[0][rows, :] + _swap32(xn) * rope_refs[1][rows, :]
    return xn


def _bias_lane(h):
    return HEAD_DIM if h == 0 else 0


def _k_for_head(kn, kb_row, h, m_h, lane, blk):
    out = jnp.where(m_h, kn, 0.0)
    if kb_row is not None:
        col = jnp.transpose(jnp.broadcast_to(kb_row, (LANES, blk)))
        hi = col.astype(BF16).astype(F32)
        mid = (col - hi).astype(BF16).astype(F32)
        lo = col - hi - mid
        b = _bias_lane(h)
        out = jnp.where(lane == b, hi, jnp.where(lane == b + 1, mid, jnp.where(lane == b + 2, lo, out)))
    return out.astype(BF16)


def _q_for_head(qb, h, m_h, lane, biased):
    out = jnp.where(m_h, qb, 0)
    if biased:
        b = _bias_lane(h)
        out = jnp.where((lane >= b) & (lane < b + 3), jnp.ones_like(out), out)
    return out


def _head_rows(x, parts=3):
    rr = lax.broadcasted_iota(jnp.int32, (8, LANES), 0)
    ll = lax.broadcasted_iota(jnp.int32, (8, LANES), 1)
    sel = jnp.where(((rr == 0) & (ll < HEAD_DIM)) | ((rr == 1) & (ll >= HEAD_DIM)), 1.0, 0.0).astype(BF16)
    acc = None
    rem = x
    for i in range(parts):
        xi = rem.astype(BF16)
        t = lax.dot_general(sel, xi, NT, preferred_element_type=F32)
        acc = t if acc is None else acc + t
        if i + 1 < parts:
            rem = rem - xi.astype(F32)
    return acc


def _cumdot_left(u, x, parts):
    acc = None
    rem = x
    for i in range(parts):
        xi = rem.astype(BF16)
        t = jnp.dot(u, xi, preferred_element_type=F32)
        acc = t if acc is None else acc + t
        if i + 1 < parts:
            rem = rem - xi.astype(F32)
    return acc


def _q_minus_k(bk, bq):
    return lax.broadcasted_iota(jnp.int32, (bk, bq), 1) - lax.broadcasted_iota(jnp.int32, (bk, bq), 0)


def _stream_rows(stride):
    if stride == 1:
        return _rows
    c = pl.program_id(2)
    return lambda i, blk: pl.ds(c + i * (blk * stride), blk, stride=stride)


def _attn_specs(t, pairs, stride, nq, bq):
    col = lambda off: pl.BlockSpec((None, t, LANES), lambda z, p, c: (z, 0, off + p * pairs))
    vec = pl.BlockSpec((1, LANES), lambda z, p, c: (0, 0))
    seq = pl.BlockSpec((None, t, LANES), lambda z, p, c: (z, 0, 0))
    stat = pl.BlockSpec((None, 2 * pairs, nq, 1, bq), lambda z, p, c: (z * stride + c, p, 0, 0, 0))
    return col, vec, seq, stat


def _attn_fwd(src, offs, npairs, gq, gk, *, rope=None, kbias=None, window, stride=1, pairs=1, name, cargo=None):
    bs, t, _ = src.shape
    n = t // stride
    full = window >= n
    bq, bk = _att_blocks(n, wide_keys=full)
    nq, nk, rq = n // bq, n // bk, bq // bk
    wblk = -(-window // bk)
    biased = kbias is not None
    heads = range(2 * pairs)

    def body(*refs):
        it = iter(refs)
        q_refs, k_refs, v_refs = ([next(it) for _ in range(pairs)] for _ in range(3))
        gq_ref, gk_ref = next(it), next(it)
        rope_refs = (next(it), next(it)) if rope is not None else None
        kb_ref = next(it) if biased else None
        o_refs = [next(it) for _ in range(pairs)]
        lse_ref, qn_s, kh_s, vt_s, acc_s, m_s = (next(it) for _ in range(6))
        m_a = _mask_a()
        masks = (m_a, jnp.logical_not(m_a))
        lane = lax.broadcasted_iota(jnp.int32, (1, LANES), 1)
        row = lax.broadcasted_iota(jnp.int32, (LANES, 1), 0)
        tok = _stream_rows(stride)

        def prep(c):
            rows = _rows(c, bk)
            trows = tok(c, bk)
            for pp in range(pairs):
                qn = _normed(q_refs[pp], gq_ref, rope_refs, trows, m_a)
                qn_s[pp, rows, :] = (qn * SCALE).astype(BF16)
                kn = _normed(k_refs[pp], gk_ref, rope_refs, trows, m_a)
                vt = jnp.transpose(v_refs[pp][trows, :])
                for half in (0, 1):
                    h = 2 * pp + half
                    kh_s[h, rows, :] = _k_for_head(kn, kb_ref[h, c] if biased else None, half, masks[half], lane, bk)
                    vt_s[h, c] = jnp.where(row == _bias_lane(half), 1.0, vt).astype(BF16)

        _loop(0, nk, prep)
        qk = _q_minus_k(bk, bq)

        def qblock(qi):
            rows = _rows(qi, bq)
            qh = [_q_for_head(qn_s[h // 2, rows, :], h % 2, masks[h % 2], lane, biased) for h in heads]
            m_s[...] = jnp.full(m_s.shape, NEG, F32)
            acc_s[...] = jnp.zeros_like(acc_s)

            def step(kj, masked):
                cols = _rows(kj, bk)
                sts = [lax.dot_general(kh_s[h, cols, :], qh[h], NT, preferred_element_type=F32) for h in heads]
                old = [(m_s[h], acc_s[h]) for h in heads]
                if masked:
                    d = qk + (qi * bq - kj * bk)
                    ok = (d >= 0) & (d <= window)
                    sts = [jnp.where(ok, st, NEG) for st in sts]
                new = []
                for h in heads:
                    m, acc = old[h]
                    m2 = jnp.maximum(m, jnp.max(sts[h], axis=0, keepdims=True))
                    pt = jnp.exp(sts[h] - m2).astype(BF16)
                    new.append((m2, jnp.exp(m - m2) * acc + jnp.dot(vt_s[h, kj], pt, preferred_element_type=F32)))
                for h in heads:
                    m_s[h], acc_s[h] = new[h]

            if full:
                _loop(0, qi * rq, lambda kj: step(kj, False))
                _loop(qi * rq, (qi + 1) * rq, lambda kj: step(kj, True))
            else:
                _loop(jnp.maximum(qi * rq - wblk, 0), (qi + 1) * rq, lambda kj: step(kj, True))
            outs = []
            for h in heads:
                acc_t = acc_s[h]
                den = acc_t[_bias_lane(h % 2):_bias_lane(h % 2) + 1, :]
                outs.append(jnp.transpose(acc_t / den))
                lse_ref[h, qi] = m_s[h] + jnp.log(den)
            for pp in range(pairs):
                o_refs[pp][tok(qi, bq), :] = jnp.where(m_a, outs[2 * pp], outs[2 * pp + 1])

        _loop(0, nq, qblock)

    col, vec, seq, stat = _attn_specs(t, pairs, stride, nq, bq)
    ins = [src] * (3 * pairs) + [gq, gk]
    specs = [col(off + pp) for off in offs for pp in range(pairs)] + [vec, vec]
    if rope is not None:
        ins += list(rope)
        specs += [seq, seq]
    if biased:
        ins.append(kbias)
        specs.append(pl.BlockSpec((None, 2, nk, 1, bk), lambda z, p, c: (z, p, 0, 0, 0)))
    scratch = [pltpu.VMEM((pairs, n, LANES), BF16), pltpu.VMEM((2 * pairs, n, LANES), BF16)]
    scratch += [pltpu.VMEM((2 * pairs, nk, LANES, bk), BF16), pltpu.VMEM((2 * pairs, LANES, bq), F32)]
    scratch += [pltpu.VMEM((2 * pairs, 1, bq), F32)]
    assert pairs in (1, npairs)
    ospec = pl.BlockSpec((None, t, LANES), lambda z, p, c: (z, 0, p))
    *os_, lse = _call(
        body, cargo=cargo, name=name, grid=(bs, npairs // pairs, stride), in_specs=specs, out_specs=[ospec] * pairs + [stat],
        out_shape=[jax.ShapeDtypeStruct((bs, t, LANES * npairs // pairs), F32)] * pairs
        + [jax.ShapeDtypeStruct((bs * stride, 2 * npairs, nq, 1, bq), F32)],
        scratch_shapes=scratch, compiler_params=_params(("parallel", "parallel", "arbitrary")),
    )(*ins)
    return (os_[0] if pairs == 1 else jnp.concatenate(os_, axis=-1)), lse


def _attn_bwd(src, offs, npairs, gq, gk, o, do, lse, *, rope=None, kbias=None, window, stride=1, pairs=1, name, cargo=None):
    bs, t, _ = src.shape
    n = t // stride
    full = window >= n
    bq, bk = _att_blocks(n, wide_keys=full)
    nq, nk, rq = n // bq, n // bk, bq // bk
    wblk = -(-window // bk)
    biased = kbias is not None
    heads = range(2 * pairs)
    gdt = BF16 if stride == 1 else F32

    def body(*refs):
        it = iter(refs)
        q_refs, k_refs, v_refs = ([next(it) for _ in range(pairs)] for _ in range(3))
        gq_ref, gk_ref = next(it), next(it)
        o_refs, do_refs = ([next(it) for _ in range(pairs)] for _ in range(2))
        lse_ref = next(it)
        rope_refs = (next(it), next(it)) if rope is not None else None
        kb_ref = next(it) if biased else None
        dq_refs, dk_refs, dv_refs = ([next(it) for _ in range(pairs)] for _ in range(3))
        dg_ref = next(it)
        dkb_ref = next(it) if biased else None
        qn_s, kh_s, vb_s, kt_s, dqn_s, dkh_s, dv_s, dq_s, rs_s = (next(it) for _ in range(9))
        m_a = _mask_a()
        masks = (m_a, jnp.logical_not(m_a))
        lane = lax.broadcasted_iota(jnp.int32, (1, LANES), 1)
        tok = _stream_rows(stride)

        @pl.when((pl.program_id(0) == 0) & (pl.program_id(1) == 0) & (pl.program_id(2) == 0))
        def _():
            dg_ref[...] = jnp.zeros_like(dg_ref)

        def prep(c):
            rows = _rows(c, bk)
            trows = tok(c, bk)
            for pp in range(pairs):
                qn = _normed(q_refs[pp], gq_ref, rope_refs, trows, m_a)
                qn_s[pp, rows, :] = (qn * SCALE).astype(BF16)
                kn = _normed(k_refs[pp], gk_ref, rope_refs, trows, m_a)
                kt_s[pp, c] = jnp.transpose(kn).astype(BF16)
                vb_s[pp, rows, :] = v_refs[pp][trows, :].astype(BF16)
                for half in (0, 1):
                    h = 2 * pp + half
                    kh_s[h, rows, :] = _k_for_head(kn, kb_ref[h, c] if biased else None, half, masks[half], lane, bk)

        _loop(0, nk, prep)
        dkh_s[...] = jnp.zeros_like(dkh_s)
        dv_s[...] = jnp.zeros_like(dv_s)
        qk = _q_minus_k(bk, bq)

        def qblock(qi):
            rows = _rows(qi, bq)
            trows = tok(qi, bq)
            dobs = [do_refs[pp][trows, :] for pp in range(pairs)]
            deltas = [_head_rows(dobs[pp] * o_refs[pp][trows, :]) for pp in range(pairs)]
            qh = [_q_for_head(qn_s[h // 2, rows, :], h % 2, masks[h % 2], lane, biased) for h in heads]
            doms = [jnp.where(masks[h % 2], dobs[h // 2], 0.0).astype(BF16) for h in heads]
            delta = [deltas[h // 2][h % 2:h % 2 + 1, :] for h in heads]
            lses = [lse_ref[h, qi] for h in heads]
            dq_s[...] = jnp.zeros_like(dq_s)
            if biased:
                for h in heads:
                    rs_s[h, qi] = jnp.zeros((1, bq), F32)

            def step(kj, masked):
                cols = _rows(kj, bk)
                vbs = [vb_s[pp, cols, :] for pp in range(pairs)]
                kts = [kt_s[pp, kj] for pp in range(pairs)]
                sts = [lax.dot_general(kh_s[h, cols, :], qh[h], NT, preferred_element_type=F32) for h in heads]
                dpts = [lax.dot_general(vbs[h // 2], doms[h], NT, preferred_element_type=F32) for h in heads]
                if masked:
                    d = qk + (qi * bq - kj * bk)
                    ok = (d >= 0) & (d <= window)
                    sts = [jnp.where(ok, st, NEG) for st in sts]
                new = []
                for h in heads:
                    pt = jnp.exp(sts[h] - lses[h])
                    dst = pt * (dpts[h] - delta[h])
                    dsb = dst.astype(BF16)
                    tk = jnp.dot(dsb, qh[h], preferred_element_type=F32)
                    if biased:
                        tk = tk + jnp.dot((dst - dsb.astype(F32)).astype(BF16), qh[h], preferred_element_type=F32)
                    tv = jnp.dot(pt.astype(BF16), doms[h], preferred_element_type=F32)
                    tq = jnp.dot(kts[h // 2], dsb, preferred_element_type=F32)
                    new.append((tk, tv, tq, jnp.sum(dst, axis=0, keepdims=True) if biased else None))
                for h in heads:
                    dkh_s[h, cols, :] += new[h][0]
                    dq_s[h] += new[h][2]
                    if biased:
                        rs_s[h, qi] += new[h][3]
                for pp in range(pairs):
                    dv_s[pp, cols, :] += new[2 * pp][1] + new[2 * pp + 1][1]

            if full:
                _loop(0, qi * rq, lambda kj: step(kj, False))
                _loop(qi * rq, (qi + 1) * rq, lambda kj: step(kj, True))
            else:
                _loop(jnp.maximum(qi * rq - wblk, 0), (qi + 1) * rq, lambda kj: step(kj, True))
            for pp in range(pairs):
                dqn_s[pp, rows, :] = jnp.where(m_a, jnp.transpose(dq_s[2 * pp]), jnp.transpose(dq_s[2 * pp + 1])) * SCALE

        _loop(0, nq, qblock)

        def finish(c, carry):
            rows = _rows(c, bq)
            trows = tok(c, bq)
            dgq, dgk = carry
            for pp in range(pairs):
                dk_pair = [dkh_s[2 * pp, rows, :], dkh_s[2 * pp + 1, rows, :]]
                if biased:
                    for half in (0, 1):
                        b = _bias_lane(half)
                        dkb_row = jnp.transpose(dk_pair[half])[b:b + 1, :] - rs_s[2 * pp + half, c]
                        for j in range(rq):
                            dkb_ref[2 * pp + half, c * rq + j] = dkb_row[:, j * bk:(j + 1) * bk]
                dv_refs[pp][trows, :] = dv_s[pp, rows, :].astype(gdt)
                grads = (dqn_s[pp, rows, :], jnp.where(m_a, dk_pair[0], dk_pair[1]))
                out = []
                for src_ref, g_ref, dxn, dst in ((q_refs[pp], gq_ref, grads[0], dq_refs[pp]),
                                                 (k_refs[pp], gk_ref, grads[1], dk_refs[pp])):
                    xv = src_ref[trows, :]
                    inv = _head_inv(xv, m_a)
                    y = xv * inv
                    if rope_refs is not None:
                        dxn = dxn * rope_refs[0][trows, :] + _swap32(dxn * rope_refs[1][trows, :])
                    dy = dxn * g_ref[...]
                    dst[trows, :] = (inv * (dy - y * (_half_sum(dy * y, m_a) * (1.0 / HEAD_DIM)))).astype(gdt)
                    out.append(jnp.sum(dxn * y, axis=0, keepdims=True))
                dgq, dgk = dgq + out[0], dgk + out[1]
            return dgq, dgk

        zero = jnp.zeros((1, LANES), F32)
        dgq, dgk = lax.fori_loop(0, nq, finish, (zero, zero))
        dg_ref[0:1, :] += dgq
        dg_ref[1:2, :] += dgk

    assert pairs in (1, npairs)
    col, vec, seq, stat = _attn_specs(t, pairs, stride, nq, bq)
    ins = [src] * (3 * pairs) + [gq, gk] + [o] * pairs + [do] * pairs + [lse]
    specs = [col(off + pp) for off in offs for pp in range(pairs)] + [vec, vec] + [col(pp) for pp in range(pairs)] * 2 + [stat]
    if rope is not None:
        ins += list(rope)
        specs += [seq, seq]
    sds = jax.ShapeDtypeStruct((bs, t, LANES * npairs // pairs), gdt)
    out_shape = [sds] * (3 * pairs) + [jax.ShapeDtypeStruct((8, LANES), F32)]
    ospec = pl.BlockSpec((None, t, LANES), lambda z, p, c: (z, 0, p))
    out_specs = [ospec] * (3 * pairs) + [pl.BlockSpec((8, LANES), lambda z, p, c: (0, 0))]
    if biased:
        kbspec = pl.BlockSpec((None, 2, nk, 1, bk), lambda z, p, c: (z, p, 0, 0, 0))
        ins.append(kbias)
        specs.append(kbspec)
        out_shape.append(jax.ShapeDtypeStruct(kbias.shape, F32))
        out_specs.append(kbspec)
    scratch = [pltpu.VMEM((pairs, n, LANES), BF16), pltpu.VMEM((2 * pairs, n, LANES), BF16), pltpu.VMEM((pairs, n, LANES), BF16)]
    scratch += [pltpu.VMEM((pairs, nk, LANES, bk), BF16), pltpu.VMEM((pairs, n, LANES), F32)]
    scratch += [pltpu.VMEM((2 * pairs, n, LANES), F32), pltpu.VMEM((pairs, n, LANES), F32)]
    scratch += [pltpu.VMEM((2 * pairs, LANES, bq), F32), pltpu.VMEM((2 * pairs, nq, 1, bq), F32)]
    res = _call(
        body, cargo=cargo, name=name, grid=(bs, npairs // pairs, stride), in_specs=specs, out_specs=out_specs, out_shape=out_shape,
        scratch_shapes=scratch, compiler_params=_params(("arbitrary", "arbitrary", "arbitrary")),
    )(*ins)
    return (list(res[:pairs]), list(res[pairs:2 * pairs]), list(res[2 * pairs:3 * pairs]), *res[3 * pairs:])


SB_LOG_PARTS = 2
SB_GRAD_PARTS = 1


def _log_sig_pair(z):
    lsn = jnp.minimum(-z, 0.0) - jnp.log(1.0 + jnp.exp(-jnp.abs(z)))
    return lsn, z + lsn


def _sb_specs(n, pairs, nq, bq):
    col = lambda off: pl.BlockSpec((None, n, LANES * pairs), lambda z, p: (z, 0, off // pairs + p))
    stat = pl.BlockSpec((None, 2 * pairs, nq, 1, bq), lambda z, p: (z, p, 0, 0, 0))
    return col, stat


def _sb_pairs(npairs):
    return 2 if npairs % 2 == 0 else 1


def _sb_fwd(src, offs, npairs, *, name, cargo=None):
    zs, n, _ = src.shape
    bq, bk = _att_blocks(n)
    nq, nk, rq = n // bq, n // bk, bq // bk
    pairs = _sb_pairs(npairs)
    heads = range(2 * pairs)
    lanes = [slice(pp * LANES, (pp + 1) * LANES) for pp in range(pairs)]

    def body(q_ref, k_ref, v_ref, o_ref, lt_ref, qs_s, kb_s, vt_s, acc_s, c_s):
        m_a = _mask_a()
        masks = (m_a, jnp.logical_not(m_a))

        def prep(c):
            rows = _rows(c, bk)
            for pp in range(pairs):
                qs_s[pp, rows, :] = (q_ref[rows, lanes[pp]] * SCALE).astype(BF16)
                kb_s[pp, rows, :] = k_ref[rows, lanes[pp]].astype(BF16)
                vt_s[pp, c] = jnp.transpose(v_ref[rows, lanes[pp]]).astype(BF16)

        _loop(0, nk, prep)
        qk = _q_minus_k(bk, bq)
        u_gt = _tri(bk, lambda r, c: c > r)

        def qblock(qi):
            rows = _rows(qi, bq)
            qms = [jnp.where(masks[h % 2], qs_s[h // 2, rows, :], 0) for h in heads]
            acc_s[...] = jnp.zeros_like(acc_s)
            c_s[...] = jnp.zeros_like(c_s)

            def step(kj, masked):
                cols = _rows(kj, bk)
                zts = [lax.dot_general(kb_s[h // 2, cols, :], qms[h], NT, preferred_element_type=F32) for h in heads]
                old = [c_s[h] for h in heads]
                if masked:
                    ok = (qk + (qi * bq - kj * bk)) > 0
                new = []
                for h in heads:
                    lsn, lsp = _log_sig_pair(zts[h])
                    if masked:
                        lsn = jnp.where(ok, lsn, 0.0)
                    at = jnp.exp(lsp + (old[h] + _cumdot_left(u_gt, lsn, SB_LOG_PARTS)))
                    if masked:
                        at = jnp.where(ok, at, 0.0)
                    new.append((jnp.dot(vt_s[h // 2, kj], at.astype(BF16), preferred_element_type=F32),
                                old[h] + jnp.sum(lsn, axis=0, keepdims=True)))
                for h in heads:
                    acc_s[h] += new[h][0]
                    c_s[h] = new[h][1]

            _loop(0, rq, lambda t: step((qi + 1) * rq - 1 - t, True))
            _loop(0, qi * rq, lambda t: step(qi * rq - 1 - t, False))
            for pp in range(pairs):
                o_ref[rows, lanes[pp]] = jnp.where(m_a, jnp.transpose(acc_s[2 * pp]), jnp.transpose(acc_s[2 * pp + 1]))
            for h in heads:
                lt_ref[h, qi] = c_s[h]

        _loop(0, nq, qblock)

    col, stat = _sb_specs(n, pairs, nq, bq)
    scratch = [pltpu.VMEM((pairs, n, LANES), BF16)] * 2 + [pltpu.VMEM((pairs, nk, LANES, bk), BF16)]
    scratch += [pltpu.VMEM((2 * pairs, LANES, bq), F32), pltpu.VMEM((2 * pairs, 1, bq), F32)]
    return _call(
        body, cargo=cargo, name=name, grid=(zs, npairs // pairs),
        in_specs=[col(offs[0]), col(offs[1]), col(offs[2])], out_specs=[col(0), stat],
        out_shape=[jax.ShapeDtypeStruct((zs, n, LANES * npairs), F32), jax.ShapeDtypeStruct((zs, 2 * npairs, nq, 1, bq), F32)],
        scratch_shapes=scratch, compiler_params=_params(("parallel", "parallel")),
    )(src, src, src)


def _sb_bwd(src, offs, npairs, do, ltot, *, name, cargo=None):
    zs, n, _ = src.shape
    bq, bk = _att_blocks(n)
    nq, nk, rq = n // bq, n // bk, bq // bk
    pairs = _sb_pairs(npairs)
    heads = range(2 * pairs)
    lanes = [slice(pp * LANES, (pp + 1) * LANES) for pp in range(pairs)]

    def body(q_ref, k_ref, v_ref, do_ref, lt_ref, dq_ref, dk_ref, dv_ref, qs_s, kb_s, vb_s, kt_s, dk_s, dv_s, dq_s, lp_s, ep_s):
        m_a = _mask_a()
        masks = (m_a, jnp.logical_not(m_a))

        def prep(c):
            rows = _rows(c, bk)
            for pp in range(pairs):
                qs_s[pp, rows, :] = (q_ref[rows, lanes[pp]] * SCALE).astype(BF16)
                kv = k_ref[rows, lanes[pp]]
                kb_s[pp, rows, :] = kv.astype(BF16)
                kt_s[pp, c] = jnp.transpose(kv).astype(BF16)
                vb_s[pp, rows, :] = v_ref[rows, lanes[pp]].astype(BF16)

        _loop(0, nk, prep)
        dk_s[...] = jnp.zeros_like(dk_s)
        dv_s[...] = jnp.zeros_like(dv_s)
        qk = _q_minus_k(bk, bq)
        u_le = _tri(bk, lambda r, c: c <= r)
        u_lt = _tri(bk, lambda r, c: c < r)

        def qblock(qi):
            rows = _rows(qi, bq)
            qms = [jnp.where(masks[h % 2], qs_s[h // 2, rows, :], 0) for h in heads]
            doms = [jnp.where(masks[h % 2], do_ref[rows, lanes[h // 2]], 0.0).astype(BF16) for h in heads]
            lts = [lt_ref[h, qi] for h in heads]
            dq_s[...] = jnp.zeros_like(dq_s)
            lp_s[...] = jnp.zeros_like(lp_s)
            ep_s[...] = jnp.zeros_like(ep_s)

            def step(kj, masked):
                cols = _rows(kj, bk)
                kbs = [kb_s[pp, cols, :] for pp in range(pairs)]
                kts = [kt_s[pp, kj] for pp in range(pairs)]
                zts = [lax.dot_general(kbs[h // 2], qms[h], NT, preferred_element_type=F32) for h in heads]
                dats = [lax.dot_general(vb_s[h // 2, cols, :], doms[h], NT, preferred_element_type=F32) for h in heads]
                old = [(lp_s[h], ep_s[h]) for h in heads]
                if masked:
                    ok = (qk + (qi * bq - kj * bk)) > 0
                new = []
                for h in heads:
                    lp, ep = old[h]
                    lsn, lsp = _log_sig_pair(zts[h])
                    sig = jnp.exp(lsp)
                    if masked:
                        lsn = jnp.where(ok, lsn, 0.0)
                    at = jnp.exp(lsp + (lts[h] - (lp + _cumdot_left(u_le, lsn, SB_LOG_PARTS))))
                    if masked:
                        at = jnp.where(ok, at, 0.0)
                    et = dats[h] * at
                    big_e = ep + _cumdot_left(u_lt, et, SB_GRAD_PARTS)
                    dzt = et - sig * (et + big_e)
                    if masked:
                        dzt = jnp.where(ok, dzt, 0.0)
                    dzb = dzt.astype(BF16)
                    new.append((jnp.dot(dzb, qms[h], preferred_element_type=F32),
                                jnp.dot(at.astype(BF16), doms[h], preferred_element_type=F32),
                                jnp.dot(kts[h // 2], dzb, preferred_element_type=F32),
                                lp + jnp.sum(lsn, axis=0, keepdims=True), ep + jnp.sum(et, axis=0, keepdims=True)))
                for h in heads:
                    dq_s[h] += new[h][2]
                    lp_s[h], ep_s[h] = new[h][3], new[h][4]
                for pp in range(pairs):
                    dk_s[pp, cols, :] += new[2 * pp][0] + new[2 * pp + 1][0]
                    dv_s[pp, cols, :] += new[2 * pp][1] + new[2 * pp + 1][1]

            _loop(0, qi * rq, lambda kj: step(kj, False))
            _loop(qi * rq, (qi + 1) * rq, lambda kj: step(kj, True))
            for pp in range(pairs):
                dq = jnp.where(m_a, jnp.transpose(dq_s[2 * pp]), jnp.transpose(dq_s[2 * pp + 1]))
                dq_ref[rows, lanes[pp]] = (dq * SCALE).astype(BF16)

        _loop(0, nq, qblock)

        def store(c):
            rows = _rows(c, bk)
            for pp in range(pairs):
                dk_ref[rows, lanes[pp]] = dk_s[pp, rows, :].astype(BF16)
                dv_ref[rows, lanes[pp]] = dv_s[pp, rows, :].astype(BF16)

        _loop(0, nk, store)

    col, stat = _sb_specs(n, pairs, nq, bq)
    ospec = col(0)
    sds = jax.ShapeDtypeStruct((zs, n, LANES * npairs), BF16)
    scratch = [pltpu.VMEM((pairs, n, LANES), BF16)] * 3 + [pltpu.VMEM((pairs, nk, LANES, bk), BF16)]
    scratch += [pltpu.VMEM((pairs, n, LANES), F32)] * 2
    scratch += [pltpu.VMEM((2 * pairs, LANES, bq), F32), pltpu.VMEM((2 * pairs, 1, bq), F32), pltpu.VMEM((2 * pairs, 1, bq), F32)]
    return _call(
        body, cargo=cargo, name=name, grid=(zs, npairs // pairs),
        in_specs=[col(offs[0]), col(offs[1]), col(offs[2]), ospec, stat],
        out_specs=[ospec] * 3, out_shape=[sds] * 3, scratch_shapes=scratch,
        compiler_params=_params(("parallel", "parallel")),
    )(src, src, src, do, ltot)


def _fox_gate_fwd(lg, bias, *, name):
    bs, nh, t = lg.shape
    blk = min(LANES, t)

    def body(lg_ref, b_ref, kb_ref):
        u_le = _tri(blk, lambda r, c: r <= c)
        carry = jnp.zeros((nh, 1), F32)
        for j in range(t // blk):
            sl = slice(j * blk, (j + 1) * blk)
            xv = lg_ref[:, sl] + b_ref[...]
            lf = jnp.minimum(xv, 0.0) - jnp.log(1.0 + jnp.exp(-jnp.abs(xv)))
            kb_ref[:, sl] = -(carry + _cumdot(lf, u_le, 3))
            carry = carry + jnp.sum(lf, axis=1, keepdims=True)

    spec = pl.BlockSpec((None, nh, t), lambda i: (i, 0, 0))
    return _pcall(
        body, name=name, grid=(bs,), in_specs=[spec, pl.BlockSpec((nh, 1), lambda i: (0, 0))], out_specs=spec,
        out_shape=jax.ShapeDtypeStruct((bs, nh, t), F32), compiler_params=_params(("parallel",)),
    )(lg, bias)


def _fox_gate_bwd(dkb, lg, bias, *, name):
    bs, nh, t = lg.shape
    blk = min(LANES, t)

    def body(dkb_ref, lg_ref, b_ref, dlg_ref, db_ref):
        @pl.when(pl.program_id(0) == 0)
        def _():
            db_ref[...] = jnp.zeros_like(db_ref)

        u_ge = _tri(blk, lambda r, c: r >= c)
        carry = jnp.zeros((nh, 1), F32)
        tot = jnp.zeros((nh, 1), F32)
        for j in reversed(range(t // blk)):
            sl = slice(j * blk, (j + 1) * blk)
            df = -dkb_ref[:, sl]
            dlf = carry + _cumdot(df, u_ge, 3)
            carry = carry + jnp.sum(df, axis=1, keepdims=True)
            xv = lg_ref[:, sl] + b_ref[...]
            dlg = dlf * jax.nn.sigmoid(-xv)
            dlg_ref[:, sl] = dlg
            tot = tot + jnp.sum(dlg, axis=1, keepdims=True)
        db_ref[...] += jnp.broadcast_to(tot, db_ref.shape)

    spec = pl.BlockSpec((None, nh, t), lambda i: (i, 0, 0))
    return _pcall(
        body, name=name, grid=(bs,), in_specs=[spec, spec, pl.BlockSpec((nh, 1), lambda i: (0, 0))],
        out_specs=[spec, pl.BlockSpec((nh, LANES), lambda i: (0, 0))],
        out_shape=[jax.ShapeDtypeStruct((bs, nh, t), F32), jax.ShapeDtypeStruct((nh, LANES), F32)],
        compiler_params=_params(("arbitrary",)),
    )(dkb, lg, bias)


def _place():
    return lax.axis_index("x"), lax.axis_index("y"), lax.axis_index("c")


def _flip(v, f):
    return 1 - v if f else v


FLIPS = [(fx, fy, fc) for fx in (0, 1) for fy in (0, 1) for fc in (0, 1)][1:]


def _comm_sems(nw):
    return [pltpu.SemaphoreType.DMA((7, nw)), pltpu.SemaphoreType.DMA((7, nw)), pltpu.SemaphoreType.DMA((nw,))]


def _gather_cargo(shards, on_done):
    nw = len(shards)

    def parts(x_refs, out_refs, sems):
        send_sems, recv_sems, local_sems = sems
        x, y, cc = _place()
        me, sibling = (x, y, cc), (x, y, 1 - cc)
        chips = [(1 - x, y), (x, 1 - y), (1 - x, 1 - y)]

        def slot(i, px, py, pc):
            return out_refs[i].at[4 * px + 2 * py + pc]

        def copy(i, k, block, to, src=None):
            return pltpu.make_async_remote_copy(
                src_ref=slot(i, *block) if src is None else src, dst_ref=slot(i, *block),
                send_sem=send_sems.at[k, i], recv_sem=recv_sems.at[k, i], device_id=to, device_id_type=MESH)

        mine = [pltpu.make_async_copy(x_refs[i], slot(i, *me), local_sems.at[i]) for i in range(nw)]
        first = []
        for i in range(nw):
            first.append(copy(i, 0, me, sibling, src=x_refs[i]))
            first += [copy(i, 1 + j, me, (*chip, cc), src=x_refs[i]) for j, chip in enumerate(chips)]
        return me, sibling, chips, cc, copy, mine, first

    def start(x_refs, out_refs, sems):
        *_, mine, first = parts(x_refs, out_refs, sems)
        for cp in mine + first:
            cp.start()

    def finish(x_refs, out_refs, sems):
        me, sibling, chips, cc, copy, mine, first = parts(x_refs, out_refs, sems)
        passed = []
        for i in range(nw):
            for j, chip in enumerate(chips):
                copy(i, 1 + j, (*chip, cc), me).wait_recv()
                passed.append(copy(i, 4 + j, (*chip, cc), sibling))
                passed[-1].start()
        for i in range(nw):
            copy(i, 0, sibling, me).wait_recv()
            for j, chip in enumerate(chips):
                copy(i, 4 + j, (*chip, 1 - cc), me).wait_recv()
        for cp in first + passed:
            cp.wait_send()
        for cp in mine:
            cp.wait()

    out_shape = [jax.ShapeDtypeStruct((N_DEV, *s.shape), s.dtype) for s in shards]
    return _Cargo(shards, out_shape, _comm_sems(nw), start, finish, on_done)


def _scatter_cargo(slots, prev, layer, depth, on_done, row0=0, rows=None):
    nw = len(slots)

    def parts(refs, recv_refs, sems):
        g_refs = refs[:nw]
        send_sems, recv_sems, local_sems = sems
        x, y, cc = _place()
        my = 4 * x + 2 * y + cc

        def dst(i):
            return recv_refs[i].at[my, layer, pl.ds(row0, slots[i].shape[1])]

        mine, copies = [], []
        for i in range(nw):
            mine.append(pltpu.make_async_copy(g_refs[i].at[my], dst(i), local_sems.at[i]))
            for k, (fx, fy, fc) in enumerate(FLIPS):
                px, py, pc = _flip(x, fx), _flip(y, fy), _flip(cc, fc)
                copies.append(pltpu.make_async_remote_copy(
                    src_ref=g_refs[i].at[4 * px + 2 * py + pc], dst_ref=dst(i),
                    send_sem=send_sems.at[k, i], recv_sem=recv_sems.at[k, i], device_id=(px, py, pc), device_id_type=MESH))
        return mine, copies

    def start(refs, recv_refs, sems):
        mine, copies = parts(refs, recv_refs, sems)
        for cp in mine + copies:
            cp.start()

    def finish(refs, recv_refs, sems):
        mine, copies = parts(refs, recv_refs, sems)
        for cp in copies:
            cp.wait_recv()
        for cp in copies:
            cp.wait_send()
        for cp in mine:
            cp.wait()

    ins, aliases = list(slots), {}
    for i, p in enumerate(prev):
        if p is not None:
            aliases[len(ins)] = i
            ins.append(p)
    out_shape = [jax.ShapeDtypeStruct((N_DEV, depth, rows or s.shape[1], s.shape[2]), s.dtype) for s in slots]
    return _Cargo(ins, out_shape, _comm_sems(nw), start, finish, on_done, aliases)


def _exchange(cargo, *, name):
    def body(*refs):
        c_in = len(cargo.ins)
        c_out = len(cargo.out_shape)
        cargo.start(refs[:c_in], refs[c_in:c_in + c_out], refs[c_in + c_out:])
        cargo.finish(refs[:c_in], refs[c_in:c_in + c_out], refs[c_in + c_out:])

    hbm = pl.BlockSpec(memory_space=pl.ANY)
    res = _pcall(
        body, name=name, in_specs=[hbm] * len(cargo.ins), out_specs=[hbm] * len(cargo.out_shape), out_shape=cargo.out_shape,
        scratch_shapes=cargo.sems, input_output_aliases=dict(cargo.aliases),
    )(*cargo.ins)
    cargo.on_done(list(res))


def _allreduce_small(blob, *, name):
    r, c = blob.shape

    def body(x_ref, out_ref, buf, send_sems, recv_sems):
        x, y, cc = _place()
        my = 4 * x + 2 * y + cc
        copies = []
        for k, (fx, fy, fc) in enumerate(FLIPS):
            peer = (_flip(x, fx), _flip(y, fy), _flip(cc, fc))
            copies.append(pltpu.make_async_remote_copy(
                src_ref=x_ref, dst_ref=buf.at[my], send_sem=send_sems.at[k], recv_sem=recv_sems.at[k],
                device_id=peer, device_id_type=MESH))
        for cp in copies:
            cp.start()
        buf[my] = x_ref[...]
        for cp in copies:
            cp.wait_recv()
        for cp in copies:
            cp.wait_send()
        acc = buf[0]
        for i in range(1, N_DEV):
            acc = acc + buf[i]
        out_ref[...] = acc

    vmem = pl.BlockSpec(memory_space=pltpu.VMEM)
    return _pcall(
        body, name=name, in_specs=[vmem], out_specs=vmem, out_shape=jax.ShapeDtypeStruct((r, c), F32),
        scratch_shapes=[pltpu.VMEM((N_DEV, r, c), F32), pltpu.SemaphoreType.DMA((7,)), pltpu.SemaphoreType.DMA((7,))],
    )(blob)


BIG = ("w_in", "w_mlp_in", "w_mlp_out", "w_up_fox", "w_up_sb", "w_up_dil", "w_out")
ROW_SHARDED = ("w_out", "w_mlp_out")
SMALL = ("attn_norm", "b_forget", "q_norm_fox", "k_norm_fox", "q_norm_dil", "k_norm_dil", "mlp_norm")
BLOB_ROWS = 512


def _pack(parts, dtype):
    flat = jnp.concatenate([p.reshape(-1).astype(dtype) for p in parts])
    size = -(-flat.shape[0] // (BLOB_ROWS * LANES)) * (BLOB_ROWS * LANES)
    return jnp.pad(flat, (0, size - flat.shape[0])).reshape(-1, LANES)


def _unpack(blob, shapes):
    flat = blob.reshape(-1)
    out, off = [], 0
    for shp in shapes:
        size = 1
        for s in shp:
            size *= s
        out.append(flat[off:off + size].reshape(shp))
        off += size
    return out


def _join_shards(name, sh):
    if name in ROW_SHARDED:
        return sh.reshape(-1, sh.shape[2])
    return jnp.transpose(sh, (1, 0, 2)).reshape(sh.shape[1], -1)


def _split_shards(name, full):
    a, b = full.shape
    if name in ROW_SHARDED:
        return full.reshape(N_DEV, a // N_DEV, b)
    return jnp.transpose(full.reshape(a, N_DEV, b // N_DEV), (1, 0, 2))


def _in_segments(d_in):
    o1 = 3 * W_FOX
    o2 = o1 + N_HEADS_FOX
    return (0, o1, 0), (o2, d_in, -N_HEADS_FOX), (o1, o2, d_in - o2)


def _join_w_in(sh, dp):
    b = sh.shape[2]
    pieces = []
    for s, e, _ in _in_segments(N_DEV * b):
        for j in range(s // b, (e - 1) // b + 1):
            pieces.append(sh[j, :, max(s, j * b) - j * b:min(e, (j + 1) * b) - j * b])
    pieces.append(jnp.zeros((sh.shape[1], dp - N_DEV * b), sh.dtype))
    return jnp.concatenate(pieces, axis=1)


def _split_w_in(gp, d_in):
    b = d_in // N_DEV
    shards = []
    for j in range(N_DEV):
        runs = []
        for s, e, shift in sorted(_in_segments(d_in)):
            lo, hi = max(s, j * b), min(e, (j + 1) * b)
            if lo < hi:
                runs.append(gp[:, lo + shift:hi + shift])
        shards.append(jnp.concatenate(runs, axis=1))
    return jnp.stack(shards)


def _stat_to_tokens(st, r, b):
    hh = st.shape[1]
    n = st.shape[2] * st.shape[4]
    return jnp.transpose(st.reshape(b, r, hh, n), (0, 2, 3, 1)).reshape(b, hh, n * r)


def _stat_to_streams(tok, r, blk):
    b, hh, t = tok.shape
    n = t // r
    return jnp.transpose(tok.reshape(b, hh, n, r), (0, 3, 1, 2)).reshape(b * r, hh, n // blk, 1, blk)


def _rope_tables(positions):
    half = HEAD_DIM // 2
    inv = 1.0 / (ROPE_THETA ** (jnp.arange(half, dtype=F32) / half))
    ang = positions.astype(F32)[..., None] * inv
    cos, sin = jnp.cos(ang), jnp.sin(ang)
    return jnp.tile(cos, (1, 1, 4)), jnp.tile(jnp.concatenate([-sin, sin], axis=-1), (1, 1, 2))


def _gain2(g):
    return jnp.tile(g.reshape(1, HEAD_DIM), (1, 2))


def _dil_offs(g):
    c0 = (P_DIL + g * W_DIL) // LANES
    return c0, c0 + W_DILQ // LANES, c0 + 2 * W_DILQ // LANES


def _dil_pairs(r):
    return N_HEADS_DIL // 2 if r > 1 else 1


def _layer_fwd(l, x, w, small, ropes, bl, t, cargo):
    n, d = x.shape
    s = {}
    s["x"] = x
    s["h"] = _rmsnorm_fwd(x, small["attn_norm"][l].reshape(1, d), name=f"norm_attn_fwd{l}")
    proj = _mm(s["h"], w["w_in"][l], tn=PROJ_TILE, name=f"mm_proj{l}")
    s["proj"] = proj
    dp = proj.shape[1]
    proj3 = proj.reshape(bl, t, dp)
    p_fg = P_GATE + 3 * d

    lg = jnp.transpose(proj3[:, :, p_fg:p_fg + N_HEADS_FOX], (0, 2, 1))
    s["lg"] = lg
    kb = _fox_gate_fwd(lg, small["b_forget"][l].reshape(N_HEADS_FOX, 1), name=f"fox_gate_fwd{l}")
    blk = _att_blocks(t, wide_keys=True)[1]
    kb5 = kb.reshape(bl, N_HEADS_FOX, t // blk, 1, blk)
    s["kb5"] = kb5
    gqf, gkf = _gain2(small["q_norm_fox"][l]), _gain2(small["k_norm_fox"][l])
    fo = P_FOX // LANES
    fox_offs = (fo, fo + W_FOX // LANES, fo + 2 * W_FOX // LANES)
    out_a, lse_a = _attn_fwd(proj3, fox_offs, N_HEADS_FOX // 2, gqf, gkf, kbias=kb5, window=t, name=f"fox_fwd{l}",
                             cargo=cargo.get("fox_fwd"))
    s["out_a"], s["lse_a"] = out_a, lse_a

    so = P_SB // LANES
    sb_offs = (so, so + W_SB // LANES, so + 2 * W_SB // LANES)
    out_b, lt_b = _sb_fwd(proj3, sb_offs, N_HEADS_SB // 2, name=f"sb_fwd{l}", cargo=cargo.get("sb_fwd"))
    s["out_b"], s["lt_b"] = out_b, lt_b

    gqd, gkd = _gain2(small["q_norm_dil"][l]), _gain2(small["k_norm_dil"][l])
    os_, lses = [], []
    for g, (window, r) in enumerate(DIL_PATTERNS):
        o_g, lse_g = _attn_fwd(proj3, _dil_offs(g), N_HEADS_DIL // 2, gqd, gkd, rope=ropes, window=window // r, stride=r,
                               pairs=_dil_pairs(r), name=f"dil_fwd{l}_{g}")
        os_.append(o_g.reshape(n, W_DIL))
        lses.append(_stat_to_tokens(lse_g, r, bl).reshape(bl * N_HEADS_DIL, t))
    lse_c, *ws = _dil_weights(lses, name=f"dil_weights{l}")
    ws = [jnp.repeat(jnp.transpose(wg.reshape(bl, N_HEADS_DIL, t), (0, 2, 1)).reshape(n, N_HEADS_DIL), HEAD_DIM, axis=1) for wg in ws]
    out_c = _dil_mix(os_, ws, name=f"dil_mix{l}")
    s["out_c"], s["lse_c"] = out_c, lse_c.reshape(bl, N_HEADS_DIL, t)

    ys = [_mm(out_a.reshape(n, W_FOX), w["w_up_fox"][l], name=f"mm_up_fox{l}"),
          _mm(out_b.reshape(n, W_SB), w["w_up_sb"][l], name=f"mm_up_sb{l}"),
          _mm(out_c, w["w_up_dil"][l], name=f"mm_up_dil{l}")]
    s["ys"] = ys
    s["merged"] = _gate_merge_fwd(proj, ys, name=f"gate_merge_fwd{l}")
    x1 = _mm(s["merged"], w["w_out"][l], add=x, name=f"mm_out{l}")
    s["x1"] = x1

    s["h2"] = _rmsnorm_fwd(x1, small["mlp_norm"][l].reshape(1, d), name=f"norm_mlp_fwd{l}")
    s["u"], s["a"] = _mm(s["h2"], w["w_mlp_in"][l], relu2=True, out_dtype=BF16, name=f"mm_mlp_in{l}")
    x2 = _mm(s["a"], w["w_mlp_out"][l], add=x1, name=f"mm_mlp_out{l}")
    return x2, s


def _layer_bwd(l, dx2, s, w, small, ropes, bl, t, hooks):
    n, d = dx2.shape
    gw, gs = {}, {}

    def cargo(call):
        return hooks[call](gw) if call in hooks else None
    du = _mm(dx2, w["w_mlp_out"][l], tb=True, relu_grad_of=s["u"], out_dtype=BF16, name=f"mm_du{l}")
    gw["w_mlp_out"] = _mm(s["a"], dx2, ta=True, name=f"mm_dw_mlp_out{l}")
    gw["w_mlp_in"] = _mm(s["h2"], du, ta=True, name=f"mm_dw_mlp_in{l}")
    dh2 = _mm(du, w["w_mlp_in"][l], tb=True, name=f"mm_dh2{l}")
    dx1, gs["mlp_norm"] = _rmsnorm_bwd(s["x1"], small["mlp_norm"][l].reshape(1, d), dh2, dx2, name=f"norm_mlp_bwd{l}")

    dmerged = _mm(dx1, w["w_out"][l], tb=True, name=f"mm_dmerged{l}")
    gw["w_out"] = _mm(s["merged"], dx1, ta=True, name=f"mm_dw_out{l}")
    dya, dyb, dyc, dgl0, dgl1, dgl2 = _gate_merge_bwd(s["proj"], s["ys"], dmerged, name=f"gate_merge_bwd{l}")
    out_a2, out_b2 = s["out_a"].reshape(n, W_FOX), s["out_b"].reshape(n, W_SB)
    gw["w_up_fox"] = _mm(out_a2, dya, ta=True, name=f"mm_dw_up_fox{l}")
    gw["w_up_sb"] = _mm(out_b2, dyb, ta=True, name=f"mm_dw_up_sb{l}")
    gw["w_up_dil"] = _mm(s["out_c"], dyc, ta=True, name=f"mm_dw_up_dil{l}")
    dout_a = _mm(dya, w["w_up_fox"][l], tb=True, name=f"mm_dout_a{l}").reshape(bl, t, W_FOX)
    dout_b = _mm(dyb, w["w_up_sb"][l], tb=True, name=f"mm_dout_b{l}").reshape(bl, t, W_SB)
    dout_c = _mm(dyc, w["w_up_dil"][l], tb=True, name=f"mm_dout_c{l}").reshape(bl, t, W_DIL)

    proj3 = s["proj"].reshape(bl, t, -1)
    gqf, gkf = _gain2(small["q_norm_fox"][l]), _gain2(small["k_norm_fox"][l])
    fo = P_FOX // LANES
    fox_offs = (fo, fo + W_FOX // LANES, fo + 2 * W_FOX // LANES)
    (dq_a,), (dk_a,), (dv_a,), dg_a, dkb5 = _attn_bwd(proj3, fox_offs, N_HEADS_FOX // 2, gqf, gkf, s["out_a"], dout_a, s["lse_a"],
                                             kbias=s["kb5"], window=t, name=f"fox_bwd{l}", cargo=cargo("fox_bwd"))
    gs["fox_gains"] = dg_a
    dlg, gs["b_forget"] = _fox_gate_bwd(dkb5.reshape(bl, N_HEADS_FOX, t), s["lg"], small["b_forget"][l].reshape(N_HEADS_FOX, 1),
                                        name=f"fox_gate_bwd{l}")
    so = P_SB // LANES
    sb_offs = (so, so + W_SB // LANES, so + 2 * W_SB // LANES)
    dq_b, dk_b, dv_b = _sb_bwd(proj3, sb_offs, N_HEADS_SB // 2, dout_b, s["lt_b"], name=f"sb_bwd{l}", cargo=cargo("sb_bwd"))
    gqd, gkd = _gain2(small["q_norm_dil"][l]), _gain2(small["k_norm_dil"][l])
    out_c3 = s["out_c"].reshape(bl, t, W_DIL)
    dqs, dks, dvs, dgd = [], [], [], None
    for g, (window, r) in enumerate(DIL_PATTERNS):
        lse_g = _stat_to_streams(s["lse_c"], r, _att_blocks(t // r)[0])
        dq_g, dk_g, dv_g, dg_g = _attn_bwd(proj3, _dil_offs(g), N_HEADS_DIL // 2, gqd, gkd, out_c3, dout_c, lse_g, rope=ropes,
                                           window=window // r, stride=r, pairs=_dil_pairs(r), name=f"dil_bwd{l}_{g}")
        dqs += dq_g
        dks += dk_g
        dvs += dv_g
        dgd = dg_g if dgd is None else jnp.concatenate([dgd, dg_g], axis=0)
    gs["dil_gains"] = dgd

    dlg_cols = jnp.pad(jnp.transpose(dlg, (0, 2, 1)).reshape(n, N_HEADS_FOX), ((0, 0), (0, LANES - N_HEADS_FOX)))
    parts = [p.reshape(n, -1) for p in [dq_a, dk_a, dv_a, dq_b, dk_b, dv_b] + dqs + dks + dvs] + [dgl0, dgl1, dgl2, dlg_cols]
    dproj = _assemble_cols(parts, s["proj"].shape[1], name=f"assemble_dproj{l}")
    if "mm_dw_in_hi" in hooks:
        half = d // 2
        gw["w_in_lo"] = _mm(s["h"][:, :half], dproj, ta=True, tn=PROJ_TILE, name=f"mm_dw_in_lo{l}")
        gw["w_in_hi"] = _mm(s["h"][:, half:], dproj, ta=True, tn=PROJ_TILE, name=f"mm_dw_in_hi{l}", cargo=cargo("mm_dw_in_hi"))
    else:
        gw["w_in"] = _mm(s["h"], dproj, ta=True, tn=PROJ_TILE, name=f"mm_dw_in{l}")
    dh = _mm(dproj, w["w_in"][l], tb=True, tn=1024, tk=PROJ_TILE, name=f"mm_dh{l}", cargo=cargo("mm_dh"))
    dx, gs["attn_norm"] = _rmsnorm_bwd(s["x"], small["attn_norm"][l].reshape(1, d), dh, dx1, name=f"norm_attn_bwd{l}")
    return dx, gw, gs


def kernel(x, positions, attn_norm, w_in, b_forget, q_norm_fox, k_norm_fox, q_norm_dil, k_norm_dil, w_up_fox, w_up_sb, w_up_dil, w_out, mlp_norm, w_mlp_in, w_mlp_out, loss_target, m_attn_norm, m_w_in, m_b_forget, m_q_norm_fox, m_k_norm_fox, m_q_norm_dil, m_k_norm_dil, m_w_up_fox, m_w_up_sb, m_w_up_dil, m_w_out, m_mlp_norm, m_w_mlp_in, m_w_mlp_out, v_attn_norm, v_w_in, v_b_forget, v_q_norm_fox, v_k_norm_fox, v_q_norm_dil, v_k_norm_dil, v_w_up_fox, v_w_up_sb, v_w_up_dil, v_w_out, v_mlp_norm, v_w_mlp_in, v_w_mlp_out):
    bl, t, d = x.shape
    n = bl * t
    depth = attn_norm.shape[0]
    wl = dict(w_in=w_in, w_up_fox=w_up_fox, w_up_sb=w_up_sb, w_up_dil=w_up_dil, w_out=w_out, w_mlp_in=w_mlp_in, w_mlp_out=w_mlp_out)
    ml = dict(w_in=m_w_in, w_up_fox=m_w_up_fox, w_up_sb=m_w_up_sb, w_up_dil=m_w_up_dil, w_out=m_w_out, w_mlp_in=m_w_mlp_in, w_mlp_out=m_w_mlp_out)
    vl = dict(w_in=v_w_in, w_up_fox=v_w_up_fox, w_up_sb=v_w_up_sb, w_up_dil=v_w_up_dil, w_out=v_w_out, w_mlp_in=v_w_mlp_in, w_mlp_out=v_w_mlp_out)
    small = dict(attn_norm=attn_norm, b_forget=b_forget, q_norm_fox=q_norm_fox, k_norm_fox=k_norm_fox, q_norm_dil=q_norm_dil,
                 k_norm_dil=k_norm_dil, mlp_norm=mlp_norm)
    m_small = dict(attn_norm=m_attn_norm, b_forget=m_b_forget, q_norm_fox=m_q_norm_fox, k_norm_fox=m_k_norm_fox,
                   q_norm_dil=m_q_norm_dil, k_norm_dil=m_k_norm_dil, mlp_norm=m_mlp_norm)
    v_small = dict(attn_norm=v_attn_norm, b_forget=v_b_forget, q_norm_fox=v_q_norm_fox, k_norm_fox=v_k_norm_fox,
                   q_norm_dil=v_q_norm_dil, k_norm_dil=v_k_norm_dil, mlp_norm=v_mlp_norm)

    d_in = w_in.shape[-1] * N_DEV
    dp = -(-d_in // 512) * 512
    rest = [k for k in BIG if k != "w_in"]
    w = {k: [None] * depth for k in BIG}

    def gather(items):
        def done(res):
            for (k, l), sh in zip(items, res):
                w[k][l] = _join_w_in(sh, dp) if k == "w_in" else _join_shards(k, sh)

        return _gather_cargo([wl[k][l].astype(BF16) for k, l in items], done)

    _exchange(gather([("w_in", 0)]), name="gather_first")
    cos, sin = _rope_tables(positions)
    ropes = (cos, sin)

    h = x.reshape(n, d)
    saved = []
    for l in range(depth):
        cargo = {"fox_fwd": gather([(k, l) for k in rest])}
        if l + 1 < depth:
            cargo["sb_fwd"] = gather([("w_in", l + 1)])
        h, s = _layer_fwd(l, h, w, small, ropes, bl, t, cargo)
        saved.append(s)
    dy, loss_part = _loss_head(h, loss_target.reshape(n, d), name="loss_head")

    recv = {}

    def scatter(names, l, grads):
        def done(res):
            recv.update(zip(names, res))

        slots = [(_split_w_in(grads[k], d_in) if k == "w_in" else _split_shards(k, grads[k])).astype(BF16) for k in names]
        return _scatter_cargo(slots, [recv.get(k) for k in names], l, depth, done)

    def scatter_w_in_rows(part, l, row0):
        def done(res):
            recv["w_in"] = res[0]

        return _scatter_cargo([_split_w_in(part, d_in).astype(BF16)], [recv.get("w_in")], l, depth, done, row0=row0, rows=d)

    gss = [None] * depth
    above = None
    for l in reversed(range(depth)):
        hooks = {"fox_bwd": lambda gw, l=l: scatter(rest, l, gw)}
        if above is not None:
            hooks["sb_bwd"] = lambda gw, l=l, g=above: scatter(["w_in"], l + 1, g)
        if l == 0:
            hooks["mm_dw_in_hi"] = lambda gw: scatter_w_in_rows(gw["w_in_lo"], 0, 0)
            hooks["mm_dh"] = lambda gw: scatter_w_in_rows(gw["w_in_hi"], 0, d // 2)
        dy, above, gss[l] = _layer_bwd(l, dy, saved[l], w, small, ropes, bl, t, hooks)
    grad_x = dy.reshape(bl, t, d)

    g_big, d_big, m_big, v_big = {}, {}, {}, {}
    for k in BIG:
        g_big[k], d_big[k], m_big[k], v_big[k] = _adamw(recv[k], wl[k], ml[k], vl[k], name=f"adamw_{k}")

    rows = [loss_part]
    for l in range(depth):
        gs = gss[l]
        rows += [gs["attn_norm"].reshape(-1, LANES), gs["mlp_norm"].reshape(-1, LANES), gs["fox_gains"], gs["dil_gains"], gs["b_forget"]]
    row_counts = [r.shape[0] for r in rows]
    part = jnp.concatenate(rows, axis=0)
    pad_rows = -(-part.shape[0] // 8) * 8 - part.shape[0]
    summed = _allreduce_small(jnp.pad(part, ((0, pad_rows), (0, 0))), name="allreduce_small")
    pieces, off = [], 0
    for c in row_counts:
        pieces.append(summed[off:off + c])
        off += c
    loss = pieces[0][0, 0]

    def fold(row):
        return row[:HEAD_DIM] + row[HEAD_DIM:]

    g_small = {k: [] for k in SMALL}
    for l in range(depth):
        an, mn, fg, dg, bf = pieces[1 + 5 * l:6 + 5 * l]
        g_small["attn_norm"].append(an.reshape(d))
        g_small["mlp_norm"].append(mn.reshape(d))
        g_small["q_norm_fox"].append(fold(fg[0]))
        g_small["k_norm_fox"].append(fold(fg[1]))
        g_small["q_norm_dil"].append(fold(dg[0]) + fold(dg[8]) + fold(dg[16]))
        g_small["k_norm_dil"].append(fold(dg[1]) + fold(dg[9]) + fold(dg[17]))
        g_small["b_forget"].append(bf[:, 0])
    g_small = {k: jnp.stack(vs) for k, vs in g_small.items()}
    small_shapes = [small[k].shape for k in SMALL]
    outs = _adamw(_pack([g_small[k] for k in SMALL], F32)[None, None], _pack([small[k] for k in SMALL], F32)[None],
                  _pack([m_small[k] for k in SMALL], F32)[None], _pack([v_small[k] for k in SMALL], F32)[None], name="adamw_small")
    g_sm, d_sm, m_sm, v_sm = (dict(zip(SMALL, _unpack(o, small_shapes))) for o in outs)

    order = ("attn_norm", "w_in", "b_forget", "q_norm_fox", "k_norm_fox", "q_norm_dil", "k_norm_dil", "w_up_fox", "w_up_sb",
             "w_up_dil", "w_out", "mlp_norm", "w_mlp_in", "w_mlp_out")
    res = [loss, grad_x]
    for big, sm in ((g_big, g_sm), (d_big, d_sm), (m_big, m_sm), (v_big, v_sm)):
        res += [big[k] if k in big else sm[k] for k in order]
    return tuple(res)
```

```python
import jax
import jax.numpy as jnp
from jax import lax
from jax.experimental import pallas as pl
from jax.experimental.pallas import tpu as pltpu

F32 = jnp.float32
BF16 = jnp.bfloat16

HEAD_DIM = 64
LANES = 128
N_HEADS_FOX = 8
N_HEADS_SB = 8
N_HEADS_DIL = 4
DIL_PATTERNS = ((128, 1), (512, 4), (2048, 16))
ROPE_THETA = 10000.0
EPS = 1e-6
SCALE = 0.125
W_FOX = N_HEADS_FOX * HEAD_DIM
W_SB = N_HEADS_SB * HEAD_DIM
W_DIL = N_HEADS_DIL * HEAD_DIM
W_DILQ = len(DIL_PATTERNS) * W_DIL
P_FOX = 0
P_SB = 3 * W_FOX
P_DIL = P_SB + 3 * W_SB
P_GATE = P_DIL + 3 * W_DILQ
N_DEV = 8
ATT_BLK = 256
ATT_BQ = 512
NEG = -1e30
VMEM_LIMIT = 56 * 1024 * 1024
ADAMW_BLOCK_ELEMS = 128 * 1024
PROJ_TILE = 2176

ADAM_LR = 0.001
ADAM_B1 = 0.9
ADAM_B2 = 0.999
ADAM_EPS = 1e-08
ADAM_WD = 0.01
ADAM_STEP = 10

NT = (((1,), (1,)), ((), ()))
MESH = pl.DeviceIdType.MESH


def _pcall(body, **kw):
    return pl.pallas_call(body, **kw)


def _params(sem=None):
    return pltpu.CompilerParams(dimension_semantics=sem, vmem_limit_bytes=VMEM_LIMIT)


class _Cargo:
    def __init__(self, ins, out_shape, sems, start, finish, on_done, aliases=None):
        self.ins, self.out_shape, self.sems = list(ins), list(out_shape), list(sems)
        self.start, self.finish, self.on_done, self.aliases = start, finish, on_done, dict(aliases or {})


def _call(body, *, cargo=None, name, grid=(), in_specs, out_specs, out_shape, scratch_shapes=(), compiler_params=None):
    if cargo is None:
        kw = dict(grid=grid) if grid else {}
        if compiler_params is not None:
            kw["compiler_params"] = compiler_params
        return _pcall(body, name=name, in_specs=in_specs, out_specs=out_specs, out_shape=out_shape,
                      scratch_shapes=list(scratch_shapes), **kw)
    single = not isinstance(out_shape, (list, tuple))
    o_specs, o_shape = ([out_specs], [out_shape]) if single else (list(out_specs), list(out_shape))
    n_in, n_out, n_scr = len(in_specs), len(o_shape), len(scratch_shapes)
    c_in, c_out = len(cargo.ins), len(cargo.out_shape)

    def wrapped(*refs):
        ins, cins = refs[:n_in], refs[n_in:n_in + c_in]
        o0 = n_in + c_in
        outs, couts = refs[o0:o0 + n_out], refs[o0 + n_out:o0 + n_out + c_out]
        s0 = o0 + n_out + c_out
        scr, sems = refs[s0:s0 + n_scr], refs[s0 + n_scr:]
        first = last = None
        for ax, size in enumerate(grid):
            pid = pl.program_id(ax)
            first = (pid == 0) if first is None else first & (pid == 0)
            last = (pid == size - 1) if last is None else last & (pid == size - 1)
        if first is None:
            cargo.start(cins, couts, sems)
            body(*ins, *outs, *scr)
            cargo.finish(cins, couts, sems)
            return

        @pl.when(first)
        def _():
            cargo.start(cins, couts, sems)

        body(*ins, *outs, *scr)

        @pl.when(last)
        def _():
            cargo.finish(cins, couts, sems)

    hbm = pl.BlockSpec(memory_space=pl.ANY)
    kw = dict(grid=grid, compiler_params=_params(("arbitrary",) * len(grid))) if grid else {}
    call = _pcall(
        wrapped, name=name, in_specs=list(in_specs) + [hbm] * c_in, out_specs=o_specs + [hbm] * c_out,
        out_shape=o_shape + cargo.out_shape, scratch_shapes=list(scratch_shapes) + cargo.sems,
        input_output_aliases={n_in + i: n_out + j for i, j in cargo.aliases.items()}, **kw)

    def run(*args):
        res = call(*args, *cargo.ins)
        cargo.on_done(list(res[n_out:]))
        return res[0] if single else list(res[:n_out])

    return run


def _tile(dim, target, mult=LANES):
    t = (min(dim, target) // mult) * mult
    while t >= mult:
        if dim % t == 0:
            return t
        t -= mult
    return dim


def _mm(a, b, *, ta=False, tb=False, add=None, relu2=False, relu_grad_of=None, out_dtype=F32, name, tm=1024, tn=1024,
        tk=1024, cargo=None):
    m, k = (a.shape[1], a.shape[0]) if ta else a.shape
    n = b.shape[0] if tb else b.shape[1]
    tm, tn, tk = _tile(m, tm), _tile(n, tn), _tile(k, tk)
    nk = k // tk
    dn = (((0,) if ta else (1,), (1,) if tb else (0,)), ((), ()))

    extra = add if add is not None else relu_grad_of

    def body(*refs):
        a_ref, b_ref = refs[:2]
        x_ref = refs[2] if extra is not None else None
        outs = refs[2 + (extra is not None):-1]
        acc = refs[-1]
        kk = pl.program_id(2)
        part = lax.dot_general(a_ref[...].astype(BF16), b_ref[...].astype(BF16), dn, preferred_element_type=F32)

        def finish(r):
            if add is not None:
                r = r + x_ref[...]
            if relu_grad_of is not None:
                r = r * (2.0 * jnp.maximum(x_ref[...].astype(F32), 0.0))
            outs[0][...] = r.astype(out_dtype)
            if relu2:
                rr = jnp.maximum(r, 0.0)
                outs[1][...] = (rr * rr).astype(BF16)

        if nk == 1:
            finish(part)
            return

        @pl.when(kk == 0)
        def _():
            acc[...] = part

        @pl.when((kk > 0) & (kk < nk - 1))
        def _():
            acc[...] += part

        @pl.when(kk == nk - 1)
        def _():
            finish(acc[...] + part)

    a_spec = pl.BlockSpec((tk, tm), lambda i, j, q: (q, i)) if ta else pl.BlockSpec((tm, tk), lambda i, j, q: (i, q))
    b_spec = pl.BlockSpec((tn, tk), lambda i, j, q: (j, q)) if tb else pl.BlockSpec((tk, tn), lambda i, j, q: (q, j))
    o_spec = pl.BlockSpec((tm, tn), lambda i, j, q: (i, j))
    ins, specs = [a, b], [a_spec, b_spec]
    if extra is not None:
        ins.append(extra)
        specs.append(o_spec)
    sds = jax.ShapeDtypeStruct((m, n), out_dtype)
    return _call(
        body, cargo=cargo, name=name, grid=(m // tm, n // tn, nk), in_specs=specs,
        out_specs=[o_spec, o_spec] if relu2 else o_spec,
        out_shape=[sds, jax.ShapeDtypeStruct((m, n), BF16)] if relu2 else sds,
        scratch_shapes=[pltpu.VMEM((tm, tn) if nk > 1 else (8, LANES), F32)],
        compiler_params=_params(("parallel", "parallel", "arbitrary")),
    )(*ins)


def _rmsnorm_fwd(x, g, *, name):
    n, d = x.shape
    tm = _tile(n, 256, 8)

    def body(x_ref, g_ref, h_ref):
        xv = x_ref[...]
        inv = lax.rsqrt(jnp.mean(xv * xv, axis=1, keepdims=True) + EPS)
        h_ref[...] = (xv * inv * g_ref[...]).astype(BF16)

    row = pl.BlockSpec((tm, d), lambda i: (i, 0))
    return _pcall(
        body, name=name, grid=(n // tm,), in_specs=[row, pl.BlockSpec((1, d), lambda i: (0, 0))], out_specs=row,
        out_shape=jax.ShapeDtypeStruct((n, d), BF16), compiler_params=_params(("parallel",)),
    )(x, g)


def _rmsnorm_bwd(x, g, dh, dres, *, name):
    n, d = x.shape
    tm = _tile(n, 256, 8)

    def body(x_ref, g_ref, dh_ref, dres_ref, dx_ref, dg_ref):
        @pl.when(pl.program_id(0) == 0)
        def _():
            dg_ref[...] = jnp.zeros_like(dg_ref)

        xv = x_ref[...]
        inv = lax.rsqrt(jnp.mean(xv * xv, axis=1, keepdims=True) + EPS)
        y = xv * inv
        dhv = dh_ref[...]
        dg_ref[...] += jnp.sum(dhv * y, axis=0, keepdims=True)
        dy = dhv * g_ref[...]
        dx_ref[...] = dres_ref[...] + inv * (dy - y * jnp.mean(dy * y, axis=1, keepdims=True))

    row = pl.BlockSpec((tm, d), lambda i: (i, 0))
    vec = pl.BlockSpec((1, d), lambda i: (0, 0))
    return _pcall(
        body, name=name, grid=(n // tm,), in_specs=[row, vec, row, row], out_specs=[row, vec],
        out_shape=[jax.ShapeDtypeStruct((n, d), F32), jax.ShapeDtypeStruct((1, d), F32)],
        compiler_params=_params(("arbitrary",)),
    )(x, g, dh, dres)


def _gate_specs(n, d):
    bw = 256 if d % 256 == 0 else LANES
    tm = _tile(n, 512, 8)
    nb = d // bw
    yspec = pl.BlockSpec((tm, bw), lambda i, j: (i, j))
    gspecs = [pl.BlockSpec((tm, bw), lambda i, j, b=b: (i, P_GATE // bw + b * nb + j)) for b in range(3)]
    return tm, bw, nb, yspec, gspecs


def _gate_merge_fwd(proj, ys, *, name):
    n, d = ys[0].shape
    tm, bw, nb, yspec, gspecs = _gate_specs(n, d)

    def body(g0, g1, g2, y0, y1, y2, o_ref):
        acc = jax.nn.sigmoid(g0[...]) * y0[...]
        acc += jax.nn.sigmoid(g1[...]) * y1[...]
        acc += jax.nn.sigmoid(g2[...]) * y2[...]
        o_ref[...] = acc.astype(BF16)

    return _pcall(
        body, name=name, grid=(n // tm, nb), in_specs=gspecs + [yspec] * 3, out_specs=yspec,
        out_shape=jax.ShapeDtypeStruct((n, d), BF16), compiler_params=_params(("parallel", "parallel")),
    )(proj, proj, proj, *ys)


def _gate_merge_bwd(proj, ys, dmerged, *, name):
    n, d = ys[0].shape
    tm, bw, nb, yspec, gspecs = _gate_specs(n, d)

    def body(g0, g1, g2, y0, y1, y2, dm_ref, dy0, dy1, dy2, dgl0, dgl1, dgl2):
        dm = dm_ref[...]
        for g_ref, y_ref, dy_ref, dgl_ref in ((g0, y0, dy0, dgl0), (g1, y1, dy1, dgl1), (g2, y2, dy2, dgl2)):
            s = jax.nn.sigmoid(g_ref[...])
            dy_ref[...] = (dm * s).astype(BF16)
            dgl_ref[...] = (dm * y_ref[...] * s * (1.0 - s)).astype(BF16)

    sds = jax.ShapeDtypeStruct((n, d), BF16)
    return _pcall(
        body, name=name, grid=(n // tm, nb), in_specs=gspecs + [yspec] * 4, out_specs=[yspec] * 6,
        out_shape=[sds] * 6, compiler_params=_params(("parallel", "parallel")),
    )(proj, proj, proj, *ys, dmerged)


def _loss_head(y, tgt, *, name):
    n, d = y.shape
    tm = _tile(n, 256, 8)
    steps = n // tm

    def body(y_ref, t_ref, dy_ref, loss_ref, acc):
        i = pl.program_id(0)

        @pl.when(i == 0)
        def _():
            acc[...] = jnp.zeros_like(acc)

        e = y_ref[...] - t_ref[...]
        dy_ref[...] = e * (1.0 / d)
        acc[...] += jnp.sum(e * e, axis=0, keepdims=True)

        @pl.when(i == steps - 1)
        def _():
            tot = jnp.sum(acc[...], axis=1, keepdims=True) * (0.5 / d)
            loss_ref[...] = jnp.broadcast_to(tot, loss_ref.shape)

    row = pl.BlockSpec((tm, d), lambda i: (i, 0))
    return _pcall(
        body, name=name, grid=(steps,), in_specs=[row, row], out_specs=[row, pl.BlockSpec((8, LANES), lambda i: (0, 0))],
        out_shape=[jax.ShapeDtypeStruct((n, d), F32), jax.ShapeDtypeStruct((8, LANES), F32)],
        scratch_shapes=[pltpu.VMEM((1, d), F32)], compiler_params=_params(("arbitrary",)),
    )(y, tgt)


def _assemble_cols(parts, width, *, name):
    n = parts[0].shape[0]
    tm = _tile(n, 256, 16)
    widths = [p.shape[1] for p in parts]

    def body(*refs):
        o_ref = refs[-1]
        off = 0
        for ref, w in zip(refs[:-1], widths):
            o_ref[:, off:off + w] = ref[...].astype(BF16)
            off += w
        if off < width:
            o_ref[:, off:] = jnp.zeros((tm, width - off), BF16)

    return _pcall(
        body, name=name, grid=(n // tm,), in_specs=[pl.BlockSpec((tm, w), lambda i: (i, 0)) for w in widths],
        out_specs=pl.BlockSpec((tm, width), lambda i: (i, 0)), out_shape=jax.ShapeDtypeStruct((n, width), BF16),
        compiler_params=_params(("parallel",)),
    )(*parts)


def _dil_weights(lses, *, name):
    shp = lses[0].shape

    def body(l0, l1, l2, lse_ref, w0, w1, w2):
        a, b, c = l0[...], l1[...], l2[...]
        m = jnp.maximum(jnp.maximum(a, b), c)
        ea, eb, ec = jnp.exp(a - m), jnp.exp(b - m), jnp.exp(c - m)
        den = ea + eb + ec
        lse_ref[...] = m + jnp.log(den)
        w0[...] = ea / den
        w1[...] = eb / den
        w2[...] = ec / den

    vmem = pl.BlockSpec(memory_space=pltpu.VMEM)
    return _pcall(body, name=name, in_specs=[vmem] * 3, out_specs=[vmem] * 4, out_shape=[jax.ShapeDtypeStruct(shp, F32)] * 4)(*lses)


def _dil_mix(os_, ws, *, name):
    n, w = os_[0].shape
    tm = _tile(n, 512, 8)

    def body(o0, o1, o2, w0, w1, w2, out_ref):
        out_ref[...] = w0[...] * o0[...] + w1[...] * o1[...] + w2[...] * o2[...]

    spec = pl.BlockSpec((tm, w), lambda i: (i, 0))
    return _pcall(
        body, name=name, grid=(n // tm,), in_specs=[spec] * 6, out_specs=spec, out_shape=jax.ShapeDtypeStruct((n, w), F32),
        compiler_params=_params(("parallel",)),
    )(*os_, *ws)


def _adamw(gsrc, w, m, v, *, name):
    s, dep, a, b = gsrc.shape
    ta = _tile(a, max(16, (ADAMW_BLOCK_ELEMS // b) // 16 * 16), 16)
    c1 = 1.0 / (1.0 - ADAM_B1 ** ADAM_STEP)
    c2 = 1.0 / (1.0 - ADAM_B2 ** ADAM_STEP)

    def body(gs_ref, w_ref, m_ref, v_ref, g_ref, d_ref, m2_ref, v2_ref):
        g = gs_ref[0].astype(F32)
        for i in range(1, s):
            g = g + gs_ref[i].astype(F32)
        m2 = ADAM_B1 * m_ref[...] + (1.0 - ADAM_B1) * g
        v2 = ADAM_B2 * v_ref[...] + (1.0 - ADAM_B2) * (g * g)
        g_ref[...] = g
        m2_ref[...] = m2
        v2_ref[...] = v2
        d_ref[...] = -ADAM_LR * ((m2 * c1) / (jnp.sqrt(v2 * c2) + ADAM_EPS) + ADAM_WD * w_ref[...])

    spec = pl.BlockSpec((None, ta, b), lambda l, i: (l, i, 0))
    sds = jax.ShapeDtypeStruct((dep, a, b), F32)
    return _pcall(
        body, name=name, grid=(dep, a // ta),
        in_specs=[pl.BlockSpec((s, None, ta, b), lambda l, i: (0, l, i, 0)), spec, spec, spec],
        out_specs=[spec] * 4, out_shape=[sds] * 4, compiler_params=_params(("parallel", "parallel")),
    )(gsrc, w, m, v)


def _mask_a():
    return lax.broadcasted_iota(jnp.int32, (1, LANES), 1) < HEAD_DIM


def _half_sum(x, m_a):
    sa = jnp.sum(jnp.where(m_a, x, 0.0), axis=1, keepdims=True)
    sb = jnp.sum(jnp.where(m_a, 0.0, x), axis=1, keepdims=True)
    return jnp.where(m_a, sa, sb)


def _head_inv(x, m_a):
    return lax.rsqrt(_half_sum(x * x, m_a) * (1.0 / HEAD_DIM) + EPS)


def _swap32(x):
    first = (lax.broadcasted_iota(jnp.int32, (1, LANES), 1) % HEAD_DIM) < (HEAD_DIM // 2)
    return jnp.where(first, pltpu.roll(x, LANES - HEAD_DIM // 2, 1), pltpu.roll(x, HEAD_DIM // 2, 1))


def _tri(blk, rel):
    r = lax.broadcasted_iota(jnp.int32, (blk, blk), 0)
    c = lax.broadcasted_iota(jnp.int32, (blk, blk), 1)
    return jnp.where(rel(r, c), 1.0, 0.0).astype(BF16)


def _cumdot(x, u, parts):
    acc = None
    r = x
    for i in range(parts):
        xi = r.astype(BF16)
        t = jnp.dot(xi, u, preferred_element_type=F32)
        acc = t if acc is None else acc + t
        if i + 1 < parts:
            r = r - xi.astype(F32)
    return acc


def _rows(i, blk):
    return pl.ds(pl.multiple_of(i * blk, blk), blk)


def _att_blk(n):
    return ATT_BLK if n % ATT_BLK == 0 else min(LANES, n)


def _att_blocks(n, wide_keys=False):
    bk = _att_blk(n)
    bq = ATT_BQ if n % ATT_BQ == 0 else bk
    return bq, (bq if wide_keys else bk)


def _loop(lo, hi, fn):
    def it(i, c):
        fn(i)
        return c

    lax.fori_loop(lo, hi, it, 0)


def _normed(src, g_ref, rope_refs, rows, m_a):
    xv = src[rows, :]
    xn = xv * _head_inv(xv, m_a) * g_ref[...]
    if rope_refs is not None:
        xn = xn * rope_refs[0][rows, :] + _swap32(xn) * rope_refs[1][rows, :]
    return xn


def _bias_lane(h):
    return HEAD_DIM if h == 0 else 0


def _k_for_head(kn, kb_row, h, m_h, lane, blk):
    out = jnp.where(m_h, kn, 0.0)
    if kb_row is not None:
        col = jnp.transpose(jnp.broadcast_to(kb_row, (LANES, blk)))
        hi = col.astype(BF16).astype(F32)
        mid = (col - hi).astype(BF16).astype(F32)
        lo = col - hi - mid
        b = _bias_lane(h)
        out = jnp.where(lane == b, hi, jnp.where(lane == b + 1, mid, jnp.where(lane == b + 2, lo, out)))
    return out.astype(BF16)


def _q_for_head(qb, h, m_h, lane, biased):
    out = jnp.where(m_h, qb, 0)
    if biased:
        b = _bias_lane(h)
        out = jnp.where((lane >= b) & (lane < b + 3), jnp.ones_like(out), out)
    return out


def _head_rows(x, parts=3):
    rr = lax.broadcasted_iota(jnp.int32, (8, LANES), 0)
    ll = lax.broadcasted_iota(jnp.int32, (8, LANES), 1)
    sel = jnp.where(((rr == 0) & (ll < HEAD_DIM)) | ((rr == 1) & (ll >= HEAD_DIM)), 1.0, 0.0).astype(BF16)
    acc = None
    rem = x
    for i in range(parts):
        xi = rem.astype(BF16)
        t = lax.dot_general(sel, xi, NT, preferred_element_type=F32)
        acc = t if acc is None else acc + t
        if i + 1 < parts:
            rem = rem - xi.astype(F32)
    return acc


def _cumdot_left(u, x, parts):
    acc = None
    rem = x
    for i in range(parts):
        xi = rem.astype(BF16)
        t = jnp.dot(u, xi, preferred_element_type=F32)
        acc = t if acc is None else acc + t
        if i + 1 < parts:
            rem = rem - xi.astype(F32)
    return acc


def _q_minus_k(bk, bq):
    return lax.broadcasted_iota(jnp.int32, (bk, bq), 1) - lax.broadcasted_iota(jnp.int32, (bk, bq), 0)


def _stream_rows(stride):
    if stride == 1:
        return _rows
    c = pl.program_id(2)
    return lambda i, blk: pl.ds(c + i * (blk * stride), blk, stride=stride)


def _attn_specs(t, pairs, stride, nq, bq):
    col = lambda off: pl.BlockSpec((None, t, LANES), lambda z, p, c: (z, 0, off + p * pairs))
    vec = pl.BlockSpec((1, LANES), lambda z, p, c: (0, 0))
    seq = pl.BlockSpec((None, t, LANES), lambda z, p, c: (z, 0, 0))
    stat = pl.BlockSpec((None, 2 * pairs, nq, 1, bq), lambda z, p, c: (z * stride + c, p, 0, 0, 0))
    return col, vec, seq, stat


def _attn_fwd(src, offs, npairs, gq, gk, *, rope=None, kbias=None, window, stride=1, pairs=1, name, cargo=None):
    bs, t, _ = src.shape
    n = t // stride
    full = window >= n
    bq, bk = _att_blocks(n, wide_keys=full)
    nq, nk, rq = n // bq, n // bk, bq // bk
    wblk = -(-window // bk)
    biased = kbias is not None
    heads = range(2 * pairs)

    def body(*refs):
        it = iter(refs)
        q_refs, k_refs, v_refs = ([next(it) for _ in range(pairs)] for _ in range(3))
        gq_ref, gk_ref = next(it), next(it)
        rope_refs = (next(it), next(it)) if rope is not None else None
        kb_ref = next(it) if biased else None
        o_refs = [next(it) for _ in range(pairs)]
        lse_ref, qn_s, kh_s, vt_s, acc_s, m_s = (next(it) for _ in range(6))
        m_a = _mask_a()
        masks = (m_a, jnp.logical_not(m_a))
        lane = lax.broadcasted_iota(jnp.int32, (1, LANES), 1)
        row = lax.broadcasted_iota(jnp.int32, (LANES, 1), 0)
        tok = _stream_rows(stride)

        def prep(c):
            rows = _rows(c, bk)
            trows = tok(c, bk)
            for pp in range(pairs):
                qn = _normed(q_refs[pp], gq_ref, rope_refs, trows, m_a)
                qn_s[pp, rows, :] = (qn * SCALE).astype(BF16)
                kn = _normed(k_refs[pp], gk_ref, rope_refs, trows, m_a)
                vt = jnp.transpose(v_refs[pp][trows, :])
                for half in (0, 1):
                    h = 2 * pp + half
                    kh_s[h, rows, :] = _k_for_head(kn, kb_ref[h, c] if biased else None, half, masks[half], lane, bk)
                    vt_s[h, c] = jnp.where(row == _bias_lane(half), 1.0, vt).astype(BF16)

        _loop(0, nk, prep)
        qk = _q_minus_k(bk, bq)

        def qblock(qi):
            rows = _rows(qi, bq)
            qh = [_q_for_head(qn_s[h // 2, rows, :], h % 2, masks[h % 2], lane, biased) for h in heads]
            m_s[...] = jnp.full(m_s.shape, NEG, F32)
            acc_s[...] = jnp.zeros_like(acc_s)

            def step(kj, masked):
                cols = _rows(kj, bk)
                sts = [lax.dot_general(kh_s[h, cols, :], qh[h], NT, preferred_element_type=F32) for h in heads]
                old = [(m_s[h], acc_s[h]) for h in heads]
                if masked:
                    d = qk + (qi * bq - kj * bk)
                    ok = (d >= 0) & (d <= window)
                    sts = [jnp.where(ok, st, NEG) for st in sts]
                new = []
                for h in heads:
                    m, acc = old[h]
                    m2 = jnp.maximum(m, jnp.max(sts[h], axis=0, keepdims=True))
                    pt = jnp.exp(sts[h] - m2).astype(BF16)
                    new.append((m2, jnp.exp(m - m2) * acc + jnp.dot(vt_s[h, kj], pt, preferred_element_type=F32)))
                for h in heads:
                    m_s[h], acc_s[h] = new[h]

            if full:
                _loop(0, qi * rq, lambda kj: step(kj, False))
                _loop(qi * rq, (qi + 1) * rq, lambda kj: step(kj, True))
            else:
                _loop(jnp.maximum(qi * rq - wblk, 0), (qi + 1) * rq, lambda kj: step(kj, True))
            outs = []
            for h in heads:
                acc_t = acc_s[h]
                den = acc_t[_bias_lane(h % 2):_bias_lane(h % 2) + 1, :]
                outs.append(jnp.transpose(acc_t / den))
                lse_ref[h, qi] = m_s[h] + jnp.log(den)
            for pp in range(pairs):
                o_refs[pp][tok(qi, bq), :] = jnp.where(m_a, outs[2 * pp], outs[2 * pp + 1])

        _loop(0, nq, qblock)

    col, vec, seq, stat = _attn_specs(t, pairs, stride, nq, bq)
    ins = [src] * (3 * pairs) + [gq, gk]
    specs = [col(off + pp) for off in offs for pp in range(pairs)] + [vec, vec]
    if rope is not None:
        ins += list(rope)
        specs += [seq, seq]
    if biased:
        ins.append(kbias)
        specs.append(pl.BlockSpec((None, 2, nk, 1, bk), lambda z, p, c: (z, p, 0, 0, 0)))
    scratch = [pltpu.VMEM((pairs, n, LANES), BF16), pltpu.VMEM((2 * pairs, n, LANES), BF16)]
    scratch += [pltpu.VMEM((2 * pairs, nk, LANES, bk), BF16), pltpu.VMEM((2 * pairs, LANES, bq), F32)]
    scratch += [pltpu.VMEM((2 * pairs, 1, bq), F32)]
    assert pairs in (1, npairs)
    ospec = pl.BlockSpec((None, t, LANES), lambda z, p, c: (z, 0, p))
    *os_, lse = _call(
        body, cargo=cargo, name=name, grid=(bs, npairs // pairs, stride), in_specs=specs, out_specs=[ospec] * pairs + [stat],
        out_shape=[jax.ShapeDtypeStruct((bs, t, LANES * npairs // pairs), F32)] * pairs
        + [jax.ShapeDtypeStruct((bs * stride, 2 * npairs, nq, 1, bq), F32)],
        scratch_shapes=scratch, compiler_params=_params(("parallel", "parallel", "arbitrary")),
    )(*ins)
    return (os_[0] if pairs == 1 else jnp.concatenate(os_, axis=-1)), lse


def _attn_bwd(src, offs, npairs, gq, gk, o, do, lse, *, rope=None, kbias=None, window, stride=1, pairs=1, name, cargo=None):
    bs, t, _ = src.shape
    n = t // stride
    full = window >= n
    bq, bk = _att_blocks(n, wide_keys=full)
    nq, nk, rq = n // bq, n // bk, bq // bk
    wblk = -(-window // bk)
    biased = kbias is not None
    heads = range(2 * pairs)
    gdt = BF16 if stride == 1 else F32

    def body(*refs):
        it = iter(refs)
        q_refs, k_refs, v_refs = ([next(it) for _ in range(pairs)] for _ in range(3))
        gq_ref, gk_ref = next(it), next(it)
        o_refs, do_refs = ([next(it) for _ in range(pairs)] for _ in range(2))
        lse_ref = next(it)
        rope_refs = (next(it), next(it)) if rope is not None else None
        kb_ref = next(it) if biased else None
        dq_refs, dk_refs, dv_refs = ([next(it) for _ in range(pairs)] for _ in range(3))
        dg_ref = next(it)
        dkb_ref = next(it) if biased else None
        qn_s, kh_s, vb_s, kt_s, dqn_s, dkh_s, dv_s, dq_s, rs_s = (next(it) for _ in range(9))
        m_a = _mask_a()
        masks = (m_a, jnp.logical_not(m_a))
        lane = lax.broadcasted_iota(jnp.int32, (1, LANES), 1)
        tok = _stream_rows(stride)

        @pl.when((pl.program_id(0) == 0) & (pl.program_id(1) == 0) & (pl.program_id(2) == 0))
        def _():
            dg_ref[...] = jnp.zeros_like(dg_ref)

        def prep(c):
            rows = _rows(c, bk)
            trows = tok(c, bk)
            for pp in range(pairs):
                qn = _normed(q_refs[pp], gq_ref, rope_refs, trows, m_a)
                qn_s[pp, rows, :] = (qn * SCALE).astype(BF16)
                kn = _normed(k_refs[pp], gk_ref, rope_refs, trows, m_a)
                kt_s[pp, c] = jnp.transpose(kn).astype(BF16)
                vb_s[pp, rows, :] = v_refs[pp][trows, :].astype(BF16)
                for half in (0, 1):
                    h = 2 * pp + half
                    kh_s[h, rows, :] = _k_for_head(kn, kb_ref[h, c] if biased else None, half, masks[half], lane, bk)

        _loop(0, nk, prep)
        dkh_s[...] = jnp.zeros_like(dkh_s)
        dv_s[...] = jnp.zeros_like(dv_s)
        qk = _q_minus_k(bk, bq)

        def qblock(qi):
            rows = _rows(qi, bq)
            trows = tok(qi, bq)
            dobs = [do_refs[pp][trows, :] for pp in range(pairs)]
            deltas = [_head_rows(dobs[pp] * o_refs[pp][trows, :]) for pp in range(pairs)]
            qh = [_q_for_head(qn_s[h // 2, rows, :], h % 2, masks[h % 2], lane, biased) for h in heads]
            doms = [jnp.where(masks[h % 2], dobs[h // 2], 0.0).astype(BF16) for h in heads]
            delta = [deltas[h // 2][h % 2:h % 2 + 1, :] for h in heads]
            lses = [lse_ref[h, qi] for h in heads]
            dq_s[...] = jnp.zeros_like(dq_s)
            if biased:
                for h in heads:
                    rs_s[h, qi] = jnp.zeros((1, bq), F32)

            def step(kj, masked):
                cols = _rows(kj, bk)
                vbs = [vb_s[pp, cols, :] for pp in range(pairs)]
                kts = [kt_s[pp, kj] for pp in range(pairs)]
                sts = [lax.dot_general(kh_s[h, cols, :], qh[h], NT, preferred_element_type=F32) for h in heads]
                dpts = [lax.dot_general(vbs[h // 2], doms[h], NT, preferred_element_type=F32) for h in heads]
                if masked:
                    d = qk + (qi * bq - kj * bk)
                    ok = (d >= 0) & (d <= window)
                    sts = [jnp.where(ok, st, NEG) for st in sts]
                new = []
                for h in heads:
                    pt = jnp.exp(sts[h] - lses[h])
                    dst = pt * (dpts[h] - delta[h])
                    dsb = dst.astype(BF16)
                    tk = jnp.dot(dsb, qh[h], preferred_element_type=F32)
                    if biased:
                        tk = tk + jnp.dot((dst - dsb.astype(F32)).astype(BF16), qh[h], preferred_element_type=F32)
                    tv = jnp.dot(pt.astype(BF16), doms[h], preferred_element_type=F32)
                    tq = jnp.dot(kts[h // 2], dsb, preferred_element_type=F32)
                    new.append((tk, tv, tq, jnp.sum(dst, axis=0, keepdims=True) if biased else None))
                for h in heads:
                    dkh_s[h, cols, :] += new[h][0]
                    dq_s[h] += new[h][2]
                    if biased:
                        rs_s[h, qi] += new[h][3]
                for pp in range(pairs):
                    dv_s[pp, cols, :] += new[2 * pp][1] + new[2 * pp + 1][1]

            if full:
                _loop(0, qi * rq, lambda kj: step(kj, False))
                _loop(qi * rq, (qi + 1) * rq, lambda kj: step(kj, True))
            else:
                _loop(jnp.maximum(qi * rq - wblk, 0), (qi + 1) * rq, lambda kj: step(kj, True))
            for pp in range(pairs):
                dqn_s[pp, rows, :] = jnp.where(m_a, jnp.transpose(dq_s[2 * pp]), jnp.transpose(dq_s[2 * pp + 1])) * SCALE

        _loop(0, nq, qblock)

        def finish(c, carry):
            rows = _rows(c, bq)
            trows = tok(c, bq)
            dgq, dgk = carry
            for pp in range(pairs):
                dk_pair = [dkh_s[2 * pp, rows, :], dkh_s[2 * pp + 1, rows, :]]
                if biased:
                    for half in (0, 1):
                        b = _bias_lane(half)
                        dkb_row = jnp.transpose(dk_pair[half])[b:b + 1, :] - rs_s[2 * pp + half, c]
                        for j in range(rq):
                            dkb_ref[2 * pp + half, c * rq + j] = dkb_row[:, j * bk:(j + 1) * bk]
                dv_refs[pp][trows, :] = dv_s[pp, rows, :].astype(gdt)
                grads = (dqn_s[pp, rows, :], jnp.where(m_a, dk_pair[0], dk_pair[1]))
                out = []
                for src_ref, g_ref, dxn, dst in ((q_refs[pp], gq_ref, grads[0], dq_refs[pp]),
                                                 (k_refs[pp], gk_ref, grads[1], dk_refs[pp])):
                    xv = src_ref[trows, :]
                    inv = _head_inv(xv, m_a)
                    y = xv * inv
                    if rope_refs is not None:
                        dxn = dxn * rope_refs[0][trows, :] + _swap32(dxn * rope_refs[1][trows, :])
                    dy = dxn * g_ref[...]
                    dst[trows, :] = (inv * (dy - y * (_half_sum(dy * y, m_a) * (1.0 / HEAD_DIM)))).astype(gdt)
                    out.append(jnp.sum(dxn * y, axis=0, keepdims=True))
                dgq, dgk = dgq + out[0], dgk + out[1]
            return dgq, dgk

        zero = jnp.zeros((1, LANES), F32)
        dgq, dgk = lax.fori_loop(0, nq, finish, (zero, zero))
        dg_ref[0:1, :] += dgq
        dg_ref[1:2, :] += dgk

    assert pairs in (1, npairs)
    col, vec, seq, stat = _attn_specs(t, pairs, stride, nq, bq)
    ins = [src] * (3 * pairs) + [gq, gk] + [o] * pairs + [do] * pairs + [lse]
    specs = [col(off + pp) for off in offs for pp in range(pairs)] + [vec, vec] + [col(pp) for pp in range(pairs)] * 2 + [stat]
    if rope is not None:
        ins += list(rope)
        specs += [seq, seq]
    sds = jax.ShapeDtypeStruct((bs, t, LANES * npairs // pairs), gdt)
    out_shape = [sds] * (3 * pairs) + [jax.ShapeDtypeStruct((8, LANES), F32)]
    ospec = pl.BlockSpec((None, t, LANES), lambda z, p, c: (z, 0, p))
    out_specs = [ospec] * (3 * pairs) + [pl.BlockSpec((8, LANES), lambda z, p, c: (0, 0))]
    if biased:
        kbspec = pl.BlockSpec((None, 2, nk, 1, bk), lambda z, p, c: (z, p, 0, 0, 0))
        ins.append(kbias)
        specs.append(kbspec)
        out_shape.append(jax.ShapeDtypeStruct(kbias.shape, F32))
        out_specs.append(kbspec)
    scratch = [pltpu.VMEM((pairs, n, LANES), BF16), pltpu.VMEM((2 * pairs, n, LANES), BF16), pltpu.VMEM((pairs, n, LANES), BF16)]
    scratch += [pltpu.VMEM((pairs, nk, LANES, bk), BF16), pltpu.VMEM((pairs, n, LANES), F32)]
    scratch += [pltpu.VMEM((2 * pairs, n, LANES), F32), pltpu.VMEM((pairs, n, LANES), F32)]
    scratch += [pltpu.VMEM((2 * pairs, LANES, bq), F32), pltpu.VMEM((2 * pairs, nq, 1, bq), F32)]
    res = _call(
        body, cargo=cargo, name=name, grid=(bs, npairs // pairs, stride), in_specs=specs, out_specs=out_specs, out_shape=out_shape,
        scratch_shapes=scratch, compiler_params=_params(("arbitrary", "arbitrary", "arbitrary")),
    )(*ins)
    return (list(res[:pairs]), list(res[pairs:2 * pairs]), list(res[2 * pairs:3 * pairs]), *res[3 * pairs:])


SB_LOG_PARTS = 2
SB_GRAD_PARTS = 1


def _log_sig_pair(z):
    lsn = jnp.minimum(-z, 0.0) - jnp.log(1.0 + jnp.exp(-jnp.abs(z)))
    return lsn, z + lsn


def _sb_specs(n, pairs, nq, bq):
    col = lambda off: pl.BlockSpec((None, n, LANES * pairs), lambda z, p: (z, 0, off // pairs + p))
    stat = pl.BlockSpec((None, 2 * pairs, nq, 1, bq), lambda z, p: (z, p, 0, 0, 0))
    return col, stat


def _sb_pairs(npairs):
    return 2 if npairs % 2 == 0 else 1


def _sb_fwd(src, offs, npairs, *, name, cargo=None):
    zs, n, _ = src.shape
    bq, bk = _att_blocks(n)
    nq, nk, rq = n // bq, n // bk, bq // bk
    pairs = _sb_pairs(npairs)
    heads = range(2 * pairs)
    lanes = [slice(pp * LANES, (pp + 1) * LANES) for pp in range(pairs)]

    def body(q_ref, k_ref, v_ref, o_ref, lt_ref, qs_s, kb_s, vt_s, acc_s, c_s):
        m_a = _mask_a()
        masks = (m_a, jnp.logical_not(m_a))

        def prep(c):
            rows = _rows(c, bk)
            for pp in range(pairs):
                qs_s[pp, rows, :] = (q_ref[rows, lanes[pp]] * SCALE).astype(BF16)
                kb_s[pp, rows, :] = k_ref[rows, lanes[pp]].astype(BF16)
                vt_s[pp, c] = jnp.transpose(v_ref[rows, lanes[pp]]).astype(BF16)

        _loop(0, nk, prep)
        qk = _q_minus_k(bk, bq)
        u_gt = _tri(bk, lambda r, c: c > r)

        def qblock(qi):
            rows = _rows(qi, bq)
            qms = [jnp.where(masks[h % 2], qs_s[h // 2, rows, :], 0) for h in heads]
            acc_s[...] = jnp.zeros_like(acc_s)
            c_s[...] = jnp.zeros_like(c_s)

            def step(kj, masked):
                cols = _rows(kj, bk)
                zts = [lax.dot_general(kb_s[h // 2, cols, :], qms[h], NT, preferred_element_type=F32) for h in heads]
                old = [c_s[h] for h in heads]
                if masked:
                    ok = (qk + (qi * bq - kj * bk)) > 0
                new = []
                for h in heads:
                    lsn, lsp = _log_sig_pair(zts[h])
                    if masked:
                        lsn = jnp.where(ok, lsn, 0.0)
                    at = jnp.exp(lsp + (old[h] + _cumdot_left(u_gt, lsn, SB_LOG_PARTS)))
                    if masked:
                        at = jnp.where(ok, at, 0.0)
                    new.append((jnp.dot(vt_s[h // 2, kj], at.astype(BF16), preferred_element_type=F32),
                                old[h] + jnp.sum(lsn, axis=0, keepdims=True)))
                for h in heads:
                    acc_s[h] += new[h][0]
                    c_s[h] = new[h][1]

            _loop(0, rq, lambda t: step((qi + 1) * rq - 1 - t, True))
            _loop(0, qi * rq, lambda t: step(qi * rq - 1 - t, False))
            for pp in range(pairs):
                o_ref[rows, lanes[pp]] = jnp.where(m_a, jnp.transpose(acc_s[2 * pp]), jnp.transpose(acc_s[2 * pp + 1]))
            for h in heads:
                lt_ref[h, qi] = c_s[h]

        _loop(0, nq, qblock)

    col, stat = _sb_specs(n, pairs, nq, bq)
    scratch = [pltpu.VMEM((pairs, n, LANES), BF16)] * 2 + [pltpu.VMEM((pairs, nk, LANES, bk), BF16)]
    scratch += [pltpu.VMEM((2 * pairs, LANES, bq), F32), pltpu.VMEM((2 * pairs, 1, bq), F32)]
    return _call(
        body, cargo=cargo, name=name, grid=(zs, npairs // pairs),
        in_specs=[col(offs[0]), col(offs[1]), col(offs[2])], out_specs=[col(0), stat],
        out_shape=[jax.ShapeDtypeStruct((zs, n, LANES * npairs), F32), jax.ShapeDtypeStruct((zs, 2 * npairs, nq, 1, bq), F32)],
        scratch_shapes=scratch, compiler_params=_params(("parallel", "parallel")),
    )(src, src, src)


def _sb_bwd(src, offs, npairs, do, ltot, *, name, cargo=None):
    zs, n, _ = src.shape
    bq, bk = _att_blocks(n)
    nq, nk, rq = n // bq, n // bk, bq // bk
    pairs = _sb_pairs(npairs)
    heads = range(2 * pairs)
    lanes = [slice(pp * LANES, (pp + 1) * LANES) for pp in range(pairs)]

    def body(q_ref, k_ref, v_ref, do_ref, lt_ref, dq_ref, dk_ref, dv_ref, qs_s, kb_s, vb_s, kt_s, dk_s, dv_s, dq_s, lp_s, ep_s):
        m_a = _mask_a()
        masks = (m_a, jnp.logical_not(m_a))

        def prep(c):
            rows = _rows(c, bk)
            for pp in range(pairs):
                qs_s[pp, rows, :] = (q_ref[rows, lanes[pp]] * SCALE).astype(BF16)
                kv = k_ref[rows, lanes[pp]]
                kb_s[pp, rows, :] = kv.astype(BF16)
                kt_s[pp, c] = jnp.transpose(kv).astype(BF16)
                vb_s[pp, rows, :] = v_ref[rows, lanes[pp]].astype(BF16)

        _loop(0, nk, prep)
        dk_s[...] = jnp.zeros_like(dk_s)
        dv_s[...] = jnp.zeros_like(dv_s)
        qk = _q_minus_k(bk, bq)
        u_le = _tri(bk, lambda r, c: c <= r)
        u_lt = _tri(bk, lambda r, c: c < r)

        def qblock(qi):
            rows = _rows(qi, bq)
            qms = [jnp.where(masks[h % 2], qs_s[h // 2, rows, :], 0) for h in heads]
            doms = [jnp.where(masks[h % 2], do_ref[rows, lanes[h // 2]], 0.0).astype(BF16) for h in heads]
            lts = [lt_ref[h, qi] for h in heads]
            dq_s[...] = jnp.zeros_like(dq_s)
            lp_s[...] = jnp.zeros_like(lp_s)
            ep_s[...] = jnp.zeros_like(ep_s)

            def step(kj, masked):
                cols = _rows(kj, bk)
                kbs = [kb_s[pp, cols, :] for pp in range(pairs)]
                kts = [kt_s[pp, kj] for pp in range(pairs)]
                zts = [lax.dot_general(kbs[h // 2], qms[h], NT, preferred_element_type=F32) for h in heads]
                dats = [lax.dot_general(vb_s[h // 2, cols, :], doms[h], NT, preferred_element_type=F32) for h in heads]
                old = [(lp_s[h], ep_s[h]) for h in heads]
                if masked:
                    ok = (qk + (qi * bq - kj * bk)) > 0
                new = []
                for h in heads:
                    lp, ep = old[h]
                    lsn, lsp = _log_sig_pair(zts[h])
                    sig = jnp.exp(lsp)
                    if masked:
                        lsn = jnp.where(ok, lsn, 0.0)
                    at = jnp.exp(lsp + (lts[h] - (lp + _cumdot_left(u_le, lsn, SB_LOG_PARTS))))
                    if masked:
                        at = jnp.where(ok, at, 0.0)
                    et = dats[h] * at
                    big_e = ep + _cumdot_left(u_lt, et, SB_GRAD_PARTS)
                    dzt = et - sig * (et + big_e)
                    if masked:
                        dzt = jnp.where(ok, dzt, 0.0)
                    dzb = dzt.astype(BF16)
                    new.append((jnp.dot(dzb, qms[h], preferred_element_type=F32),
                                jnp.dot(at.astype(BF16), doms[h], preferred_element_type=F32),
                                jnp.dot(kts[h // 2], dzb, preferred_element_type=F32),
                                lp + jnp.sum(lsn, axis=0, keepdims=True), ep + jnp.sum(et, axis=0, keepdims=True)))
                for h in heads:
                    dq_s[h] += new[h][2]
                    lp_s[h], ep_s[h] = new[h][3], new[h][4]
                for pp in range(pairs):
                    dk_s[pp, cols, :] += new[2 * pp][0] + new[2 * pp + 1][0]
                    dv_s[pp, cols, :] += new[2 * pp][1] + new[2 * pp + 1][1]

            _loop(0, qi * rq, lambda kj: step(kj, False))
            _loop(qi * rq, (qi + 1) * rq, lambda kj: step(kj, True))
            for pp in range(pairs):
                dq = jnp.where(m_a, jnp.transpose(dq_s[2 * pp]), jnp.transpose(dq_s[2 * pp + 1]))
                dq_ref[rows, lanes[pp]] = (dq * SCALE).astype(BF16)

        _loop(0, nq, qblock)

        def store(c):
            rows = _rows(c, bk)
            for pp in range(pairs):
                dk_ref[rows, lanes[pp]] = dk_s[pp, rows, :].astype(BF16)
                dv_ref[rows, lanes[pp]] = dv_s[pp, rows, :].astype(BF16)

        _loop(0, nk, store)

    col, stat = _sb_specs(n, pairs, nq, bq)
    ospec = col(0)
    sds = jax.ShapeDtypeStruct((zs, n, LANES * npairs), BF16)
    scratch = [pltpu.VMEM((pairs, n, LANES), BF16)] * 3 + [pltpu.VMEM((pairs, nk, LANES, bk), BF16)]
    scratch += [pltpu.VMEM((pairs, n, LANES), F32)] * 2
    scratch += [pltpu.VMEM((2 * pairs, LANES, bq), F32), pltpu.VMEM((2 * pairs, 1, bq), F32), pltpu.VMEM((2 * pairs, 1, bq), F32)]
    return _call(
        body, cargo=cargo, name=name, grid=(zs, npairs // pairs),
        in_specs=[col(offs[0]), col(offs[1]), col(offs[2]), ospec, stat],
        out_specs=[ospec] * 3, out_shape=[sds] * 3, scratch_shapes=scratch,
        compiler_params=_params(("parallel", "parallel")),
    )(src, src, src, do, ltot)


def _fox_gate_fwd(lg, bias, *, name):
    bs, nh, t = lg.shape
    blk = min(LANES, t)

    def body(lg_ref, b_ref, kb_ref):
        u_le = _tri(blk, lambda r, c: r <= c)
        carry = jnp.zeros((nh, 1), F32)
        for j in range(t // blk):
            sl = slice(j * blk, (j + 1) * blk)
            xv = lg_ref[:, sl] + b_ref[...]
            lf = jnp.minimum(xv, 0.0) - jnp.log(1.0 + jnp.exp(-jnp.abs(xv)))
            kb_ref[:, sl] = -(carry + _cumdot(lf, u_le, 3))
            carry = carry + jnp.sum(lf, axis=1, keepdims=True)

    spec = pl.BlockSpec((None, nh, t), lambda i: (i, 0, 0))
    return _pcall(
        body, name=name, grid=(bs,), in_specs=[spec, pl.BlockSpec((nh, 1), lambda i: (0, 0))], out_specs=spec,
        out_shape=jax.ShapeDtypeStruct((bs, nh, t), F32), compiler_params=_params(("parallel",)),
    )(lg, bias)


def _fox_gate_bwd(dkb, lg, bias, *, name):
    bs, nh, t = lg.shape
    blk = min(LANES, t)

    def body(dkb_ref, lg_ref, b_ref, dlg_ref, db_ref):
        @pl.when(pl.program_id(0) == 0)
        def _():
            db_ref[...] = jnp.zeros_like(db_ref)

        u_ge = _tri(blk, lambda r, c: r >= c)
        carry = jnp.zeros((nh, 1), F32)
        tot = jnp.zeros((nh, 1), F32)
        for j in reversed(range(t // blk)):
            sl = slice(j * blk, (j + 1) * blk)
            df = -dkb_ref[:, sl]
            dlf = carry + _cumdot(df, u_ge, 3)
            carry = carry + jnp.sum(df, axis=1, keepdims=True)
            xv = lg_ref[:, sl] + b_ref[...]
            dlg = dlf * jax.nn.sigmoid(-xv)
            dlg_ref[:, sl] = dlg
            tot = tot + jnp.sum(dlg, axis=1, keepdims=True)
        db_ref[...] += jnp.broadcast_to(tot, db_ref.shape)

    spec = pl.BlockSpec((None, nh, t), lambda i: (i, 0, 0))
    return _pcall(
        body, name=name, grid=(bs,), in_specs=[spec, spec, pl.BlockSpec((nh, 1), lambda i: (0, 0))],
        out_specs=[spec, pl.BlockSpec((nh, LANES), lambda i: (0, 0))],
        out_shape=[jax.ShapeDtypeStruct((bs, nh, t), F32), jax.ShapeDtypeStruct((nh, LANES), F32)],
        compiler_params=_params(("arbitrary",)),
    )(dkb, lg, bias)


def _place():
    return lax.axis_index("x"), lax.axis_index("y"), lax.axis_index("c")


def _flip(v, f):
    return 1 - v if f else v


FLIPS = [(fx, fy, fc) for fx in (0, 1) for fy in (0, 1) for fc in (0, 1)][1:]


def _comm_sems(nw):
    return [pltpu.SemaphoreType.DMA((7, nw)), pltpu.SemaphoreType.DMA((7, nw)), pltpu.SemaphoreType.DMA((nw,))]


def _gather_cargo(shards, on_done):
    nw = len(shards)

    def parts(x_refs, out_refs, sems):
        send_sems, recv_sems, local_sems = sems
        x, y, cc = _place()
        me, sibling = (x, y, cc), (x, y, 1 - cc)
        chips = [(1 - x, y), (x, 1 - y), (1 - x, 1 - y)]

        def slot(i, px, py, pc):
            return out_refs[i].at[4 * px + 2 * py + pc]

        def copy(i, k, block, to, src=None):
            return pltpu.make_async_remote_copy(
                src_ref=slot(i, *block) if src is None else src, dst_ref=slot(i, *block),
                send_sem=send_sems.at[k, i], recv_sem=recv_sems.at[k, i], device_id=to, device_id_type=MESH)

        mine = [pltpu.make_async_copy(x_refs[i], slot(i, *me), local_sems.at[i]) for i in range(nw)]
        first = []
        for i in range(nw):
            first.append(copy(i, 0, me, sibling, src=x_refs[i]))
            first += [copy(i, 1 + j, me, (*chip, cc), src=x_refs[i]) for j, chip in enumerate(chips)]
        return me, sibling, chips, cc, copy, mine, first

    def start(x_refs, out_refs, sems):
        *_, mine, first = parts(x_refs, out_refs, sems)
        for cp in mine + first:
            cp.start()

    def finish(x_refs, out_refs, sems):
        me, sibling, chips, cc, copy, mine, first = parts(x_refs, out_refs, sems)
        passed = []
        for i in range(nw):
            for j, chip in enumerate(chips):
                copy(i, 1 + j, (*chip, cc), me).wait_recv()
                passed.append(copy(i, 4 + j, (*chip, cc), sibling))
                passed[-1].start()
        for i in range(nw):
            copy(i, 0, sibling, me).wait_recv()
            for j, chip in enumerate(chips):
                copy(i, 4 + j, (*chip, 1 - cc), me).wait_recv()
        for cp in first + passed:
            cp.wait_send()
        for cp in mine:
            cp.wait()

    out_shape = [jax.ShapeDtypeStruct((N_DEV, *s.shape), s.dtype) for s in shards]
    return _Cargo(shards, out_shape, _comm_sems(nw), start, finish, on_done)


def _scatter_cargo(slots, prev, layer, depth, on_done, row0=0, rows=None):
    nw = len(slots)

    def parts(refs, recv_refs, sems):
        g_refs = refs[:nw]
        send_sems, recv_sems, local_sems = sems
        x, y, cc = _place()
        my = 4 * x + 2 * y + cc

        def dst(i):
            return recv_refs[i].at[my, layer, pl.ds(row0, slots[i].shape[1])]

        mine, copies = [], []
        for i in range(nw):
            mine.append(pltpu.make_async_copy(g_refs[i].at[my], dst(i), local_sems.at[i]))
            for k, (fx, fy, fc) in enumerate(FLIPS):
                px, py, pc = _flip(x, fx), _flip(y, fy), _flip(cc, fc)
                copies.append(pltpu.make_async_remote_copy(
                    src_ref=g_refs[i].at[4 * px + 2 * py + pc], dst_ref=dst(i),
                    send_sem=send_sems.at[k, i], recv_sem=recv_sems.at[k, i], device_id=(px, py, pc), device_id_type=MESH))
        return mine, copies

    def start(refs, recv_refs, sems):
        mine, copies = parts(refs, recv_refs, sems)
        for cp in mine + copies:
            cp.start()

    def finish(refs, recv_refs, sems):
        mine, copies = parts(refs, recv_refs, sems)
        for cp in copies:
            cp.wait_recv()
        for cp in copies:
            cp.wait_send()
        for cp in mine:
            cp.wait()

    ins, aliases = list(slots), {}
    for i, p in enumerate(prev):
        if p is not None:
            aliases[len(ins)] = i
            ins.append(p)
    out_shape = [jax.ShapeDtypeStruct((N_DEV, depth, rows or s.shape[1], s.shape[2]), s.dtype) for s in slots]
    return _Cargo(ins, out_shape, _comm_sems(nw), start, finish, on_done, aliases)


def _exchange(cargo, *, name):
    def body(*refs):
        c_in = len(cargo.ins)
        c_out = len(cargo.out_shape)
        cargo.start(refs[:c_in], refs[c_in:c_in + c_out], refs[c_in + c_out:])
        cargo.finish(refs[:c_in], refs[c_in:c_in + c_out], refs[c_in + c_out:])

    hbm = pl.BlockSpec(memory_space=pl.ANY)
    res = _pcall(
        body, name=name, in_specs=[hbm] * len(cargo.ins), out_specs=[hbm] * len(cargo.out_shape), out_shape=cargo.out_shape,
        scratch_shapes=cargo.sems, input_output_aliases=dict(cargo.aliases),
    )(*cargo.ins)
    cargo.on_done(list(res))


def _allreduce_small(blob, *, name):
    r, c = blob.shape

    def body(x_ref, out_ref, buf, send_sems, recv_sems):
        x, y, cc = _place()
        my = 4 * x + 2 * y + cc
        copies = []
        for k, (fx, fy, fc) in enumerate(FLIPS):
            peer = (_flip(x, fx), _flip(y, fy), _flip(cc, fc))
            copies.append(pltpu.make_async_remote_copy(
                src_ref=x_ref, dst_ref=buf.at[my], send_sem=send_sems.at[k], recv_sem=recv_sems.at[k],
                device_id=peer, device_id_type=MESH))
        for cp in copies:
            cp.start()
        buf[my] = x_ref[...]
        for cp in copies:
            cp.wait_recv()
        for cp in copies:
            cp.wait_send()
        acc = buf[0]
        for i in range(1, N_DEV):
            acc = acc + buf[i]
        out_ref[...] = acc

    vmem = pl.BlockSpec(memory_space=pltpu.VMEM)
    return _pcall(
        body, name=name, in_specs=[vmem], out_specs=vmem, out_shape=jax.ShapeDtypeStruct((r, c), F32),
        scratch_shapes=[pltpu.VMEM((N_DEV, r, c), F32), pltpu.SemaphoreType.DMA((7,)), pltpu.SemaphoreType.DMA((7,))],
    )(blob)


BIG = ("w_in", "w_mlp_in", "w_mlp_out", "w_up_fox", "w_up_sb", "w_up_dil", "w_out")
ROW_SHARDED = ("w_out", "w_mlp_out")
SMALL = ("attn_norm", "b_forget", "q_norm_fox", "k_norm_fox", "q_norm_dil", "k_norm_dil", "mlp_norm")
BLOB_ROWS = 512


def _pack(parts, dtype):
    flat = jnp.concatenate([p.reshape(-1).astype(dtype) for p in parts])
    size = -(-flat.shape[0] // (BLOB_ROWS * LANES)) * (BLOB_ROWS * LANES)
    return jnp.pad(flat, (0, size - flat.shape[0])).reshape(-1, LANES)


def _unpack(blob, shapes):
    flat = blob.reshape(-1)
    out, off = [], 0
    for shp in shapes:
        size = 1
        for s in shp:
            size *= s
        out.append(flat[off:off + size].reshape(shp))
        off += size
    return out


def _join_shards(name, sh):
    if name in ROW_SHARDED:
        return sh.reshape(-1, sh.shape[2])
    return jnp.transpose(sh, (1, 0, 2)).reshape(sh.shape[1], -1)


def _split_shards(name, full):
    a, b = full.shape
    if name in ROW_SHARDED:
        return full.reshape(N_DEV, a // N_DEV, b)
    return jnp.transpose(full.reshape(a, N_DEV, b // N_DEV), (1, 0, 2))


def _in_segments(d_in):
    o1 = 3 * W_FOX
    o2 = o1 + N_HEADS_FOX
    return (0, o1, 0), (o2, d_in, -N_HEADS_FOX), (o1, o2, d_in - o2)


def _join_w_in(sh, dp):
    b = sh.shape[2]
    pieces = []
    for s, e, _ in _in_segments(N_DEV * b):
        for j in range(s // b, (e - 1) // b + 1):
            pieces.append(sh[j, :, max(s, j * b) - j * b:min(e, (j + 1) * b) - j * b])
    pieces.append(jnp.zeros((sh.shape[1], dp - N_DEV * b), sh.dtype))
    return jnp.concatenate(pieces, axis=1)


def _split_w_in(gp, d_in):
    b = d_in // N_DEV
    shards = []
    for j in range(N_DEV):
        runs = []
        for s, e, shift in sorted(_in_segments(d_in)):
            lo, hi = max(s, j * b), min(e, (j + 1) * b)
            if lo < hi:
                runs.append(gp[:, lo + shift:hi + shift])
        shards.append(jnp.concatenate(runs, axis=1))
    return jnp.stack(shards)


def _stat_to_tokens(st, r, b):
    hh = st.shape[1]
    n = st.shape[2] * st.shape[4]
    return jnp.transpose(st.reshape(b, r, hh, n), (0, 2, 3, 1)).reshape(b, hh, n * r)


def _stat_to_streams(tok, r, blk):
    b, hh, t = tok.shape
    n = t // r
    return jnp.transpose(tok.reshape(b, hh, n, r), (0, 3, 1, 2)).reshape(b * r, hh, n // blk, 1, blk)


def _rope_tables(positions):
    half = HEAD_DIM // 2
    inv = 1.0 / (ROPE_THETA ** (jnp.arange(half, dtype=F32) / half))
    ang = positions.astype(F32)[..., None] * inv
    cos, sin = jnp.cos(ang), jnp.sin(ang)
    return jnp.tile(cos, (1, 1, 4)), jnp.tile(jnp.concatenate([-sin, sin], axis=-1), (1, 1, 2))


def _gain2(g):
    return jnp.tile(g.reshape(1, HEAD_DIM), (1, 2))


def _dil_offs(g):
    c0 = (P_DIL + g * W_DIL) // LANES
    return c0, c0 + W_DILQ // LANES, c0 + 2 * W_DILQ // LANES


def _dil_pairs(r):
    return N_HEADS_DIL // 2


def _layer_fwd(l, x, w, small, ropes, bl, t, cargo):
    n, d = x.shape
    s = {}
    s["x"] = x
    s["h"] = _rmsnorm_fwd(x, small["attn_norm"][l].reshape(1, d), name=f"norm_attn_fwd{l}")
    proj = _mm(s["h"], w["w_in"][l], tn=PROJ_TILE, name=f"mm_proj{l}")
    s["proj"] = proj
    dp = proj.shape[1]
    proj3 = proj.reshape(bl, t, dp)
    p_fg = P_GATE + 3 * d

    lg = jnp.transpose(proj3[:, :, p_fg:p_fg + N_HEADS_FOX], (0, 2, 1))
    s["lg"] = lg
    kb = _fox_gate_fwd(lg, small["b_forget"][l].reshape(N_HEADS_FOX, 1), name=f"fox_gate_fwd{l}")
    blk = _att_blocks(t, wide_keys=True)[1]
    kb5 = kb.reshape(bl, N_HEADS_FOX, t // blk, 1, blk)
    s["kb5"] = kb5
    gqf, gkf = _gain2(small["q_norm_fox"][l]), _gain2(small["k_norm_fox"][l])
    fo = P_FOX // LANES
    fox_offs = (fo, fo + W_FOX // LANES, fo + 2 * W_FOX // LANES)
    out_a, lse_a = _attn_fwd(proj3, fox_offs, N_HEADS_FOX // 2, gqf, gkf, kbias=kb5, window=t, name=f"fox_fwd{l}",
                             cargo=cargo.get("fox_fwd"))
    s["out_a"], s["lse_a"] = out_a, lse_a

    so = P_SB // LANES
    sb_offs = (so, so + W_SB // LANES, so + 2 * W_SB // LANES)
    out_b, lt_b = _sb_fwd(proj3, sb_offs, N_HEADS_SB // 2, name=f"sb_fwd{l}", cargo=cargo.get("sb_fwd"))
    s["out_b"], s["lt_b"] = out_b, lt_b

    gqd, gkd = _gain2(small["q_norm_dil"][l]), _gain2(small["k_norm_dil"][l])
    os_, lses = [], []
    for g, (window, r) in enumerate(DIL_PATTERNS):
        o_g, lse_g = _attn_fwd(proj3, _dil_offs(g), N_HEADS_DIL // 2, gqd, gkd, rope=ropes, window=window // r, stride=r,
                               pairs=_dil_pairs(r), name=f"dil_fwd{l}_{g}")
        os_.append(o_g.reshape(n, W_DIL))
        lses.append(_stat_to_tokens(lse_g, r, bl).reshape(bl * N_HEADS_DIL, t))
    lse_c, *ws = _dil_weights(lses, name=f"dil_weights{l}")
    ws = [jnp.repeat(jnp.transpose(wg.reshape(bl, N_HEADS_DIL, t), (0, 2, 1)).reshape(n, N_HEADS_DIL), HEAD_DIM, axis=1) for wg in ws]
    out_c = _dil_mix(os_, ws, name=f"dil_mix{l}")
    s["out_c"], s["lse_c"] = out_c, lse_c.reshape(bl, N_HEADS_DIL, t)

    ys = [_mm(out_a.reshape(n, W_FOX), w["w_up_fox"][l], out_dtype=BF16, name=f"mm_up_fox{l}"),
          _mm(out_b.reshape(n, W_SB), w["w_up_sb"][l], out_dtype=BF16, name=f"mm_up_sb{l}"),
          _mm(out_c, w["w_up_dil"][l], out_dtype=BF16, name=f"mm_up_dil{l}")]
    s["ys"] = ys
    s["merged"] = _gate_merge_fwd(proj, ys, name=f"gate_merge_fwd{l}")
    x1 = _mm(s["merged"], w["w_out"][l], add=x, name=f"mm_out{l}")
    s["x1"] = x1

    s["h2"] = _rmsnorm_fwd(x1, small["mlp_norm"][l].reshape(1, d), name=f"norm_mlp_fwd{l}")
    s["u"], s["a"] = _mm(s["h2"], w["w_mlp_in"][l], relu2=True, out_dtype=BF16, name=f"mm_mlp_in{l}")
    x2 = _mm(s["a"], w["w_mlp_out"][l], add=x1, name=f"mm_mlp_out{l}")
    return x2, s


def _layer_bwd(l, dx2, s, w, small, ropes, bl, t, hooks):
    n, d = dx2.shape
    gw, gs = {}, {}

    def cargo(call):
        return hooks[call](gw) if call in hooks else None
    du = _mm(dx2, w["w_mlp_out"][l], tb=True, relu_grad_of=s["u"], out_dtype=BF16, name=f"mm_du{l}")
    gw["w_mlp_out"] = _mm(s["a"], dx2, ta=True, name=f"mm_dw_mlp_out{l}")
    gw["w_mlp_in"] = _mm(s["h2"], du, ta=True, name=f"mm_dw_mlp_in{l}")
    dh2 = _mm(du, w["w_mlp_in"][l], tb=True, name=f"mm_dh2{l}")
    dx1, gs["mlp_norm"] = _rmsnorm_bwd(s["x1"], small["mlp_norm"][l].reshape(1, d), dh2, dx2, name=f"norm_mlp_bwd{l}")

    dmerged = _mm(dx1, w["w_out"][l], tb=True, name=f"mm_dmerged{l}")
    gw["w_out"] = _mm(s["merged"], dx1, ta=True, name=f"mm_dw_out{l}")
    dya, dyb, dyc, dgl0, dgl1, dgl2 = _gate_merge_bwd(s["proj"], s["ys"], dmerged, name=f"gate_merge_bwd{l}")
    out_a2, out_b2 = s["out_a"].reshape(n, W_FOX), s["out_b"].reshape(n, W_SB)
    gw["w_up_fox"] = _mm(out_a2, dya, ta=True, name=f"mm_dw_up_fox{l}")
    gw["w_up_sb"] = _mm(out_b2, dyb, ta=True, name=f"mm_dw_up_sb{l}")
    gw["w_up_dil"] = _mm(s["out_c"], dyc, ta=True, name=f"mm_dw_up_dil{l}")
    dout_a = _mm(dya, w["w_up_fox"][l], tb=True, name=f"mm_dout_a{l}").reshape(bl, t, W_FOX)
    dout_b = _mm(dyb, w["w_up_sb"][l], tb=True, name=f"mm_dout_b{l}").reshape(bl, t, W_SB)
    dout_c = _mm(dyc, w["w_up_dil"][l], tb=True, name=f"mm_dout_c{l}").reshape(bl, t, W_DIL)

    proj3 = s["proj"].reshape(bl, t, -1)
    gqf, gkf = _gain2(small["q_norm_fox"][l]), _gain2(small["k_norm_fox"][l])
    fo = P_FOX // LANES
    fox_offs = (fo, fo + W_FOX // LANES, fo + 2 * W_FOX // LANES)
    (dq_a,), (dk_a,), (dv_a,), dg_a, dkb5 = _attn_bwd(proj3, fox_offs, N_HEADS_FOX // 2, gqf, gkf, s["out_a"], dout_a, s["lse_a"],
                                             kbias=s["kb5"], window=t, name=f"fox_bwd{l}", cargo=cargo("fox_bwd"))
    gs["fox_gains"] = dg_a
    dlg, gs["b_forget"] = _fox_gate_bwd(dkb5.reshape(bl, N_HEADS_FOX, t), s["lg"], small["b_forget"][l].reshape(N_HEADS_FOX, 1),
                                        name=f"fox_gate_bwd{l}")
    so = P_SB // LANES
    sb_offs = (so, so + W_SB // LANES, so + 2 * W_SB // LANES)
    dq_b, dk_b, dv_b = _sb_bwd(proj3, sb_offs, N_HEADS_SB // 2, dout_b, s["lt_b"], name=f"sb_bwd{l}", cargo=cargo("sb_bwd"))
    gqd, gkd = _gain2(small["q_norm_dil"][l]), _gain2(small["k_norm_dil"][l])
    out_c3 = s["out_c"].reshape(bl, t, W_DIL)
    dqs, dks, dvs, dgd = [], [], [], None
    for g, (window, r) in enumerate(DIL_PATTERNS):
        lse_g = _stat_to_streams(s["lse_c"], r, _att_blocks(t // r)[0])
        dq_g, dk_g, dv_g, dg_g = _attn_bwd(proj3, _dil_offs(g), N_HEADS_DIL // 2, gqd, gkd, out_c3, dout_c, lse_g, rope=ropes,
                                           window=window // r, stride=r, pairs=_dil_pairs(r), name=f"dil_bwd{l}_{g}")
        dqs += dq_g
        dks += dk_g
        dvs += dv_g
        dgd = dg_g if dgd is None else jnp.concatenate([dgd, dg_g], axis=0)
    gs["dil_gains"] = dgd

    dlg_cols = jnp.pad(jnp.transpose(dlg, (0, 2, 1)).reshape(n, N_HEADS_FOX), ((0, 0), (0, LANES - N_HEADS_FOX)))
    parts = [p.reshape(n, -1) for p in [dq_a, dk_a, dv_a, dq_b, dk_b, dv_b] + dqs + dks + dvs] + [dgl0, dgl1, dgl2, dlg_cols]
    dproj = _assemble_cols(parts, s["proj"].shape[1], name=f"assemble_dproj{l}")
    if "mm_dw_in_hi" in hooks:
        half = d // 2
        gw["w_in_lo"] = _mm(s["h"][:, :half], dproj, ta=True, tn=PROJ_TILE, name=f"mm_dw_in_lo{l}")
        gw["w_in_hi"] = _mm(s["h"][:, half:], dproj, ta=True, tn=PROJ_TILE, name=f"mm_dw_in_hi{l}", cargo=cargo("mm_dw_in_hi"))
    else:
        gw["w_in"] = _mm(s["h"], dproj, ta=True, tn=PROJ_TILE, name=f"mm_dw_in{l}")
    dh = _mm(dproj, w["w_in"][l], tb=True, tn=1024, tk=PROJ_TILE, name=f"mm_dh{l}", cargo=cargo("mm_dh"))
    dx, gs["attn_norm"] = _rmsnorm_bwd(s["x"], small["attn_norm"][l].reshape(1, d), dh, dx1, name=f"norm_attn_bwd{l}")
    return dx, gw, gs


def kernel(x, positions, attn_norm, w_in, b_forget, q_norm_fox, k_norm_fox, q_norm_dil, k_norm_dil, w_up_fox, w_up_sb, w_up_dil, w_out, mlp_norm, w_mlp_in, w_mlp_out, loss_target, m_attn_norm, m_w_in, m_b_forget, m_q_norm_fox, m_k_norm_fox, m_q_norm_dil, m_k_norm_dil, m_w_up_fox, m_w_up_sb, m_w_up_dil, m_w_out, m_mlp_norm, m_w_mlp_in, m_w_mlp_out, v_attn_norm, v_w_in, v_b_forget, v_q_norm_fox, v_k_norm_fox, v_q_norm_dil, v_k_norm_dil, v_w_up_fox, v_w_up_sb, v_w_up_dil, v_w_out, v_mlp_norm, v_w_mlp_in, v_w_mlp_out):
    bl, t, d = x.shape
    n = bl * t
    depth = attn_norm.shape[0]
    wl = dict(w_in=w_in, w_up_fox=w_up_fox, w_up_sb=w_up_sb, w_up_dil=w_up_dil, w_out=w_out, w_mlp_in=w_mlp_in, w_mlp_out=w_mlp_out)
    ml = dict(w_in=m_w_in, w_up_fox=m_w_up_fox, w_up_sb=m_w_up_sb, w_up_dil=m_w_up_dil, w_out=m_w_out, w_mlp_in=m_w_mlp_in, w_mlp_out=m_w_mlp_out)
    vl = dict(w_in=v_w_in, w_up_fox=v_w_up_fox, w_up_sb=v_w_up_sb, w_up_dil=v_w_up_dil, w_out=v_w_out, w_mlp_in=v_w_mlp_in, w_mlp_out=v_w_mlp_out)
    small = dict(attn_norm=attn_norm, b_forget=b_forget, q_norm_fox=q_norm_fox, k_norm_fox=k_norm_fox, q_norm_dil=q_norm_dil,
                 k_norm_dil=k_norm_dil, mlp_norm=mlp_norm)
    m_small = dict(attn_norm=m_attn_norm, b_forget=m_b_forget, q_norm_fox=m_q_norm_fox, k_norm_fox=m_k_norm_fox,
                   q_norm_dil=m_q_norm_dil, k_norm_dil=m_k_norm_dil, mlp_norm=m_mlp_norm)
    v_small = dict(attn_norm=v_attn_norm, b_forget=v_b_forget, q_norm_fox=v_q_norm_fox, k_norm_fox=v_k_norm_fox,
                   q_norm_dil=v_q_norm_dil, k_norm_dil=v_k_norm_dil, mlp_norm=v_mlp_norm)

    d_in = w_in.shape[-1] * N_DEV
    dp = -(-d_in // 512) * 512
    rest = [k for k in BIG if k != "w_in"]
    w = {k: [None] * depth for k in BIG}

    def gather(items):
        def done(res):
            for (k, l), sh in zip(items, res):
                w[k][l] = _join_w_in(sh, dp) if k == "w_in" else _join_shards(k, sh)

        return _gather_cargo([wl[k][l].astype(BF16) for k, l in items], done)

    _exchange(gather([("w_in", 0)]), name="gather_first")
    cos, sin = _rope_tables(positions)
    ropes = (cos, sin)

    h = x.reshape(n, d)
    saved = []
    for l in range(depth):
        cargo = {"fox_fwd": gather([(k, l) for k in rest])}
        if l + 1 < depth:
            cargo["sb_fwd"] = gather([("w_in", l + 1)])
        h, s = _layer_fwd(l, h, w, small, ropes, bl, t, cargo)
        saved.append(s)
    dy, loss_part = _loss_head(h, loss_target.reshape(n, d), name="loss_head")

    recv = {}

    def scatter(names, l, grads):
        def done(res):
            recv.update(zip(names, res))

        slots = [(_split_w_in(grads[k], d_in) if k == "w_in" else _split_shards(k, grads[k])).astype(BF16) for k in names]
        return _scatter_cargo(slots, [recv.get(k) for k in names], l, depth, done)

    def scatter_w_in_rows(part, l, row0):
        def done(res):
            recv["w_in"] = res[0]

        return _scatter_cargo([_split_w_in(part, d_in).astype(BF16)], [recv.get("w_in")], l, depth, done, row0=row0, rows=d)

    gss = [None] * depth
    above = None
    for l in reversed(range(depth)):
        hooks = {"fox_bwd": lambda gw, l=l: scatter(rest, l, gw)}
        if above is not None:
            hooks["sb_bwd"] = lambda gw, l=l, g=above: scatter(["w_in"], l + 1, g)
        if l == 0:
            hooks["mm_dw_in_hi"] = lambda gw: scatter_w_in_rows(gw["w_in_lo"], 0, 0)
            hooks["mm_dh"] = lambda gw: scatter_w_in_rows(gw["w_in_hi"], 0, d // 2)
        dy, above, gss[l] = _layer_bwd(l, dy, saved[l], w, small, ropes, bl, t, hooks)
    grad_x = dy.reshape(bl, t, d)

    g_big, d_big, m_big, v_big = {}, {}, {}, {}
    for k in BIG:
        g_big[k], d_big[k], m_big[k], v_big[k] = _adamw(recv[k], wl[k], ml[k], vl[k], name=f"adamw_{k}")

    rows = [loss_part]
    for l in range(depth):
        gs = gss[l]
        rows += [gs["attn_norm"].reshape(-1, LANES), gs["mlp_norm"].reshape(-1, LANES), gs["fox_gains"], gs["dil_gains"], gs["b_forget"]]
    row_counts = [r.shape[0] for r in rows]
    part = jnp.concatenate(rows, axis=0)
    pad_rows = -(-part.shape[0] // 8) * 8 - part.shape[0]
    summed = _allreduce_small(jnp.pad(part, ((0, pad_rows), (0, 0))), name="allreduce_small")
    pieces, off = [], 0
    for c in row_counts:
        pieces.append(summed[off:off + c])
        off += c
    loss = pieces[0][0, 0]

    def fold(row):
        return row[:HEAD_DIM] + row[HEAD_DIM:]

    g_small = {k: [] for k in SMALL}
    for l in range(depth):
        an, mn, fg, dg, bf = pieces[1 + 5 * l:6 + 5 * l]
        g_small["attn_norm"].append(an.reshape(d))
        g_small["mlp_norm"].append(mn.reshape(d))
        g_small["q_norm_fox"].append(fold(fg[0]))
        g_small["k_norm_fox"].append(fold(fg[1]))
        g_small["q_norm_dil"].append(fold(dg[0]) + fold(dg[8]) + fold(dg[16]))
        g_small["k_norm_dil"].append(fold(dg[1]) + fold(dg[9]) + fold(dg[17]))
        g_small["b_forget"].append(bf[:, 0])
    g_small = {k: jnp.stack(vs) for k, vs in g_small.items()}
    small_shapes = [small[k].shape for k in SMALL]
    outs = _adamw(_pack([g_small[k] for k in SMALL], F32)[None, None], _pack([small[k] for k in SMALL], F32)[None],
                  _pack([m_small[k] for k in SMALL], F32)[None], _pack([v_small[k] for k in SMALL], F32)[None], name="adamw_small")
    g_sm, d_sm, m_sm, v_sm = (dict(zip(SMALL, _unpack(o, small_shapes))) for o in outs)

    order = ("attn_norm", "w_in", "b_forget", "q_norm_fox", "k_norm_fox", "q_norm_dil", "k_norm_dil", "w_up_fox", "w_up_sb",
             "w_up_dil", "w_out", "mlp_norm", "w_mlp_in", "w_mlp_out")
    res = [loss, grad_x]
    for big, sm in ((g_big, g_sm), (d_big, d_sm), (m_big, m_sm), (v_big, v_sm)):
        res += [big[k] if k in big else sm[k] for k in order]
    return tuple(res)
```

```python
import jax
import jax.numpy as jnp
from jax import lax
from jax.experimental import pallas as pl
from jax.experimental.pallas import tpu as pltpu

F32 = jnp.float32
BF16 = jnp.bfloat16

HEAD_DIM = 64
LANES = 128
N_HEADS_FOX = 8
N_HEADS_SB = 8
N_HEADS_DIL = 4
DIL_PATTERNS = ((128, 1), (512, 4), (2048, 16))
ROPE_THETA = 10000.0
EPS = 1e-6
SCALE = 0.125
W_FOX = N_HEADS_FOX * HEAD_DIM
W_SB = N_HEADS_SB * HEAD_DIM
W_DIL = N_HEADS_DIL * HEAD_DIM
W_DILQ = len(DIL_PATTERNS) * W_DIL
P_FOX = 0
P_SB = 3 * W_FOX
P_DIL = P_SB + 3 * W_SB
P_GATE = P_DIL + 3 * W_DILQ
N_DEV = 8
ATT_BLK = 256
ATT_BQ = 512
NEG = -1e30
VMEM_LIMIT = 56 * 1024 * 1024
ADAMW_BLOCK_ELEMS = 128 * 1024
PROJ_TILE = 2176

ADAM_LR = 0.001
ADAM_B1 = 0.9
ADAM_B2 = 0.999
ADAM_EPS = 1e-08
ADAM_WD = 0.01
ADAM_STEP = 10

NT = (((1,), (1,)), ((), ()))
MESH = pl.DeviceIdType.MESH


def _pcall(body, **kw):
    return pl.pallas_call(body, **kw)


def _params(sem=None):
    return pltpu.CompilerParams(dimension_semantics=sem, vmem_limit_bytes=VMEM_LIMIT)


class _Cargo:
    def __init__(self, ins, out_shape, sems, start, finish, on_done, aliases=None):
        self.ins, self.out_shape, self.sems = list(ins), list(out_shape), list(sems)
        self.start, self.finish, self.on_done, self.aliases = start, finish, on_done, dict(aliases or {})


def _call(body, *, cargo=None, name, grid=(), in_specs, out_specs, out_shape, scratch_shapes=(), compiler_params=None):
    if cargo is None:
        kw = dict(grid=grid) if grid else {}
        if compiler_params is not None:
            kw["compiler_params"] = compiler_params
        return _pcall(body, name=name, in_specs=in_specs, out_specs=out_specs, out_shape=out_shape,
                      scratch_shapes=list(scratch_shapes), **kw)
    single = not isinstance(out_shape, (list, tuple))
    o_specs, o_shape = ([out_specs], [out_shape]) if single else (list(out_specs), list(out_shape))
    n_in, n_out, n_scr = len(in_specs), len(o_shape), len(scratch_shapes)
    c_in, c_out = len(cargo.ins), len(cargo.out_shape)

    def wrapped(*refs):
        ins, cins = refs[:n_in], refs[n_in:n_in + c_in]
        o0 = n_in + c_in
        outs, couts = refs[o0:o0 + n_out], refs[o0 + n_out:o0 + n_out + c_out]
        s0 = o0 + n_out + c_out
        scr, sems = refs[s0:s0 + n_scr], refs[s0 + n_scr:]
        first = last = None
        for ax, size in enumerate(grid):
            pid = pl.program_id(ax)
            first = (pid == 0) if first is None else first & (pid == 0)
            last = (pid == size - 1) if last is None else last & (pid == size - 1)
        if first is None:
            cargo.start(cins, couts, sems)
            body(*ins, *outs, *scr)
            cargo.finish(cins, couts, sems)
            return

        @pl.when(first)
        def _():
            cargo.start(cins, couts, sems)

        body(*ins, *outs, *scr)

        @pl.when(last)
        def _():
            cargo.finish(cins, couts, sems)

    hbm = pl.BlockSpec(memory_space=pl.ANY)
    kw = dict(grid=grid, compiler_params=_params(("arbitrary",) * len(grid))) if grid else {}
    call = _pcall(
        wrapped, name=name, in_specs=list(in_specs) + [hbm] * c_in, out_specs=o_specs + [hbm] * c_out,
        out_shape=o_shape + cargo.out_shape, scratch_shapes=list(scratch_shapes) + cargo.sems,
        input_output_aliases={n_in + i: n_out + j for i, j in cargo.aliases.items()}, **kw)

    def run(*args):
        res = call(*args, *cargo.ins)
        cargo.on_done(list(res[n_out:]))
        return res[0] if single else list(res[:n_out])

    return run


def _tile(dim, target, mult=LANES):
    t = (min(dim, target) // mult) * mult
    while t >= mult:
        if dim % t == 0:
            return t
        t -= mult
    return dim


def _mm(a, b, *, ta=False, tb=False, add=None, relu2=False, relu_grad_of=None, norm_gain=None, out_dtype=F32, name,
        tm=1024, tn=1024, tk=1024, cargo=None):
    m, k = (a.shape[1], a.shape[0]) if ta else a.shape
    n = b.shape[0] if tb else b.shape[1]
    tm, tn, tk = _tile(m, tm), _tile(n, tn), _tile(k, tk)
    nk = k // tk
    dn = (((0,) if ta else (1,), (1,) if tb else (0,)), ((), ()))

    extra = add if add is not None else relu_grad_of
    normed = norm_gain is not None
    assert not (normed and relu2) and (not normed or tn == n)

    def body(*refs):
        a_ref, b_ref = refs[:2]
        x_ref = refs[2] if extra is not None else None
        g_ref = refs[2 + (extra is not None)] if normed else None
        outs = refs[2 + (extra is not None) + normed:-1]
        acc = refs[-1]
        kk = pl.program_id(2)
        part = lax.dot_general(a_ref[...].astype(BF16), b_ref[...].astype(BF16), dn, preferred_element_type=F32)

        def finish(r):
            if add is not None:
                r = r + x_ref[...]
            if relu_grad_of is not None:
                r = r * (2.0 * jnp.maximum(x_ref[...].astype(F32), 0.0))
            outs[0][...] = r.astype(out_dtype)
            if relu2:
                rr = jnp.maximum(r, 0.0)
                outs[1][...] = (rr * rr).astype(BF16)
            if normed:
                inv = lax.rsqrt(jnp.mean(r * r, axis=1, keepdims=True) + EPS)
                outs[1][...] = (r * inv * g_ref[...]).astype(BF16)

        if nk == 1:
            finish(part)
            return

        @pl.when(kk == 0)
        def _():
            acc[...] = part

        @pl.when((kk > 0) & (kk < nk - 1))
        def _():
            acc[...] += part

        @pl.when(kk == nk - 1)
        def _():
            finish(acc[...] + part)

    a_spec = pl.BlockSpec((tk, tm), lambda i, j, q: (q, i)) if ta else pl.BlockSpec((tm, tk), lambda i, j, q: (i, q))
    b_spec = pl.BlockSpec((tn, tk), lambda i, j, q: (j, q)) if tb else pl.BlockSpec((tk, tn), lambda i, j, q: (q, j))
    o_spec = pl.BlockSpec((tm, tn), lambda i, j, q: (i, j))
    ins, specs = [a, b], [a_spec, b_spec]
    if extra is not None:
        ins.append(extra)
        specs.append(o_spec)
    if normed:
        ins.append(norm_gain)
        specs.append(pl.BlockSpec((1, tn), lambda i, j, q: (0, j)))
    two = relu2 or normed
    sds = jax.ShapeDtypeStruct((m, n), out_dtype)
    return _call(
        body, cargo=cargo, name=name, grid=(m // tm, n // tn, nk), in_specs=specs,
        out_specs=[o_spec, o_spec] if two else o_spec,
        out_shape=[sds, jax.ShapeDtypeStruct((m, n), BF16)] if two else sds,
        scratch_shapes=[pltpu.VMEM((tm, tn) if nk > 1 else (8, LANES), F32)],
        compiler_params=_params(("parallel", "parallel", "arbitrary")),
    )(*ins)


def _rmsnorm_fwd(x, g, *, name):
    n, d = x.shape
    tm = _tile(n, 256, 8)

    def body(x_ref, g_ref, h_ref):
        xv = x_ref[...]
        inv = lax.rsqrt(jnp.mean(xv * xv, axis=1, keepdims=True) + EPS)
        h_ref[...] = (xv * inv * g_ref[...]).astype(BF16)

    row = pl.BlockSpec((tm, d), lambda i: (i, 0))
    return _pcall(
        body, name=name, grid=(n // tm,), in_specs=[row, pl.BlockSpec((1, d), lambda i: (0, 0))], out_specs=row,
        out_shape=jax.ShapeDtypeStruct((n, d), BF16), compiler_params=_params(("parallel",)),
    )(x, g)


def _rmsnorm_bwd(x, g, dh, dres, *, name):
    n, d = x.shape
    tm = _tile(n, 256, 8)

    def body(x_ref, g_ref, dh_ref, dres_ref, dx_ref, dg_ref):
        @pl.when(pl.program_id(0) == 0)
        def _():
            dg_ref[...] = jnp.zeros_like(dg_ref)

        xv = x_ref[...]
        inv = lax.rsqrt(jnp.mean(xv * xv, axis=1, keepdims=True) + EPS)
        y = xv * inv
        dhv = dh_ref[...]
        dg_ref[...] += jnp.sum(dhv * y, axis=0, keepdims=True)
        dy = dhv * g_ref[...]
        dx_ref[...] = dres_ref[...] + inv * (dy - y * jnp.mean(dy * y, axis=1, keepdims=True))

    row = pl.BlockSpec((tm, d), lambda i: (i, 0))
    vec = pl.BlockSpec((1, d), lambda i: (0, 0))
    return _pcall(
        body, name=name, grid=(n // tm,), in_specs=[row, vec, row, row], out_specs=[row, vec],
        out_shape=[jax.ShapeDtypeStruct((n, d), F32), jax.ShapeDtypeStruct((1, d), F32)],
        compiler_params=_params(("arbitrary",)),
    )(x, g, dh, dres)


def _gate_specs(n, d):
    bw = 256 if d % 256 == 0 else LANES
    tm = _tile(n, 512, 8)
    nb = d // bw
    yspec = pl.BlockSpec((tm, bw), lambda i, j: (i, j))
    gspecs = [pl.BlockSpec((tm, bw), lambda i, j, b=b: (i, P_GATE // bw + b * nb + j)) for b in range(3)]
    return tm, bw, nb, yspec, gspecs


def _gate_merge_fwd(proj, ys, *, name):
    n, d = ys[0].shape
    tm, bw, nb, yspec, gspecs = _gate_specs(n, d)

    def body(g0, g1, g2, y0, y1, y2, o_ref):
        acc = jax.nn.sigmoid(g0[...]) * y0[...]
        acc += jax.nn.sigmoid(g1[...]) * y1[...]
        acc += jax.nn.sigmoid(g2[...]) * y2[...]
        o_ref[...] = acc.astype(BF16)

    return _pcall(
        body, name=name, grid=(n // tm, nb), in_specs=gspecs + [yspec] * 3, out_specs=yspec,
        out_shape=jax.ShapeDtypeStruct((n, d), BF16), compiler_params=_params(("parallel", "parallel")),
    )(proj, proj, proj, *ys)


def _gate_merge_bwd(proj, ys, dmerged, *, name):
    n, d = ys[0].shape
    tm, bw, nb, yspec, gspecs = _gate_specs(n, d)

    def body(g0, g1, g2, y0, y1, y2, dm_ref, dy0, dy1, dy2, dgl0, dgl1, dgl2):
        dm = dm_ref[...]
        for g_ref, y_ref, dy_ref, dgl_ref in ((g0, y0, dy0, dgl0), (g1, y1, dy1, dgl1), (g2, y2, dy2, dgl2)):
            s = jax.nn.sigmoid(g_ref[...])
            dy_ref[...] = (dm * s).astype(BF16)
            dgl_ref[...] = (dm * y_ref[...] * s * (1.0 - s)).astype(BF16)

    sds = jax.ShapeDtypeStruct((n, d), BF16)
    return _pcall(
        body, name=name, grid=(n // tm, nb), in_specs=gspecs + [yspec] * 4, out_specs=[yspec] * 6,
        out_shape=[sds] * 6, compiler_params=_params(("parallel", "parallel")),
    )(proj, proj, proj, *ys, dmerged)


def _loss_head(y, tgt, *, name):
    n, d = y.shape
    tm = _tile(n, 256, 8)
    steps = n // tm

    def body(y_ref, t_ref, dy_ref, loss_ref, acc):
        i = pl.program_id(0)

        @pl.when(i == 0)
        def _():
            acc[...] = jnp.zeros_like(acc)

        e = y_ref[...] - t_ref[...]
        dy_ref[...] = e * (1.0 / d)
        acc[...] += jnp.sum(e * e, axis=0, keepdims=True)

        @pl.when(i == steps - 1)
        def _():
            tot = jnp.sum(acc[...], axis=1, keepdims=True) * (0.5 / d)
            loss_ref[...] = jnp.broadcast_to(tot, loss_ref.shape)

    row = pl.BlockSpec((tm, d), lambda i: (i, 0))
    return _pcall(
        body, name=name, grid=(steps,), in_specs=[row, row], out_specs=[row, pl.BlockSpec((8, LANES), lambda i: (0, 0))],
        out_shape=[jax.ShapeDtypeStruct((n, d), F32), jax.ShapeDtypeStruct((8, LANES), F32)],
        scratch_shapes=[pltpu.VMEM((1, d), F32)], compiler_params=_params(("arbitrary",)),
    )(y, tgt)


def _assemble_cols(parts, width, *, name):
    n = parts[0].shape[0]
    tm = _tile(n, 256, 16)
    widths = [p.shape[1] for p in parts]

    def body(*refs):
        o_ref = refs[-1]
        off = 0
        for ref, w in zip(refs[:-1], widths):
            o_ref[:, off:off + w] = ref[...].astype(BF16)
            off += w
        if off < width:
            o_ref[:, off:] = jnp.zeros((tm, width - off), BF16)

    return _pcall(
        body, name=name, grid=(n // tm,), in_specs=[pl.BlockSpec((tm, w), lambda i: (i, 0)) for w in widths],
        out_specs=pl.BlockSpec((tm, width), lambda i: (i, 0)), out_shape=jax.ShapeDtypeStruct((n, width), BF16),
        compiler_params=_params(("parallel",)),
    )(*parts)


def _dil_weights(lses, *, name):
    shp = lses[0].shape

    def body(l0, l1, l2, lse_ref, w0, w1, w2):
        a, b, c = l0[...], l1[...], l2[...]
        m = jnp.maximum(jnp.maximum(a, b), c)
        ea, eb, ec = jnp.exp(a - m), jnp.exp(b - m), jnp.exp(c - m)
        den = ea + eb + ec
        lse_ref[...] = m + jnp.log(den)
        w0[...] = ea / den
        w1[...] = eb / den
        w2[...] = ec / den

    vmem = pl.BlockSpec(memory_space=pltpu.VMEM)
    return _pcall(body, name=name, in_specs=[vmem] * 3, out_specs=[vmem] * 4, out_shape=[jax.ShapeDtypeStruct(shp, F32)] * 4)(*lses)


def _dil_mix(os_, ws, *, name):
    n, w = os_[0].shape
    tm = _tile(n, 512, 8)

    def body(o0, o1, o2, w0, w1, w2, out_ref):
        out_ref[...] = w0[...] * o0[...] + w1[...] * o1[...] + w2[...] * o2[...]

    spec = pl.BlockSpec((tm, w), lambda i: (i, 0))
    return _pcall(
        body, name=name, grid=(n // tm,), in_specs=[spec] * 6, out_specs=spec, out_shape=jax.ShapeDtypeStruct((n, w), F32),
        compiler_params=_params(("parallel",)),
    )(*os_, *ws)


def _adamw(gsrc, w, m, v, *, name):
    s, dep, a, b = gsrc.shape
    ta = _tile(a, max(16, (ADAMW_BLOCK_ELEMS // b) // 16 * 16), 16)
    c1 = 1.0 / (1.0 - ADAM_B1 ** ADAM_STEP)
    c2 = 1.0 / (1.0 - ADAM_B2 ** ADAM_STEP)

    def body(gs_ref, w_ref, m_ref, v_ref, g_ref, d_ref, m2_ref, v2_ref):
        g = gs_ref[0].astype(F32)
        for i in range(1, s):
            g = g + gs_ref[i].astype(F32)
        m2 = ADAM_B1 * m_ref[...] + (1.0 - ADAM_B1) * g
        v2 = ADAM_B2 * v_ref[...] + (1.0 - ADAM_B2) * (g * g)
        g_ref[...] = g
        m2_ref[...] = m2
        v2_ref[...] = v2
        d_ref[...] = -ADAM_LR * ((m2 * c1) / (jnp.sqrt(v2 * c2) + ADAM_EPS) + ADAM_WD * w_ref[...])

    spec = pl.BlockSpec((None, ta, b), lambda l, i: (l, i, 0))
    sds = jax.ShapeDtypeStruct((dep, a, b), F32)
    return _pcall(
        body, name=name, grid=(dep, a // ta),
        in_specs=[pl.BlockSpec((s, None, ta, b), lambda l, i: (0, l, i, 0)), spec, spec, spec],
        out_specs=[spec] * 4, out_shape=[sds] * 4, compiler_params=_params(("parallel", "parallel")),
    )(gsrc, w, m, v)


def _mask_a():
    return lax.broadcasted_iota(jnp.int32, (1, LANES), 1) < HEAD_DIM


def _half_sum(x, m_a):
    sa = jnp.sum(jnp.where(m_a, x, 0.0), axis=1, keepdims=True)
    sb = jnp.sum(jnp.where(m_a, 0.0, x), axis=1, keepdims=True)
    return jnp.where(m_a, sa, sb)


def _head_inv(x, m_a):
    return lax.rsqrt(_half_sum(x * x, m_a) * (1.0 / HEAD_DIM) + EPS)


def _swap32(x):
    first = (lax.broadcasted_iota(jnp.int32, (1, LANES), 1) % HEAD_DIM) < (HEAD_DIM // 2)
    return jnp.where(first, pltpu.roll(x, LANES - HEAD_DIM // 2, 1), pltpu.roll(x, HEAD_DIM // 2, 1))


def _tri(blk, rel):
    r = lax.broadcasted_iota(jnp.int32, (blk, blk), 0)
    c = lax.broadcasted_iota(jnp.int32, (blk, blk), 1)
    return jnp.where(rel(r, c), 1.0, 0.0).astype(BF16)


def _cumdot(x, u, parts):
    acc = None
    r = x
    for i in range(parts):
        xi = r.astype(BF16)
        t = jnp.dot(xi, u, preferred_element_type=F32)
        acc = t if acc is None else acc + t
        if i + 1 < parts:
            r = r - xi.astype(F32)
    return acc


def _rows(i, blk):
    return pl.ds(pl.multiple_of(i * blk, blk), blk)


def _att_blk(n):
    return ATT_BLK if n % ATT_BLK == 0 else min(LANES, n)


def _att_blocks(n, wide_keys=False):
    bk = _att_blk(n)
    bq = ATT_BQ if n % ATT_BQ == 0 else bk
    return bq, (bq if wide_keys else bk)


def _loop(lo, hi, fn):
    def it(i, c):
        fn(i)
        return c

    lax.fori_loop(lo, hi, it, 0)


def _normed(src, g_ref, rope_refs, rows, m_a):
    xv = src[rows, :]
    xn = xv * _head_inv(xv, m_a) * g_ref[...]
    if rope_refs is not None:
        xn = xn * rope_refs[0][rows, :] + _swap32(xn) * rope_refs[1][rows, :]
    return xn


def _bias_lane(h):
    return HEAD_DIM if h == 0 else 0


def _k_for_head(kn, kb_row, h, m_h, lane, blk):
    out = jnp.where(m_h, kn, 0.0)
    if kb_row is not None:
        col = jnp.transpose(jnp.broadcast_to(kb_row, (LANES, blk)))
        hi = col.astype(BF16).astype(F32)
        mid = (col - hi).astype(BF16).astype(F32)
        lo = col - hi - mid
        b = _bias_lane(h)
        out = jnp.where(lane == b, hi, jnp.where(lane == b + 1, mid, jnp.where(lane == b + 2, lo, out)))
    return out.astype(BF16)


def _q_for_head(qb, h, m_h, lane, biased):
    out = jnp.where(m_h, qb, 0)
    if biased:
        b = _bias_lane(h)
        out = jnp.where((lane >= b) & (lane < b + 3), jnp.ones_like(out), out)
    return out


def _head_rows(x, parts=3):
    rr = lax.broadcasted_iota(jnp.int32, (8, LANES), 0)
    ll = lax.broadcasted_iota(jnp.int32, (8, LANES), 1)
    sel = jnp.where(((rr == 0) & (ll < HEAD_DIM)) | ((rr == 1) & (ll >= HEAD_DIM)), 1.0, 0.0).astype(BF16)
    acc = None
    rem = x
    for i in range(parts):
        xi = rem.astype(BF16)
        t = lax.dot_general(sel, xi, NT, preferred_element_type=F32)
        acc = t if acc is None else acc + t
        if i + 1 < parts:
            rem = rem - xi.astype(F32)
    return acc


def _cumdot_left(u, x, parts):
    acc = None
    rem = x
    for i in range(parts):
        xi = rem.astype(BF16)
        t = jnp.dot(u, xi, preferred_element_type=F32)
        acc = t if acc is None else acc + t
        if i + 1 < parts:
            rem = rem - xi.astype(F32)
    return acc


def _q_minus_k(bk, bq):
    return lax.broadcasted_iota(jnp.int32, (bk, bq), 1) - lax.broadcasted_iota(jnp.int32, (bk, bq), 0)


def _stream_rows(stride):
    if stride == 1:
        return _rows
    c = pl.program_id(2)
    return lambda i, blk: pl.ds(c + i * (blk * stride), blk, stride=stride)


def _attn_specs(t, pairs, stride, nq, bq):
    col = lambda off: pl.BlockSpec((None, t, LANES), lambda z, p, c: (z, 0, off + p * pairs))
    vec = pl.BlockSpec((1, LANES), lambda z, p, c: (0, 0))
    seq = pl.BlockSpec((None, t, LANES), lambda z, p, c: (z, 0, 0))
    stat = pl.BlockSpec((None, 2 * pairs, nq, 1, bq), lambda z, p, c: (z * stride + c, p, 0, 0, 0))
    return col, vec, seq, stat


def _attn_fwd(src, offs, npairs, gq, gk, *, rope=None, kbias=None, window, stride=1, pairs=1, name, cargo=None):
    bs, t, _ = src.shape
    n = t // stride
    full = window >= n
    bq, bk = _att_blocks(n, wide_keys=full)
    nq, nk, rq = n // bq, n // bk, bq // bk
    wblk = -(-window // bk)
    biased = kbias is not None
    heads = range(2 * pairs)

    def body(*refs):
        it = iter(refs)
        q_refs, k_refs, v_refs = ([next(it) for _ in range(pairs)] for _ in range(3))
        gq_ref, gk_ref = next(it), next(it)
        rope_refs = (next(it), next(it)) if rope is not None else None
        kb_ref = next(it) if biased else None
        o_refs = [next(it) for _ in range(pairs)]
        lse_ref, qn_s, kh_s, vt_s, acc_s, m_s = (next(it) for _ in range(6))
        m_a = _mask_a()
        masks = (m_a, jnp.logical_not(m_a))
        lane = lax.broadcasted_iota(jnp.int32, (1, LANES), 1)
        row = lax.broadcasted_iota(jnp.int32, (LANES, 1), 0)
        tok = _stream_rows(stride)

        def prep(c):
            rows = _rows(c, bk)
            trows = tok(c, bk)
            for pp in range(pairs):
                qn = _normed(q_refs[pp], gq_ref, rope_refs, trows, m_a)
                qn_s[pp, rows, :] = (qn * SCALE).astype(BF16)
                kn = _normed(k_refs[pp], gk_ref, rope_refs, trows, m_a)
                vt = jnp.transpose(v_refs[pp][trows, :])
                for half in (0, 1):
                    h = 2 * pp + half
                    kh_s[h, rows, :] = _k_for_head(kn, kb_ref[h, c] if biased else None, half, masks[half], lane, bk)
                    vt_s[h, c] = jnp.where(row == _bias_lane(half), 1.0, vt).astype(BF16)

        _loop(0, nk, prep)
        qk = _q_minus_k(bk, bq)

        def qblock(qi):
            rows = _rows(qi, bq)
            qh = [_q_for_head(qn_s[h // 2, rows, :], h % 2, masks[h % 2], lane, biased) for h in heads]
            m_s[...] = jnp.full(m_s.shape, NEG, F32)
            acc_s[...] = jnp.zeros_like(acc_s)

            def step(kj, masked):
                cols = _rows(kj, bk)
                sts = [lax.dot_general(kh_s[h, cols, :], qh[h], NT, preferred_element_type=F32) for h in heads]
                old = [(m_s[h], acc_s[h]) for h in heads]
                if masked:
                    d = qk + (qi * bq - kj * bk)
                    ok = (d >= 0) & (d <= window)
                    sts = [jnp.where(ok, st, NEG) for st in sts]
                new = []
                for h in heads:
                    m, acc = old[h]
                    m2 = jnp.maximum(m, jnp.max(sts[h], axis=0, keepdims=True))
                    pt = jnp.exp(sts[h] - m2).astype(BF16)
                    new.append((m2, jnp.exp(m - m2) * acc + jnp.dot(vt_s[h, kj], pt, preferred_element_type=F32)))
                for h in heads:
                    m_s[h], acc_s[h] = new[h]

            if full:
                _loop(0, qi * rq, lambda kj: step(kj, False))
                _loop(qi * rq, (qi + 1) * rq, lambda kj: step(kj, True))
            else:
                _loop(jnp.maximum(qi * rq - wblk, 0), (qi + 1) * rq, lambda kj: step(kj, True))
            outs = []
            for h in heads:
                acc_t = acc_s[h]
                den = acc_t[_bias_lane(h % 2):_bias_lane(h % 2) + 1, :]
                outs.append(jnp.transpose(acc_t / den))
                lse_ref[h, qi] = m_s[h] + jnp.log(den)
            for pp in range(pairs):
                o_refs[pp][tok(qi, bq), :] = jnp.where(m_a, outs[2 * pp], outs[2 * pp + 1])

        _loop(0, nq, qblock)

    col, vec, seq, stat = _attn_specs(t, pairs, stride, nq, bq)
    ins = [src] * (3 * pairs) + [gq, gk]
    specs = [col(off + pp) for off in offs for pp in range(pairs)] + [vec, vec]
    if rope is not None:
        ins += list(rope)
        specs += [seq, seq]
    if biased:
        ins.append(kbias)
        specs.append(pl.BlockSpec((None, 2, nk, 1, bk), lambda z, p, c: (z, p, 0, 0, 0)))
    scratch = [pltpu.VMEM((pairs, n, LANES), BF16), pltpu.VMEM((2 * pairs, n, LANES), BF16)]
    scratch += [pltpu.VMEM((2 * pairs, nk, LANES, bk), BF16), pltpu.VMEM((2 * pairs, LANES, bq), F32)]
    scratch += [pltpu.VMEM((2 * pairs, 1, bq), F32)]
    assert pairs in (1, npairs)
    ospec = pl.BlockSpec((None, t, LANES), lambda z, p, c: (z, 0, p))
    *os_, lse = _call(
        body, cargo=cargo, name=name, grid=(bs, npairs // pairs, stride), in_specs=specs, out_specs=[ospec] * pairs + [stat],
        out_shape=[jax.ShapeDtypeStruct((bs, t, LANES * npairs // pairs), F32)] * pairs
        + [jax.ShapeDtypeStruct((bs * stride, 2 * npairs, nq, 1, bq), F32)],
        scratch_shapes=scratch, compiler_params=_params(("parallel", "parallel", "arbitrary")),
    )(*ins)
    return (os_[0] if pairs == 1 else jnp.concatenate(os_, axis=-1)), lse


def _attn_bwd(src, offs, npairs, gq, gk, o, do, lse, *, rope=None, kbias=None, window, stride=1, pairs=1, name, cargo=None):
    bs, t, _ = src.shape
    n = t // stride
    full = window >= n
    bq, bk = _att_blocks(n, wide_keys=full)
    nq, nk, rq = n // bq, n // bk, bq // bk
    wblk = -(-window // bk)
    biased = kbias is not None
    heads = range(2 * pairs)
    gdt = BF16 if stride == 1 else F32

    def body(*refs):
        it = iter(refs)
        q_refs, k_refs, v_refs = ([next(it) for _ in range(pairs)] for _ in range(3))
        gq_ref, gk_ref = next(it), next(it)
        o_refs, do_refs = ([next(it) for _ in range(pairs)] for _ in range(2))
        lse_ref = next(it)
        rope_refs = (next(it), next(it)) if rope is not None else None
        kb_ref = next(it) if biased else None
        dq_refs, dk_refs, dv_refs = ([next(it) for _ in range(pairs)] for _ in range(3))
        dg_ref = next(it)
        dkb_ref = next(it) if biased else None
        qn_s, kh_s, vb_s, kt_s, dqn_s, dkh_s, dv_s, dq_s, rs_s = (next(it) for _ in range(9))
        m_a = _mask_a()
        masks = (m_a, jnp.logical_not(m_a))
        lane = lax.broadcasted_iota(jnp.int32, (1, LANES), 1)
        tok = _stream_rows(stride)

        @pl.when((pl.program_id(0) == 0) & (pl.program_id(1) == 0) & (pl.program_id(2) == 0))
        def _():
            dg_ref[...] = jnp.zeros_like(dg_ref)

        def prep(c):
            rows = _rows(c, bk)
            trows = tok(c, bk)
            for pp in range(pairs):
                qn = _normed(q_refs[pp], gq_ref, rope_refs, trows, m_a)
                qn_s[pp, rows, :] = (qn * SCALE).astype(BF16)
                kn = _normed(k_refs[pp], gk_ref, rope_refs, trows, m_a)
                kt_s[pp, c] = jnp.transpose(kn).astype(BF16)
                vb_s[pp, rows, :] = v_refs[pp][trows, :].astype(BF16)
                for half in (0, 1):
                    h = 2 * pp + half
                    kh_s[h, rows, :] = _k_for_head(kn, kb_ref[h, c] if biased else None, half, masks[half], lane, bk)

        _loop(0, nk, prep)
        dkh_s[...] = jnp.zeros_like(dkh_s)
        dv_s[...] = jnp.zeros_like(dv_s)
        qk = _q_minus_k(bk, bq)

        def qblock(qi):
            rows = _rows(qi, bq)
            trows = tok(qi, bq)
            dobs = [do_refs[pp][trows, :] for pp in range(pairs)]
            deltas = [_head_rows(dobs[pp] * o_refs[pp][trows, :]) for pp in range(pairs)]
            qh = [_q_for_head(qn_s[h // 2, rows, :], h % 2, masks[h % 2], lane, biased) for h in heads]
            doms = [jnp.where(masks[h % 2], dobs[h // 2], 0.0).astype(BF16) for h in heads]
            delta = [deltas[h // 2][h % 2:h % 2 + 1, :] for h in heads]
            lses = [lse_ref[h, qi] for h in heads]
            dq_s[...] = jnp.zeros_like(dq_s)
            if biased:
                for h in heads:
                    rs_s[h, qi] = jnp.zeros((1, bq), F32)

            def step(kj, masked):
                cols = _rows(kj, bk)
                vbs = [vb_s[pp, cols, :] for pp in range(pairs)]
                kts = [kt_s[pp, kj] for pp in range(pairs)]
                sts = [lax.dot_general(kh_s[h, cols, :], qh[h], NT, preferred_element_type=F32) for h in heads]
                dpts = [lax.dot_general(vbs[h // 2], doms[h], NT, preferred_element_type=F32) for h in heads]
                if masked:
                    d = qk + (qi * bq - kj * bk)
                    ok = (d >= 0) & (d <= window)
                    sts = [jnp.where(ok, st, NEG) for st in sts]
                new = []
                for h in heads:
                    pt = jnp.exp(sts[h] - lses[h])
                    dst = pt * (dpts[h] - delta[h])
                    dsb = dst.astype(BF16)
                    tk = jnp.dot(dsb, qh[h], preferred_element_type=F32)
                    if biased:
                        tk = tk + jnp.dot((dst - dsb.astype(F32)).astype(BF16), qh[h], preferred_element_type=F32)
                    tv = jnp.dot(pt.astype(BF16), doms[h], preferred_element_type=F32)
                    tq = jnp.dot(kts[h // 2], dsb, preferred_element_type=F32)
                    new.append((tk, tv, tq, jnp.sum(dst, axis=0, keepdims=True) if biased else None))
                for h in heads:
                    dkh_s[h, cols, :] += new[h][0]
                    dq_s[h] += new[h][2]
                    if biased:
                        rs_s[h, qi] += new[h][3]
                for pp in range(pairs):
                    dv_s[pp, cols, :] += new[2 * pp][1] + new[2 * pp + 1][1]

            if full:
                _loop(0, qi * rq, lambda kj: step(kj, False))
                _loop(qi * rq, (qi + 1) * rq, lambda kj: step(kj, True))
            else:
                _loop(jnp.maximum(qi * rq - wblk, 0), (qi + 1) * rq, lambda kj: step(kj, True))
            for pp in range(pairs):
                dqn_s[pp, rows, :] = jnp.where(m_a, jnp.transpose(dq_s[2 * pp]), jnp.transpose(dq_s[2 * pp + 1])) * SCALE

        _loop(0, nq, qblock)

        def finish(c, carry):
            rows = _rows(c, bq)
            trows = tok(c, bq)
            dgq, dgk = carry
            for pp in range(pairs):
                dk_pair = [dkh_s[2 * pp, rows, :], dkh_s[2 * pp + 1, rows, :]]
                if biased:
                    for half in (0, 1):
                        b = _bias_lane(half)
                        dkb_row = jnp.transpose(dk_pair[half])[b:b + 1, :] - rs_s[2 * pp + half, c]
                        for j in range(rq):
                            dkb_ref[2 * pp + half, c * rq + j] = dkb_row[:, j * bk:(j + 1) * bk]
                dv_refs[pp][trows, :] = dv_s[pp, rows, :].astype(gdt)
                grads = (dqn_s[pp, rows, :], jnp.where(m_a, dk_pair[0], dk_pair[1]))
                out = []
                for src_ref, g_ref, dxn, dst in ((q_refs[pp], gq_ref, grads[0], dq_refs[pp]),
                                                 (k_refs[pp], gk_ref, grads[1], dk_refs[pp])):
                    xv = src_ref[trows, :]
                    inv = _head_inv(xv, m_a)
                    y = xv * inv
                    if rope_refs is not None:
                        dxn = dxn * rope_refs[0][trows, :] + _swap32(dxn * rope_refs[1][trows, :])
                    dy = dxn * g_ref[...]
                    dst[trows, :] = (inv * (dy - y * (_half_sum(dy * y, m_a) * (1.0 / HEAD_DIM)))).astype(gdt)
                    out.append(jnp.sum(dxn * y, axis=0, keepdims=True))
                dgq, dgk = dgq + out[0], dgk + out[1]
            return dgq, dgk

        zero = jnp.zeros((1, LANES), F32)
        dgq, dgk = lax.fori_loop(0, nq, finish, (zero, zero))
        dg_ref[0:1, :] += dgq
        dg_ref[1:2, :] += dgk

    assert pairs in (1, npairs)
    col, vec, seq, stat = _attn_specs(t, pairs, stride, nq, bq)
    ins = [src] * (3 * pairs) + [gq, gk] + [o] * pairs + [do] * pairs + [lse]
    specs = [col(off + pp) for off in offs for pp in range(pairs)] + [vec, vec] + [col(pp) for pp in range(pairs)] * 2 + [stat]
    if rope is not None:
        ins += list(rope)
        specs += [seq, seq]
    sds = jax.ShapeDtypeStruct((bs, t, LANES * npairs // pairs), gdt)
    out_shape = [sds] * (3 * pairs) + [jax.ShapeDtypeStruct((8, LANES), F32)]
    ospec = pl.BlockSpec((None, t, LANES), lambda z, p, c: (z, 0, p))
    out_specs = [ospec] * (3 * pairs) + [pl.BlockSpec((8, LANES), lambda z, p, c: (0, 0))]
    if biased:
        kbspec = pl.BlockSpec((None, 2, nk, 1, bk), lambda z, p, c: (z, p, 0, 0, 0))
        ins.append(kbias)
        specs.append(kbspec)
        out_shape.append(jax.ShapeDtypeStruct(kbias.shape, F32))
        out_specs.append(kbspec)
    scratch = [pltpu.VMEM((pairs, n, LANES), BF16), pltpu.VMEM((2 * pairs, n, LANES), BF16), pltpu.VMEM((pairs, n, LANES), BF16)]
    scratch += [pltpu.VMEM((pairs, nk, LANES, bk), BF16), pltpu.VMEM((pairs, n, LANES), F32)]
    scratch += [pltpu.VMEM((2 * pairs, n, LANES), F32), pltpu.VMEM((pairs, n, LANES), F32)]
    scratch += [pltpu.VMEM((2 * pairs, LANES, bq), F32), pltpu.VMEM((2 * pairs, nq, 1, bq), F32)]
    res = _call(
        body, cargo=cargo, name=name, grid=(bs, npairs // pairs, stride), in_specs=specs, out_specs=out_specs, out_shape=out_shape,
        scratch_shapes=scratch, compiler_params=_params(("arbitrary", "arbitrary", "arbitrary")),
    )(*ins)
    return (list(res[:pairs]), list(res[pairs:2 * pairs]), list(res[2 * pairs:3 * pairs]), *res[3 * pairs:])


SB_LOG_PARTS = 2
SB_GRAD_PARTS = 1


def _log_sig_pair(z):
    lsn = jnp.minimum(-z, 0.0) - jnp.log(1.0 + jnp.exp(-jnp.abs(z)))
    return lsn, z + lsn


def _sb_specs(n, pairs, nq, bq):
    col = lambda off: pl.BlockSpec((None, n, LANES * pairs), lambda z, p: (z, 0, off // pairs + p))
    stat = pl.BlockSpec((None, 2 * pairs, nq, 1, bq), lambda z, p: (z, p, 0, 0, 0))
    return col, stat


def _sb_pairs(npairs):
    return 2 if npairs % 2 == 0 else 1


def _sb_fwd(src, offs, npairs, *, name, cargo=None):
    zs, n, _ = src.shape
    bq, bk = _att_blocks(n)
    nq, nk, rq = n // bq, n // bk, bq // bk
    pairs = _sb_pairs(npairs)
    heads = range(2 * pairs)
    lanes = [slice(pp * LANES, (pp + 1) * LANES) for pp in range(pairs)]

    def body(q_ref, k_ref, v_ref, o_ref, lt_ref, qs_s, kb_s, vt_s, acc_s, c_s):
        m_a = _mask_a()
        masks = (m_a, jnp.logical_not(m_a))

        def prep(c):
            rows = _rows(c, bk)
            for pp in range(pairs):
                qs_s[pp, rows, :] = (q_ref[rows, lanes[pp]] * SCALE).astype(BF16)
                kb_s[pp, rows, :] = k_ref[rows, lanes[pp]].astype(BF16)
                vt_s[pp, c] = jnp.transpose(v_ref[rows, lanes[pp]]).astype(BF16)

        _loop(0, nk, prep)
        qk = _q_minus_k(bk, bq)
        u_gt = _tri(bk, lambda r, c: c > r)

        def qblock(qi):
            rows = _rows(qi, bq)
            qms = [jnp.where(masks[h % 2], qs_s[h // 2, rows, :], 0) for h in heads]
            acc_s[...] = jnp.zeros_like(acc_s)
            c_s[...] = jnp.zeros_like(c_s)

            def step(kj, masked):
                cols = _rows(kj, bk)
                zts = [lax.dot_general(kb_s[h // 2, cols, :], qms[h], NT, preferred_element_type=F32) for h in heads]
                old = [c_s[h] for h in heads]
                if masked:
                    ok = (qk + (qi * bq - kj * bk)) > 0
                new = []
                for h in heads:
                    lsn, lsp = _log_sig_pair(zts[h])
                    if masked:
                        lsn = jnp.where(ok, lsn, 0.0)
                    at = jnp.exp(lsp + (old[h] + _cumdot_left(u_gt, lsn, SB_LOG_PARTS)))
                    if masked:
                        at = jnp.where(ok, at, 0.0)
                    new.append((jnp.dot(vt_s[h // 2, kj], at.astype(BF16), preferred_element_type=F32),
                                old[h] + jnp.sum(lsn, axis=0, keepdims=True)))
                for h in heads:
                    acc_s[h] += new[h][0]
                    c_s[h] = new[h][1]

            _loop(0, rq, lambda t: step((qi + 1) * rq - 1 - t, True))
            _loop(0, qi * rq, lambda t: step(qi * rq - 1 - t, False))
            for pp in range(pairs):
                o_ref[rows, lanes[pp]] = jnp.where(m_a, jnp.transpose(acc_s[2 * pp]), jnp.transpose(acc_s[2 * pp + 1]))
            for h in heads:
                lt_ref[h, qi] = c_s[h]

        _loop(0, nq, qblock)

    col, stat = _sb_specs(n, pairs, nq, bq)
    scratch = [pltpu.VMEM((pairs, n, LANES), BF16)] * 2 + [pltpu.VMEM((pairs, nk, LANES, bk), BF16)]
    scratch += [pltpu.VMEM((2 * pairs, LANES, bq), F32), pltpu.VMEM((2 * pairs, 1, bq), F32)]
    return _call(
        body, cargo=cargo, name=name, grid=(zs, npairs // pairs),
        in_specs=[col(offs[0]), col(offs[1]), col(offs[2])], out_specs=[col(0), stat],
        out_shape=[jax.ShapeDtypeStruct((zs, n, LANES * npairs), F32), jax.ShapeDtypeStruct((zs, 2 * npairs, nq, 1, bq), F32)],
        scratch_shapes=scratch, compiler_params=_params(("parallel", "parallel")),
    )(src, src, src)


def _sb_bwd(src, offs, npairs, do, ltot, *, name, cargo=None):
    zs, n, _ = src.shape
    bq, bk = _att_blocks(n)
    nq, nk, rq = n // bq, n // bk, bq // bk
    pairs = _sb_pairs(npairs)
    heads = range(2 * pairs)
    lanes = [slice(pp * LANES, (pp + 1) * LANES) for pp in range(pairs)]

    def body(q_ref, k_ref, v_ref, do_ref, lt_ref, dq_ref, dk_ref, dv_ref, qs_s, kb_s, vb_s, kt_s, dk_s, dv_s, dq_s, lp_s, ep_s):
        m_a = _mask_a()
        masks = (m_a, jnp.logical_not(m_a))

        def prep(c):
            rows = _rows(c, bk)
            for pp in range(pairs):
                qs_s[pp, rows, :] = (q_ref[rows, lanes[pp]] * SCALE).astype(BF16)
                kv = k_ref[rows, lanes[pp]]
                kb_s[pp, rows, :] = kv.astype(BF16)
                kt_s[pp, c] = jnp.transpose(kv).astype(BF16)
                vb_s[pp, rows, :] = v_ref[rows, lanes[pp]].astype(BF16)

        _loop(0, nk, prep)
        dk_s[...] = jnp.zeros_like(dk_s)
        dv_s[...] = jnp.zeros_like(dv_s)
        qk = _q_minus_k(bk, bq)
        u_le = _tri(bk, lambda r, c: c <= r)
        u_lt = _tri(bk, lambda r, c: c < r)

        def qblock(qi):
            rows = _rows(qi, bq)
            qms = [jnp.where(masks[h % 2], qs_s[h // 2, rows, :], 0) for h in heads]
            doms = [jnp.where(masks[h % 2], do_ref[rows, lanes[h // 2]], 0.0).astype(BF16) for h in heads]
            lts = [lt_ref[h, qi] for h in heads]
            dq_s[...] = jnp.zeros_like(dq_s)
            lp_s[...] = jnp.zeros_like(lp_s)
            ep_s[...] = jnp.zeros_like(ep_s)

            def step(kj, masked):
                cols = _rows(kj, bk)
                kbs = [kb_s[pp, cols, :] for pp in range(pairs)]
                kts = [kt_s[pp, kj] for pp in range(pairs)]
                zts = [lax.dot_general(kbs[h // 2], qms[h], NT, preferred_element_type=F32) for h in heads]
                dats = [lax.dot_general(vb_s[h // 2, cols, :], doms[h], NT, preferred_element_type=F32) for h in heads]
                old = [(lp_s[h], ep_s[h]) for h in heads]
                if masked:
                    ok = (qk + (qi * bq - kj * bk)) > 0
                new = []
                for h in heads:
                    lp, ep = old[h]
                    lsn, lsp = _log_sig_pair(zts[h])
                    sig = jnp.exp(lsp)
                    if masked:
                        lsn = jnp.where(ok, lsn, 0.0)
                    at = jnp.exp(lsp + (lts[h] - (lp + _cumdot_left(u_le, lsn, SB_LOG_PARTS))))
                    if masked:
                        at = jnp.where(ok, at, 0.0)
                    et = dats[h] * at
                    big_e = ep + _cumdot_left(u_lt, et, SB_GRAD_PARTS)
                    dzt = et - sig * (et + big_e)
                    if masked:
                        dzt = jnp.where(ok, dzt, 0.0)
                    dzb = dzt.astype(BF16)
                    new.append((jnp.dot(dzb, qms[h], preferred_element_type=F32),
                                jnp.dot(at.astype(BF16), doms[h], preferred_element_type=F32),
                                jnp.dot(kts[h // 2], dzb, preferred_element_type=F32),
                                lp + jnp.sum(lsn, axis=0, keepdims=True), ep + jnp.sum(et, axis=0, keepdims=True)))
                for h in heads:
                    dq_s[h] += new[h][2]
                    lp_s[h], ep_s[h] = new[h][3], new[h][4]
                for pp in range(pairs):
                    dk_s[pp, cols, :] += new[2 * pp][0] + new[2 * pp + 1][0]
                    dv_s[pp, cols, :] += new[2 * pp][1] + new[2 * pp + 1][1]

            _loop(0, qi * rq, lambda kj: step(kj, False))
            _loop(qi * rq, (qi + 1) * rq, lambda kj: step(kj, True))
            for pp in range(pairs):
                dq = jnp.where(m_a, jnp.transpose(dq_s[2 * pp]), jnp.transpose(dq_s[2 * pp + 1]))
                dq_ref[rows, lanes[pp]] = (dq * SCALE).astype(BF16)

        _loop(0, nq, qblock)

        def store(c):
            rows = _rows(c, bk)
            for pp in range(pairs):
                dk_ref[rows, lanes[pp]] = dk_s[pp, rows, :].astype(BF16)
                dv_ref[rows, lanes[pp]] = dv_s[pp, rows, :].astype(BF16)

        _loop(0, nk, store)

    col, stat = _sb_specs(n, pairs, nq, bq)
    ospec = col(0)
    sds = jax.ShapeDtypeStruct((zs, n, LANES * npairs), BF16)
    scratch = [pltpu.VMEM((pairs, n, LANES), BF16)] * 3 + [pltpu.VMEM((pairs, nk, LANES, bk), BF16)]
    scratch += [pltpu.VMEM((pairs, n, LANES), F32)] * 2
    scratch += [pltpu.VMEM((2 * pairs, LANES, bq), F32), pltpu.VMEM((2 * pairs, 1, bq), F32), pltpu.VMEM((2 * pairs, 1, bq), F32)]
    return _call(
        body, cargo=cargo, name=name, grid=(zs, npairs // pairs),
        in_specs=[col(offs[0]), col(offs[1]), col(offs[2]), ospec, stat],
        out_specs=[ospec] * 3, out_shape=[sds] * 3, scratch_shapes=scratch,
        compiler_params=_params(("parallel", "parallel")),
    )(src, src, src, do, ltot)


def _fox_gate_fwd(lg, bias, *, name):
    bs, nh, t = lg.shape
    blk = min(LANES, t)

    def body(lg_ref, b_ref, kb_ref):
        u_le = _tri(blk, lambda r, c: r <= c)
        carry = jnp.zeros((nh, 1), F32)
        for j in range(t // blk):
            sl = slice(j * blk, (j + 1) * blk)
            xv = lg_ref[:, sl] + b_ref[...]
            lf = jnp.minimum(xv, 0.0) - jnp.log(1.0 + jnp.exp(-jnp.abs(xv)))
            kb_ref[:, sl] = -(carry + _cumdot(lf, u_le, 3))
            carry = carry + jnp.sum(lf, axis=1, keepdims=True)

    spec = pl.BlockSpec((None, nh, t), lambda i: (i, 0, 0))
    return _pcall(
        body, name=name, grid=(bs,), in_specs=[spec, pl.BlockSpec((nh, 1), lambda i: (0, 0))], out_specs=spec,
        out_shape=jax.ShapeDtypeStruct((bs, nh, t), F32), compiler_params=_params(("parallel",)),
    )(lg, bias)


def _fox_gate_bwd(dkb, lg, bias, *, name):
    bs, nh, t = lg.shape
    blk = min(LANES, t)

    def body(dkb_ref, lg_ref, b_ref, dlg_ref, db_ref):
        @pl.when(pl.program_id(0) == 0)
        def _():
            db_ref[...] = jnp.zeros_like(db_ref)

        u_ge = _tri(blk, lambda r, c: r >= c)
        carry = jnp.zeros((nh, 1), F32)
        tot = jnp.zeros((nh, 1), F32)
        for j in reversed(range(t // blk)):
            sl = slice(j * blk, (j + 1) * blk)
            df = -dkb_ref[:, sl]
            dlf = carry + _cumdot(df, u_ge, 3)
            carry = carry + jnp.sum(df, axis=1, keepdims=True)
            xv = lg_ref[:, sl] + b_ref[...]
            dlg = dlf * jax.nn.sigmoid(-xv)
            dlg_ref[:, sl] = dlg
            tot = tot + jnp.sum(dlg, axis=1, keepdims=True)
        db_ref[...] += jnp.broadcast_to(tot, db_ref.shape)

    spec = pl.BlockSpec((None, nh, t), lambda i: (i, 0, 0))
    return _pcall(
        body, name=name, grid=(bs,), in_specs=[spec, spec, pl.BlockSpec((nh, 1), lambda i: (0, 0))],
        out_specs=[spec, pl.BlockSpec((nh, LANES), lambda i: (0, 0))],
        out_shape=[jax.ShapeDtypeStruct((bs, nh, t), F32), jax.ShapeDtypeStruct((nh, LANES), F32)],
        compiler_params=_params(("arbitrary",)),
    )(dkb, lg, bias)


def _place():
    return lax.axis_index("x"), lax.axis_index("y"), lax.axis_index("c")


def _flip(v, f):
    return 1 - v if f else v


FLIPS = [(fx, fy, fc) for fx in (0, 1) for fy in (0, 1) for fc in (0, 1)][1:]


def _comm_sems(nw):
    return [pltpu.SemaphoreType.DMA((7, nw)), pltpu.SemaphoreType.DMA((7, nw)), pltpu.SemaphoreType.DMA((nw,))]


def _gather_cargo(shards, on_done):
    nw = len(shards)

    def parts(x_refs, out_refs, sems):
        send_sems, recv_sems, local_sems = sems
        x, y, cc = _place()
        me, sibling = (x, y, cc), (x, y, 1 - cc)
        chips = [(1 - x, y), (x, 1 - y), (1 - x, 1 - y)]

        def slot(i, px, py, pc):
            return out_refs[i].at[4 * px + 2 * py + pc]

        def copy(i, k, block, to, src=None):
            return pltpu.make_async_remote_copy(
                src_ref=slot(i, *block) if src is None else src, dst_ref=slot(i, *block),
                send_sem=send_sems.at[k, i], recv_sem=recv_sems.at[k, i], device_id=to, device_id_type=MESH)

        mine = [pltpu.make_async_copy(x_refs[i], slot(i, *me), local_sems.at[i]) for i in range(nw)]
        first = []
        for i in range(nw):
            first.append(copy(i, 0, me, sibling, src=x_refs[i]))
            first += [copy(i, 1 + j, me, (*chip, cc), src=x_refs[i]) for j, chip in enumerate(chips)]
        return me, sibling, chips, cc, copy, mine, first

    def start(x_refs, out_refs, sems):
        *_, mine, first = parts(x_refs, out_refs, sems)
        for cp in mine + first:
            cp.start()

    def finish(x_refs, out_refs, sems):
        me, sibling, chips, cc, copy, mine, first = parts(x_refs, out_refs, sems)
        passed = []
        for i in range(nw):
            for j, chip in enumerate(chips):
                copy(i, 1 + j, (*chip, cc), me).wait_recv()
                passed.append(copy(i, 4 + j, (*chip, cc), sibling))
                passed[-1].start()
        for i in range(nw):
            copy(i, 0, sibling, me).wait_recv()
            for j, chip in enumerate(chips):
                copy(i, 4 + j, (*chip, 1 - cc), me).wait_recv()
        for cp in first + passed:
            cp.wait_send()
        for cp in mine:
            cp.wait()

    out_shape = [jax.ShapeDtypeStruct((N_DEV, *s.shape), s.dtype) for s in shards]
    return _Cargo(shards, out_shape, _comm_sems(nw), start, finish, on_done)


def _scatter_cargo(slots, prev, layer, depth, on_done, row0=0, rows=None):
    nw = len(slots)

    def parts(refs, recv_refs, sems):
        g_refs = refs[:nw]
        send_sems, recv_sems, local_sems = sems
        x, y, cc = _place()
        my = 4 * x + 2 * y + cc

        def dst(i):
            return recv_refs[i].at[my, layer, pl.ds(row0, slots[i].shape[1])]

        mine, copies = [], []
        for i in range(nw):
            mine.append(pltpu.make_async_copy(g_refs[i].at[my], dst(i), local_sems.at[i]))
            for k, (fx, fy, fc) in enumerate(FLIPS):
                px, py, pc = _flip(x, fx), _flip(y, fy), _flip(cc, fc)
                copies.append(pltpu.make_async_remote_copy(
                    src_ref=g_refs[i].at[4 * px + 2 * py + pc], dst_ref=dst(i),
                    send_sem=send_sems.at[k, i], recv_sem=recv_sems.at[k, i], device_id=(px, py, pc), device_id_type=MESH))
        return mine, copies

    def start(refs, recv_refs, sems):
        mine, copies = parts(refs, recv_refs, sems)
        for cp in mine + copies:
            cp.start()

    def finish(refs, recv_refs, sems):
        mine, copies = parts(refs, recv_refs, sems)
        for cp in copies:
            cp.wait_recv()
        for cp in copies:
            cp.wait_send()
        for cp in mine:
            cp.wait()

    ins, aliases = list(slots), {}
    for i, p in enumerate(prev):
        if p is not None:
            aliases[len(ins)] = i
            ins.append(p)
    out_shape = [jax.ShapeDtypeStruct((N_DEV, depth, rows or s.shape[1], s.shape[2]), s.dtype) for s in slots]
    return _Cargo(ins, out_shape, _comm_sems(nw), start, finish, on_done, aliases)


def _exchange(cargo, *, name):
    def body(*refs):
        c_in = len(cargo.ins)
        c_out = len(cargo.out_shape)
        cargo.start(refs[:c_in], refs[c_in:c_in + c_out], refs[c_in + c_out:])
        cargo.finish(refs[:c_in], refs[c_in:c_in + c_out], refs[c_in + c_out:])

    hbm = pl.BlockSpec(memory_space=pl.ANY)
    res = _pcall(
        body, name=name, in_specs=[hbm] * len(cargo.ins), out_specs=[hbm] * len(cargo.out_shape), out_shape=cargo.out_shape,
        scratch_shapes=cargo.sems, input_output_aliases=dict(cargo.aliases),
    )(*cargo.ins)
    cargo.on_done(list(res))


def _allreduce_small(blob, *, name):
    r, c = blob.shape

    def body(x_ref, out_ref, buf, send_sems, recv_sems):
        x, y, cc = _place()
        my = 4 * x + 2 * y + cc
        copies = []
        for k, (fx, fy, fc) in enumerate(FLIPS):
            peer = (_flip(x, fx), _flip(y, fy), _flip(cc, fc))
            copies.append(pltpu.make_async_remote_copy(
                src_ref=x_ref, dst_ref=buf.at[my], send_sem=send_sems.at[k], recv_sem=recv_sems.at[k],
                device_id=peer, device_id_type=MESH))
        for cp in copies:
            cp.start()
        buf[my] = x_ref[...]
        for cp in copies:
            cp.wait_recv()
        for cp in copies:
            cp.wait_send()
        acc = buf[0]
        for i in range(1, N_DEV):
            acc = acc + buf[i]
        out_ref[...] = acc

    vmem = pl.BlockSpec(memory_space=pltpu.VMEM)
    return _pcall(
        body, name=name, in_specs=[vmem], out_specs=vmem, out_shape=jax.ShapeDtypeStruct((r, c), F32),
        scratch_shapes=[pltpu.VMEM((N_DEV, r, c), F32), pltpu.SemaphoreType.DMA((7,)), pltpu.SemaphoreType.DMA((7,))],
    )(blob)


BIG = ("w_in", "w_mlp_in", "w_mlp_out", "w_up_fox", "w_up_sb", "w_up_dil", "w_out")
ROW_SHARDED = ("w_out", "w_mlp_out")
SMALL = ("attn_norm", "b_forget", "q_norm_fox", "k_norm_fox", "q_norm_dil", "k_norm_dil", "mlp_norm")
BLOB_ROWS = 512


def _pack(parts, dtype):
    flat = jnp.concatenate([p.reshape(-1).astype(dtype) for p in parts])
    size = -(-flat.shape[0] // (BLOB_ROWS * LANES)) * (BLOB_ROWS * LANES)
    return jnp.pad(flat, (0, size - flat.shape[0])).reshape(-1, LANES)


def _unpack(blob, shapes):
    flat = blob.reshape(-1)
    out, off = [], 0
    for shp in shapes:
        size = 1
        for s in shp:
            size *= s
        out.append(flat[off:off + size].reshape(shp))
        off += size
    return out


def _join_shards(name, sh):
    if name in ROW_SHARDED:
        return sh.reshape(-1, sh.shape[2])
    return jnp.transpose(sh, (1, 0, 2)).reshape(sh.shape[1], -1)


def _split_shards(name, full):
    a, b = full.shape
    if name in ROW_SHARDED:
        return full.reshape(N_DEV, a // N_DEV, b)
    return jnp.transpose(full.reshape(a, N_DEV, b // N_DEV), (1, 0, 2))


def _in_segments(d_in):
    o1 = 3 * W_FOX
    o2 = o1 + N_HEADS_FOX
    return (0, o1, 0), (o2, d_in, -N_HEADS_FOX), (o1, o2, d_in - o2)


def _join_w_in(sh, dp):
    b = sh.shape[2]
    pieces = []
    for s, e, _ in _in_segments(N_DEV * b):
        for j in range(s // b, (e - 1) // b + 1):
            pieces.append(sh[j, :, max(s, j * b) - j * b:min(e, (j + 1) * b) - j * b])
    pieces.append(jnp.zeros((sh.shape[1], dp - N_DEV * b), sh.dtype))
    return jnp.concatenate(pieces, axis=1)


def _split_w_in(gp, d_in):
    b = d_in // N_DEV
    shards = []
    for j in range(N_DEV):
        runs = []
        for s, e, shift in sorted(_in_segments(d_in)):
            lo, hi = max(s, j * b), min(e, (j + 1) * b)
            if lo < hi:
                runs.append(gp[:, lo + shift:hi + shift])
        shards.append(jnp.concatenate(runs, axis=1))
    return jnp.stack(shards)


def _stat_to_tokens(st, r, b):
    hh = st.shape[1]
    n = st.shape[2] * st.shape[4]
    return jnp.transpose(st.reshape(b, r, hh, n), (0, 2, 3, 1)).reshape(b, hh, n * r)


def _stat_to_streams(tok, r, blk):
    b, hh, t = tok.shape
    n = t // r
    return jnp.transpose(tok.reshape(b, hh, n, r), (0, 3, 1, 2)).reshape(b * r, hh, n // blk, 1, blk)


def _rope_tables(positions):
    half = HEAD_DIM // 2
    inv = 1.0 / (ROPE_THETA ** (jnp.arange(half, dtype=F32) / half))
    ang = positions.astype(F32)[..., None] * inv
    cos, sin = jnp.cos(ang), jnp.sin(ang)
    return jnp.tile(cos, (1, 1, 4)), jnp.tile(jnp.concatenate([-sin, sin], axis=-1), (1, 1, 2))


def _gain2(g):
    return jnp.tile(g.reshape(1, HEAD_DIM), (1, 2))


def _dil_offs(g):
    c0 = (P_DIL + g * W_DIL) // LANES
    return c0, c0 + W_DILQ // LANES, c0 + 2 * W_DILQ // LANES


def _dil_pairs(r):
    return N_HEADS_DIL // 2


def _layer_fwd(l, x, h, w, small, ropes, bl, t, cargo, next_gain):
    n, d = x.shape
    s = {}
    s["x"] = x
    s["h"] = _rmsnorm_fwd(x, small["attn_norm"][l].reshape(1, d), name=f"norm_attn_fwd{l}") if h is None else h
    proj = _mm(s["h"], w["w_in"][l], tn=PROJ_TILE, name=f"mm_proj{l}")
    s["proj"] = proj
    dp = proj.shape[1]
    proj3 = proj.reshape(bl, t, dp)
    p_fg = P_GATE + 3 * d

    lg = jnp.transpose(proj3[:, :, p_fg:p_fg + N_HEADS_FOX], (0, 2, 1))
    s["lg"] = lg
    kb = _fox_gate_fwd(lg, small["b_forget"][l].reshape(N_HEADS_FOX, 1), name=f"fox_gate_fwd{l}")
    blk = _att_blocks(t, wide_keys=True)[1]
    kb5 = kb.reshape(bl, N_HEADS_FOX, t // blk, 1, blk)
    s["kb5"] = kb5
    gqf, gkf = _gain2(small["q_norm_fox"][l]), _gain2(small["k_norm_fox"][l])
    fo = P_FOX // LANES
    fox_offs = (fo, fo + W_FOX // LANES, fo + 2 * W_FOX // LANES)
    out_a, lse_a = _attn_fwd(proj3, fox_offs, N_HEADS_FOX // 2, gqf, gkf, kbias=kb5, window=t, name=f"fox_fwd{l}",
                             cargo=cargo.get("fox_fwd"))
    s["out_a"], s["lse_a"] = out_a, lse_a

    so = P_SB // LANES
    sb_offs = (so, so + W_SB // LANES, so + 2 * W_SB // LANES)
    out_b, lt_b = _sb_fwd(proj3, sb_offs, N_HEADS_SB // 2, name=f"sb_fwd{l}", cargo=cargo.get("sb_fwd"))
    s["out_b"], s["lt_b"] = out_b, lt_b

    gqd, gkd = _gain2(small["q_norm_dil"][l]), _gain2(small["k_norm_dil"][l])
    os_, lses = [], []
    for g, (window, r) in enumerate(DIL_PATTERNS):
        o_g, lse_g = _attn_fwd(proj3, _dil_offs(g), N_HEADS_DIL // 2, gqd, gkd, rope=ropes, window=window // r, stride=r,
                               pairs=_dil_pairs(r), name=f"dil_fwd{l}_{g}")
        os_.append(o_g.reshape(n, W_DIL))
        lses.append(_stat_to_tokens(lse_g, r, bl).reshape(bl * N_HEADS_DIL, t))
    lse_c, *ws = _dil_weights(lses, name=f"dil_weights{l}")
    ws = [jnp.repeat(jnp.transpose(wg.reshape(bl, N_HEADS_DIL, t), (0, 2, 1)).reshape(n, N_HEADS_DIL), HEAD_DIM, axis=1) for wg in ws]
    out_c = _dil_mix(os_, ws, name=f"dil_mix{l}")
    s["out_c"], s["lse_c"] = out_c, lse_c.reshape(bl, N_HEADS_DIL, t)

    ys = [_mm(out_a.reshape(n, W_FOX), w["w_up_fox"][l], out_dtype=BF16, name=f"mm_up_fox{l}"),
          _mm(out_b.reshape(n, W_SB), w["w_up_sb"][l], out_dtype=BF16, name=f"mm_up_sb{l}"),
          _mm(out_c, w["w_up_dil"][l], out_dtype=BF16, name=f"mm_up_dil{l}")]
    s["ys"] = ys
    s["merged"] = _gate_merge_fwd(proj, ys, name=f"gate_merge_fwd{l}")
    x1, s["h2"] = _mm(s["merged"], w["w_out"][l], add=x, norm_gain=small["mlp_norm"][l].reshape(1, d), name=f"mm_out{l}")
    s["x1"] = x1

    s["u"], s["a"] = _mm(s["h2"], w["w_mlp_in"][l], relu2=True, out_dtype=BF16, name=f"mm_mlp_in{l}")
    if next_gain is None:
        return _mm(s["a"], w["w_mlp_out"][l], add=x1, name=f"mm_mlp_out{l}"), None, s
    x2, h_next = _mm(s["a"], w["w_mlp_out"][l], add=x1, norm_gain=next_gain, name=f"mm_mlp_out{l}")
    return x2, h_next, s


def _layer_bwd(l, dx2, s, w, small, ropes, bl, t, hooks):
    n, d = dx2.shape
    gw, gs = {}, {}

    def cargo(call):
        return hooks[call](gw) if call in hooks else None
    du = _mm(dx2, w["w_mlp_out"][l], tb=True, relu_grad_of=s["u"], out_dtype=BF16, name=f"mm_du{l}")
    gw["w_mlp_out"] = _mm(s["a"], dx2, ta=True, name=f"mm_dw_mlp_out{l}")
    gw["w_mlp_in"] = _mm(s["h2"], du, ta=True, name=f"mm_dw_mlp_in{l}")
    dh2 = _mm(du, w["w_mlp_in"][l], tb=True, name=f"mm_dh2{l}")
    dx1, gs["mlp_norm"] = _rmsnorm_bwd(s["x1"], small["mlp_norm"][l].reshape(1, d), dh2, dx2, name=f"norm_mlp_bwd{l}")

    dmerged = _mm(dx1, w["w_out"][l], tb=True, name=f"mm_dmerged{l}")
    gw["w_out"] = _mm(s["merged"], dx1, ta=True, name=f"mm_dw_out{l}")
    dya, dyb, dyc, dgl0, dgl1, dgl2 = _gate_merge_bwd(s["proj"], s["ys"], dmerged, name=f"gate_merge_bwd{l}")
    out_a2, out_b2 = s["out_a"].reshape(n, W_FOX), s["out_b"].reshape(n, W_SB)
    gw["w_up_fox"] = _mm(out_a2, dya, ta=True, name=f"mm_dw_up_fox{l}")
    gw["w_up_sb"] = _mm(out_b2, dyb, ta=True, name=f"mm_dw_up_sb{l}")
    gw["w_up_dil"] = _mm(s["out_c"], dyc, ta=True, name=f"mm_dw_up_dil{l}")
    dout_a = _mm(dya, w["w_up_fox"][l], tb=True, name=f"mm_dout_a{l}").reshape(bl, t, W_FOX)
    dout_b = _mm(dyb, w["w_up_sb"][l], tb=True, name=f"mm_dout_b{l}").reshape(bl, t, W_SB)
    dout_c = _mm(dyc, w["w_up_dil"][l], tb=True, name=f"mm_dout_c{l}").reshape(bl, t, W_DIL)

    proj3 = s["proj"].reshape(bl, t, -1)
    gqf, gkf = _gain2(small["q_norm_fox"][l]), _gain2(small["k_norm_fox"][l])
    fo = P_FOX // LANES
    fox_offs = (fo, fo + W_FOX // LANES, fo + 2 * W_FOX // LANES)
    (dq_a,), (dk_a,), (dv_a,), dg_a, dkb5 = _attn_bwd(proj3, fox_offs, N_HEADS_FOX // 2, gqf, gkf, s["out_a"], dout_a, s["lse_a"],
                                             kbias=s["kb5"], window=t, name=f"fox_bwd{l}", cargo=cargo("fox_bwd"))
    gs["fox_gains"] = dg_a
    dlg, gs["b_forget"] = _fox_gate_bwd(dkb5.reshape(bl, N_HEADS_FOX, t), s["lg"], small["b_forget"][l].reshape(N_HEADS_FOX, 1),
                                        name=f"fox_gate_bwd{l}")
    so = P_SB // LANES
    sb_offs = (so, so + W_SB // LANES, so + 2 * W_SB // LANES)
    dq_b, dk_b, dv_b = _sb_bwd(proj3, sb_offs, N_HEADS_SB // 2, dout_b, s["lt_b"], name=f"sb_bwd{l}", cargo=cargo("sb_bwd"))
    gqd, gkd = _gain2(small["q_norm_dil"][l]), _gain2(small["k_norm_dil"][l])
    out_c3 = s["out_c"].reshape(bl, t, W_DIL)
    dqs, dks, dvs, dgd = [], [], [], None
    for g, (window, r) in enumerate(DIL_PATTERNS):
        lse_g = _stat_to_streams(s["lse_c"], r, _att_blocks(t // r)[0])
        dq_g, dk_g, dv_g, dg_g = _attn_bwd(proj3, _dil_offs(g), N_HEADS_DIL // 2, gqd, gkd, out_c3, dout_c, lse_g, rope=ropes,
                                           window=window // r, stride=r, pairs=_dil_pairs(r), name=f"dil_bwd{l}_{g}")
        dqs += dq_g
        dks += dk_g
        dvs += dv_g
        dgd = dg_g if dgd is None else jnp.concatenate([dgd, dg_g], axis=0)
    gs["dil_gains"] = dgd

    dlg_cols = jnp.pad(jnp.transpose(dlg, (0, 2, 1)).reshape(n, N_HEADS_FOX), ((0, 0), (0, LANES - N_HEADS_FOX)))
    parts = [p.reshape(n, -1) for p in [dq_a, dk_a, dv_a, dq_b, dk_b, dv_b] + dqs + dks + dvs] + [dgl0, dgl1, dgl2, dlg_cols]
    dproj = _assemble_cols(parts, s["proj"].shape[1], name=f"assemble_dproj{l}")
    if "mm_dw_in_hi" in hooks:
        half = d // 2
        gw["w_in_lo"] = _mm(s["h"][:, :half], dproj, ta=True, tn=PROJ_TILE, name=f"mm_dw_in_lo{l}")
        gw["w_in_hi"] = _mm(s["h"][:, half:], dproj, ta=True, tn=PROJ_TILE, name=f"mm_dw_in_hi{l}", cargo=cargo("mm_dw_in_hi"))
    else:
        gw["w_in"] = _mm(s["h"], dproj, ta=True, tn=PROJ_TILE, name=f"mm_dw_in{l}")
    dh = _mm(dproj, w["w_in"][l], tb=True, tn=1024, tk=PROJ_TILE, name=f"mm_dh{l}", cargo=cargo("mm_dh"))
    dx, gs["attn_norm"] = _rmsnorm_bwd(s["x"], small["attn_norm"][l].reshape(1, d), dh, dx1, name=f"norm_attn_bwd{l}")
    return dx, gw, gs


def kernel(x, positions, attn_norm, w_in, b_forget, q_norm_fox, k_norm_fox, q_norm_dil, k_norm_dil, w_up_fox, w_up_sb, w_up_dil, w_out, mlp_norm, w_mlp_in, w_mlp_out, loss_target, m_attn_norm, m_w_in, m_b_forget, m_q_norm_fox, m_k_norm_fox, m_q_norm_dil, m_k_norm_dil, m_w_up_fox, m_w_up_sb, m_w_up_dil, m_w_out, m_mlp_norm, m_w_mlp_in, m_w_mlp_out, v_attn_norm, v_w_in, v_b_forget, v_q_norm_fox, v_k_norm_fox, v_q_norm_dil, v_k_norm_dil, v_w_up_fox, v_w_up_sb, v_w_up_dil, v_w_out, v_mlp_norm, v_w_mlp_in, v_w_mlp_out):
    bl, t, d = x.shape
    n = bl * t
    depth = attn_norm.shape[0]
    wl = dict(w_in=w_in, w_up_fox=w_up_fox, w_up_sb=w_up_sb, w_up_dil=w_up_dil, w_out=w_out, w_mlp_in=w_mlp_in, w_mlp_out=w_mlp_out)
    ml = dict(w_in=m_w_in, w_up_fox=m_w_up_fox, w_up_sb=m_w_up_sb, w_up_dil=m_w_up_dil, w_out=m_w_out, w_mlp_in=m_w_mlp_in, w_mlp_out=m_w_mlp_out)
    vl = dict(w_in=v_w_in, w_up_fox=v_w_up_fox, w_up_sb=v_w_up_sb, w_up_dil=v_w_up_dil, w_out=v_w_out, w_mlp_in=v_w_mlp_in, w_mlp_out=v_w_mlp_out)
    small = dict(attn_norm=attn_norm, b_forget=b_forget, q_norm_fox=q_norm_fox, k_norm_fox=k_norm_fox, q_norm_dil=q_norm_dil,
                 k_norm_dil=k_norm_dil, mlp_norm=mlp_norm)
    m_small = dict(attn_norm=m_attn_norm, b_forget=m_b_forget, q_norm_fox=m_q_norm_fox, k_norm_fox=m_k_norm_fox,
                   q_norm_dil=m_q_norm_dil, k_norm_dil=m_k_norm_dil, mlp_norm=m_mlp_norm)
    v_small = dict(attn_norm=v_attn_norm, b_forget=v_b_forget, q_norm_fox=v_q_norm_fox, k_norm_fox=v_k_norm_fox,
                   q_norm_dil=v_q_norm_dil, k_norm_dil=v_k_norm_dil, mlp_norm=v_mlp_norm)

    d_in = w_in.shape[-1] * N_DEV
    dp = -(-d_in // 512) * 512
    rest = [k for k in BIG if k != "w_in"]
    w = {k: [None] * depth for k in BIG}

    def gather(items):
        def done(res):
            for (k, l), sh in zip(items, res):
                w[k][l] = _join_w_in(sh, dp) if k == "w_in" else _join_shards(k, sh)

        return _gather_cargo([wl[k][l].astype(BF16) for k, l in items], done)

    _exchange(gather([("w_in", 0)]), name="gather_first")
    cos, sin = _rope_tables(positions)
    ropes = (cos, sin)

    xl, hl = x.reshape(n, d), None
    saved = []
    for l in range(depth):
        cargo = {"fox_fwd": gather([(k, l) for k in rest])}
        if l + 1 < depth:
            cargo["sb_fwd"] = gather([("w_in", l + 1)])
        next_gain = attn_norm[l + 1].reshape(1, d) if l + 1 < depth else None
        xl, hl, s = _layer_fwd(l, xl, hl, w, small, ropes, bl, t, cargo, next_gain)
        saved.append(s)
    dy, loss_part = _loss_head(xl, loss_target.reshape(n, d), name="loss_head")

    recv = {}

    def scatter(names, l, grads):
        def done(res):
            recv.update(zip(names, res))

        slots = [(_split_w_in(grads[k], d_in) if k == "w_in" else _split_shards(k, grads[k])).astype(BF16) for k in names]
        return _scatter_cargo(slots, [recv.get(k) for k in names], l, depth, done)

    def scatter_w_in_rows(part, l, row0):
        def done(res):
            recv["w_in"] = res[0]

        return _scatter_cargo([_split_w_in(part, d_in).astype(BF16)], [recv.get("w_in")], l, depth, done, row0=row0, rows=d)

    gss = [None] * depth
    above = None
    for l in reversed(range(depth)):
        hooks = {"fox_bwd": lambda gw, l=l: scatter(rest, l, gw)}
        if above is not None:
            hooks["sb_bwd"] = lambda gw, l=l, g=above: scatter(["w_in"], l + 1, g)
        if l == 0:
            hooks["mm_dw_in_hi"] = lambda gw: scatter_w_in_rows(gw["w_in_lo"], 0, 0)
            hooks["mm_dh"] = lambda gw: scatter_w_in_rows(gw["w_in_hi"], 0, d // 2)
        dy, above, gss[l] = _layer_bwd(l, dy, saved[l], w, small, ropes, bl, t, hooks)
    grad_x = dy.reshape(bl, t, d)

    g_big, d_big, m_big, v_big = {}, {}, {}, {}
    for k in BIG:
        g_big[k], d_big[k], m_big[k], v_big[k] = _adamw(recv[k], wl[k], ml[k], vl[k], name=f"adamw_{k}")

    rows = [loss_part]
    for l in range(depth):
        gs = gss[l]
        rows += [gs["attn_norm"].reshape(-1, LANES), gs["mlp_norm"].reshape(-1, LANES), gs["fox_gains"], gs["dil_gains"], gs["b_forget"]]
    row_counts = [r.shape[0] for r in rows]
    part = jnp.concatenate(rows, axis=0)
    pad_rows = -(-part.shape[0] // 8) * 8 - part.shape[0]
    summed = _allreduce_small(jnp.pad(part, ((0, pad_rows), (0, 0))), name="allreduce_small")
    pieces, off = [], 0
    for c in row_counts:
        pieces.append(summed[off:off + c])
        off += c
    loss = pieces[0][0, 0]

    def fold(row):
        return row[:HEAD_DIM] + row[HEAD_DIM:]

    g_small = {k: [] for k in SMALL}
    for l in range(depth):
        an, mn, fg, dg, bf = pieces[1 + 5 * l:6 + 5 * l]
        g_small["attn_norm"].append(an.reshape(d))
        g_small["mlp_norm"].append(mn.reshape(d))
        g_small["q_norm_fox"].append(fold(fg[0]))
        g_small["k_norm_fox"].append(fold(fg[1]))
        g_small["q_norm_dil"].append(fold(dg[0]) + fold(dg[8]) + fold(dg[16]))
        g_small["k_norm_dil"].append(fold(dg[1]) + fold(dg[9]) + fold(dg[17]))
        g_small["b_forget"].append(bf[:, 0])
    g_small = {k: jnp.stack(vs) for k, vs in g_small.items()}
    small_shapes = [small[k].shape for k in SMALL]
    outs = _adamw(_pack([g_small[k] for k in SMALL], F32)[None, None], _pack([small[k] for k in SMALL], F32)[None],
                  _pack([m_small[k] for k in SMALL], F32)[None], _pack([v_small[k] for k in SMALL], F32)[None], name="adamw_small")
    g_sm, d_sm, m_sm, v_sm = (dict(zip(SMALL, _unpack(o, small_shapes))) for o in outs)

    order = ("attn_norm", "w_in", "b_forget", "q_norm_fox", "k_norm_fox", "q_norm_dil", "k_norm_dil", "w_up_fox", "w_up_sb",
             "w_up_dil", "w_out", "mlp_norm", "w_mlp_in", "w_mlp_out")
    res = [loss, grad_x]
    for big, sm in ((g_big, g_sm), (d_big, d_sm), (m_big, m_sm), (v_big, v_sm)):
        res += [big[k] if k in big else sm[k] for k in order]
    return tuple(res)
```

```python
import jax
import jax.numpy as jnp
from jax import lax
from jax.experimental import pallas as pl
from jax.experimental.pallas import tpu as pltpu

F32 = jnp.float32
BF16 = jnp.bfloat16

HEAD_DIM = 64
LANES = 128
N_HEADS_FOX = 8
N_HEADS_SB = 8
N_HEADS_DIL = 4
DIL_PATTERNS = ((128, 1), (512, 4), (2048, 16))
ROPE_THETA = 10000.0
EPS = 1e-6
SCALE = 0.125
W_FOX = N_HEADS_FOX * HEAD_DIM
W_SB = N_HEADS_SB * HEAD_DIM
W_DIL = N_HEADS_DIL * HEAD_DIM
W_DILQ = len(DIL_PATTERNS) * W_DIL
P_FOX = 0
P_SB = 3 * W_FOX
P_DIL = P_SB + 3 * W_SB
P_GATE = P_DIL + 3 * W_DILQ
N_DEV = 8
ATT_BLK = 256
ATT_BQ = 512
NEG = -1e30
VMEM_LIMIT = 56 * 1024 * 1024
ADAMW_BLOCK_ELEMS = 128 * 1024
PROJ_TILE = 2176

ADAM_LR = 0.001
ADAM_B1 = 0.9
ADAM_B2 = 0.999
ADAM_EPS = 1e-08
ADAM_WD = 0.01
ADAM_STEP = 10

NT = (((1,), (1,)), ((), ()))
MESH = pl.DeviceIdType.MESH


def _pcall(body, **kw):
    return pl.pallas_call(body, **kw)


def _params(sem=None):
    return pltpu.CompilerParams(dimension_semantics=sem, vmem_limit_bytes=VMEM_LIMIT)


class _Cargo:
    def __init__(self, ins, out_shape, sems, start, finish, on_done, aliases=None):
        self.ins, self.out_shape, self.sems = list(ins), list(out_shape), list(sems)
        self.start, self.finish, self.on_done, self.aliases = start, finish, on_done, dict(aliases or {})


def _call(body, *, cargo=None, name, grid=(), in_specs, out_specs, out_shape, scratch_shapes=(), compiler_params=None):
    if cargo is None:
        kw = dict(grid=grid) if grid else {}
        if compiler_params is not None:
            kw["compiler_params"] = compiler_params
        return _pcall(body, name=name, in_specs=in_specs, out_specs=out_specs, out_shape=out_shape,
                      scratch_shapes=list(scratch_shapes), **kw)
    single = not isinstance(out_shape, (list, tuple))
    o_specs, o_shape = ([out_specs], [out_shape]) if single else (list(out_specs), list(out_shape))
    n_in, n_out, n_scr = len(in_specs), len(o_shape), len(scratch_shapes)
    c_in, c_out = len(cargo.ins), len(cargo.out_shape)

    def wrapped(*refs):
        ins, cins = refs[:n_in], refs[n_in:n_in + c_in]
        o0 = n_in + c_in
        outs, couts = refs[o0:o0 + n_out], refs[o0 + n_out:o0 + n_out + c_out]
        s0 = o0 + n_out + c_out
        scr, sems = refs[s0:s0 + n_scr], refs[s0 + n_scr:]
        first = last = None
        for ax, size in enumerate(grid):
            pid = pl.program_id(ax)
            first = (pid == 0) if first is None else first & (pid == 0)
            last = (pid == size - 1) if last is None else last & (pid == size - 1)
        if first is None:
            cargo.start(cins, couts, sems)
            body(*ins, *outs, *scr)
            cargo.finish(cins, couts, sems)
            return

        @pl.when(first)
        def _():
            cargo.start(cins, couts, sems)

        body(*ins, *outs, *scr)

        @pl.when(last)
        def _():
            cargo.finish(cins, couts, sems)

    hbm = pl.BlockSpec(memory_space=pl.ANY)
    kw = dict(grid=grid, compiler_params=_params(("arbitrary",) * len(grid))) if grid else {}
    call = _pcall(
        wrapped, name=name, in_specs=list(in_specs) + [hbm] * c_in, out_specs=o_specs + [hbm] * c_out,
        out_shape=o_shape + cargo.out_shape, scratch_shapes=list(scratch_shapes) + cargo.sems,
        input_output_aliases={n_in + i: n_out + j for i, j in cargo.aliases.items()}, **kw)

    def run(*args):
        res = call(*args, *cargo.ins)
        cargo.on_done(list(res[n_out:]))
        return res[0] if single else list(res[:n_out])

    return run


def _tile(dim, target, mult=LANES):
    t = (min(dim, target) // mult) * mult
    while t >= mult:
        if dim % t == 0:
            return t
        t -= mult
    return dim


def _mm(a, b, *, ta=False, tb=False, add=None, relu2=False, relu_grad_of=None, norm_gain=None, out_dtype=F32, name,
        tm=1024, tn=1024, tk=1024, cargo=None):
    m, k = (a.shape[1], a.shape[0]) if ta else a.shape
    n = b.shape[0] if tb else b.shape[1]
    tm, tn, tk = _tile(m, tm), _tile(n, tn), _tile(k, tk)
    nk = k // tk
    dn = (((0,) if ta else (1,), (1,) if tb else (0,)), ((), ()))

    extra = add if add is not None else relu_grad_of
    normed = norm_gain is not None
    assert not (normed and relu2) and (not normed or tn == n)

    def body(*refs):
        a_ref, b_ref = refs[:2]
        x_ref = refs[2] if extra is not None else None
        g_ref = refs[2 + (extra is not None)] if normed else None
        outs = refs[2 + (extra is not None) + normed:-1]
        acc = refs[-1]
        kk = pl.program_id(2)
        part = lax.dot_general(a_ref[...].astype(BF16), b_ref[...].astype(BF16), dn, preferred_element_type=F32)

        def finish(r):
            if add is not None:
                r = r + x_ref[...]
            if relu_grad_of is not None:
                r = r * (2.0 * jnp.maximum(x_ref[...].astype(F32), 0.0))
            outs[0][...] = r.astype(out_dtype)
            if relu2:
                rr = jnp.maximum(r, 0.0)
                outs[1][...] = (rr * rr).astype(BF16)
            if normed:
                inv = lax.rsqrt(jnp.mean(r * r, axis=1, keepdims=True) + EPS)
                outs[1][...] = (r * inv * g_ref[...]).astype(BF16)

        if nk == 1:
            finish(part)
            return

        @pl.when(kk == 0)
        def _():
            acc[...] = part

        @pl.when((kk > 0) & (kk < nk - 1))
        def _():
            acc[...] += part

        @pl.when(kk == nk - 1)
        def _():
            finish(acc[...] + part)

    a_spec = pl.BlockSpec((tk, tm), lambda i, j, q: (q, i)) if ta else pl.BlockSpec((tm, tk), lambda i, j, q: (i, q))
    b_spec = pl.BlockSpec((tn, tk), lambda i, j, q: (j, q)) if tb else pl.BlockSpec((tk, tn), lambda i, j, q: (q, j))
    o_spec = pl.BlockSpec((tm, tn), lambda i, j, q: (i, j))
    ins, specs = [a, b], [a_spec, b_spec]
    if extra is not None:
        ins.append(extra)
        specs.append(o_spec)
    if normed:
        ins.append(norm_gain)
        specs.append(pl.BlockSpec((1, tn), lambda i, j, q: (0, j)))
    two = relu2 or normed
    sds = jax.ShapeDtypeStruct((m, n), out_dtype)
    return _call(
        body, cargo=cargo, name=name, grid=(m // tm, n // tn, nk), in_specs=specs,
        out_specs=[o_spec, o_spec] if two else o_spec,
        out_shape=[sds, jax.ShapeDtypeStruct((m, n), BF16)] if two else sds,
        scratch_shapes=[pltpu.VMEM((tm, tn) if nk > 1 else (8, LANES), F32)],
        compiler_params=_params(("parallel", "parallel", "arbitrary")),
    )(*ins)


def _rmsnorm_fwd(x, g, *, name):
    n, d = x.shape
    tm = _tile(n, 256, 8)

    def body(x_ref, g_ref, h_ref):
        xv = x_ref[...]
        inv = lax.rsqrt(jnp.mean(xv * xv, axis=1, keepdims=True) + EPS)
        h_ref[...] = (xv * inv * g_ref[...]).astype(BF16)

    row = pl.BlockSpec((tm, d), lambda i: (i, 0))
    return _pcall(
        body, name=name, grid=(n // tm,), in_specs=[row, pl.BlockSpec((1, d), lambda i: (0, 0))], out_specs=row,
        out_shape=jax.ShapeDtypeStruct((n, d), BF16), compiler_params=_params(("parallel",)),
    )(x, g)


def _rmsnorm_bwd(x, g, dh, dres, *, name):
    n, d = x.shape
    tm = _tile(n, 256, 8)

    def body(x_ref, g_ref, dh_ref, dres_ref, dx_ref, dg_ref):
        @pl.when(pl.program_id(0) == 0)
        def _():
            dg_ref[...] = jnp.zeros_like(dg_ref)

        xv = x_ref[...]
        inv = lax.rsqrt(jnp.mean(xv * xv, axis=1, keepdims=True) + EPS)
        y = xv * inv
        dhv = dh_ref[...]
        dg_ref[...] += jnp.sum(dhv * y, axis=0, keepdims=True)
        dy = dhv * g_ref[...]
        dx_ref[...] = dres_ref[...] + inv * (dy - y * jnp.mean(dy * y, axis=1, keepdims=True))

    row = pl.BlockSpec((tm, d), lambda i: (i, 0))
    vec = pl.BlockSpec((1, d), lambda i: (0, 0))
    return _pcall(
        body, name=name, grid=(n // tm,), in_specs=[row, vec, row, row], out_specs=[row, vec],
        out_shape=[jax.ShapeDtypeStruct((n, d), F32), jax.ShapeDtypeStruct((1, d), F32)],
        compiler_params=_params(("arbitrary",)),
    )(x, g, dh, dres)


def _gate_specs(n, d):
    bw = 256 if d % 256 == 0 else LANES
    tm = _tile(n, 512, 8)
    nb = d // bw
    yspec = pl.BlockSpec((tm, bw), lambda i, j: (i, j))
    gspecs = [pl.BlockSpec((tm, bw), lambda i, j, b=b: (i, P_GATE // bw + b * nb + j)) for b in range(3)]
    return tm, bw, nb, yspec, gspecs


def _gate_merge_fwd(proj, ys, *, name):
    n, d = ys[0].shape
    tm, bw, nb, yspec, gspecs = _gate_specs(n, d)

    def body(g0, g1, g2, y0, y1, y2, o_ref):
        acc = jax.nn.sigmoid(g0[...]) * y0[...]
        acc += jax.nn.sigmoid(g1[...]) * y1[...]
        acc += jax.nn.sigmoid(g2[...]) * y2[...]
        o_ref[...] = acc.astype(BF16)

    return _pcall(
        body, name=name, grid=(n // tm, nb), in_specs=gspecs + [yspec] * 3, out_specs=yspec,
        out_shape=jax.ShapeDtypeStruct((n, d), BF16), compiler_params=_params(("parallel", "parallel")),
    )(proj, proj, proj, *ys)


def _gate_merge_bwd(proj, ys, dmerged, *, name):
    n, d = ys[0].shape
    tm, bw, nb, yspec, gspecs = _gate_specs(n, d)

    def body(g0, g1, g2, y0, y1, y2, dm_ref, dy0, dy1, dy2, dgl0, dgl1, dgl2):
        dm = dm_ref[...]
        for g_ref, y_ref, dy_ref, dgl_ref in ((g0, y0, dy0, dgl0), (g1, y1, dy1, dgl1), (g2, y2, dy2, dgl2)):
            s = jax.nn.sigmoid(g_ref[...])
            dy_ref[...] = (dm * s).astype(BF16)
            dgl_ref[...] = (dm * y_ref[...] * s * (1.0 - s)).astype(BF16)

    sds = jax.ShapeDtypeStruct((n, d), BF16)
    return _pcall(
        body, name=name, grid=(n // tm, nb), in_specs=gspecs + [yspec] * 4, out_specs=[yspec] * 6,
        out_shape=[sds] * 6, compiler_params=_params(("parallel", "parallel")),
    )(proj, proj, proj, *ys, dmerged)


def _loss_head(y, tgt, *, name):
    n, d = y.shape
    tm = _tile(n, 256, 8)
    steps = n // tm

    def body(y_ref, t_ref, dy_ref, loss_ref, acc):
        i = pl.program_id(0)

        @pl.when(i == 0)
        def _():
            acc[...] = jnp.zeros_like(acc)

        e = y_ref[...] - t_ref[...]
        dy_ref[...] = e * (1.0 / d)
        acc[...] += jnp.sum(e * e, axis=0, keepdims=True)

        @pl.when(i == steps - 1)
        def _():
            tot = jnp.sum(acc[...], axis=1, keepdims=True) * (0.5 / d)
            loss_ref[...] = jnp.broadcast_to(tot, loss_ref.shape)

    row = pl.BlockSpec((tm, d), lambda i: (i, 0))
    return _pcall(
        body, name=name, grid=(steps,), in_specs=[row, row], out_specs=[row, pl.BlockSpec((8, LANES), lambda i: (0, 0))],
        out_shape=[jax.ShapeDtypeStruct((n, d), F32), jax.ShapeDtypeStruct((8, LANES), F32)],
        scratch_shapes=[pltpu.VMEM((1, d), F32)], compiler_params=_params(("arbitrary",)),
    )(y, tgt)


def _assemble_cols(parts, width, *, name):
    n = parts[0].shape[0]
    tm = _tile(n, 256, 16)
    widths = [p.shape[1] for p in parts]

    def body(*refs):
        o_ref = refs[-1]
        off = 0
        for ref, w in zip(refs[:-1], widths):
            o_ref[:, off:off + w] = ref[...].astype(BF16)
            off += w
        if off < width:
            o_ref[:, off:] = jnp.zeros((tm, width - off), BF16)

    return _pcall(
        body, name=name, grid=(n // tm,), in_specs=[pl.BlockSpec((tm, w), lambda i: (i, 0)) for w in widths],
        out_specs=pl.BlockSpec((tm, width), lambda i: (i, 0)), out_shape=jax.ShapeDtypeStruct((n, width), BF16),
        compiler_params=_params(("parallel",)),
    )(*parts)


def _dil_weights(lses, *, name):
    shp = lses[0].shape

    def body(l0, l1, l2, lse_ref, w0, w1, w2):
        a, b, c = l0[...], l1[...], l2[...]
        m = jnp.maximum(jnp.maximum(a, b), c)
        ea, eb, ec = jnp.exp(a - m), jnp.exp(b - m), jnp.exp(c - m)
        den = ea + eb + ec
        lse_ref[...] = m + jnp.log(den)
        w0[...] = ea / den
        w1[...] = eb / den
        w2[...] = ec / den

    vmem = pl.BlockSpec(memory_space=pltpu.VMEM)
    return _pcall(body, name=name, in_specs=[vmem] * 3, out_specs=[vmem] * 4, out_shape=[jax.ShapeDtypeStruct(shp, F32)] * 4)(*lses)


def _dil_mix(os_, ws, *, name):
    n, w = os_[0].shape
    tm = _tile(n, 512, 8)

    def body(o0, o1, o2, w0, w1, w2, out_ref):
        out_ref[...] = w0[...] * o0[...] + w1[...] * o1[...] + w2[...] * o2[...]

    spec = pl.BlockSpec((tm, w), lambda i: (i, 0))
    return _pcall(
        body, name=name, grid=(n // tm,), in_specs=[spec] * 6, out_specs=spec, out_shape=jax.ShapeDtypeStruct((n, w), F32),
        compiler_params=_params(("parallel",)),
    )(*os_, *ws)


def _adamw(gsrc, w, m, v, *, name):
    s, dep, a, b = gsrc.shape
    ta = _tile(a, max(16, (ADAMW_BLOCK_ELEMS // b) // 16 * 16), 16)
    c1 = 1.0 / (1.0 - ADAM_B1 ** ADAM_STEP)
    c2 = 1.0 / (1.0 - ADAM_B2 ** ADAM_STEP)

    def body(gs_ref, w_ref, m_ref, v_ref, g_ref, d_ref, m2_ref, v2_ref):
        g = gs_ref[0].astype(F32)
        for i in range(1, s):
            g = g + gs_ref[i].astype(F32)
        m2 = ADAM_B1 * m_ref[...] + (1.0 - ADAM_B1) * g
        v2 = ADAM_B2 * v_ref[...] + (1.0 - ADAM_B2) * (g * g)
        g_ref[...] = g
        m2_ref[...] = m2
        v2_ref[...] = v2
        d_ref[...] = -ADAM_LR * ((m2 * c1) / (jnp.sqrt(v2 * c2) + ADAM_EPS) + ADAM_WD * w_ref[...])

    spec = pl.BlockSpec((None, ta, b), lambda l, i: (l, i, 0))
    sds = jax.ShapeDtypeStruct((dep, a, b), F32)
    return _pcall(
        body, name=name, grid=(dep, a // ta),
        in_specs=[pl.BlockSpec((s, None, ta, b), lambda l, i: (0, l, i, 0)), spec, spec, spec],
        out_specs=[spec] * 4, out_shape=[sds] * 4, compiler_params=_params(("parallel", "parallel")),
    )(gsrc, w, m, v)


def _mask_a():
    return lax.broadcasted_iota(jnp.int32, (1, LANES), 1) < HEAD_DIM


def _half_sum(x, m_a):
    sa = jnp.sum(jnp.where(m_a, x, 0.0), axis=1, keepdims=True)
    sb = jnp.sum(jnp.where(m_a, 0.0, x), axis=1, keepdims=True)
    return jnp.where(m_a, sa, sb)


def _head_inv(x, m_a):
    return lax.rsqrt(_half_sum(x * x, m_a) * (1.0 / HEAD_DIM) + EPS)


def _swap32(x):
    first = (lax.broadcasted_iota(jnp.int32, (1, LANES), 1) % HEAD_DIM) < (HEAD_DIM // 2)
    return jnp.where(first, pltpu.roll(x, LANES - HEAD_DIM // 2, 1), pltpu.roll(x, HEAD_DIM // 2, 1))


def _tri(blk, rel):
    r = lax.broadcasted_iota(jnp.int32, (blk, blk), 0)
    c = lax.broadcasted_iota(jnp.int32, (blk, blk), 1)
    return jnp.where(rel(r, c), 1.0, 0.0).astype(BF16)


def _cumdot(x, u, parts):
    acc = None
    r = x
    for i in range(parts):
        xi = r.astype(BF16)
        t = jnp.dot(xi, u, preferred_element_type=F32)
        acc = t if acc is None else acc + t
        if i + 1 < parts:
            r = r - xi.astype(F32)
    return acc


def _rows(i, blk):
    return pl.ds(pl.multiple_of(i * blk, blk), blk)


def _att_blk(n):
    return ATT_BLK if n % ATT_BLK == 0 else min(LANES, n)


def _att_blocks(n, wide_keys=False):
    bk = _att_blk(n)
    bq = ATT_BQ if n % ATT_BQ == 0 else bk
    return bq, (bq if wide_keys else bk)


def _loop(lo, hi, fn):
    def it(i, c):
        fn(i)
        return c

    lax.fori_loop(lo, hi, it, 0)


def _normed(src, g_ref, rope_refs, rows, m_a):
    xv = src[rows, :]
    xn = xv * _head_inv(xv, m_a) * g_ref[...]
    if rope_refs is not None:
        xn = xn * rope_refs[0][rows, :] + _swap32(xn) * rope_refs[1][rows, :]
    return xn


def _bias_lane(h):
    return HEAD_DIM if h == 0 else 0


def _k_for_head(kn, kb_row, h, m_h, lane, blk):
    out = jnp.where(m_h, kn, 0.0)
    if kb_row is not None:
        col = jnp.transpose(jnp.broadcast_to(kb_row, (LANES, blk)))
        hi = col.astype(BF16).astype(F32)
        mid = (col - hi).astype(BF16).astype(F32)
        lo = col - hi - mid
        b = _bias_lane(h)
        out = jnp.where(lane == b, hi, jnp.where(lane == b + 1, mid, jnp.where(lane == b + 2, lo, out)))
    return out.astype(BF16)


def _q_for_head(qb, h, m_h, lane, biased):
    out = jnp.where(m_h, qb, 0)
    if biased:
        b = _bias_lane(h)
        out = jnp.where((lane >= b) & (lane < b + 3), jnp.ones_like(out), out)
    return out


def _head_rows(x, parts=3):
    rr = lax.broadcasted_iota(jnp.int32, (8, LANES), 0)
    ll = lax.broadcasted_iota(jnp.int32, (8, LANES), 1)
    sel = jnp.where(((rr == 0) & (ll < HEAD_DIM)) | ((rr == 1) & (ll >= HEAD_DIM)), 1.0, 0.0).astype(BF16)
    acc = None
    rem = x
    for i in range(parts):
        xi = rem.astype(BF16)
        t = lax.dot_general(sel, xi, NT, preferred_element_type=F32)
        acc = t if acc is None else acc + t
        if i + 1 < parts:
            rem = rem - xi.astype(F32)
    return acc


def _cumdot_left(u, x, parts):
    acc = None
    rem = x
    for i in range(parts):
        xi = rem.astype(BF16)
        t = jnp.dot(u, xi, preferred_element_type=F32)
        acc = t if acc is None else acc + t
        if i + 1 < parts:
            rem = rem - xi.astype(F32)
    return acc


def _q_minus_k(bk, bq):
    return lax.broadcasted_iota(jnp.int32, (bk, bq), 1) - lax.broadcasted_iota(jnp.int32, (bk, bq), 0)


def _stream_rows(stride):
    if stride == 1:
        return _rows
    c = pl.program_id(2)
    return lambda i, blk: pl.ds(c + i * (blk * stride), blk, stride=stride)


def _attn_specs(t, pairs, stride, nq, bq):
    col = lambda off: pl.BlockSpec((None, t, LANES), lambda z, p, c: (z, 0, off + p * pairs))
    vec = pl.BlockSpec((1, LANES), lambda z, p, c: (0, 0))
    seq = pl.BlockSpec((None, t, LANES), lambda z, p, c: (z, 0, 0))
    stat = pl.BlockSpec((None, 2 * pairs, nq, 1, bq), lambda z, p, c: (z * stride + c, p, 0, 0, 0))
    return col, vec, seq, stat


def _attn_fwd(src, offs, npairs, gq, gk, *, rope=None, kbias=None, window, stride=1, pairs=1, name, cargo=None):
    bs, t, _ = src.shape
    n = t // stride
    full = window >= n
    bq, bk = _att_blocks(n, wide_keys=full)
    nq, nk, rq = n // bq, n // bk, bq // bk
    wblk = -(-window // bk)
    biased = kbias is not None
    heads = range(2 * pairs)

    def body(*refs):
        it = iter(refs)
        q_refs, k_refs, v_refs = ([next(it) for _ in range(pairs)] for _ in range(3))
        gq_ref, gk_ref = next(it), next(it)
        rope_refs = (next(it), next(it)) if rope is not None else None
        kb_ref = next(it) if biased else None
        o_refs = [next(it) for _ in range(pairs)]
        lse_ref, qn_s, kh_s, vt_s, acc_s, m_s = (next(it) for _ in range(6))
        m_a = _mask_a()
        masks = (m_a, jnp.logical_not(m_a))
        lane = lax.broadcasted_iota(jnp.int32, (1, LANES), 1)
        row = lax.broadcasted_iota(jnp.int32, (LANES, 1), 0)
        tok = _stream_rows(stride)

        def prep(c):
            rows = _rows(c, bk)
            trows = tok(c, bk)
            for pp in range(pairs):
                qn = _normed(q_refs[pp], gq_ref, rope_refs, trows, m_a)
                qn_s[pp, rows, :] = (qn * SCALE).astype(BF16)
                kn = _normed(k_refs[pp], gk_ref, rope_refs, trows, m_a)
                vt = jnp.transpose(v_refs[pp][trows, :])
                for half in (0, 1):
                    h = 2 * pp + half
                    kh_s[h, rows, :] = _k_for_head(kn, kb_ref[h, c] if biased else None, half, masks[half], lane, bk)
                    vt_s[h, c] = jnp.where(row == _bias_lane(half), 1.0, vt).astype(BF16)

        _loop(0, nk, prep)
        qk = _q_minus_k(bk, bq)

        def qblock(qi):
            rows = _rows(qi, bq)
            qh = [_q_for_head(qn_s[h // 2, rows, :], h % 2, masks[h % 2], lane, biased) for h in heads]
            m_s[...] = jnp.full(m_s.shape, NEG, F32)
            acc_s[...] = jnp.zeros_like(acc_s)

            def step(kj, masked):
                cols = _rows(kj, bk)
                sts = [lax.dot_general(kh_s[h, cols, :], qh[h], NT, preferred_element_type=F32) for h in heads]
                old = [(m_s[h], acc_s[h]) for h in heads]
                if masked:
                    d = qk + (qi * bq - kj * bk)
                    ok = (d >= 0) & (d <= window)
                    sts = [jnp.where(ok, st, NEG) for st in sts]
                new = []
                for h in heads:
                    m, acc = old[h]
                    m2 = jnp.maximum(m, jnp.max(sts[h], axis=0, keepdims=True))
                    pt = jnp.exp(sts[h] - m2).astype(BF16)
                    new.append((m2, jnp.exp(m - m2) * acc + jnp.dot(vt_s[h, kj], pt, preferred_element_type=F32)))
                for h in heads:
                    m_s[h], acc_s[h] = new[h]

            if full:
                _loop(0, qi * rq, lambda kj: step(kj, False))
                _loop(qi * rq, (qi + 1) * rq, lambda kj: step(kj, True))
            else:
                _loop(jnp.maximum(qi * rq - wblk, 0), (qi + 1) * rq, lambda kj: step(kj, True))
            outs = []
            for h in heads:
                acc_t = acc_s[h]
                den = acc_t[_bias_lane(h % 2):_bias_lane(h % 2) + 1, :]
                outs.append(jnp.transpose(acc_t / den))
                lse_ref[h, qi] = m_s[h] + jnp.log(den)
            for pp in range(pairs):
                o_refs[pp][tok(qi, bq), :] = jnp.where(m_a, outs[2 * pp], outs[2 * pp + 1])

        _loop(0, nq, qblock)

    col, vec, seq, stat = _attn_specs(t, pairs, stride, nq, bq)
    ins = [src] * (3 * pairs) + [gq, gk]
    specs = [col(off + pp) for off in offs for pp in range(pairs)] + [vec, vec]
    if rope is not None:
        ins += list(rope)
        specs += [seq, seq]
    if biased:
        ins.append(kbias)
        specs.append(pl.BlockSpec((None, 2, nk, 1, bk), lambda z, p, c: (z, p, 0, 0, 0)))
    scratch = [pltpu.VMEM((pairs, n, LANES), BF16), pltpu.VMEM((2 * pairs, n, LANES), BF16)]
    scratch += [pltpu.VMEM((2 * pairs, nk, LANES, bk), BF16), pltpu.VMEM((2 * pairs, LANES, bq), F32)]
    scratch += [pltpu.VMEM((2 * pairs, 1, bq), F32)]
    assert pairs in (1, npairs)
    ospec = pl.BlockSpec((None, t, LANES), lambda z, p, c: (z, 0, p))
    *os_, lse = _call(
        body, cargo=cargo, name=name, grid=(bs, npairs // pairs, stride), in_specs=specs, out_specs=[ospec] * pairs + [stat],
        out_shape=[jax.ShapeDtypeStruct((bs, t, LANES * npairs // pairs), F32)] * pairs
        + [jax.ShapeDtypeStruct((bs * stride, 2 * npairs, nq, 1, bq), F32)],
        scratch_shapes=scratch, compiler_params=_params(("parallel", "parallel", "arbitrary")),
    )(*ins)
    return (os_[0] if pairs == 1 else jnp.concatenate(os_, axis=-1)), lse


def _attn_bwd(src, offs, npairs, gq, gk, o, do, lse, *, rope=None, kbias=None, window, stride=1, pairs=1, name, cargo=None):
    bs, t, _ = src.shape
    n = t // stride
    full = window >= n
    bq, bk = _att_blocks(n, wide_keys=full)
    nq, nk, rq = n // bq, n // bk, bq // bk
    wblk = -(-window // bk)
    biased = kbias is not None
    heads = range(2 * pairs)
    gdt = BF16 if stride == 1 else F32

    def body(*refs):
        it = iter(refs)
        q_refs, k_refs, v_refs = ([next(it) for _ in range(pairs)] for _ in range(3))
        gq_ref, gk_ref = next(it), next(it)
        o_refs, do_refs = ([next(it) for _ in range(pairs)] for _ in range(2))
        lse_ref = next(it)
        rope_refs = (next(it), next(it)) if rope is not None else None
        kb_ref = next(it) if biased else None
        dq_refs, dk_refs, dv_refs = ([next(it) for _ in range(pairs)] for _ in range(3))
        dg_ref = next(it)
        dkb_ref = next(it) if biased else None
        qn_s, kh_s, vb_s, kt_s, dqn_s, dkh_s, dv_s, dq_s, rs_s = (next(it) for _ in range(9))
        m_a = _mask_a()
        masks = (m_a, jnp.logical_not(m_a))
        lane = lax.broadcasted_iota(jnp.int32, (1, LANES), 1)
        tok = _stream_rows(stride)

        @pl.when((pl.program_id(0) == 0) & (pl.program_id(1) == 0) & (pl.program_id(2) == 0))
        def _():
            dg_ref[...] = jnp.zeros_like(dg_ref)

        def prep(c):
            rows = _rows(c, bk)
            trows = tok(c, bk)
            for pp in range(pairs):
                qn = _normed(q_refs[pp], gq_ref, rope_refs, trows, m_a)
                qn_s[pp, rows, :] = (qn * SCALE).astype(BF16)
                kn = _normed(k_refs[pp], gk_ref, rope_refs, trows, m_a)
                kt_s[pp, c] = jnp.transpose(kn).astype(BF16)
                vb_s[pp, rows, :] = v_refs[pp][trows, :].astype(BF16)
                for half in (0, 1):
                    h = 2 * pp + half
                    kh_s[h, rows, :] = _k_for_head(kn, kb_ref[h, c] if biased else None, half, masks[half], lane, bk)

        _loop(0, nk, prep)
        dkh_s[...] = jnp.zeros_like(dkh_s)
        dv_s[...] = jnp.zeros_like(dv_s)
        qk = _q_minus_k(bk, bq)

        def qblock(qi):
            rows = _rows(qi, bq)
            trows = tok(qi, bq)
            dobs = [do_refs[pp][trows, :] for pp in range(pairs)]
            deltas = [_head_rows(dobs[pp] * o_refs[pp][trows, :]) for pp in range(pairs)]
            qh = [_q_for_head(qn_s[h // 2, rows, :], h % 2, masks[h % 2], lane, biased) for h in heads]
            doms = [jnp.where(masks[h % 2], dobs[h // 2], 0.0).astype(BF16) for h in heads]
            delta = [deltas[h // 2][h % 2:h % 2 + 1, :] for h in heads]
            lses = [lse_ref[h, qi] for h in heads]
            dq_s[...] = jnp.zeros_like(dq_s)
            if biased:
                for h in heads:
                    rs_s[h, qi] = jnp.zeros((1, bq), F32)

            def step(kj, masked):
                cols = _rows(kj, bk)
                vbs = [vb_s[pp, cols, :] for pp in range(pairs)]
                kts = [kt_s[pp, kj] for pp in range(pairs)]
                sts = [lax.dot_general(kh_s[h, cols, :], qh[h], NT, preferred_element_type=F32) for h in heads]
                dpts = [lax.dot_general(vbs[h // 2], doms[h], NT, preferred_element_type=F32) for h in heads]
                if masked:
                    d = qk + (qi * bq - kj * bk)
                    ok = (d >= 0) & (d <= window)
                    sts = [jnp.where(ok, st, NEG) for st in sts]
                new = []
                for h in heads:
                    pt = jnp.exp(sts[h] - lses[h])
                    dst = pt * (dpts[h] - delta[h])
                    dsb = dst.astype(BF16)
                    tk = jnp.dot(dsb, qh[h], preferred_element_type=F32)
                    if biased:
                        tk = tk + jnp.dot((dst - dsb.astype(F32)).astype(BF16), qh[h], preferred_element_type=F32)
                    tv = jnp.dot(pt.astype(BF16), doms[h], preferred_element_type=F32)
                    tq = jnp.dot(kts[h // 2], dsb, preferred_element_type=F32)
                    new.append((tk, tv, tq, jnp.sum(dst, axis=0, keepdims=True) if biased else None))
                for h in heads:
                    dkh_s[h, cols, :] += new[h][0]
                    dq_s[h] += new[h][2]
                    if biased:
                        rs_s[h, qi] += new[h][3]
                for pp in range(pairs):
                    dv_s[pp, cols, :] += new[2 * pp][1] + new[2 * pp + 1][1]

            if full:
                _loop(0, qi * rq, lambda kj: step(kj, False))
                _loop(qi * rq, (qi + 1) * rq, lambda kj: step(kj, True))
            else:
                _loop(jnp.maximum(qi * rq - wblk, 0), (qi + 1) * rq, lambda kj: step(kj, True))
            for pp in range(pairs):
                dqn_s[pp, rows, :] = jnp.where(m_a, jnp.transpose(dq_s[2 * pp]), jnp.transpose(dq_s[2 * pp + 1])) * SCALE

        _loop(0, nq, qblock)

        def finish(c, carry):
            rows = _rows(c, bq)
            trows = tok(c, bq)
            dgq, dgk = carry
            for pp in range(pairs):
                dk_pair = [dkh_s[2 * pp, rows, :], dkh_s[2 * pp + 1, rows, :]]
                if biased:
                    for half in (0, 1):
                        b = _bias_lane(half)
                        dkb_row = jnp.transpose(dk_pair[half])[b:b + 1, :] - rs_s[2 * pp + half, c]
                        for j in range(rq):
                            dkb_ref[2 * pp + half, c * rq + j] = dkb_row[:, j * bk:(j + 1) * bk]
                dv_refs[pp][trows, :] = dv_s[pp, rows, :].astype(gdt)
                grads = (dqn_s[pp, rows, :], jnp.where(m_a, dk_pair[0], dk_pair[1]))
                out = []
                for src_ref, g_ref, dxn, dst in ((q_refs[pp], gq_ref, grads[0], dq_refs[pp]),
                                                 (k_refs[pp], gk_ref, grads[1], dk_refs[pp])):
                    xv = src_ref[trows, :]
                    inv = _head_inv(xv, m_a)
                    y = xv * inv
                    if rope_refs is not None:
                        dxn = dxn * rope_refs[0][trows, :] + _swap32(dxn * rope_refs[1][trows, :])
                    dy = dxn * g_ref[...]
                    dst[trows, :] = (inv * (dy - y * (_half_sum(dy * y, m_a) * (1.0 / HEAD_DIM)))).astype(gdt)
                    out.append(jnp.sum(dxn * y, axis=0, keepdims=True))
                dgq, dgk = dgq + out[0], dgk + out[1]
            return dgq, dgk

        zero = jnp.zeros((1, LANES), F32)
        dgq, dgk = lax.fori_loop(0, nq, finish, (zero, zero))
        dg_ref[0:1, :] += dgq
        dg_ref[1:2, :] += dgk

    assert pairs in (1, npairs)
    col, vec, seq, stat = _attn_specs(t, pairs, stride, nq, bq)
    ins = [src] * (3 * pairs) + [gq, gk] + [o] * pairs + [do] * pairs + [lse]
    specs = [col(off + pp) for off in offs for pp in range(pairs)] + [vec, vec] + [col(pp) for pp in range(pairs)] * 2 + [stat]
    if rope is not None:
        ins += list(rope)
        specs += [seq, seq]
    sds = jax.ShapeDtypeStruct((bs, t, LANES * npairs // pairs), gdt)
    out_shape = [sds] * (3 * pairs) + [jax.ShapeDtypeStruct((8, LANES), F32)]
    ospec = pl.BlockSpec((None, t, LANES), lambda z, p, c: (z, 0, p))
    out_specs = [ospec] * (3 * pairs) + [pl.BlockSpec((8, LANES), lambda z, p, c: (0, 0))]
    if biased:
        kbspec = pl.BlockSpec((None, 2, nk, 1, bk), lambda z, p, c: (z, p, 0, 0, 0))
        ins.append(kbias)
        specs.append(kbspec)
        out_shape.append(jax.ShapeDtypeStruct(kbias.shape, F32))
        out_specs.append(kbspec)
    scratch = [pltpu.VMEM((pairs, n, LANES), BF16), pltpu.VMEM((2 * pairs, n, LANES), BF16), pltpu.VMEM((pairs, n, LANES), BF16)]
    scratch += [pltpu.VMEM((pairs, nk, LANES, bk), BF16), pltpu.VMEM((pairs, n, LANES), F32)]
    scratch += [pltpu.VMEM((2 * pairs, n, LANES), F32), pltpu.VMEM((pairs, n, LANES), F32)]
    scratch += [pltpu.VMEM((2 * pairs, LANES, bq), F32), pltpu.VMEM((2 * pairs, nq, 1, bq), F32)]
    res = _call(
        body, cargo=cargo, name=name, grid=(bs, npairs // pairs, stride), in_specs=specs, out_specs=out_specs, out_shape=out_shape,
        scratch_shapes=scratch, compiler_params=_params(("arbitrary", "arbitrary", "arbitrary")),
    )(*ins)
    return (list(res[:pairs]), list(res[pairs:2 * pairs]), list(res[2 * pairs:3 * pairs]), *res[3 * pairs:])


SB_LOG_PARTS = 2
SB_GRAD_PARTS = 1


def _log_sig_pair(z):
    lsn = jnp.minimum(-z, 0.0) - jnp.log(1.0 + jnp.exp(-jnp.abs(z)))
    return lsn, z + lsn


def _sb_specs(n, pairs, nq, bq):
    col = lambda off: pl.BlockSpec((None, n, LANES * pairs), lambda z, p: (z, 0, off // pairs + p))
    stat = pl.BlockSpec((None, 2 * pairs, nq, 1, bq), lambda z, p: (z, p, 0, 0, 0))
    return col, stat


def _sb_pairs(npairs):
    return 2 if npairs % 2 == 0 else 1


def _sb_fwd(src, offs, npairs, *, name, cargo=None):
    zs, n, _ = src.shape
    bq, bk = _att_blocks(n)
    nq, nk, rq = n // bq, n // bk, bq // bk
    pairs = _sb_pairs(npairs)
    heads = range(2 * pairs)
    lanes = [slice(pp * LANES, (pp + 1) * LANES) for pp in range(pairs)]

    def body(q_ref, k_ref, v_ref, o_ref, lt_ref, qs_s, kb_s, vt_s, acc_s, c_s):
        m_a = _mask_a()
        masks = (m_a, jnp.logical_not(m_a))

        def prep(c):
            rows = _rows(c, bk)
            for pp in range(pairs):
                qs_s[pp, rows, :] = (q_ref[rows, lanes[pp]] * SCALE).astype(BF16)
                kb_s[pp, rows, :] = k_ref[rows, lanes[pp]].astype(BF16)
                vt_s[pp, c] = jnp.transpose(v_ref[rows, lanes[pp]]).astype(BF16)

        _loop(0, nk, prep)
        qk = _q_minus_k(bk, bq)
        u_gt = _tri(bk, lambda r, c: c > r)

        def qblock(qi):
            rows = _rows(qi, bq)
            qms = [jnp.where(masks[h % 2], qs_s[h // 2, rows, :], 0) for h in heads]
            acc_s[...] = jnp.zeros_like(acc_s)
            c_s[...] = jnp.zeros_like(c_s)

            def step(kj, masked):
                cols = _rows(kj, bk)
                zts = [lax.dot_general(kb_s[h // 2, cols, :], qms[h], NT, preferred_element_type=F32) for h in heads]
                old = [c_s[h] for h in heads]
                if masked:
                    ok = (qk + (qi * bq - kj * bk)) > 0
                new = []
                for h in heads:
                    lsn, lsp = _log_sig_pair(zts[h])
                    if masked:
                        lsn = jnp.where(ok, lsn, 0.0)
                    at = jnp.exp(lsp + (old[h] + _cumdot_left(u_gt, lsn, SB_LOG_PARTS)))
                    if masked:
                        at = jnp.where(ok, at, 0.0)
                    new.append((jnp.dot(vt_s[h // 2, kj], at.astype(BF16), preferred_element_type=F32),
                                old[h] + jnp.sum(lsn, axis=0, keepdims=True)))
                for h in heads:
                    acc_s[h] += new[h][0]
                    c_s[h] = new[h][1]

            _loop(0, rq, lambda t: step((qi + 1) * rq - 1 - t, True))
            _loop(0, qi * rq, lambda t: step(qi * rq - 1 - t, False))
            for pp in range(pairs):
                o_ref[rows, lanes[pp]] = jnp.where(m_a, jnp.transpose(acc_s[2 * pp]), jnp.transpose(acc_s[2 * pp + 1]))
            for h in heads:
                lt_ref[h, qi] = c_s[h]

        _loop(0, nq, qblock)

    col, stat = _sb_specs(n, pairs, nq, bq)
    scratch = [pltpu.VMEM((pairs, n, LANES), BF16)] * 2 + [pltpu.VMEM((pairs, nk, LANES, bk), BF16)]
    scratch += [pltpu.VMEM((2 * pairs, LANES, bq), F32), pltpu.VMEM((2 * pairs, 1, bq), F32)]
    return _call(
        body, cargo=cargo, name=name, grid=(zs, npairs // pairs),
        in_specs=[col(offs[0]), col(offs[1]), col(offs[2])], out_specs=[col(0), stat],
        out_shape=[jax.ShapeDtypeStruct((zs, n, LANES * npairs), F32), jax.ShapeDtypeStruct((zs, 2 * npairs, nq, 1, bq), F32)],
        scratch_shapes=scratch, compiler_params=_params(("parallel", "parallel")),
    )(src, src, src)


def _sb_bwd(src, offs, npairs, do, ltot, *, name, cargo=None):
    zs, n, _ = src.shape
    bq, bk = _att_blocks(n)
    nq, nk, rq = n // bq, n // bk, bq // bk
    pairs = _sb_pairs(npairs)
    heads = range(2 * pairs)
    lanes = [slice(pp * LANES, (pp + 1) * LANES) for pp in range(pairs)]

    def body(q_ref, k_ref, v_ref, do_ref, lt_ref, dq_ref, dk_ref, dv_ref, qs_s, kb_s, vb_s, kt_s, dk_s, dv_s, dq_s, lp_s, ep_s):
        m_a = _mask_a()
        masks = (m_a, jnp.logical_not(m_a))

        def prep(c):
            rows = _rows(c, bk)
            for pp in range(pairs):
                qs_s[pp, rows, :] = (q_ref[rows, lanes[pp]] * SCALE).astype(BF16)
                kv = k_ref[rows, lanes[pp]]
                kb_s[pp, rows, :] = kv.astype(BF16)
                kt_s[pp, c] = jnp.transpose(kv).astype(BF16)
                vb_s[pp, rows, :] = v_ref[rows, lanes[pp]].astype(BF16)

        _loop(0, nk, prep)
        dk_s[...] = jnp.zeros_like(dk_s)
        dv_s[...] = jnp.zeros_like(dv_s)
        qk = _q_minus_k(bk, bq)
        u_le = _tri(bk, lambda r, c: c <= r)
        u_lt = _tri(bk, lambda r, c: c < r)

        def qblock(qi):
            rows = _rows(qi, bq)
            qms = [jnp.where(masks[h % 2], qs_s[h // 2, rows, :], 0) for h in heads]
            doms = [jnp.where(masks[h % 2], do_ref[rows, lanes[h // 2]], 0.0).astype(BF16) for h in heads]
            lts = [lt_ref[h, qi] for h in heads]
            dq_s[...] = jnp.zeros_like(dq_s)
            lp_s[...] = jnp.zeros_like(lp_s)
            ep_s[...] = jnp.zeros_like(ep_s)

            def step(kj, masked):
                cols = _rows(kj, bk)
                kbs = [kb_s[pp, cols, :] for pp in range(pairs)]
                kts = [kt_s[pp, kj] for pp in range(pairs)]
                zts = [lax.dot_general(kbs[h // 2], qms[h], NT, preferred_element_type=F32) for h in heads]
                dats = [lax.dot_general(vb_s[h // 2, cols, :], doms[h], NT, preferred_element_type=F32) for h in heads]
                old = [(lp_s[h], ep_s[h]) for h in heads]
                if masked:
                    ok = (qk + (qi * bq - kj * bk)) > 0
                new = []
                for h in heads:
                    lp, ep = old[h]
                    lsn, lsp = _log_sig_pair(zts[h])
                    sig = jnp.exp(lsp)
                    if masked:
                        lsn = jnp.where(ok, lsn, 0.0)
                    at = jnp.exp(lsp + (lts[h] - (lp + _cumdot_left(u_le, lsn, SB_LOG_PARTS))))
                    if masked:
                        at = jnp.where(ok, at, 0.0)
                    et = dats[h] * at
                    big_e = ep + _cumdot_left(u_lt, et, SB_GRAD_PARTS)
                    dzt = et - sig * (et + big_e)
                    if masked:
                        dzt = jnp.where(ok, dzt, 0.0)
                    dzb = dzt.astype(BF16)
                    new.append((jnp.dot(dzb, qms[h], preferred_element_type=F32),
                                jnp.dot(at.astype(BF16), doms[h], preferred_element_type=F32),
                                jnp.dot(kts[h // 2], dzb, preferred_element_type=F32),
                                lp + jnp.sum(lsn, axis=0, keepdims=True), ep + jnp.sum(et, axis=0, keepdims=True)))
                for h in heads:
                    dq_s[h] += new[h][2]
                    lp_s[h], ep_s[h] = new[h][3], new[h][4]
                for pp in range(pairs):
                    dk_s[pp, cols, :] += new[2 * pp][0] + new[2 * pp + 1][0]
                    dv_s[pp, cols, :] += new[2 * pp][1] + new[2 * pp + 1][1]

            _loop(0, qi * rq, lambda kj: step(kj, False))
            _loop(qi * rq, (qi + 1) * rq, lambda kj: step(kj, True))
            for pp in range(pairs):
                dq = jnp.where(m_a, jnp.transpose(dq_s[2 * pp]), jnp.transpose(dq_s[2 * pp + 1]))
                dq_ref[rows, lanes[pp]] = (dq * SCALE).astype(BF16)

        _loop(0, nq, qblock)

        def store(c):
            rows = _rows(c, bk)
            for pp in range(pairs):
                dk_ref[rows, lanes[pp]] = dk_s[pp, rows, :].astype(BF16)
                dv_ref[rows, lanes[pp]] = dv_s[pp, rows, :].astype(BF16)

        _loop(0, nk, store)

    col, stat = _sb_specs(n, pairs, nq, bq)
    ospec = col(0)
    sds = jax.ShapeDtypeStruct((zs, n, LANES * npairs), BF16)
    scratch = [pltpu.VMEM((pairs, n, LANES), BF16)] * 3 + [pltpu.VMEM((pairs, nk, LANES, bk), BF16)]
    scratch += [pltpu.VMEM((pairs, n, LANES), F32)] * 2
    scratch += [pltpu.VMEM((2 * pairs, LANES, bq), F32), pltpu.VMEM((2 * pairs, 1, bq), F32), pltpu.VMEM((2 * pairs, 1, bq), F32)]
    return _call(
        body, cargo=cargo, name=name, grid=(zs, npairs // pairs),
        in_specs=[col(offs[0]), col(offs[1]), col(offs[2]), ospec, stat],
        out_specs=[ospec] * 3, out_shape=[sds] * 3, scratch_shapes=scratch,
        compiler_params=_params(("parallel", "parallel")),
    )(src, src, src, do, ltot)


def _fox_gate_fwd(lg, bias, *, name):
    bs, nh, t = lg.shape
    blk = min(LANES, t)

    def body(lg_ref, b_ref, kb_ref):
        u_le = _tri(blk, lambda r, c: r <= c)
        carry = jnp.zeros((nh, 1), F32)
        for j in range(t // blk):
            sl = slice(j * blk, (j + 1) * blk)
            xv = lg_ref[:, sl] + b_ref[...]
            lf = jnp.minimum(xv, 0.0) - jnp.log(1.0 + jnp.exp(-jnp.abs(xv)))
            kb_ref[:, sl] = -(carry + _cumdot(lf, u_le, 3))
            carry = carry + jnp.sum(lf, axis=1, keepdims=True)

    spec = pl.BlockSpec((None, nh, t), lambda i: (i, 0, 0))
    return _pcall(
        body, name=name, grid=(bs,), in_specs=[spec, pl.BlockSpec((nh, 1), lambda i: (0, 0))], out_specs=spec,
        out_shape=jax.ShapeDtypeStruct((bs, nh, t), F32), compiler_params=_params(("parallel",)),
    )(lg, bias)


def _fox_gate_bwd(dkb, lg, bias, *, name):
    bs, nh, t = lg.shape
    blk = min(LANES, t)

    def body(dkb_ref, lg_ref, b_ref, dlg_ref, db_ref):
        @pl.when(pl.program_id(0) == 0)
        def _():
            db_ref[...] = jnp.zeros_like(db_ref)

        u_ge = _tri(blk, lambda r, c: r >= c)
        carry = jnp.zeros((nh, 1), F32)
        tot = jnp.zeros((nh, 1), F32)
        for j in reversed(range(t // blk)):
            sl = slice(j * blk, (j + 1) * blk)
            df = -dkb_ref[:, sl]
            dlf = carry + _cumdot(df, u_ge, 3)
            carry = carry + jnp.sum(df, axis=1, keepdims=True)
            xv = lg_ref[:, sl] + b_ref[...]
            dlg = dlf * jax.nn.sigmoid(-xv)
            dlg_ref[:, sl] = dlg
            tot = tot + jnp.sum(dlg, axis=1, keepdims=True)
        db_ref[...] += jnp.broadcast_to(tot, db_ref.shape)

    spec = pl.BlockSpec((None, nh, t), lambda i: (i, 0, 0))
    return _pcall(
        body, name=name, grid=(bs,), in_specs=[spec, spec, pl.BlockSpec((nh, 1), lambda i: (0, 0))],
        out_specs=[spec, pl.BlockSpec((nh, LANES), lambda i: (0, 0))],
        out_shape=[jax.ShapeDtypeStruct((bs, nh, t), F32), jax.ShapeDtypeStruct((nh, LANES), F32)],
        compiler_params=_params(("arbitrary",)),
    )(dkb, lg, bias)


def _place():
    return lax.axis_index("x"), lax.axis_index("y"), lax.axis_index("c")


def _flip(v, f):
    return 1 - v if f else v


FLIPS = [(fx, fy, fc) for fx in (0, 1) for fy in (0, 1) for fc in (0, 1)][1:]


def _comm_sems(nw):
    return [pltpu.SemaphoreType.DMA((7, nw)), pltpu.SemaphoreType.DMA((7, nw)), pltpu.SemaphoreType.DMA((nw,))]


def _gather_cargo(shards, on_done):
    nw = len(shards)

    def parts(x_refs, out_refs, sems):
        send_sems, recv_sems, local_sems = sems
        x, y, cc = _place()
        me, sibling = (x, y, cc), (x, y, 1 - cc)
        chips = [(1 - x, y), (x, 1 - y), (1 - x, 1 - y)]

        def slot(i, px, py, pc):
            return out_refs[i].at[4 * px + 2 * py + pc]

        def copy(i, k, block, to, src=None):
            return pltpu.make_async_remote_copy(
                src_ref=slot(i, *block) if src is None else src, dst_ref=slot(i, *block),
                send_sem=send_sems.at[k, i], recv_sem=recv_sems.at[k, i], device_id=to, device_id_type=MESH)

        mine = [pltpu.make_async_copy(x_refs[i], slot(i, *me), local_sems.at[i]) for i in range(nw)]
        first = []
        for i in range(nw):
            first.append(copy(i, 0, me, sibling, src=x_refs[i]))
            first += [copy(i, 1 + j, me, (*chip, cc), src=x_refs[i]) for j, chip in enumerate(chips)]
        return me, sibling, chips, cc, copy, mine, first

    def start(x_refs, out_refs, sems):
        *_, mine, first = parts(x_refs, out_refs, sems)
        for cp in mine + first:
            cp.start()

    def finish(x_refs, out_refs, sems):
        me, sibling, chips, cc, copy, mine, first = parts(x_refs, out_refs, sems)
        passed = []
        for i in range(nw):
            for j, chip in enumerate(chips):
                copy(i, 1 + j, (*chip, cc), me).wait_recv()
                passed.append(copy(i, 4 + j, (*chip, cc), sibling))
                passed[-1].start()
        for i in range(nw):
            copy(i, 0, sibling, me).wait_recv()
            for j, chip in enumerate(chips):
                copy(i, 4 + j, (*chip, 1 - cc), me).wait_recv()
        for cp in first + passed:
            cp.wait_send()
        for cp in mine:
            cp.wait()

    out_shape = [jax.ShapeDtypeStruct((N_DEV, *s.shape), s.dtype) for s in shards]
    return _Cargo(shards, out_shape, _comm_sems(nw), start, finish, on_done)


def _scatter_cargo(slots, prev, layer, depth, on_done, row0=0, rows=None):
    nw = len(slots)

    def parts(refs, recv_refs, sems):
        g_refs = refs[:nw]
        send_sems, recv_sems, local_sems = sems
        x, y, cc = _place()
        my = 4 * x + 2 * y + cc

        def dst(i):
            return recv_refs[i].at[my, layer, pl.ds(row0, slots[i].shape[1])]

        mine, copies = [], []
        for i in range(nw):
            mine.append(pltpu.make_async_copy(g_refs[i].at[my], dst(i), local_sems.at[i]))
            for k, (fx, fy, fc) in enumerate(FLIPS):
                px, py, pc = _flip(x, fx), _flip(y, fy), _flip(cc, fc)
                copies.append(pltpu.make_async_remote_copy(
                    src_ref=g_refs[i].at[4 * px + 2 * py + pc], dst_ref=dst(i),
                    send_sem=send_sems.at[k, i], recv_sem=recv_sems.at[k, i], device_id=(px, py, pc), device_id_type=MESH))
        return mine, copies

    def start(refs, recv_refs, sems):
        mine, copies = parts(refs, recv_refs, sems)
        for cp in mine + copies:
            cp.start()

    def finish(refs, recv_refs, sems):
        mine, copies = parts(refs, recv_refs, sems)
        for cp in copies:
            cp.wait_recv()
        for cp in copies:
            cp.wait_send()
        for cp in mine:
            cp.wait()

    ins, aliases = list(slots), {}
    for i, p in enumerate(prev):
        if p is not None:
            aliases[len(ins)] = i
            ins.append(p)
    out_shape = [jax.ShapeDtypeStruct((N_DEV, depth, rows or s.shape[1], s.shape[2]), s.dtype) for s in slots]
    return _Cargo(ins, out_shape, _comm_sems(nw), start, finish, on_done, aliases)


def _exchange(cargo, *, name):
    def body(*refs):
        c_in = len(cargo.ins)
        c_out = len(cargo.out_shape)
        cargo.start(refs[:c_in], refs[c_in:c_in + c_out], refs[c_in + c_out:])
        cargo.finish(refs[:c_in], refs[c_in:c_in + c_out], refs[c_in + c_out:])

    hbm = pl.BlockSpec(memory_space=pl.ANY)
    res = _pcall(
        body, name=name, in_specs=[hbm] * len(cargo.ins), out_specs=[hbm] * len(cargo.out_shape), out_shape=cargo.out_shape,
        scratch_shapes=cargo.sems, input_output_aliases=dict(cargo.aliases),
    )(*cargo.ins)
    cargo.on_done(list(res))


def _allreduce_small(blob, *, name):
    r, c = blob.shape

    def body(x_ref, out_ref, buf, send_sems, recv_sems):
        x, y, cc = _place()
        my = 4 * x + 2 * y + cc
        copies = []
        for k, (fx, fy, fc) in enumerate(FLIPS):
            peer = (_flip(x, fx), _flip(y, fy), _flip(cc, fc))
            copies.append(pltpu.make_async_remote_copy(
                src_ref=x_ref, dst_ref=buf.at[my], send_sem=send_sems.at[k], recv_sem=recv_sems.at[k],
                device_id=peer, device_id_type=MESH))
        for cp in copies:
            cp.start()
        buf[my] = x_ref[...]
        for cp in copies:
            cp.wait_recv()
        for cp in copies:
            cp.wait_send()
        acc = buf[0]
        for i in range(1, N_DEV):
            acc = acc + buf[i]
        out_ref[...] = acc

    vmem = pl.BlockSpec(memory_space=pltpu.VMEM)
    return _pcall(
        body, name=name, in_specs=[vmem], out_specs=vmem, out_shape=jax.ShapeDtypeStruct((r, c), F32),
        scratch_shapes=[pltpu.VMEM((N_DEV, r, c), F32), pltpu.SemaphoreType.DMA((7,)), pltpu.SemaphoreType.DMA((7,))],
    )(blob)


BIG = ("w_in", "w_mlp_in", "w_mlp_out", "w_up_fox", "w_up_sb", "w_up_dil", "w_out")
ROW_SHARDED = ("w_out", "w_mlp_out")
SMALL = ("attn_norm", "b_forget", "q_norm_fox", "k_norm_fox", "q_norm_dil", "k_norm_dil", "mlp_norm")
BLOB_ROWS = 512


def _pack(parts, dtype):
    flat = jnp.concatenate([p.reshape(-1).astype(dtype) for p in parts])
    size = -(-flat.shape[0] // (BLOB_ROWS * LANES)) * (BLOB_ROWS * LANES)
    return jnp.pad(flat, (0, size - flat.shape[0])).reshape(-1, LANES)


def _unpack(blob, shapes):
    flat = blob.reshape(-1)
    out, off = [], 0
    for shp in shapes:
        size = 1
        for s in shp:
            size *= s
        out.append(flat[off:off + size].reshape(shp))
        off += size
    return out


def _join_shards(name, sh):
    if name in ROW_SHARDED:
        return sh.reshape(-1, sh.shape[2])
    return jnp.transpose(sh, (1, 0, 2)).reshape(sh.shape[1], -1)


def _split_shards(name, full):
    a, b = full.shape
    if name in ROW_SHARDED:
        return full.reshape(N_DEV, a // N_DEV, b)
    return jnp.transpose(full.reshape(a, N_DEV, b // N_DEV), (1, 0, 2))


def _in_segments(d_in):
    o1 = 3 * W_FOX
    o2 = o1 + N_HEADS_FOX
    return (0, o1, 0), (o2, d_in, -N_HEADS_FOX), (o1, o2, d_in - o2)


def _join_w_in(sh, dp):
    b = sh.shape[2]
    pieces = []
    for s, e, _ in _in_segments(N_DEV * b):
        for j in range(s // b, (e - 1) // b + 1):
            pieces.append(sh[j, :, max(s, j * b) - j * b:min(e, (j + 1) * b) - j * b])
    pieces.append(jnp.zeros((sh.shape[1], dp - N_DEV * b), sh.dtype))
    return jnp.concatenate(pieces, axis=1)


def _split_w_in(gp, d_in):
    b = d_in // N_DEV
    shards = []
    for j in range(N_DEV):
        runs = []
        for s, e, shift in sorted(_in_segments(d_in)):
            lo, hi = max(s, j * b), min(e, (j + 1) * b)
            if lo < hi:
                runs.append(gp[:, lo + shift:hi + shift])
        shards.append(jnp.concatenate(runs, axis=1))
    return jnp.stack(shards)


def _stat_to_tokens(st, r, b):
    hh = st.shape[1]
    n = st.shape[2] * st.shape[4]
    return jnp.transpose(st.reshape(b, r, hh, n), (0, 2, 3, 1)).reshape(b, hh, n * r)


def _stat_to_streams(tok, r, blk):
    b, hh, t = tok.shape
    n = t // r
    return jnp.transpose(tok.reshape(b, hh, n, r), (0, 3, 1, 2)).reshape(b * r, hh, n // blk, 1, blk)


def _rope_tables(positions):
    half = HEAD_DIM // 2
    inv = 1.0 / (ROPE_THETA ** (jnp.arange(half, dtype=F32) / half))
    ang = positions.astype(F32)[..., None] * inv
    cos, sin = jnp.cos(ang), jnp.sin(ang)
    return jnp.tile(cos, (1, 1, 4)), jnp.tile(jnp.concatenate([-sin, sin], axis=-1), (1, 1, 2))


def _gain2(g):
    return jnp.tile(g.reshape(1, HEAD_DIM), (1, 2))


def _dil_offs(g):
    c0 = (P_DIL + g * W_DIL) // LANES
    return c0, c0 + W_DILQ // LANES, c0 + 2 * W_DILQ // LANES


def _w_in_split(d):
    cut = (3 * d // 8) // LANES * LANES
    return cut if cut > 0 else d // 2


def _dil_pairs(r):
    return N_HEADS_DIL // 2


def _layer_fwd(l, x, h, w, small, ropes, bl, t, cargo, next_gain):
    n, d = x.shape
    s = {}
    s["x"] = x
    s["h"] = _rmsnorm_fwd(x, small["attn_norm"][l].reshape(1, d), name=f"norm_attn_fwd{l}") if h is None else h
    proj = _mm(s["h"], w["w_in"][l], tn=PROJ_TILE, name=f"mm_proj{l}")
    s["proj"] = proj
    dp = proj.shape[1]
    proj3 = proj.reshape(bl, t, dp)
    p_fg = P_GATE + 3 * d

    lg = jnp.transpose(proj3[:, :, p_fg:p_fg + N_HEADS_FOX], (0, 2, 1))
    s["lg"] = lg
    kb = _fox_gate_fwd(lg, small["b_forget"][l].reshape(N_HEADS_FOX, 1), name=f"fox_gate_fwd{l}")
    blk = _att_blocks(t, wide_keys=True)[1]
    kb5 = kb.reshape(bl, N_HEADS_FOX, t // blk, 1, blk)
    s["kb5"] = kb5
    gqf, gkf = _gain2(small["q_norm_fox"][l]), _gain2(small["k_norm_fox"][l])
    fo = P_FOX // LANES
    fox_offs = (fo, fo + W_FOX // LANES, fo + 2 * W_FOX // LANES)
    out_a, lse_a = _attn_fwd(proj3, fox_offs, N_HEADS_FOX // 2, gqf, gkf, kbias=kb5, window=t, name=f"fox_fwd{l}",
                             cargo=cargo.get("fox_fwd"))
    s["out_a"], s["lse_a"] = out_a, lse_a

    so = P_SB // LANES
    sb_offs = (so, so + W_SB // LANES, so + 2 * W_SB // LANES)
    out_b, lt_b = _sb_fwd(proj3, sb_offs, N_HEADS_SB // 2, name=f"sb_fwd{l}", cargo=cargo.get("sb_fwd"))
    s["out_b"], s["lt_b"] = out_b, lt_b

    gqd, gkd = _gain2(small["q_norm_dil"][l]), _gain2(small["k_norm_dil"][l])
    os_, lses = [], []
    for g, (window, r) in enumerate(DIL_PATTERNS):
        o_g, lse_g = _attn_fwd(proj3, _dil_offs(g), N_HEADS_DIL // 2, gqd, gkd, rope=ropes, window=window // r, stride=r,
                               pairs=_dil_pairs(r), name=f"dil_fwd{l}_{g}")
        os_.append(o_g.reshape(n, W_DIL))
        lses.append(_stat_to_tokens(lse_g, r, bl).reshape(bl * N_HEADS_DIL, t))
    lse_c, *ws = _dil_weights(lses, name=f"dil_weights{l}")
    ws = [jnp.repeat(jnp.transpose(wg.reshape(bl, N_HEADS_DIL, t), (0, 2, 1)).reshape(n, N_HEADS_DIL), HEAD_DIM, axis=1) for wg in ws]
    out_c = _dil_mix(os_, ws, name=f"dil_mix{l}")
    s["out_c"], s["lse_c"] = out_c, lse_c.reshape(bl, N_HEADS_DIL, t)

    ys = [_mm(out_a.reshape(n, W_FOX), w["w_up_fox"][l], out_dtype=BF16, name=f"mm_up_fox{l}"),
          _mm(out_b.reshape(n, W_SB), w["w_up_sb"][l], out_dtype=BF16, name=f"mm_up_sb{l}"),
          _mm(out_c, w["w_up_dil"][l], out_dtype=BF16, name=f"mm_up_dil{l}")]
    s["ys"] = ys
    s["merged"] = _gate_merge_fwd(proj, ys, name=f"gate_merge_fwd{l}")
    x1, s["h2"] = _mm(s["merged"], w["w_out"][l], add=x, norm_gain=small["mlp_norm"][l].reshape(1, d), name=f"mm_out{l}")
    s["x1"] = x1

    s["u"], s["a"] = _mm(s["h2"], w["w_mlp_in"][l], relu2=True, out_dtype=BF16, name=f"mm_mlp_in{l}")
    if next_gain is None:
        return _mm(s["a"], w["w_mlp_out"][l], add=x1, name=f"mm_mlp_out{l}"), None, s
    x2, h_next = _mm(s["a"], w["w_mlp_out"][l], add=x1, norm_gain=next_gain, name=f"mm_mlp_out{l}")
    return x2, h_next, s


def _layer_bwd(l, dx2, s, w, small, ropes, bl, t, hooks):
    n, d = dx2.shape
    gw, gs = {}, {}

    def cargo(call):
        return hooks[call](gw) if call in hooks else None
    du = _mm(dx2, w["w_mlp_out"][l], tb=True, relu_grad_of=s["u"], out_dtype=BF16, name=f"mm_du{l}")
    gw["w_mlp_out"] = _mm(s["a"], dx2, ta=True, tm=2048, name=f"mm_dw_mlp_out{l}")
    gw["w_mlp_in"] = _mm(s["h2"], du, ta=True, name=f"mm_dw_mlp_in{l}")
    dh2 = _mm(du, w["w_mlp_in"][l], tb=True, name=f"mm_dh2{l}")
    dx1, gs["mlp_norm"] = _rmsnorm_bwd(s["x1"], small["mlp_norm"][l].reshape(1, d), dh2, dx2, name=f"norm_mlp_bwd{l}")

    dmerged = _mm(dx1, w["w_out"][l], tb=True, name=f"mm_dmerged{l}")
    gw["w_out"] = _mm(s["merged"], dx1, ta=True, name=f"mm_dw_out{l}")
    dya, dyb, dyc, dgl0, dgl1, dgl2 = _gate_merge_bwd(s["proj"], s["ys"], dmerged, name=f"gate_merge_bwd{l}")
    out_a2, out_b2 = s["out_a"].reshape(n, W_FOX), s["out_b"].reshape(n, W_SB)
    gw["w_up_fox"] = _mm(out_a2, dya, ta=True, name=f"mm_dw_up_fox{l}")
    gw["w_up_sb"] = _mm(out_b2, dyb, ta=True, name=f"mm_dw_up_sb{l}")
    gw["w_up_dil"] = _mm(s["out_c"], dyc, ta=True, name=f"mm_dw_up_dil{l}")
    dout_a = _mm(dya, w["w_up_fox"][l], tb=True, name=f"mm_dout_a{l}").reshape(bl, t, W_FOX)
    dout_b = _mm(dyb, w["w_up_sb"][l], tb=True, name=f"mm_dout_b{l}").reshape(bl, t, W_SB)
    dout_c = _mm(dyc, w["w_up_dil"][l], tb=True, name=f"mm_dout_c{l}").reshape(bl, t, W_DIL)

    proj3 = s["proj"].reshape(bl, t, -1)
    gqf, gkf = _gain2(small["q_norm_fox"][l]), _gain2(small["k_norm_fox"][l])
    fo = P_FOX // LANES
    fox_offs = (fo, fo + W_FOX // LANES, fo + 2 * W_FOX // LANES)
    (dq_a,), (dk_a,), (dv_a,), dg_a, dkb5 = _attn_bwd(proj3, fox_offs, N_HEADS_FOX // 2, gqf, gkf, s["out_a"], dout_a, s["lse_a"],
                                             kbias=s["kb5"], window=t, name=f"fox_bwd{l}", cargo=cargo("fox_bwd"))
    gs["fox_gains"] = dg_a
    dlg, gs["b_forget"] = _fox_gate_bwd(dkb5.reshape(bl, N_HEADS_FOX, t), s["lg"], small["b_forget"][l].reshape(N_HEADS_FOX, 1),
                                        name=f"fox_gate_bwd{l}")
    so = P_SB // LANES
    sb_offs = (so, so + W_SB // LANES, so + 2 * W_SB // LANES)
    dq_b, dk_b, dv_b = _sb_bwd(proj3, sb_offs, N_HEADS_SB // 2, dout_b, s["lt_b"], name=f"sb_bwd{l}", cargo=cargo("sb_bwd"))
    gqd, gkd = _gain2(small["q_norm_dil"][l]), _gain2(small["k_norm_dil"][l])
    out_c3 = s["out_c"].reshape(bl, t, W_DIL)
    dqs, dks, dvs, dgd = [], [], [], None
    for g, (window, r) in enumerate(DIL_PATTERNS):
        lse_g = _stat_to_streams(s["lse_c"], r, _att_blocks(t // r)[0])
        dq_g, dk_g, dv_g, dg_g = _attn_bwd(proj3, _dil_offs(g), N_HEADS_DIL // 2, gqd, gkd, out_c3, dout_c, lse_g, rope=ropes,
                                           window=window // r, stride=r, pairs=_dil_pairs(r), name=f"dil_bwd{l}_{g}")
        dqs += dq_g
        dks += dk_g
        dvs += dv_g
        dgd = dg_g if dgd is None else jnp.concatenate([dgd, dg_g], axis=0)
    gs["dil_gains"] = dgd

    dlg_cols = jnp.pad(jnp.transpose(dlg, (0, 2, 1)).reshape(n, N_HEADS_FOX), ((0, 0), (0, LANES - N_HEADS_FOX)))
    parts = [p.reshape(n, -1) for p in [dq_a, dk_a, dv_a, dq_b, dk_b, dv_b] + dqs + dks + dvs] + [dgl0, dgl1, dgl2, dlg_cols]
    dproj = _assemble_cols(parts, s["proj"].shape[1], name=f"assemble_dproj{l}")
    if "mm_dw_in_hi" in hooks:
        half = _w_in_split(d)
        gw["w_in_lo"] = _mm(s["h"][:, :half], dproj, ta=True, tn=PROJ_TILE, name=f"mm_dw_in_lo{l}")
        gw["w_in_hi"] = _mm(s["h"][:, half:], dproj, ta=True, tn=PROJ_TILE, name=f"mm_dw_in_hi{l}", cargo=cargo("mm_dw_in_hi"))
    else:
        gw["w_in"] = _mm(s["h"], dproj, ta=True, tn=PROJ_TILE, name=f"mm_dw_in{l}")
    dh = _mm(dproj, w["w_in"][l], tb=True, tn=1024, tk=PROJ_TILE, name=f"mm_dh{l}", cargo=cargo("mm_dh"))
    dx, gs["attn_norm"] = _rmsnorm_bwd(s["x"], small["attn_norm"][l].reshape(1, d), dh, dx1, name=f"norm_attn_bwd{l}")
    return dx, gw, gs


def kernel(x, positions, attn_norm, w_in, b_forget, q_norm_fox, k_norm_fox, q_norm_dil, k_norm_dil, w_up_fox, w_up_sb, w_up_dil, w_out, mlp_norm, w_mlp_in, w_mlp_out, loss_target, m_attn_norm, m_w_in, m_b_forget, m_q_norm_fox, m_k_norm_fox, m_q_norm_dil, m_k_norm_dil, m_w_up_fox, m_w_up_sb, m_w_up_dil, m_w_out, m_mlp_norm, m_w_mlp_in, m_w_mlp_out, v_attn_norm, v_w_in, v_b_forget, v_q_norm_fox, v_k_norm_fox, v_q_norm_dil, v_k_norm_dil, v_w_up_fox, v_w_up_sb, v_w_up_dil, v_w_out, v_mlp_norm, v_w_mlp_in, v_w_mlp_out):
    bl, t, d = x.shape
    n = bl * t
    depth = attn_norm.shape[0]
    wl = dict(w_in=w_in, w_up_fox=w_up_fox, w_up_sb=w_up_sb, w_up_dil=w_up_dil, w_out=w_out, w_mlp_in=w_mlp_in, w_mlp_out=w_mlp_out)
    ml = dict(w_in=m_w_in, w_up_fox=m_w_up_fox, w_up_sb=m_w_up_sb, w_up_dil=m_w_up_dil, w_out=m_w_out, w_mlp_in=m_w_mlp_in, w_mlp_out=m_w_mlp_out)
    vl = dict(w_in=v_w_in, w_up_fox=v_w_up_fox, w_up_sb=v_w_up_sb, w_up_dil=v_w_up_dil, w_out=v_w_out, w_mlp_in=v_w_mlp_in, w_mlp_out=v_w_mlp_out)
    small = dict(attn_norm=attn_norm, b_forget=b_forget, q_norm_fox=q_norm_fox, k_norm_fox=k_norm_fox, q_norm_dil=q_norm_dil,
                 k_norm_dil=k_norm_dil, mlp_norm=mlp_norm)
    m_small = dict(attn_norm=m_attn_norm, b_forget=m_b_forget, q_norm_fox=m_q_norm_fox, k_norm_fox=m_k_norm_fox,
                   q_norm_dil=m_q_norm_dil, k_norm_dil=m_k_norm_dil, mlp_norm=m_mlp_norm)
    v_small = dict(attn_norm=v_attn_norm, b_forget=v_b_forget, q_norm_fox=v_q_norm_fox, k_norm_fox=v_k_norm_fox,
                   q_norm_dil=v_q_norm_dil, k_norm_dil=v_k_norm_dil, mlp_norm=v_mlp_norm)

    d_in = w_in.shape[-1] * N_DEV
    dp = -(-d_in // 512) * 512
    rest = [k for k in BIG if k != "w_in"]
    w = {k: [None] * depth for k in BIG}

    def gather(items):
        def done(res):
            for (k, l), sh in zip(items, res):
                w[k][l] = _join_w_in(sh, dp) if k == "w_in" else _join_shards(k, sh)

        return _gather_cargo([wl[k][l].astype(BF16) for k, l in items], done)

    _exchange(gather([("w_in", 0)]), name="gather_first")
    cos, sin = _rope_tables(positions)
    ropes = (cos, sin)

    xl, hl = x.reshape(n, d), None
    saved = []
    for l in range(depth):
        cargo = {"fox_fwd": gather([(k, l) for k in rest])}
        if l + 1 < depth:
            cargo["sb_fwd"] = gather([("w_in", l + 1)])
        next_gain = attn_norm[l + 1].reshape(1, d) if l + 1 < depth else None
        xl, hl, s = _layer_fwd(l, xl, hl, w, small, ropes, bl, t, cargo, next_gain)
        saved.append(s)
    dy, loss_part = _loss_head(xl, loss_target.reshape(n, d), name="loss_head")

    recv = {}

    def scatter(names, l, grads):
        def done(res):
            recv.update(zip(names, res))

        slots = [(_split_w_in(grads[k], d_in) if k == "w_in" else _split_shards(k, grads[k])).astype(BF16) for k in names]
        return _scatter_cargo(slots, [recv.get(k) for k in names], l, depth, done)

    def scatter_w_in_rows(part, l, row0):
        def done(res):
            recv["w_in"] = res[0]

        return _scatter_cargo([_split_w_in(part, d_in).astype(BF16)], [recv.get("w_in")], l, depth, done, row0=row0, rows=d)

    gss = [None] * depth
    above = None
    for l in reversed(range(depth)):
        hooks = {"fox_bwd": lambda gw, l=l: scatter(rest, l, gw)}
        if above is not None:
            hooks["sb_bwd"] = lambda gw, l=l, g=above: scatter(["w_in"], l + 1, g)
        if l == 0:
            hooks["mm_dw_in_hi"] = lambda gw: scatter_w_in_rows(gw["w_in_lo"], 0, 0)
            hooks["mm_dh"] = lambda gw: scatter_w_in_rows(gw["w_in_hi"], 0, _w_in_split(d))
        dy, above, gss[l] = _layer_bwd(l, dy, saved[l], w, small, ropes, bl, t, hooks)
    grad_x = dy.reshape(bl, t, d)

    g_big, d_big, m_big, v_big = {}, {}, {}, {}
    for k in BIG:
        g_big[k], d_big[k], m_big[k], v_big[k] = _adamw(recv[k], wl[k], ml[k], vl[k], name=f"adamw_{k}")

    rows = [loss_part]
    for l in range(depth):
        gs = gss[l]
        rows += [gs["attn_norm"].reshape(-1, LANES), gs["mlp_norm"].reshape(-1, LANES), gs["fox_gains"], gs["dil_gains"], gs["b_forget"]]
    row_counts = [r.shape[0] for r in rows]
    part = jnp.concatenate(rows, axis=0)
    pad_rows = -(-part.shape[0] // 8) * 8 - part.shape[0]
    summed = _allreduce_small(jnp.pad(part, ((0, pad_rows), (0, 0))), name="allreduce_small")
    pieces, off = [], 0
    for c in row_counts:
        pieces.append(summed[off:off + c])
        off += c
    loss = pieces[0][0, 0]

    def fold(row):
        return row[:HEAD_DIM] + row[HEAD_DIM:]

    g_small = {k: [] for k in SMALL}
    for l in range(depth):
        an, mn, fg, dg, bf = pieces[1 + 5 * l:6 + 5 * l]
        g_small["attn_norm"].append(an.reshape(d))
        g_small["mlp_norm"].append(mn.reshape(d))
        g_small["q_norm_fox"].append(fold(fg[0]))
        g_small["k_norm_fox"].append(fold(fg[1]))
        g_small["q_norm_dil"].append(fold(dg[0]) + fold(dg[8]) + fold(dg[16]))
        g_small["k_norm_dil"].append(fold(dg[1]) + fold(dg[9]) + fold(dg[17]))
        g_small["b_forget"].append(bf[:, 0])
    g_small = {k: jnp.stack(vs) for k, vs in g_small.items()}
    small_shapes = [small[k].shape for k in SMALL]
    outs = _adamw(_pack([g_small[k] for k in SMALL], F32)[None, None], _pack([small[k] for k in SMALL], F32)[None],
                  _pack([m_small[k] for k in SMALL], F32)[None], _pack([v_small[k] for k in SMALL], F32)[None], name="adamw_small")
    g_sm, d_sm, m_sm, v_sm = (dict(zip(SMALL, _unpack(o, small_shapes))) for o in outs)

    order = ("attn_norm", "w_in", "b_forget", "q_norm_fox", "k_norm_fox", "q_norm_dil", "k_norm_dil", "w_up_fox", "w_up_sb",
             "w_up_dil", "w_out", "mlp_norm", "w_mlp_in", "w_mlp_out")
    res = [loss, grad_x]
    for big, sm in ((g_big, g_sm), (d_big, d_sm), (m_big, m_sm), (v_big, v_sm)):
        res += [big[k] if k in big else sm[k] for k in order]
    return tuple(res)
```

```python
import jax
import jax.numpy as jnp
from jax import lax
from jax.experimental import pallas as pl
from jax.experimental.pallas import tpu as pltpu

F32 = jnp.float32
BF16 = jnp.bfloat16

HEAD_DIM = 64
LANES = 128
N_HEADS_FOX = 8
N_HEADS_SB = 8
N_HEADS_DIL = 4
DIL_PATTERNS = ((128, 1), (512, 4), (2048, 16))
ROPE_THETA = 10000.0
EPS = 1e-6
SCALE = 0.125
W_FOX = N_HEADS_FOX * HEAD_DIM
W_SB = N_HEADS_SB * HEAD_DIM
W_DIL = N_HEADS_DIL * HEAD_DIM
W_DILQ = len(DIL_PATTERNS) * W_DIL
P_FOX = 0
P_SB = 3 * W_FOX
P_DIL = P_SB + 3 * W_SB
P_GATE = P_DIL + 3 * W_DILQ
N_DEV = 8
ATT_BLK = 256
ATT_BQ = 512
NEG = -1e30
VMEM_LIMIT = 56 * 1024 * 1024
ADAMW_BLOCK_ELEMS = 128 * 1024
PROJ_TILE = 2176

ADAM_LR = 0.001
ADAM_B1 = 0.9
ADAM_B2 = 0.999
ADAM_EPS = 1e-08
ADAM_WD = 0.01
ADAM_STEP = 10

NT = (((1,), (1,)), ((), ()))
MESH = pl.DeviceIdType.MESH


def _pcall(body, **kw):
    return pl.pallas_call(body, **kw)


def _params(sem=None):
    return pltpu.CompilerParams(dimension_semantics=sem, vmem_limit_bytes=VMEM_LIMIT)


class _Cargo:
    def __init__(self, ins, out_shape, sems, start, finish, on_done, aliases=None):
        self.ins, self.out_shape, self.sems = list(ins), list(out_shape), list(sems)
        self.start, self.finish, self.on_done, self.aliases = start, finish, on_done, dict(aliases or {})


def _call(body, *, cargo=None, name, grid=(), in_specs, out_specs, out_shape, scratch_shapes=(), compiler_params=None):
    if cargo is None:
        kw = dict(grid=grid) if grid else {}
        if compiler_params is not None:
            kw["compiler_params"] = compiler_params
        return _pcall(body, name=name, in_specs=in_specs, out_specs=out_specs, out_shape=out_shape,
                      scratch_shapes=list(scratch_shapes), **kw)
    single = not isinstance(out_shape, (list, tuple))
    o_specs, o_shape = ([out_specs], [out_shape]) if single else (list(out_specs), list(out_shape))
    n_in, n_out, n_scr = len(in_specs), len(o_shape), len(scratch_shapes)
    c_in, c_out = len(cargo.ins), len(cargo.out_shape)

    def wrapped(*refs):
        ins, cins = refs[:n_in], refs[n_in:n_in + c_in]
        o0 = n_in + c_in
        outs, couts = refs[o0:o0 + n_out], refs[o0 + n_out:o0 + n_out + c_out]
        s0 = o0 + n_out + c_out
        scr, sems = refs[s0:s0 + n_scr], refs[s0 + n_scr:]
        first = last = None
        for ax, size in enumerate(grid):
            pid = pl.program_id(ax)
            first = (pid == 0) if first is None else first & (pid == 0)
            last = (pid == size - 1) if last is None else last & (pid == size - 1)
        if first is None:
            cargo.start(cins, couts, sems)
            body(*ins, *outs, *scr)
            cargo.finish(cins, couts, sems)
            return

        @pl.when(first)
        def _():
            cargo.start(cins, couts, sems)

        body(*ins, *outs, *scr)

        @pl.when(last)
        def _():
            cargo.finish(cins, couts, sems)

    hbm = pl.BlockSpec(memory_space=pl.ANY)
    kw = dict(grid=grid, compiler_params=_params(("arbitrary",) * len(grid))) if grid else {}
    call = _pcall(
        wrapped, name=name, in_specs=list(in_specs) + [hbm] * c_in, out_specs=o_specs + [hbm] * c_out,
        out_shape=o_shape + cargo.out_shape, scratch_shapes=list(scratch_shapes) + cargo.sems,
        input_output_aliases={n_in + i: n_out + j for i, j in cargo.aliases.items()}, **kw)

    def run(*args):
        res = call(*args, *cargo.ins)
        cargo.on_done(list(res[n_out:]))
        return res[0] if single else list(res[:n_out])

    return run


def _tile(dim, target, mult=LANES):
    t = (min(dim, target) // mult) * mult
    while t >= mult:
        if dim % t == 0:
            return t
        t -= mult
    return dim


def _mm(a, b, *, ta=False, tb=False, add=None, relu2=False, relu_grad_of=None, norm_gain=None, out_dtype=F32, name,
        tm=1024, tn=1024, tk=1024, cargo=None):
    m, k = (a.shape[1], a.shape[0]) if ta else a.shape
    n = b.shape[0] if tb else b.shape[1]
    tm, tn, tk = _tile(m, tm), _tile(n, tn), _tile(k, tk)
    nk = k // tk
    dn = (((0,) if ta else (1,), (1,) if tb else (0,)), ((), ()))

    extra = add if add is not None else relu_grad_of
    normed = norm_gain is not None
    assert not (normed and relu2) and (not normed or tn == n)

    def body(*refs):
        a_ref, b_ref = refs[:2]
        x_ref = refs[2] if extra is not None else None
        g_ref = refs[2 + (extra is not None)] if normed else None
        outs = refs[2 + (extra is not None) + normed:-1]
        acc = refs[-1]
        kk = pl.program_id(2)
        part = lax.dot_general(a_ref[...].astype(BF16), b_ref[...].astype(BF16), dn, preferred_element_type=F32)

        def finish(r):
            if add is not None:
                r = r + x_ref[...]
            if relu_grad_of is not None:
                r = r * (2.0 * jnp.maximum(x_ref[...].astype(F32), 0.0))
            outs[0][...] = r.astype(out_dtype)
            if relu2:
                rr = jnp.maximum(r, 0.0)
                outs[1][...] = (rr * rr).astype(BF16)
            if normed:
                inv = lax.rsqrt(jnp.mean(r * r, axis=1, keepdims=True) + EPS)
                outs[1][...] = (r * inv * g_ref[...]).astype(BF16)

        if nk == 1:
            finish(part)
            return

        @pl.when(kk == 0)
        def _():
            acc[...] = part

        @pl.when((kk > 0) & (kk < nk - 1))
        def _():
            acc[...] += part

        @pl.when(kk == nk - 1)
        def _():
            finish(acc[...] + part)

    a_spec = pl.BlockSpec((tk, tm), lambda i, j, q: (q, i)) if ta else pl.BlockSpec((tm, tk), lambda i, j, q: (i, q))
    b_spec = pl.BlockSpec((tn, tk), lambda i, j, q: (j, q)) if tb else pl.BlockSpec((tk, tn), lambda i, j, q: (q, j))
    o_spec = pl.BlockSpec((tm, tn), lambda i, j, q: (i, j))
    ins, specs = [a, b], [a_spec, b_spec]
    if extra is not None:
        ins.append(extra)
        specs.append(o_spec)
    if normed:
        ins.append(norm_gain)
        specs.append(pl.BlockSpec((1, tn), lambda i, j, q: (0, j)))
    two = relu2 or normed
    sds = jax.ShapeDtypeStruct((m, n), out_dtype)
    return _call(
        body, cargo=cargo, name=name, grid=(m // tm, n // tn, nk), in_specs=specs,
        out_specs=[o_spec, o_spec] if two else o_spec,
        out_shape=[sds, jax.ShapeDtypeStruct((m, n), BF16)] if two else sds,
        scratch_shapes=[pltpu.VMEM((tm, tn) if nk > 1 else (8, LANES), F32)],
        compiler_params=_params(("parallel", "parallel", "arbitrary")),
    )(*ins)


def _rmsnorm_fwd(x, g, *, name):
    n, d = x.shape
    tm = _tile(n, 256, 8)

    def body(x_ref, g_ref, h_ref):
        xv = x_ref[...]
        inv = lax.rsqrt(jnp.mean(xv * xv, axis=1, keepdims=True) + EPS)
        h_ref[...] = (xv * inv * g_ref[...]).astype(BF16)

    row = pl.BlockSpec((tm, d), lambda i: (i, 0))
    return _pcall(
        body, name=name, grid=(n // tm,), in_specs=[row, pl.BlockSpec((1, d), lambda i: (0, 0))], out_specs=row,
        out_shape=jax.ShapeDtypeStruct((n, d), BF16), compiler_params=_params(("parallel",)),
    )(x, g)


def _rmsnorm_bwd(x, g, dh, dres, *, name):
    n, d = x.shape
    tm = _tile(n, 256, 8)

    def body(x_ref, g_ref, dh_ref, dres_ref, dx_ref, dg_ref):
        @pl.when(pl.program_id(0) == 0)
        def _():
            dg_ref[...] = jnp.zeros_like(dg_ref)

        xv = x_ref[...]
        inv = lax.rsqrt(jnp.mean(xv * xv, axis=1, keepdims=True) + EPS)
        y = xv * inv
        dhv = dh_ref[...]
        dg_ref[...] += jnp.sum(dhv * y, axis=0, keepdims=True)
        dy = dhv * g_ref[...]
        dx_ref[...] = dres_ref[...] + inv * (dy - y * jnp.mean(dy * y, axis=1, keepdims=True))

    row = pl.BlockSpec((tm, d), lambda i: (i, 0))
    vec = pl.BlockSpec((1, d), lambda i: (0, 0))
    return _pcall(
        body, name=name, grid=(n // tm,), in_specs=[row, vec, row, row], out_specs=[row, vec],
        out_shape=[jax.ShapeDtypeStruct((n, d), F32), jax.ShapeDtypeStruct((1, d), F32)],
        compiler_params=_params(("arbitrary",)),
    )(x, g, dh, dres)


def _gate_specs(n, d):
    bw = 256 if d % 256 == 0 else LANES
    tm = _tile(n, 512, 8)
    nb = d // bw
    yspec = pl.BlockSpec((tm, bw), lambda i, j: (i, j))
    gspecs = [pl.BlockSpec((tm, bw), lambda i, j, b=b: (i, P_GATE // bw + b * nb + j)) for b in range(3)]
    return tm, bw, nb, yspec, gspecs


def _gate_merge_fwd(proj, ys, *, name):
    n, d = ys[0].shape
    tm, bw, nb, yspec, gspecs = _gate_specs(n, d)

    def body(g0, g1, g2, y0, y1, y2, o_ref):
        acc = jax.nn.sigmoid(g0[...]) * y0[...]
        acc += jax.nn.sigmoid(g1[...]) * y1[...]
        acc += jax.nn.sigmoid(g2[...]) * y2[...]
        o_ref[...] = acc.astype(BF16)

    return _pcall(
        body, name=name, grid=(n // tm, nb), in_specs=gspecs + [yspec] * 3, out_specs=yspec,
        out_shape=jax.ShapeDtypeStruct((n, d), BF16), compiler_params=_params(("parallel", "parallel")),
    )(proj, proj, proj, *ys)


def _gate_merge_bwd(proj, ys, dmerged, *, name):
    n, d = ys[0].shape
    tm, bw, nb, yspec, gspecs = _gate_specs(n, d)

    def body(g0, g1, g2, y0, y1, y2, dm_ref, dy0, dy1, dy2, dgl0, dgl1, dgl2):
        dm = dm_ref[...]
        for g_ref, y_ref, dy_ref, dgl_ref in ((g0, y0, dy0, dgl0), (g1, y1, dy1, dgl1), (g2, y2, dy2, dgl2)):
            s = jax.nn.sigmoid(g_ref[...])
            dy_ref[...] = (dm * s).astype(BF16)
            dgl_ref[...] = (dm * y_ref[...] * s * (1.0 - s)).astype(BF16)

    sds = jax.ShapeDtypeStruct((n, d), BF16)
    return _pcall(
        body, name=name, grid=(n // tm, nb), in_specs=gspecs + [yspec] * 4, out_specs=[yspec] * 6,
        out_shape=[sds] * 6, compiler_params=_params(("parallel", "parallel")),
    )(proj, proj, proj, *ys, dmerged)


def _loss_head(y, tgt, *, name):
    n, d = y.shape
    tm = _tile(n, 256, 8)
    steps = n // tm

    def body(y_ref, t_ref, dy_ref, loss_ref, acc):
        i = pl.program_id(0)

        @pl.when(i == 0)
        def _():
            acc[...] = jnp.zeros_like(acc)

        e = y_ref[...] - t_ref[...]
        dy_ref[...] = e * (1.0 / d)
        acc[...] += jnp.sum(e * e, axis=0, keepdims=True)

        @pl.when(i == steps - 1)
        def _():
            tot = jnp.sum(acc[...], axis=1, keepdims=True) * (0.5 / d)
            loss_ref[...] = jnp.broadcast_to(tot, loss_ref.shape)

    row = pl.BlockSpec((tm, d), lambda i: (i, 0))
    return _pcall(
        body, name=name, grid=(steps,), in_specs=[row, row], out_specs=[row, pl.BlockSpec((8, LANES), lambda i: (0, 0))],
        out_shape=[jax.ShapeDtypeStruct((n, d), F32), jax.ShapeDtypeStruct((8, LANES), F32)],
        scratch_shapes=[pltpu.VMEM((1, d), F32)], compiler_params=_params(("arbitrary",)),
    )(y, tgt)


def _assemble_cols(parts, width, *, name):
    n = parts[0].shape[0]
    tm = _tile(n, 256, 16)
    widths = [p.shape[1] for p in parts]

    def body(*refs):
        o_ref = refs[-1]
        off = 0
        for ref, w in zip(refs[:-1], widths):
            o_ref[:, off:off + w] = ref[...].astype(BF16)
            off += w
        if off < width:
            o_ref[:, off:] = jnp.zeros((tm, width - off), BF16)

    return _pcall(
        body, name=name, grid=(n // tm,), in_specs=[pl.BlockSpec((tm, w), lambda i: (i, 0)) for w in widths],
        out_specs=pl.BlockSpec((tm, width), lambda i: (i, 0)), out_shape=jax.ShapeDtypeStruct((n, width), BF16),
        compiler_params=_params(("parallel",)),
    )(*parts)


def _dil_weights(lses, *, name):
    shp = lses[0].shape

    def body(l0, l1, l2, lse_ref, w0, w1, w2):
        a, b, c = l0[...], l1[...], l2[...]
        m = jnp.maximum(jnp.maximum(a, b), c)
        ea, eb, ec = jnp.exp(a - m), jnp.exp(b - m), jnp.exp(c - m)
        den = ea + eb + ec
        lse_ref[...] = m + jnp.log(den)
        w0[...] = ea / den
        w1[...] = eb / den
        w2[...] = ec / den

    vmem = pl.BlockSpec(memory_space=pltpu.VMEM)
    return _pcall(body, name=name, in_specs=[vmem] * 3, out_specs=[vmem] * 4, out_shape=[jax.ShapeDtypeStruct(shp, F32)] * 4)(*lses)


def _dil_mix(os_, ws, *, name):
    n, w = os_[0].shape
    tm = _tile(n, 512, 8)

    def body(o0, o1, o2, w0, w1, w2, out_ref):
        out_ref[...] = w0[...] * o0[...] + w1[...] * o1[...] + w2[...] * o2[...]

    spec = pl.BlockSpec((tm, w), lambda i: (i, 0))
    return _pcall(
        body, name=name, grid=(n // tm,), in_specs=[spec] * 6, out_specs=spec, out_shape=jax.ShapeDtypeStruct((n, w), F32),
        compiler_params=_params(("parallel",)),
    )(*os_, *ws)


def _adamw(gsrc, w, m, v, *, name):
    s, dep, a, b = gsrc.shape
    ta = _tile(a, max(16, (ADAMW_BLOCK_ELEMS // b) // 16 * 16), 16)
    c1 = 1.0 / (1.0 - ADAM_B1 ** ADAM_STEP)
    c2 = 1.0 / (1.0 - ADAM_B2 ** ADAM_STEP)

    def body(gs_ref, w_ref, m_ref, v_ref, g_ref, d_ref, m2_ref, v2_ref):
        g = gs_ref[0].astype(F32)
        for i in range(1, s):
            g = g + gs_ref[i].astype(F32)
        m2 = ADAM_B1 * m_ref[...] + (1.0 - ADAM_B1) * g
        v2 = ADAM_B2 * v_ref[...] + (1.0 - ADAM_B2) * (g * g)
        g_ref[...] = g
        m2_ref[...] = m2
        v2_ref[...] = v2
        d_ref[...] = -ADAM_LR * ((m2 * c1) / (jnp.sqrt(v2 * c2) + ADAM_EPS) + ADAM_WD * w_ref[...])

    spec = pl.BlockSpec((None, ta, b), lambda l, i: (l, i, 0))
    sds = jax.ShapeDtypeStruct((dep, a, b), F32)
    return _pcall(
        body, name=name, grid=(dep, a // ta),
        in_specs=[pl.BlockSpec((s, None, ta, b), lambda l, i: (0, l, i, 0)), spec, spec, spec],
        out_specs=[spec] * 4, out_shape=[sds] * 4, compiler_params=_params(("parallel", "parallel")),
    )(gsrc, w, m, v)


def _mask_a():
    return lax.broadcasted_iota(jnp.int32, (1, LANES), 1) < HEAD_DIM


def _half_sum(x, m_a):
    sa = jnp.sum(jnp.where(m_a, x, 0.0), axis=1, keepdims=True)
    sb = jnp.sum(jnp.where(m_a, 0.0, x), axis=1, keepdims=True)
    return jnp.where(m_a, sa, sb)


def _head_inv(x, m_a):
    return lax.rsqrt(_half_sum(x * x, m_a) * (1.0 / HEAD_DIM) + EPS)


def _swap32(x):
    first = (lax.broadcasted_iota(jnp.int32, (1, LANES), 1) % HEAD_DIM) < (HEAD_DIM // 2)
    return jnp.where(first, pltpu.roll(x, LANES - HEAD_DIM // 2, 1), pltpu.roll(x, HEAD_DIM // 2, 1))


def _tri(blk, rel):
    r = lax.broadcasted_iota(jnp.int32, (blk, blk), 0)
    c = lax.broadcasted_iota(jnp.int32, (blk, blk), 1)
    return jnp.where(rel(r, c), 1.0, 0.0).astype(BF16)


def _cumdot(x, u, parts):
    acc = None
    r = x
    for i in range(parts):
        xi = r.astype(BF16)
        t = jnp.dot(xi, u, preferred_element_type=F32)
        acc = t if acc is None else acc + t
        if i + 1 < parts:
            r = r - xi.astype(F32)
    return acc


def _rows(i, blk):
    return pl.ds(pl.multiple_of(i * blk, blk), blk)


def _att_blk(n):
    return ATT_BLK if n % ATT_BLK == 0 else min(LANES, n)


def _att_blocks(n, wide_keys=False):
    bk = _att_blk(n)
    bq = ATT_BQ if n % ATT_BQ == 0 else bk
    return bq, (bq if wide_keys else bk)


def _loop(lo, hi, fn):
    def it(i, c):
        fn(i)
        return c

    lax.fori_loop(lo, hi, it, 0)


def _normed(src, g_ref, rope_refs, rows, m_a):
    xv = src[rows, :]
    xn = xv * _head_inv(xv, m_a) * g_ref[...]
    if rope_refs is not None:
        xn = xn * rope_refs[0][rows, :] + _swap32(xn) * rope_refs[1][rows, :]
    return xn


def _bias_lane(h):
    return HEAD_DIM if h == 0 else 0


def _k_for_head(kn, kb_row, h, m_h, lane, blk):
    out = jnp.where(m_h, kn, 0.0)
    if kb_row is not None:
        col = jnp.transpose(jnp.broadcast_to(kb_row, (LANES, blk)))
        hi = col.astype(BF16).astype(F32)
        mid = (col - hi).astype(BF16).astype(F32)
        lo = col - hi - mid
        b = _bias_lane(h)
        out = jnp.where(lane == b, hi, jnp.where(lane == b + 1, mid, jnp.where(lane == b + 2, lo, out)))
    return out.astype(BF16)


def _q_for_head(qb, h, m_h, lane, biased):
    out = jnp.where(m_h, qb, 0)
    if biased:
        b = _bias_lane(h)
        out = jnp.where((lane >= b) & (lane < b + 3), jnp.ones_like(out), out)
    return out


def _head_rows(x, parts=3):
    rr = lax.broadcasted_iota(jnp.int32, (8, LANES), 0)
    ll = lax.broadcasted_iota(jnp.int32, (8, LANES), 1)
    sel = jnp.where(((rr == 0) & (ll < HEAD_DIM)) | ((rr == 1) & (ll >= HEAD_DIM)), 1.0, 0.0).astype(BF16)
    acc = None
    rem = x
    for i in range(parts):
        xi = rem.astype(BF16)
        t = lax.dot_general(sel, xi, NT, preferred_element_type=F32)
        acc = t if acc is None else acc + t
        if i + 1 < parts:
            rem = rem - xi.astype(F32)
    return acc


def _cumdot_left(u, x, parts):
    acc = None
    rem = x
    for i in range(parts):
        xi = rem.astype(BF16)
        t = jnp.dot(u, xi, preferred_element_type=F32)
        acc = t if acc is None else acc + t
        if i + 1 < parts:
            rem = rem - xi.astype(F32)
    return acc


def _q_minus_k(bk, bq):
    return lax.broadcasted_iota(jnp.int32, (bk, bq), 1) - lax.broadcasted_iota(jnp.int32, (bk, bq), 0)


def _stream_rows(stride):
    if stride == 1:
        return _rows
    c = pl.program_id(2)
    return lambda i, blk: pl.ds(c + i * (blk * stride), blk, stride=stride)


def _attn_specs(t, pairs, stride, nq, bq):
    col = lambda off: pl.BlockSpec((None, t, LANES), lambda z, p, c: (z, 0, off + p * pairs))
    vec = pl.BlockSpec((1, LANES), lambda z, p, c: (0, 0))
    seq = pl.BlockSpec((None, t, LANES), lambda z, p, c: (z, 0, 0))
    stat = pl.BlockSpec((None, 2 * pairs, nq, 1, bq), lambda z, p, c: (z * stride + c, p, 0, 0, 0))
    return col, vec, seq, stat


def _attn_fwd(src, offs, npairs, gq, gk, *, rope=None, kbias=None, window, stride=1, pairs=1, name, cargo=None):
    bs, t, _ = src.shape
    n = t // stride
    full = window >= n
    bq, bk = _att_blocks(n, wide_keys=full)
    nq, nk, rq = n // bq, n // bk, bq // bk
    wblk = -(-window // bk)
    biased = kbias is not None
    heads = range(2 * pairs)

    def body(*refs):
        it = iter(refs)
        q_refs, k_refs, v_refs = ([next(it) for _ in range(pairs)] for _ in range(3))
        gq_ref, gk_ref = next(it), next(it)
        rope_refs = (next(it), next(it)) if rope is not None else None
        kb_ref = next(it) if biased else None
        o_refs = [next(it) for _ in range(pairs)]
        lse_ref, qn_s, kh_s, vt_s, acc_s, m_s = (next(it) for _ in range(6))
        m_a = _mask_a()
        masks = (m_a, jnp.logical_not(m_a))
        lane = lax.broadcasted_iota(jnp.int32, (1, LANES), 1)
        row = lax.broadcasted_iota(jnp.int32, (LANES, 1), 0)
        tok = _stream_rows(stride)

        def prep(c):
            rows = _rows(c, bk)
            trows = tok(c, bk)
            for pp in range(pairs):
                qn = _normed(q_refs[pp], gq_ref, rope_refs, trows, m_a)
                qn_s[pp, rows, :] = (qn * SCALE).astype(BF16)
                kn = _normed(k_refs[pp], gk_ref, rope_refs, trows, m_a)
                vt = jnp.transpose(v_refs[pp][trows, :])
                for half in (0, 1):
                    h = 2 * pp + half
                    kh_s[h, rows, :] = _k_for_head(kn, kb_ref[h, c] if biased else None, half, masks[half], lane, bk)
                    vt_s[h, c] = jnp.where(row == _bias_lane(half), 1.0, vt).astype(BF16)

        _loop(0, nk, prep)
        qk = _q_minus_k(bk, bq)

        def qblock(qi):
            rows = _rows(qi, bq)
            qh = [_q_for_head(qn_s[h // 2, rows, :], h % 2, masks[h % 2], lane, biased) for h in heads]
            m_s[...] = jnp.full(m_s.shape, NEG, F32)
            acc_s[...] = jnp.zeros_like(acc_s)

            def step(kj, masked):
                cols = _rows(kj, bk)
                sts = [lax.dot_general(kh_s[h, cols, :], qh[h], NT, preferred_element_type=F32) for h in heads]
                old = [(m_s[h], acc_s[h]) for h in heads]
                if masked:
                    d = qk + (qi * bq - kj * bk)
                    ok = (d >= 0) & (d <= window)
                    sts = [jnp.where(ok, st, NEG) for st in sts]
                new = []
                for h in heads:
                    m, acc = old[h]
                    m2 = jnp.maximum(m, jnp.max(sts[h], axis=0, keepdims=True))
                    pt = jnp.exp(sts[h] - m2).astype(BF16)
                    new.append((m2, jnp.exp(m - m2) * acc + jnp.dot(vt_s[h, kj], pt, preferred_element_type=F32)))
                for h in heads:
                    m_s[h], acc_s[h] = new[h]

            if full:
                _loop(0, qi * rq, lambda kj: step(kj, False))
                _loop(qi * rq, (qi + 1) * rq, lambda kj: step(kj, True))
            else:
                _loop(jnp.maximum(qi * rq - wblk, 0), (qi + 1) * rq, lambda kj: step(kj, True))
            outs = []
            for h in heads:
                acc_t = acc_s[h]
                den = acc_t[_bias_lane(h % 2):_bias_lane(h % 2) + 1, :]
                outs.append(jnp.transpose(acc_t / den))
                lse_ref[h, qi] = m_s[h] + jnp.log(den)
            for pp in range(pairs):
                o_refs[pp][tok(qi, bq), :] = jnp.where(m_a, outs[2 * pp], outs[2 * pp + 1])

        _loop(0, nq, qblock)

    col, vec, seq, stat = _attn_specs(t, pairs, stride, nq, bq)
    ins = [src] * (3 * pairs) + [gq, gk]
    specs = [col(off + pp) for off in offs for pp in range(pairs)] + [vec, vec]
    if rope is not None:
        ins += list(rope)
        specs += [seq, seq]
    if biased:
        ins.append(kbias)
        specs.append(pl.BlockSpec((None, 2, nk, 1, bk), lambda z, p, c: (z, p, 0, 0, 0)))
    scratch = [pltpu.VMEM((pairs, n, LANES), BF16), pltpu.VMEM((2 * pairs, n, LANES), BF16)]
    scratch += [pltpu.VMEM((2 * pairs, nk, LANES, bk), BF16), pltpu.VMEM((2 * pairs, LANES, bq), F32)]
    scratch += [pltpu.VMEM((2 * pairs, 1, bq), F32)]
    assert pairs in (1, npairs)
    ospec = pl.BlockSpec((None, t, LANES), lambda z, p, c: (z, 0, p))
    *os_, lse = _call(
        body, cargo=cargo, name=name, grid=(bs, npairs // pairs, stride), in_specs=specs, out_specs=[ospec] * pairs + [stat],
        out_shape=[jax.ShapeDtypeStruct((bs, t, LANES * npairs // pairs), F32)] * pairs
        + [jax.ShapeDtypeStruct((bs * stride, 2 * npairs, nq, 1, bq), F32)],
        scratch_shapes=scratch, compiler_params=_params(("parallel", "parallel", "arbitrary")),
    )(*ins)
    return (os_[0] if pairs == 1 else jnp.concatenate(os_, axis=-1)), lse


def _attn_bwd(src, offs, npairs, gq, gk, o, do, lse, *, rope=None, kbias=None, window, stride=1, pairs=1, name, cargo=None):
    bs, t, _ = src.shape
    n = t // stride
    full = window >= n
    bq, bk = _att_blocks(n, wide_keys=full)
    nq, nk, rq = n // bq, n // bk, bq // bk
    wblk = -(-window // bk)
    biased = kbias is not None
    heads = range(2 * pairs)
    gdt = BF16 if stride == 1 else F32

    def body(*refs):
        it = iter(refs)
        q_refs, k_refs, v_refs = ([next(it) for _ in range(pairs)] for _ in range(3))
        gq_ref, gk_ref = next(it), next(it)
        o_refs, do_refs = ([next(it) for _ in range(pairs)] for _ in range(2))
        lse_ref = next(it)
        rope_refs = (next(it), next(it)) if rope is not None else None
        kb_ref = next(it) if biased else None
        dq_refs, dk_refs, dv_refs = ([next(it) for _ in range(pairs)] for _ in range(3))
        dg_ref = next(it)
        dkb_ref = next(it) if biased else None
        qn_s, kh_s, vb_s, kt_s, dqn_s, dkh_s, dv_s, dq_s, rs_s = (next(it) for _ in range(9))
        m_a = _mask_a()
        masks = (m_a, jnp.logical_not(m_a))
        lane = lax.broadcasted_iota(jnp.int32, (1, LANES), 1)
        tok = _stream_rows(stride)

        @pl.when((pl.program_id(0) == 0) & (pl.program_id(1) == 0) & (pl.program_id(2) == 0))
        def _():
            dg_ref[...] = jnp.zeros_like(dg_ref)

        def prep(c):
            rows = _rows(c, bk)
            trows = tok(c, bk)
            for pp in range(pairs):
                qn = _normed(q_refs[pp], gq_ref, rope_refs, trows, m_a)
                qn_s[pp, rows, :] = (qn * SCALE).astype(BF16)
                kn = _normed(k_refs[pp], gk_ref, rope_refs, trows, m_a)
                kt_s[pp, c] = jnp.transpose(kn).astype(BF16)
                vb_s[pp, rows, :] = v_refs[pp][trows, :].astype(BF16)
                for half in (0, 1):
                    h = 2 * pp + half
                    kh_s[h, rows, :] = _k_for_head(kn, kb_ref[h, c] if biased else None, half, masks[half], lane, bk)

        _loop(0, nk, prep)
        dkh_s[...] = jnp.zeros_like(dkh_s)
        dv_s[...] = jnp.zeros_like(dv_s)
        qk = _q_minus_k(bk, bq)

        def qblock(qi):
            rows = _rows(qi, bq)
            trows = tok(qi, bq)
            dobs = [do_refs[pp][trows, :] for pp in range(pairs)]
            deltas = [_head_rows(dobs[pp] * o_refs[pp][trows, :]) for pp in range(pairs)]
            qh = [_q_for_head(qn_s[h // 2, rows, :], h % 2, masks[h % 2], lane, biased) for h in heads]
            doms = [jnp.where(masks[h % 2], dobs[h // 2], 0.0).astype(BF16) for h in heads]
            delta = [deltas[h // 2][h % 2:h % 2 + 1, :] for h in heads]
            lses = [lse_ref[h, qi] for h in heads]
            dq_s[...] = jnp.zeros_like(dq_s)
            if biased:
                for h in heads:
                    rs_s[h, qi] = jnp.zeros((1, bq), F32)

            def step(kj, masked):
                cols = _rows(kj, bk)
                vbs = [vb_s[pp, cols, :] for pp in range(pairs)]
                kts = [kt_s[pp, kj] for pp in range(pairs)]
                sts = [lax.dot_general(kh_s[h, cols, :], qh[h], NT, preferred_element_type=F32) for h in heads]
                dpts = [lax.dot_general(vbs[h // 2], doms[h], NT, preferred_element_type=F32) for h in heads]
                if masked:
                    d = qk + (qi * bq - kj * bk)
                    ok = (d >= 0) & (d <= window)
                    sts = [jnp.where(ok, st, NEG) for st in sts]
                new = []
                for h in heads:
                    pt = jnp.exp(sts[h] - lses[h])
                    dst = pt * (dpts[h] - delta[h])
                    dsb = dst.astype(BF16)
                    tk = jnp.dot(dsb, qh[h], preferred_element_type=F32)
                    if biased:
                        tk = tk + jnp.dot((dst - dsb.astype(F32)).astype(BF16), qh[h], preferred_element_type=F32)
                    tv = jnp.dot(pt.astype(BF16), doms[h], preferred_element_type=F32)
                    tq = jnp.dot(kts[h // 2], dsb, preferred_element_type=F32)
                    new.append((tk, tv, tq, jnp.sum(dst, axis=0, keepdims=True) if biased else None))
                for h in heads:
                    dkh_s[h, cols, :] += new[h][0]
                    dq_s[h] += new[h][2]
                    if biased:
                        rs_s[h, qi] += new[h][3]
                for pp in range(pairs):
                    dv_s[pp, cols, :] += new[2 * pp][1] + new[2 * pp + 1][1]

            if full:
                _loop(0, qi * rq, lambda kj: step(kj, False))
                _loop(qi * rq, (qi + 1) * rq, lambda kj: step(kj, True))
            else:
                _loop(jnp.maximum(qi * rq - wblk, 0), (qi + 1) * rq, lambda kj: step(kj, True))
            for pp in range(pairs):
                dqn_s[pp, rows, :] = jnp.where(m_a, jnp.transpose(dq_s[2 * pp]), jnp.transpose(dq_s[2 * pp + 1])) * SCALE

        _loop(0, nq, qblock)

        def finish(c, carry):
            rows = _rows(c, bq)
            trows = tok(c, bq)
            dgq, dgk = carry
            for pp in range(pairs):
                dk_pair = [dkh_s[2 * pp, rows, :], dkh_s[2 * pp + 1, rows, :]]
                if biased:
                    for half in (0, 1):
                        b = _bias_lane(half)
                        dkb_row = jnp.transpose(dk_pair[half])[b:b + 1, :] - rs_s[2 * pp + half, c]
                        for j in range(rq):
                            dkb_ref[2 * pp + half, c * rq + j] = dkb_row[:, j * bk:(j + 1) * bk]
                dv_refs[pp][trows, :] = dv_s[pp, rows, :].astype(gdt)
                grads = (dqn_s[pp, rows, :], jnp.where(m_a, dk_pair[0], dk_pair[1]))
                out = []
                for src_ref, g_ref, dxn, dst in ((q_refs[pp], gq_ref, grads[0], dq_refs[pp]),
                                                 (k_refs[pp], gk_ref, grads[1], dk_refs[pp])):
                    xv = src_ref[trows, :]
                    inv = _head_inv(xv, m_a)
                    y = xv * inv
                    if rope_refs is not None:
                        dxn = dxn * rope_refs[0][trows, :] + _swap32(dxn * rope_refs[1][trows, :])
                    dy = dxn * g_ref[...]
                    dst[trows, :] = (inv * (dy - y * (_half_sum(dy * y, m_a) * (1.0 / HEAD_DIM)))).astype(gdt)
                    out.append(jnp.sum(dxn * y, axis=0, keepdims=True))
                dgq, dgk = dgq + out[0], dgk + out[1]
            return dgq, dgk

        zero = jnp.zeros((1, LANES), F32)
        dgq, dgk = lax.fori_loop(0, nq, finish, (zero, zero))
        dg_ref[0:1, :] += dgq
        dg_ref[1:2, :] += dgk

    assert pairs in (1, npairs)
    col, vec, seq, stat = _attn_specs(t, pairs, stride, nq, bq)
    ins = [src] * (3 * pairs) + [gq, gk] + [o] * pairs + [do] * pairs + [lse]
    specs = [col(off + pp) for off in offs for pp in range(pairs)] + [vec, vec] + [col(pp) for pp in range(pairs)] * 2 + [stat]
    if rope is not None:
        ins += list(rope)
        specs += [seq, seq]
    sds = jax.ShapeDtypeStruct((bs, t, LANES * npairs // pairs), gdt)
    out_shape = [sds] * (3 * pairs) + [jax.ShapeDtypeStruct((8, LANES), F32)]
    ospec = pl.BlockSpec((None, t, LANES), lambda z, p, c: (z, 0, p))
    out_specs = [ospec] * (3 * pairs) + [pl.BlockSpec((8, LANES), lambda z, p, c: (0, 0))]
    if biased:
        kbspec = pl.BlockSpec((None, 2, nk, 1, bk), lambda z, p, c: (z, p, 0, 0, 0))
        ins.append(kbias)
        specs.append(kbspec)
        out_shape.append(jax.ShapeDtypeStruct(kbias.shape, F32))
        out_specs.append(kbspec)
    scratch = [pltpu.VMEM((pairs, n, LANES), BF16), pltpu.VMEM((2 * pairs, n, LANES), BF16), pltpu.VMEM((pairs, n, LANES), BF16)]
    scratch += [pltpu.VMEM((pairs, nk, LANES, bk), BF16), pltpu.VMEM((pairs, n, LANES), F32)]
    scratch += [pltpu.VMEM((2 * pairs, n, LANES), F32), pltpu.VMEM((pairs, n, LANES), F32)]
    scratch += [pltpu.VMEM((2 * pairs, LANES, bq), F32), pltpu.VMEM((2 * pairs, nq, 1, bq), F32)]
    res = _call(
        body, cargo=cargo, name=name, grid=(bs, npairs // pairs, stride), in_specs=specs, out_specs=out_specs, out_shape=out_shape,
        scratch_shapes=scratch, compiler_params=_params(("arbitrary", "arbitrary", "arbitrary")),
    )(*ins)
    return (list(res[:pairs]), list(res[pairs:2 * pairs]), list(res[2 * pairs:3 * pairs]), *res[3 * pairs:])


SB_LOG_PARTS = 2
SB_GRAD_PARTS = 1


def _log_sig_pair(z):
    lsn = jnp.minimum(-z, 0.0) - jnp.log(1.0 + jnp.exp(-jnp.abs(z)))
    return lsn, z + lsn


def _sb_specs(n, pairs, nq, bq):
    col = lambda off: pl.BlockSpec((None, n, LANES * pairs), lambda z, p: (z, 0, off // pairs + p))
    stat = pl.BlockSpec((None, 2 * pairs, nq, 1, bq), lambda z, p: (z, p, 0, 0, 0))
    return col, stat


def _sb_pairs(npairs):
    return 2 if npairs % 2 == 0 else 1


def _sb_fwd(src, offs, npairs, *, name, cargo=None):
    zs, n, _ = src.shape
    bq, bk = _att_blocks(n)
    nq, nk, rq = n // bq, n // bk, bq // bk
    pairs = _sb_pairs(npairs)
    heads = range(2 * pairs)
    lanes = [slice(pp * LANES, (pp + 1) * LANES) for pp in range(pairs)]

    def body(q_ref, k_ref, v_ref, o_ref, lt_ref, qs_s, kb_s, vt_s, acc_s, c_s):
        m_a = _mask_a()
        masks = (m_a, jnp.logical_not(m_a))

        def prep(c):
            rows = _rows(c, bk)
            for pp in range(pairs):
                qs_s[pp, rows, :] = (q_ref[rows, lanes[pp]] * SCALE).astype(BF16)
                kb_s[pp, rows, :] = k_ref[rows, lanes[pp]].astype(BF16)
                vt_s[pp, c] = jnp.transpose(v_ref[rows, lanes[pp]]).astype(BF16)

        _loop(0, nk, prep)
        qk = _q_minus_k(bk, bq)
        u_gt = _tri(bk, lambda r, c: c > r)

        def qblock(qi):
            rows = _rows(qi, bq)
            qms = [jnp.where(masks[h % 2], qs_s[h // 2, rows, :], 0) for h in heads]
            acc_s[...] = jnp.zeros_like(acc_s)
            c_s[...] = jnp.zeros_like(c_s)

            def step(kj, masked, q0=0):
                cols = _rows(kj, bk)
                zts = [lax.dot_general(kb_s[h // 2, cols, :], qms[h][q0:, :], NT, preferred_element_type=F32) for h in heads]
                old = [c_s[h, :, q0:] for h in heads]
                if masked:
                    ok = (qk[:, q0:] + (qi * bq - kj * bk)) > 0
                new = []
                for h in heads:
                    lsn, lsp = _log_sig_pair(zts[h])
                    if masked:
                        lsn = jnp.where(ok, lsn, 0.0)
                    at = jnp.exp(lsp + (old[h] + _cumdot_left(u_gt, lsn, SB_LOG_PARTS)))
                    if masked:
                        at = jnp.where(ok, at, 0.0)
                    new.append((jnp.dot(vt_s[h // 2, kj], at.astype(BF16), preferred_element_type=F32),
                                old[h] + jnp.sum(lsn, axis=0, keepdims=True)))
                for h in heads:
                    acc_s[h, :, q0:] += new[h][0]
                    c_s[h, :, q0:] = new[h][1]

            for j in reversed(range(rq)):
                step(qi * rq + j, True, j * bk)
            _loop(0, qi * rq, lambda t: step(qi * rq - 1 - t, False))
            for pp in range(pairs):
                o_ref[rows, lanes[pp]] = jnp.where(m_a, jnp.transpose(acc_s[2 * pp]), jnp.transpose(acc_s[2 * pp + 1]))
            for h in heads:
                lt_ref[h, qi] = c_s[h]

        _loop(0, nq, qblock)

    col, stat = _sb_specs(n, pairs, nq, bq)
    scratch = [pltpu.VMEM((pairs, n, LANES), BF16)] * 2 + [pltpu.VMEM((pairs, nk, LANES, bk), BF16)]
    scratch += [pltpu.VMEM((2 * pairs, LANES, bq), F32), pltpu.VMEM((2 * pairs, 1, bq), F32)]
    return _call(
        body, cargo=cargo, name=name, grid=(zs, npairs // pairs),
        in_specs=[col(offs[0]), col(offs[1]), col(offs[2])], out_specs=[col(0), stat],
        out_shape=[jax.ShapeDtypeStruct((zs, n, LANES * npairs), F32), jax.ShapeDtypeStruct((zs, 2 * npairs, nq, 1, bq), F32)],
        scratch_shapes=scratch, compiler_params=_params(("parallel", "parallel")),
    )(src, src, src)


def _sb_bwd(src, offs, npairs, do, ltot, *, name, cargo=None):
    zs, n, _ = src.shape
    bq, bk = _att_blocks(n)
    nq, nk, rq = n // bq, n // bk, bq // bk
    pairs = _sb_pairs(npairs)
    heads = range(2 * pairs)
    lanes = [slice(pp * LANES, (pp + 1) * LANES) for pp in range(pairs)]

    def body(q_ref, k_ref, v_ref, do_ref, lt_ref, dq_ref, dk_ref, dv_ref, qs_s, kb_s, vb_s, kt_s, dk_s, dv_s, dq_s, lp_s, ep_s):
        m_a = _mask_a()
        masks = (m_a, jnp.logical_not(m_a))

        def prep(c):
            rows = _rows(c, bk)
            for pp in range(pairs):
                qs_s[pp, rows, :] = (q_ref[rows, lanes[pp]] * SCALE).astype(BF16)
                kv = k_ref[rows, lanes[pp]]
                kb_s[pp, rows, :] = kv.astype(BF16)
                kt_s[pp, c] = jnp.transpose(kv).astype(BF16)
                vb_s[pp, rows, :] = v_ref[rows, lanes[pp]].astype(BF16)

        _loop(0, nk, prep)
        dk_s[...] = jnp.zeros_like(dk_s)
        dv_s[...] = jnp.zeros_like(dv_s)
        qk = _q_minus_k(bk, bq)
        u_le = _tri(bk, lambda r, c: c <= r)
        u_lt = _tri(bk, lambda r, c: c < r)

        def qblock(qi):
            rows = _rows(qi, bq)
            qms = [jnp.where(masks[h % 2], qs_s[h // 2, rows, :], 0) for h in heads]
            doms = [jnp.where(masks[h % 2], do_ref[rows, lanes[h // 2]], 0.0).astype(BF16) for h in heads]
            lts = [lt_ref[h, qi] for h in heads]
            dq_s[...] = jnp.zeros_like(dq_s)
            lp_s[...] = jnp.zeros_like(lp_s)
            ep_s[...] = jnp.zeros_like(ep_s)

            def step(kj, masked, q0=0):
                cols = _rows(kj, bk)
                kbs = [kb_s[pp, cols, :] for pp in range(pairs)]
                kts = [kt_s[pp, kj] for pp in range(pairs)]
                qs, ds = [q[q0:, :] for q in qms], [dm[q0:, :] for dm in doms]
                zts = [lax.dot_general(kbs[h // 2], qs[h], NT, preferred_element_type=F32) for h in heads]
                dats = [lax.dot_general(vb_s[h // 2, cols, :], ds[h], NT, preferred_element_type=F32) for h in heads]
                old = [(lp_s[h, :, q0:], ep_s[h, :, q0:]) for h in heads]
                if masked:
                    ok = (qk[:, q0:] + (qi * bq - kj * bk)) > 0
                new = []
                for h in heads:
                    lp, ep = old[h]
                    lsn, lsp = _log_sig_pair(zts[h])
                    sig = jnp.exp(lsp)
                    if masked:
                        lsn = jnp.where(ok, lsn, 0.0)
                    at = jnp.exp(lsp + (lts[h][:, q0:] - (lp + _cumdot_left(u_le, lsn, SB_LOG_PARTS))))
                    if masked:
                        at = jnp.where(ok, at, 0.0)
                    et = dats[h] * at
                    big_e = ep + _cumdot_left(u_lt, et, SB_GRAD_PARTS)
                    dzt = et - sig * (et + big_e)
                    if masked:
                        dzt = jnp.where(ok, dzt, 0.0)
                    dzb = dzt.astype(BF16)
                    new.append((jnp.dot(dzb, qs[h], preferred_element_type=F32),
                                jnp.dot(at.astype(BF16), ds[h], preferred_element_type=F32),
                                jnp.dot(kts[h // 2], dzb, preferred_element_type=F32),
                                lp + jnp.sum(lsn, axis=0, keepdims=True), ep + jnp.sum(et, axis=0, keepdims=True)))
                for h in heads:
                    dq_s[h, :, q0:] += new[h][2]
                    lp_s[h, :, q0:], ep_s[h, :, q0:] = new[h][3], new[h][4]
                for pp in range(pairs):
                    dk_s[pp, cols, :] += new[2 * pp][0] + new[2 * pp + 1][0]
                    dv_s[pp, cols, :] += new[2 * pp][1] + new[2 * pp + 1][1]

            _loop(0, qi * rq, lambda kj: step(kj, False))
            for j in range(rq):
                step(qi * rq + j, True, j * bk)
            for pp in range(pairs):
                dq = jnp.where(m_a, jnp.transpose(dq_s[2 * pp]), jnp.transpose(dq_s[2 * pp + 1]))
                dq_ref[rows, lanes[pp]] = (dq * SCALE).astype(BF16)

        _loop(0, nq, qblock)

        def store(c):
            rows = _rows(c, bk)
            for pp in range(pairs):
                dk_ref[rows, lanes[pp]] = dk_s[pp, rows, :].astype(BF16)
                dv_ref[rows, lanes[pp]] = dv_s[pp, rows, :].astype(BF16)

        _loop(0, nk, store)

    col, stat = _sb_specs(n, pairs, nq, bq)
    ospec = col(0)
    sds = jax.ShapeDtypeStruct((zs, n, LANES * npairs), BF16)
    scratch = [pltpu.VMEM((pairs, n, LANES), BF16)] * 3 + [pltpu.VMEM((pairs, nk, LANES, bk), BF16)]
    scratch += [pltpu.VMEM((pairs, n, LANES), F32)] * 2
    scratch += [pltpu.VMEM((2 * pairs, LANES, bq), F32), pltpu.VMEM((2 * pairs, 1, bq), F32), pltpu.VMEM((2 * pairs, 1, bq), F32)]
    return _call(
        body, cargo=cargo, name=name, grid=(zs, npairs // pairs),
        in_specs=[col(offs[0]), col(offs[1]), col(offs[2]), ospec, stat],
        out_specs=[ospec] * 3, out_shape=[sds] * 3, scratch_shapes=scratch,
        compiler_params=_params(("parallel", "parallel")),
    )(src, src, src, do, ltot)


def _fox_gate_fwd(lg, bias, *, name):
    bs, nh, t = lg.shape
    blk = min(LANES, t)

    def body(lg_ref, b_ref, kb_ref):
        u_le = _tri(blk, lambda r, c: r <= c)
        carry = jnp.zeros((nh, 1), F32)
        for j in range(t // blk):
            sl = slice(j * blk, (j + 1) * blk)
            xv = lg_ref[:, sl] + b_ref[...]
            lf = jnp.minimum(xv, 0.0) - jnp.log(1.0 + jnp.exp(-jnp.abs(xv)))
            kb_ref[:, sl] = -(carry + _cumdot(lf, u_le, 3))
            carry = carry + jnp.sum(lf, axis=1, keepdims=True)

    spec = pl.BlockSpec((None, nh, t), lambda i: (i, 0, 0))
    return _pcall(
        body, name=name, grid=(bs,), in_specs=[spec, pl.BlockSpec((nh, 1), lambda i: (0, 0))], out_specs=spec,
        out_shape=jax.ShapeDtypeStruct((bs, nh, t), F32), compiler_params=_params(("parallel",)),
    )(lg, bias)


def _fox_gate_bwd(dkb, lg, bias, *, name):
    bs, nh, t = lg.shape
    blk = min(LANES, t)

    def body(dkb_ref, lg_ref, b_ref, dlg_ref, db_ref):
        @pl.when(pl.program_id(0) == 0)
        def _():
            db_ref[...] = jnp.zeros_like(db_ref)

        u_ge = _tri(blk, lambda r, c: r >= c)
        carry = jnp.zeros((nh, 1), F32)
        tot = jnp.zeros((nh, 1), F32)
        for j in reversed(range(t // blk)):
            sl = slice(j * blk, (j + 1) * blk)
            df = -dkb_ref[:, sl]
            dlf = carry + _cumdot(df, u_ge, 3)
            carry = carry + jnp.sum(df, axis=1, keepdims=True)
            xv = lg_ref[:, sl] + b_ref[...]
            dlg = dlf * jax.nn.sigmoid(-xv)
            dlg_ref[:, sl] = dlg
            tot = tot + jnp.sum(dlg, axis=1, keepdims=True)
        db_ref[...] += jnp.broadcast_to(tot, db_ref.shape)

    spec = pl.BlockSpec((None, nh, t), lambda i: (i, 0, 0))
    return _pcall(
        body, name=name, grid=(bs,), in_specs=[spec, spec, pl.BlockSpec((nh, 1), lambda i: (0, 0))],
        out_specs=[spec, pl.BlockSpec((nh, LANES), lambda i: (0, 0))],
        out_shape=[jax.ShapeDtypeStruct((bs, nh, t), F32), jax.ShapeDtypeStruct((nh, LANES), F32)],
        compiler_params=_params(("arbitrary",)),
    )(dkb, lg, bias)


def _place():
    return lax.axis_index("x"), lax.axis_index("y"), lax.axis_index("c")


def _flip(v, f):
    return 1 - v if f else v


FLIPS = [(fx, fy, fc) for fx in (0, 1) for fy in (0, 1) for fc in (0, 1)][1:]


def _comm_sems(nw):
    return [pltpu.SemaphoreType.DMA((7, nw)), pltpu.SemaphoreType.DMA((7, nw)), pltpu.SemaphoreType.DMA((nw,))]


def _gather_cargo(shards, on_done):
    nw = len(shards)

    def parts(x_refs, out_refs, sems):
        send_sems, recv_sems, local_sems = sems
        x, y, cc = _place()
        me, sibling = (x, y, cc), (x, y, 1 - cc)
        chips = [(1 - x, y), (x, 1 - y), (1 - x, 1 - y)]

        def slot(i, px, py, pc):
            return out_refs[i].at[4 * px + 2 * py + pc]

        def copy(i, k, block, to, src=None):
            return pltpu.make_async_remote_copy(
                src_ref=slot(i, *block) if src is None else src, dst_ref=slot(i, *block),
                send_sem=send_sems.at[k, i], recv_sem=recv_sems.at[k, i], device_id=to, device_id_type=MESH)

        mine = [pltpu.make_async_copy(x_refs[i], slot(i, *me), local_sems.at[i]) for i in range(nw)]
        first = []
        for i in range(nw):
            first.append(copy(i, 0, me, sibling, src=x_refs[i]))
            first += [copy(i, 1 + j, me, (*chip, cc), src=x_refs[i]) for j, chip in enumerate(chips)]
        return me, sibling, chips, cc, copy, mine, first

    def start(x_refs, out_refs, sems):
        *_, mine, first = parts(x_refs, out_refs, sems)
        for cp in mine + first:
            cp.start()

    def finish(x_refs, out_refs, sems):
        me, sibling, chips, cc, copy, mine, first = parts(x_refs, out_refs, sems)
        passed = []
        for i in range(nw):
            for j, chip in enumerate(chips):
                copy(i, 1 + j, (*chip, cc), me).wait_recv()
                passed.append(copy(i, 4 + j, (*chip, cc), sibling))
                passed[-1].start()
        for i in range(nw):
            copy(i, 0, sibling, me).wait_recv()
            for j, chip in enumerate(chips):
                copy(i, 4 + j, (*chip, 1 - cc), me).wait_recv()
        for cp in first + passed:
            cp.wait_send()
        for cp in mine:
            cp.wait()

    out_shape = [jax.ShapeDtypeStruct((N_DEV, *s.shape), s.dtype) for s in shards]
    return _Cargo(shards, out_shape, _comm_sems(nw), start, finish, on_done)


def _scatter_cargo(slots, prev, layer, depth, on_done, row0=0, rows=None):
    nw = len(slots)

    def parts(refs, recv_refs, sems):
        g_refs = refs[:nw]
        send_sems, recv_sems, local_sems = sems
        x, y, cc = _place()
        my = 4 * x + 2 * y + cc

        def dst(i):
            return recv_refs[i].at[my, layer, pl.ds(row0, slots[i].shape[1])]

        mine, copies = [], []
        for i in range(nw):
            mine.append(pltpu.make_async_copy(g_refs[i].at[my], dst(i), local_sems.at[i]))
            for k, (fx, fy, fc) in enumerate(FLIPS):
                px, py, pc = _flip(x, fx), _flip(y, fy), _flip(cc, fc)
                copies.append(pltpu.make_async_remote_copy(
                    src_ref=g_refs[i].at[4 * px + 2 * py + pc], dst_ref=dst(i),
                    send_sem=send_sems.at[k, i], recv_sem=recv_sems.at[k, i], device_id=(px, py, pc), device_id_type=MESH))
        return mine, copies

    def start(refs, recv_refs, sems):
        mine, copies = parts(refs, recv_refs, sems)
        for cp in mine + copies:
            cp.start()

    def finish(refs, recv_refs, sems):
        mine, copies = parts(refs, recv_refs, sems)
        for cp in copies:
            cp.wait_recv()
        for cp in copies:
            cp.wait_send()
        for cp in mine:
            cp.wait()

    ins, aliases = list(slots), {}
    for i, p in enumerate(prev):
        if p is not None:
            aliases[len(ins)] = i
            ins.append(p)
    out_shape = [jax.ShapeDtypeStruct((N_DEV, depth, rows or s.shape[1], s.shape[2]), s.dtype) for s in slots]
    return _Cargo(ins, out_shape, _comm_sems(nw), start, finish, on_done, aliases)


def _exchange(cargo, *, name):
    def body(*refs):
        c_in = len(cargo.ins)
        c_out = len(cargo.out_shape)
        cargo.start(refs[:c_in], refs[c_in:c_in + c_out], refs[c_in + c_out:])
        cargo.finish(refs[:c_in], refs[c_in:c_in + c_out], refs[c_in + c_out:])

    hbm = pl.BlockSpec(memory_space=pl.ANY)
    res = _pcall(
        body, name=name, in_specs=[hbm] * len(cargo.ins), out_specs=[hbm] * len(cargo.out_shape), out_shape=cargo.out_shape,
        scratch_shapes=cargo.sems, input_output_aliases=dict(cargo.aliases),
    )(*cargo.ins)
    cargo.on_done(list(res))


def _allreduce_small(blob, *, name):
    r, c = blob.shape

    def body(x_ref, out_ref, buf, send_sems, recv_sems):
        x, y, cc = _place()
        my = 4 * x + 2 * y + cc
        copies = []
        for k, (fx, fy, fc) in enumerate(FLIPS):
            peer = (_flip(x, fx), _flip(y, fy), _flip(cc, fc))
            copies.append(pltpu.make_async_remote_copy(
                src_ref=x_ref, dst_ref=buf.at[my], send_sem=send_sems.at[k], recv_sem=recv_sems.at[k],
                device_id=peer, device_id_type=MESH))
        for cp in copies:
            cp.start()
        buf[my] = x_ref[...]
        for cp in copies:
            cp.wait_recv()
        for cp in copies:
            cp.wait_send()
        acc = buf[0]
        for i in range(1, N_DEV):
            acc = acc + buf[i]
        out_ref[...] = acc

    vmem = pl.BlockSpec(memory_space=pltpu.VMEM)
    return _pcall(
        body, name=name, in_specs=[vmem], out_specs=vmem, out_shape=jax.ShapeDtypeStruct((r, c), F32),
        scratch_shapes=[pltpu.VMEM((N_DEV, r, c), F32), pltpu.SemaphoreType.DMA((7,)), pltpu.SemaphoreType.DMA((7,))],
    )(blob)


BIG = ("w_in", "w_mlp_in", "w_mlp_out", "w_up_fox", "w_up_sb", "w_up_dil", "w_out")
ROW_SHARDED = ("w_out", "w_mlp_out")
SMALL = ("attn_norm", "b_forget", "q_norm_fox", "k_norm_fox", "q_norm_dil", "k_norm_dil", "mlp_norm")
BLOB_ROWS = 512


def _pack(parts, dtype):
    flat = jnp.concatenate([p.reshape(-1).astype(dtype) for p in parts])
    size = -(-flat.shape[0] // (BLOB_ROWS * LANES)) * (BLOB_ROWS * LANES)
    return jnp.pad(flat, (0, size - flat.shape[0])).reshape(-1, LANES)


def _unpack(blob, shapes):
    flat = blob.reshape(-1)
    out, off = [], 0
    for shp in shapes:
        size = 1
        for s in shp:
            size *= s
        out.append(flat[off:off + size].reshape(shp))
        off += size
    return out


def _join_shards(name, sh):
    if name in ROW_SHARDED:
        return sh.reshape(-1, sh.shape[2])
    return jnp.transpose(sh, (1, 0, 2)).reshape(sh.shape[1], -1)


def _split_shards(name, full):
    a, b = full.shape
    if name in ROW_SHARDED:
        return full.reshape(N_DEV, a // N_DEV, b)
    return jnp.transpose(full.reshape(a, N_DEV, b // N_DEV), (1, 0, 2))


def _in_segments(d_in):
    o1 = 3 * W_FOX
    o2 = o1 + N_HEADS_FOX
    return (0, o1, 0), (o2, d_in, -N_HEADS_FOX), (o1, o2, d_in - o2)


def _join_w_in(sh, dp):
    b = sh.shape[2]
    pieces = []
    for s, e, _ in _in_segments(N_DEV * b):
        for j in range(s // b, (e - 1) // b + 1):
            pieces.append(sh[j, :, max(s, j * b) - j * b:min(e, (j + 1) * b) - j * b])
    pieces.append(jnp.zeros((sh.shape[1], dp - N_DEV * b), sh.dtype))
    return jnp.concatenate(pieces, axis=1)


def _split_w_in(gp, d_in):
    b = d_in // N_DEV
    shards = []
    for j in range(N_DEV):
        runs = []
        for s, e, shift in sorted(_in_segments(d_in)):
            lo, hi = max(s, j * b), min(e, (j + 1) * b)
            if lo < hi:
                runs.append(gp[:, lo + shift:hi + shift])
        shards.append(jnp.concatenate(runs, axis=1))
    return jnp.stack(shards)


def _stat_to_tokens(st, r, b):
    hh = st.shape[1]
    n = st.shape[2] * st.shape[4]
    return jnp.transpose(st.reshape(b, r, hh, n), (0, 2, 3, 1)).reshape(b, hh, n * r)


def _stat_to_streams(tok, r, blk):
    b, hh, t = tok.shape
    n = t // r
    return jnp.transpose(tok.reshape(b, hh, n, r), (0, 3, 1, 2)).reshape(b * r, hh, n // blk, 1, blk)


def _rope_tables(positions):
    half = HEAD_DIM // 2
    inv = 1.0 / (ROPE_THETA ** (jnp.arange(half, dtype=F32) / half))
    ang = positions.astype(F32)[..., None] * inv
    cos, sin = jnp.cos(ang), jnp.sin(ang)
    return jnp.tile(cos, (1, 1, 4)), jnp.tile(jnp.concatenate([-sin, sin], axis=-1), (1, 1, 2))


def _gain2(g):
    return jnp.tile(g.reshape(1, HEAD_DIM), (1, 2))


def _dil_offs(g):
    c0 = (P_DIL + g * W_DIL) // LANES
    return c0, c0 + W_DILQ // LANES, c0 + 2 * W_DILQ // LANES


def _w_in_split(d):
    cut = (3 * d // 8) // LANES * LANES
    return cut if cut > 0 else d // 2


def _dil_pairs(r):
    return N_HEADS_DIL // 2


def _layer_fwd(l, x, h, w, small, ropes, bl, t, cargo, next_gain):
    n, d = x.shape
    s = {}
    s["x"] = x
    s["h"] = _rmsnorm_fwd(x, small["attn_norm"][l].reshape(1, d), name=f"norm_attn_fwd{l}") if h is None else h
    proj = _mm(s["h"], w["w_in"][l], tn=PROJ_TILE, name=f"mm_proj{l}")
    s["proj"] = proj
    dp = proj.shape[1]
    proj3 = proj.reshape(bl, t, dp)
    p_fg = P_GATE + 3 * d

    lg = jnp.transpose(proj3[:, :, p_fg:p_fg + N_HEADS_FOX], (0, 2, 1))
    s["lg"] = lg
    kb = _fox_gate_fwd(lg, small["b_forget"][l].reshape(N_HEADS_FOX, 1), name=f"fox_gate_fwd{l}")
    blk = _att_blocks(t, wide_keys=True)[1]
    kb5 = kb.reshape(bl, N_HEADS_FOX, t // blk, 1, blk)
    s["kb5"] = kb5
    gqf, gkf = _gain2(small["q_norm_fox"][l]), _gain2(small["k_norm_fox"][l])
    fo = P_FOX // LANES
    fox_offs = (fo, fo + W_FOX // LANES, fo + 2 * W_FOX // LANES)
    out_a, lse_a = _attn_fwd(proj3, fox_offs, N_HEADS_FOX // 2, gqf, gkf, kbias=kb5, window=t, name=f"fox_fwd{l}",
                             cargo=cargo.get("fox_fwd"))
    s["out_a"], s["lse_a"] = out_a, lse_a

    so = P_SB // LANES
    sb_offs = (so, so + W_SB // LANES, so + 2 * W_SB // LANES)
    out_b, lt_b = _sb_fwd(proj3, sb_offs, N_HEADS_SB // 2, name=f"sb_fwd{l}", cargo=cargo.get("sb_fwd"))
    s["out_b"], s["lt_b"] = out_b, lt_b

    gqd, gkd = _gain2(small["q_norm_dil"][l]), _gain2(small["k_norm_dil"][l])
    os_, lses = [], []
    for g, (window, r) in enumerate(DIL_PATTERNS):
        o_g, lse_g = _attn_fwd(proj3, _dil_offs(g), N_HEADS_DIL // 2, gqd, gkd, rope=ropes, window=window // r, stride=r,
                               pairs=_dil_pairs(r), name=f"dil_fwd{l}_{g}")
        os_.append(o_g.reshape(n, W_DIL))
        lses.append(_stat_to_tokens(lse_g, r, bl).reshape(bl * N_HEADS_DIL, t))
    lse_c, *ws = _dil_weights(lses, name=f"dil_weights{l}")
    ws = [jnp.repeat(jnp.transpose(wg.reshape(bl, N_HEADS_DIL, t), (0, 2, 1)).reshape(n, N_HEADS_DIL), HEAD_DIM, axis=1) for wg in ws]
    out_c = _dil_mix(os_, ws, name=f"dil_mix{l}")
    s["out_c"], s["lse_c"] = out_c, lse_c.reshape(bl, N_HEADS_DIL, t)

    ys = [_mm(out_a.reshape(n, W_FOX), w["w_up_fox"][l], out_dtype=BF16, name=f"mm_up_fox{l}"),
          _mm(out_b.reshape(n, W_SB), w["w_up_sb"][l], out_dtype=BF16, name=f"mm_up_sb{l}"),
          _mm(out_c, w["w_up_dil"][l], out_dtype=BF16, name=f"mm_up_dil{l}")]
    s["ys"] = ys
    s["merged"] = _gate_merge_fwd(proj, ys, name=f"gate_merge_fwd{l}")
    x1, s["h2"] = _mm(s["merged"], w["w_out"][l], add=x, norm_gain=small["mlp_norm"][l].reshape(1, d), name=f"mm_out{l}")
    s["x1"] = x1

    s["u"], s["a"] = _mm(s["h2"], w["w_mlp_in"][l], relu2=True, out_dtype=BF16, name=f"mm_mlp_in{l}")
    if next_gain is None:
        return _mm(s["a"], w["w_mlp_out"][l], add=x1, name=f"mm_mlp_out{l}"), None, s
    x2, h_next = _mm(s["a"], w["w_mlp_out"][l], add=x1, norm_gain=next_gain, name=f"mm_mlp_out{l}")
    return x2, h_next, s


def _layer_bwd(l, dx2, s, w, small, ropes, bl, t, hooks):
    n, d = dx2.shape
    gw, gs = {}, {}

    def cargo(call):
        return hooks[call](gw) if call in hooks else None
    du = _mm(dx2, w["w_mlp_out"][l], tb=True, relu_grad_of=s["u"], out_dtype=BF16, name=f"mm_du{l}")
    gw["w_mlp_out"] = _mm(s["a"], dx2, ta=True, tm=2048, name=f"mm_dw_mlp_out{l}")
    gw["w_mlp_in"] = _mm(s["h2"], du, ta=True, name=f"mm_dw_mlp_in{l}")
    dh2 = _mm(du, w["w_mlp_in"][l], tb=True, name=f"mm_dh2{l}")
    dx1, gs["mlp_norm"] = _rmsnorm_bwd(s["x1"], small["mlp_norm"][l].reshape(1, d), dh2, dx2, name=f"norm_mlp_bwd{l}")

    dmerged = _mm(dx1, w["w_out"][l], tb=True, name=f"mm_dmerged{l}")
    gw["w_out"] = _mm(s["merged"], dx1, ta=True, name=f"mm_dw_out{l}")
    dya, dyb, dyc, dgl0, dgl1, dgl2 = _gate_merge_bwd(s["proj"], s["ys"], dmerged, name=f"gate_merge_bwd{l}")
    out_a2, out_b2 = s["out_a"].reshape(n, W_FOX), s["out_b"].reshape(n, W_SB)
    gw["w_up_fox"] = _mm(out_a2, dya, ta=True, name=f"mm_dw_up_fox{l}")
    gw["w_up_sb"] = _mm(out_b2, dyb, ta=True, name=f"mm_dw_up_sb{l}")
    gw["w_up_dil"] = _mm(s["out_c"], dyc, ta=True, name=f"mm_dw_up_dil{l}")
    dout_a = _mm(dya, w["w_up_fox"][l], tb=True, name=f"mm_dout_a{l}").reshape(bl, t, W_FOX)
    dout_b = _mm(dyb, w["w_up_sb"][l], tb=True, name=f"mm_dout_b{l}").reshape(bl, t, W_SB)
    dout_c = _mm(dyc, w["w_up_dil"][l], tb=True, name=f"mm_dout_c{l}").reshape(bl, t, W_DIL)

    proj3 = s["proj"].reshape(bl, t, -1)
    gqf, gkf = _gain2(small["q_norm_fox"][l]), _gain2(small["k_norm_fox"][l])
    fo = P_FOX // LANES
    fox_offs = (fo, fo + W_FOX // LANES, fo + 2 * W_FOX // LANES)
    (dq_a,), (dk_a,), (dv_a,), dg_a, dkb5 = _attn_bwd(proj3, fox_offs, N_HEADS_FOX // 2, gqf, gkf, s["out_a"], dout_a, s["lse_a"],
                                             kbias=s["kb5"], window=t, name=f"fox_bwd{l}", cargo=cargo("fox_bwd"))
    gs["fox_gains"] = dg_a
    dlg, gs["b_forget"] = _fox_gate_bwd(dkb5.reshape(bl, N_HEADS_FOX, t), s["lg"], small["b_forget"][l].reshape(N_HEADS_FOX, 1),
                                        name=f"fox_gate_bwd{l}")
    so = P_SB // LANES
    sb_offs = (so, so + W_SB // LANES, so + 2 * W_SB // LANES)
    dq_b, dk_b, dv_b = _sb_bwd(proj3, sb_offs, N_HEADS_SB // 2, dout_b, s["lt_b"], name=f"sb_bwd{l}", cargo=cargo("sb_bwd"))
    gqd, gkd = _gain2(small["q_norm_dil"][l]), _gain2(small["k_norm_dil"][l])
    out_c3 = s["out_c"].reshape(bl, t, W_DIL)
    dqs, dks, dvs, dgd = [], [], [], None
    for g, (window, r) in enumerate(DIL_PATTERNS):
        lse_g = _stat_to_streams(s["lse_c"], r, _att_blocks(t // r)[0])
        dq_g, dk_g, dv_g, dg_g = _attn_bwd(proj3, _dil_offs(g), N_HEADS_DIL // 2, gqd, gkd, out_c3, dout_c, lse_g, rope=ropes,
                                           window=window // r, stride=r, pairs=_dil_pairs(r), name=f"dil_bwd{l}_{g}")
        dqs += dq_g
        dks += dk_g
        dvs += dv_g
        dgd = dg_g if dgd is None else jnp.concatenate([dgd, dg_g], axis=0)
    gs["dil_gains"] = dgd

    dlg_cols = jnp.pad(jnp.transpose(dlg, (0, 2, 1)).reshape(n, N_HEADS_FOX), ((0, 0), (0, LANES - N_HEADS_FOX)))
    parts = [p.reshape(n, -1) for p in [dq_a, dk_a, dv_a, dq_b, dk_b, dv_b] + dqs + dks + dvs] + [dgl0, dgl1, dgl2, dlg_cols]
    dproj = _assemble_cols(parts, s["proj"].shape[1], name=f"assemble_dproj{l}")
    if "mm_dw_in_hi" in hooks:
        half = _w_in_split(d)
        gw["w_in_lo"] = _mm(s["h"][:, :half], dproj, ta=True, tn=PROJ_TILE, name=f"mm_dw_in_lo{l}")
        gw["w_in_hi"] = _mm(s["h"][:, half:], dproj, ta=True, tn=PROJ_TILE, name=f"mm_dw_in_hi{l}", cargo=cargo("mm_dw_in_hi"))
    else:
        gw["w_in"] = _mm(s["h"], dproj, ta=True, tn=PROJ_TILE, name=f"mm_dw_in{l}")
    dh = _mm(dproj, w["w_in"][l], tb=True, tn=1024, tk=PROJ_TILE, name=f"mm_dh{l}", cargo=cargo("mm_dh"))
    dx, gs["attn_norm"] = _rmsnorm_bwd(s["x"], small["attn_norm"][l].reshape(1, d), dh, dx1, name=f"norm_attn_bwd{l}")
    return dx, gw, gs


def kernel(x, positions, attn_norm, w_in, b_forget, q_norm_fox, k_norm_fox, q_norm_dil, k_norm_dil, w_up_fox, w_up_sb, w_up_dil, w_out, mlp_norm, w_mlp_in, w_mlp_out, loss_target, m_attn_norm, m_w_in, m_b_forget, m_q_norm_fox, m_k_norm_fox, m_q_norm_dil, m_k_norm_dil, m_w_up_fox, m_w_up_sb, m_w_up_dil, m_w_out, m_mlp_norm, m_w_mlp_in, m_w_mlp_out, v_attn_norm, v_w_in, v_b_forget, v_q_norm_fox, v_k_norm_fox, v_q_norm_dil, v_k_norm_dil, v_w_up_fox, v_w_up_sb, v_w_up_dil, v_w_out, v_mlp_norm, v_w_mlp_in, v_w_mlp_out):
    bl, t, d = x.shape
    n = bl * t
    depth = attn_norm.shape[0]
    wl = dict(w_in=w_in, w_up_fox=w_up_fox, w_up_sb=w_up_sb, w_up_dil=w_up_dil, w_out=w_out, w_mlp_in=w_mlp_in, w_mlp_out=w_mlp_out)
    ml = dict(w_in=m_w_in, w_up_fox=m_w_up_fox, w_up_sb=m_w_up_sb, w_up_dil=m_w_up_dil, w_out=m_w_out, w_mlp_in=m_w_mlp_in, w_mlp_out=m_w_mlp_out)
    vl = dict(w_in=v_w_in, w_up_fox=v_w_up_fox, w_up_sb=v_w_up_sb, w_up_dil=v_w_up_dil, w_out=v_w_out, w_mlp_in=v_w_mlp_in, w_mlp_out=v_w_mlp_out)
    small = dict(attn_norm=attn_norm, b_forget=b_forget, q_norm_fox=q_norm_fox, k_norm_fox=k_norm_fox, q_norm_dil=q_norm_dil,
                 k_norm_dil=k_norm_dil, mlp_norm=mlp_norm)
    m_small = dict(attn_norm=m_attn_norm, b_forget=m_b_forget, q_norm_fox=m_q_norm_fox, k_norm_fox=m_k_norm_fox,
                   q_norm_dil=m_q_norm_dil, k_norm_dil=m_k_norm_dil, mlp_norm=m_mlp_norm)
    v_small = dict(attn_norm=v_attn_norm, b_forget=v_b_forget, q_norm_fox=v_q_norm_fox, k_norm_fox=v_k_norm_fox,
                   q_norm_dil=v_q_norm_dil, k_norm_dil=v_k_norm_dil, mlp_norm=v_mlp_norm)

    d_in = w_in.shape[-1] * N_DEV
    dp = -(-d_in // 512) * 512
    rest = [k for k in BIG if k != "w_in"]
    w = {k: [None] * depth for k in BIG}

    def gather(items):
        def done(res):
            for (k, l), sh in zip(items, res):
                w[k][l] = _join_w_in(sh, dp) if k == "w_in" else _join_shards(k, sh)

        return _gather_cargo([wl[k][l].astype(BF16) for k, l in items], done)

    _exchange(gather([("w_in", 0)]), name="gather_first")
    cos, sin = _rope_tables(positions)
    ropes = (cos, sin)

    xl, hl = x.reshape(n, d), None
    saved = []
    for l in range(depth):
        cargo = {"fox_fwd": gather([(k, l) for k in rest])}
        if l + 1 < depth:
            cargo["sb_fwd"] = gather([("w_in", l + 1)])
        next_gain = attn_norm[l + 1].reshape(1, d) if l + 1 < depth else None
        xl, hl, s = _layer_fwd(l, xl, hl, w, small, ropes, bl, t, cargo, next_gain)
        saved.append(s)
    dy, loss_part = _loss_head(xl, loss_target.reshape(n, d), name="loss_head")

    recv = {}

    def scatter(names, l, grads):
        def done(res):
            recv.update(zip(names, res))

        slots = [(_split_w_in(grads[k], d_in) if k == "w_in" else _split_shards(k, grads[k])).astype(BF16) for k in names]
        return _scatter_cargo(slots, [recv.get(k) for k in names], l, depth, done)

    def scatter_w_in_rows(part, l, row0):
        def done(res):
            recv["w_in"] = res[0]

        return _scatter_cargo([_split_w_in(part, d_in).astype(BF16)], [recv.get("w_in")], l, depth, done, row0=row0, rows=d)

    gss = [None] * depth
    above = None
    for l in reversed(range(depth)):
        hooks = {"fox_bwd": lambda gw, l=l: scatter(rest, l, gw)}
        if above is not None:
            hooks["sb_bwd"] = lambda gw, l=l, g=above: scatter(["w_in"], l + 1, g)
        if l == 0:
            hooks["mm_dw_in_hi"] = lambda gw: scatter_w_in_rows(gw["w_in_lo"], 0, 0)
            hooks["mm_dh"] = lambda gw: scatter_w_in_rows(gw["w_in_hi"], 0, _w_in_split(d))
        dy, above, gss[l] = _layer_bwd(l, dy, saved[l], w, small, ropes, bl, t, hooks)
    grad_x = dy.reshape(bl, t, d)

    g_big, d_big, m_big, v_big = {}, {}, {}, {}
    for k in BIG:
        g_big[k], d_big[k], m_big[k], v_big[k] = _adamw(recv[k], wl[k], ml[k], vl[k], name=f"adamw_{k}")

    rows = [loss_part]
    for l in range(depth):
        gs = gss[l]
        rows += [gs["attn_norm"].reshape(-1, LANES), gs["mlp_norm"].reshape(-1, LANES), gs["fox_gains"], gs["dil_gains"], gs["b_forget"]]
    row_counts = [r.shape[0] for r in rows]
    part = jnp.concatenate(rows, axis=0)
    pad_rows = -(-part.shape[0] // 8) * 8 - part.shape[0]
    summed = _allreduce_small(jnp.pad(part, ((0, pad_rows), (0, 0))), name="allreduce_small")
    pieces, off = [], 0
    for c in row_counts:
        pieces.append(summed[off:off + c])
        off += c
    loss = pieces[0][0, 0]

    def fold(row):
        return row[:HEAD_DIM] + row[HEAD_DIM:]

    g_small = {k: [] for k in SMALL}
    for l in range(depth):
        an, mn, fg, dg, bf = pieces[1 + 5 * l:6 + 5 * l]
        g_small["attn_norm"].append(an.reshape(d))
        g_small["mlp_norm"].append(mn.reshape(d))
        g_small["q_norm_fox"].append(fold(fg[0]))
        g_small["k_norm_fox"].append(fold(fg[1]))
        g_small["q_norm_dil"].append(fold(dg[0]) + fold(dg[8]) + fold(dg[16]))
        g_small["k_norm_dil"].append(fold(dg[1]) + fold(dg[9]) + fold(dg[17]))
        g_small["b_forget"].append(bf[:, 0])
    g_small = {k: jnp.stack(vs) for k, vs in g_small.items()}
    small_shapes = [small[k].shape for k in SMALL]
    outs = _adamw(_pack([g_small[k] for k in SMALL], F32)[None, None], _pack([small[k] for k in SMALL], F32)[None],
                  _pack([m_small[k] for k in SMALL], F32)[None], _pack([v_small[k] for k in SMALL], F32)[None], name="adamw_small")
    g_sm, d_sm, m_sm, v_sm = (dict(zip(SMALL, _unpack(o, small_shapes))) for o in outs)

    order = ("attn_norm", "w_in", "b_forget", "q_norm_fox", "k_norm_fox", "q_norm_dil", "k_norm_dil", "w_up_fox", "w_up_sb",
             "w_up_dil", "w_out", "mlp_norm", "w_mlp_in", "w_mlp_out")
    res = [loss, grad_x]
    for big, sm in ((g_big, g_sm), (d_big, d_sm), (m_big, m_sm), (v_big, v_sm)):
        res += [big[k] if k in big else sm[k] for k in order]
    return tuple(res)
```

```python
import jax
import jax.numpy as jnp
from jax import lax
from jax.experimental import pallas as pl
from jax.experimental.pallas import tpu as pltpu

F32 = jnp.float32
BF16 = jnp.bfloat16

HEAD_DIM = 64
LANES = 128
N_HEADS_FOX = 8
N_HEADS_SB = 8
N_HEADS_DIL = 4
DIL_PATTERNS = ((128, 1), (512, 4), (2048, 16))
ROPE_THETA = 10000.0
EPS = 1e-6
SCALE = 0.125
W_FOX = N_HEADS_FOX * HEAD_DIM
W_SB = N_HEADS_SB * HEAD_DIM
W_DIL = N_HEADS_DIL * HEAD_DIM
W_DILQ = len(DIL_PATTERNS) * W_DIL
P_FOX = 0
P_SB = 3 * W_FOX
P_DIL = P_SB + 3 * W_SB
P_GATE = P_DIL + 3 * W_DILQ
N_DEV = 8
ATT_BLK = 256
ATT_BQ = 512
NEG = -1e30
VMEM_LIMIT = 56 * 1024 * 1024
ADAMW_BLOCK_ELEMS = 128 * 1024
PROJ_TILE = 2176

ADAM_LR = 0.001
ADAM_B1 = 0.9
ADAM_B2 = 0.999
ADAM_EPS = 1e-08
ADAM_WD = 0.01
ADAM_STEP = 10

NT = (((1,), (1,)), ((), ()))
MESH = pl.DeviceIdType.MESH


def _pcall(body, **kw):
    return pl.pallas_call(body, **kw)


def _params(sem=None):
    return pltpu.CompilerParams(dimension_semantics=sem, vmem_limit_bytes=VMEM_LIMIT)


class _Cargo:
    def __init__(self, ins, out_shape, sems, start, finish, on_done, aliases=None):
        self.ins, self.out_shape, self.sems = list(ins), list(out_shape), list(sems)
        self.start, self.finish, self.on_done, self.aliases = start, finish, on_done, dict(aliases or {})


def _call(body, *, cargo=None, name, grid=(), in_specs, out_specs, out_shape, scratch_shapes=(), compiler_params=None):
    if cargo is None:
        kw = dict(grid=grid) if grid else {}
        if compiler_params is not None:
            kw["compiler_params"] = compiler_params
        return _pcall(body, name=name, in_specs=in_specs, out_specs=out_specs, out_shape=out_shape,
                      scratch_shapes=list(scratch_shapes), **kw)
    single = not isinstance(out_shape, (list, tuple))
    o_specs, o_shape = ([out_specs], [out_shape]) if single else (list(out_specs), list(out_shape))
    n_in, n_out, n_scr = len(in_specs), len(o_shape), len(scratch_shapes)
    c_in, c_out = len(cargo.ins), len(cargo.out_shape)

    def wrapped(*refs):
        ins, cins = refs[:n_in], refs[n_in:n_in + c_in]
        o0 = n_in + c_in
        outs, couts = refs[o0:o0 + n_out], refs[o0 + n_out:o0 + n_out + c_out]
        s0 = o0 + n_out + c_out
        scr, sems = refs[s0:s0 + n_scr], refs[s0 + n_scr:]
        first = last = None
        for ax, size in enumerate(grid):
            pid = pl.program_id(ax)
            first = (pid == 0) if first is None else first & (pid == 0)
            last = (pid == size - 1) if last is None else last & (pid == size - 1)
        if first is None:
            cargo.start(cins, couts, sems)
            body(*ins, *outs, *scr)
            cargo.finish(cins, couts, sems)
            return

        @pl.when(first)
        def _():
            cargo.start(cins, couts, sems)

        body(*ins, *outs, *scr)

        @pl.when(last)
        def _():
            cargo.finish(cins, couts, sems)

    hbm = pl.BlockSpec(memory_space=pl.ANY)
    kw = dict(grid=grid, compiler_params=_params(("arbitrary",) * len(grid))) if grid else {}
    call = _pcall(
        wrapped, name=name, in_specs=list(in_specs) + [hbm] * c_in, out_specs=o_specs + [hbm] * c_out,
        out_shape=o_shape + cargo.out_shape, scratch_shapes=list(scratch_shapes) + cargo.sems,
        input_output_aliases={n_in + i: n_out + j for i, j in cargo.aliases.items()}, **kw)

    def run(*args):
        res = call(*args, *cargo.ins)
        cargo.on_done(list(res[n_out:]))
        return res[0] if single else list(res[:n_out])

    return run


def _tile(dim, target, mult=LANES):
    t = (min(dim, target) // mult) * mult
    while t >= mult:
        if dim % t == 0:
            return t
        t -= mult
    return dim


def _mm(a, b, *, ta=False, tb=False, add=None, relu2=False, relu_grad_of=None, norm_gain=None, out_dtype=F32, name,
        tm=1024, tn=1024, tk=1024, cargo=None):
    m, k = (a.shape[1], a.shape[0]) if ta else a.shape
    n = b.shape[0] if tb else b.shape[1]
    tm, tn, tk = _tile(m, tm), _tile(n, tn), _tile(k, tk)
    nk = k // tk
    dn = (((0,) if ta else (1,), (1,) if tb else (0,)), ((), ()))

    extra = add if add is not None else relu_grad_of
    normed = norm_gain is not None
    assert not (normed and relu2) and (not normed or tn == n)

    def body(*refs):
        a_ref, b_ref = refs[:2]
        x_ref = refs[2] if extra is not None else None
        g_ref = refs[2 + (extra is not None)] if normed else None
        outs = refs[2 + (extra is not None) + normed:-1]
        acc = refs[-1]
        kk = pl.program_id(2)
        part = lax.dot_general(a_ref[...].astype(BF16), b_ref[...].astype(BF16), dn, preferred_element_type=F32)

        def finish(r):
            if add is not None:
                r = r + x_ref[...]
            if relu_grad_of is not None:
                r = r * (2.0 * jnp.maximum(x_ref[...].astype(F32), 0.0))
            outs[0][...] = r.astype(out_dtype)
            if relu2:
                rr = jnp.maximum(r, 0.0)
                outs[1][...] = (rr * rr).astype(BF16)
            if normed:
                inv = lax.rsqrt(jnp.mean(r * r, axis=1, keepdims=True) + EPS)
                outs[1][...] = (r * inv * g_ref[...]).astype(BF16)

        if nk == 1:
            finish(part)
            return

        @pl.when(kk == 0)
        def _():
            acc[...] = part

        @pl.when((kk > 0) & (kk < nk - 1))
        def _():
            acc[...] += part

        @pl.when(kk == nk - 1)
        def _():
            finish(acc[...] + part)

    a_spec = pl.BlockSpec((tk, tm), lambda i, j, q: (q, i)) if ta else pl.BlockSpec((tm, tk), lambda i, j, q: (i, q))
    b_spec = pl.BlockSpec((tn, tk), lambda i, j, q: (j, q)) if tb else pl.BlockSpec((tk, tn), lambda i, j, q: (q, j))
    o_spec = pl.BlockSpec((tm, tn), lambda i, j, q: (i, j))
    ins, specs = [a, b], [a_spec, b_spec]
    if extra is not None:
        ins.append(extra)
        specs.append(o_spec)
    if normed:
        ins.append(norm_gain)
        specs.append(pl.BlockSpec((1, tn), lambda i, j, q: (0, j)))
    two = relu2 or normed
    sds = jax.ShapeDtypeStruct((m, n), out_dtype)
    return _call(
        body, cargo=cargo, name=name, grid=(m // tm, n // tn, nk), in_specs=specs,
        out_specs=[o_spec, o_spec] if two else o_spec,
        out_shape=[sds, jax.ShapeDtypeStruct((m, n), BF16)] if two else sds,
        scratch_shapes=[pltpu.VMEM((tm, tn) if nk > 1 else (8, LANES), F32)],
        compiler_params=_params(("parallel", "parallel", "arbitrary")),
    )(*ins)


def _rmsnorm_fwd(x, g, *, name):
    n, d = x.shape
    tm = _tile(n, 256, 8)

    def body(x_ref, g_ref, h_ref):
        xv = x_ref[...]
        inv = lax.rsqrt(jnp.mean(xv * xv, axis=1, keepdims=True) + EPS)
        h_ref[...] = (xv * inv * g_ref[...]).astype(BF16)

    row = pl.BlockSpec((tm, d), lambda i: (i, 0))
    return _pcall(
        body, name=name, grid=(n // tm,), in_specs=[row, pl.BlockSpec((1, d), lambda i: (0, 0))], out_specs=row,
        out_shape=jax.ShapeDtypeStruct((n, d), BF16), compiler_params=_params(("parallel",)),
    )(x, g)


def _rmsnorm_bwd(x, g, dh, dres, *, name):
    n, d = x.shape
    tm = _tile(n, 256, 8)

    def body(x_ref, g_ref, dh_ref, dres_ref, dx_ref, dg_ref):
        @pl.when(pl.program_id(0) == 0)
        def _():
            dg_ref[...] = jnp.zeros_like(dg_ref)

        xv = x_ref[...]
        inv = lax.rsqrt(jnp.mean(xv * xv, axis=1, keepdims=True) + EPS)
        y = xv * inv
        dhv = dh_ref[...]
        dg_ref[...] += jnp.sum(dhv * y, axis=0, keepdims=True)
        dy = dhv * g_ref[...]
        dx_ref[...] = dres_ref[...] + inv * (dy - y * jnp.mean(dy * y, axis=1, keepdims=True))

    row = pl.BlockSpec((tm, d), lambda i: (i, 0))
    vec = pl.BlockSpec((1, d), lambda i: (0, 0))
    return _pcall(
        body, name=name, grid=(n // tm,), in_specs=[row, vec, row, row], out_specs=[row, vec],
        out_shape=[jax.ShapeDtypeStruct((n, d), F32), jax.ShapeDtypeStruct((1, d), F32)],
        compiler_params=_params(("arbitrary",)),
    )(x, g, dh, dres)


def _gate_specs(n, d):
    bw = 256 if d % 256 == 0 else LANES
    tm = _tile(n, 512, 8)
    nb = d // bw
    yspec = pl.BlockSpec((tm, bw), lambda i, j: (i, j))
    gspecs = [pl.BlockSpec((tm, bw), lambda i, j, b=b: (i, P_GATE // bw + b * nb + j)) for b in range(3)]
    return tm, bw, nb, yspec, gspecs


def _gate_merge_fwd(proj, ys, *, name):
    n, d = ys[0].shape
    tm, bw, nb, yspec, gspecs = _gate_specs(n, d)

    def body(g0, g1, g2, y0, y1, y2, o_ref):
        acc = jax.nn.sigmoid(g0[...]) * y0[...]
        acc += jax.nn.sigmoid(g1[...]) * y1[...]
        acc += jax.nn.sigmoid(g2[...]) * y2[...]
        o_ref[...] = acc.astype(BF16)

    return _pcall(
        body, name=name, grid=(n // tm, nb), in_specs=gspecs + [yspec] * 3, out_specs=yspec,
        out_shape=jax.ShapeDtypeStruct((n, d), BF16), compiler_params=_params(("parallel", "parallel")),
    )(proj, proj, proj, *ys)


def _gate_merge_bwd(proj, ys, dmerged, *, name):
    n, d = ys[0].shape
    tm, bw, nb, yspec, gspecs = _gate_specs(n, d)

    def body(g0, g1, g2, y0, y1, y2, dm_ref, dy0, dy1, dy2, dgl0, dgl1, dgl2):
        dm = dm_ref[...]
        for g_ref, y_ref, dy_ref, dgl_ref in ((g0, y0, dy0, dgl0), (g1, y1, dy1, dgl1), (g2, y2, dy2, dgl2)):
            s = jax.nn.sigmoid(g_ref[...])
            dy_ref[...] = (dm * s).astype(BF16)
            dgl_ref[...] = (dm * y_ref[...] * s * (1.0 - s)).astype(BF16)

    sds = jax.ShapeDtypeStruct((n, d), BF16)
    return _pcall(
        body, name=name, grid=(n // tm, nb), in_specs=gspecs + [yspec] * 4, out_specs=[yspec] * 6,
        out_shape=[sds] * 6, compiler_params=_params(("parallel", "parallel")),
    )(proj, proj, proj, *ys, dmerged)


def _loss_head(y, tgt, *, name):
    n, d = y.shape
    tm = _tile(n, 256, 8)
    steps = n // tm

    def body(y_ref, t_ref, dy_ref, loss_ref, acc):
        i = pl.program_id(0)

        @pl.when(i == 0)
        def _():
            acc[...] = jnp.zeros_like(acc)

        e = y_ref[...] - t_ref[...]
        dy_ref[...] = e * (1.0 / d)
        acc[...] += jnp.sum(e * e, axis=0, keepdims=True)

        @pl.when(i == steps - 1)
        def _():
            tot = jnp.sum(acc[...], axis=1, keepdims=True) * (0.5 / d)
            loss_ref[...] = jnp.broadcast_to(tot, loss_ref.shape)

    row = pl.BlockSpec((tm, d), lambda i: (i, 0))
    return _pcall(
        body, name=name, grid=(steps,), in_specs=[row, row], out_specs=[row, pl.BlockSpec((8, LANES), lambda i: (0, 0))],
        out_shape=[jax.ShapeDtypeStruct((n, d), F32), jax.ShapeDtypeStruct((8, LANES), F32)],
        scratch_shapes=[pltpu.VMEM((1, d), F32)], compiler_params=_params(("arbitrary",)),
    )(y, tgt)


def _assemble_cols(parts, width, *, name):
    n = parts[0].shape[0]
    tm = _tile(n, 256, 16)
    widths = [p.shape[1] for p in parts]

    def body(*refs):
        o_ref = refs[-1]
        off = 0
        for ref, w in zip(refs[:-1], widths):
            o_ref[:, off:off + w] = ref[...].astype(BF16)
            off += w
        if off < width:
            o_ref[:, off:] = jnp.zeros((tm, width - off), BF16)

    return _pcall(
        body, name=name, grid=(n // tm,), in_specs=[pl.BlockSpec((tm, w), lambda i: (i, 0)) for w in widths],
        out_specs=pl.BlockSpec((tm, width), lambda i: (i, 0)), out_shape=jax.ShapeDtypeStruct((n, width), BF16),
        compiler_params=_params(("parallel",)),
    )(*parts)


def _dil_weights(lses, *, name):
    shp = lses[0].shape

    def body(l0, l1, l2, lse_ref, w0, w1, w2):
        a, b, c = l0[...], l1[...], l2[...]
        m = jnp.maximum(jnp.maximum(a, b), c)
        ea, eb, ec = jnp.exp(a - m), jnp.exp(b - m), jnp.exp(c - m)
        den = ea + eb + ec
        lse_ref[...] = m + jnp.log(den)
        w0[...] = ea / den
        w1[...] = eb / den
        w2[...] = ec / den

    vmem = pl.BlockSpec(memory_space=pltpu.VMEM)
    return _pcall(body, name=name, in_specs=[vmem] * 3, out_specs=[vmem] * 4, out_shape=[jax.ShapeDtypeStruct(shp, F32)] * 4)(*lses)


def _dil_mix(os_, ws, *, name):
    n, w = os_[0].shape
    tm = _tile(n, 512, 8)

    def body(o0, o1, o2, w0, w1, w2, out_ref):
        out_ref[...] = w0[...] * o0[...] + w1[...] * o1[...] + w2[...] * o2[...]

    spec = pl.BlockSpec((tm, w), lambda i: (i, 0))
    return _pcall(
        body, name=name, grid=(n // tm,), in_specs=[spec] * 6, out_specs=spec, out_shape=jax.ShapeDtypeStruct((n, w), F32),
        compiler_params=_params(("parallel",)),
    )(*os_, *ws)


def _adamw(gsrc, w, m, v, *, name):
    s, dep, a, b = gsrc.shape
    ta = _tile(a, max(16, (ADAMW_BLOCK_ELEMS // b) // 16 * 16), 16)
    c1 = 1.0 / (1.0 - ADAM_B1 ** ADAM_STEP)
    c2 = 1.0 / (1.0 - ADAM_B2 ** ADAM_STEP)

    def body(gs_ref, w_ref, m_ref, v_ref, g_ref, d_ref, m2_ref, v2_ref):
        g = gs_ref[0].astype(F32)
        for i in range(1, s):
            g = g + gs_ref[i].astype(F32)
        m2 = ADAM_B1 * m_ref[...] + (1.0 - ADAM_B1) * g
        v2 = ADAM_B2 * v_ref[...] + (1.0 - ADAM_B2) * (g * g)
        g_ref[...] = g
        m2_ref[...] = m2
        v2_ref[...] = v2
        d_ref[...] = -ADAM_LR * ((m2 * c1) / (jnp.sqrt(v2 * c2) + ADAM_EPS) + ADAM_WD * w_ref[...])

    spec = pl.BlockSpec((None, ta, b), lambda l, i: (l, i, 0))
    sds = jax.ShapeDtypeStruct((dep, a, b), F32)
    return _pcall(
        body, name=name, grid=(dep, a // ta),
        in_specs=[pl.BlockSpec((s, None, ta, b), lambda l, i: (0, l, i, 0)), spec, spec, spec],
        out_specs=[spec] * 4, out_shape=[sds] * 4, compiler_params=_params(("parallel", "parallel")),
    )(gsrc, w, m, v)


def _mask_a():
    return lax.broadcasted_iota(jnp.int32, (1, LANES), 1) < HEAD_DIM


def _half_sum(x, m_a):
    sa = jnp.sum(jnp.where(m_a, x, 0.0), axis=1, keepdims=True)
    sb = jnp.sum(jnp.where(m_a, 0.0, x), axis=1, keepdims=True)
    return jnp.where(m_a, sa, sb)


def _head_inv(x, m_a):
    return lax.rsqrt(_half_sum(x * x, m_a) * (1.0 / HEAD_DIM) + EPS)


def _swap32(x):
    first = (lax.broadcasted_iota(jnp.int32, (1, LANES), 1) % HEAD_DIM) < (HEAD_DIM // 2)
    return jnp.where(first, pltpu.roll(x, LANES - HEAD_DIM // 2, 1), pltpu.roll(x, HEAD_DIM // 2, 1))


def _tri(blk, rel):
    r = lax.broadcasted_iota(jnp.int32, (blk, blk), 0)
    c = lax.broadcasted_iota(jnp.int32, (blk, blk), 1)
    return jnp.where(rel(r, c), 1.0, 0.0).astype(BF16)


def _cumdot(x, u, parts):
    acc = None
    r = x
    for i in range(parts):
        xi = r.astype(BF16)
        t = jnp.dot(xi, u, preferred_element_type=F32)
        acc = t if acc is None else acc + t
        if i + 1 < parts:
            r = r - xi.astype(F32)
    return acc


def _rows(i, blk):
    return pl.ds(pl.multiple_of(i * blk, blk), blk)


def _att_blk(n):
    return ATT_BLK if n % ATT_BLK == 0 else min(LANES, n)


def _att_blocks(n, wide_keys=False):
    bk = _att_blk(n)
    bq = ATT_BQ if n % ATT_BQ == 0 else bk
    return bq, (bq if wide_keys else bk)


def _loop(lo, hi, fn):
    def it(i, c):
        fn(i)
        return c

    lax.fori_loop(lo, hi, it, 0)


def _normed(src, g_ref, rope_refs, rows, m_a):
    xv = src[rows, :]
    xn = xv * _head_inv(xv, m_a) * g_ref[...]
    if rope_refs is not None:
        xn = xn * rope_refs[0][rows, :] + _swap32(xn) * rope_refs[1][rows, :]
    return xn


def _bias_lane(h):
    return HEAD_DIM if h == 0 else 0


def _k_for_head(kn, kb_row, h, m_h, lane, blk):
    out = jnp.where(m_h, kn, 0.0)
    if kb_row is not None:
        col = jnp.transpose(jnp.broadcast_to(kb_row, (LANES, blk)))
        hi = col.astype(BF16).astype(F32)
        mid = (col - hi).astype(BF16).astype(F32)
        lo = col - hi - mid
        b = _bias_lane(h)
        out = jnp.where(lane == b, hi, jnp.where(lane == b + 1, mid, jnp.where(lane == b + 2, lo, out)))
    return out.astype(BF16)


def _q_for_head(qb, h, m_h, lane, biased):
    out = jnp.where(m_h, qb, 0)
    if biased:
        b = _bias_lane(h)
        out = jnp.where((lane >= b) & (lane < b + 3), jnp.ones_like(out), out)
    return out


def _head_rows(x, parts=3):
    rr = lax.broadcasted_iota(jnp.int32, (8, LANES), 0)
    ll = lax.broadcasted_iota(jnp.int32, (8, LANES), 1)
    sel = jnp.where(((rr == 0) & (ll < HEAD_DIM)) | ((rr == 1) & (ll >= HEAD_DIM)), 1.0, 0.0).astype(BF16)
    acc = None
    rem = x
    for i in range(parts):
        xi = rem.astype(BF16)
        t = lax.dot_general(sel, xi, NT, preferred_element_type=F32)
        acc = t if acc is None else acc + t
        if i + 1 < parts:
            rem = rem - xi.astype(F32)
    return acc


def _cumdot_left(u, x, parts):
    acc = None
    rem = x
    for i in range(parts):
        xi = rem.astype(BF16)
        t = jnp.dot(u, xi, preferred_element_type=F32)
        acc = t if acc is None else acc + t
        if i + 1 < parts:
            rem = rem - xi.astype(F32)
    return acc


def _q_minus_k(bk, bq):
    return lax.broadcasted_iota(jnp.int32, (bk, bq), 1) - lax.broadcasted_iota(jnp.int32, (bk, bq), 0)


def _stream_rows(stride):
    if stride == 1:
        return _rows
    c = pl.program_id(2)
    return lambda i, blk: pl.ds(c + i * (blk * stride), blk, stride=stride)


def _attn_specs(t, pairs, stride, nq, bq):
    col = lambda off: pl.BlockSpec((None, t, LANES), lambda z, p, c: (z, 0, off + p * pairs))
    vec = pl.BlockSpec((1, LANES), lambda z, p, c: (0, 0))
    seq = pl.BlockSpec((None, t, LANES), lambda z, p, c: (z, 0, 0))
    stat = pl.BlockSpec((None, 2 * pairs, nq, 1, bq), lambda z, p, c: (z * stride + c, p, 0, 0, 0))
    return col, vec, seq, stat


def _attn_fwd(src, offs, npairs, gq, gk, *, rope=None, kbias=None, window, stride=1, pairs=1, name, cargo=None):
    bs, t, _ = src.shape
    n = t // stride
    full = window >= n
    bq, bk = _att_blocks(n, wide_keys=full)
    nq, nk, rq = n // bq, n // bk, bq // bk
    wblk = -(-window // bk)
    biased = kbias is not None
    heads = range(2 * pairs)

    def body(*refs):
        it = iter(refs)
        q_refs, k_refs, v_refs = ([next(it) for _ in range(pairs)] for _ in range(3))
        gq_ref, gk_ref = next(it), next(it)
        rope_refs = (next(it), next(it)) if rope is not None else None
        kb_ref = next(it) if biased else None
        o_refs = [next(it) for _ in range(pairs)]
        lse_ref, qn_s, kh_s, vt_s, acc_s, m_s = (next(it) for _ in range(6))
        m_a = _mask_a()
        masks = (m_a, jnp.logical_not(m_a))
        lane = lax.broadcasted_iota(jnp.int32, (1, LANES), 1)
        row = lax.broadcasted_iota(jnp.int32, (LANES, 1), 0)
        tok = _stream_rows(stride)

        def prep(c):
            rows = _rows(c, bk)
            trows = tok(c, bk)
            for pp in range(pairs):
                qn = _normed(q_refs[pp], gq_ref, rope_refs, trows, m_a)
                qn_s[pp, rows, :] = (qn * SCALE).astype(BF16)
                kn = _normed(k_refs[pp], gk_ref, rope_refs, trows, m_a)
                vt = jnp.transpose(v_refs[pp][trows, :])
                for half in (0, 1):
                    h = 2 * pp + half
                    kh_s[h, rows, :] = _k_for_head(kn, kb_ref[h, c] if biased else None, half, masks[half], lane, bk)
                    vt_s[h, c] = jnp.where(row == _bias_lane(half), 1.0, vt).astype(BF16)

        _loop(0, nk, prep)
        qk = _q_minus_k(bk, bq)

        def qblock(qi):
            rows = _rows(qi, bq)
            qh = [_q_for_head(qn_s[h // 2, rows, :], h % 2, masks[h % 2], lane, biased) for h in heads]
            m_s[...] = jnp.full(m_s.shape, NEG, F32)
            acc_s[...] = jnp.zeros_like(acc_s)

            def step(kj, masked, q0=0):
                cols = _rows(kj, bk)
                sts = [lax.dot_general(kh_s[h, cols, :], qh[h][q0:, :], NT, preferred_element_type=F32) for h in heads]
                old = [(m_s[h, :, q0:], acc_s[h, :, q0:]) for h in heads]
                if masked:
                    d = qk[:, q0:] + (qi * bq - kj * bk)
                    ok = (d >= 0) & (d <= window)
                    sts = [jnp.where(ok, st, NEG) for st in sts]
                new = []
                for h in heads:
                    m, acc = old[h]
                    m2 = jnp.maximum(m, jnp.max(sts[h], axis=0, keepdims=True))
                    pt = jnp.exp(sts[h] - m2).astype(BF16)
                    new.append((m2, jnp.exp(m - m2) * acc + jnp.dot(vt_s[h, kj], pt, preferred_element_type=F32)))
                for h in heads:
                    m_s[h, :, q0:], acc_s[h, :, q0:] = new[h]

            if full:
                _loop(0, qi * rq, lambda kj: step(kj, False))
                _loop(qi * rq, (qi + 1) * rq, lambda kj: step(kj, True))
            else:
                _loop(jnp.maximum(qi * rq - wblk, 0), (qi + 1) * rq - 1, lambda kj: step(kj, True))
                step((qi + 1) * rq - 1, True, (rq - 1) * bk)
            outs = []
            for h in heads:
                acc_t = acc_s[h]
                den = acc_t[_bias_lane(h % 2):_bias_lane(h % 2) + 1, :]
                outs.append(jnp.transpose(acc_t / den))
                lse_ref[h, qi] = m_s[h] + jnp.log(den)
            for pp in range(pairs):
                o_refs[pp][tok(qi, bq), :] = jnp.where(m_a, outs[2 * pp], outs[2 * pp + 1])

        _loop(0, nq, qblock)

    col, vec, seq, stat = _attn_specs(t, pairs, stride, nq, bq)
    ins = [src] * (3 * pairs) + [gq, gk]
    specs = [col(off + pp) for off in offs for pp in range(pairs)] + [vec, vec]
    if rope is not None:
        ins += list(rope)
        specs += [seq, seq]
    if biased:
        ins.append(kbias)
        specs.append(pl.BlockSpec((None, 2, nk, 1, bk), lambda z, p, c: (z, p, 0, 0, 0)))
    scratch = [pltpu.VMEM((pairs, n, LANES), BF16), pltpu.VMEM((2 * pairs, n, LANES), BF16)]
    scratch += [pltpu.VMEM((2 * pairs, nk, LANES, bk), BF16), pltpu.VMEM((2 * pairs, LANES, bq), F32)]
    scratch += [pltpu.VMEM((2 * pairs, 1, bq), F32)]
    assert pairs in (1, npairs)
    ospec = pl.BlockSpec((None, t, LANES), lambda z, p, c: (z, 0, p))
    *os_, lse = _call(
        body, cargo=cargo, name=name, grid=(bs, npairs // pairs, stride), in_specs=specs, out_specs=[ospec] * pairs + [stat],
        out_shape=[jax.ShapeDtypeStruct((bs, t, LANES * npairs // pairs), F32)] * pairs
        + [jax.ShapeDtypeStruct((bs * stride, 2 * npairs, nq, 1, bq), F32)],
        scratch_shapes=scratch, compiler_params=_params(("parallel", "parallel", "arbitrary")),
    )(*ins)
    return (os_[0] if pairs == 1 else jnp.concatenate(os_, axis=-1)), lse


def _attn_bwd(src, offs, npairs, gq, gk, o, do, lse, *, rope=None, kbias=None, window, stride=1, pairs=1, name, cargo=None):
    bs, t, _ = src.shape
    n = t // stride
    full = window >= n
    bq, bk = _att_blocks(n, wide_keys=full)
    nq, nk, rq = n // bq, n // bk, bq // bk
    wblk = -(-window // bk)
    biased = kbias is not None
    heads = range(2 * pairs)
    gdt = BF16 if stride == 1 else F32

    def body(*refs):
        it = iter(refs)
        q_refs, k_refs, v_refs = ([next(it) for _ in range(pairs)] for _ in range(3))
        gq_ref, gk_ref = next(it), next(it)
        o_refs, do_refs = ([next(it) for _ in range(pairs)] for _ in range(2))
        lse_ref = next(it)
        rope_refs = (next(it), next(it)) if rope is not None else None
        kb_ref = next(it) if biased else None
        dq_refs, dk_refs, dv_refs = ([next(it) for _ in range(pairs)] for _ in range(3))
        dg_ref = next(it)
        dkb_ref = next(it) if biased else None
        qn_s, kh_s, vb_s, kt_s, dqn_s, dkh_s, dv_s, dq_s, rs_s = (next(it) for _ in range(9))
        m_a = _mask_a()
        masks = (m_a, jnp.logical_not(m_a))
        lane = lax.broadcasted_iota(jnp.int32, (1, LANES), 1)
        tok = _stream_rows(stride)

        @pl.when((pl.program_id(0) == 0) & (pl.program_id(1) == 0) & (pl.program_id(2) == 0))
        def _():
            dg_ref[...] = jnp.zeros_like(dg_ref)

        def prep(c):
            rows = _rows(c, bk)
            trows = tok(c, bk)
            for pp in range(pairs):
                qn = _normed(q_refs[pp], gq_ref, rope_refs, trows, m_a)
                qn_s[pp, rows, :] = (qn * SCALE).astype(BF16)
                kn = _normed(k_refs[pp], gk_ref, rope_refs, trows, m_a)
                kt_s[pp, c] = jnp.transpose(kn).astype(BF16)
                vb_s[pp, rows, :] = v_refs[pp][trows, :].astype(BF16)
                for half in (0, 1):
                    h = 2 * pp + half
                    kh_s[h, rows, :] = _k_for_head(kn, kb_ref[h, c] if biased else None, half, masks[half], lane, bk)

        _loop(0, nk, prep)
        dkh_s[...] = jnp.zeros_like(dkh_s)
        dv_s[...] = jnp.zeros_like(dv_s)
        qk = _q_minus_k(bk, bq)

        def qblock(qi):
            rows = _rows(qi, bq)
            trows = tok(qi, bq)
            dobs = [do_refs[pp][trows, :] for pp in range(pairs)]
            deltas = [_head_rows(dobs[pp] * o_refs[pp][trows, :]) for pp in range(pairs)]
            qh = [_q_for_head(qn_s[h // 2, rows, :], h % 2, masks[h % 2], lane, biased) for h in heads]
            doms = [jnp.where(masks[h % 2], dobs[h // 2], 0.0).astype(BF16) for h in heads]
            delta = [deltas[h // 2][h % 2:h % 2 + 1, :] for h in heads]
            lses = [lse_ref[h, qi] for h in heads]
            dq_s[...] = jnp.zeros_like(dq_s)
            if biased:
                for h in heads:
                    rs_s[h, qi] = jnp.zeros((1, bq), F32)

            def step(kj, masked):
                cols = _rows(kj, bk)
                vbs = [vb_s[pp, cols, :] for pp in range(pairs)]
                kts = [kt_s[pp, kj] for pp in range(pairs)]
                sts = [lax.dot_general(kh_s[h, cols, :], qh[h], NT, preferred_element_type=F32) for h in heads]
                dpts = [lax.dot_general(vbs[h // 2], doms[h], NT, preferred_element_type=F32) for h in heads]
                if masked:
                    d = qk + (qi * bq - kj * bk)
                    ok = (d >= 0) & (d <= window)
                    sts = [jnp.where(ok, st, NEG) for st in sts]
                new = []
                for h in heads:
                    pt = jnp.exp(sts[h] - lses[h])
                    dst = pt * (dpts[h] - delta[h])
                    dsb = dst.astype(BF16)
                    tk = jnp.dot(dsb, qh[h], preferred_element_type=F32)
                    if biased:
                        tk = tk + jnp.dot((dst - dsb.astype(F32)).astype(BF16), qh[h], preferred_element_type=F32)
                    tv = jnp.dot(pt.astype(BF16), doms[h], preferred_element_type=F32)
                    tq = jnp.dot(kts[h // 2], dsb, preferred_element_type=F32)
                    new.append((tk, tv, tq, jnp.sum(dst, axis=0, keepdims=True) if biased else None))
                for h in heads:
                    dkh_s[h, cols, :] += new[h][0]
                    dq_s[h] += new[h][2]
                    if biased:
                        rs_s[h, qi] += new[h][3]
                for pp in range(pairs):
                    dv_s[pp, cols, :] += new[2 * pp][1] + new[2 * pp + 1][1]

            if full:
                _loop(0, qi * rq, lambda kj: step(kj, False))
                _loop(qi * rq, (qi + 1) * rq, lambda kj: step(kj, True))
            else:
                _loop(jnp.maximum(qi * rq - wblk, 0), (qi + 1) * rq, lambda kj: step(kj, True))
            for pp in range(pairs):
                dqn_s[pp, rows, :] = jnp.where(m_a, jnp.transpose(dq_s[2 * pp]), jnp.transpose(dq_s[2 * pp + 1])) * SCALE

        _loop(0, nq, qblock)

        def finish(c, carry):
            rows = _rows(c, bq)
            trows = tok(c, bq)
            dgq, dgk = carry
            for pp in range(pairs):
                dk_pair = [dkh_s[2 * pp, rows, :], dkh_s[2 * pp + 1, rows, :]]
                if biased:
                    for half in (0, 1):
                        b = _bias_lane(half)
                        dkb_row = jnp.transpose(dk_pair[half])[b:b + 1, :] - rs_s[2 * pp + half, c]
                        for j in range(rq):
                            dkb_ref[2 * pp + half, c * rq + j] = dkb_row[:, j * bk:(j + 1) * bk]
                dv_refs[pp][trows, :] = dv_s[pp, rows, :].astype(gdt)
                grads = (dqn_s[pp, rows, :], jnp.where(m_a, dk_pair[0], dk_pair[1]))
                out = []
                for src_ref, g_ref, dxn, dst in ((q_refs[pp], gq_ref, grads[0], dq_refs[pp]),
                                                 (k_refs[pp], gk_ref, grads[1], dk_refs[pp])):
                    xv = src_ref[trows, :]
                    inv = _head_inv(xv, m_a)
                    y = xv * inv
                    if rope_refs is not None:
                        dxn = dxn * rope_refs[0][trows, :] + _swap32(dxn * rope_refs[1][trows, :])
                    dy = dxn * g_ref[...]
                    dst[trows, :] = (inv * (dy - y * (_half_sum(dy * y, m_a) * (1.0 / HEAD_DIM)))).astype(gdt)
                    out.append(jnp.sum(dxn * y, axis=0, keepdims=True))
                dgq, dgk = dgq + out[0], dgk + out[1]
            return dgq, dgk

        zero = jnp.zeros((1, LANES), F32)
        dgq, dgk = lax.fori_loop(0, nq, finish, (zero, zero))
        dg_ref[0:1, :] += dgq
        dg_ref[1:2, :] += dgk

    assert pairs in (1, npairs)
    col, vec, seq, stat = _attn_specs(t, pairs, stride, nq, bq)
    ins = [src] * (3 * pairs) + [gq, gk] + [o] * pairs + [do] * pairs + [lse]
    specs = [col(off + pp) for off in offs for pp in range(pairs)] + [vec, vec] + [col(pp) for pp in range(pairs)] * 2 + [stat]
    if rope is not None:
        ins += list(rope)
        specs += [seq, seq]
    sds = jax.ShapeDtypeStruct((bs, t, LANES * npairs // pairs), gdt)
    out_shape = [sds] * (3 * pairs) + [jax.ShapeDtypeStruct((8, LANES), F32)]
    ospec = pl.BlockSpec((None, t, LANES), lambda z, p, c: (z, 0, p))
    out_specs = [ospec] * (3 * pairs) + [pl.BlockSpec((8, LANES), lambda z, p, c: (0, 0))]
    if biased:
        kbspec = pl.BlockSpec((None, 2, nk, 1, bk), lambda z, p, c: (z, p, 0, 0, 0))
        ins.append(kbias)
        specs.append(kbspec)
        out_shape.append(jax.ShapeDtypeStruct(kbias.shape, F32))
        out_specs.append(kbspec)
    scratch = [pltpu.VMEM((pairs, n, LANES), BF16), pltpu.VMEM((2 * pairs, n, LANES), BF16), pltpu.VMEM((pairs, n, LANES), BF16)]
    scratch += [pltpu.VMEM((pairs, nk, LANES, bk), BF16), pltpu.VMEM((pairs, n, LANES), F32)]
    scratch += [pltpu.VMEM((2 * pairs, n, LANES), F32), pltpu.VMEM((pairs, n, LANES), F32)]
    scratch += [pltpu.VMEM((2 * pairs, LANES, bq), F32), pltpu.VMEM((2 * pairs, nq, 1, bq), F32)]
    res = _call(
        body, cargo=cargo, name=name, grid=(bs, npairs // pairs, stride), in_specs=specs, out_specs=out_specs, out_shape=out_shape,
        scratch_shapes=scratch, compiler_params=_params(("arbitrary", "arbitrary", "arbitrary")),
    )(*ins)
    return (list(res[:pairs]), list(res[pairs:2 * pairs]), list(res[2 * pairs:3 * pairs]), *res[3 * pairs:])


SB_LOG_PARTS = 2
SB_GRAD_PARTS = 1


def _log_sig_pair(z):
    lsn = jnp.minimum(-z, 0.0) - jnp.log(1.0 + jnp.exp(-jnp.abs(z)))
    return lsn, z + lsn


def _sb_specs(n, pairs, nq, bq):
    col = lambda off: pl.BlockSpec((None, n, LANES * pairs), lambda z, p: (z, 0, off // pairs + p))
    stat = pl.BlockSpec((None, 2 * pairs, nq, 1, bq), lambda z, p: (z, p, 0, 0, 0))
    return col, stat


def _sb_pairs(npairs):
    return 2 if npairs % 2 == 0 else 1


def _sb_fwd(src, offs, npairs, *, name, cargo=None):
    zs, n, _ = src.shape
    bq, bk = _att_blocks(n)
    nq, nk, rq = n // bq, n // bk, bq // bk
    pairs = _sb_pairs(npairs)
    heads = range(2 * pairs)
    lanes = [slice(pp * LANES, (pp + 1) * LANES) for pp in range(pairs)]

    def body(q_ref, k_ref, v_ref, o_ref, lt_ref, qs_s, kb_s, vt_s, acc_s, c_s):
        m_a = _mask_a()
        masks = (m_a, jnp.logical_not(m_a))

        def prep(c):
            rows = _rows(c, bk)
            for pp in range(pairs):
                qs_s[pp, rows, :] = (q_ref[rows, lanes[pp]] * SCALE).astype(BF16)
                kb_s[pp, rows, :] = k_ref[rows, lanes[pp]].astype(BF16)
                vt_s[pp, c] = jnp.transpose(v_ref[rows, lanes[pp]]).astype(BF16)

        _loop(0, nk, prep)
        qk = _q_minus_k(bk, bq)
        u_gt = _tri(bk, lambda r, c: c > r)

        def qblock(qi):
            rows = _rows(qi, bq)
            qms = [jnp.where(masks[h % 2], qs_s[h // 2, rows, :], 0) for h in heads]
            acc_s[...] = jnp.zeros_like(acc_s)
            c_s[...] = jnp.zeros_like(c_s)

            def step(kj, masked, q0=0):
                cols = _rows(kj, bk)
                zts = [lax.dot_general(kb_s[h // 2, cols, :], qms[h][q0:, :], NT, preferred_element_type=F32) for h in heads]
                old = [c_s[h, :, q0:] for h in heads]
                if masked:
                    ok = (qk[:, q0:] + (qi * bq - kj * bk)) > 0
                new = []
                for h in heads:
                    lsn, lsp = _log_sig_pair(zts[h])
                    if masked:
                        lsn = jnp.where(ok, lsn, 0.0)
                    at = jnp.exp(lsp + (old[h] + _cumdot_left(u_gt, lsn, SB_LOG_PARTS)))
                    if masked:
                        at = jnp.where(ok, at, 0.0)
                    new.append((jnp.dot(vt_s[h // 2, kj], at.astype(BF16), preferred_element_type=F32),
                                old[h] + jnp.sum(lsn, axis=0, keepdims=True)))
                for h in heads:
                    acc_s[h, :, q0:] += new[h][0]
                    c_s[h, :, q0:] = new[h][1]

            for j in reversed(range(rq)):
                step(qi * rq + j, True, j * bk)
            _loop(0, qi * rq, lambda t: step(qi * rq - 1 - t, False))
            for pp in range(pairs):
                o_ref[rows, lanes[pp]] = jnp.where(m_a, jnp.transpose(acc_s[2 * pp]), jnp.transpose(acc_s[2 * pp + 1]))
            for h in heads:
                lt_ref[h, qi] = c_s[h]

        _loop(0, nq, qblock)

    col, stat = _sb_specs(n, pairs, nq, bq)
    scratch = [pltpu.VMEM((pairs, n, LANES), BF16)] * 2 + [pltpu.VMEM((pairs, nk, LANES, bk), BF16)]
    scratch += [pltpu.VMEM((2 * pairs, LANES, bq), F32), pltpu.VMEM((2 * pairs, 1, bq), F32)]
    return _call(
        body, cargo=cargo, name=name, grid=(zs, npairs // pairs),
        in_specs=[col(offs[0]), col(offs[1]), col(offs[2])], out_specs=[col(0), stat],
        out_shape=[jax.ShapeDtypeStruct((zs, n, LANES * npairs), F32), jax.ShapeDtypeStruct((zs, 2 * npairs, nq, 1, bq), F32)],
        scratch_shapes=scratch, compiler_params=_params(("parallel", "parallel")),
    )(src, src, src)


def _sb_bwd(src, offs, npairs, do, ltot, *, name, cargo=None):
    zs, n, _ = src.shape
    bq, bk = _att_blocks(n)
    nq, nk, rq = n // bq, n // bk, bq // bk
    pairs = _sb_pairs(npairs)
    heads = range(2 * pairs)
    lanes = [slice(pp * LANES, (pp + 1) * LANES) for pp in range(pairs)]

    def body(q_ref, k_ref, v_ref, do_ref, lt_ref, dq_ref, dk_ref, dv_ref, qs_s, kb_s, vb_s, kt_s, dk_s, dv_s, dq_s, lp_s, ep_s):
        m_a = _mask_a()
        masks = (m_a, jnp.logical_not(m_a))

        def prep(c):
            rows = _rows(c, bk)
            for pp in range(pairs):
                qs_s[pp, rows, :] = (q_ref[rows, lanes[pp]] * SCALE).astype(BF16)
                kv = k_ref[rows, lanes[pp]]
                kb_s[pp, rows, :] = kv.astype(BF16)
                kt_s[pp, c] = jnp.transpose(kv).astype(BF16)
                vb_s[pp, rows, :] = v_ref[rows, lanes[pp]].astype(BF16)

        _loop(0, nk, prep)
        dk_s[...] = jnp.zeros_like(dk_s)
        dv_s[...] = jnp.zeros_like(dv_s)
        qk = _q_minus_k(bk, bq)
        u_le = _tri(bk, lambda r, c: c <= r)
        u_lt = _tri(bk, lambda r, c: c < r)

        def qblock(qi):
            rows = _rows(qi, bq)
            qms = [jnp.where(masks[h % 2], qs_s[h // 2, rows, :], 0) for h in heads]
            doms = [jnp.where(masks[h % 2], do_ref[rows, lanes[h // 2]], 0.0).astype(BF16) for h in heads]
            lts = [lt_ref[h, qi] for h in heads]
            dq_s[...] = jnp.zeros_like(dq_s)
            lp_s[...] = jnp.zeros_like(lp_s)
            ep_s[...] = jnp.zeros_like(ep_s)

            def step(kj, masked, q0=0):
                cols = _rows(kj, bk)
                kbs = [kb_s[pp, cols, :] for pp in range(pairs)]
                kts = [kt_s[pp, kj] for pp in range(pairs)]
                qs, ds = [q[q0:, :] for q in qms], [dm[q0:, :] for dm in doms]
                zts = [lax.dot_general(kbs[h // 2], qs[h], NT, preferred_element_type=F32) for h in heads]
                dats = [lax.dot_general(vb_s[h // 2, cols, :], ds[h], NT, preferred_element_type=F32) for h in heads]
                old = [(lp_s[h, :, q0:], ep_s[h, :, q0:]) for h in heads]
                if masked:
                    ok = (qk[:, q0:] + (qi * bq - kj * bk)) > 0
                new = []
                for h in heads:
                    lp, ep = old[h]
                    lsn, lsp = _log_sig_pair(zts[h])
                    sig = jnp.exp(lsp)
                    if masked:
                        lsn = jnp.where(ok, lsn, 0.0)
                    at = jnp.exp(lsp + (lts[h][:, q0:] - (lp + _cumdot_left(u_le, lsn, SB_LOG_PARTS))))
                    if masked:
                        at = jnp.where(ok, at, 0.0)
                    et = dats[h] * at
                    big_e = ep + _cumdot_left(u_lt, et, SB_GRAD_PARTS)
                    dzt = et - sig * (et + big_e)
                    if masked:
                        dzt = jnp.where(ok, dzt, 0.0)
                    dzb = dzt.astype(BF16)
                    new.append((jnp.dot(dzb, qs[h], preferred_element_type=F32),
                                jnp.dot(at.astype(BF16), ds[h], preferred_element_type=F32),
                                jnp.dot(kts[h // 2], dzb, preferred_element_type=F32),
                                lp + jnp.sum(lsn, axis=0, keepdims=True), ep + jnp.sum(et, axis=0, keepdims=True)))
                for h in heads:
                    dq_s[h, :, q0:] += new[h][2]
                    lp_s[h, :, q0:], ep_s[h, :, q0:] = new[h][3], new[h][4]
                for pp in range(pairs):
                    dk_s[pp, cols, :] += new[2 * pp][0] + new[2 * pp + 1][0]
                    dv_s[pp, cols, :] += new[2 * pp][1] + new[2 * pp + 1][1]

            _loop(0, qi * rq, lambda kj: step(kj, False))
            for j in range(rq):
                step(qi * rq + j, True, j * bk)
            for pp in range(pairs):
                dq = jnp.where(m_a, jnp.transpose(dq_s[2 * pp]), jnp.transpose(dq_s[2 * pp + 1]))
                dq_ref[rows, lanes[pp]] = (dq * SCALE).astype(BF16)

        _loop(0, nq, qblock)

        def store(c):
            rows = _rows(c, bk)
            for pp in range(pairs):
                dk_ref[rows, lanes[pp]] = dk_s[pp, rows, :].astype(BF16)
                dv_ref[rows, lanes[pp]] = dv_s[pp, rows, :].astype(BF16)

        _loop(0, nk, store)

    col, stat = _sb_specs(n, pairs, nq, bq)
    ospec = col(0)
    sds = jax.ShapeDtypeStruct((zs, n, LANES * npairs), BF16)
    scratch = [pltpu.VMEM((pairs, n, LANES), BF16)] * 3 + [pltpu.VMEM((pairs, nk, LANES, bk), BF16)]
    scratch += [pltpu.VMEM((pairs, n, LANES), F32)] * 2
    scratch += [pltpu.VMEM((2 * pairs, LANES, bq), F32), pltpu.VMEM((2 * pairs, 1, bq), F32), pltpu.VMEM((2 * pairs, 1, bq), F32)]
    return _call(
        body, cargo=cargo, name=name, grid=(zs, npairs // pairs),
        in_specs=[col(offs[0]), col(offs[1]), col(offs[2]), ospec, stat],
        out_specs=[ospec] * 3, out_shape=[sds] * 3, scratch_shapes=scratch,
        compiler_params=_params(("parallel", "parallel")),
    )(src, src, src, do, ltot)


def _fox_gate_fwd(lg, bias, *, name):
    bs, nh, t = lg.shape
    blk = min(LANES, t)

    def body(lg_ref, b_ref, kb_ref):
        u_le = _tri(blk, lambda r, c: r <= c)
        carry = jnp.zeros((nh, 1), F32)
        for j in range(t // blk):
            sl = slice(j * blk, (j + 1) * blk)
            xv = lg_ref[:, sl] + b_ref[...]
            lf = jnp.minimum(xv, 0.0) - jnp.log(1.0 + jnp.exp(-jnp.abs(xv)))
            kb_ref[:, sl] = -(carry + _cumdot(lf, u_le, 3))
            carry = carry + jnp.sum(lf, axis=1, keepdims=True)

    spec = pl.BlockSpec((None, nh, t), lambda i: (i, 0, 0))
    return _pcall(
        body, name=name, grid=(bs,), in_specs=[spec, pl.BlockSpec((nh, 1), lambda i: (0, 0))], out_specs=spec,
        out_shape=jax.ShapeDtypeStruct((bs, nh, t), F32), compiler_params=_params(("parallel",)),
    )(lg, bias)


def _fox_gate_bwd(dkb, lg, bias, *, name):
    bs, nh, t = lg.shape
    blk = min(LANES, t)

    def body(dkb_ref, lg_ref, b_ref, dlg_ref, db_ref):
        @pl.when(pl.program_id(0) == 0)
        def _():
            db_ref[...] = jnp.zeros_like(db_ref)

        u_ge = _tri(blk, lambda r, c: r >= c)
        carry = jnp.zeros((nh, 1), F32)
        tot = jnp.zeros((nh, 1), F32)
        for j in reversed(range(t // blk)):
            sl = slice(j * blk, (j + 1) * blk)
            df = -dkb_ref[:, sl]
            dlf = carry + _cumdot(df, u_ge, 3)
            carry = carry + jnp.sum(df, axis=1, keepdims=True)
            xv = lg_ref[:, sl] + b_ref[...]
            dlg = dlf * jax.nn.sigmoid(-xv)
            dlg_ref[:, sl] = dlg
            tot = tot + jnp.sum(dlg, axis=1, keepdims=True)
        db_ref[...] += jnp.broadcast_to(tot, db_ref.shape)

    spec = pl.BlockSpec((None, nh, t), lambda i: (i, 0, 0))
    return _pcall(
        body, name=name, grid=(bs,), in_specs=[spec, spec, pl.BlockSpec((nh, 1), lambda i: (0, 0))],
        out_specs=[spec, pl.BlockSpec((nh, LANES), lambda i: (0, 0))],
        out_shape=[jax.ShapeDtypeStruct((bs, nh, t), F32), jax.ShapeDtypeStruct((nh, LANES), F32)],
        compiler_params=_params(("arbitrary",)),
    )(dkb, lg, bias)


def _place():
    return lax.axis_index("x"), lax.axis_index("y"), lax.axis_index("c")


def _flip(v, f):
    return 1 - v if f else v


FLIPS = [(fx, fy, fc) for fx in (0, 1) for fy in (0, 1) for fc in (0, 1)][1:]


def _comm_sems(nw):
    return [pltpu.SemaphoreType.DMA((7, nw)), pltpu.SemaphoreType.DMA((7, nw)), pltpu.SemaphoreType.DMA((nw,))]


def _gather_cargo(shards, on_done):
    nw = len(shards)

    def parts(x_refs, out_refs, sems):
        send_sems, recv_sems, local_sems = sems
        x, y, cc = _place()
        me, sibling = (x, y, cc), (x, y, 1 - cc)
        chips = [(1 - x, y), (x, 1 - y), (1 - x, 1 - y)]

        def slot(i, px, py, pc):
            return out_refs[i].at[4 * px + 2 * py + pc]

        def copy(i, k, block, to, src=None):
            return pltpu.make_async_remote_copy(
                src_ref=slot(i, *block) if src is None else src, dst_ref=slot(i, *block),
                send_sem=send_sems.at[k, i], recv_sem=recv_sems.at[k, i], device_id=to, device_id_type=MESH)

        mine = [pltpu.make_async_copy(x_refs[i], slot(i, *me), local_sems.at[i]) for i in range(nw)]
        first = []
        for i in range(nw):
            first.append(copy(i, 0, me, sibling, src=x_refs[i]))
            first += [copy(i, 1 + j, me, (*chip, cc), src=x_refs[i]) for j, chip in enumerate(chips)]
        return me, sibling, chips, cc, copy, mine, first

    def start(x_refs, out_refs, sems):
        *_, mine, first = parts(x_refs, out_refs, sems)
        for cp in mine + first:
            cp.start()

    def finish(x_refs, out_refs, sems):
        me, sibling, chips, cc, copy, mine, first = parts(x_refs, out_refs, sems)
        passed = []
        for i in range(nw):
            for j, chip in enumerate(chips):
                copy(i, 1 + j, (*chip, cc), me).wait_recv()
                passed.append(copy(i, 4 + j, (*chip, cc), sibling))
                passed[-1].start()
        for i in range(nw):
            copy(i, 0, sibling, me).wait_recv()
            for j, chip in enumerate(chips):
                copy(i, 4 + j, (*chip, 1 - cc), me).wait_recv()
        for cp in first + passed:
            cp.wait_send()
        for cp in mine:
            cp.wait()

    out_shape = [jax.ShapeDtypeStruct((N_DEV, *s.shape), s.dtype) for s in shards]
    return _Cargo(shards, out_shape, _comm_sems(nw), start, finish, on_done)


def _scatter_cargo(slots, prev, layer, depth, on_done, row0=0, rows=None):
    nw = len(slots)

    def parts(refs, recv_refs, sems):
        g_refs = refs[:nw]
        send_sems, recv_sems, local_sems = sems
        x, y, cc = _place()
        my = 4 * x + 2 * y + cc

        def dst(i):
            return recv_refs[i].at[my, layer, pl.ds(row0, slots[i].shape[1])]

        mine, copies = [], []
        for i in range(nw):
            mine.append(pltpu.make_async_copy(g_refs[i].at[my], dst(i), local_sems.at[i]))
            for k, (fx, fy, fc) in enumerate(FLIPS):
                px, py, pc = _flip(x, fx), _flip(y, fy), _flip(cc, fc)
                copies.append(pltpu.make_async_remote_copy(
                    src_ref=g_refs[i].at[4 * px + 2 * py + pc], dst_ref=dst(i),
                    send_sem=send_sems.at[k, i], recv_sem=recv_sems.at[k, i], device_id=(px, py, pc), device_id_type=MESH))
        return mine, copies

    def start(refs, recv_refs, sems):
        mine, copies = parts(refs, recv_refs, sems)
        for cp in mine + copies:
            cp.start()

    def finish(refs, recv_refs, sems):
        mine, copies = parts(refs, recv_refs, sems)
        for cp in copies:
            cp.wait_recv()
        for cp in copies:
            cp.wait_send()
        for cp in mine:
            cp.wait()

    ins, aliases = list(slots), {}
    for i, p in enumerate(prev):
        if p is not None:
            aliases[len(ins)] = i
            ins.append(p)
    out_shape = [jax.ShapeDtypeStruct((N_DEV, depth, rows or s.shape[1], s.shape[2]), s.dtype) for s in slots]
    return _Cargo(ins, out_shape, _comm_sems(nw), start, finish, on_done, aliases)


def _exchange(cargo, *, name):
    def body(*refs):
        c_in = len(cargo.ins)
        c_out = len(cargo.out_shape)
        cargo.start(refs[:c_in], refs[c_in:c_in + c_out], refs[c_in + c_out:])
        cargo.finish(refs[:c_in], refs[c_in:c_in + c_out], refs[c_in + c_out:])

    hbm = pl.BlockSpec(memory_space=pl.ANY)
    res = _pcall(
        body, name=name, in_specs=[hbm] * len(cargo.ins), out_specs=[hbm] * len(cargo.out_shape), out_shape=cargo.out_shape,
        scratch_shapes=cargo.sems, input_output_aliases=dict(cargo.aliases),
    )(*cargo.ins)
    cargo.on_done(list(res))


def _allreduce_small(blob, *, name):
    r, c = blob.shape

    def body(x_ref, out_ref, buf, send_sems, recv_sems):
        x, y, cc = _place()
        my = 4 * x + 2 * y + cc
        copies = []
        for k, (fx, fy, fc) in enumerate(FLIPS):
            peer = (_flip(x, fx), _flip(y, fy), _flip(cc, fc))
            copies.append(pltpu.make_async_remote_copy(
                src_ref=x_ref, dst_ref=buf.at[my], send_sem=send_sems.at[k], recv_sem=recv_sems.at[k],
                device_id=peer, device_id_type=MESH))
        for cp in copies:
            cp.start()
        buf[my] = x_ref[...]
        for cp in copies:
            cp.wait_recv()
        for cp in copies:
            cp.wait_send()
        acc = buf[0]
        for i in range(1, N_DEV):
            acc = acc + buf[i]
        out_ref[...] = acc

    vmem = pl.BlockSpec(memory_space=pltpu.VMEM)
    return _pcall(
        body, name=name, in_specs=[vmem], out_specs=vmem, out_shape=jax.ShapeDtypeStruct((r, c), F32),
        scratch_shapes=[pltpu.VMEM((N_DEV, r, c), F32), pltpu.SemaphoreType.DMA((7,)), pltpu.SemaphoreType.DMA((7,))],
    )(blob)


BIG = ("w_in", "w_mlp_in", "w_mlp_out", "w_up_fox", "w_up_sb", "w_up_dil", "w_out")
ROW_SHARDED = ("w_out", "w_mlp_out")
SMALL = ("attn_norm", "b_forget", "q_norm_fox", "k_norm_fox", "q_norm_dil", "k_norm_dil", "mlp_norm")
BLOB_ROWS = 512


def _pack(parts, dtype):
    flat = jnp.concatenate([p.reshape(-1).astype(dtype) for p in parts])
    size = -(-flat.shape[0] // (BLOB_ROWS * LANES)) * (BLOB_ROWS * LANES)
    return jnp.pad(flat, (0, size - flat.shape[0])).reshape(-1, LANES)


def _unpack(blob, shapes):
    flat = blob.reshape(-1)
    out, off = [], 0
    for shp in shapes:
        size = 1
        for s in shp:
            size *= s
        out.append(flat[off:off + size].reshape(shp))
        off += size
    return out


def _join_shards(name, sh):
    if name in ROW_SHARDED:
        return sh.reshape(-1, sh.shape[2])
    return jnp.transpose(sh, (1, 0, 2)).reshape(sh.shape[1], -1)


def _split_shards(name, full):
    a, b = full.shape
    if name in ROW_SHARDED:
        return full.reshape(N_DEV, a // N_DEV, b)
    return jnp.transpose(full.reshape(a, N_DEV, b // N_DEV), (1, 0, 2))


def _in_segments(d_in):
    o1 = 3 * W_FOX
    o2 = o1 + N_HEADS_FOX
    return (0, o1, 0), (o2, d_in, -N_HEADS_FOX), (o1, o2, d_in - o2)


def _join_w_in(sh, dp):
    b = sh.shape[2]
    pieces = []
    for s, e, _ in _in_segments(N_DEV * b):
        for j in range(s // b, (e - 1) // b + 1):
            pieces.append(sh[j, :, max(s, j * b) - j * b:min(e, (j + 1) * b) - j * b])
    pieces.append(jnp.zeros((sh.shape[1], dp - N_DEV * b), sh.dtype))
    return jnp.concatenate(pieces, axis=1)


def _split_w_in(gp, d_in):
    b = d_in // N_DEV
    shards = []
    for j in range(N_DEV):
        runs = []
        for s, e, shift in sorted(_in_segments(d_in)):
            lo, hi = max(s, j * b), min(e, (j + 1) * b)
            if lo < hi:
                runs.append(gp[:, lo + shift:hi + shift])
        shards.append(jnp.concatenate(runs, axis=1))
    return jnp.stack(shards)


def _stat_to_tokens(st, r, b):
    hh = st.shape[1]
    n = st.shape[2] * st.shape[4]
    return jnp.transpose(st.reshape(b, r, hh, n), (0, 2, 3, 1)).reshape(b, hh, n * r)


def _stat_to_streams(tok, r, blk):
    b, hh, t = tok.shape
    n = t // r
    return jnp.transpose(tok.reshape(b, hh, n, r), (0, 3, 1, 2)).reshape(b * r, hh, n // blk, 1, blk)


def _rope_tables(positions):
    half = HEAD_DIM // 2
    inv = 1.0 / (ROPE_THETA ** (jnp.arange(half, dtype=F32) / half))
    ang = positions.astype(F32)[..., None] * inv
    cos, sin = jnp.cos(ang), jnp.sin(ang)
    return jnp.tile(cos, (1, 1, 4)), jnp.tile(jnp.concatenate([-sin, sin], axis=-1), (1, 1, 2))


def _gain2(g):
    return jnp.tile(g.reshape(1, HEAD_DIM), (1, 2))


def _dil_offs(g):
    c0 = (P_DIL + g * W_DIL) // LANES
    return c0, c0 + W_DILQ // LANES, c0 + 2 * W_DILQ // LANES


def _w_in_split(d):
    cut = (3 * d // 8) // LANES * LANES
    return cut if cut > 0 else d // 2


def _dil_pairs(r):
    return N_HEADS_DIL // 2


def _layer_fwd(l, x, h, w, small, ropes, bl, t, cargo, next_gain):
    n, d = x.shape
    s = {}
    s["x"] = x
    s["h"] = _rmsnorm_fwd(x, small["attn_norm"][l].reshape(1, d), name=f"norm_attn_fwd{l}") if h is None else h
    proj = _mm(s["h"], w["w_in"][l], tn=PROJ_TILE, name=f"mm_proj{l}")
    s["proj"] = proj
    dp = proj.shape[1]
    proj3 = proj.reshape(bl, t, dp)
    p_fg = P_GATE + 3 * d

    lg = jnp.transpose(proj3[:, :, p_fg:p_fg + N_HEADS_FOX], (0, 2, 1))
    s["lg"] = lg
    kb = _fox_gate_fwd(lg, small["b_forget"][l].reshape(N_HEADS_FOX, 1), name=f"fox_gate_fwd{l}")
    blk = _att_blocks(t, wide_keys=True)[1]
    kb5 = kb.reshape(bl, N_HEADS_FOX, t // blk, 1, blk)
    s["kb5"] = kb5
    gqf, gkf = _gain2(small["q_norm_fox"][l]), _gain2(small["k_norm_fox"][l])
    fo = P_FOX // LANES
    fox_offs = (fo, fo + W_FOX // LANES, fo + 2 * W_FOX // LANES)
    out_a, lse_a = _attn_fwd(proj3, fox_offs, N_HEADS_FOX // 2, gqf, gkf, kbias=kb5, window=t, name=f"fox_fwd{l}",
                             cargo=cargo.get("fox_fwd"))
    s["out_a"], s["lse_a"] = out_a, lse_a

    so = P_SB // LANES
    sb_offs = (so, so + W_SB // LANES, so + 2 * W_SB // LANES)
    out_b, lt_b = _sb_fwd(proj3, sb_offs, N_HEADS_SB // 2, name=f"sb_fwd{l}", cargo=cargo.get("sb_fwd"))
    s["out_b"], s["lt_b"] = out_b, lt_b

    gqd, gkd = _gain2(small["q_norm_dil"][l]), _gain2(small["k_norm_dil"][l])
    os_, lses = [], []
    for g, (window, r) in enumerate(DIL_PATTERNS):
        o_g, lse_g = _attn_fwd(proj3, _dil_offs(g), N_HEADS_DIL // 2, gqd, gkd, rope=ropes, window=window // r, stride=r,
                               pairs=_dil_pairs(r), name=f"dil_fwd{l}_{g}")
        os_.append(o_g.reshape(n, W_DIL))
        lses.append(_stat_to_tokens(lse_g, r, bl).reshape(bl * N_HEADS_DIL, t))
    lse_c, *ws = _dil_weights(lses, name=f"dil_weights{l}")
    ws = [jnp.repeat(jnp.transpose(wg.reshape(bl, N_HEADS_DIL, t), (0, 2, 1)).reshape(n, N_HEADS_DIL), HEAD_DIM, axis=1) for wg in ws]
    out_c = _dil_mix(os_, ws, name=f"dil_mix{l}")
    s["out_c"], s["lse_c"] = out_c, lse_c.reshape(bl, N_HEADS_DIL, t)

    ys = [_mm(out_a.reshape(n, W_FOX), w["w_up_fox"][l], out_dtype=BF16, name=f"mm_up_fox{l}"),
          _mm(out_b.reshape(n, W_SB), w["w_up_sb"][l], out_dtype=BF16, name=f"mm_up_sb{l}"),
          _mm(out_c, w["w_up_dil"][l], out_dtype=BF16, name=f"mm_up_dil{l}")]
    s["ys"] = ys
    s["merged"] = _gate_merge_fwd(proj, ys, name=f"gate_merge_fwd{l}")
    x1, s["h2"] = _mm(s["merged"], w["w_out"][l], add=x, norm_gain=small["mlp_norm"][l].reshape(1, d), name=f"mm_out{l}")
    s["x1"] = x1

    s["u"], s["a"] = _mm(s["h2"], w["w_mlp_in"][l], relu2=True, out_dtype=BF16, name=f"mm_mlp_in{l}")
    if next_gain is None:
        return _mm(s["a"], w["w_mlp_out"][l], add=x1, name=f"mm_mlp_out{l}"), None, s
    x2, h_next = _mm(s["a"], w["w_mlp_out"][l], add=x1, norm_gain=next_gain, name=f"mm_mlp_out{l}")
    return x2, h_next, s


def _layer_bwd(l, dx2, s, w, small, ropes, bl, t, hooks):
    n, d = dx2.shape
    gw, gs = {}, {}

    def cargo(call):
        return hooks[call](gw) if call in hooks else None
    du = _mm(dx2, w["w_mlp_out"][l], tb=True, relu_grad_of=s["u"], out_dtype=BF16, name=f"mm_du{l}")
    gw["w_mlp_out"] = _mm(s["a"], dx2, ta=True, tm=2048, name=f"mm_dw_mlp_out{l}")
    gw["w_mlp_in"] = _mm(s["h2"], du, ta=True, name=f"mm_dw_mlp_in{l}")
    dh2 = _mm(du, w["w_mlp_in"][l], tb=True, name=f"mm_dh2{l}")
    dx1, gs["mlp_norm"] = _rmsnorm_bwd(s["x1"], small["mlp_norm"][l].reshape(1, d), dh2, dx2, name=f"norm_mlp_bwd{l}")

    dmerged = _mm(dx1, w["w_out"][l], tb=True, name=f"mm_dmerged{l}")
    gw["w_out"] = _mm(s["merged"], dx1, ta=True, name=f"mm_dw_out{l}")
    dya, dyb, dyc, dgl0, dgl1, dgl2 = _gate_merge_bwd(s["proj"], s["ys"], dmerged, name=f"gate_merge_bwd{l}")
    out_a2, out_b2 = s["out_a"].reshape(n, W_FOX), s["out_b"].reshape(n, W_SB)
    gw["w_up_fox"] = _mm(out_a2, dya, ta=True, name=f"mm_dw_up_fox{l}")
    gw["w_up_sb"] = _mm(out_b2, dyb, ta=True, name=f"mm_dw_up_sb{l}")
    gw["w_up_dil"] = _mm(s["out_c"], dyc, ta=True, name=f"mm_dw_up_dil{l}")
    dout_a = _mm(dya, w["w_up_fox"][l], tb=True, name=f"mm_dout_a{l}").reshape(bl, t, W_FOX)
    dout_b = _mm(dyb, w["w_up_sb"][l], tb=True, name=f"mm_dout_b{l}").reshape(bl, t, W_SB)
    dout_c = _mm(dyc, w["w_up_dil"][l], tb=True, name=f"mm_dout_c{l}").reshape(bl, t, W_DIL)

    proj3 = s["proj"].reshape(bl, t, -1)
    gqf, gkf = _gain2(small["q_norm_fox"][l]), _gain2(small["k_norm_fox"][l])
    fo = P_FOX // LANES
    fox_offs = (fo, fo + W_FOX // LANES, fo + 2 * W_FOX // LANES)
    (dq_a,), (dk_a,), (dv_a,), dg_a, dkb5 = _attn_bwd(proj3, fox_offs, N_HEADS_FOX // 2, gqf, gkf, s["out_a"], dout_a, s["lse_a"],
                                             kbias=s["kb5"], window=t, name=f"fox_bwd{l}", cargo=cargo("fox_bwd"))
    gs["fox_gains"] = dg_a
    dlg, gs["b_forget"] = _fox_gate_bwd(dkb5.reshape(bl, N_HEADS_FOX, t), s["lg"], small["b_forget"][l].reshape(N_HEADS_FOX, 1),
                                        name=f"fox_gate_bwd{l}")
    so = P_SB // LANES
    sb_offs = (so, so + W_SB // LANES, so + 2 * W_SB // LANES)
    dq_b, dk_b, dv_b = _sb_bwd(proj3, sb_offs, N_HEADS_SB // 2, dout_b, s["lt_b"], name=f"sb_bwd{l}", cargo=cargo("sb_bwd"))
    gqd, gkd = _gain2(small["q_norm_dil"][l]), _gain2(small["k_norm_dil"][l])
    out_c3 = s["out_c"].reshape(bl, t, W_DIL)
    dqs, dks, dvs, dgd = [], [], [], None
    for g, (window, r) in enumerate(DIL_PATTERNS):
        lse_g = _stat_to_streams(s["lse_c"], r, _att_blocks(t // r)[0])
        dq_g, dk_g, dv_g, dg_g = _attn_bwd(proj3, _dil_offs(g), N_HEADS_DIL // 2, gqd, gkd, out_c3, dout_c, lse_g, rope=ropes,
                                           window=window // r, stride=r, pairs=_dil_pairs(r), name=f"dil_bwd{l}_{g}")
        dqs += dq_g
        dks += dk_g
        dvs += dv_g
        dgd = dg_g if dgd is None else jnp.concatenate([dgd, dg_g], axis=0)
    gs["dil_gains"] = dgd

    dlg_cols = jnp.pad(jnp.transpose(dlg, (0, 2, 1)).reshape(n, N_HEADS_FOX), ((0, 0), (0, LANES - N_HEADS_FOX)))
    parts = [p.reshape(n, -1) for p in [dq_a, dk_a, dv_a, dq_b, dk_b, dv_b] + dqs + dks + dvs] + [dgl0, dgl1, dgl2, dlg_cols]
    dproj = _assemble_cols(parts, s["proj"].shape[1], name=f"assemble_dproj{l}")
    if "mm_dw_in_hi" in hooks:
        half = _w_in_split(d)
        gw["w_in_lo"] = _mm(s["h"][:, :half], dproj, ta=True, tn=PROJ_TILE, name=f"mm_dw_in_lo{l}")
        gw["w_in_hi"] = _mm(s["h"][:, half:], dproj, ta=True, tn=PROJ_TILE, name=f"mm_dw_in_hi{l}", cargo=cargo("mm_dw_in_hi"))
    else:
        gw["w_in"] = _mm(s["h"], dproj, ta=True, tn=PROJ_TILE, name=f"mm_dw_in{l}")
    dh = _mm(dproj, w["w_in"][l], tb=True, tn=1024, tk=PROJ_TILE, name=f"mm_dh{l}", cargo=cargo("mm_dh"))
    dx, gs["attn_norm"] = _rmsnorm_bwd(s["x"], small["attn_norm"][l].reshape(1, d), dh, dx1, name=f"norm_attn_bwd{l}")
    return dx, gw, gs


def kernel(x, positions, attn_norm, w_in, b_forget, q_norm_fox, k_norm_fox, q_norm_dil, k_norm_dil, w_up_fox, w_up_sb, w_up_dil, w_out, mlp_norm, w_mlp_in, w_mlp_out, loss_target, m_attn_norm, m_w_in, m_b_forget, m_q_norm_fox, m_k_norm_fox, m_q_norm_dil, m_k_norm_dil, m_w_up_fox, m_w_up_sb, m_w_up_dil, m_w_out, m_mlp_norm, m_w_mlp_in, m_w_mlp_out, v_attn_norm, v_w_in, v_b_forget, v_q_norm_fox, v_k_norm_fox, v_q_norm_dil, v_k_norm_dil, v_w_up_fox, v_w_up_sb, v_w_up_dil, v_w_out, v_mlp_norm, v_w_mlp_in, v_w_mlp_out):
    bl, t, d = x.shape
    n = bl * t
    depth = attn_norm.shape[0]
    wl = dict(w_in=w_in, w_up_fox=w_up_fox, w_up_sb=w_up_sb, w_up_dil=w_up_dil, w_out=w_out, w_mlp_in=w_mlp_in, w_mlp_out=w_mlp_out)
    ml = dict(w_in=m_w_in, w_up_fox=m_w_up_fox, w_up_sb=m_w_up_sb, w_up_dil=m_w_up_dil, w_out=m_w_out, w_mlp_in=m_w_mlp_in, w_mlp_out=m_w_mlp_out)
    vl = dict(w_in=v_w_in, w_up_fox=v_w_up_fox, w_up_sb=v_w_up_sb, w_up_dil=v_w_up_dil, w_out=v_w_out, w_mlp_in=v_w_mlp_in, w_mlp_out=v_w_mlp_out)
    small = dict(attn_norm=attn_norm, b_forget=b_forget, q_norm_fox=q_norm_fox, k_norm_fox=k_norm_fox, q_norm_dil=q_norm_dil,
                 k_norm_dil=k_norm_dil, mlp_norm=mlp_norm)
    m_small = dict(attn_norm=m_attn_norm, b_forget=m_b_forget, q_norm_fox=m_q_norm_fox, k_norm_fox=m_k_norm_fox,
                   q_norm_dil=m_q_norm_dil, k_norm_dil=m_k_norm_dil, mlp_norm=m_mlp_norm)
    v_small = dict(attn_norm=v_attn_norm, b_forget=v_b_forget, q_norm_fox=v_q_norm_fox, k_norm_fox=v_k_norm_fox,
                   q_norm_dil=v_q_norm_dil, k_norm_dil=v_k_norm_dil, mlp_norm=v_mlp_norm)

    d_in = w_in.shape[-1] * N_DEV
    dp = -(-d_in // 512) * 512
    rest = [k for k in BIG if k != "w_in"]
    w = {k: [None] * depth for k in BIG}

    def gather(items):
        def done(res):
            for (k, l), sh in zip(items, res):
                w[k][l] = _join_w_in(sh, dp) if k == "w_in" else _join_shards(k, sh)

        return _gather_cargo([wl[k][l].astype(BF16) for k, l in items], done)

    _exchange(gather([("w_in", 0)]), name="gather_first")
    cos, sin = _rope_tables(positions)
    ropes = (cos, sin)

    xl, hl = x.reshape(n, d), None
    saved = []
    for l in range(depth):
        cargo = {"fox_fwd": gather([(k, l) for k in rest])}
        if l + 1 < depth:
            cargo["sb_fwd"] = gather([("w_in", l + 1)])
        next_gain = attn_norm[l + 1].reshape(1, d) if l + 1 < depth else None
        xl, hl, s = _layer_fwd(l, xl, hl, w, small, ropes, bl, t, cargo, next_gain)
        saved.append(s)
    dy, loss_part = _loss_head(xl, loss_target.reshape(n, d), name="loss_head")

    recv = {}

    def scatter(names, l, grads):
        def done(res):
            recv.update(zip(names, res))

        slots = [(_split_w_in(grads[k], d_in) if k == "w_in" else _split_shards(k, grads[k])).astype(BF16) for k in names]
        return _scatter_cargo(slots, [recv.get(k) for k in names], l, depth, done)

    def scatter_w_in_rows(part, l, row0):
        def done(res):
            recv["w_in"] = res[0]

        return _scatter_cargo([_split_w_in(part, d_in).astype(BF16)], [recv.get("w_in")], l, depth, done, row0=row0, rows=d)

    gss = [None] * depth
    above = None
    for l in reversed(range(depth)):
        hooks = {"fox_bwd": lambda gw, l=l: scatter(rest, l, gw)}
        if above is not None:
            hooks["sb_bwd"] = lambda gw, l=l, g=above: scatter(["w_in"], l + 1, g)
        if l == 0:
            hooks["mm_dw_in_hi"] = lambda gw: scatter_w_in_rows(gw["w_in_lo"], 0, 0)
            hooks["mm_dh"] = lambda gw: scatter_w_in_rows(gw["w_in_hi"], 0, _w_in_split(d))
        dy, above, gss[l] = _layer_bwd(l, dy, saved[l], w, small, ropes, bl, t, hooks)
    grad_x = dy.reshape(bl, t, d)

    g_big, d_big, m_big, v_big = {}, {}, {}, {}
    for k in BIG:
        g_big[k], d_big[k], m_big[k], v_big[k] = _adamw(recv[k], wl[k], ml[k], vl[k], name=f"adamw_{k}")

    rows = [loss_part]
    for l in range(depth):
        gs = gss[l]
        rows += [gs["attn_norm"].reshape(-1, LANES), gs["mlp_norm"].reshape(-1, LANES), gs["fox_gains"], gs["dil_gains"], gs["b_forget"]]
    row_counts = [r.shape[0] for r in rows]
    part = jnp.concatenate(rows, axis=0)
    pad_rows = -(-part.shape[0] // 8) * 8 - part.shape[0]
    summed = _allreduce_small(jnp.pad(part, ((0, pad_rows), (0, 0))), name="allreduce_small")
    pieces, off = [], 0
    for c in row_counts:
        pieces.append(summed[off:off + c])
        off += c
    loss = pieces[0][0, 0]

    def fold(row):
        return row[:HEAD_DIM] + row[HEAD_DIM:]

    g_small = {k: [] for k in SMALL}
    for l in range(depth):
        an, mn, fg, dg, bf = pieces[1 + 5 * l:6 + 5 * l]
        g_small["attn_norm"].append(an.reshape(d))
        g_small["mlp_norm"].append(mn.reshape(d))
        g_small["q_norm_fox"].append(fold(fg[0]))
        g_small["k_norm_fox"].append(fold(fg[1]))
        g_small["q_norm_dil"].append(fold(dg[0]) + fold(dg[8]) + fold(dg[16]))
        g_small["k_norm_dil"].append(fold(dg[1]) + fold(dg[9]) + fold(dg[17]))
        g_small["b_forget"].append(bf[:, 0])
    g_small = {k: jnp.stack(vs) for k, vs in g_small.items()}
    small_shapes = [small[k].shape for k in SMALL]
    outs = _adamw(_pack([g_small[k] for k in SMALL], F32)[None, None], _pack([small[k] for k in SMALL], F32)[None],
                  _pack([m_small[k] for k in SMALL], F32)[None], _pack([v_small[k] for k in SMALL], F32)[None], name="adamw_small")
    g_sm, d_sm, m_sm, v_sm = (dict(zip(SMALL, _unpack(o, small_shapes))) for o in outs)

    order = ("attn_norm", "w_in", "b_forget", "q_norm_fox", "k_norm_fox", "q_norm_dil", "k_norm_dil", "w_up_fox", "w_up_sb",
             "w_up_dil", "w_out", "mlp_norm", "w_mlp_in", "w_mlp_out")
    res = [loss, grad_x]
    for big, sm in ((g_big, g_sm), (d_big, d_sm), (m_big, m_sm), (v_big, v_sm)):
        res += [big[k] if k in big else sm[k] for k in order]
    return tuple(res)
```
